```python
import math
import jax, jax.numpy as jnp
from jax import lax
import numpy as np

D_MODEL = 1024
BATCH = 8
SEQ = 2048
DEPTH = 1

HEAD_DIM = 64
MIX_WIDTH = D_MODEL
ATTN_WIDTH = MIX_WIDTH // 2
RET_WIDTH = MIX_WIDTH - ATTN_WIDTH
N_ATTN_HEADS = ATTN_WIDTH // HEAD_DIM
N_RET_HEADS = RET_WIDTH // HEAD_DIM
PROJ_WIDTH = 3 * ATTN_WIDTH + 4 * RET_WIDTH
ATTN_BRANCHES = ((128, 1), (512, 4), (2048, 16))
ATTN_BLOCK = 64
N_BUCKETS = 32
REL_MAX_DIST = 1024
RET_CHUNK = 128
ROPE_BASE = 10000.0
N_GROUPS = 4
EXPERTS_PER_GROUP = 4
N_EXPERTS = N_GROUPS * EXPERTS_PER_GROUP
EXPERT_FF = D_MODEL // 2
TOP_K_INNER = 2
EPS = 1e-6
NEG_INF = -1e30

kernel_name = 'hybrid_dilated_attn_retention_hier_moe'


def _rms_norm(x, gain):
    xf = x.astype(jnp.float32)
    y = xf * lax.rsqrt(jnp.mean(xf * xf, axis=-1, keepdims=True) + EPS)
    return (y * gain.astype(jnp.float32)).astype(x.dtype)


def _to_heads(t, n_heads):
    b, s, _ = t.shape
    return t.reshape(b, s, n_heads, -1).transpose(0, 2, 1, 3)


def _from_heads(t):
    b, h, s, hd = t.shape
    return t.transpose(0, 2, 1, 3).reshape(b, s, h * hd)


def _t5_bucket(rel):
    half = N_BUCKETS // 2
    max_exact = half // 2
    offset = jnp.where(rel > 0, half, 0)
    n = jnp.abs(rel)
    nf = jnp.maximum(n, 1).astype(jnp.float32)
    large = max_exact + (jnp.log(nf / max_exact) / math.log(REL_MAX_DIST / max_exact)
                         * (half - max_exact)).astype(jnp.int32)
    large = jnp.minimum(large, half - 1)
    return offset + jnp.where(n < max_exact, n, large)


def _dilated_window_branch(q, k, v, rel_bias, window, dilation):
    b, h, s, hd = q.shape
    radius = (window // 2) // dilation
    sub_len = s // dilation
    n_blk = -(-sub_len // ATTN_BLOCK)
    pad_len = n_blk * ATTN_BLOCK - sub_len

    def to_sub(t):
        return t.reshape(b, h, sub_len, dilation, hd).transpose(0, 1, 3, 2, 4)

    qs = jnp.pad(to_sub(q), ((0, 0), (0, 0), (0, 0), (0, pad_len), (0, 0)))
    qs = qs.reshape(b, h, dilation, n_blk, ATTN_BLOCK, hd)

    def neighbour_blocks(t):
        tp = jnp.pad(to_sub(t), ((0, 0), (0, 0), (0, 0), (ATTN_BLOCK, pad_len + ATTN_BLOCK), (0, 0)))
        tp = tp.reshape(b, h, dilation, n_blk + 2, ATTN_BLOCK, hd)
        return jnp.concatenate([tp[:, :, :, :-2], tp[:, :, :, 1:-1], tp[:, :, :, 2:]], axis=4)

    kb = neighbour_blocks(k)
    vb = neighbour_blocks(v)
    logits = jnp.einsum('bhrnqd,bhrnkd->bhrnqk', qs, kb).astype(jnp.float32)

    rel_sub = jnp.arange(3 * ATTN_BLOCK)[None, :] - ATTN_BLOCK - jnp.arange(ATTN_BLOCK)[:, None]
    bias = rel_bias[_t5_bucket(rel_sub * dilation)].astype(jnp.float32)
    bias = jnp.transpose(bias, (2, 0, 1))
    key_pos = (jnp.arange(n_blk)[:, None] - 1) * ATTN_BLOCK + jnp.arange(3 * ATTN_BLOCK)[None, :]
    valid = (jnp.abs(rel_sub) <= radius)[None] & ((key_pos >= 0) & (key_pos < sub_len))[:, None, :]
    logits = jnp.where(valid, logits + bias[None, :, None, None], NEG_INF)

    m = jnp.max(logits, axis=-1, keepdims=True)
    p = jnp.exp(logits - m)
    denom = jnp.sum(p, axis=-1, keepdims=True)
    out = jnp.einsum('bhrnqk,bhrnkd->bhrnqd', p, vb.astype(jnp.float32)) / denom
    lse = (m + jnp.log(denom))[..., 0]

    def from_sub(t):
        tail = t.shape[5:]
        t = t.reshape((b, h, dilation, n_blk * ATTN_BLOCK) + tail)[:, :, :, :sub_len]
        t = jnp.moveaxis(t, 2, 3)
        return t.reshape((b, h, s) + tail)

    return from_sub(out), from_sub(lse)


def _dilated_attention(q, k, v, rel_bias):
    outs, lses = [], []
    for window, dilation in ATTN_BRANCHES:
        o, l = _dilated_window_branch(q, k, v, rel_bias, window, dilation)
        outs.append(o)
        lses.append(l)
    w = jax.nn.softmax(jnp.stack(lses, axis=0), axis=0)
    return jnp.sum(w[..., None] * jnp.stack(outs, axis=0), axis=0)


def _rotary(t, pos):
    half = t.shape[-1] // 2
    inv = ROPE_BASE ** (-jnp.arange(half, dtype=jnp.float32) / half)
    ang = pos[:, None] * inv[None, :]
    cos, sin = jnp.cos(ang), jnp.sin(ang)
    t1, t2 = t[..., :half], t[..., half:]
    return jnp.concatenate([t1 * cos - t2 * sin, t1 * sin + t2 * cos], axis=-1)


def _chunk_retention(q, k, v, log_g, include_diag):
    b, h, s, hd = q.shape
    nc = s // RET_CHUNK
    qc = q.reshape(b, h, nc, RET_CHUNK, hd)
    kc = k.reshape(b, h, nc, RET_CHUNK, hd)
    vc = v.reshape(b, h, nc, RET_CHUNK, v.shape[-1])
    idx = jnp.arange(RET_CHUNK, dtype=jnp.float32)
    rel = idx[:, None] - idx[None, :]
    mask = (rel >= 0) if include_diag else (rel > 0)
    decay = jnp.where(mask[None], jnp.exp(log_g[:, None, None] * jnp.maximum(rel, 0.0)[None]), 0.0)
    scores = jnp.einsum('bhncd,bhnkd->bhnck', qc, kc) * decay[None, :, None]
    intra = jnp.einsum('bhnck,bhnke->bhnce', scores, vc)
    k_dec = jnp.exp(log_g[:, None] * (RET_CHUNK - 1 - idx)[None])
    kv = jnp.einsum('bhnkd,bhnke->bhnde', kc * k_dec[None, :, None, :, None], vc)
    chunk_decay = jnp.exp(log_g * RET_CHUNK)[None, :, None, None]

    def step(state, kv_n):
        return state * chunk_decay + kv_n, state

    init = jnp.zeros((b, h, hd, v.shape[-1]), jnp.float32)
    _, states = lax.scan(step, init, jnp.moveaxis(kv, 2, 0))
    states = jnp.moveaxis(states, 0, 2)
    q_dec = jnp.exp(log_g[:, None] * (idx + 1.0)[None])
    cross = jnp.einsum('bhncd,bhnde->bhnce', qc * q_dec[None, :, None, :, None], states)
    return (intra + cross).reshape(b, h, s, v.shape[-1])


def _retention(q, k, v, g, decay_fwd, decay_bwd):
    s, hd = q.shape[2], q.shape[3]
    pos = jnp.arange(s, dtype=jnp.float32)
    qf = _rotary(q.astype(jnp.float32), pos)
    kf = _rotary(k.astype(jnp.float32), pos) * (hd ** -0.5)
    vf = v.astype(jnp.float32)
    log_g_f = -jnp.exp(decay_fwd.astype(jnp.float32))
    log_g_b = -jnp.exp(decay_bwd.astype(jnp.float32))
    y_f = _chunk_retention(qf, kf, vf, log_g_f, True)
    flip = lambda t: jnp.flip(t, axis=2)
    y_b = flip(_chunk_retention(flip(qf), flip(kf), flip(vf), log_g_b, False))
    y = y_f + y_b
    y = y * lax.rsqrt(jnp.mean(y * y, axis=-1, keepdims=True) + EPS)
    y = _from_heads(y) * jax.nn.silu(g.astype(jnp.float32))
    return y.astype(g.dtype)


def _hier_moe(xn, w_g, b_g, w_e, b_e, w1, w3, w2):
    b, s, d = xn.shape
    t = xn.reshape(b * s, d)
    group_logits = (t @ w_g).astype(jnp.float32) + b_g.astype(jnp.float32)
    group_probs = jax.nn.softmax(group_logits, axis=-1)
    p_group, g_idx = lax.top_k(group_probs, 1)
    expert_logits = jnp.einsum('td,gde->tge', t, w_e).astype(jnp.float32) + b_e.astype(jnp.float32)
    chosen = jnp.take_along_axis(expert_logits, g_idx[:, :, None], axis=1)[:, 0]
    top_vals, top_idx = lax.top_k(chosen, TOP_K_INNER)
    top_p = jax.nn.softmax(top_vals, axis=-1)
    inner = jnp.sum(jax.nn.one_hot(top_idx, EXPERTS_PER_GROUP) * top_p[..., None], axis=1)
    gates = jax.nn.one_hot(g_idx[:, 0], N_GROUPS)[:, :, None] * (p_group[:, :, None] * inner[:, None, :])
    out = jnp.zeros((b * s, d), jnp.float32)
    for grp in range(N_GROUPS):
        sl = slice(grp * EXPERTS_PER_GROUP, (grp + 1) * EXPERTS_PER_GROUP)
        hidden = jax.nn.silu(jnp.einsum('td,edf->tef', t, w1[sl])) * jnp.einsum('td,edf->tef', t, w3[sl])
        hidden = hidden.astype(jnp.float32) * gates[:, grp, :, None]
        out = out + jnp.einsum('tef,efd->td', hidden, w2[sl].astype(jnp.float32))
    return out.reshape(b, s, d).astype(xn.dtype)


def setup_inputs(seed: int = 0) -> dict:
    key = jax.random.key(seed)
    ks = jax.random.split(key, 20)
    f32 = jnp.float32
    nrm = lambda k, shape, scale: jax.random.normal(k, shape, f32) * scale
    heads = jnp.arange(N_RET_HEADS, dtype=f32)
    base_decay = jnp.log(-jnp.log1p(-(2.0 ** (-5.0 - heads))))
    return {
        'x': nrm(ks[0], (BATCH, SEQ, D_MODEL), 1.0),
        'w_in': nrm(ks[1], (DEPTH, D_MODEL, PROJ_WIDTH), D_MODEL ** -0.5),
        'w_out': nrm(ks[2], (DEPTH, MIX_WIDTH, D_MODEL), MIX_WIDTH ** -0.5),
        'norm_mix': 1.0 + nrm(ks[3], (DEPTH, D_MODEL), 0.02),
        'norm_ffn': 1.0 + nrm(ks[4], (DEPTH, D_MODEL), 0.02),
        'norm_final': 1.0 + nrm(ks[5], (D_MODEL,), 0.02),
        'attn_out_gain': 1.0 + nrm(ks[6], (DEPTH, ATTN_WIDTH), 0.02),
        'rel_bias': nrm(ks[7], (N_BUCKETS, N_ATTN_HEADS), 0.2),
        'ret_decay_fwd': base_decay[None] + nrm(ks[8], (DEPTH, N_RET_HEADS), 0.05),
        'ret_decay_bwd': base_decay[None] + nrm(ks[9], (DEPTH, N_RET_HEADS), 0.05),
        'router_group_w': nrm(ks[10], (DEPTH, D_MODEL, N_GROUPS), D_MODEL ** -0.5),
        'router_group_b': nrm(ks[11], (DEPTH, N_GROUPS), 0.01),
        'router_expert_w': nrm(ks[12], (DEPTH, N_GROUPS, D_MODEL, EXPERTS_PER_GROUP), D_MODEL ** -0.5),
        'router_expert_b': nrm(ks[13], (DEPTH, N_GROUPS, EXPERTS_PER_GROUP), 0.01),
        'expert_w1': nrm(ks[14], (DEPTH, N_EXPERTS, D_MODEL, EXPERT_FF), D_MODEL ** -0.5),
        'expert_w3': nrm(ks[15], (DEPTH, N_EXPERTS, D_MODEL, EXPERT_FF), D_MODEL ** -0.5),
        'expert_w2': nrm(ks[16], (DEPTH, N_EXPERTS, EXPERT_FF, D_MODEL), EXPERT_FF ** -0.5),
    }


def reference(x, w_in, w_out, norm_mix, norm_ffn, norm_final, attn_out_gain, rel_bias,
              ret_decay_fwd, ret_decay_bwd, router_group_w, router_group_b,
              router_expert_w, router_expert_b, expert_w1, expert_w3, expert_w2):
    a = ATTN_WIDTH
    r = RET_WIDTH
    split_at = [a, 2 * a, 3 * a, 3 * a + r, 3 * a + 2 * r, 3 * a + 3 * r]
    h = x
    for layer in range(DEPTH):
        xn = _rms_norm(h, norm_mix[layer])
        proj = xn @ w_in[layer]
        aq, ak, av, rq, rk, rv, rg = jnp.split(proj, split_at, axis=-1)
        attn = _dilated_attention(_to_heads(aq, N_ATTN_HEADS) * (HEAD_DIM ** -0.5),
                                  _to_heads(ak, N_ATTN_HEADS), _to_heads(av, N_ATTN_HEADS), rel_bias)
        attn = _rms_norm(_from_heads(attn).astype(x.dtype), attn_out_gain[layer])
        ret = _retention(_to_heads(rq, N_RET_HEADS), _to_heads(rk, N_RET_HEADS),
                         _to_heads(rv, N_RET_HEADS), rg, ret_decay_fwd[layer], ret_decay_bwd[layer])
        h = h + jnp.concatenate([attn, ret], axis=-1) @ w_out[layer]
        hn = _rms_norm(h, norm_ffn[layer])
        h = h + _hier_moe(hn, router_group_w[layer], router_group_b[layer],
                          router_expert_w[layer], router_expert_b[layer],
                          expert_w1[layer], expert_w3[layer], expert_w2[layer])
    return _rms_norm(h, norm_final)
```

```python
import functools
import math

import jax
import jax.numpy as jnp
from jax import lax
from jax.experimental import pallas as pl
from jax.experimental.pallas import tpu as pltpu

F32 = jnp.float32
BF16 = jnp.bfloat16

D_MODEL = 1024
HEAD_DIM = 64
ATTN_WIDTH = 512
RET_WIDTH = 512
N_HEADS = 8
PAIR = 2 * HEAD_DIM
N_PAIRS = N_HEADS // 2
ATTN_DILATIONS = (1, 4, 16)
ATTN_RADIUS = 64
N_BUCKETS = 32
REL_MAX_DIST = 1024
ROPE_BASE = 10000.0
N_GROUPS = 4
EXPERTS_PER_GROUP = 4
N_EXPERTS = 16
EXPERT_FF = 512
EPS = 1e-6
NEG_INF = -1e30

LANES = 128
ATTN_TQ = 128
ATTN_W = 256
RET_CHUNK = 256
VMEM_LIMIT = 48 * 1024 * 1024


def _rms(x, gain):
    return x * lax.rsqrt(jnp.mean(x * x, axis=-1, keepdims=True) + EPS) * gain


def _inproj_kernel(x_ref, gain_ref, w_ref, cos_ref, sin_ref,
                   aq_ref, ak_ref, av_ref, rq_ref, rk_ref, rv_ref, rg_ref):
    xn = _rms(x_ref[...], gain_ref[...]).astype(BF16)

    def seg(i):
        return jnp.dot(xn, w_ref[:, i * 512:(i + 1) * 512], preferred_element_type=F32)

    def rotary(t):
        cos, sin = cos_ref[...], sin_ref[...]
        first_half = (lax.broadcasted_iota(jnp.int32, (1, LANES), 1) % HEAD_DIM) < HEAD_DIM // 2
        outs = []
        for j in range(t.shape[1] // LANES):
            tj = t[:, j * LANES:(j + 1) * LANES]
            partner = jnp.where(first_half, pltpu.roll(tj, LANES - 32, 1), pltpu.roll(tj, 32, 1))
            outs.append(tj * cos + partner * sin)
        return jnp.concatenate(outs, axis=1)

    aq_ref[...] = (seg(0) * (HEAD_DIM ** -0.5)).astype(BF16)
    ak_ref[...] = seg(1).astype(BF16)
    av_ref[...] = seg(2).astype(BF16)
    rq_ref[...] = rotary(seg(3)).astype(BF16)
    rk_ref[...] = (rotary(seg(4)) * (HEAD_DIM ** -0.5)).astype(BF16)
    rv_ref[...] = seg(5).astype(BF16)
    rg_ref[...] = seg(6).astype(BF16)


def _inproj(x, gain, w_in, cos_t, sin_t, tm=512):
    b, s, d = x.shape
    n = w_in.shape[1]
    out = jax.ShapeDtypeStruct((b, s, 512), BF16)
    ospec = pl.BlockSpec((None, tm, 512), lambda si, bi: (bi, si, 0))
    return pl.pallas_call(
        _inproj_kernel,
        grid=(s // tm, b),
        in_specs=[
            pl.BlockSpec((None, tm, d), lambda si, bi: (bi, si, 0)),
            pl.BlockSpec((1, d), lambda si, bi: (0, 0)),
            pl.BlockSpec((d, n), lambda si, bi: (0, 0)),
            pl.BlockSpec((tm, LANES), lambda si, bi: (si, 0)),
            pl.BlockSpec((tm, LANES), lambda si, bi: (si, 0)),
        ],
        out_specs=[ospec] * 7,
        out_shape=[out] * 7,
        compiler_params=pltpu.CompilerParams(
            dimension_semantics=("arbitrary", "arbitrary"), vmem_limit_bytes=VMEM_LIMIT),
        name="inproj",
    )(x, gain, w_in, cos_t, sin_t)


def _t5_bucket(rel):
    half = N_BUCKETS // 2
    max_exact = half // 2
    offset = jnp.where(rel > 0, half, 0)
    n = jnp.abs(rel)
    nf = jnp.maximum(n, 1).astype(F32)
    large = max_exact + (jnp.log(nf / max_exact) / math.log(REL_MAX_DIST / max_exact)
                         * (half - max_exact)).astype(jnp.int32)
    large = jnp.minimum(large, half - 1)
    return offset + jnp.where(n < max_exact, n, large)


def _attn_bias_tables(rel_bias):
    i = jnp.arange(ATTN_TQ)[:, None]
    j = jnp.arange(ATTN_W)[None, :]
    tabs = []
    for dil, offs in ((1, (0, 64, 128)), (4, (0, 64, 128)), (16, (0,))):
        for off in offs:
            rel = j - off - i
            bias = rel_bias[_t5_bucket(rel * dil)].astype(F32)
            tabs.append(jnp.where((jnp.abs(rel) <= ATTN_RADIUS)[..., None], bias, NEG_INF))
    t = jnp.stack(tabs, axis=0)
    t = jnp.transpose(t, (3, 0, 1, 2))
    return t.reshape(N_PAIRS, 2, 7, ATTN_TQ, ATTN_W)


def _attention_kernel(q_ref, k_ref, v_ref, bias_ref, o_ref,
                      nat_ref, qp_ref, kp_ref, vp_ref, acc_ref, m_ref, l_ref, out_ref):
    s = q_ref.shape[0]
    n_tiles = s // ATTN_TQ
    lane = lax.broadcasted_iota(jnp.int32, (1, PAIR), 1)
    head0 = lane < HEAD_DIM

    def run_branch(bi, dil, qs_ref, ks_ref, vs_ref):
        sub_len = s // dil
        tiles_per_sub = sub_len // ATTN_TQ
        width = min(ATTN_W, sub_len)

        def tile(t, carry):
            q0 = pl.multiple_of(t * ATTN_TQ, ATTN_TQ)
            if tiles_per_sub == 1:
                ws = q0
                var = 6
            else:
                pos = t % tiles_per_sub
                sub_lo = (t // tiles_per_sub) * sub_len
                ws = jnp.clip(q0 - 64, sub_lo, sub_lo + sub_len - ATTN_W)
                ws = pl.multiple_of(ws, 64)
                var = jnp.where(pos == 0, 0, jnp.where(pos == tiles_per_sub - 1, 2, 1)) + 3 * bi
            q = qs_ref[pl.ds(q0, ATTN_TQ), :]
            k = ks_ref[pl.ds(ws, width), :]
            v = vs_ref[pl.ds(ws, width), :]
            outs, ms, ls = [], [], []
            for h in range(2):
                qh = jnp.where(head0 if h == 0 else ~head0, q, jnp.zeros_like(q))
                sc = lax.dot_general(qh, k, (((1,), (1,)), ((), ())), preferred_element_type=F32)
                sc = sc + bias_ref[h, var][:, :width]
                m = jnp.max(sc, axis=-1, keepdims=True)
                p = jnp.exp(sc - m)
                ls.append(jnp.sum(p, axis=-1, keepdims=True))
                ms.append(m)
                outs.append(jnp.dot(p.astype(BF16), v, preferred_element_type=F32))
            acc_ref[bi, pl.ds(q0, ATTN_TQ), :] = jnp.where(head0, outs[0], outs[1])
            m_ref[bi, pl.ds(q0, ATTN_TQ), :] = jnp.where(head0, ms[0], ms[1])
            l_ref[bi, pl.ds(q0, ATTN_TQ), :] = jnp.where(head0, ls[0], ls[1])
            return carry

        lax.fori_loop(0, n_tiles, tile, 0)

    def permute(src_ref, dst_ref, dil):
        sub_len = s // dil
        nat_ref[...] = src_ref[...].astype(F32)
        for r in range(dil):
            dst_ref[r * sub_len:(r + 1) * sub_len, :] = (
                nat_ref[pl.ds(r, sub_len, stride=dil), :].astype(BF16))

    run_branch(0, 1, q_ref, k_ref, v_ref)
    for bi, dil in ((1, 4), (2, 16)):
        permute(q_ref, qp_ref, dil)
        permute(k_ref, kp_ref, dil)
        permute(v_ref, vp_ref, dil)
        run_branch(bi, dil, qp_ref, kp_ref, vp_ref)

    rows = s // 16
    for r in range(16):
        sl = (pl.ds(r, rows, stride=16),
              pl.ds((r % 4) * (s // 4) + r // 4, rows, stride=4),
              pl.ds(r * rows, rows))
        m = [m_ref[bi, sl[bi], :] for bi in range(3)]
        mx = jnp.maximum(jnp.maximum(m[0], m[1]), m[2])
        num = jnp.zeros((rows, PAIR), F32)
        den = jnp.zeros((rows, PAIR), F32)
        for bi in range(3):
            e = jnp.exp(m[bi] - mx)
            num = num + e * acc_ref[bi, sl[bi], :]
            den = den + e * l_ref[bi, sl[bi], :]
        out_ref[pl.ds(r, rows, stride=16), :] = num / den
    o_ref[...] = out_ref[...].astype(o_ref.dtype)


def _attention(aq, ak, av, bias_tabs):
    b, s, _ = aq.shape
    spec = pl.BlockSpec((None, s, PAIR), lambda hp, bi: (bi, 0, hp))
    return pl.pallas_call(
        _attention_kernel,
        grid=(N_PAIRS, b),
        in_specs=[spec, spec, spec,
                  pl.BlockSpec((None, 2, 7, ATTN_TQ, ATTN_W), lambda hp, bi: (hp, 0, 0, 0, 0))],
        out_specs=spec,
        out_shape=jax.ShapeDtypeStruct((b, s, ATTN_WIDTH), BF16),
        scratch_shapes=[
            pltpu.VMEM((s, PAIR), F32),
            pltpu.VMEM((s, PAIR), BF16),
            pltpu.VMEM((s, PAIR), BF16),
            pltpu.VMEM((s, PAIR), BF16),
            pltpu.VMEM((3, s, PAIR), F32),
            pltpu.VMEM((3, s, PAIR), F32),
            pltpu.VMEM((3, s, PAIR), F32),
            pltpu.VMEM((s, PAIR), F32),
        ],
        compiler_params=pltpu.CompilerParams(
            dimension_semantics=("arbitrary", "arbitrary"), vmem_limit_bytes=VMEM_LIMIT),
        name="attention",
    )(aq, ak, av, bias_tabs)


def _retention_tables(decay_fwd, decay_bwd):
    c = RET_CHUNK
    lg_f = -jnp.exp(decay_fwd.astype(F32))
    lg_b = -jnp.exp(decay_bwd.astype(F32))
    idx = jnp.arange(c, dtype=F32)
    rel = idx[:, None] - idx[None, :]
    dmat = jnp.where(rel >= 0,
                     jnp.exp(lg_f[:, None, None] * jnp.maximum(rel, 0.0)[None]),
                     jnp.exp(lg_b[:, None, None] * jnp.maximum(-rel, 0.0)[None]))
    dmat = dmat.reshape(N_PAIRS, 2, c, c)

    def lanes(v):
        v = v.reshape(N_PAIRS, 2, -1)
        return jnp.repeat(jnp.transpose(v, (0, 2, 1)), HEAD_DIM, axis=2)

    vec = jnp.stack([
        lanes(jnp.exp(lg_f[:, None] * (idx + 1.0)[None])),
        lanes(jnp.exp(lg_f[:, None] * (c - 1.0 - idx)[None])),
        lanes(jnp.exp(lg_b[:, None] * (c - idx)[None])),
        lanes(jnp.exp(lg_b[:, None] * idx[None])),
    ], axis=1)
    same_head = (jnp.arange(PAIR)[:, None] // HEAD_DIM) == (jnp.arange(PAIR)[None, :] // HEAD_DIM)
    cd = jnp.stack([lanes(jnp.exp(lg_f * c)[:, None]), lanes(jnp.exp(lg_b * c)[:, None])], axis=1)
    cd = jnp.transpose(cd, (0, 1, 3, 2)) * same_head[None, None].astype(F32)
    return dmat, vec, cd


def _retention_kernel(q_ref, k_ref, v_ref, g_ref, dmat_ref, vec_ref, cd_ref, o_ref, y_ref, kv_ref):
    s = q_ref.shape[0]
    c = RET_CHUNK
    nc = s // c
    lane = lax.broadcasted_iota(jnp.int32, (1, PAIR), 1)
    head0 = lane < HEAD_DIM
    same_head = ((lax.broadcasted_iota(jnp.int32, (PAIR, PAIR), 0) // HEAD_DIM)
                 == (lax.broadcasted_iota(jnp.int32, (PAIR, PAIR), 1) // HEAD_DIM))

    for n in range(nc):
        rows = slice(n * c, (n + 1) * c)
        q, k, v = q_ref[rows, :], k_ref[rows, :], v_ref[rows, :]
        outs = []
        for h in range(2):
            qh = jnp.where(head0 if h == 0 else ~head0, q, jnp.zeros_like(q))
            sc = lax.dot_general(qh, k, (((1,), (1,)), ((), ())), preferred_element_type=F32)
            outs.append(jnp.dot((sc * dmat_ref[h]).astype(BF16), v, preferred_element_type=F32))
        y_ref[rows, :] = jnp.where(head0, outs[0], outs[1])
        kf = k.astype(F32)
        kcat = jnp.concatenate([(kf * vec_ref[1]).astype(BF16), (kf * vec_ref[3]).astype(BF16)], axis=1)
        kv = lax.dot_general(kcat, v, (((0,), (0,)), ((), ())), preferred_element_type=F32)
        kv_ref[0, n] = jnp.where(same_head, kv[:PAIR], 0.0)
        kv_ref[1, n] = jnp.where(same_head, kv[PAIR:], 0.0)

    def cross(n, state, qdec):
        rows = slice(n * c, (n + 1) * c)
        qd = (q_ref[rows, :].astype(F32) * qdec).astype(BF16)
        y_ref[rows, :] += jnp.dot(qd, state.astype(BF16), preferred_element_type=F32)

    state = jnp.zeros((PAIR, PAIR), F32)
    for n in range(nc):
        cross(n, state, vec_ref[0])
        state = state * cd_ref[0] + kv_ref[0, n]
    state = jnp.zeros((PAIR, PAIR), F32)
    for n in reversed(range(nc)):
        cross(n, state, vec_ref[2])
        state = state * cd_ref[1] + kv_ref[1, n]

    y = y_ref[...]
    y2 = y * y
    ms0 = jnp.sum(jnp.where(head0, y2, 0.0), axis=-1, keepdims=True)
    ms1 = jnp.sum(jnp.where(head0, 0.0, y2), axis=-1, keepdims=True)
    ms = jnp.where(head0, ms0, ms1) * (1.0 / HEAD_DIM)
    g = g_ref[...].astype(F32)
    o_ref[...] = (y * lax.rsqrt(ms + EPS) * (g * jax.nn.sigmoid(g))).astype(o_ref.dtype)


def _retention(rq, rk, rv, rg, dmat, vec, cd):
    b, s, _ = rq.shape
    c = RET_CHUNK
    spec = pl.BlockSpec((None, s, PAIR), lambda hp, bi: (bi, 0, hp))
    return pl.pallas_call(
        _retention_kernel,
        grid=(N_PAIRS, b),
        in_specs=[spec, spec, spec, spec,
                  pl.BlockSpec((None, 2, c, c), lambda hp, bi: (hp, 0, 0, 0)),
                  pl.BlockSpec((None, 4, c, PAIR), lambda hp, bi: (hp, 0, 0, 0)),
                  pl.BlockSpec((None, 2, PAIR, PAIR), lambda hp, bi: (hp, 0, 0, 0))],
        out_specs=spec,
        out_shape=jax.ShapeDtypeStruct((b, s, RET_WIDTH), BF16),
        scratch_shapes=[pltpu.VMEM((s, PAIR), F32),
                        pltpu.VMEM((2, s // c, PAIR, PAIR), F32)],
        compiler_params=pltpu.CompilerParams(
            dimension_semantics=("arbitrary", "arbitrary"), vmem_limit_bytes=VMEM_LIMIT),
        name="retention",
    )(rq, rk, rv, rg, dmat, vec, cd)


def _outproj_kernel(attn_ref, ret_ref, x_ref, wa_ref, wr_ref, again_ref, fgain_ref,
                    rw_ref, rb_ref, h_ref, hn_ref, gates_ref):
    a = _rms(attn_ref[...].astype(F32), again_ref[...]).astype(BF16)
    h = (x_ref[...]
         + jnp.dot(a, wa_ref[...], preferred_element_type=F32)
         + jnp.dot(ret_ref[...], wr_ref[...], preferred_element_type=F32))
    h_ref[...] = h
    hn = _rms(h, fgain_ref[...]).astype(BF16)
    hn_ref[...] = hn

    logits = jnp.dot(hn, rw_ref[...], preferred_element_type=F32) + rb_ref[...]
    lane = lax.broadcasted_iota(jnp.int32, logits.shape, 1)
    big = jnp.int32(LANES)

    def first_argmax(vals):
        top = jnp.max(vals, axis=-1, keepdims=True)
        return top, jnp.min(jnp.where(vals == top, lane, big), axis=-1, keepdims=True)

    gl = jnp.where((lane >= N_EXPERTS) & (lane < N_EXPERTS + N_GROUPS), logits, -jnp.inf)
    gmax, gidx = first_argmax(gl)
    p_group = 1.0 / jnp.sum(jnp.exp(gl - gmax), axis=-1, keepdims=True)
    lo = (gidx - N_EXPERTS) * EXPERTS_PER_GROUP
    el = jnp.where((lane >= lo) & (lane < lo + EXPERTS_PER_GROUP), logits, -jnp.inf)
    v1, i1 = first_argmax(el)
    v2, i2 = first_argmax(jnp.where(lane == i1, -jnp.inf, el))
    e2 = jnp.exp(v2 - v1)
    p1 = p_group / (1.0 + e2)
    gates_ref[...] = jnp.where(lane == i1, p1, jnp.where(lane == i2, p1 * e2, 0.0))


def _outproj(attn, ret, x, w_a, w_r, again, fgain, rw, rb, tm=512):
    t, d = x.shape
    row = lambda w: pl.BlockSpec((tm, w), lambda i: (i, 0))
    full = lambda r, c: pl.BlockSpec((r, c), lambda i: (0, 0))
    return pl.pallas_call(
        _outproj_kernel,
        grid=(t // tm,),
        in_specs=[row(ATTN_WIDTH), row(RET_WIDTH), row(d), full(ATTN_WIDTH, d), full(RET_WIDTH, d),
                  full(1, ATTN_WIDTH), full(1, d), full(d, LANES), full(1, LANES)],
        out_specs=[row(d), row(d), row(LANES)],
        out_shape=[jax.ShapeDtypeStruct((t, d), F32), jax.ShapeDtypeStruct((t, d), BF16),
                   jax.ShapeDtypeStruct((t, LANES), F32)],
        compiler_params=pltpu.CompilerParams(
            dimension_semantics=("arbitrary",), vmem_limit_bytes=VMEM_LIMIT),
        name="outproj",
    )(attn, ret, x, w_a, w_r, again, fgain, rw, rb)


def _moe_kernel(hn_ref, h_ref, gates_ref, w1_ref, w3_ref, w2_ref, ngain_ref, o_ref):
    e = pl.program_id(1)

    @pl.when(e == 0)
    def _():
        o_ref[...] = h_ref[...]

    x = hn_ref[...]
    lane = lax.broadcasted_iota(jnp.int32, gates_ref.shape, 1)
    gate = jnp.sum(jnp.where(lane == e, gates_ref[...], 0.0), axis=-1, keepdims=True)
    a = jnp.dot(x, w1_ref[...], preferred_element_type=F32)
    b = jnp.dot(x, w3_ref[...], preferred_element_type=F32)
    hidden = (a * jax.nn.sigmoid(a) * b * gate).astype(BF16)
    o_ref[...] += jnp.dot(hidden, w2_ref[...], preferred_element_type=F32)

    @pl.when(e == pl.num_programs(1) - 1)
    def _():
        o_ref[...] = _rms(o_ref[...], ngain_ref[...])


def _moe(hn, h, gates, w1, w3, w2, ngain, tm=1024):
    t, d = hn.shape
    row = lambda w: pl.BlockSpec((tm, w), lambda i, e: (i, 0))
    return pl.pallas_call(
        _moe_kernel,
        grid=(t // tm, N_EXPERTS),
        in_specs=[row(d), row(d), row(LANES),
                  pl.BlockSpec((None, d, EXPERT_FF), lambda i, e: (e, 0, 0)),
                  pl.BlockSpec((None, d, EXPERT_FF), lambda i, e: (e, 0, 0)),
                  pl.BlockSpec((None, EXPERT_FF, d), lambda i, e: (e, 0, 0)),
                  pl.BlockSpec((1, d), lambda i, e: (0, 0))],
        out_specs=row(d),
        out_shape=jax.ShapeDtypeStruct((t, d), F32),
        compiler_params=pltpu.CompilerParams(
            dimension_semantics=("arbitrary", "arbitrary"), vmem_limit_bytes=VMEM_LIMIT),
        name="moe",
    )(hn, h, gates, w1, w3, w2, ngain)


def _rotary_tables(s):
    half = HEAD_DIM // 2
    inv = ROPE_BASE ** (-jnp.arange(half, dtype=F32) / half)
    ang = jnp.arange(s, dtype=F32)[:, None] * inv[None, :]
    cos, sin = jnp.cos(ang), jnp.sin(ang)
    cos_t = jnp.tile(jnp.concatenate([cos, cos], axis=-1), (1, LANES // HEAD_DIM))
    sin_t = jnp.tile(jnp.concatenate([-sin, sin], axis=-1), (1, LANES // HEAD_DIM))
    return cos_t, sin_t


def kernel(x, w_in, w_out, norm_mix, norm_ffn, norm_final, attn_out_gain, rel_bias, ret_decay_fwd, ret_decay_bwd, router_group_w, router_group_b, router_expert_w, router_expert_b, expert_w1, expert_w3, expert_w2):
    b, s, d = x.shape
    depth = w_in.shape[0]
    cos_t, sin_t = _rotary_tables(s)
    bias_tabs = _attn_bias_tables(rel_bias)
    h = x
    for layer in range(depth):
        aq, ak, av, rq, rk, rv, rg = _inproj(
            h, norm_mix[layer][None], w_in[layer].astype(BF16), cos_t, sin_t)
        attn = _attention(aq, ak, av, bias_tabs)
        ret = _retention(rq, rk, rv, rg, *_retention_tables(ret_decay_fwd[layer], ret_decay_bwd[layer]))

        w_o = w_out[layer].astype(BF16)
        rw = jnp.concatenate(
            [jnp.transpose(router_expert_w[layer], (1, 0, 2)).reshape(d, N_EXPERTS),
             router_group_w[layer],
             jnp.zeros((d, LANES - N_EXPERTS - N_GROUPS), F32)], axis=1).astype(BF16)
        rb = jnp.concatenate(
            [router_expert_b[layer].reshape(N_EXPERTS), router_group_b[layer],
             jnp.zeros((LANES - N_EXPERTS - N_GROUPS,), F32)])[None].astype(F32)
        h1, hn, gates = _outproj(
            attn.reshape(b * s, ATTN_WIDTH), ret.reshape(b * s, RET_WIDTH), h.reshape(b * s, d),
            w_o[:ATTN_WIDTH], w_o[ATTN_WIDTH:], attn_out_gain[layer][None], norm_ffn[layer][None],
            rw, rb)
        assert depth == 1, "the MoE kernel fuses the final norm, so it must run on the last layer"
        out = _moe(hn, h1, gates, expert_w1[layer].astype(BF16), expert_w3[layer].astype(BF16),
                   expert_w2[layer].astype(BF16), norm_final[None])
        h = out.reshape(b, s, d)
    return h
```

```python
import functools
import math

import jax
import jax.numpy as jnp
from jax import lax
from jax.experimental import pallas as pl
from jax.experimental.pallas import tpu as pltpu

F32 = jnp.float32
BF16 = jnp.bfloat16

D_MODEL = 1024
HEAD_DIM = 64
ATTN_WIDTH = 512
RET_WIDTH = 512
N_HEADS = 8
PAIR = 2 * HEAD_DIM
N_PAIRS = N_HEADS // 2
ATTN_DILATIONS = (1, 4, 16)
ATTN_RADIUS = 64
N_BUCKETS = 32
REL_MAX_DIST = 1024
ROPE_BASE = 10000.0
N_GROUPS = 4
EXPERTS_PER_GROUP = 4
N_EXPERTS = 16
EXPERT_FF = 512
EPS = 1e-6
NEG_INF = -1e30

LANES = 128
ATTN_TQ = 128
ATTN_W = 256
N_BIAS_VARIANTS = 8
RET_CHUNK = 256
VMEM_LIMIT = 48 * 1024 * 1024


def _rms(x, gain):
    return x * lax.rsqrt(jnp.mean(x * x, axis=-1, keepdims=True) + EPS) * gain


def _inproj_kernel(x_ref, gain_ref, w_ref, cos_ref, sin_ref,
                   aq_ref, ak_ref, av_ref, rq_ref, rk_ref, rv_ref, rg_ref):
    xn = _rms(x_ref[...], gain_ref[...]).astype(BF16)

    def seg(i):
        return jnp.dot(xn, w_ref[:, i * 512:(i + 1) * 512], preferred_element_type=F32)

    def rotary(t):
        cos, sin = cos_ref[...], sin_ref[...]
        first_half = (lax.broadcasted_iota(jnp.int32, (1, LANES), 1) % HEAD_DIM) < HEAD_DIM // 2
        outs = []
        for j in range(t.shape[1] // LANES):
            tj = t[:, j * LANES:(j + 1) * LANES]
            partner = jnp.where(first_half, pltpu.roll(tj, LANES - 32, 1), pltpu.roll(tj, 32, 1))
            outs.append(tj * cos + partner * sin)
        return jnp.concatenate(outs, axis=1)

    aq_ref[...] = (seg(0) * (HEAD_DIM ** -0.5)).astype(BF16)
    ak_ref[...] = seg(1).astype(BF16)
    av_ref[...] = seg(2).astype(BF16)
    rq_ref[...] = rotary(seg(3)).astype(BF16)
    rk_ref[...] = (rotary(seg(4)) * (HEAD_DIM ** -0.5)).astype(BF16)
    rv_ref[...] = seg(5).astype(BF16)
    rg_ref[...] = seg(6).astype(BF16)


def _inproj(x, gain, w_in, cos_t, sin_t, tm=512):
    b, s, d = x.shape
    n = w_in.shape[1]
    out = jax.ShapeDtypeStruct((b, s, 512), BF16)
    ospec = pl.BlockSpec((None, tm, 512), lambda si, bi: (bi, si, 0))
    return pl.pallas_call(
        _inproj_kernel,
        grid=(s // tm, b),
        in_specs=[
            pl.BlockSpec((None, tm, d), lambda si, bi: (bi, si, 0)),
            pl.BlockSpec((1, d), lambda si, bi: (0, 0)),
            pl.BlockSpec((d, n), lambda si, bi: (0, 0)),
            pl.BlockSpec((tm, LANES), lambda si, bi: (si, 0)),
            pl.BlockSpec((tm, LANES), lambda si, bi: (si, 0)),
        ],
        out_specs=[ospec] * 7,
        out_shape=[out] * 7,
        compiler_params=pltpu.CompilerParams(
            dimension_semantics=("arbitrary", "arbitrary"), vmem_limit_bytes=VMEM_LIMIT),
        name="inproj",
    )(x, gain, w_in, cos_t, sin_t)


def _t5_bucket(rel):
    half = N_BUCKETS // 2
    max_exact = half // 2
    offset = jnp.where(rel > 0, half, 0)
    n = jnp.abs(rel)
    nf = jnp.maximum(n, 1).astype(F32)
    large = max_exact + (jnp.log(nf / max_exact) / math.log(REL_MAX_DIST / max_exact)
                         * (half - max_exact)).astype(jnp.int32)
    large = jnp.minimum(large, half - 1)
    return offset + jnp.where(n < max_exact, n, large)


def _attn_bias_tables(rel_bias):
    period = 2 * ATTN_W
    band = 2 * ATTN_RADIUS + 1
    rel = jnp.arange(-ATTN_RADIUS, ATTN_RADIUS + 1)
    rows = []
    for dil, offs in ((1, (0, 64, 128)), (4, (0, 64, 128)), (16, (0, 128))):
        vals = rel_bias[_t5_bucket(rel * dil)].astype(F32).T
        for off in offs:
            lo = off - ATTN_RADIUS
            pad = jnp.full((N_HEADS, period - band), NEG_INF, F32)
            if lo >= 0:
                row = jnp.concatenate([pad[:, :lo], vals, pad[:, lo:]], axis=1)
            else:
                row = jnp.concatenate([vals[:, -lo:], pad, vals[:, :-lo]], axis=1)
            rows.append(row)
    v = jnp.stack(rows, axis=1)
    t = jnp.tile(v, (1, 1, ATTN_TQ))[:, :, :ATTN_TQ * (period - 1)]
    t = t.reshape(N_HEADS, N_BIAS_VARIANTS, ATTN_TQ, period - 1)[..., :ATTN_W]
    own_half = (jnp.arange(ATTN_W)[None, :] // ATTN_TQ) == jnp.arange(2)[:, None]
    t = t.at[:, 6:8].set(jnp.where(own_half[None, :, None, :], t[:, 6:8], NEG_INF))
    return t.reshape(N_PAIRS, 2, N_BIAS_VARIANTS, ATTN_TQ, ATTN_W)


def _attention_kernel(q_ref, k_ref, v_ref, bias_ref, o_ref,
                      nat_ref, qp_ref, kp_ref, vp_ref, acc_ref, m_ref, l_ref, out_ref):
    s = q_ref.shape[0]
    n_tiles = s // ATTN_TQ
    lane = lax.broadcasted_iota(jnp.int32, (1, PAIR), 1)
    head0 = lane < HEAD_DIM

    def run_branch(bi, dil, qs_ref, ks_ref, vs_ref):
        sub_len = s // dil
        tiles_per_sub = sub_len // ATTN_TQ

        def tile(t, carry):
            q0 = pl.multiple_of(t * ATTN_TQ, ATTN_TQ)
            if tiles_per_sub == 1:
                ws = pl.multiple_of((t // 2) * ATTN_W, ATTN_W)
                var = 6 + t % 2
            else:
                pos = t % tiles_per_sub
                sub_lo = (t // tiles_per_sub) * sub_len
                ws = jnp.clip(q0 - 64, sub_lo, sub_lo + sub_len - ATTN_W)
                ws = pl.multiple_of(ws, 64)
                var = jnp.where(pos == 0, 0, jnp.where(pos == tiles_per_sub - 1, 2, 1)) + 3 * bi
            q = qs_ref[pl.ds(q0, ATTN_TQ), :]
            k = ks_ref[pl.ds(ws, ATTN_W), :]
            v = vs_ref[pl.ds(ws, ATTN_W), :]
            outs, ms, ls = [], [], []
            for h in range(2):
                qh = jnp.where(head0 if h == 0 else ~head0, q, jnp.zeros_like(q))
                sc = lax.dot_general(qh, k, (((1,), (1,)), ((), ())), preferred_element_type=F32)
                sc = sc + bias_ref[h, var]
                m = jnp.max(sc, axis=-1, keepdims=True)
                p = jnp.exp(sc - m)
                ls.append(jnp.sum(p, axis=-1, keepdims=True))
                ms.append(m)
                outs.append(jnp.dot(p.astype(BF16), v, preferred_element_type=F32))
            acc_ref[bi, pl.ds(q0, ATTN_TQ), :] = jnp.where(head0, outs[0], outs[1])
            m_ref[bi, pl.ds(q0, ATTN_TQ), :] = jnp.where(head0, ms[0], ms[1])
            l_ref[bi, pl.ds(q0, ATTN_TQ), :] = jnp.where(head0, ls[0], ls[1])
            return carry

        lax.fori_loop(0, n_tiles, tile, 0, unroll=8)

    def permute(src_ref, dst_ref, dil):
        sub_len = s // dil
        nat_ref[...] = src_ref[...].astype(F32)
        for r in range(dil):
            dst_ref[r * sub_len:(r + 1) * sub_len, :] = (
                nat_ref[pl.ds(r, sub_len, stride=dil), :].astype(BF16))

    run_branch(0, 1, q_ref, k_ref, v_ref)
    for bi, dil in ((1, 4), (2, 16)):
        permute(q_ref, qp_ref, dil)
        permute(k_ref, kp_ref, dil)
        permute(v_ref, vp_ref, dil)
        run_branch(bi, dil, qp_ref, kp_ref, vp_ref)

    rows = s // 16
    for r in range(16):
        sl = (pl.ds(r, rows, stride=16),
              pl.ds((r % 4) * (s // 4) + r // 4, rows, stride=4),
              pl.ds(r * rows, rows))
        m = [m_ref[bi, sl[bi], :] for bi in range(3)]
        mx = jnp.maximum(jnp.maximum(m[0], m[1]), m[2])
        num = jnp.zeros((rows, PAIR), F32)
        den = jnp.zeros((rows, PAIR), F32)
        for bi in range(3):
            e = jnp.exp(m[bi] - mx)
            num = num + e * acc_ref[bi, sl[bi], :]
            den = den + e * l_ref[bi, sl[bi], :]
        out_ref[pl.ds(r, rows, stride=16), :] = num / den
    o_ref[...] = out_ref[...].astype(o_ref.dtype)


def _attention(aq, ak, av, bias_tabs):
    b, s, _ = aq.shape
    spec = pl.BlockSpec((None, s, PAIR), lambda hp, bi: (bi, 0, hp))
    return pl.pallas_call(
        _attention_kernel,
        grid=(N_PAIRS, b),
        in_specs=[spec, spec, spec,
                  pl.BlockSpec((None, 2, N_BIAS_VARIANTS, ATTN_TQ, ATTN_W),
                               lambda hp, bi: (hp, 0, 0, 0, 0))],
        out_specs=spec,
        out_shape=jax.ShapeDtypeStruct((b, s, ATTN_WIDTH), BF16),
        scratch_shapes=[
            pltpu.VMEM((s, PAIR), F32),
            pltpu.VMEM((s, PAIR), BF16),
            pltpu.VMEM((s, PAIR), BF16),
            pltpu.VMEM((s, PAIR), BF16),
            pltpu.VMEM((3, s, PAIR), F32),
            pltpu.VMEM((3, s, PAIR), F32),
            pltpu.VMEM((3, s, PAIR), F32),
            pltpu.VMEM((s, PAIR), F32),
        ],
        compiler_params=pltpu.CompilerParams(
            dimension_semantics=("arbitrary", "arbitrary"), vmem_limit_bytes=VMEM_LIMIT),
        name="attention",
    )(aq, ak, av, bias_tabs)


def _retention_tables(decay_fwd, decay_bwd):
    c = RET_CHUNK
    lg_f = -jnp.exp(decay_fwd.astype(F32))
    lg_b = -jnp.exp(decay_bwd.astype(F32))
    idx = jnp.arange(c, dtype=F32)
    rel = idx[:, None] - idx[None, :]
    dmat = jnp.where(rel >= 0,
                     jnp.exp(lg_f[:, None, None] * jnp.maximum(rel, 0.0)[None]),
                     jnp.exp(lg_b[:, None, None] * jnp.maximum(-rel, 0.0)[None]))
    dmat = dmat.reshape(N_PAIRS, 2, c, c)

    def lanes(v):
        v = v.reshape(N_PAIRS, 2, -1)
        return jnp.repeat(jnp.transpose(v, (0, 2, 1)), HEAD_DIM, axis=2)

    vec = jnp.stack([
        lanes(jnp.exp(lg_f[:, None] * (idx + 1.0)[None])),
        lanes(jnp.exp(lg_f[:, None] * (c - 1.0 - idx)[None])),
        lanes(jnp.exp(lg_b[:, None] * (c - idx)[None])),
        lanes(jnp.exp(lg_b[:, None] * idx[None])),
    ], axis=1)
    same_head = (jnp.arange(PAIR)[:, None] // HEAD_DIM) == (jnp.arange(PAIR)[None, :] // HEAD_DIM)
    cd = jnp.stack([lanes(jnp.exp(lg_f * c)[:, None]), lanes(jnp.exp(lg_b * c)[:, None])], axis=1)
    cd = jnp.transpose(cd, (0, 1, 3, 2)) * same_head[None, None].astype(F32)
    return dmat, vec, cd


def _retention_kernel(q_ref, k_ref, v_ref, g_ref, dmat_ref, vec_ref, cd_ref, o_ref, y_ref, kv_ref):
    s = q_ref.shape[0]
    c = RET_CHUNK
    nc = s // c
    lane = lax.broadcasted_iota(jnp.int32, (1, PAIR), 1)
    head0 = lane < HEAD_DIM
    same_head = ((lax.broadcasted_iota(jnp.int32, (PAIR, PAIR), 0) // HEAD_DIM)
                 == (lax.broadcasted_iota(jnp.int32, (PAIR, PAIR), 1) // HEAD_DIM))

    for n in range(nc):
        rows = slice(n * c, (n + 1) * c)
        q, k, v = q_ref[rows, :], k_ref[rows, :], v_ref[rows, :]
        outs = []
        for h in range(2):
            qh = jnp.where(head0 if h == 0 else ~head0, q, jnp.zeros_like(q))
            sc = lax.dot_general(qh, k, (((1,), (1,)), ((), ())), preferred_element_type=F32)
            outs.append(jnp.dot((sc * dmat_ref[h]).astype(BF16), v, preferred_element_type=F32))
        y_ref[rows, :] = jnp.where(head0, outs[0], outs[1])
        kf = k.astype(F32)
        kcat = jnp.concatenate([(kf * vec_ref[1]).astype(BF16), (kf * vec_ref[3]).astype(BF16)], axis=1)
        kv = lax.dot_general(kcat, v, (((0,), (0,)), ((), ())), preferred_element_type=F32)
        kv_ref[0, n] = jnp.where(same_head, kv[:PAIR], 0.0)
        kv_ref[1, n] = jnp.where(same_head, kv[PAIR:], 0.0)

    def cross(n, state, qdec):
        rows = slice(n * c, (n + 1) * c)
        qd = (q_ref[rows, :].astype(F32) * qdec).astype(BF16)
        y_ref[rows, :] += jnp.dot(qd, state.astype(BF16), preferred_element_type=F32)

    state = jnp.zeros((PAIR, PAIR), F32)
    for n in range(nc):
        cross(n, state, vec_ref[0])
        state = state * cd_ref[0] + kv_ref[0, n]
    state = jnp.zeros((PAIR, PAIR), F32)
    for n in reversed(range(nc)):
        cross(n, state, vec_ref[2])
        state = state * cd_ref[1] + kv_ref[1, n]

    y = y_ref[...]
    y2 = y * y
    ms0 = jnp.sum(jnp.where(head0, y2, 0.0), axis=-1, keepdims=True)
    ms1 = jnp.sum(jnp.where(head0, 0.0, y2), axis=-1, keepdims=True)
    ms = jnp.where(head0, ms0, ms1) * (1.0 / HEAD_DIM)
    g = g_ref[...].astype(F32)
    o_ref[...] = (y * lax.rsqrt(ms + EPS) * (g * jax.nn.sigmoid(g))).astype(o_ref.dtype)


def _retention(rq, rk, rv, rg, dmat, vec, cd):
    b, s, _ = rq.shape
    c = RET_CHUNK
    spec = pl.BlockSpec((None, s, PAIR), lambda hp, bi: (bi, 0, hp))
    return pl.pallas_call(
        _retention_kernel,
        grid=(N_PAIRS, b),
        in_specs=[spec, spec, spec, spec,
                  pl.BlockSpec((None, 2, c, c), lambda hp, bi: (hp, 0, 0, 0)),
                  pl.BlockSpec((None, 4, c, PAIR), lambda hp, bi: (hp, 0, 0, 0)),
                  pl.BlockSpec((None, 2, PAIR, PAIR), lambda hp, bi: (hp, 0, 0, 0))],
        out_specs=spec,
        out_shape=jax.ShapeDtypeStruct((b, s, RET_WIDTH), BF16),
        scratch_shapes=[pltpu.VMEM((s, PAIR), F32),
                        pltpu.VMEM((2, s // c, PAIR, PAIR), F32)],
        compiler_params=pltpu.CompilerParams(
            dimension_semantics=("arbitrary", "arbitrary"), vmem_limit_bytes=VMEM_LIMIT),
        name="retention",
    )(rq, rk, rv, rg, dmat, vec, cd)


def _outproj_kernel(attn_ref, ret_ref, x_ref, wa_ref, wr_ref, again_ref, fgain_ref,
                    rw_ref, rb_ref, h_ref, hn_ref, gates_ref):
    a = _rms(attn_ref[...].astype(F32), again_ref[...]).astype(BF16)
    h = (x_ref[...]
         + jnp.dot(a, wa_ref[...], preferred_element_type=F32)
         + jnp.dot(ret_ref[...], wr_ref[...], preferred_element_type=F32))
    h_ref[...] = h
    hn = _rms(h, fgain_ref[...]).astype(BF16)
    hn_ref[...] = hn

    logits = jnp.dot(hn, rw_ref[...], preferred_element_type=F32) + rb_ref[...]
    lane = lax.broadcasted_iota(jnp.int32, logits.shape, 1)
    big = jnp.int32(LANES)

    def first_argmax(vals):
        top = jnp.max(vals, axis=-1, keepdims=True)
        return top, jnp.min(jnp.where(vals == top, lane, big), axis=-1, keepdims=True)

    gl = jnp.where((lane >= N_EXPERTS) & (lane < N_EXPERTS + N_GROUPS), logits, -jnp.inf)
    gmax, gidx = first_argmax(gl)
    p_group = 1.0 / jnp.sum(jnp.exp(gl - gmax), axis=-1, keepdims=True)
    lo = (gidx - N_EXPERTS) * EXPERTS_PER_GROUP
    el = jnp.where((lane >= lo) & (lane < lo + EXPERTS_PER_GROUP), logits, -jnp.inf)
    v1, i1 = first_argmax(el)
    v2, i2 = first_argmax(jnp.where(lane == i1, -jnp.inf, el))
    e2 = jnp.exp(v2 - v1)
    p1 = p_group / (1.0 + e2)
    gates_ref[...] = jnp.where(lane == i1, p1, jnp.where(lane == i2, p1 * e2, 0.0))


def _outproj(attn, ret, x, w_a, w_r, again, fgain, rw, rb, tm=512):
    t, d = x.shape
    row = lambda w: pl.BlockSpec((tm, w), lambda i: (i, 0))
    full = lambda r, c: pl.BlockSpec((r, c), lambda i: (0, 0))
    return pl.pallas_call(
        _outproj_kernel,
        grid=(t // tm,),
        in_specs=[row(ATTN_WIDTH), row(RET_WIDTH), row(d), full(ATTN_WIDTH, d), full(RET_WIDTH, d),
                  full(1, ATTN_WIDTH), full(1, d), full(d, LANES), full(1, LANES)],
        out_specs=[row(d), row(d), row(LANES)],
        out_shape=[jax.ShapeDtypeStruct((t, d), F32), jax.ShapeDtypeStruct((t, d), BF16),
                   jax.ShapeDtypeStruct((t, LANES), F32)],
        compiler_params=pltpu.CompilerParams(
            dimension_semantics=("arbitrary",), vmem_limit_bytes=VMEM_LIMIT),
        name="outproj",
    )(attn, ret, x, w_a, w_r, again, fgain, rw, rb)


def _moe_kernel(hn_ref, h_ref, gates_ref, w1_ref, w3_ref, w2_ref, ngain_ref, o_ref):
    e = pl.program_id(1)

    @pl.when(e == 0)
    def _():
        o_ref[...] = h_ref[...]

    x = hn_ref[...]
    lane = lax.broadcasted_iota(jnp.int32, gates_ref.shape, 1)
    gate = jnp.sum(jnp.where(lane == e, gates_ref[...], 0.0), axis=-1, keepdims=True)
    a = jnp.dot(x, w1_ref[...], preferred_element_type=F32)
    b = jnp.dot(x, w3_ref[...], preferred_element_type=F32)
    hidden = (a * jax.nn.sigmoid(a) * b * gate).astype(BF16)
    o_ref[...] += jnp.dot(hidden, w2_ref[...], preferred_element_type=F32)

    @pl.when(e == pl.num_programs(1) - 1)
    def _():
        o_ref[...] = _rms(o_ref[...], ngain_ref[...])


def _moe(hn, h, gates, w1, w3, w2, ngain, tm=1024):
    t, d = hn.shape
    row = lambda w: pl.BlockSpec((tm, w), lambda i, e: (i, 0))
    return pl.pallas_call(
        _moe_kernel,
        grid=(t // tm, N_EXPERTS),
        in_specs=[row(d), row(d), row(LANES),
                  pl.BlockSpec((None, d, EXPERT_FF), lambda i, e: (e, 0, 0)),
                  pl.BlockSpec((None, d, EXPERT_FF), lambda i, e: (e, 0, 0)),
                  pl.BlockSpec((None, EXPERT_FF, d), lambda i, e: (e, 0, 0)),
                  pl.BlockSpec((1, d), lambda i, e: (0, 0))],
        out_specs=row(d),
        out_shape=jax.ShapeDtypeStruct((t, d), F32),
        compiler_params=pltpu.CompilerParams(
            dimension_semantics=("arbitrary", "arbitrary"), vmem_limit_bytes=VMEM_LIMIT),
        name="moe",
    )(hn, h, gates, w1, w3, w2, ngain)


def _rotary_tables(s):
    half = HEAD_DIM // 2
    inv = ROPE_BASE ** (-jnp.arange(half, dtype=F32) / half)
    ang = jnp.arange(s, dtype=F32)[:, None] * inv[None, :]
    cos, sin = jnp.cos(ang), jnp.sin(ang)
    cos_t = jnp.tile(jnp.concatenate([cos, cos], axis=-1), (1, LANES // HEAD_DIM))
    sin_t = jnp.tile(jnp.concatenate([-sin, sin], axis=-1), (1, LANES // HEAD_DIM))
    return cos_t, sin_t


def kernel(x, w_in, w_out, norm_mix, norm_ffn, norm_final, attn_out_gain, rel_bias, ret_decay_fwd, ret_decay_bwd, router_group_w, router_group_b, router_expert_w, router_expert_b, expert_w1, expert_w3, expert_w2):
    b, s, d = x.shape
    depth = w_in.shape[0]
    cos_t, sin_t = _rotary_tables(s)
    bias_tabs = _attn_bias_tables(rel_bias)
    h = x
    for layer in range(depth):
        aq, ak, av, rq, rk, rv, rg = _inproj(
            h, norm_mix[layer][None], w_in[layer].astype(BF16), cos_t, sin_t)
        attn = _attention(aq, ak, av, bias_tabs)
        ret = _retention(rq, rk, rv, rg, *_retention_tables(ret_decay_fwd[layer], ret_decay_bwd[layer]))

        w_o = w_out[layer].astype(BF16)
        rw = jnp.concatenate(
            [jnp.transpose(router_expert_w[layer], (1, 0, 2)).reshape(d, N_EXPERTS),
             router_group_w[layer],
             jnp.zeros((d, LANES - N_EXPERTS - N_GROUPS), F32)], axis=1).astype(BF16)
        rb = jnp.concatenate(
            [router_expert_b[layer].reshape(N_EXPERTS), router_group_b[layer],
             jnp.zeros((LANES - N_EXPERTS - N_GROUPS,), F32)])[None].astype(F32)
        h1, hn, gates = _outproj(
            attn.reshape(b * s, ATTN_WIDTH), ret.reshape(b * s, RET_WIDTH), h.reshape(b * s, d),
            w_o[:ATTN_WIDTH], w_o[ATTN_WIDTH:], attn_out_gain[layer][None], norm_ffn[layer][None],
            rw, rb)
        assert depth == 1, "the MoE kernel fuses the final norm, so it must run on the last layer"
        out = _moe(hn, h1, gates, expert_w1[layer].astype(BF16), expert_w3[layer].astype(BF16),
                   expert_w2[layer].astype(BF16), norm_final[None])
        h = out.reshape(b, s, d)
    return h
```

```python
import functools
import math

import jax
import jax.numpy as jnp
from jax import lax
from jax.experimental import pallas as pl
from jax.experimental.pallas import tpu as pltpu

F32 = jnp.float32
BF16 = jnp.bfloat16

D_MODEL = 1024
HEAD_DIM = 64
ATTN_WIDTH = 512
RET_WIDTH = 512
N_HEADS = 8
PAIR = 2 * HEAD_DIM
N_PAIRS = N_HEADS // 2
ATTN_DILATIONS = (1, 4, 16)
ATTN_RADIUS = 64
N_BUCKETS = 32
REL_MAX_DIST = 1024
ROPE_BASE = 10000.0
N_GROUPS = 4
EXPERTS_PER_GROUP = 4
N_EXPERTS = 16
EXPERT_FF = 512
EPS = 1e-6
NEG_INF = -1e30

LANES = 128
ATTN_TQ = 128
ATTN_W = 256
N_BIAS_VARIANTS = 8
RET_CHUNK = 256
ROUTE_TM = 512
SORT_ROWS = 640
CHUNK = 16
MOE_TM = 512
XS_W = D_MODEL + 2 * LANES
VMEM_LIMIT = 48 * 1024 * 1024


def _rms(x, gain):
    return x * lax.rsqrt(jnp.mean(x * x, axis=-1, keepdims=True) + EPS) * gain


def _inproj_kernel(x_ref, gain_ref, w_ref, cos_ref, sin_ref,
                   aq_ref, ak_ref, av_ref, rq_ref, rk_ref, rv_ref, rg_ref):
    xn = _rms(x_ref[...], gain_ref[...]).astype(BF16)

    def seg(i):
        return jnp.dot(xn, w_ref[:, i * 512:(i + 1) * 512], preferred_element_type=F32)

    def rotary(t):
        cos, sin = cos_ref[...], sin_ref[...]
        first_half = (lax.broadcasted_iota(jnp.int32, (1, LANES), 1) % HEAD_DIM) < HEAD_DIM // 2
        outs = []
        for j in range(t.shape[1] // LANES):
            tj = t[:, j * LANES:(j + 1) * LANES]
            partner = jnp.where(first_half, pltpu.roll(tj, LANES - 32, 1), pltpu.roll(tj, 32, 1))
            outs.append(tj * cos + partner * sin)
        return jnp.concatenate(outs, axis=1)

    aq_ref[...] = (seg(0) * (HEAD_DIM ** -0.5)).astype(BF16)
    ak_ref[...] = seg(1).astype(BF16)
    av_ref[...] = seg(2).astype(BF16)
    rq_ref[...] = rotary(seg(3)).astype(BF16)
    rk_ref[...] = (rotary(seg(4)) * (HEAD_DIM ** -0.5)).astype(BF16)
    rv_ref[...] = seg(5).astype(BF16)
    rg_ref[...] = seg(6).astype(BF16)


def _inproj(x, gain, w_in, cos_t, sin_t, tm=512):
    b, s, d = x.shape
    n = w_in.shape[1]
    out = jax.ShapeDtypeStruct((b, s, 512), BF16)
    ospec = pl.BlockSpec((None, tm, 512), lambda si, bi: (bi, si, 0))
    return pl.pallas_call(
        _inproj_kernel,
        grid=(s // tm, b),
        in_specs=[
            pl.BlockSpec((None, tm, d), lambda si, bi: (bi, si, 0)),
            pl.BlockSpec((1, d), lambda si, bi: (0, 0)),
            pl.BlockSpec((d, n), lambda si, bi: (0, 0)),
            pl.BlockSpec((tm, LANES), lambda si, bi: (si, 0)),
            pl.BlockSpec((tm, LANES), lambda si, bi: (si, 0)),
        ],
        out_specs=[ospec] * 7,
        out_shape=[out] * 7,
        compiler_params=pltpu.CompilerParams(
            dimension_semantics=("arbitrary", "arbitrary"), vmem_limit_bytes=VMEM_LIMIT),
        name="inproj",
    )(x, gain, w_in, cos_t, sin_t)


def _t5_bucket(rel):
    half = N_BUCKETS // 2
    max_exact = half // 2
    offset = jnp.where(rel > 0, half, 0)
    n = jnp.abs(rel)
    nf = jnp.maximum(n, 1).astype(F32)
    large = max_exact + (jnp.log(nf / max_exact) / math.log(REL_MAX_DIST / max_exact)
                         * (half - max_exact)).astype(jnp.int32)
    large = jnp.minimum(large, half - 1)
    return offset + jnp.where(n < max_exact, n, large)


def _attn_bias_tables(rel_bias):
    period = 2 * ATTN_W
    band = 2 * ATTN_RADIUS + 1
    rel = jnp.arange(-ATTN_RADIUS, ATTN_RADIUS + 1)
    rows = []
    for dil, offs in ((1, (0, 64, 128)), (4, (0, 64, 128)), (16, (0, 128))):
        vals = rel_bias[_t5_bucket(rel * dil)].astype(F32).T
        for off in offs:
            lo = off - ATTN_RADIUS
            pad = jnp.full((N_HEADS, period - band), NEG_INF, F32)
            if lo >= 0:
                row = jnp.concatenate([pad[:, :lo], vals, pad[:, lo:]], axis=1)
            else:
                row = jnp.concatenate([vals[:, -lo:], pad, vals[:, :-lo]], axis=1)
            rows.append(row)
    v = jnp.stack(rows, axis=1)
    t = jnp.tile(v, (1, 1, ATTN_TQ))[:, :, :ATTN_TQ * (period - 1)]
    t = t.reshape(N_HEADS, N_BIAS_VARIANTS, ATTN_TQ, period - 1)[..., :ATTN_W]
    own_half = (jnp.arange(ATTN_W)[None, :] // ATTN_TQ) == jnp.arange(2)[:, None]
    t = t.at[:, 6:8].set(jnp.where(own_half[None, :, None, :], t[:, 6:8], NEG_INF))
    return t.reshape(N_PAIRS, 2, N_BIAS_VARIANTS, ATTN_TQ, ATTN_W)


def _attention_kernel(q_ref, k_ref, v_ref, bias_ref, o_ref,
                      nat_ref, qp_ref, kp_ref, vp_ref, acc_ref, m_ref, l_ref, out_ref):
    s = q_ref.shape[0]
    n_tiles = s // ATTN_TQ
    lane = lax.broadcasted_iota(jnp.int32, (1, PAIR), 1)
    head0 = lane < HEAD_DIM

    def run_branch(bi, dil, qs_ref, ks_ref, vs_ref):
        sub_len = s // dil
        tiles_per_sub = sub_len // ATTN_TQ

        def tile(t, carry):
            q0 = pl.multiple_of(t * ATTN_TQ, ATTN_TQ)
            if tiles_per_sub == 1:
                ws = pl.multiple_of((t // 2) * ATTN_W, ATTN_W)
                var = 6 + t % 2
            else:
                pos = t % tiles_per_sub
                sub_lo = (t // tiles_per_sub) * sub_len
                ws = jnp.clip(q0 - 64, sub_lo, sub_lo + sub_len - ATTN_W)
                ws = pl.multiple_of(ws, 64)
                var = jnp.where(pos == 0, 0, jnp.where(pos == tiles_per_sub - 1, 2, 1)) + 3 * bi
            q = qs_ref[pl.ds(q0, ATTN_TQ), :]
            k = ks_ref[pl.ds(ws, ATTN_W), :]
            v = vs_ref[pl.ds(ws, ATTN_W), :]
            outs, ms, ls = [], [], []
            for h in range(2):
                qh = jnp.where(head0 if h == 0 else ~head0, q, jnp.zeros_like(q))
                sc = lax.dot_general(qh, k, (((1,), (1,)), ((), ())), preferred_element_type=F32)
                sc = sc + bias_ref[h, var]
                m = jnp.max(sc, axis=-1, keepdims=True)
                p = jnp.exp(sc - m)
                ls.append(jnp.sum(p, axis=-1, keepdims=True))
                ms.append(m)
                outs.append(jnp.dot(p.astype(BF16), v, preferred_element_type=F32))
            acc_ref[bi, pl.ds(q0, ATTN_TQ), :] = jnp.where(head0, outs[0], outs[1])
            m_ref[bi, pl.ds(q0, ATTN_TQ), :] = jnp.where(head0, ms[0], ms[1])
            l_ref[bi, pl.ds(q0, ATTN_TQ), :] = jnp.where(head0, ls[0], ls[1])
            return carry

        lax.fori_loop(0, n_tiles, tile, 0, unroll=8)

    def permute(src_ref, dst_ref, dil):
        sub_len = s // dil
        nat_ref[...] = src_ref[...].astype(F32)
        for r in range(dil):
            dst_ref[r * sub_len:(r + 1) * sub_len, :] = (
                nat_ref[pl.ds(r, sub_len, stride=dil), :].astype(BF16))

    run_branch(0, 1, q_ref, k_ref, v_ref)
    for bi, dil in ((1, 4), (2, 16)):
        permute(q_ref, qp_ref, dil)
        permute(k_ref, kp_ref, dil)
        permute(v_ref, vp_ref, dil)
        run_branch(bi, dil, qp_ref, kp_ref, vp_ref)

    rows = s // 16
    for r in range(16):
        sl = (pl.ds(r, rows, stride=16),
              pl.ds((r % 4) * (s // 4) + r // 4, rows, stride=4),
              pl.ds(r * rows, rows))
        m = [m_ref[bi, sl[bi], :] for bi in range(3)]
        mx = jnp.maximum(jnp.maximum(m[0], m[1]), m[2])
        num = jnp.zeros((rows, PAIR), F32)
        den = jnp.zeros((rows, PAIR), F32)
        for bi in range(3):
            e = jnp.exp(m[bi] - mx)
            num = num + e * acc_ref[bi, sl[bi], :]
            den = den + e * l_ref[bi, sl[bi], :]
        out_ref[pl.ds(r, rows, stride=16), :] = num / den
    o_ref[...] = out_ref[...].astype(o_ref.dtype)


def _attention(aq, ak, av, bias_tabs):
    b, s, _ = aq.shape
    spec = pl.BlockSpec((None, s, PAIR), lambda hp, bi: (bi, 0, hp))
    return pl.pallas_call(
        _attention_kernel,
        grid=(N_PAIRS, b),
        in_specs=[spec, spec, spec,
                  pl.BlockSpec((None, 2, N_BIAS_VARIANTS, ATTN_TQ, ATTN_W),
                               lambda hp, bi: (hp, 0, 0, 0, 0))],
        out_specs=spec,
        out_shape=jax.ShapeDtypeStruct((b, s, ATTN_WIDTH), BF16),
        scratch_shapes=[
            pltpu.VMEM((s, PAIR), F32),
            pltpu.VMEM((s, PAIR), BF16),
            pltpu.VMEM((s, PAIR), BF16),
            pltpu.VMEM((s, PAIR), BF16),
            pltpu.VMEM((3, s, PAIR), F32),
            pltpu.VMEM((3, s, PAIR), F32),
            pltpu.VMEM((3, s, PAIR), F32),
            pltpu.VMEM((s, PAIR), F32),
        ],
        compiler_params=pltpu.CompilerParams(
            dimension_semantics=("arbitrary", "arbitrary"), vmem_limit_bytes=VMEM_LIMIT),
        name="attention",
    )(aq, ak, av, bias_tabs)


def _retention_tables(decay_fwd, decay_bwd):
    c = RET_CHUNK
    lg_f = -jnp.exp(decay_fwd.astype(F32))
    lg_b = -jnp.exp(decay_bwd.astype(F32))
    idx = jnp.arange(c, dtype=F32)
    rel = idx[:, None] - idx[None, :]
    dmat = jnp.where(rel >= 0,
                     jnp.exp(lg_f[:, None, None] * jnp.maximum(rel, 0.0)[None]),
                     jnp.exp(lg_b[:, None, None] * jnp.maximum(-rel, 0.0)[None]))
    dmat = dmat.reshape(N_PAIRS, 2, c, c)

    def lanes(v):
        v = v.reshape(N_PAIRS, 2, -1)
        return jnp.repeat(jnp.transpose(v, (0, 2, 1)), HEAD_DIM, axis=2)

    vec = jnp.stack([
        lanes(jnp.exp(lg_f[:, None] * (idx + 1.0)[None])),
        lanes(jnp.exp(lg_f[:, None] * (c - 1.0 - idx)[None])),
        lanes(jnp.exp(lg_b[:, None] * (c - idx)[None])),
        lanes(jnp.exp(lg_b[:, None] * idx[None])),
    ], axis=1)
    same_head = (jnp.arange(PAIR)[:, None] // HEAD_DIM) == (jnp.arange(PAIR)[None, :] // HEAD_DIM)
    cd = jnp.stack([lanes(jnp.exp(lg_f * c)[:, None]), lanes(jnp.exp(lg_b * c)[:, None])], axis=1)
    cd = jnp.transpose(cd, (0, 1, 3, 2)) * same_head[None, None].astype(F32)
    return dmat, vec, cd


def _retention_kernel(q_ref, k_ref, v_ref, g_ref, dmat_ref, vec_ref, cd_ref, o_ref, y_ref, kv_ref):
    s = q_ref.shape[0]
    c = RET_CHUNK
    nc = s // c
    lane = lax.broadcasted_iota(jnp.int32, (1, PAIR), 1)
    head0 = lane < HEAD_DIM
    same_head = ((lax.broadcasted_iota(jnp.int32, (PAIR, PAIR), 0) // HEAD_DIM)
                 == (lax.broadcasted_iota(jnp.int32, (PAIR, PAIR), 1) // HEAD_DIM))

    for n in range(nc):
        rows = slice(n * c, (n + 1) * c)
        q, k, v = q_ref[rows, :], k_ref[rows, :], v_ref[rows, :]
        outs = []
        for h in range(2):
            qh = jnp.where(head0 if h == 0 else ~head0, q, jnp.zeros_like(q))
            sc = lax.dot_general(qh, k, (((1,), (1,)), ((), ())), preferred_element_type=F32)
            outs.append(jnp.dot((sc * dmat_ref[h]).astype(BF16), v, preferred_element_type=F32))
        y_ref[rows, :] = jnp.where(head0, outs[0], outs[1])
        kf = k.astype(F32)
        kcat = jnp.concatenate([(kf * vec_ref[1]).astype(BF16), (kf * vec_ref[3]).astype(BF16)], axis=1)
        kv = lax.dot_general(kcat, v, (((0,), (0,)), ((), ())), preferred_element_type=F32)
        kv_ref[0, n] = jnp.where(same_head, kv[:PAIR], 0.0)
        kv_ref[1, n] = jnp.where(same_head, kv[PAIR:], 0.0)

    def cross(n, state, qdec):
        rows = slice(n * c, (n + 1) * c)
        qd = (q_ref[rows, :].astype(F32) * qdec).astype(BF16)
        y_ref[rows, :] += jnp.dot(qd, state.astype(BF16), preferred_element_type=F32)

    state = jnp.zeros((PAIR, PAIR), F32)
    for n in range(nc):
        cross(n, state, vec_ref[0])
        state = state * cd_ref[0] + kv_ref[0, n]
    state = jnp.zeros((PAIR, PAIR), F32)
    for n in reversed(range(nc)):
        cross(n, state, vec_ref[2])
        state = state * cd_ref[1] + kv_ref[1, n]

    y = y_ref[...]
    y2 = y * y
    ms0 = jnp.sum(jnp.where(head0, y2, 0.0), axis=-1, keepdims=True)
    ms1 = jnp.sum(jnp.where(head0, 0.0, y2), axis=-1, keepdims=True)
    ms = jnp.where(head0, ms0, ms1) * (1.0 / HEAD_DIM)
    g = g_ref[...].astype(F32)
    o_ref[...] = (y * lax.rsqrt(ms + EPS) * (g * jax.nn.sigmoid(g))).astype(o_ref.dtype)


def _retention(rq, rk, rv, rg, dmat, vec, cd):
    b, s, _ = rq.shape
    c = RET_CHUNK
    spec = pl.BlockSpec((None, s, PAIR), lambda hp, bi: (bi, 0, hp))
    return pl.pallas_call(
        _retention_kernel,
        grid=(N_PAIRS, b),
        in_specs=[spec, spec, spec, spec,
                  pl.BlockSpec((None, 2, c, c), lambda hp, bi: (hp, 0, 0, 0)),
                  pl.BlockSpec((None, 4, c, PAIR), lambda hp, bi: (hp, 0, 0, 0)),
                  pl.BlockSpec((None, 2, PAIR, PAIR), lambda hp, bi: (hp, 0, 0, 0))],
        out_specs=spec,
        out_shape=jax.ShapeDtypeStruct((b, s, RET_WIDTH), BF16),
        scratch_shapes=[pltpu.VMEM((s, PAIR), F32),
                        pltpu.VMEM((2, s // c, PAIR, PAIR), F32)],
        compiler_params=pltpu.CompilerParams(
            dimension_semantics=("arbitrary", "arbitrary"), vmem_limit_bytes=VMEM_LIMIT),
        name="retention",
    )(rq, rk, rv, rg, dmat, vec, cd)


def _outproj_kernel(attn_ref, ret_ref, x_ref, wa_ref, wr_ref, again_ref, fgain_ref,
                    rw_ref, rb_ref, h_ref, xs_ref, dest_ref, cnt_ref):
    a = _rms(attn_ref[...].astype(F32), again_ref[...]).astype(BF16)
    h = (x_ref[...]
         + jnp.dot(a, wa_ref[...], preferred_element_type=F32)
         + jnp.dot(ret_ref[...], wr_ref[...], preferred_element_type=F32))
    h_ref[...] = h
    hn = _rms(h, fgain_ref[...]).astype(BF16)

    logits = jnp.dot(hn, rw_ref[...], preferred_element_type=F32) + rb_ref[...]
    tm = logits.shape[0]
    lane = lax.broadcasted_iota(jnp.int32, logits.shape, 1)
    big = jnp.int32(LANES)

    def first_argmax(vals):
        top = jnp.max(vals, axis=-1, keepdims=True)
        return top, jnp.min(jnp.where(vals == top, lane, big), axis=-1, keepdims=True)

    gl = jnp.where((lane >= N_EXPERTS) & (lane < N_EXPERTS + N_GROUPS), logits, -jnp.inf)
    gmax, gidx = first_argmax(gl)
    p_group = 1.0 / jnp.sum(jnp.exp(gl - gmax), axis=-1, keepdims=True)
    grp = gidx - N_EXPERTS
    lo = grp * EXPERTS_PER_GROUP
    el = jnp.where((lane >= lo) & (lane < lo + EXPERTS_PER_GROUP), logits, -jnp.inf)
    v1, i1 = first_argmax(el)
    v2, i2 = first_argmax(jnp.where(lane == i1, -jnp.inf, el))
    e2 = jnp.exp(v2 - v1)
    p1 = p_group / (1.0 + e2)
    gates = jnp.where(lane == i1, p1, jnp.where(lane == i2, p1 * e2, 0.0))

    onehot = jnp.where(lane == grp, 1.0, 0.0)
    r_i = lax.broadcasted_iota(jnp.int32, (tm, tm), 0)
    c_i = lax.broadcasted_iota(jnp.int32, (tm, tm), 1)
    before = jnp.where(c_i < r_i, 1.0, 0.0).astype(BF16)
    rank = jnp.dot(before, onehot.astype(BF16), preferred_element_type=F32)
    cnt = jnp.sum(onehot, axis=0, keepdims=True)
    c16 = jnp.broadcast_to(jnp.ceil(cnt * (1.0 / CHUNK)) * CHUNK, (8, LANES))
    lane8 = lax.broadcasted_iota(jnp.int32, (8, LANES), 1)
    start = jnp.zeros((8, LANES), F32)
    for sft in range(1, N_GROUPS):
        start = start + jnp.where(lane8 >= sft, pltpu.roll(c16, sft, 1), 0.0)
    dest = jnp.sum(onehot * (start[0:1] + rank), axis=-1, keepdims=True)
    dest_ref[...] = jnp.broadcast_to(dest, (tm, LANES))
    cnt_ref[...] = jnp.where(lane8 < N_GROUPS, c16, pltpu.roll(start, N_GROUPS, 1)).astype(jnp.int32)

    g_hi = gates.astype(BF16)
    g_lo = (gates - g_hi.astype(F32)).astype(BF16)
    payload = jnp.concatenate([hn, g_hi, g_lo], axis=1)
    perm_t = jnp.where(lax.broadcasted_iota(jnp.int32, (tm, SORT_ROWS), 1) == dest.astype(jnp.int32),
                       1.0, 0.0).astype(BF16)
    xs_ref[...] = lax.dot_general(perm_t, payload, (((0,), (0,)), ((), ())),
                                  preferred_element_type=F32).astype(BF16)


def _outproj(attn, ret, x, w_a, w_r, again, fgain, rw, rb):
    t, d = x.shape
    tm = ROUTE_TM
    nt = t // tm
    row = lambda w: pl.BlockSpec((tm, w), lambda i: (i, 0))
    full = lambda r, c: pl.BlockSpec((r, c), lambda i: (0, 0))
    return pl.pallas_call(
        _outproj_kernel,
        grid=(nt,),
        in_specs=[row(ATTN_WIDTH), row(RET_WIDTH), row(d), full(ATTN_WIDTH, d), full(RET_WIDTH, d),
                  full(1, ATTN_WIDTH), full(1, d), full(d, LANES), full(1, LANES)],
        out_specs=[row(d), pl.BlockSpec((SORT_ROWS, XS_W), lambda i: (i, 0)), row(LANES),
                   pl.BlockSpec((None, 8, LANES), lambda i: (i, 0, 0))],
        out_shape=[jax.ShapeDtypeStruct((t, d), F32),
                   jax.ShapeDtypeStruct((nt * SORT_ROWS, XS_W), BF16),
                   jax.ShapeDtypeStruct((t, LANES), F32),
                   jax.ShapeDtypeStruct((nt, 8, LANES), jnp.int32)],
        compiler_params=pltpu.CompilerParams(
            dimension_semantics=("arbitrary",), vmem_limit_bytes=VMEM_LIMIT),
        name="outproj",
    )(attn, ret, x, w_a, w_r, again, fgain, rw, rb)


def _dispatch_tables(cnt, n_moe_tiles):
    c16 = cnt[:, 0, :N_GROUPS]
    start = cnt[:, 0, N_GROUPS:2 * N_GROUPS]
    nt = c16.shape[0]
    cum = jnp.cumsum(c16, axis=0)
    rows_g = cum[-1]
    tiles_g = (rows_g + MOE_TM - 1) // MOE_TM
    tile_end = jnp.cumsum(tiles_g)
    tile_base = tile_end - tiles_g
    m = jnp.arange(n_moe_tiles)
    tile_group = jnp.minimum(jnp.sum(m[:, None] >= tile_end[None, :], axis=1), N_GROUPS - 1)
    tile_used = (m < tile_end[-1]).astype(jnp.int32)
    cpt = MOE_TM // CHUNK
    k = jnp.arange(n_moe_tiles * cpt)
    g = tile_group[k // cpt]
    off = k * CHUNK - tile_base[g] * MOE_TM
    cum_g = cum.T[g]
    i = jnp.sum(off[:, None] >= cum_g, axis=1)
    valid = (off < rows_g[g]) & (tile_used[k // cpt] > 0)
    i_c = jnp.minimum(i, nt - 1)
    seg_lo = jnp.take_along_axis(cum_g, i_c[:, None], axis=1)[:, 0] - c16[i_c, g]
    src_row = i_c * SORT_ROWS + start[i_c, g] + (off - seg_lo)
    src_chunk = jnp.where(valid, src_row // CHUNK, -1)
    return tile_group.astype(jnp.int32), tile_used, src_chunk.astype(jnp.int32)


def _moe_kernel(grp_ref, used_ref, src_ref, xs_hbm, w1_ref, w3_ref, w2_ref, ys_in_hbm, ys_hbm,
                xbuf, obuf, in_sem, out_sem):
    del ys_in_hbm
    m = pl.program_id(0)
    n_tiles = pl.num_programs(0)
    cpt = MOE_TM // CHUNK
    slot = m % 2

    def rows(c):
        return pl.ds(pl.multiple_of(c * CHUNK, CHUNK), CHUNK)

    def gather(tile, sl, wait):
        def body(c, carry):
            src = src_ref[tile * cpt + c]

            @pl.when(src >= 0)
            def _():
                cp = pltpu.make_async_copy(xs_hbm.at[rows(src), :], xbuf.at[sl, rows(c), :], in_sem.at[sl])
                cp.wait() if wait else cp.start()

            if not wait:
                @pl.when(src < 0)
                def _():
                    xbuf[sl, rows(c), :] = jnp.zeros((CHUNK, XS_W), BF16)
            return carry
        lax.fori_loop(0, cpt, body, 0)

    def scatter(tile, sl, wait):
        def body(c, carry):
            src = src_ref[tile * cpt + c]

            @pl.when(src >= 0)
            def _():
                cp = pltpu.make_async_copy(obuf.at[sl, rows(c), :], ys_hbm.at[rows(src), :], out_sem.at[sl])
                cp.wait() if wait else cp.start()
            return carry
        lax.fori_loop(0, cpt, body, 0)

    @pl.when(m == 0)
    def _():
        gather(0, 0, False)

    @pl.when(m + 1 < n_tiles)
    def _():
        gather(m + 1, 1 - slot, False)

    gather(m, slot, True)

    @pl.when(m >= 2)
    def _():
        scatter(m - 2, slot, True)

    @pl.when(used_ref[m] > 0)
    def _():
        x = xbuf[slot, :, :D_MODEL]
        gate = (xbuf[slot, :, D_MODEL:D_MODEL + LANES].astype(F32)
                + xbuf[slot, :, D_MODEL + LANES:].astype(F32))
        lane = lax.broadcasted_iota(jnp.int32, gate.shape, 1)
        base = grp_ref[m] * EXPERTS_PER_GROUP
        acc = jnp.zeros((MOE_TM, D_MODEL), F32)
        for j in range(EXPERTS_PER_GROUP):
            a = jnp.dot(x, w1_ref[j], preferred_element_type=F32)
            b = jnp.dot(x, w3_ref[j], preferred_element_type=F32)
            gj = jnp.sum(jnp.where(lane == base + j, gate, 0.0), axis=-1, keepdims=True)
            hid = (a * jax.nn.sigmoid(a) * b * gj).astype(BF16)
            acc = acc + jnp.dot(hid, w2_ref[j], preferred_element_type=F32)
        obuf[slot] = acc.astype(BF16)

    scatter(m, slot, False)

    @pl.when(m == n_tiles - 1)
    def _():
        scatter(m, slot, True)

        @pl.when(m >= 1)
        def _():
            scatter(m - 1, 1 - slot, True)


def _moe(xs, w1, w3, w2, tile_group, tile_used, src_chunk):
    n_moe_tiles = tile_group.shape[0]
    rows = xs.shape[0]
    d = D_MODEL
    wspec = lambda r, c: pl.BlockSpec((EXPERTS_PER_GROUP, r, c), lambda m, grp, used, src: (grp[m], 0, 0))
    ys0 = jnp.zeros((rows, d), BF16)
    return pl.pallas_call(
        _moe_kernel,
        grid_spec=pltpu.PrefetchScalarGridSpec(
            num_scalar_prefetch=3,
            grid=(n_moe_tiles,),
            in_specs=[pl.BlockSpec(memory_space=pl.ANY),
                      wspec(d, EXPERT_FF), wspec(d, EXPERT_FF), wspec(EXPERT_FF, d),
                      pl.BlockSpec(memory_space=pl.ANY)],
            out_specs=pl.BlockSpec(memory_space=pl.ANY),
            scratch_shapes=[pltpu.VMEM((2, MOE_TM, XS_W), BF16),
                            pltpu.VMEM((2, MOE_TM, d), BF16),
                            pltpu.SemaphoreType.DMA((2,)),
                            pltpu.SemaphoreType.DMA((2,))]),
        out_shape=jax.ShapeDtypeStruct((rows, d), BF16),
        input_output_aliases={7: 0},
        compiler_params=pltpu.CompilerParams(
            dimension_semantics=("arbitrary",), vmem_limit_bytes=VMEM_LIMIT),
        name="moe",
    )(tile_group, tile_used, src_chunk, xs, w1, w3, w2, ys0)


def _combine_kernel(ys_ref, h_ref, dest_ref, gain_ref, o_ref):
    tm = h_ref.shape[0]
    dest = dest_ref[:, 0:1].astype(jnp.int32)
    perm_t = jnp.where(lax.broadcasted_iota(jnp.int32, (tm, SORT_ROWS), 1) == dest, 1.0, 0.0).astype(BF16)
    moe = jnp.dot(perm_t, ys_ref[...], preferred_element_type=F32)
    o_ref[...] = _rms(h_ref[...] + moe, gain_ref[...])


def _combine(ys, h, dest, gain):
    t, d = h.shape
    tm = ROUTE_TM
    return pl.pallas_call(
        _combine_kernel,
        grid=(t // tm,),
        in_specs=[pl.BlockSpec((SORT_ROWS, d), lambda i: (i, 0)),
                  pl.BlockSpec((tm, d), lambda i: (i, 0)),
                  pl.BlockSpec((tm, LANES), lambda i: (i, 0)),
                  pl.BlockSpec((1, d), lambda i: (0, 0))],
        out_specs=pl.BlockSpec((tm, d), lambda i: (i, 0)),
        out_shape=jax.ShapeDtypeStruct((t, d), F32),
        compiler_params=pltpu.CompilerParams(
            dimension_semantics=("arbitrary",), vmem_limit_bytes=VMEM_LIMIT),
        name="combine",
    )(ys, h, dest, gain)


def _rotary_tables(s):
    half = HEAD_DIM // 2
    inv = ROPE_BASE ** (-jnp.arange(half, dtype=F32) / half)
    ang = jnp.arange(s, dtype=F32)[:, None] * inv[None, :]
    cos, sin = jnp.cos(ang), jnp.sin(ang)
    cos_t = jnp.tile(jnp.concatenate([cos, cos], axis=-1), (1, LANES // HEAD_DIM))
    sin_t = jnp.tile(jnp.concatenate([-sin, sin], axis=-1), (1, LANES // HEAD_DIM))
    return cos_t, sin_t


def kernel(x, w_in, w_out, norm_mix, norm_ffn, norm_final, attn_out_gain, rel_bias, ret_decay_fwd, ret_decay_bwd, router_group_w, router_group_b, router_expert_w, router_expert_b, expert_w1, expert_w3, expert_w2):
    b, s, d = x.shape
    depth = w_in.shape[0]
    cos_t, sin_t = _rotary_tables(s)
    bias_tabs = _attn_bias_tables(rel_bias)
    h = x
    for layer in range(depth):
        aq, ak, av, rq, rk, rv, rg = _inproj(
            h, norm_mix[layer][None], w_in[layer].astype(BF16), cos_t, sin_t)
        attn = _attention(aq, ak, av, bias_tabs)
        ret = _retention(rq, rk, rv, rg, *_retention_tables(ret_decay_fwd[layer], ret_decay_bwd[layer]))

        w_o = w_out[layer].astype(BF16)
        rw = jnp.concatenate(
            [jnp.transpose(router_expert_w[layer], (1, 0, 2)).reshape(d, N_EXPERTS),
             router_group_w[layer],
             jnp.zeros((d, LANES - N_EXPERTS - N_GROUPS), F32)], axis=1).astype(BF16)
        rb = jnp.concatenate(
            [router_expert_b[layer].reshape(N_EXPERTS), router_group_b[layer],
             jnp.zeros((LANES - N_EXPERTS - N_GROUPS,), F32)])[None].astype(F32)
        h1, xs, dest, cnt = _outproj(
            attn.reshape(b * s, ATTN_WIDTH), ret.reshape(b * s, RET_WIDTH), h.reshape(b * s, d),
            w_o[:ATTN_WIDTH], w_o[ATTN_WIDTH:], attn_out_gain[layer][None], norm_ffn[layer][None],
            rw, rb)
        n_route_tiles = (b * s) // ROUTE_TM
        n_moe_tiles = (b * s + n_route_tiles * N_GROUPS * (CHUNK - 1)) // MOE_TM + N_GROUPS
        tile_group, tile_used, src_chunk = _dispatch_tables(cnt, n_moe_tiles)
        ys = _moe(xs, expert_w1[layer].astype(BF16), expert_w3[layer].astype(BF16),
                  expert_w2[layer].astype(BF16), tile_group, tile_used, src_chunk)
        assert depth == 1, "the combine kernel fuses the final norm, so it must run on the last layer"
        h = _combine(ys, h1, dest, norm_final[None]).reshape(b, s, d)
    return h
```

```python
import functools
import math

import jax
import jax.numpy as jnp
from jax import lax
from jax.experimental import pallas as pl
from jax.experimental.pallas import tpu as pltpu

F32 = jnp.float32
BF16 = jnp.bfloat16

D_MODEL = 1024
HEAD_DIM = 64
ATTN_WIDTH = 512
RET_WIDTH = 512
N_HEADS = 8
PAIR = 2 * HEAD_DIM
N_PAIRS = N_HEADS // 2
ATTN_DILATIONS = (1, 4, 16)
ATTN_RADIUS = 64
N_BUCKETS = 32
REL_MAX_DIST = 1024
ROPE_BASE = 10000.0
N_GROUPS = 4
EXPERTS_PER_GROUP = 4
N_EXPERTS = 16
EXPERT_FF = 512
EPS = 1e-6
NEG_INF = -1e30

LANES = 128
ATTN_TQ = 128
ATTN_W = 256
N_BIAS_VARIANTS = 8
RET_CHUNK = 256
ROUTE_TM = 512
SORT_ROWS = 640
CHUNK = 16
MOE_TM = 512
XS_W = D_MODEL + 2 * LANES
VMEM_LIMIT = 48 * 1024 * 1024


def _rms(x, gain):
    return x * lax.rsqrt(jnp.mean(x * x, axis=-1, keepdims=True) + EPS) * gain


def _inproj_kernel(x_ref, gain_ref, w_ref, cos_ref, sin_ref,
                   aq_ref, ak_ref, av_ref, rq_ref, rk_ref, rv_ref, rg_ref):
    xn = _rms(x_ref[...], gain_ref[...]).astype(BF16)

    def seg(i):
        return jnp.dot(xn, w_ref[:, i * 512:(i + 1) * 512], preferred_element_type=F32)

    def rotary(t):
        cos, sin = cos_ref[...], sin_ref[...]
        first_half = (lax.broadcasted_iota(jnp.int32, (1, LANES), 1) % HEAD_DIM) < HEAD_DIM // 2
        outs = []
        for j in range(t.shape[1] // LANES):
            tj = t[:, j * LANES:(j + 1) * LANES]
            partner = jnp.where(first_half, pltpu.roll(tj, LANES - 32, 1), pltpu.roll(tj, 32, 1))
            outs.append(tj * cos + partner * sin)
        return jnp.concatenate(outs, axis=1)

    aq_ref[...] = (seg(0) * (HEAD_DIM ** -0.5)).astype(BF16)
    ak_ref[...] = seg(1).astype(BF16)
    av_ref[...] = seg(2).astype(BF16)
    rq_ref[...] = rotary(seg(3)).astype(BF16)
    rk_ref[...] = (rotary(seg(4)) * (HEAD_DIM ** -0.5)).astype(BF16)
    rv_ref[...] = seg(5).astype(BF16)
    rg_ref[...] = seg(6).astype(BF16)


def _inproj(x, gain, w_in, cos_t, sin_t, tm=512):
    b, s, d = x.shape
    n = w_in.shape[1]
    out = jax.ShapeDtypeStruct((b, s, 512), BF16)
    ospec = pl.BlockSpec((None, tm, 512), lambda si, bi: (bi, si, 0))
    return pl.pallas_call(
        _inproj_kernel,
        grid=(s // tm, b),
        in_specs=[
            pl.BlockSpec((None, tm, d), lambda si, bi: (bi, si, 0)),
            pl.BlockSpec((1, d), lambda si, bi: (0, 0)),
            pl.BlockSpec((d, n), lambda si, bi: (0, 0)),
            pl.BlockSpec((tm, LANES), lambda si, bi: (si, 0)),
            pl.BlockSpec((tm, LANES), lambda si, bi: (si, 0)),
        ],
        out_specs=[ospec] * 7,
        out_shape=[out] * 7,
        compiler_params=pltpu.CompilerParams(
            dimension_semantics=("arbitrary", "arbitrary"), vmem_limit_bytes=VMEM_LIMIT),
        name="inproj",
    )(x, gain, w_in, cos_t, sin_t)


def _t5_bucket(rel):
    half = N_BUCKETS // 2
    max_exact = half // 2
    offset = jnp.where(rel > 0, half, 0)
    n = jnp.abs(rel)
    nf = jnp.maximum(n, 1).astype(F32)
    large = max_exact + (jnp.log(nf / max_exact) / math.log(REL_MAX_DIST / max_exact)
                         * (half - max_exact)).astype(jnp.int32)
    large = jnp.minimum(large, half - 1)
    return offset + jnp.where(n < max_exact, n, large)


def _attn_bias_rows(rel_bias):
    period = 2 * ATTN_W
    band = 2 * ATTN_RADIUS + 1
    rel = jnp.arange(-ATTN_RADIUS, ATTN_RADIUS + 1)
    rows = []
    for dil, offs in ((1, (0, 64, 128)), (4, (0, 64, 128)), (16, (0, 128))):
        vals = rel_bias[_t5_bucket(rel * dil)].astype(F32).T
        for off in offs:
            lo = off - ATTN_RADIUS
            pad = jnp.full((N_HEADS, period - band), NEG_INF, F32)
            if lo >= 0:
                row = jnp.concatenate([pad[:, :lo], vals, pad[:, lo:]], axis=1)
            else:
                row = jnp.concatenate([vals[:, -lo:], pad, vals[:, :-lo]], axis=1)
            rows.append(row)
    v = jnp.stack(rows, axis=1)
    return v.reshape(N_PAIRS, 2 * N_BIAS_VARIANTS, period)


def _attention_kernel(q_ref, k_ref, v_ref, rows_ref, o_ref,
                      bias_ref, nat_ref, qp_ref, kp_ref, vp_ref, acc_ref, m_ref, l_ref, out_ref):
    s = q_ref.shape[0]
    n_tiles = s // ATTN_TQ
    lane = lax.broadcasted_iota(jnp.int32, (1, PAIR), 1)
    head0 = lane < HEAD_DIM

    @pl.when(pl.program_id(1) == 0)
    def _():
        col = lax.broadcasted_iota(jnp.int32, (ATTN_TQ, ATTN_W), 1)
        for idx in range(2 * N_BIAS_VARIANTS):
            gen = jnp.broadcast_to(rows_ref[idx:idx + 1, :], (ATTN_TQ, 2 * ATTN_W))
            tab = pltpu.roll(gen, 0, 1, stride=1, stride_axis=0)[:, :ATTN_W]
            var = idx % N_BIAS_VARIANTS
            if var >= 6:
                tab = jnp.where((col // ATTN_TQ) == var - 6, tab, NEG_INF)
            bias_ref[idx] = tab

    def run_branch(bi, dil, qs_ref, ks_ref, vs_ref):
        sub_len = s // dil
        tiles_per_sub = sub_len // ATTN_TQ

        def tile(t, carry):
            q0 = pl.multiple_of(t * ATTN_TQ, ATTN_TQ)
            if tiles_per_sub == 1:
                ws = pl.multiple_of((t // 2) * ATTN_W, ATTN_W)
                var = 6 + t % 2
            else:
                pos = t % tiles_per_sub
                sub_lo = (t // tiles_per_sub) * sub_len
                ws = jnp.clip(q0 - 64, sub_lo, sub_lo + sub_len - ATTN_W)
                ws = pl.multiple_of(ws, 64)
                var = jnp.where(pos == 0, 0, jnp.where(pos == tiles_per_sub - 1, 2, 1)) + 3 * bi
            q = qs_ref[pl.ds(q0, ATTN_TQ), :]
            k = ks_ref[pl.ds(ws, ATTN_W), :]
            v = vs_ref[pl.ds(ws, ATTN_W), :]
            outs, ms, ls = [], [], []
            for h in range(2):
                qh = jnp.where(head0 if h == 0 else ~head0, q, jnp.zeros_like(q))
                sc = lax.dot_general(qh, k, (((1,), (1,)), ((), ())), preferred_element_type=F32)
                sc = sc + bias_ref[h * N_BIAS_VARIANTS + var]
                m = jnp.max(sc, axis=-1, keepdims=True)
                p = jnp.exp(sc - m)
                ls.append(jnp.sum(p, axis=-1, keepdims=True))
                ms.append(m)
                outs.append(jnp.dot(p.astype(BF16), v, preferred_element_type=F32))
            acc_ref[bi, pl.ds(q0, ATTN_TQ), :] = jnp.where(head0, outs[0], outs[1])
            m_ref[bi, pl.ds(q0, ATTN_TQ), :] = jnp.where(head0, ms[0], ms[1])
            l_ref[bi, pl.ds(q0, ATTN_TQ), :] = jnp.where(head0, ls[0], ls[1])
            return carry

        lax.fori_loop(0, n_tiles, tile, 0, unroll=8)

    def permute(src_ref, dst_ref, dil):
        sub_len = s // dil
        nat_ref[...] = src_ref[...].astype(F32)
        for r in range(dil):
            dst_ref[r * sub_len:(r + 1) * sub_len, :] = (
                nat_ref[pl.ds(r, sub_len, stride=dil), :].astype(BF16))

    run_branch(0, 1, q_ref, k_ref, v_ref)
    for bi, dil in ((1, 4), (2, 16)):
        permute(q_ref, qp_ref, dil)
        permute(k_ref, kp_ref, dil)
        permute(v_ref, vp_ref, dil)
        run_branch(bi, dil, qp_ref, kp_ref, vp_ref)

    rows = s // 16
    for r in range(16):
        sl = (pl.ds(r, rows, stride=16),
              pl.ds((r % 4) * (s // 4) + r // 4, rows, stride=4),
              pl.ds(r * rows, rows))
        m = [m_ref[bi, sl[bi], :] for bi in range(3)]
        mx = jnp.maximum(jnp.maximum(m[0], m[1]), m[2])
        num = jnp.zeros((rows, PAIR), F32)
        den = jnp.zeros((rows, PAIR), F32)
        for bi in range(3):
            e = jnp.exp(m[bi] - mx)
            num = num + e * acc_ref[bi, sl[bi], :]
            den = den + e * l_ref[bi, sl[bi], :]
        out_ref[pl.ds(r, rows, stride=16), :] = num / den
    o_ref[...] = out_ref[...].astype(o_ref.dtype)


def _attention(aq, ak, av, bias_rows):
    b, s, _ = aq.shape
    spec = pl.BlockSpec((None, s, PAIR), lambda hp, bi: (bi, 0, hp))
    return pl.pallas_call(
        _attention_kernel,
        grid=(N_PAIRS, b),
        in_specs=[spec, spec, spec,
                  pl.BlockSpec((None, 2 * N_BIAS_VARIANTS, 2 * ATTN_W), lambda hp, bi: (hp, 0, 0))],
        out_specs=spec,
        out_shape=jax.ShapeDtypeStruct((b, s, ATTN_WIDTH), BF16),
        scratch_shapes=[
            pltpu.VMEM((2 * N_BIAS_VARIANTS, ATTN_TQ, ATTN_W), F32),
            pltpu.VMEM((s, PAIR), F32),
            pltpu.VMEM((s, PAIR), BF16),
            pltpu.VMEM((s, PAIR), BF16),
            pltpu.VMEM((s, PAIR), BF16),
            pltpu.VMEM((3, s, PAIR), F32),
            pltpu.VMEM((3, s, PAIR), F32),
            pltpu.VMEM((3, s, PAIR), F32),
            pltpu.VMEM((s, PAIR), F32),
        ],
        compiler_params=pltpu.CompilerParams(
            dimension_semantics=("arbitrary", "arbitrary"), vmem_limit_bytes=VMEM_LIMIT),
        name="attention",
    )(aq, ak, av, bias_rows)


def _retention_tables(decay_fwd, decay_bwd):
    c = RET_CHUNK
    lg_f = -jnp.exp(decay_fwd.astype(F32))
    lg_b = -jnp.exp(decay_bwd.astype(F32))
    idx = jnp.arange(c, dtype=F32)
    rel = idx[:, None] - idx[None, :]
    dmat = jnp.where(rel >= 0,
                     jnp.exp(lg_f[:, None, None] * jnp.maximum(rel, 0.0)[None]),
                     jnp.exp(lg_b[:, None, None] * jnp.maximum(-rel, 0.0)[None]))
    dmat = dmat.reshape(N_PAIRS, 2, c, c)

    def lanes(v):
        v = v.reshape(N_PAIRS, 2, -1)
        return jnp.repeat(jnp.transpose(v, (0, 2, 1)), HEAD_DIM, axis=2)

    vec = jnp.stack([
        lanes(jnp.exp(lg_f[:, None] * (idx + 1.0)[None])),
        lanes(jnp.exp(lg_f[:, None] * (c - 1.0 - idx)[None])),
        lanes(jnp.exp(lg_b[:, None] * (c - idx)[None])),
        lanes(jnp.exp(lg_b[:, None] * idx[None])),
    ], axis=1)
    same_head = (jnp.arange(PAIR)[:, None] // HEAD_DIM) == (jnp.arange(PAIR)[None, :] // HEAD_DIM)
    cd = jnp.stack([lanes(jnp.exp(lg_f * c)[:, None]), lanes(jnp.exp(lg_b * c)[:, None])], axis=1)
    cd = jnp.transpose(cd, (0, 1, 3, 2)) * same_head[None, None].astype(F32)
    return dmat, vec, cd


def _retention_kernel(q_ref, k_ref, v_ref, g_ref, dmat_ref, vec_ref, cd_ref, o_ref, y_ref, kv_ref):
    s = q_ref.shape[0]
    c = RET_CHUNK
    nc = s // c
    lane = lax.broadcasted_iota(jnp.int32, (1, PAIR), 1)
    head0 = lane < HEAD_DIM
    same_head = ((lax.broadcasted_iota(jnp.int32, (PAIR, PAIR), 0) // HEAD_DIM)
                 == (lax.broadcasted_iota(jnp.int32, (PAIR, PAIR), 1) // HEAD_DIM))

    for n in range(nc):
        rows = slice(n * c, (n + 1) * c)
        q, k, v = q_ref[rows, :], k_ref[rows, :], v_ref[rows, :]
        outs = []
        for h in range(2):
            qh = jnp.where(head0 if h == 0 else ~head0, q, jnp.zeros_like(q))
            sc = lax.dot_general(qh, k, (((1,), (1,)), ((), ())), preferred_element_type=F32)
            outs.append(jnp.dot((sc * dmat_ref[h]).astype(BF16), v, preferred_element_type=F32))
        y_ref[rows, :] = jnp.where(head0, outs[0], outs[1])
        kf = k.astype(F32)
        kcat = jnp.concatenate([(kf * vec_ref[1]).astype(BF16), (kf * vec_ref[3]).astype(BF16)], axis=1)
        kv = lax.dot_general(kcat, v, (((0,), (0,)), ((), ())), preferred_element_type=F32)
        kv_ref[0, n] = jnp.where(same_head, kv[:PAIR], 0.0)
        kv_ref[1, n] = jnp.where(same_head, kv[PAIR:], 0.0)

    def cross(n, state, qdec):
        rows = slice(n * c, (n + 1) * c)
        qd = (q_ref[rows, :].astype(F32) * qdec).astype(BF16)
        y_ref[rows, :] += jnp.dot(qd, state.astype(BF16), preferred_element_type=F32)

    state = jnp.zeros((PAIR, PAIR), F32)
    for n in range(nc):
        cross(n, state, vec_ref[0])
        state = state * cd_ref[0] + kv_ref[0, n]
    state = jnp.zeros((PAIR, PAIR), F32)
    for n in reversed(range(nc)):
        cross(n, state, vec_ref[2])
        state = state * cd_ref[1] + kv_ref[1, n]

    y = y_ref[...]
    y2 = y * y
    ms0 = jnp.sum(jnp.where(head0, y2, 0.0), axis=-1, keepdims=True)
    ms1 = jnp.sum(jnp.where(head0, 0.0, y2), axis=-1, keepdims=True)
    ms = jnp.where(head0, ms0, ms1) * (1.0 / HEAD_DIM)
    g = g_ref[...].astype(F32)
    o_ref[...] = (y * lax.rsqrt(ms + EPS) * (g * jax.nn.sigmoid(g))).astype(o_ref.dtype)


def _retention(rq, rk, rv, rg, dmat, vec, cd):
    b, s, _ = rq.shape
    c = RET_CHUNK
    spec = pl.BlockSpec((None, s, PAIR), lambda hp, bi: (bi, 0, hp))
    return pl.pallas_call(
        _retention_kernel,
        grid=(N_PAIRS, b),
        in_specs=[spec, spec, spec, spec,
                  pl.BlockSpec((None, 2, c, c), lambda hp, bi: (hp, 0, 0, 0)),
                  pl.BlockSpec((None, 4, c, PAIR), lambda hp, bi: (hp, 0, 0, 0)),
                  pl.BlockSpec((None, 2, PAIR, PAIR), lambda hp, bi: (hp, 0, 0, 0))],
        out_specs=spec,
        out_shape=jax.ShapeDtypeStruct((b, s, RET_WIDTH), BF16),
        scratch_shapes=[pltpu.VMEM((s, PAIR), F32),
                        pltpu.VMEM((2, s // c, PAIR, PAIR), F32)],
        compiler_params=pltpu.CompilerParams(
            dimension_semantics=("arbitrary", "arbitrary"), vmem_limit_bytes=VMEM_LIMIT),
        name="retention",
    )(rq, rk, rv, rg, dmat, vec, cd)


def _outproj_kernel(attn_ref, ret_ref, x_ref, wa_ref, wr_ref, again_ref, fgain_ref,
                    rw_ref, rb_ref, h_ref, xs_ref, dest_ref, cnt_ref):
    a = _rms(attn_ref[...].astype(F32), again_ref[...]).astype(BF16)
    h = (x_ref[...]
         + jnp.dot(a, wa_ref[...], preferred_element_type=F32)
         + jnp.dot(ret_ref[...], wr_ref[...], preferred_element_type=F32))
    h_ref[...] = h
    hn = _rms(h, fgain_ref[...]).astype(BF16)

    logits = jnp.dot(hn, rw_ref[...], preferred_element_type=F32) + rb_ref[...]
    tm = logits.shape[0]
    lane = lax.broadcasted_iota(jnp.int32, logits.shape, 1)
    big = jnp.int32(LANES)

    def first_argmax(vals):
        top = jnp.max(vals, axis=-1, keepdims=True)
        return top, jnp.min(jnp.where(vals == top, lane, big), axis=-1, keepdims=True)

    gl = jnp.where((lane >= N_EXPERTS) & (lane < N_EXPERTS + N_GROUPS), logits, -jnp.inf)
    gmax, gidx = first_argmax(gl)
    p_group = 1.0 / jnp.sum(jnp.exp(gl - gmax), axis=-1, keepdims=True)
    grp = gidx - N_EXPERTS
    lo = grp * EXPERTS_PER_GROUP
    el = jnp.where((lane >= lo) & (lane < lo + EXPERTS_PER_GROUP), logits, -jnp.inf)
    v1, i1 = first_argmax(el)
    v2, i2 = first_argmax(jnp.where(lane == i1, -jnp.inf, el))
    e2 = jnp.exp(v2 - v1)
    p1 = p_group / (1.0 + e2)
    gates = jnp.where(lane == i1, p1, jnp.where(lane == i2, p1 * e2, 0.0))

    onehot = jnp.where(lane == grp, 1.0, 0.0)
    r_i = lax.broadcasted_iota(jnp.int32, (tm, tm), 0)
    c_i = lax.broadcasted_iota(jnp.int32, (tm, tm), 1)
    before = jnp.where(c_i < r_i, 1.0, 0.0).astype(BF16)
    rank = jnp.dot(before, onehot.astype(BF16), preferred_element_type=F32)
    cnt = jnp.sum(onehot, axis=0, keepdims=True)
    c16 = jnp.broadcast_to(jnp.ceil(cnt * (1.0 / CHUNK)) * CHUNK, (8, LANES))
    lane8 = lax.broadcasted_iota(jnp.int32, (8, LANES), 1)
    start = jnp.zeros((8, LANES), F32)
    for sft in range(1, N_GROUPS):
        start = start + jnp.where(lane8 >= sft, pltpu.roll(c16, sft, 1), 0.0)
    dest = jnp.sum(onehot * (start[0:1] + rank), axis=-1, keepdims=True)
    dest_ref[...] = jnp.broadcast_to(dest, (tm, LANES))
    cnt_ref[...] = jnp.where(lane8 < N_GROUPS, c16, pltpu.roll(start, N_GROUPS, 1)).astype(jnp.int32)

    g_hi = gates.astype(BF16)
    g_lo = (gates - g_hi.astype(F32)).astype(BF16)
    payload = jnp.concatenate([hn, g_hi, g_lo], axis=1)
    perm_t = jnp.where(lax.broadcasted_iota(jnp.int32, (tm, SORT_ROWS), 1) == dest.astype(jnp.int32),
                       1.0, 0.0).astype(BF16)
    xs_ref[...] = lax.dot_general(perm_t, payload, (((0,), (0,)), ((), ())),
                                  preferred_element_type=F32).astype(BF16)


def _outproj(attn, ret, x, w_a, w_r, again, fgain, rw, rb):
    t, d = x.shape
    tm = ROUTE_TM
    nt = t // tm
    row = lambda w: pl.BlockSpec((tm, w), lambda i: (i, 0))
    full = lambda r, c: pl.BlockSpec((r, c), lambda i: (0, 0))
    return pl.pallas_call(
        _outproj_kernel,
        grid=(nt,),
        in_specs=[row(ATTN_WIDTH), row(RET_WIDTH), row(d), full(ATTN_WIDTH, d), full(RET_WIDTH, d),
                  full(1, ATTN_WIDTH), full(1, d), full(d, LANES), full(1, LANES)],
        out_specs=[row(d), pl.BlockSpec((SORT_ROWS, XS_W), lambda i: (i, 0)), row(LANES),
                   pl.BlockSpec((None, 8, LANES), lambda i: (i, 0, 0))],
        out_shape=[jax.ShapeDtypeStruct((t, d), F32),
                   jax.ShapeDtypeStruct((nt * SORT_ROWS, XS_W), BF16),
                   jax.ShapeDtypeStruct((t, LANES), F32),
                   jax.ShapeDtypeStruct((nt, 8, LANES), jnp.int32)],
        compiler_params=pltpu.CompilerParams(
            dimension_semantics=("arbitrary",), vmem_limit_bytes=VMEM_LIMIT),
        name="outproj",
    )(attn, ret, x, w_a, w_r, again, fgain, rw, rb)


def _dispatch_tables(cnt, n_moe_tiles):
    nt = cnt.shape[0]
    cpt = MOE_TM // CHUNK

    def schedule_kernel(seg_ref, grp_ref, used_ref, src_ref):
        pos = jnp.int32(0)
        for g in range(N_GROUPS):
            def tile_body(i, p):
                n = seg_ref[i, g] // CHUNK
                first = (i * SORT_ROWS + seg_ref[i, N_GROUPS + g]) // CHUNK

                def chunk_body(j, carry):
                    src_ref[p + j] = first + j
                    return carry
                lax.fori_loop(0, n, chunk_body, 0)
                return p + n
            end = lax.fori_loop(0, nt, tile_body, pos)
            padded = ((end + cpt - 1) // cpt) * cpt

            def pad_body(k, carry):
                src_ref[k] = -1
                return carry
            lax.fori_loop(end, padded, pad_body, 0)

            def mark_body(m, carry):
                grp_ref[m] = g
                used_ref[m] = 1
                return carry
            lax.fori_loop(pos // cpt, padded // cpt, mark_body, 0)
            pos = padded

        def idle_body(m, carry):
            grp_ref[m] = N_GROUPS - 1
            used_ref[m] = 0
            return carry
        lax.fori_loop(pos // cpt, n_moe_tiles, idle_body, 0)
        lax.fori_loop(pos, n_moe_tiles * cpt, pad_body, 0)

    smem = lambda: pl.BlockSpec(memory_space=pltpu.SMEM)
    return pl.pallas_call(
        schedule_kernel,
        in_specs=[smem()],
        out_specs=[smem(), smem(), smem()],
        out_shape=[jax.ShapeDtypeStruct((n_moe_tiles,), jnp.int32),
                   jax.ShapeDtypeStruct((n_moe_tiles,), jnp.int32),
                   jax.ShapeDtypeStruct((n_moe_tiles * cpt,), jnp.int32)],
        name="schedule",
    )(cnt[:, 0, :2 * N_GROUPS])


def _moe_kernel(grp_ref, used_ref, src_ref, xs_hbm, w1_ref, w3_ref, w2_ref, ys_in_hbm, ys_hbm,
                xbuf, obuf, in_sem, out_sem):
    del ys_in_hbm
    m = pl.program_id(0)
    n_tiles = pl.num_programs(0)
    cpt = MOE_TM // CHUNK
    slot = m % 2

    def rows(c):
        return pl.ds(pl.multiple_of(c * CHUNK, CHUNK), CHUNK)

    def gather(tile, sl, wait):
        def body(c, carry):
            src = src_ref[tile * cpt + c]

            @pl.when(src >= 0)
            def _():
                cp = pltpu.make_async_copy(xs_hbm.at[rows(src), :], xbuf.at[sl, rows(c), :], in_sem.at[sl])
                cp.wait() if wait else cp.start()

            if not wait:
                @pl.when(src < 0)
                def _():
                    xbuf[sl, rows(c), :] = jnp.zeros((CHUNK, XS_W), BF16)
            return carry
        lax.fori_loop(0, cpt, body, 0)

    def scatter(tile, sl, wait):
        def body(c, carry):
            src = src_ref[tile * cpt + c]

            @pl.when(src >= 0)
            def _():
                cp = pltpu.make_async_copy(obuf.at[sl, rows(c), :], ys_hbm.at[rows(src), :], out_sem.at[sl])
                cp.wait() if wait else cp.start()
            return carry
        lax.fori_loop(0, cpt, body, 0)

    @pl.when(m == 0)
    def _():
        gather(0, 0, False)

    @pl.when(m + 1 < n_tiles)
    def _():
        gather(m + 1, 1 - slot, False)

    gather(m, slot, True)

    @pl.when(m >= 2)
    def _():
        scatter(m - 2, slot, True)

    @pl.when(used_ref[m] > 0)
    def _():
        x = xbuf[slot, :, :D_MODEL]
        gate = (xbuf[slot, :, D_MODEL:D_MODEL + LANES].astype(F32)
                + xbuf[slot, :, D_MODEL + LANES:].astype(F32))
        lane = lax.broadcasted_iota(jnp.int32, gate.shape, 1)
        base = grp_ref[m] * EXPERTS_PER_GROUP
        acc = jnp.zeros((MOE_TM, D_MODEL), F32)
        for j in range(EXPERTS_PER_GROUP):
            a = jnp.dot(x, w1_ref[j], preferred_element_type=F32)
            b = jnp.dot(x, w3_ref[j], preferred_element_type=F32)
            gj = jnp.sum(jnp.where(lane == base + j, gate, 0.0), axis=-1, keepdims=True)
            hid = (a * jax.nn.sigmoid(a) * b * gj).astype(BF16)
            acc = acc + jnp.dot(hid, w2_ref[j], preferred_element_type=F32)
        obuf[slot] = acc.astype(BF16)

    scatter(m, slot, False)

    @pl.when(m == n_tiles - 1)
    def _():
        scatter(m, slot, True)

        @pl.when(m >= 1)
        def _():
            scatter(m - 1, 1 - slot, True)


def _moe(xs, w1, w3, w2, tile_group, tile_used, src_chunk):
    n_moe_tiles = tile_group.shape[0]
    rows = xs.shape[0]
    d = D_MODEL
    wspec = lambda r, c: pl.BlockSpec((EXPERTS_PER_GROUP, r, c), lambda m, grp, used, src: (grp[m], 0, 0))
    ys0 = jnp.zeros((rows, d), BF16)
    return pl.pallas_call(
        _moe_kernel,
        grid_spec=pltpu.PrefetchScalarGridSpec(
            num_scalar_prefetch=3,
            grid=(n_moe_tiles,),
            in_specs=[pl.BlockSpec(memory_space=pl.ANY),
                      wspec(d, EXPERT_FF), wspec(d, EXPERT_FF), wspec(EXPERT_FF, d),
                      pl.BlockSpec(memory_space=pl.ANY)],
            out_specs=pl.BlockSpec(memory_space=pl.ANY),
            scratch_shapes=[pltpu.VMEM((2, MOE_TM, XS_W), BF16),
                            pltpu.VMEM((2, MOE_TM, d), BF16),
                            pltpu.SemaphoreType.DMA((2,)),
                            pltpu.SemaphoreType.DMA((2,))]),
        out_shape=jax.ShapeDtypeStruct((rows, d), BF16),
        input_output_aliases={7: 0},
        compiler_params=pltpu.CompilerParams(
            dimension_semantics=("arbitrary",), vmem_limit_bytes=VMEM_LIMIT),
        name="moe",
    )(tile_group, tile_used, src_chunk, xs, w1, w3, w2, ys0)


def _combine_kernel(ys_ref, h_ref, dest_ref, gain_ref, o_ref):
    tm = h_ref.shape[0]
    dest = dest_ref[:, 0:1].astype(jnp.int32)
    perm_t = jnp.where(lax.broadcasted_iota(jnp.int32, (tm, SORT_ROWS), 1) == dest, 1.0, 0.0).astype(BF16)
    moe = jnp.dot(perm_t, ys_ref[...], preferred_element_type=F32)
    o_ref[...] = _rms(h_ref[...] + moe, gain_ref[...])


def _combine(ys, h, dest, gain):
    t, d = h.shape
    tm = ROUTE_TM
    return pl.pallas_call(
        _combine_kernel,
        grid=(t // tm,),
        in_specs=[pl.BlockSpec((SORT_ROWS, d), lambda i: (i, 0)),
                  pl.BlockSpec((tm, d), lambda i: (i, 0)),
                  pl.BlockSpec((tm, LANES), lambda i: (i, 0)),
                  pl.BlockSpec((1, d), lambda i: (0, 0))],
        out_specs=pl.BlockSpec((tm, d), lambda i: (i, 0)),
        out_shape=jax.ShapeDtypeStruct((t, d), F32),
        compiler_params=pltpu.CompilerParams(
            dimension_semantics=("arbitrary",), vmem_limit_bytes=VMEM_LIMIT),
        name="combine",
    )(ys, h, dest, gain)


def _rotary_tables(s):
    half = HEAD_DIM // 2
    inv = ROPE_BASE ** (-jnp.arange(half, dtype=F32) / half)
    ang = jnp.arange(s, dtype=F32)[:, None] * inv[None, :]
    cos, sin = jnp.cos(ang), jnp.sin(ang)
    cos_t = jnp.tile(jnp.concatenate([cos, cos], axis=-1), (1, LANES // HEAD_DIM))
    sin_t = jnp.tile(jnp.concatenate([-sin, sin], axis=-1), (1, LANES // HEAD_DIM))
    return cos_t, sin_t


def kernel(x, w_in, w_out, norm_mix, norm_ffn, norm_final, attn_out_gain, rel_bias, ret_decay_fwd, ret_decay_bwd, router_group_w, router_group_b, router_expert_w, router_expert_b, expert_w1, expert_w3, expert_w2):
    b, s, d = x.shape
    depth = w_in.shape[0]
    cos_t, sin_t = _rotary_tables(s)
    bias_rows = _attn_bias_rows(rel_bias)
    h = x
    for layer in range(depth):
        aq, ak, av, rq, rk, rv, rg = _inproj(
            h, norm_mix[layer][None], w_in[layer].astype(BF16), cos_t, sin_t)
        attn = _attention(aq, ak, av, bias_rows)
        ret = _retention(rq, rk, rv, rg, *_retention_tables(ret_decay_fwd[layer], ret_decay_bwd[layer]))

        w_o = w_out[layer].astype(BF16)
        rw = jnp.concatenate(
            [jnp.transpose(router_expert_w[layer], (1, 0, 2)).reshape(d, N_EXPERTS),
             router_group_w[layer],
             jnp.zeros((d, LANES - N_EXPERTS - N_GROUPS), F32)], axis=1).astype(BF16)
        rb = jnp.concatenate(
            [router_expert_b[layer].reshape(N_EXPERTS), router_group_b[layer],
             jnp.zeros((LANES - N_EXPERTS - N_GROUPS,), F32)])[None].astype(F32)
        h1, xs, dest, cnt = _outproj(
            attn.reshape(b * s, ATTN_WIDTH), ret.reshape(b * s, RET_WIDTH), h.reshape(b * s, d),
            w_o[:ATTN_WIDTH], w_o[ATTN_WIDTH:], attn_out_gain[layer][None], norm_ffn[layer][None],
            rw, rb)
        n_route_tiles = (b * s) // ROUTE_TM
        n_moe_tiles = (b * s + n_route_tiles * N_GROUPS * (CHUNK - 1)) // MOE_TM + N_GROUPS
        tile_group, tile_used, src_chunk = _dispatch_tables(cnt, n_moe_tiles)
        ys = _moe(xs, expert_w1[layer].astype(BF16), expert_w3[layer].astype(BF16),
                  expert_w2[layer].astype(BF16), tile_group, tile_used, src_chunk)
        assert depth == 1, "the combine kernel fuses the final norm, so it must run on the last layer"
        h = _combine(ys, h1, dest, norm_final[None]).reshape(b, s, d)
    return h
```

```python
import functools
import math

import jax
import jax.numpy as jnp
from jax import lax
from jax.experimental import pallas as pl
from jax.experimental.pallas import tpu as pltpu

F32 = jnp.float32
BF16 = jnp.bfloat16

D_MODEL = 1024
HEAD_DIM = 64
ATTN_WIDTH = 512
RET_WIDTH = 512
N_HEADS = 8
PAIR = 2 * HEAD_DIM
N_PAIRS = N_HEADS // 2
ATTN_DILATIONS = (1, 4, 16)
ATTN_RADIUS = 64
N_BUCKETS = 32
REL_MAX_DIST = 1024
ROPE_BASE = 10000.0
N_GROUPS = 4
EXPERTS_PER_GROUP = 4
N_EXPERTS = 16
EXPERT_FF = 512
EPS = 1e-6
NEG_INF = -1e30

LANES = 128
ATTN_TQ = 128
ATTN_W = 256
N_BIAS_VARIANTS = 8
RET_CHUNK = 256
ROUTE_TM = 512
SORT_ROWS = 640
CHUNK = 16
MOE_TM = 512
XS_W = D_MODEL + 2 * LANES
VMEM_LIMIT = 48 * 1024 * 1024


def _rms(x, gain):
    return x * lax.rsqrt(jnp.mean(x * x, axis=-1, keepdims=True) + EPS) * gain


def _inproj_kernel(x_ref, gain_ref, w_ref, cos_ref, sin_ref,
                   aq_ref, ak_ref, av_ref, rq_ref, rk_ref, rv_ref, rg_ref,
                   aq4_ref, ak4_ref, av4_ref, aq16_ref, ak16_ref, av16_ref, stage_ref, stage4_ref):
    tm = x_ref.shape[0]
    xn = _rms(x_ref[...], gain_ref[...]).astype(BF16)

    def emit(i, t, nat_ref, d4_ref, d16_ref):
        nat_ref[...] = t.astype(BF16)
        for hp in range(N_PAIRS):
            stage_ref[i, hp] = t[:, hp * PAIR:(hp + 1) * PAIR]
            for r4 in range(4):
                g4 = stage_ref[i, hp, pl.ds(r4, tm // 4, stride=4), :]
                d4_ref[hp, r4] = g4.astype(BF16)
                stage4_ref[i, hp, r4] = g4
                for j in range(4):
                    d16_ref[hp, r4 + 4 * j] = (
                        stage4_ref[i, hp, r4, pl.ds(j, tm // 16, stride=4), :].astype(BF16))

    def seg(i):
        return jnp.dot(xn, w_ref[:, i * 512:(i + 1) * 512], preferred_element_type=F32)

    def rotary(t):
        cos, sin = cos_ref[...], sin_ref[...]
        first_half = (lax.broadcasted_iota(jnp.int32, (1, LANES), 1) % HEAD_DIM) < HEAD_DIM // 2
        outs = []
        for j in range(t.shape[1] // LANES):
            tj = t[:, j * LANES:(j + 1) * LANES]
            partner = jnp.where(first_half, pltpu.roll(tj, LANES - 32, 1), pltpu.roll(tj, 32, 1))
            outs.append(tj * cos + partner * sin)
        return jnp.concatenate(outs, axis=1)

    emit(0, seg(0) * (HEAD_DIM ** -0.5), aq_ref, aq4_ref, aq16_ref)
    emit(1, seg(1), ak_ref, ak4_ref, ak16_ref)
    emit(2, seg(2), av_ref, av4_ref, av16_ref)
    rq_ref[...] = rotary(seg(3)).astype(BF16)
    rk_ref[...] = (rotary(seg(4)) * (HEAD_DIM ** -0.5)).astype(BF16)
    rv_ref[...] = seg(5).astype(BF16)
    rg_ref[...] = seg(6).astype(BF16)


def _inproj(x, gain, w_in, cos_t, sin_t, tm=512):
    b, s, d = x.shape
    n = w_in.shape[1]
    out = jax.ShapeDtypeStruct((b, s, 512), BF16)
    ospec = pl.BlockSpec((None, tm, 512), lambda si, bi: (bi, si, 0))

    def grouped(dil):
        shape = jax.ShapeDtypeStruct((b, N_PAIRS, dil, s // dil, PAIR), BF16)
        spec = pl.BlockSpec((None, N_PAIRS, dil, tm // dil, PAIR), lambda si, bi: (bi, 0, 0, si, 0))
        return [shape] * 3, [spec] * 3

    shapes4, specs4 = grouped(4)
    shapes16, specs16 = grouped(16)
    outs = pl.pallas_call(
        _inproj_kernel,
        grid=(s // tm, b),
        in_specs=[
            pl.BlockSpec((None, tm, d), lambda si, bi: (bi, si, 0)),
            pl.BlockSpec((1, d), lambda si, bi: (0, 0)),
            pl.BlockSpec((d, n), lambda si, bi: (0, 0)),
            pl.BlockSpec((tm, LANES), lambda si, bi: (si, 0)),
            pl.BlockSpec((tm, LANES), lambda si, bi: (si, 0)),
        ],
        out_specs=[ospec] * 7 + specs4 + specs16,
        out_shape=[out] * 7 + shapes4 + shapes16,
        scratch_shapes=[pltpu.VMEM((3, N_PAIRS, tm, PAIR), F32),
                        pltpu.VMEM((3, N_PAIRS, 4, tm // 4, PAIR), F32)],
        compiler_params=pltpu.CompilerParams(
            dimension_semantics=("arbitrary", "arbitrary"), vmem_limit_bytes=VMEM_LIMIT),
        name="inproj",
    )(x, gain, w_in, cos_t, sin_t)
    return outs[:7], [o.reshape(b, N_PAIRS, s, PAIR) for o in outs[7:]]


def _t5_bucket(rel):
    half = N_BUCKETS // 2
    max_exact = half // 2
    offset = jnp.where(rel > 0, half, 0)
    n = jnp.abs(rel)
    nf = jnp.maximum(n, 1).astype(F32)
    large = max_exact + (jnp.log(nf / max_exact) / math.log(REL_MAX_DIST / max_exact)
                         * (half - max_exact)).astype(jnp.int32)
    large = jnp.minimum(large, half - 1)
    return offset + jnp.where(n < max_exact, n, large)


def _attn_bias_rows(rel_bias):
    period = 2 * ATTN_W
    band = 2 * ATTN_RADIUS + 1
    rel = jnp.arange(-ATTN_RADIUS, ATTN_RADIUS + 1)
    rows = []
    for dil, offs in ((1, (0, 64, 128)), (4, (0, 64, 128)), (16, (0, 128))):
        vals = rel_bias[_t5_bucket(rel * dil)].astype(F32).T
        for off in offs:
            lo = off - ATTN_RADIUS
            pad = jnp.full((N_HEADS, period - band), NEG_INF, F32)
            if lo >= 0:
                row = jnp.concatenate([pad[:, :lo], vals, pad[:, lo:]], axis=1)
            else:
                row = jnp.concatenate([vals[:, -lo:], pad, vals[:, :-lo]], axis=1)
            rows.append(row)
    v = jnp.stack(rows, axis=1)
    return v.reshape(N_PAIRS, 2 * N_BIAS_VARIANTS, period)


def _attention_kernel(q_ref, k_ref, v_ref, q4_ref, k4_ref, v4_ref, q16_ref, k16_ref, v16_ref,
                      rows_ref, o_ref, bias_ref, acc_ref, m_ref, l_ref, out_ref):
    s = q_ref.shape[0]
    n_tiles = s // ATTN_TQ
    lane = lax.broadcasted_iota(jnp.int32, (1, PAIR), 1)
    head0 = lane < HEAD_DIM

    @pl.when(pl.program_id(1) == 0)
    def _():
        col = lax.broadcasted_iota(jnp.int32, (ATTN_TQ, ATTN_W), 1)
        for idx in range(2 * N_BIAS_VARIANTS):
            gen = jnp.broadcast_to(rows_ref[idx:idx + 1, :], (ATTN_TQ, 2 * ATTN_W))
            tab = pltpu.roll(gen, 0, 1, stride=1, stride_axis=0)[:, :ATTN_W]
            var = idx % N_BIAS_VARIANTS
            if var >= 6:
                tab = jnp.where((col // ATTN_TQ) == var - 6, tab, NEG_INF)
            head = idx // N_BIAS_VARIANTS
            bias_ref[var, head * ATTN_TQ:(head + 1) * ATTN_TQ, :] = tab

    def run_branch(bi, dil, qs_ref, ks_ref, vs_ref):
        sub_len = s // dil
        tiles_per_sub = sub_len // ATTN_TQ

        def tile(t, carry):
            q0 = pl.multiple_of(t * ATTN_TQ, ATTN_TQ)
            if tiles_per_sub == 1:
                ws = pl.multiple_of((t // 2) * ATTN_W, ATTN_W)
                var = 6 + t % 2
            else:
                pos = t % tiles_per_sub
                sub_lo = (t // tiles_per_sub) * sub_len
                ws = jnp.clip(q0 - 64, sub_lo, sub_lo + sub_len - ATTN_W)
                ws = pl.multiple_of(ws, 64)
                var = jnp.where(pos == 0, 0, jnp.where(pos == tiles_per_sub - 1, 2, 1)) + 3 * bi
            q = qs_ref[pl.ds(q0, ATTN_TQ), :]
            k = ks_ref[pl.ds(ws, ATTN_W), :]
            v = vs_ref[pl.ds(ws, ATTN_W), :]
            q2 = jnp.concatenate([jnp.where(head0, q, jnp.zeros_like(q)),
                                  jnp.where(head0, jnp.zeros_like(q), q)], axis=0)
            sc = lax.dot_general(q2, k, (((1,), (1,)), ((), ())), preferred_element_type=F32)
            sc = sc + bias_ref[var]
            m = jnp.max(sc, axis=-1, keepdims=True)
            p = jnp.exp(sc - m)
            l = jnp.sum(p, axis=-1, keepdims=True)
            o = jnp.dot(p.astype(BF16), v, preferred_element_type=F32)
            outs, ms, ls = (o[:ATTN_TQ], o[ATTN_TQ:]), (m[:ATTN_TQ], m[ATTN_TQ:]), (l[:ATTN_TQ], l[ATTN_TQ:])
            if dil == 1:
                dst = pl.ds(q0, ATTN_TQ)
            elif dil == 4:
                dst = pl.ds((t % 4) * (4 * ATTN_TQ) + t // 4, ATTN_TQ, stride=4)
            else:
                dst = pl.ds((t % 4) * (s // 4) + t // 4, ATTN_TQ, stride=4)
            acc_ref[bi, dst, :] = jnp.where(head0, outs[0], outs[1])
            m_ref[bi, dst, :] = jnp.where(head0, ms[0], ms[1])
            l_ref[bi, dst, :] = jnp.where(head0, ls[0], ls[1])
            return carry

        lax.fori_loop(0, n_tiles, tile, 0, unroll=8)

    run_branch(0, 1, q_ref, k_ref, v_ref)
    run_branch(1, 4, q4_ref, k4_ref, v4_ref)
    run_branch(2, 16, q16_ref, k16_ref, v16_ref)

    rows = ATTN_TQ
    for r4 in range(4):
        for blk in range(s // (4 * rows)):
            nat = pl.ds(r4 + 4 * rows * blk, rows, stride=4)
            sl = (nat, nat, pl.ds(r4 * (s // 4) + rows * blk, rows))
            m = [m_ref[bi, sl[bi], :] for bi in range(3)]
            mx = jnp.maximum(jnp.maximum(m[0], m[1]), m[2])
            num = jnp.zeros((rows, PAIR), F32)
            den = jnp.zeros((rows, PAIR), F32)
            for bi in range(3):
                e = jnp.exp(m[bi] - mx)
                num = num + e * acc_ref[bi, sl[bi], :]
                den = den + e * l_ref[bi, sl[bi], :]
            out_ref[nat, :] = num / den
    o_ref[...] = out_ref[...].astype(o_ref.dtype)


def _attention(qkv, qkv_grouped, bias_rows):
    b, s, _ = qkv[0].shape
    spec = pl.BlockSpec((None, s, PAIR), lambda hp, bi: (bi, 0, hp))
    gspec = pl.BlockSpec((None, None, s, PAIR), lambda hp, bi: (bi, hp, 0, 0))
    return pl.pallas_call(
        _attention_kernel,
        grid=(N_PAIRS, b),
        in_specs=[spec] * 3 + [gspec] * 6 + [
            pl.BlockSpec((None, 2 * N_BIAS_VARIANTS, 2 * ATTN_W), lambda hp, bi: (hp, 0, 0))],
        out_specs=spec,
        out_shape=jax.ShapeDtypeStruct((b, s, ATTN_WIDTH), BF16),
        scratch_shapes=[
            pltpu.VMEM((N_BIAS_VARIANTS, 2 * ATTN_TQ, ATTN_W), F32),
            pltpu.VMEM((3, s, PAIR), F32),
            pltpu.VMEM((3, s, PAIR), F32),
            pltpu.VMEM((3, s, PAIR), F32),
            pltpu.VMEM((s, PAIR), F32),
        ],
        compiler_params=pltpu.CompilerParams(
            dimension_semantics=("arbitrary", "arbitrary"), vmem_limit_bytes=VMEM_LIMIT),
        name="attention",
    )(*qkv, *qkv_grouped, bias_rows)


def _retention_tables(decay_fwd, decay_bwd):
    c = RET_CHUNK
    lg_f = -jnp.exp(decay_fwd.astype(F32))
    lg_b = -jnp.exp(decay_bwd.astype(F32))
    idx = jnp.arange(c, dtype=F32)
    rel = idx[:, None] - idx[None, :]
    dmat = jnp.where(rel >= 0,
                     jnp.exp(lg_f[:, None, None] * jnp.maximum(rel, 0.0)[None]),
                     jnp.exp(lg_b[:, None, None] * jnp.maximum(-rel, 0.0)[None]))
    dmat = dmat.reshape(N_PAIRS, 2, c, c)

    def lanes(v):
        v = v.reshape(N_PAIRS, 2, -1)
        return jnp.repeat(jnp.transpose(v, (0, 2, 1)), HEAD_DIM, axis=2)

    vec = jnp.stack([
        lanes(jnp.exp(lg_f[:, None] * (idx + 1.0)[None])),
        lanes(jnp.exp(lg_f[:, None] * (c - 1.0 - idx)[None])),
        lanes(jnp.exp(lg_b[:, None] * (c - idx)[None])),
        lanes(jnp.exp(lg_b[:, None] * idx[None])),
    ], axis=1)
    same_head = (jnp.arange(PAIR)[:, None] // HEAD_DIM) == (jnp.arange(PAIR)[None, :] // HEAD_DIM)
    cd = jnp.stack([lanes(jnp.exp(lg_f * c)[:, None]), lanes(jnp.exp(lg_b * c)[:, None])], axis=1)
    cd = jnp.transpose(cd, (0, 1, 3, 2)) * same_head[None, None].astype(F32)
    return dmat, vec, cd


def _retention_kernel(q_ref, k_ref, v_ref, g_ref, dmat_ref, vec_ref, cd_ref, o_ref, y_ref, kv_ref):
    s = q_ref.shape[0]
    c = RET_CHUNK
    nc = s // c
    lane = lax.broadcasted_iota(jnp.int32, (1, PAIR), 1)
    head0 = lane < HEAD_DIM
    same_head = ((lax.broadcasted_iota(jnp.int32, (PAIR, PAIR), 0) // HEAD_DIM)
                 == (lax.broadcasted_iota(jnp.int32, (PAIR, PAIR), 1) // HEAD_DIM))

    for n in range(nc):
        rows = slice(n * c, (n + 1) * c)
        q, k, v = q_ref[rows, :], k_ref[rows, :], v_ref[rows, :]
        outs = []
        for h in range(2):
            qh = jnp.where(head0 if h == 0 else ~head0, q, jnp.zeros_like(q))
            sc = lax.dot_general(qh, k, (((1,), (1,)), ((), ())), preferred_element_type=F32)
            outs.append(jnp.dot((sc * dmat_ref[h]).astype(BF16), v, preferred_element_type=F32))
        y_ref[rows, :] = jnp.where(head0, outs[0], outs[1])
        kf = k.astype(F32)
        kcat = jnp.concatenate([(kf * vec_ref[1]).astype(BF16), (kf * vec_ref[3]).astype(BF16)], axis=1)
        kv = lax.dot_general(kcat, v, (((0,), (0,)), ((), ())), preferred_element_type=F32)
        kv_ref[0, n] = jnp.where(same_head, kv[:PAIR], 0.0)
        kv_ref[1, n] = jnp.where(same_head, kv[PAIR:], 0.0)

    def cross(n, state, qdec):
        rows = slice(n * c, (n + 1) * c)
        qd = (q_ref[rows, :].astype(F32) * qdec).astype(BF16)
        y_ref[rows, :] += jnp.dot(qd, state.astype(BF16), preferred_element_type=F32)

    state = jnp.zeros((PAIR, PAIR), F32)
    for n in range(nc):
        cross(n, state, vec_ref[0])
        state = state * cd_ref[0] + kv_ref[0, n]
    state = jnp.zeros((PAIR, PAIR), F32)
    for n in reversed(range(nc)):
        cross(n, state, vec_ref[2])
        state = state * cd_ref[1] + kv_ref[1, n]

    y = y_ref[...]
    y2 = y * y
    ms0 = jnp.sum(jnp.where(head0, y2, 0.0), axis=-1, keepdims=True)
    ms1 = jnp.sum(jnp.where(head0, 0.0, y2), axis=-1, keepdims=True)
    ms = jnp.where(head0, ms0, ms1) * (1.0 / HEAD_DIM)
    g = g_ref[...].astype(F32)
    o_ref[...] = (y * lax.rsqrt(ms + EPS) * (g * jax.nn.sigmoid(g))).astype(o_ref.dtype)


def _retention(rq, rk, rv, rg, dmat, vec, cd):
    b, s, _ = rq.shape
    c = RET_CHUNK
    spec = pl.BlockSpec((None, s, PAIR), lambda hp, bi: (bi, 0, hp))
    return pl.pallas_call(
        _retention_kernel,
        grid=(N_PAIRS, b),
        in_specs=[spec, spec, spec, spec,
                  pl.BlockSpec((None, 2, c, c), lambda hp, bi: (hp, 0, 0, 0)),
                  pl.BlockSpec((None, 4, c, PAIR), lambda hp, bi: (hp, 0, 0, 0)),
                  pl.BlockSpec((None, 2, PAIR, PAIR), lambda hp, bi: (hp, 0, 0, 0))],
        out_specs=spec,
        out_shape=jax.ShapeDtypeStruct((b, s, RET_WIDTH), BF16),
        scratch_shapes=[pltpu.VMEM((s, PAIR), F32),
                        pltpu.VMEM((2, s // c, PAIR, PAIR), F32)],
        compiler_params=pltpu.CompilerParams(
            dimension_semantics=("arbitrary", "arbitrary"), vmem_limit_bytes=VMEM_LIMIT),
        name="retention",
    )(rq, rk, rv, rg, dmat, vec, cd)


def _outproj_kernel(attn_ref, ret_ref, x_ref, wa_ref, wr_ref, again_ref, fgain_ref,
                    rw_ref, rb_ref, h_ref, xs_ref, dest_ref, cnt_ref):
    a = _rms(attn_ref[...].astype(F32), again_ref[...]).astype(BF16)
    h = (x_ref[...]
         + jnp.dot(a, wa_ref[...], preferred_element_type=F32)
         + jnp.dot(ret_ref[...], wr_ref[...], preferred_element_type=F32))
    h_ref[...] = h
    hn = _rms(h, fgain_ref[...]).astype(BF16)

    logits = jnp.dot(hn, rw_ref[...], preferred_element_type=F32) + rb_ref[...]
    tm = logits.shape[0]
    lane = lax.broadcasted_iota(jnp.int32, logits.shape, 1)
    big = jnp.int32(LANES)

    def first_argmax(vals):
        top = jnp.max(vals, axis=-1, keepdims=True)
        return top, jnp.min(jnp.where(vals == top, lane, big), axis=-1, keepdims=True)

    gl = jnp.where((lane >= N_EXPERTS) & (lane < N_EXPERTS + N_GROUPS), logits, -jnp.inf)
    gmax, gidx = first_argmax(gl)
    p_group = 1.0 / jnp.sum(jnp.exp(gl - gmax), axis=-1, keepdims=True)
    grp = gidx - N_EXPERTS
    lo = grp * EXPERTS_PER_GROUP
    el = jnp.where((lane >= lo) & (lane < lo + EXPERTS_PER_GROUP), logits, -jnp.inf)
    v1, i1 = first_argmax(el)
    v2, i2 = first_argmax(jnp.where(lane == i1, -jnp.inf, el))
    e2 = jnp.exp(v2 - v1)
    p1 = p_group / (1.0 + e2)
    gates = jnp.where(lane == i1, p1, jnp.where(lane == i2, p1 * e2, 0.0))

    onehot = jnp.where(lane == grp, 1.0, 0.0)
    r_i = lax.broadcasted_iota(jnp.int32, (tm, tm), 0)
    c_i = lax.broadcasted_iota(jnp.int32, (tm, tm), 1)
    before = jnp.where(c_i < r_i, 1.0, 0.0).astype(BF16)
    rank = jnp.dot(before, onehot.astype(BF16), preferred_element_type=F32)
    cnt = jnp.sum(onehot, axis=0, keepdims=True)
    c16 = jnp.broadcast_to(jnp.ceil(cnt * (1.0 / CHUNK)) * CHUNK, (8, LANES))
    lane8 = lax.broadcasted_iota(jnp.int32, (8, LANES), 1)
    start = jnp.zeros((8, LANES), F32)
    for sft in range(1, N_GROUPS):
        start = start + jnp.where(lane8 >= sft, pltpu.roll(c16, sft, 1), 0.0)
    dest = jnp.sum(onehot * (start[0:1] + rank), axis=-1, keepdims=True)
    dest_ref[...] = jnp.broadcast_to(dest, (tm, LANES))
    cnt_ref[...] = jnp.where(lane8 < N_GROUPS, c16, pltpu.roll(start, N_GROUPS, 1)).astype(jnp.int32)

    g_hi = gates.astype(BF16)
    g_lo = (gates - g_hi.astype(F32)).astype(BF16)
    payload = jnp.concatenate([hn, g_hi, g_lo], axis=1)
    perm_t = jnp.where(lax.broadcasted_iota(jnp.int32, (tm, SORT_ROWS), 1) == dest.astype(jnp.int32),
                       1.0, 0.0).astype(BF16)
    xs_ref[...] = lax.dot_general(perm_t, payload, (((0,), (0,)), ((), ())),
                                  preferred_element_type=F32).astype(BF16)


def _outproj(attn, ret, x, w_a, w_r, again, fgain, rw, rb):
    t, d = x.shape
    tm = ROUTE_TM
    nt = t // tm
    row = lambda w: pl.BlockSpec((tm, w), lambda i: (i, 0))
    full = lambda r, c: pl.BlockSpec((r, c), lambda i: (0, 0))
    return pl.pallas_call(
        _outproj_kernel,
        grid=(nt,),
        in_specs=[row(ATTN_WIDTH), row(RET_WIDTH), row(d), full(ATTN_WIDTH, d), full(RET_WIDTH, d),
                  full(1, ATTN_WIDTH), full(1, d), full(d, LANES), full(1, LANES)],
        out_specs=[row(d), pl.BlockSpec((SORT_ROWS, XS_W), lambda i: (i, 0)), row(LANES),
                   pl.BlockSpec((None, 8, LANES), lambda i: (i, 0, 0))],
        out_shape=[jax.ShapeDtypeStruct((t, d), F32),
                   jax.ShapeDtypeStruct((nt * SORT_ROWS, XS_W), BF16),
                   jax.ShapeDtypeStruct((t, LANES), F32),
                   jax.ShapeDtypeStruct((nt, 8, LANES), jnp.int32)],
        compiler_params=pltpu.CompilerParams(
            dimension_semantics=("arbitrary",), vmem_limit_bytes=VMEM_LIMIT),
        name="outproj",
    )(attn, ret, x, w_a, w_r, again, fgain, rw, rb)


def _dispatch_tables(cnt, n_moe_tiles):
    nt = cnt.shape[0]
    cpt = MOE_TM // CHUNK

    def schedule_kernel(seg_ref, grp_ref, used_ref, src_ref):
        pos = jnp.int32(0)
        for g in range(N_GROUPS):
            def tile_body(i, p):
                n = seg_ref[i, g] // CHUNK
                first = (i * SORT_ROWS + seg_ref[i, N_GROUPS + g]) // CHUNK

                def chunk_body(j, carry):
                    src_ref[p + j] = first + j
                    return carry
                lax.fori_loop(0, n, chunk_body, 0)
                return p + n
            end = lax.fori_loop(0, nt, tile_body, pos)
            padded = ((end + cpt - 1) // cpt) * cpt

            def pad_body(k, carry):
                src_ref[k] = -1
                return carry
            lax.fori_loop(end, padded, pad_body, 0)

            def mark_body(m, carry):
                grp_ref[m] = g
                used_ref[m] = 1
                return carry
            lax.fori_loop(pos // cpt, padded // cpt, mark_body, 0)
            pos = padded

        def idle_body(m, carry):
            grp_ref[m] = N_GROUPS - 1
            used_ref[m] = 0
            return carry
        lax.fori_loop(pos // cpt, n_moe_tiles, idle_body, 0)
        lax.fori_loop(pos, n_moe_tiles * cpt, pad_body, 0)

    smem = lambda: pl.BlockSpec(memory_space=pltpu.SMEM)
    return pl.pallas_call(
        schedule_kernel,
        in_specs=[smem()],
        out_specs=[smem(), smem(), smem()],
        out_shape=[jax.ShapeDtypeStruct((n_moe_tiles,), jnp.int32),
                   jax.ShapeDtypeStruct((n_moe_tiles,), jnp.int32),
                   jax.ShapeDtypeStruct((n_moe_tiles * cpt,), jnp.int32)],
        name="schedule",
    )(cnt[:, 0, :2 * N_GROUPS])


def _moe_kernel(grp_ref, used_ref, src_ref, xs_hbm, w1_ref, w3_ref, w2_ref, ys_in_hbm, ys_hbm,
                xbuf, obuf, in_sem, out_sem):
    del ys_in_hbm
    m = pl.program_id(0)
    n_tiles = pl.num_programs(0)
    cpt = MOE_TM // CHUNK
    slot = m % 2

    def rows(c):
        return pl.ds(pl.multiple_of(c * CHUNK, CHUNK), CHUNK)

    def gather(tile, sl, wait):
        def body(c, carry):
            src = src_ref[tile * cpt + c]

            @pl.when(src >= 0)
            def _():
                cp = pltpu.make_async_copy(xs_hbm.at[rows(src), :], xbuf.at[sl, rows(c), :], in_sem.at[sl])
                cp.wait() if wait else cp.start()

            if not wait:
                @pl.when(src < 0)
                def _():
                    xbuf[sl, rows(c), :] = jnp.zeros((CHUNK, XS_W), BF16)
            return carry
        lax.fori_loop(0, cpt, body, 0)

    def scatter(tile, sl, wait):
        def body(c, carry):
            src = src_ref[tile * cpt + c]

            @pl.when(src >= 0)
            def _():
                cp = pltpu.make_async_copy(obuf.at[sl, rows(c), :], ys_hbm.at[rows(src), :], out_sem.at[sl])
                cp.wait() if wait else cp.start()
            return carry
        lax.fori_loop(0, cpt, body, 0)

    @pl.when(m == 0)
    def _():
        gather(0, 0, False)

    @pl.when(m + 1 < n_tiles)
    def _():
        gather(m + 1, 1 - slot, False)

    gather(m, slot, True)

    @pl.when(m >= 2)
    def _():
        scatter(m - 2, slot, True)

    @pl.when(used_ref[m] > 0)
    def _():
        x = xbuf[slot, :, :D_MODEL]
        gate = (xbuf[slot, :, D_MODEL:D_MODEL + LANES].astype(F32)
                + xbuf[slot, :, D_MODEL + LANES:].astype(F32))
        lane = lax.broadcasted_iota(jnp.int32, gate.shape, 1)
        base = grp_ref[m] * EXPERTS_PER_GROUP
        acc = jnp.zeros((MOE_TM, D_MODEL), F32)
        for j in range(EXPERTS_PER_GROUP):
            a = jnp.dot(x, w1_ref[j], preferred_element_type=F32)
            b = jnp.dot(x, w3_ref[j], preferred_element_type=F32)
            gj = jnp.sum(jnp.where(lane == base + j, gate, 0.0), axis=-1, keepdims=True)
            hid = (a * jax.nn.sigmoid(a) * b * gj).astype(BF16)
            acc = acc + jnp.dot(hid, w2_ref[j], preferred_element_type=F32)
        obuf[slot] = acc.astype(BF16)

    scatter(m, slot, False)

    @pl.when(m == n_tiles - 1)
    def _():
        scatter(m, slot, True)

        @pl.when(m >= 1)
        def _():
            scatter(m - 1, 1 - slot, True)


def _moe(xs, w1, w3, w2, tile_group, tile_used, src_chunk):
    n_moe_tiles = tile_group.shape[0]
    rows = xs.shape[0]
    d = D_MODEL
    wspec = lambda r, c: pl.BlockSpec((EXPERTS_PER_GROUP, r, c), lambda m, grp, used, src: (grp[m], 0, 0))
    ys0 = jnp.zeros((rows, d), BF16)
    return pl.pallas_call(
        _moe_kernel,
        grid_spec=pltpu.PrefetchScalarGridSpec(
            num_scalar_prefetch=3,
            grid=(n_moe_tiles,),
            in_specs=[pl.BlockSpec(memory_space=pl.ANY),
                      wspec(d, EXPERT_FF), wspec(d, EXPERT_FF), wspec(EXPERT_FF, d),
                      pl.BlockSpec(memory_space=pl.ANY)],
            out_specs=pl.BlockSpec(memory_space=pl.ANY),
            scratch_shapes=[pltpu.VMEM((2, MOE_TM, XS_W), BF16),
                            pltpu.VMEM((2, MOE_TM, d), BF16),
                            pltpu.SemaphoreType.DMA((2,)),
                            pltpu.SemaphoreType.DMA((2,))]),
        out_shape=jax.ShapeDtypeStruct((rows, d), BF16),
        input_output_aliases={7: 0},
        compiler_params=pltpu.CompilerParams(
            dimension_semantics=("arbitrary",), vmem_limit_bytes=VMEM_LIMIT),
        name="moe",
    )(tile_group, tile_used, src_chunk, xs, w1, w3, w2, ys0)


def _combine_kernel(ys_ref, h_ref, dest_ref, gain_ref, o_ref):
    tm = h_ref.shape[0]
    dest = dest_ref[:, 0:1].astype(jnp.int32)
    perm_t = jnp.where(lax.broadcasted_iota(jnp.int32, (tm, SORT_ROWS), 1) == dest, 1.0, 0.0).astype(BF16)
    moe = jnp.dot(perm_t, ys_ref[...], preferred_element_type=F32)
    o_ref[...] = _rms(h_ref[...] + moe, gain_ref[...])


def _combine(ys, h, dest, gain):
    t, d = h.shape
    tm = ROUTE_TM
    return pl.pallas_call(
        _combine_kernel,
        grid=(t // tm,),
        in_specs=[pl.BlockSpec((SORT_ROWS, d), lambda i: (i, 0)),
                  pl.BlockSpec((tm, d), lambda i: (i, 0)),
                  pl.BlockSpec((tm, LANES), lambda i: (i, 0)),
                  pl.BlockSpec((1, d), lambda i: (0, 0))],
        out_specs=pl.BlockSpec((tm, d), lambda i: (i, 0)),
        out_shape=jax.ShapeDtypeStruct((t, d), F32),
        compiler_params=pltpu.CompilerParams(
            dimension_semantics=("arbitrary",), vmem_limit_bytes=VMEM_LIMIT),
        name="combine",
    )(ys, h, dest, gain)


def _rotary_tables(s):
    half = HEAD_DIM // 2
    inv = ROPE_BASE ** (-jnp.arange(half, dtype=F32) / half)
    ang = jnp.arange(s, dtype=F32)[:, None] * inv[None, :]
    cos, sin = jnp.cos(ang), jnp.sin(ang)
    cos_t = jnp.tile(jnp.concatenate([cos, cos], axis=-1), (1, LANES // HEAD_DIM))
    sin_t = jnp.tile(jnp.concatenate([-sin, sin], axis=-1), (1, LANES // HEAD_DIM))
    return cos_t, sin_t


def kernel(x, w_in, w_out, norm_mix, norm_ffn, norm_final, attn_out_gain, rel_bias, ret_decay_fwd, ret_decay_bwd, router_group_w, router_group_b, router_expert_w, router_expert_b, expert_w1, expert_w3, expert_w2):
    b, s, d = x.shape
    depth = w_in.shape[0]
    cos_t, sin_t = _rotary_tables(s)
    bias_rows = _attn_bias_rows(rel_bias)
    h = x
    for layer in range(depth):
        (aq, ak, av, rq, rk, rv, rg), qkv_grouped = _inproj(
            h, norm_mix[layer][None], w_in[layer].astype(BF16), cos_t, sin_t)
        attn = _attention((aq, ak, av), qkv_grouped, bias_rows)
        ret = _retention(rq, rk, rv, rg, *_retention_tables(ret_decay_fwd[layer], ret_decay_bwd[layer]))

        w_o = w_out[layer].astype(BF16)
        rw = jnp.concatenate(
            [jnp.transpose(router_expert_w[layer], (1, 0, 2)).reshape(d, N_EXPERTS),
             router_group_w[layer],
             jnp.zeros((d, LANES - N_EXPERTS - N_GROUPS), F32)], axis=1).astype(BF16)
        rb = jnp.concatenate(
            [router_expert_b[layer].reshape(N_EXPERTS), router_group_b[layer],
             jnp.zeros((LANES - N_EXPERTS - N_GROUPS,), F32)])[None].astype(F32)
        h1, xs, dest, cnt = _outproj(
            attn.reshape(b * s, ATTN_WIDTH), ret.reshape(b * s, RET_WIDTH), h.reshape(b * s, d),
            w_o[:ATTN_WIDTH], w_o[ATTN_WIDTH:], attn_out_gain[layer][None], norm_ffn[layer][None],
            rw, rb)
        n_route_tiles = (b * s) // ROUTE_TM
        n_moe_tiles = (b * s + n_route_tiles * N_GROUPS * (CHUNK - 1)) // MOE_TM + N_GROUPS
        tile_group, tile_used, src_chunk = _dispatch_tables(cnt, n_moe_tiles)
        ys = _moe(xs, expert_w1[layer].astype(BF16), expert_w3[layer].astype(BF16),
                  expert_w2[layer].astype(BF16), tile_group, tile_used, src_chunk)
        assert depth == 1, "the combine kernel fuses the final norm, so it must run on the last layer"
        h = _combine(ys, h1, dest, norm_final[None]).reshape(b, s, d)
    return h
```

```python
import functools
import math

import jax
import jax.numpy as jnp
from jax import lax
from jax.experimental import pallas as pl
from jax.experimental.pallas import tpu as pltpu

F32 = jnp.float32
BF16 = jnp.bfloat16

D_MODEL = 1024
HEAD_DIM = 64
ATTN_WIDTH = 512
RET_WIDTH = 512
N_HEADS = 8
PAIR = 2 * HEAD_DIM
N_PAIRS = N_HEADS // 2
ATTN_DILATIONS = (1, 4, 16)
ATTN_RADIUS = 64
N_BUCKETS = 32
REL_MAX_DIST = 1024
ROPE_BASE = 10000.0
N_GROUPS = 4
EXPERTS_PER_GROUP = 4
N_EXPERTS = 16
EXPERT_FF = 512
EPS = 1e-6
NEG_INF = -1e30
LOG2_E = math.log2(math.e)

LANES = 128
ATTN_TQ = 128
ATTN_W = 256
N_BIAS_VARIANTS = 8
RET_CHUNK = 256
ROUTE_TM = 512
SORT_ROWS = 640
CHUNK = 16
MOE_TM = 512
XS_W = D_MODEL + 2 * LANES
VMEM_LIMIT = 48 * 1024 * 1024


def _rms(x, gain):
    return x * lax.rsqrt(jnp.mean(x * x, axis=-1, keepdims=True) + EPS) * gain


def _inproj_kernel(x_ref, gain_ref, w_ref, cos_ref, sin_ref,
                   aq_ref, ak_ref, av_ref, rq_ref, rk_ref, rv_ref, rg_ref,
                   aq4_ref, ak4_ref, av4_ref, aq16_ref, ak16_ref, av16_ref, stage_ref, stage4_ref):
    tm = x_ref.shape[0]
    xn = _rms(x_ref[...], gain_ref[...]).astype(BF16)

    def emit(i, t, nat_ref, d4_ref, d16_ref):
        nat_ref[...] = t.astype(BF16)
        for hp in range(N_PAIRS):
            stage_ref[i, hp] = t[:, hp * PAIR:(hp + 1) * PAIR]
            for r4 in range(4):
                g4 = stage_ref[i, hp, pl.ds(r4, tm // 4, stride=4), :]
                d4_ref[hp, r4] = g4.astype(BF16)
                stage4_ref[i, hp, r4] = g4
                for j in range(4):
                    d16_ref[hp, r4 + 4 * j] = (
                        stage4_ref[i, hp, r4, pl.ds(j, tm // 16, stride=4), :].astype(BF16))

    def seg(i):
        return jnp.dot(xn, w_ref[:, i * 512:(i + 1) * 512], preferred_element_type=F32)

    def rotary(t):
        cos, sin = cos_ref[...], sin_ref[...]
        first_half = (lax.broadcasted_iota(jnp.int32, (1, LANES), 1) % HEAD_DIM) < HEAD_DIM // 2
        outs = []
        for j in range(t.shape[1] // LANES):
            tj = t[:, j * LANES:(j + 1) * LANES]
            partner = jnp.where(first_half, pltpu.roll(tj, LANES - 32, 1), pltpu.roll(tj, 32, 1))
            outs.append(tj * cos + partner * sin)
        return jnp.concatenate(outs, axis=1)

    emit(0, seg(0) * (HEAD_DIM ** -0.5 * LOG2_E), aq_ref, aq4_ref, aq16_ref)
    emit(1, seg(1), ak_ref, ak4_ref, ak16_ref)
    emit(2, seg(2), av_ref, av4_ref, av16_ref)
    rq_ref[...] = rotary(seg(3)).astype(BF16)
    rk_ref[...] = (rotary(seg(4)) * (HEAD_DIM ** -0.5)).astype(BF16)
    rv_ref[...] = seg(5).astype(BF16)
    rg_ref[...] = seg(6).astype(BF16)


def _inproj(x, gain, w_in, cos_t, sin_t, tm=512):
    b, s, d = x.shape
    n = w_in.shape[1]
    out = jax.ShapeDtypeStruct((b, s, 512), BF16)
    ospec = pl.BlockSpec((None, tm, 512), lambda si, bi: (bi, si, 0))

    def grouped(dil):
        shape = jax.ShapeDtypeStruct((b, N_PAIRS, dil, s // dil, PAIR), BF16)
        spec = pl.BlockSpec((None, N_PAIRS, dil, tm // dil, PAIR), lambda si, bi: (bi, 0, 0, si, 0))
        return [shape] * 3, [spec] * 3

    shapes4, specs4 = grouped(4)
    shapes16, specs16 = grouped(16)
    outs = pl.pallas_call(
        _inproj_kernel,
        grid=(s // tm, b),
        in_specs=[
            pl.BlockSpec((None, tm, d), lambda si, bi: (bi, si, 0)),
            pl.BlockSpec((1, d), lambda si, bi: (0, 0)),
            pl.BlockSpec((d, n), lambda si, bi: (0, 0)),
            pl.BlockSpec((tm, LANES), lambda si, bi: (si, 0)),
            pl.BlockSpec((tm, LANES), lambda si, bi: (si, 0)),
        ],
        out_specs=[ospec] * 7 + specs4 + specs16,
        out_shape=[out] * 7 + shapes4 + shapes16,
        scratch_shapes=[pltpu.VMEM((3, N_PAIRS, tm, PAIR), F32),
                        pltpu.VMEM((3, N_PAIRS, 4, tm // 4, PAIR), F32)],
        compiler_params=pltpu.CompilerParams(
            dimension_semantics=("arbitrary", "arbitrary"), vmem_limit_bytes=VMEM_LIMIT),
        name="inproj",
    )(x, gain, w_in, cos_t, sin_t)
    return outs[:7], [o.reshape(b, N_PAIRS, s, PAIR) for o in outs[7:]]


def _t5_bucket(rel):
    half = N_BUCKETS // 2
    max_exact = half // 2
    offset = jnp.where(rel > 0, half, 0)
    n = jnp.abs(rel)
    nf = jnp.maximum(n, 1).astype(F32)
    large = max_exact + (jnp.log(nf / max_exact) / math.log(REL_MAX_DIST / max_exact)
                         * (half - max_exact)).astype(jnp.int32)
    large = jnp.minimum(large, half - 1)
    return offset + jnp.where(n < max_exact, n, large)


def _attn_bias_rows(rel_bias):
    period = 2 * ATTN_W
    band = 2 * ATTN_RADIUS + 1
    rel = jnp.arange(-ATTN_RADIUS, ATTN_RADIUS + 1)
    rows = []
    for dil, offs in ((1, (0, 64, 128)), (4, (0, 64, 128)), (16, (0, 128))):
        vals = rel_bias[_t5_bucket(rel * dil)].astype(F32).T * LOG2_E
        for off in offs:
            lo = off - ATTN_RADIUS
            pad = jnp.full((N_HEADS, period - band), NEG_INF, F32)
            if lo >= 0:
                row = jnp.concatenate([pad[:, :lo], vals, pad[:, lo:]], axis=1)
            else:
                row = jnp.concatenate([vals[:, -lo:], pad, vals[:, :-lo]], axis=1)
            rows.append(row)
    v = jnp.stack(rows, axis=1)
    return v.reshape(N_PAIRS, 2 * N_BIAS_VARIANTS, period)


def _attention_kernel(q_ref, k_ref, v_ref, q4_ref, k4_ref, v4_ref, q16_ref, k16_ref, v16_ref,
                      rows_ref, o_ref, bias_ref, acc_ref, m_ref, l_ref, out_ref):
    s = q_ref.shape[0]
    n_tiles = s // ATTN_TQ
    lane = lax.broadcasted_iota(jnp.int32, (1, PAIR), 1)
    head0 = lane < HEAD_DIM

    @pl.when(pl.program_id(1) == 0)
    def _():
        col = lax.broadcasted_iota(jnp.int32, (ATTN_TQ, ATTN_W), 1)
        for idx in range(2 * N_BIAS_VARIANTS):
            gen = jnp.broadcast_to(rows_ref[idx:idx + 1, :], (ATTN_TQ, 2 * ATTN_W))
            tab = pltpu.roll(gen, 0, 1, stride=1, stride_axis=0)[:, :ATTN_W]
            var = idx % N_BIAS_VARIANTS
            if var >= 6:
                tab = jnp.where((col // ATTN_TQ) == var - 6, tab, NEG_INF)
            head = idx // N_BIAS_VARIANTS
            bias_ref[var, head * ATTN_TQ:(head + 1) * ATTN_TQ, :] = tab

    def run_branch(bi, dil, qs_ref, ks_ref, vs_ref):
        sub_len = s // dil
        tiles_per_sub = sub_len // ATTN_TQ

        def tile(t, carry):
            q0 = pl.multiple_of(t * ATTN_TQ, ATTN_TQ)
            if tiles_per_sub == 1:
                ws = pl.multiple_of((t // 2) * ATTN_W, ATTN_W)
                var = 6 + t % 2
            else:
                pos = t % tiles_per_sub
                sub_lo = (t // tiles_per_sub) * sub_len
                ws = jnp.clip(q0 - 64, sub_lo, sub_lo + sub_len - ATTN_W)
                ws = pl.multiple_of(ws, 64)
                var = jnp.where(pos == 0, 0, jnp.where(pos == tiles_per_sub - 1, 2, 1)) + 3 * bi
            q = qs_ref[pl.ds(q0, ATTN_TQ), :]
            k = ks_ref[pl.ds(ws, ATTN_W), :]
            v = vs_ref[pl.ds(ws, ATTN_W), :]
            q2 = jnp.concatenate([jnp.where(head0, q, jnp.zeros_like(q)),
                                  jnp.where(head0, jnp.zeros_like(q), q)], axis=0)
            sc = lax.dot_general(q2, k, (((1,), (1,)), ((), ())), preferred_element_type=F32)
            sc = sc + bias_ref[var]
            m = jnp.max(sc, axis=-1, keepdims=True)
            p = jnp.exp2(sc - m).astype(BF16)
            o = jnp.dot(p, jnp.concatenate([v, jnp.ones_like(v)], axis=1), preferred_element_type=F32)
            l = o[:, PAIR:]
            outs, ms, ls = (o[:ATTN_TQ, :PAIR], o[ATTN_TQ:, :PAIR]), (m[:ATTN_TQ], m[ATTN_TQ:]), (l[:ATTN_TQ], l[ATTN_TQ:])
            if dil == 1:
                dst = pl.ds(q0, ATTN_TQ)
            elif dil == 4:
                dst = pl.ds((t % 4) * (4 * ATTN_TQ) + t // 4, ATTN_TQ, stride=4)
            else:
                dst = pl.ds((t % 4) * (s // 4) + t // 4, ATTN_TQ, stride=4)
            acc_ref[bi, dst, :] = jnp.where(head0, outs[0], outs[1])
            m_ref[bi, dst, :] = jnp.where(head0, ms[0], ms[1])
            l_ref[bi, dst, :] = jnp.where(head0, ls[0], ls[1])
            return carry

        lax.fori_loop(0, n_tiles, tile, 0, unroll=16)

    run_branch(0, 1, q_ref, k_ref, v_ref)
    run_branch(1, 4, q4_ref, k4_ref, v4_ref)
    run_branch(2, 16, q16_ref, k16_ref, v16_ref)

    rows = ATTN_TQ
    for r4 in range(4):
        for blk in range(s // (4 * rows)):
            nat = pl.ds(r4 + 4 * rows * blk, rows, stride=4)
            sl = (nat, nat, pl.ds(r4 * (s // 4) + rows * blk, rows))
            m = [m_ref[bi, sl[bi], :] for bi in range(3)]
            mx = jnp.maximum(jnp.maximum(m[0], m[1]), m[2])
            num = jnp.zeros((rows, PAIR), F32)
            den = jnp.zeros((rows, PAIR), F32)
            for bi in range(3):
                e = jnp.exp2(m[bi] - mx)
                num = num + e * acc_ref[bi, sl[bi], :]
                den = den + e * l_ref[bi, sl[bi], :]
            out_ref[nat, :] = num / den
    o_ref[...] = out_ref[...].astype(o_ref.dtype)


def _attention(qkv, qkv_grouped, bias_rows):
    b, s, _ = qkv[0].shape
    spec = pl.BlockSpec((None, s, PAIR), lambda hp, bi: (bi, 0, hp))
    gspec = pl.BlockSpec((None, None, s, PAIR), lambda hp, bi: (bi, hp, 0, 0))
    return pl.pallas_call(
        _attention_kernel,
        grid=(N_PAIRS, b),
        in_specs=[spec] * 3 + [gspec] * 6 + [
            pl.BlockSpec((None, 2 * N_BIAS_VARIANTS, 2 * ATTN_W), lambda hp, bi: (hp, 0, 0))],
        out_specs=spec,
        out_shape=jax.ShapeDtypeStruct((b, s, ATTN_WIDTH), BF16),
        scratch_shapes=[
            pltpu.VMEM((N_BIAS_VARIANTS, 2 * ATTN_TQ, ATTN_W), F32),
            pltpu.VMEM((3, s, PAIR), F32),
            pltpu.VMEM((3, s, PAIR), F32),
            pltpu.VMEM((3, s, PAIR), F32),
            pltpu.VMEM((s, PAIR), F32),
        ],
        compiler_params=pltpu.CompilerParams(
            dimension_semantics=("arbitrary", "arbitrary"), vmem_limit_bytes=VMEM_LIMIT),
        name="attention",
    )(*qkv, *qkv_grouped, bias_rows)


def _retention_tables(decay_fwd, decay_bwd):
    c = RET_CHUNK
    lg_f = -jnp.exp(decay_fwd.astype(F32))
    lg_b = -jnp.exp(decay_bwd.astype(F32))
    idx = jnp.arange(c, dtype=F32)
    rel = idx[:, None] - idx[None, :]
    dmat = jnp.where(rel >= 0,
                     jnp.exp(lg_f[:, None, None] * jnp.maximum(rel, 0.0)[None]),
                     jnp.exp(lg_b[:, None, None] * jnp.maximum(-rel, 0.0)[None]))
    dmat = dmat.reshape(N_PAIRS, 2, c, c)

    def lanes(v):
        v = v.reshape(N_PAIRS, 2, -1)
        return jnp.repeat(jnp.transpose(v, (0, 2, 1)), HEAD_DIM, axis=2)

    vec = jnp.stack([
        lanes(jnp.exp(lg_f[:, None] * (idx + 1.0)[None])),
        lanes(jnp.exp(lg_f[:, None] * (c - 1.0 - idx)[None])),
        lanes(jnp.exp(lg_b[:, None] * (c - idx)[None])),
        lanes(jnp.exp(lg_b[:, None] * idx[None])),
    ], axis=1)
    same_head = (jnp.arange(PAIR)[:, None] // HEAD_DIM) == (jnp.arange(PAIR)[None, :] // HEAD_DIM)
    cd = jnp.stack([lanes(jnp.exp(lg_f * c)[:, None]), lanes(jnp.exp(lg_b * c)[:, None])], axis=1)
    cd = jnp.transpose(cd, (0, 1, 3, 2)) * same_head[None, None].astype(F32)
    return dmat, vec, cd


def _retention_kernel(q_ref, k_ref, v_ref, g_ref, dmat_ref, vec_ref, cd_ref, o_ref, y_ref, kv_ref):
    s = q_ref.shape[0]
    c = RET_CHUNK
    nc = s // c
    lane = lax.broadcasted_iota(jnp.int32, (1, PAIR), 1)
    head0 = lane < HEAD_DIM
    same_head = ((lax.broadcasted_iota(jnp.int32, (PAIR, PAIR), 0) // HEAD_DIM)
                 == (lax.broadcasted_iota(jnp.int32, (PAIR, PAIR), 1) // HEAD_DIM))

    for n in range(nc):
        rows = slice(n * c, (n + 1) * c)
        q, k, v = q_ref[rows, :], k_ref[rows, :], v_ref[rows, :]
        outs = []
        for h in range(2):
            qh = jnp.where(head0 if h == 0 else ~head0, q, jnp.zeros_like(q))
            sc = lax.dot_general(qh, k, (((1,), (1,)), ((), ())), preferred_element_type=F32)
            outs.append(jnp.dot((sc * dmat_ref[h]).astype(BF16), v, preferred_element_type=F32))
        y_ref[rows, :] = jnp.where(head0, outs[0], outs[1])
        kf = k.astype(F32)
        kcat = jnp.concatenate([(kf * vec_ref[1]).astype(BF16), (kf * vec_ref[3]).astype(BF16)], axis=1)
        kv = lax.dot_general(kcat, v, (((0,), (0,)), ((), ())), preferred_element_type=F32)
        kv_ref[0, n] = jnp.where(same_head, kv[:PAIR], 0.0)
        kv_ref[1, n] = jnp.where(same_head, kv[PAIR:], 0.0)

    def cross(n, state, qdec):
        rows = slice(n * c, (n + 1) * c)
        qd = (q_ref[rows, :].astype(F32) * qdec).astype(BF16)
        y_ref[rows, :] += jnp.dot(qd, state.astype(BF16), preferred_element_type=F32)

    state = jnp.zeros((PAIR, PAIR), F32)
    for n in range(nc):
        cross(n, state, vec_ref[0])
        state = state * cd_ref[0] + kv_ref[0, n]
    state = jnp.zeros((PAIR, PAIR), F32)
    for n in reversed(range(nc)):
        cross(n, state, vec_ref[2])
        state = state * cd_ref[1] + kv_ref[1, n]

    y = y_ref[...]
    y2 = y * y
    ms0 = jnp.sum(jnp.where(head0, y2, 0.0), axis=-1, keepdims=True)
    ms1 = jnp.sum(jnp.where(head0, 0.0, y2), axis=-1, keepdims=True)
    ms = jnp.where(head0, ms0, ms1) * (1.0 / HEAD_DIM)
    g = g_ref[...].astype(F32)
    o_ref[...] = (y * lax.rsqrt(ms + EPS) * (g * jax.nn.sigmoid(g))).astype(o_ref.dtype)


def _retention(rq, rk, rv, rg, dmat, vec, cd):
    b, s, _ = rq.shape
    c = RET_CHUNK
    spec = pl.BlockSpec((None, s, PAIR), lambda hp, bi: (bi, 0, hp))
    return pl.pallas_call(
        _retention_kernel,
        grid=(N_PAIRS, b),
        in_specs=[spec, spec, spec, spec,
                  pl.BlockSpec((None, 2, c, c), lambda hp, bi: (hp, 0, 0, 0)),
                  pl.BlockSpec((None, 4, c, PAIR), lambda hp, bi: (hp, 0, 0, 0)),
                  pl.BlockSpec((None, 2, PAIR, PAIR), lambda hp, bi: (hp, 0, 0, 0))],
        out_specs=spec,
        out_shape=jax.ShapeDtypeStruct((b, s, RET_WIDTH), BF16),
        scratch_shapes=[pltpu.VMEM((s, PAIR), F32),
                        pltpu.VMEM((2, s // c, PAIR, PAIR), F32)],
        compiler_params=pltpu.CompilerParams(
            dimension_semantics=("arbitrary", "arbitrary"), vmem_limit_bytes=VMEM_LIMIT),
        name="retention",
    )(rq, rk, rv, rg, dmat, vec, cd)


def _outproj_kernel(attn_ref, ret_ref, x_ref, wa_ref, wr_ref, again_ref, fgain_ref,
                    rw_ref, rb_ref, h_ref, xs_ref, dest_ref, cnt_ref):
    a = _rms(attn_ref[...].astype(F32), again_ref[...]).astype(BF16)
    h = (x_ref[...]
         + jnp.dot(a, wa_ref[...], preferred_element_type=F32)
         + jnp.dot(ret_ref[...], wr_ref[...], preferred_element_type=F32))
    h_ref[...] = h
    hn = _rms(h, fgain_ref[...]).astype(BF16)

    logits = jnp.dot(hn, rw_ref[...], preferred_element_type=F32) + rb_ref[...]
    tm = logits.shape[0]
    lane = lax.broadcasted_iota(jnp.int32, logits.shape, 1)
    big = jnp.int32(LANES)

    def first_argmax(vals):
        top = jnp.max(vals, axis=-1, keepdims=True)
        return top, jnp.min(jnp.where(vals == top, lane, big), axis=-1, keepdims=True)

    gl = jnp.where((lane >= N_EXPERTS) & (lane < N_EXPERTS + N_GROUPS), logits, -jnp.inf)
    gmax, gidx = first_argmax(gl)
    p_group = 1.0 / jnp.sum(jnp.exp(gl - gmax), axis=-1, keepdims=True)
    grp = gidx - N_EXPERTS
    lo = grp * EXPERTS_PER_GROUP
    el = jnp.where((lane >= lo) & (lane < lo + EXPERTS_PER_GROUP), logits, -jnp.inf)
    v1, i1 = first_argmax(el)
    v2, i2 = first_argmax(jnp.where(lane == i1, -jnp.inf, el))
    e2 = jnp.exp(v2 - v1)
    p1 = p_group / (1.0 + e2)
    gates = jnp.where(lane == i1, p1, jnp.where(lane == i2, p1 * e2, 0.0))

    onehot = jnp.where(lane == grp, 1.0, 0.0)
    r_i = lax.broadcasted_iota(jnp.int32, (tm, tm), 0)
    c_i = lax.broadcasted_iota(jnp.int32, (tm, tm), 1)
    before = jnp.where(c_i < r_i, 1.0, 0.0).astype(BF16)
    rank = jnp.dot(before, onehot.astype(BF16), preferred_element_type=F32)
    cnt = jnp.sum(onehot, axis=0, keepdims=True)
    c16 = jnp.broadcast_to(jnp.ceil(cnt * (1.0 / CHUNK)) * CHUNK, (8, LANES))
    lane8 = lax.broadcasted_iota(jnp.int32, (8, LANES), 1)
    start = jnp.zeros((8, LANES), F32)
    for sft in range(1, N_GROUPS):
        start = start + jnp.where(lane8 >= sft, pltpu.roll(c16, sft, 1), 0.0)
    dest = jnp.sum(onehot * (start[0:1] + rank), axis=-1, keepdims=True)
    dest_ref[...] = jnp.broadcast_to(dest, (tm, LANES))
    cnt_ref[...] = jnp.where(lane8 < N_GROUPS, c16, pltpu.roll(start, N_GROUPS, 1)).astype(jnp.int32)

    onehot_t = onehot.T
    rank_t = lax.dot_general(onehot_t.astype(BF16), before, (((1,), (1,)), ((), ())),
                             preferred_element_type=F32)
    g_row = lax.broadcasted_iota(jnp.int32, (LANES, LANES), 0)
    g_col = lax.broadcasted_iota(jnp.int32, (LANES, LANES), 1)
    start_t = jnp.sum(jnp.where(g_col < g_row, c16[0:1], 0.0), axis=-1, keepdims=True)
    dest_t = jnp.sum(onehot_t * (start_t + rank_t), axis=0, keepdims=True)
    perm = jnp.where(lax.broadcasted_iota(jnp.int32, (SORT_ROWS, tm), 0) == dest_t.astype(jnp.int32),
                     1.0, 0.0).astype(BF16)
    g_hi = gates.astype(BF16)
    g_lo = (gates - g_hi.astype(F32)).astype(BF16)
    payload = jnp.concatenate([hn, g_hi, g_lo], axis=1)
    xs_ref[...] = jnp.dot(perm, payload, preferred_element_type=F32).astype(BF16)


def _outproj(attn, ret, x, w_a, w_r, again, fgain, rw, rb):
    t, d = x.shape
    tm = ROUTE_TM
    nt = t // tm
    row = lambda w: pl.BlockSpec((tm, w), lambda i: (i, 0))
    full = lambda r, c: pl.BlockSpec((r, c), lambda i: (0, 0))
    return pl.pallas_call(
        _outproj_kernel,
        grid=(nt,),
        in_specs=[row(ATTN_WIDTH), row(RET_WIDTH), row(d), full(ATTN_WIDTH, d), full(RET_WIDTH, d),
                  full(1, ATTN_WIDTH), full(1, d), full(d, LANES), full(1, LANES)],
        out_specs=[row(d), pl.BlockSpec((SORT_ROWS, XS_W), lambda i: (i, 0)), row(LANES),
                   pl.BlockSpec((None, 8, LANES), lambda i: (i, 0, 0))],
        out_shape=[jax.ShapeDtypeStruct((t, d), F32),
                   jax.ShapeDtypeStruct((nt * SORT_ROWS, XS_W), BF16),
                   jax.ShapeDtypeStruct((t, LANES), F32),
                   jax.ShapeDtypeStruct((nt, 8, LANES), jnp.int32)],
        compiler_params=pltpu.CompilerParams(
            dimension_semantics=("arbitrary",), vmem_limit_bytes=VMEM_LIMIT),
        name="outproj",
    )(attn, ret, x, w_a, w_r, again, fgain, rw, rb)


def _dispatch_tables(cnt, n_moe_tiles):
    nt = cnt.shape[0]
    cpt = MOE_TM // CHUNK

    def schedule_kernel(seg_ref, grp_ref, used_ref, src_ref):
        pos = jnp.int32(0)
        for g in range(N_GROUPS):
            def tile_body(i, p):
                n = seg_ref[i, g] // CHUNK
                first = (i * SORT_ROWS + seg_ref[i, N_GROUPS + g]) // CHUNK

                def chunk_body(j, carry):
                    src_ref[p + j] = first + j
                    return carry
                lax.fori_loop(0, n, chunk_body, 0)
                return p + n
            end = lax.fori_loop(0, nt, tile_body, pos)
            padded = ((end + cpt - 1) // cpt) * cpt

            def pad_body(k, carry):
                src_ref[k] = -1
                return carry
            lax.fori_loop(end, padded, pad_body, 0)

            def mark_body(m, carry):
                grp_ref[m] = g
                used_ref[m] = 1
                return carry
            lax.fori_loop(pos // cpt, padded // cpt, mark_body, 0)
            pos = padded

        def idle_body(m, carry):
            grp_ref[m] = N_GROUPS - 1
            used_ref[m] = 0
            return carry
        lax.fori_loop(pos // cpt, n_moe_tiles, idle_body, 0)
        lax.fori_loop(pos, n_moe_tiles * cpt, pad_body, 0)

    smem = lambda: pl.BlockSpec(memory_space=pltpu.SMEM)
    return pl.pallas_call(
        schedule_kernel,
        in_specs=[smem()],
        out_specs=[smem(), smem(), smem()],
        out_shape=[jax.ShapeDtypeStruct((n_moe_tiles,), jnp.int32),
                   jax.ShapeDtypeStruct((n_moe_tiles,), jnp.int32),
                   jax.ShapeDtypeStruct((n_moe_tiles * cpt,), jnp.int32)],
        name="schedule",
    )(cnt[:, 0, :2 * N_GROUPS])


def _moe_kernel(grp_ref, used_ref, src_ref, xs_hbm, w1_ref, w3_ref, w2_ref, ys_in_hbm, ys_hbm,
                xbuf, obuf, in_sem, out_sem):
    del ys_in_hbm
    m = pl.program_id(0)
    n_tiles = pl.num_programs(0)
    cpt = MOE_TM // CHUNK
    slot = m % 2

    def rows(c):
        return pl.ds(pl.multiple_of(c * CHUNK, CHUNK), CHUNK)

    def gather(tile, sl, wait):
        def body(c, carry):
            src = src_ref[tile * cpt + c]

            @pl.when(src >= 0)
            def _():
                cp = pltpu.make_async_copy(xs_hbm.at[rows(src), :], xbuf.at[sl, rows(c), :], in_sem.at[sl])
                cp.wait() if wait else cp.start()

            if not wait:
                @pl.when(src < 0)
                def _():
                    xbuf[sl, rows(c), :] = jnp.zeros((CHUNK, XS_W), BF16)
            return carry
        lax.fori_loop(0, cpt, body, 0)

    def scatter(tile, sl, wait):
        def body(c, carry):
            src = src_ref[tile * cpt + c]

            @pl.when(src >= 0)
            def _():
                cp = pltpu.make_async_copy(obuf.at[sl, rows(c), :], ys_hbm.at[rows(src), :], out_sem.at[sl])
                cp.wait() if wait else cp.start()
            return carry
        lax.fori_loop(0, cpt, body, 0)

    @pl.when(m == 0)
    def _():
        gather(0, 0, False)

    @pl.when(m + 1 < n_tiles)
    def _():
        gather(m + 1, 1 - slot, False)

    gather(m, slot, True)

    @pl.when(m >= 2)
    def _():
        scatter(m - 2, slot, True)

    @pl.when(used_ref[m] > 0)
    def _():
        x = xbuf[slot, :, :D_MODEL]
        gate = (xbuf[slot, :, D_MODEL:D_MODEL + LANES].astype(F32)
                + xbuf[slot, :, D_MODEL + LANES:].astype(F32))
        lane = lax.broadcasted_iota(jnp.int32, gate.shape, 1)
        base = grp_ref[m] * EXPERTS_PER_GROUP
        acc = jnp.zeros((MOE_TM, D_MODEL), F32)
        for j in range(EXPERTS_PER_GROUP):
            a = jnp.dot(x, w1_ref[j], preferred_element_type=F32)
            b = jnp.dot(x, w3_ref[j], preferred_element_type=F32)
            gj = jnp.sum(jnp.where(lane == base + j, gate, 0.0), axis=-1, keepdims=True)
            hid = (a * jax.nn.sigmoid(a) * b * gj).astype(BF16)
            acc = acc + jnp.dot(hid, w2_ref[j], preferred_element_type=F32)
        obuf[slot] = acc.astype(BF16)

    scatter(m, slot, False)

    @pl.when(m == n_tiles - 1)
    def _():
        scatter(m, slot, True)

        @pl.when(m >= 1)
        def _():
            scatter(m - 1, 1 - slot, True)


def _moe(xs, w1, w3, w2, tile_group, tile_used, src_chunk):
    n_moe_tiles = tile_group.shape[0]
    rows = xs.shape[0]
    d = D_MODEL
    wspec = lambda r, c: pl.BlockSpec((EXPERTS_PER_GROUP, r, c), lambda m, grp, used, src: (grp[m], 0, 0))
    ys0 = jnp.zeros((rows, d), BF16)
    return pl.pallas_call(
        _moe_kernel,
        grid_spec=pltpu.PrefetchScalarGridSpec(
            num_scalar_prefetch=3,
            grid=(n_moe_tiles,),
            in_specs=[pl.BlockSpec(memory_space=pl.ANY),
                      wspec(d, EXPERT_FF), wspec(d, EXPERT_FF), wspec(EXPERT_FF, d),
                      pl.BlockSpec(memory_space=pl.ANY)],
            out_specs=pl.BlockSpec(memory_space=pl.ANY),
            scratch_shapes=[pltpu.VMEM((2, MOE_TM, XS_W), BF16),
                            pltpu.VMEM((2, MOE_TM, d), BF16),
                            pltpu.SemaphoreType.DMA((2,)),
                            pltpu.SemaphoreType.DMA((2,))]),
        out_shape=jax.ShapeDtypeStruct((rows, d), BF16),
        input_output_aliases={7: 0},
        compiler_params=pltpu.CompilerParams(
            dimension_semantics=("arbitrary",), vmem_limit_bytes=VMEM_LIMIT),
        name="moe",
    )(tile_group, tile_used, src_chunk, xs, w1, w3, w2, ys0)


def _combine_kernel(ys_ref, h_ref, dest_ref, gain_ref, o_ref):
    tm = h_ref.shape[0]
    dest = dest_ref[:, 0:1].astype(jnp.int32)
    perm_t = jnp.where(lax.broadcasted_iota(jnp.int32, (tm, SORT_ROWS), 1) == dest, 1.0, 0.0).astype(BF16)
    moe = jnp.dot(perm_t, ys_ref[...], preferred_element_type=F32)
    o_ref[...] = _rms(h_ref[...] + moe, gain_ref[...])


def _combine(ys, h, dest, gain):
    t, d = h.shape
    tm = ROUTE_TM
    return pl.pallas_call(
        _combine_kernel,
        grid=(t // tm,),
        in_specs=[pl.BlockSpec((SORT_ROWS, d), lambda i: (i, 0)),
                  pl.BlockSpec((tm, d), lambda i: (i, 0)),
                  pl.BlockSpec((tm, LANES), lambda i: (i, 0)),
                  pl.BlockSpec((1, d), lambda i: (0, 0))],
        out_specs=pl.BlockSpec((tm, d), lambda i: (i, 0)),
        out_shape=jax.ShapeDtypeStruct((t, d), F32),
        compiler_params=pltpu.CompilerParams(
            dimension_semantics=("arbitrary",), vmem_limit_bytes=VMEM_LIMIT),
        name="combine",
    )(ys, h, dest, gain)


def _rotary_tables(s):
    half = HEAD_DIM // 2
    inv = ROPE_BASE ** (-jnp.arange(half, dtype=F32) / half)
    ang = jnp.arange(s, dtype=F32)[:, None] * inv[None, :]
    cos, sin = jnp.cos(ang), jnp.sin(ang)
    cos_t = jnp.tile(jnp.concatenate([cos, cos], axis=-1), (1, LANES // HEAD_DIM))
    sin_t = jnp.tile(jnp.concatenate([-sin, sin], axis=-1), (1, LANES // HEAD_DIM))
    return cos_t, sin_t


def kernel(x, w_in, w_out, norm_mix, norm_ffn, norm_final, attn_out_gain, rel_bias, ret_decay_fwd, ret_decay_bwd, router_group_w, router_group_b, router_expert_w, router_expert_b, expert_w1, expert_w3, expert_w2):
    b, s, d = x.shape
    depth = w_in.shape[0]
    cos_t, sin_t = _rotary_tables(s)
    bias_rows = _attn_bias_rows(rel_bias)
    h = x
    for layer in range(depth):
        (aq, ak, av, rq, rk, rv, rg), qkv_grouped = _inproj(
            h, norm_mix[layer][None], w_in[layer].astype(BF16), cos_t, sin_t)
        attn = _attention((aq, ak, av), qkv_grouped, bias_rows)
        ret = _retention(rq, rk, rv, rg, *_retention_tables(ret_decay_fwd[layer], ret_decay_bwd[layer]))

        w_o = w_out[layer].astype(BF16)
        rw = jnp.concatenate(
            [jnp.transpose(router_expert_w[layer], (1, 0, 2)).reshape(d, N_EXPERTS),
             router_group_w[layer],
             jnp.zeros((d, LANES - N_EXPERTS - N_GROUPS), F32)], axis=1).astype(BF16)
        rb = jnp.concatenate(
            [router_expert_b[layer].reshape(N_EXPERTS), router_group_b[layer],
             jnp.zeros((LANES - N_EXPERTS - N_GROUPS,), F32)])[None].astype(F32)
        h1, xs, dest, cnt = _outproj(
            attn.reshape(b * s, ATTN_WIDTH), ret.reshape(b * s, RET_WIDTH), h.reshape(b * s, d),
            w_o[:ATTN_WIDTH], w_o[ATTN_WIDTH:], attn_out_gain[layer][None], norm_ffn[layer][None],
            rw, rb)
        n_route_tiles = (b * s) // ROUTE_TM
        n_moe_tiles = (b * s + n_route_tiles * N_GROUPS * (CHUNK - 1)) // MOE_TM + N_GROUPS
        tile_group, tile_used, src_chunk = _dispatch_tables(cnt, n_moe_tiles)
        ys = _moe(xs, expert_w1[layer].astype(BF16), expert_w3[layer].astype(BF16),
                  expert_w2[layer].astype(BF16), tile_group, tile_used, src_chunk)
        assert depth == 1, "the combine kernel fuses the final norm, so it must run on the last layer"
        h = _combine(ys, h1, dest, norm_final[None]).reshape(b, s, d)
    return h
```

```python
import functools
import math

import jax
import jax.numpy as jnp
from jax import lax
from jax.experimental import pallas as pl
from jax.experimental.pallas import tpu as pltpu

F32 = jnp.float32
BF16 = jnp.bfloat16

D_MODEL = 1024
HEAD_DIM = 64
ATTN_WIDTH = 512
RET_WIDTH = 512
N_HEADS = 8
PAIR = 2 * HEAD_DIM
N_PAIRS = N_HEADS // 2
ATTN_DILATIONS = (1, 4, 16)
ATTN_RADIUS = 64
N_BUCKETS = 32
REL_MAX_DIST = 1024
ROPE_BASE = 10000.0
N_GROUPS = 4
EXPERTS_PER_GROUP = 4
N_EXPERTS = 16
EXPERT_FF = 512
EPS = 1e-6
NEG_INF = -1e30
LOG2_E = math.log2(math.e)

LANES = 128
ATTN_TQ = 128
ATTN_W = 256
N_BIAS_VARIANTS = 8
RET_CHUNK = 256
ROUTE_TM = 512
ROUTE_TILES_PER_STEP = 2
SORT_ROWS = 640
CHUNK = 16
MOE_TM = 512
XS_W = D_MODEL + 2 * LANES
VMEM_LIMIT = 48 * 1024 * 1024


def _rms(x, gain):
    return x * lax.rsqrt(jnp.mean(x * x, axis=-1, keepdims=True) + EPS) * gain


def _inproj_kernel(x_ref, gain_ref, w_ref, cos_ref, sin_ref,
                   aq_ref, ak_ref, av_ref, rq_ref, rk_ref, rv_ref, rg_ref,
                   aq4_ref, ak4_ref, av4_ref, aq16_ref, ak16_ref, av16_ref, stage_ref, stage4_ref):
    tm = x_ref.shape[0]
    xn = _rms(x_ref[...], gain_ref[...]).astype(BF16)

    def emit(i, t, nat_ref, d4_ref, d16_ref):
        nat_ref[...] = t.astype(BF16)
        for hp in range(N_PAIRS):
            stage_ref[i, hp] = t[:, hp * PAIR:(hp + 1) * PAIR]
            for r4 in range(4):
                g4 = stage_ref[i, hp, pl.ds(r4, tm // 4, stride=4), :]
                d4_ref[hp, r4] = g4.astype(BF16)
                stage4_ref[i, hp, r4] = g4
                for j in range(4):
                    d16_ref[hp, r4 + 4 * j] = (
                        stage4_ref[i, hp, r4, pl.ds(j, tm // 16, stride=4), :].astype(BF16))

    def seg(i):
        return jnp.dot(xn, w_ref[:, i * 512:(i + 1) * 512], preferred_element_type=F32)

    def rotary(t):
        cos, sin = cos_ref[...], sin_ref[...]
        first_half = (lax.broadcasted_iota(jnp.int32, (1, LANES), 1) % HEAD_DIM) < HEAD_DIM // 2
        outs = []
        for j in range(t.shape[1] // LANES):
            tj = t[:, j * LANES:(j + 1) * LANES]
            partner = jnp.where(first_half, pltpu.roll(tj, LANES - 32, 1), pltpu.roll(tj, 32, 1))
            outs.append(tj * cos + partner * sin)
        return jnp.concatenate(outs, axis=1)

    emit(0, seg(0) * (HEAD_DIM ** -0.5 * LOG2_E), aq_ref, aq4_ref, aq16_ref)
    emit(1, seg(1), ak_ref, ak4_ref, ak16_ref)
    emit(2, seg(2), av_ref, av4_ref, av16_ref)
    rq_ref[...] = rotary(seg(3)).astype(BF16)
    rk_ref[...] = (rotary(seg(4)) * (HEAD_DIM ** -0.5)).astype(BF16)
    rv_ref[...] = seg(5).astype(BF16)
    rg_ref[...] = seg(6).astype(BF16)


def _inproj(x, gain, w_in, cos_t, sin_t, tm=512):
    b, s, d = x.shape
    n = w_in.shape[1]
    out = jax.ShapeDtypeStruct((b, s, 512), BF16)
    ospec = pl.BlockSpec((None, tm, 512), lambda si, bi: (bi, si, 0))

    def grouped(dil):
        shape = jax.ShapeDtypeStruct((b, N_PAIRS, dil, s // dil, PAIR), BF16)
        spec = pl.BlockSpec((None, N_PAIRS, dil, tm // dil, PAIR), lambda si, bi: (bi, 0, 0, si, 0))
        return [shape] * 3, [spec] * 3

    shapes4, specs4 = grouped(4)
    shapes16, specs16 = grouped(16)
    outs = pl.pallas_call(
        _inproj_kernel,
        grid=(s // tm, b),
        in_specs=[
            pl.BlockSpec((None, tm, d), lambda si, bi: (bi, si, 0)),
            pl.BlockSpec((1, d), lambda si, bi: (0, 0)),
            pl.BlockSpec((d, n), lambda si, bi: (0, 0)),
            pl.BlockSpec((tm, LANES), lambda si, bi: (si, 0)),
            pl.BlockSpec((tm, LANES), lambda si, bi: (si, 0)),
        ],
        out_specs=[ospec] * 7 + specs4 + specs16,
        out_shape=[out] * 7 + shapes4 + shapes16,
        scratch_shapes=[pltpu.VMEM((3, N_PAIRS, tm, PAIR), F32),
                        pltpu.VMEM((3, N_PAIRS, 4, tm // 4, PAIR), F32)],
        compiler_params=pltpu.CompilerParams(
            dimension_semantics=("arbitrary", "arbitrary"), vmem_limit_bytes=VMEM_LIMIT),
        name="inproj",
    )(x, gain, w_in, cos_t, sin_t)
    return outs[:7], [o.reshape(b, N_PAIRS, s, PAIR) for o in outs[7:]]


def _t5_bucket(rel):
    half = N_BUCKETS // 2
    max_exact = half // 2
    offset = jnp.where(rel > 0, half, 0)
    n = jnp.abs(rel)
    nf = jnp.maximum(n, 1).astype(F32)
    large = max_exact + (jnp.log(nf / max_exact) / math.log(REL_MAX_DIST / max_exact)
                         * (half - max_exact)).astype(jnp.int32)
    large = jnp.minimum(large, half - 1)
    return offset + jnp.where(n < max_exact, n, large)


def _attn_bias_rows(rel_bias):
    period = 2 * ATTN_W
    band = 2 * ATTN_RADIUS + 1
    rel = jnp.arange(-ATTN_RADIUS, ATTN_RADIUS + 1)
    rows = []
    for dil, offs in ((1, (0, 64, 128)), (4, (0, 64, 128)), (16, (0, 128))):
        vals = rel_bias[_t5_bucket(rel * dil)].astype(F32).T * LOG2_E
        for off in offs:
            lo = off - ATTN_RADIUS
            pad = jnp.full((N_HEADS, period - band), NEG_INF, F32)
            if lo >= 0:
                row = jnp.concatenate([pad[:, :lo], vals, pad[:, lo:]], axis=1)
            else:
                row = jnp.concatenate([vals[:, -lo:], pad, vals[:, :-lo]], axis=1)
            rows.append(row)
    v = jnp.stack(rows, axis=1)
    return v.reshape(N_PAIRS, 2 * N_BIAS_VARIANTS, period)


def _attention_kernel(q_ref, k_ref, v_ref, q4_ref, k4_ref, v4_ref, q16_ref, k16_ref, v16_ref,
                      rows_ref, o_ref, bias_ref, acc_ref, m_ref, l_ref, out_ref):
    s = q_ref.shape[0]
    n_tiles = s // ATTN_TQ
    lane = lax.broadcasted_iota(jnp.int32, (1, PAIR), 1)
    head0 = lane < HEAD_DIM

    @pl.when(pl.program_id(1) == 0)
    def _():
        col = lax.broadcasted_iota(jnp.int32, (ATTN_TQ, ATTN_W), 1)
        for idx in range(2 * N_BIAS_VARIANTS):
            gen = jnp.broadcast_to(rows_ref[idx:idx + 1, :], (ATTN_TQ, 2 * ATTN_W))
            tab = pltpu.roll(gen, 0, 1, stride=1, stride_axis=0)[:, :ATTN_W]
            var = idx % N_BIAS_VARIANTS
            if var >= 6:
                tab = jnp.where((col // ATTN_TQ) == var - 6, tab, NEG_INF)
            head = idx // N_BIAS_VARIANTS
            bias_ref[var, head * ATTN_TQ:(head + 1) * ATTN_TQ, :] = tab

    def run_branch(bi, dil, qs_ref, ks_ref, vs_ref):
        sub_len = s // dil
        tiles_per_sub = sub_len // ATTN_TQ

        def tile(t, carry):
            q0 = pl.multiple_of(t * ATTN_TQ, ATTN_TQ)
            if tiles_per_sub == 1:
                ws = pl.multiple_of((t // 2) * ATTN_W, ATTN_W)
                var = 6 + t % 2
            else:
                pos = t % tiles_per_sub
                sub_lo = (t // tiles_per_sub) * sub_len
                ws = jnp.clip(q0 - 64, sub_lo, sub_lo + sub_len - ATTN_W)
                ws = pl.multiple_of(ws, 64)
                var = jnp.where(pos == 0, 0, jnp.where(pos == tiles_per_sub - 1, 2, 1)) + 3 * bi
            q = qs_ref[pl.ds(q0, ATTN_TQ), :]
            k = ks_ref[pl.ds(ws, ATTN_W), :]
            v = vs_ref[pl.ds(ws, ATTN_W), :]
            q2 = jnp.concatenate([jnp.where(head0, q, jnp.zeros_like(q)),
                                  jnp.where(head0, jnp.zeros_like(q), q)], axis=0)
            sc = lax.dot_general(q2, k, (((1,), (1,)), ((), ())), preferred_element_type=F32)
            sc = sc + bias_ref[var]
            m = jnp.max(sc, axis=-1, keepdims=True)
            p = jnp.exp2(sc - m).astype(BF16)
            o = jnp.dot(p, jnp.concatenate([v, jnp.ones_like(v)], axis=1), preferred_element_type=F32)
            l = o[:, PAIR:]
            outs, ms, ls = (o[:ATTN_TQ, :PAIR], o[ATTN_TQ:, :PAIR]), (m[:ATTN_TQ], m[ATTN_TQ:]), (l[:ATTN_TQ], l[ATTN_TQ:])
            if dil == 1:
                dst = pl.ds(q0, ATTN_TQ)
            elif dil == 4:
                dst = pl.ds((t % 4) * (4 * ATTN_TQ) + t // 4, ATTN_TQ, stride=4)
            else:
                dst = pl.ds((t % 4) * (s // 4) + t // 4, ATTN_TQ, stride=4)
            acc_ref[bi, dst, :] = jnp.where(head0, outs[0], outs[1])
            m_ref[bi, dst, :] = jnp.where(head0, ms[0], ms[1])
            l_ref[bi, dst, :] = jnp.where(head0, ls[0], ls[1])
            return carry

        lax.fori_loop(0, n_tiles, tile, 0, unroll=16)

    run_branch(0, 1, q_ref, k_ref, v_ref)
    run_branch(1, 4, q4_ref, k4_ref, v4_ref)
    run_branch(2, 16, q16_ref, k16_ref, v16_ref)

    rows = ATTN_TQ
    for r4 in range(4):
        for blk in range(s // (4 * rows)):
            nat = pl.ds(r4 + 4 * rows * blk, rows, stride=4)
            sl = (nat, nat, pl.ds(r4 * (s // 4) + rows * blk, rows))
            m = [m_ref[bi, sl[bi], :] for bi in range(3)]
            mx = jnp.maximum(jnp.maximum(m[0], m[1]), m[2])
            num = jnp.zeros((rows, PAIR), F32)
            den = jnp.zeros((rows, PAIR), F32)
            for bi in range(3):
                e = jnp.exp2(m[bi] - mx)
                num = num + e * acc_ref[bi, sl[bi], :]
                den = den + e * l_ref[bi, sl[bi], :]
            out_ref[nat, :] = num / den
    o_ref[...] = out_ref[...].astype(o_ref.dtype)


def _attention(qkv, qkv_grouped, bias_rows):
    b, s, _ = qkv[0].shape
    spec = pl.BlockSpec((None, s, PAIR), lambda hp, bi: (bi, 0, hp))
    gspec = pl.BlockSpec((None, None, s, PAIR), lambda hp, bi: (bi, hp, 0, 0))
    return pl.pallas_call(
        _attention_kernel,
        grid=(N_PAIRS, b),
        in_specs=[spec] * 3 + [gspec] * 6 + [
            pl.BlockSpec((None, 2 * N_BIAS_VARIANTS, 2 * ATTN_W), lambda hp, bi: (hp, 0, 0))],
        out_specs=spec,
        out_shape=jax.ShapeDtypeStruct((b, s, ATTN_WIDTH), BF16),
        scratch_shapes=[
            pltpu.VMEM((N_BIAS_VARIANTS, 2 * ATTN_TQ, ATTN_W), F32),
            pltpu.VMEM((3, s, PAIR), F32),
            pltpu.VMEM((3, s, PAIR), F32),
            pltpu.VMEM((3, s, PAIR), F32),
            pltpu.VMEM((s, PAIR), F32),
        ],
        compiler_params=pltpu.CompilerParams(
            dimension_semantics=("arbitrary", "arbitrary"), vmem_limit_bytes=VMEM_LIMIT),
        name="attention",
    )(*qkv, *qkv_grouped, bias_rows)


def _retention_tables(decay_fwd, decay_bwd):
    c = RET_CHUNK
    lg_f = -jnp.exp(decay_fwd.astype(F32))
    lg_b = -jnp.exp(decay_bwd.astype(F32))
    idx = jnp.arange(c, dtype=F32)
    rel = idx[:, None] - idx[None, :]
    dmat = jnp.where(rel >= 0,
                     jnp.exp(lg_f[:, None, None] * jnp.maximum(rel, 0.0)[None]),
                     jnp.exp(lg_b[:, None, None] * jnp.maximum(-rel, 0.0)[None]))
    dmat = dmat.reshape(N_PAIRS, 2, c, c)

    def lanes(v):
        v = v.reshape(N_PAIRS, 2, -1)
        return jnp.repeat(jnp.transpose(v, (0, 2, 1)), HEAD_DIM, axis=2)

    vec = jnp.stack([
        lanes(jnp.exp(lg_f[:, None] * (idx + 1.0)[None])),
        lanes(jnp.exp(lg_f[:, None] * (c - 1.0 - idx)[None])),
        lanes(jnp.exp(lg_b[:, None] * (c - idx)[None])),
        lanes(jnp.exp(lg_b[:, None] * idx[None])),
    ], axis=1)
    same_head = (jnp.arange(PAIR)[:, None] // HEAD_DIM) == (jnp.arange(PAIR)[None, :] // HEAD_DIM)
    cd = jnp.stack([lanes(jnp.exp(lg_f * c)[:, None]), lanes(jnp.exp(lg_b * c)[:, None])], axis=1)
    cd = jnp.transpose(cd, (0, 1, 3, 2)) * same_head[None, None].astype(F32)
    return dmat, vec, cd


def _retention_kernel(q_ref, k_ref, v_ref, g_ref, dmat_ref, vec_ref, cd_ref, o_ref, y_ref, kv_ref):
    s = q_ref.shape[0]
    c = RET_CHUNK
    nc = s // c
    lane = lax.broadcasted_iota(jnp.int32, (1, PAIR), 1)
    head0 = lane < HEAD_DIM
    same_head = ((lax.broadcasted_iota(jnp.int32, (PAIR, PAIR), 0) // HEAD_DIM)
                 == (lax.broadcasted_iota(jnp.int32, (PAIR, PAIR), 1) // HEAD_DIM))

    for n in range(nc):
        rows = slice(n * c, (n + 1) * c)
        q, k, v = q_ref[rows, :], k_ref[rows, :], v_ref[rows, :]
        outs = []
        for h in range(2):
            qh = jnp.where(head0 if h == 0 else ~head0, q, jnp.zeros_like(q))
            sc = lax.dot_general(qh, k, (((1,), (1,)), ((), ())), preferred_element_type=F32)
            outs.append(jnp.dot((sc * dmat_ref[h]).astype(BF16), v, preferred_element_type=F32))
        y_ref[rows, :] = jnp.where(head0, outs[0], outs[1])
        kf = k.astype(F32)
        kcat = jnp.concatenate([(kf * vec_ref[1]).astype(BF16), (kf * vec_ref[3]).astype(BF16)], axis=1)
        kv = lax.dot_general(kcat, v, (((0,), (0,)), ((), ())), preferred_element_type=F32)
        kv_ref[0, n] = jnp.where(same_head, kv[:PAIR], 0.0)
        kv_ref[1, n] = jnp.where(same_head, kv[PAIR:], 0.0)

    def cross(n, state, qdec):
        rows = slice(n * c, (n + 1) * c)
        qd = (q_ref[rows, :].astype(F32) * qdec).astype(BF16)
        y_ref[rows, :] += jnp.dot(qd, state.astype(BF16), preferred_element_type=F32)

    state = jnp.zeros((PAIR, PAIR), F32)
    for n in range(nc):
        cross(n, state, vec_ref[0])
        state = state * cd_ref[0] + kv_ref[0, n]
    state = jnp.zeros((PAIR, PAIR), F32)
    for n in reversed(range(nc)):
        cross(n, state, vec_ref[2])
        state = state * cd_ref[1] + kv_ref[1, n]

    y = y_ref[...]
    y2 = y * y
    ms0 = jnp.sum(jnp.where(head0, y2, 0.0), axis=-1, keepdims=True)
    ms1 = jnp.sum(jnp.where(head0, 0.0, y2), axis=-1, keepdims=True)
    ms = jnp.where(head0, ms0, ms1) * (1.0 / HEAD_DIM)
    g = g_ref[...].astype(F32)
    o_ref[...] = (y * lax.rsqrt(ms + EPS) * (g * jax.nn.sigmoid(g))).astype(o_ref.dtype)


def _retention(rq, rk, rv, rg, dmat, vec, cd):
    b, s, _ = rq.shape
    c = RET_CHUNK
    spec = pl.BlockSpec((None, s, PAIR), lambda hp, bi: (bi, 0, hp))
    return pl.pallas_call(
        _retention_kernel,
        grid=(N_PAIRS, b),
        in_specs=[spec, spec, spec, spec,
                  pl.BlockSpec((None, 2, c, c), lambda hp, bi: (hp, 0, 0, 0)),
                  pl.BlockSpec((None, 4, c, PAIR), lambda hp, bi: (hp, 0, 0, 0)),
                  pl.BlockSpec((None, 2, PAIR, PAIR), lambda hp, bi: (hp, 0, 0, 0))],
        out_specs=spec,
        out_shape=jax.ShapeDtypeStruct((b, s, RET_WIDTH), BF16),
        scratch_shapes=[pltpu.VMEM((s, PAIR), F32),
                        pltpu.VMEM((2, s // c, PAIR, PAIR), F32)],
        compiler_params=pltpu.CompilerParams(
            dimension_semantics=("arbitrary", "arbitrary"), vmem_limit_bytes=VMEM_LIMIT),
        name="retention",
    )(rq, rk, rv, rg, dmat, vec, cd)


def _route_tile(sub, attn_ref, ret_ref, x_ref, wo_ref, again_ref, fgain_ref,
                rw_ref, rb_ref, h_ref, xs_ref, dest_ref, cnt_ref):
    tm = ROUTE_TM
    rows = slice(sub * tm, (sub + 1) * tm)
    a = _rms(attn_ref[rows, :].astype(F32), again_ref[...]).astype(BF16)
    mixed = jnp.concatenate([a, ret_ref[rows, :]], axis=1)
    h = x_ref[rows, :] + jnp.dot(mixed, wo_ref[...], preferred_element_type=F32)
    h_ref[rows, :] = h.astype(h_ref.dtype)
    hn = _rms(h, fgain_ref[...]).astype(BF16)

    logits = jnp.dot(hn, rw_ref[...], preferred_element_type=F32) + rb_ref[...]
    lane = lax.broadcasted_iota(jnp.int32, logits.shape, 1)
    big = jnp.int32(LANES)

    def first_argmax(vals):
        top = jnp.max(vals, axis=-1, keepdims=True)
        return top, jnp.min(jnp.where(vals == top, lane, big), axis=-1, keepdims=True)

    gl = jnp.where((lane >= N_EXPERTS) & (lane < N_EXPERTS + N_GROUPS), logits, -jnp.inf)
    gmax, gidx = first_argmax(gl)
    p_group = 1.0 / jnp.sum(jnp.exp(gl - gmax), axis=-1, keepdims=True)
    grp = gidx - N_EXPERTS
    lo = grp * EXPERTS_PER_GROUP
    el = jnp.where((lane >= lo) & (lane < lo + EXPERTS_PER_GROUP), logits, -jnp.inf)
    v1, i1 = first_argmax(el)
    v2, i2 = first_argmax(jnp.where(lane == i1, -jnp.inf, el))
    e2 = jnp.exp(v2 - v1)
    p1 = p_group / (1.0 + e2)
    gates = jnp.where(lane == i1, p1, jnp.where(lane == i2, p1 * e2, 0.0))

    onehot = jnp.where(lane == grp, 1.0, 0.0)
    r_i = lax.broadcasted_iota(jnp.int32, (tm, tm), 0)
    c_i = lax.broadcasted_iota(jnp.int32, (tm, tm), 1)
    before = jnp.where(c_i < r_i, 1.0, 0.0).astype(BF16)
    rank = jnp.dot(before, onehot.astype(BF16), preferred_element_type=F32)
    cnt = jnp.sum(onehot, axis=0, keepdims=True)
    c16 = jnp.broadcast_to(jnp.ceil(cnt * (1.0 / CHUNK)) * CHUNK, (8, LANES))
    lane8 = lax.broadcasted_iota(jnp.int32, (8, LANES), 1)
    start = jnp.zeros((8, LANES), F32)
    for sft in range(1, N_GROUPS):
        start = start + jnp.where(lane8 >= sft, pltpu.roll(c16, sft, 1), 0.0)
    dest = jnp.sum(onehot * (start[0:1] + rank), axis=-1, keepdims=True)
    dest_ref[rows, :] = jnp.broadcast_to(dest, (tm, LANES))
    cnt_ref[sub] = jnp.where(lane8 < N_GROUPS, c16, pltpu.roll(start, N_GROUPS, 1)).astype(jnp.int32)

    onehot_t = onehot.T
    rank_t = lax.dot_general(onehot_t.astype(BF16), before, (((1,), (1,)), ((), ())),
                             preferred_element_type=F32)
    g_row = lax.broadcasted_iota(jnp.int32, (LANES, LANES), 0)
    g_col = lax.broadcasted_iota(jnp.int32, (LANES, LANES), 1)
    start_t = jnp.sum(jnp.where(g_col < g_row, c16[0:1], 0.0), axis=-1, keepdims=True)
    dest_t = jnp.sum(onehot_t * (start_t + rank_t), axis=0, keepdims=True)
    perm = jnp.where(lax.broadcasted_iota(jnp.int32, (SORT_ROWS, tm), 0) == dest_t.astype(jnp.int32),
                     1.0, 0.0).astype(BF16)
    g_hi = gates.astype(BF16)
    g_lo = (gates - g_hi.astype(F32)).astype(BF16)
    payload = jnp.concatenate([hn, g_hi, g_lo], axis=1)
    xs_ref[sub * SORT_ROWS:(sub + 1) * SORT_ROWS, :] = (
        jnp.dot(perm, payload, preferred_element_type=F32).astype(BF16))


def _outproj_kernel(*refs):
    for sub in range(ROUTE_TILES_PER_STEP):
        _route_tile(sub, *refs)


def _outproj(attn, ret, x, w_o, again, fgain, rw, rb):
    t, d = x.shape
    per_step = ROUTE_TILES_PER_STEP
    tm = ROUTE_TM * per_step
    nt = t // ROUTE_TM
    row = lambda w: pl.BlockSpec((tm, w), lambda i: (i, 0))
    full = lambda r, c: pl.BlockSpec((r, c), lambda i: (0, 0))
    return pl.pallas_call(
        _outproj_kernel,
        grid=(nt // per_step,),
        in_specs=[row(ATTN_WIDTH), row(RET_WIDTH), row(d), full(ATTN_WIDTH + RET_WIDTH, d),
                  full(1, ATTN_WIDTH), full(1, d), full(d, LANES), full(1, LANES)],
        out_specs=[row(d), pl.BlockSpec((per_step * SORT_ROWS, XS_W), lambda i: (i, 0)), row(LANES),
                   pl.BlockSpec((per_step, 8, LANES), lambda i: (i, 0, 0))],
        out_shape=[jax.ShapeDtypeStruct((t, d), BF16),
                   jax.ShapeDtypeStruct((nt * SORT_ROWS, XS_W), BF16),
                   jax.ShapeDtypeStruct((t, LANES), F32),
                   jax.ShapeDtypeStruct((nt, 8, LANES), jnp.int32)],
        compiler_params=pltpu.CompilerParams(
            dimension_semantics=("arbitrary",), vmem_limit_bytes=VMEM_LIMIT),
        name="outproj",
    )(attn, ret, x, w_o, again, fgain, rw, rb)


def _dispatch_tables(cnt, n_moe_tiles):
    nt = cnt.shape[0]
    cpt = MOE_TM // CHUNK

    def schedule_kernel(seg_ref, grp_ref, used_ref, src_ref, tail_ref):
        pos = jnp.int32(0)
        for g in range(N_GROUPS):
            def tile_body(i, p):
                n = seg_ref[i, g] // CHUNK
                first = (i * SORT_ROWS + seg_ref[i, N_GROUPS + g]) // CHUNK

                def chunk_body(j, carry):
                    src_ref[p + j] = first + j
                    return carry
                lax.fori_loop(0, n, chunk_body, 0)
                return p + n
            end = lax.fori_loop(0, nt, tile_body, pos)
            padded = ((end + cpt - 1) // cpt) * cpt

            def pad_body(k, carry):
                src_ref[k] = -1
                return carry
            lax.fori_loop(end, padded, pad_body, 0)

            def mark_body(m, carry):
                grp_ref[m] = g
                used_ref[m] = 1
                return carry
            lax.fori_loop(pos // cpt, padded // cpt, mark_body, 0)
            pos = padded

        def idle_body(m, carry):
            grp_ref[m] = N_GROUPS - 1
            used_ref[m] = 0
            return carry
        lax.fori_loop(pos // cpt, n_moe_tiles, idle_body, 0)
        lax.fori_loop(pos, n_moe_tiles * cpt, pad_body, 0)

        def tail_body(i, carry):
            rows = seg_ref[i, 0]
            for g in range(1, N_GROUPS):
                rows = rows + seg_ref[i, g]
            tail_ref[i] = rows // CHUNK
            return carry
        lax.fori_loop(0, nt, tail_body, 0)

    smem = lambda: pl.BlockSpec(memory_space=pltpu.SMEM)
    return pl.pallas_call(
        schedule_kernel,
        in_specs=[smem()],
        out_specs=[smem(), smem(), smem(), smem()],
        out_shape=[jax.ShapeDtypeStruct((n_moe_tiles,), jnp.int32),
                   jax.ShapeDtypeStruct((n_moe_tiles,), jnp.int32),
                   jax.ShapeDtypeStruct((n_moe_tiles * cpt,), jnp.int32),
                   jax.ShapeDtypeStruct((nt,), jnp.int32)],
        name="schedule",
    )(cnt[:, 0, :2 * N_GROUPS])


def _moe_kernel(grp_ref, used_ref, src_ref, tail_ref, xs_hbm, w1_ref, w3_ref, w2_ref, ys_hbm,
                xbuf, obuf, zbuf, in_sem, out_sem, zero_sem):
    m = pl.program_id(0)
    n_tiles = pl.num_programs(0)
    cpt = MOE_TM // CHUNK
    slot = m % 2

    def rows(c):
        return pl.ds(pl.multiple_of(c * CHUNK, CHUNK), CHUNK)

    def zero_tails(wait):
        cps = SORT_ROWS // CHUNK

        def tile_body(i, carry):
            def body(c, carry2):
                cp = pltpu.make_async_copy(zbuf, ys_hbm.at[rows(i * cps + c), :], zero_sem)
                cp.wait() if wait else cp.start()
                return carry2
            lax.fori_loop(tail_ref[i], cps, body, 0)
            return carry
        lax.fori_loop(0, tail_ref.shape[0], tile_body, 0)

    def gather(tile, sl, wait):
        def body(c, carry):
            src = src_ref[tile * cpt + c]

            @pl.when(src >= 0)
            def _():
                cp = pltpu.make_async_copy(xs_hbm.at[rows(src), :], xbuf.at[sl, rows(c), :], in_sem.at[sl])
                cp.wait() if wait else cp.start()

            if not wait:
                @pl.when(src < 0)
                def _():
                    xbuf[sl, rows(c), :] = jnp.zeros((CHUNK, XS_W), BF16)
            return carry
        lax.fori_loop(0, cpt, body, 0)

    def scatter(tile, sl, wait):
        def body(c, carry):
            src = src_ref[tile * cpt + c]

            @pl.when(src >= 0)
            def _():
                cp = pltpu.make_async_copy(obuf.at[sl, rows(c), :], ys_hbm.at[rows(src), :], out_sem.at[sl])
                cp.wait() if wait else cp.start()
            return carry
        lax.fori_loop(0, cpt, body, 0)

    @pl.when(m == 0)
    def _():
        gather(0, 0, False)
        zbuf[...] = jnp.zeros(zbuf.shape, zbuf.dtype)
        zero_tails(False)

    @pl.when(m + 1 < n_tiles)
    def _():
        gather(m + 1, 1 - slot, False)

    gather(m, slot, True)

    @pl.when(m >= 2)
    def _():
        scatter(m - 2, slot, True)

    @pl.when(used_ref[m] > 0)
    def _():
        x = xbuf[slot, :, :D_MODEL]
        gate = (xbuf[slot, :, D_MODEL:D_MODEL + LANES].astype(F32)
                + xbuf[slot, :, D_MODEL + LANES:].astype(F32))
        lane = lax.broadcasted_iota(jnp.int32, gate.shape, 1)
        base = grp_ref[m] * EXPERTS_PER_GROUP
        acc = jnp.zeros((MOE_TM, D_MODEL), F32)
        for j in range(EXPERTS_PER_GROUP):
            a = jnp.dot(x, w1_ref[j], preferred_element_type=F32)
            b = jnp.dot(x, w3_ref[j], preferred_element_type=F32)
            gj = jnp.sum(jnp.where(lane == base + j, gate, 0.0), axis=-1, keepdims=True)
            hid = (a * jax.nn.sigmoid(a) * b * gj).astype(BF16)
            acc = acc + jnp.dot(hid, w2_ref[j], preferred_element_type=F32)
        obuf[slot] = acc.astype(BF16)

    scatter(m, slot, False)

    @pl.when(m == n_tiles - 1)
    def _():
        zero_tails(True)
        scatter(m, slot, True)

        @pl.when(m >= 1)
        def _():
            scatter(m - 1, 1 - slot, True)


def _moe(xs, w1, w3, w2, tile_group, tile_used, src_chunk, tile_tail):
    n_moe_tiles = tile_group.shape[0]
    rows = xs.shape[0]
    d = D_MODEL
    wspec = lambda r, c: pl.BlockSpec((EXPERTS_PER_GROUP, r, c), lambda m, grp, *_: (grp[m], 0, 0))
    return pl.pallas_call(
        _moe_kernel,
        grid_spec=pltpu.PrefetchScalarGridSpec(
            num_scalar_prefetch=4,
            grid=(n_moe_tiles,),
            in_specs=[pl.BlockSpec(memory_space=pl.ANY),
                      wspec(d, EXPERT_FF), wspec(d, EXPERT_FF), wspec(EXPERT_FF, d)],
            out_specs=pl.BlockSpec(memory_space=pl.ANY),
            scratch_shapes=[pltpu.VMEM((2, MOE_TM, XS_W), BF16),
                            pltpu.VMEM((2, MOE_TM, d), BF16),
                            pltpu.VMEM((CHUNK, d), BF16),
                            pltpu.SemaphoreType.DMA((2,)),
                            pltpu.SemaphoreType.DMA((2,)),
                            pltpu.SemaphoreType.DMA(())]),
        out_shape=jax.ShapeDtypeStruct((rows, d), BF16),
        compiler_params=pltpu.CompilerParams(
            dimension_semantics=("arbitrary",), vmem_limit_bytes=VMEM_LIMIT),
        name="moe",
    )(tile_group, tile_used, src_chunk, tile_tail, xs, w1, w3, w2)


def _combine_kernel(ys_ref, h_ref, dest_ref, gain_ref, o_ref):
    tm = h_ref.shape[0]
    dest = dest_ref[:, 0:1].astype(jnp.int32)
    perm_t = jnp.where(lax.broadcasted_iota(jnp.int32, (tm, SORT_ROWS), 1) == dest, 1.0, 0.0).astype(BF16)
    moe = jnp.dot(perm_t, ys_ref[...], preferred_element_type=F32)
    o_ref[...] = _rms(h_ref[...].astype(F32) + moe, gain_ref[...])


def _combine(ys, h, dest, gain):
    t, d = h.shape
    tm = ROUTE_TM
    return pl.pallas_call(
        _combine_kernel,
        grid=(t // tm,),
        in_specs=[pl.BlockSpec((SORT_ROWS, d), lambda i: (i, 0)),
                  pl.BlockSpec((tm, d), lambda i: (i, 0)),
                  pl.BlockSpec((tm, LANES), lambda i: (i, 0)),
                  pl.BlockSpec((1, d), lambda i: (0, 0))],
        out_specs=pl.BlockSpec((tm, d), lambda i: (i, 0)),
        out_shape=jax.ShapeDtypeStruct((t, d), F32),
        compiler_params=pltpu.CompilerParams(
            dimension_semantics=("arbitrary",), vmem_limit_bytes=VMEM_LIMIT),
        name="combine",
    )(ys, h, dest, gain)


def _rotary_tables(s):
    half = HEAD_DIM // 2
    inv = ROPE_BASE ** (-jnp.arange(half, dtype=F32) / half)
    ang = jnp.arange(s, dtype=F32)[:, None] * inv[None, :]
    cos, sin = jnp.cos(ang), jnp.sin(ang)
    cos_t = jnp.tile(jnp.concatenate([cos, cos], axis=-1), (1, LANES // HEAD_DIM))
    sin_t = jnp.tile(jnp.concatenate([-sin, sin], axis=-1), (1, LANES // HEAD_DIM))
    return cos_t, sin_t


def kernel(x, w_in, w_out, norm_mix, norm_ffn, norm_final, attn_out_gain, rel_bias, ret_decay_fwd, ret_decay_bwd, router_group_w, router_group_b, router_expert_w, router_expert_b, expert_w1, expert_w3, expert_w2):
    b, s, d = x.shape
    depth = w_in.shape[0]
    cos_t, sin_t = _rotary_tables(s)
    bias_rows = _attn_bias_rows(rel_bias)
    h = x
    for layer in range(depth):
        (aq, ak, av, rq, rk, rv, rg), qkv_grouped = _inproj(
            h, norm_mix[layer][None], w_in[layer].astype(BF16), cos_t, sin_t)
        attn = _attention((aq, ak, av), qkv_grouped, bias_rows)
        ret = _retention(rq, rk, rv, rg, *_retention_tables(ret_decay_fwd[layer], ret_decay_bwd[layer]))

        w_o = w_out[layer].astype(BF16)
        rw = jnp.concatenate(
            [jnp.transpose(router_expert_w[layer], (1, 0, 2)).reshape(d, N_EXPERTS),
             router_group_w[layer],
             jnp.zeros((d, LANES - N_EXPERTS - N_GROUPS), F32)], axis=1).astype(BF16)
        rb = jnp.concatenate(
            [router_expert_b[layer].reshape(N_EXPERTS), router_group_b[layer],
             jnp.zeros((LANES - N_EXPERTS - N_GROUPS,), F32)])[None].astype(F32)
        h1, xs, dest, cnt = _outproj(
            attn.reshape(b * s, ATTN_WIDTH), ret.reshape(b * s, RET_WIDTH), h.reshape(b * s, d),
            w_o, attn_out_gain[layer][None], norm_ffn[layer][None],
            rw, rb)
        n_route_tiles = (b * s) // ROUTE_TM
        n_moe_tiles = (b * s + n_route_tiles * N_GROUPS * (CHUNK - 1)) // MOE_TM + N_GROUPS
        schedule = _dispatch_tables(cnt, n_moe_tiles)
        ys = _moe(xs, expert_w1[layer].astype(BF16), expert_w3[layer].astype(BF16),
                  expert_w2[layer].astype(BF16), *schedule)
        assert depth == 1, "the combine kernel fuses the final norm, so it must run on the last layer"
        h = _combine(ys, h1, dest, norm_final[None]).reshape(b, s, d)
    return h
```

```python
import functools
import math

import jax
import jax.numpy as jnp
from jax import lax
from jax.experimental import pallas as pl
from jax.experimental.pallas import tpu as pltpu

F32 = jnp.float32
BF16 = jnp.bfloat16

D_MODEL = 1024
HEAD_DIM = 64
ATTN_WIDTH = 512
RET_WIDTH = 512
N_HEADS = 8
PAIR = 2 * HEAD_DIM
N_PAIRS = N_HEADS // 2
ATTN_DILATIONS = (1, 4, 16)
ATTN_RADIUS = 64
N_BUCKETS = 32
REL_MAX_DIST = 1024
ROPE_BASE = 10000.0
N_GROUPS = 4
EXPERTS_PER_GROUP = 4
N_EXPERTS = 16
EXPERT_FF = 512
EPS = 1e-6
NEG_INF = -1e30
LOG2_E = math.log2(math.e)

LANES = 128
ATTN_TQ = 128
ATTN_W = 256
N_BIAS_VARIANTS = 8
RET_CHUNK = 256
ROUTE_TM = 512
ROUTE_TILES_PER_STEP = 2
SORT_ROWS = 640
CHUNK = 16
MOE_TM = 512
XS_W = D_MODEL + 2 * LANES
VMEM_LIMIT = 48 * 1024 * 1024


def _rms(x, gain):
    return x * lax.rsqrt(jnp.mean(x * x, axis=-1, keepdims=True) + EPS) * gain


def _inproj_kernel(x_ref, gain_ref, w_ref, cos_ref, sin_ref,
                   aq_ref, ak_ref, av_ref, rq_ref, rk_ref, rv_ref, rg_ref,
                   aq4_ref, ak4_ref, av4_ref, aq16_ref, ak16_ref, av16_ref, stage_ref, stage4_ref):
    tm = x_ref.shape[0]
    xn = _rms(x_ref[...], gain_ref[...]).astype(BF16)

    def emit(i, t, nat_ref, d4_ref, d16_ref):
        nat_ref[...] = t.astype(BF16)
        for hp in range(N_PAIRS):
            stage_ref[i, hp] = t[:, hp * PAIR:(hp + 1) * PAIR]
            for r4 in range(4):
                g4 = stage_ref[i, hp, pl.ds(r4, tm // 4, stride=4), :]
                d4_ref[hp, r4] = g4.astype(BF16)
                stage4_ref[i, hp, r4] = g4
                for j in range(4):
                    d16_ref[hp, r4 + 4 * j] = (
                        stage4_ref[i, hp, r4, pl.ds(j, tm // 16, stride=4), :].astype(BF16))

    def seg(i):
        return jnp.dot(xn, w_ref[:, i * 512:(i + 1) * 512], preferred_element_type=F32)

    def rotary(t):
        cos, sin = cos_ref[...], sin_ref[...]
        first_half = (lax.broadcasted_iota(jnp.int32, (1, LANES), 1) % HEAD_DIM) < HEAD_DIM // 2
        outs = []
        for j in range(t.shape[1] // LANES):
            tj = t[:, j * LANES:(j + 1) * LANES]
            partner = jnp.where(first_half, pltpu.roll(tj, LANES - 32, 1), pltpu.roll(tj, 32, 1))
            outs.append(tj * cos + partner * sin)
        return jnp.concatenate(outs, axis=1)

    emit(0, seg(0) * (HEAD_DIM ** -0.5 * LOG2_E), aq_ref, aq4_ref, aq16_ref)
    emit(1, seg(1), ak_ref, ak4_ref, ak16_ref)
    emit(2, seg(2), av_ref, av4_ref, av16_ref)
    rq_ref[...] = rotary(seg(3)).astype(BF16)
    rk_ref[...] = (rotary(seg(4)) * (HEAD_DIM ** -0.5)).astype(BF16)
    rv_ref[...] = seg(5).astype(BF16)
    rg_ref[...] = seg(6).astype(BF16)


def _inproj(x, gain, w_in, cos_t, sin_t, tm=512):
    b, s, d = x.shape
    n = w_in.shape[1]
    out = jax.ShapeDtypeStruct((b, s, 512), BF16)
    ospec = pl.BlockSpec((None, tm, 512), lambda si, bi: (bi, si, 0))

    def grouped(dil):
        shape = jax.ShapeDtypeStruct((b, N_PAIRS, dil, s // dil, PAIR), BF16)
        spec = pl.BlockSpec((None, N_PAIRS, dil, tm // dil, PAIR), lambda si, bi: (bi, 0, 0, si, 0))
        return [shape] * 3, [spec] * 3

    shapes4, specs4 = grouped(4)
    shapes16, specs16 = grouped(16)
    outs = pl.pallas_call(
        _inproj_kernel,
        grid=(s // tm, b),
        in_specs=[
            pl.BlockSpec((None, tm, d), lambda si, bi: (bi, si, 0)),
            pl.BlockSpec((1, d), lambda si, bi: (0, 0)),
            pl.BlockSpec((d, n), lambda si, bi: (0, 0)),
            pl.BlockSpec((tm, LANES), lambda si, bi: (si, 0)),
            pl.BlockSpec((tm, LANES), lambda si, bi: (si, 0)),
        ],
        out_specs=[ospec] * 7 + specs4 + specs16,
        out_shape=[out] * 7 + shapes4 + shapes16,
        scratch_shapes=[pltpu.VMEM((3, N_PAIRS, tm, PAIR), F32),
                        pltpu.VMEM((3, N_PAIRS, 4, tm // 4, PAIR), F32)],
        compiler_params=pltpu.CompilerParams(
            dimension_semantics=("arbitrary", "arbitrary"), vmem_limit_bytes=VMEM_LIMIT),
        name="inproj",
    )(x, gain, w_in, cos_t, sin_t)
    return outs[:7], [o.reshape(b, N_PAIRS, s, PAIR) for o in outs[7:]]


def _t5_bucket(rel):
    half = N_BUCKETS // 2
    max_exact = half // 2
    offset = jnp.where(rel > 0, half, 0)
    n = jnp.abs(rel)
    nf = jnp.maximum(n, 1).astype(F32)
    large = max_exact + (jnp.log(nf / max_exact) / math.log(REL_MAX_DIST / max_exact)
                         * (half - max_exact)).astype(jnp.int32)
    large = jnp.minimum(large, half - 1)
    return offset + jnp.where(n < max_exact, n, large)


def _attn_bias_rows(rel_bias):
    period = 2 * ATTN_W
    band = 2 * ATTN_RADIUS + 1
    rel = jnp.arange(-ATTN_RADIUS, ATTN_RADIUS + 1)
    rows = []
    for dil, offs in ((1, (0, 64, 128)), (4, (0, 64, 128)), (16, (0, 128))):
        vals = rel_bias[_t5_bucket(rel * dil)].astype(F32).T * LOG2_E
        for off in offs:
            lo = off - ATTN_RADIUS
            pad = jnp.full((N_HEADS, period - band), NEG_INF, F32)
            if lo >= 0:
                row = jnp.concatenate([pad[:, :lo], vals, pad[:, lo:]], axis=1)
            else:
                row = jnp.concatenate([vals[:, -lo:], pad, vals[:, :-lo]], axis=1)
            rows.append(row)
    v = jnp.stack(rows, axis=1)
    return v.reshape(N_PAIRS, 2 * N_BIAS_VARIANTS, period)


def _attention_kernel(q_ref, k_ref, v_ref, q4_ref, k4_ref, v4_ref, q16_ref, k16_ref, v16_ref,
                      rows_ref, o_ref, bias_ref, acc_ref, m_ref, l_ref, out_ref):
    s = q_ref.shape[0]
    n_tiles = s // ATTN_TQ
    lane = lax.broadcasted_iota(jnp.int32, (1, PAIR), 1)
    head0 = lane < HEAD_DIM

    @pl.when(pl.program_id(1) == 0)
    def _():
        col = lax.broadcasted_iota(jnp.int32, (ATTN_TQ, ATTN_W), 1)
        for idx in range(2 * N_BIAS_VARIANTS):
            gen = jnp.broadcast_to(rows_ref[idx:idx + 1, :], (ATTN_TQ, 2 * ATTN_W))
            tab = pltpu.roll(gen, 0, 1, stride=1, stride_axis=0)[:, :ATTN_W]
            var = idx % N_BIAS_VARIANTS
            if var >= 6:
                tab = jnp.where((col // ATTN_TQ) == var - 6, tab, NEG_INF)
            head = idx // N_BIAS_VARIANTS
            bias_ref[var, head * ATTN_TQ:(head + 1) * ATTN_TQ, :] = tab

    def run_branch(bi, dil, qs_ref, ks_ref, vs_ref):
        sub_len = s // dil
        tiles_per_sub = sub_len // ATTN_TQ

        def tile(t, carry):
            q0 = pl.multiple_of(t * ATTN_TQ, ATTN_TQ)
            if tiles_per_sub == 1:
                ws = pl.multiple_of((t // 2) * ATTN_W, ATTN_W)
                var = 6 + t % 2
            else:
                pos = t % tiles_per_sub
                sub_lo = (t // tiles_per_sub) * sub_len
                ws = jnp.clip(q0 - 64, sub_lo, sub_lo + sub_len - ATTN_W)
                ws = pl.multiple_of(ws, 64)
                var = jnp.where(pos == 0, 0, jnp.where(pos == tiles_per_sub - 1, 2, 1)) + 3 * bi
            q = qs_ref[pl.ds(q0, ATTN_TQ), :]
            k = ks_ref[pl.ds(ws, ATTN_W), :]
            v = vs_ref[pl.ds(ws, ATTN_W), :]
            q2 = jnp.concatenate([jnp.where(head0, q, jnp.zeros_like(q)),
                                  jnp.where(head0, jnp.zeros_like(q), q)], axis=0)
            sc = lax.dot_general(q2, k, (((1,), (1,)), ((), ())), preferred_element_type=F32)
            sc = sc + bias_ref[var]
            m = jnp.max(sc, axis=-1, keepdims=True)
            p = jnp.exp2(sc - m).astype(BF16)
            o = jnp.dot(p, jnp.concatenate([v, jnp.ones_like(v)], axis=1), preferred_element_type=F32)
            l = o[:, PAIR:]
            outs, ms, ls = (o[:ATTN_TQ, :PAIR], o[ATTN_TQ:, :PAIR]), (m[:ATTN_TQ], m[ATTN_TQ:]), (l[:ATTN_TQ], l[ATTN_TQ:])
            if dil == 1:
                dst = pl.ds(q0, ATTN_TQ)
            elif dil == 4:
                dst = pl.ds((t % 4) * (4 * ATTN_TQ) + t // 4, ATTN_TQ, stride=4)
            else:
                dst = pl.ds((t % 4) * (s // 4) + t // 4, ATTN_TQ, stride=4)
            acc_ref[bi, dst, :] = jnp.where(head0, outs[0], outs[1])
            m_ref[bi, dst, :] = jnp.where(head0, ms[0], ms[1])
            l_ref[bi, dst, :] = jnp.where(head0, ls[0], ls[1])
            return carry

        lax.fori_loop(0, n_tiles, tile, 0, unroll=16)

    run_branch(0, 1, q_ref, k_ref, v_ref)
    run_branch(1, 4, q4_ref, k4_ref, v4_ref)
    run_branch(2, 16, q16_ref, k16_ref, v16_ref)

    rows = ATTN_TQ
    for r4 in range(4):
        for blk in range(s // (4 * rows)):
            nat = pl.ds(r4 + 4 * rows * blk, rows, stride=4)
            sl = (nat, nat, pl.ds(r4 * (s // 4) + rows * blk, rows))
            m = [m_ref[bi, sl[bi], :] for bi in range(3)]
            mx = jnp.maximum(jnp.maximum(m[0], m[1]), m[2])
            num = jnp.zeros((rows, PAIR), F32)
            den = jnp.zeros((rows, PAIR), F32)
            for bi in range(3):
                e = jnp.exp2(m[bi] - mx)
                num = num + e * acc_ref[bi, sl[bi], :]
                den = den + e * l_ref[bi, sl[bi], :]
            out_ref[nat, :] = num / den
    o_ref[...] = out_ref[...].astype(o_ref.dtype)


def _attention(qkv, qkv_grouped, bias_rows):
    b, s, _ = qkv[0].shape
    spec = pl.BlockSpec((None, s, PAIR), lambda hp, bi: (bi, 0, hp))
    gspec = pl.BlockSpec((None, None, s, PAIR), lambda hp, bi: (bi, hp, 0, 0))
    return pl.pallas_call(
        _attention_kernel,
        grid=(N_PAIRS, b),
        in_specs=[spec] * 3 + [gspec] * 6 + [
            pl.BlockSpec((None, 2 * N_BIAS_VARIANTS, 2 * ATTN_W), lambda hp, bi: (hp, 0, 0))],
        out_specs=spec,
        out_shape=jax.ShapeDtypeStruct((b, s, ATTN_WIDTH), BF16),
        scratch_shapes=[
            pltpu.VMEM((N_BIAS_VARIANTS, 2 * ATTN_TQ, ATTN_W), F32),
            pltpu.VMEM((3, s, PAIR), F32),
            pltpu.VMEM((3, s, PAIR), F32),
            pltpu.VMEM((3, s, PAIR), F32),
            pltpu.VMEM((s, PAIR), F32),
        ],
        compiler_params=pltpu.CompilerParams(
            dimension_semantics=("arbitrary", "arbitrary"), vmem_limit_bytes=VMEM_LIMIT),
        name="attention",
    )(*qkv, *qkv_grouped, bias_rows)


def _retention_tables(decay_fwd, decay_bwd):
    c = RET_CHUNK
    lg_f = -jnp.exp(decay_fwd.astype(F32))
    lg_b = -jnp.exp(decay_bwd.astype(F32))
    idx = jnp.arange(c, dtype=F32)
    rel = idx[:, None] - idx[None, :]
    dmat = jnp.where(rel >= 0,
                     jnp.exp(lg_f[:, None, None] * jnp.maximum(rel, 0.0)[None]),
                     jnp.exp(lg_b[:, None, None] * jnp.maximum(-rel, 0.0)[None]))
    dmat = dmat.reshape(N_PAIRS, 2, c, c)

    def lanes(v):
        v = v.reshape(N_PAIRS, 2, -1)
        return jnp.repeat(jnp.transpose(v, (0, 2, 1)), HEAD_DIM, axis=2)

    vec = jnp.stack([
        lanes(jnp.exp(lg_f[:, None] * (idx + 1.0)[None])),
        lanes(jnp.exp(lg_f[:, None] * (c - 1.0 - idx)[None])),
        lanes(jnp.exp(lg_b[:, None] * (c - idx)[None])),
        lanes(jnp.exp(lg_b[:, None] * idx[None])),
    ], axis=1)
    same_head = (jnp.arange(PAIR)[:, None] // HEAD_DIM) == (jnp.arange(PAIR)[None, :] // HEAD_DIM)
    cd = jnp.stack([lanes(jnp.exp(lg_f * c)[:, None]), lanes(jnp.exp(lg_b * c)[:, None])], axis=1)
    cd = jnp.transpose(cd, (0, 1, 3, 2)) * same_head[None, None].astype(F32)
    return dmat, vec, cd


def _retention_kernel(q_ref, k_ref, v_ref, g_ref, dmat_ref, vec_ref, cd_ref, o_ref, kv_ref, st_ref):
    s = q_ref.shape[0]
    c = RET_CHUNK
    nc = s // c
    lane = lax.broadcasted_iota(jnp.int32, (1, PAIR), 1)
    head0 = lane < HEAD_DIM
    same_head = ((lax.broadcasted_iota(jnp.int32, (PAIR, PAIR), 0) // HEAD_DIM)
                 == (lax.broadcasted_iota(jnp.int32, (PAIR, PAIR), 1) // HEAD_DIM))

    same_head2 = jnp.concatenate([same_head, same_head], axis=0)
    for n in range(nc):
        rows = slice(n * c, (n + 1) * c)
        kf = k_ref[rows, :].astype(F32)
        kcat = jnp.concatenate([(kf * vec_ref[1]).astype(BF16), (kf * vec_ref[3]).astype(BF16)], axis=1)
        kv = lax.dot_general(kcat, v_ref[rows, :], (((0,), (0,)), ((), ())), preferred_element_type=F32)
        kv_ref[n] = jnp.where(same_head2, kv, 0.0)

    state = jnp.zeros((PAIR, PAIR), F32)
    for n in range(nc):
        st_ref[n, :PAIR, :] = state.astype(BF16)
        state = state * cd_ref[0] + kv_ref[n, :PAIR, :]
    state = jnp.zeros((PAIR, PAIR), F32)
    for n in reversed(range(nc)):
        st_ref[n, PAIR:, :] = state.astype(BF16)
        state = state * cd_ref[1] + kv_ref[n, PAIR:, :]

    for n in range(nc):
        rows = slice(n * c, (n + 1) * c)
        q, k, v = q_ref[rows, :], k_ref[rows, :], v_ref[rows, :]
        outs = []
        for h in range(2):
            qh = jnp.where(head0 if h == 0 else ~head0, q, jnp.zeros_like(q))
            sc = lax.dot_general(qh, k, (((1,), (1,)), ((), ())), preferred_element_type=F32)
            outs.append(jnp.dot((sc * dmat_ref[h]).astype(BF16), v, preferred_element_type=F32))
        qf = q.astype(F32)
        qcat = jnp.concatenate([(qf * vec_ref[0]).astype(BF16), (qf * vec_ref[2]).astype(BF16)], axis=1)
        y = jnp.where(head0, outs[0], outs[1]) + jnp.dot(qcat, st_ref[n], preferred_element_type=F32)
        y2 = y * y
        ms0 = jnp.sum(jnp.where(head0, y2, 0.0), axis=-1, keepdims=True)
        ms1 = jnp.sum(jnp.where(head0, 0.0, y2), axis=-1, keepdims=True)
        ms = jnp.where(head0, ms0, ms1) * (1.0 / HEAD_DIM)
        g = g_ref[rows, :].astype(F32)
        o_ref[rows, :] = (y * lax.rsqrt(ms + EPS) * (g * jax.nn.sigmoid(g))).astype(o_ref.dtype)


def _retention(rq, rk, rv, rg, dmat, vec, cd):
    b, s, _ = rq.shape
    c = RET_CHUNK
    spec = pl.BlockSpec((None, s, PAIR), lambda hp, bi: (bi, 0, hp))
    return pl.pallas_call(
        _retention_kernel,
        grid=(N_PAIRS, b),
        in_specs=[spec, spec, spec, spec,
                  pl.BlockSpec((None, 2, c, c), lambda hp, bi: (hp, 0, 0, 0)),
                  pl.BlockSpec((None, 4, c, PAIR), lambda hp, bi: (hp, 0, 0, 0)),
                  pl.BlockSpec((None, 2, PAIR, PAIR), lambda hp, bi: (hp, 0, 0, 0))],
        out_specs=spec,
        out_shape=jax.ShapeDtypeStruct((b, s, RET_WIDTH), BF16),
        scratch_shapes=[pltpu.VMEM((s // c, 2 * PAIR, PAIR), F32),
                        pltpu.VMEM((s // c, 2 * PAIR, PAIR), BF16)],
        compiler_params=pltpu.CompilerParams(
            dimension_semantics=("arbitrary", "arbitrary"), vmem_limit_bytes=VMEM_LIMIT),
        name="retention",
    )(rq, rk, rv, rg, dmat, vec, cd)


def _route_tile(sub, attn_ref, ret_ref, x_ref, wo_ref, again_ref, fgain_ref,
                rw_ref, rb_ref, h_ref, xs_ref, dest_ref, cnt_ref):
    tm = ROUTE_TM
    rows = slice(sub * tm, (sub + 1) * tm)
    a = _rms(attn_ref[rows, :].astype(F32), again_ref[...]).astype(BF16)
    mixed = jnp.concatenate([a, ret_ref[rows, :]], axis=1)
    h = x_ref[rows, :] + jnp.dot(mixed, wo_ref[...], preferred_element_type=F32)
    h_ref[rows, :] = h.astype(h_ref.dtype)
    hn = _rms(h, fgain_ref[...]).astype(BF16)

    logits = jnp.dot(hn, rw_ref[...], preferred_element_type=F32) + rb_ref[...]
    lane = lax.broadcasted_iota(jnp.int32, logits.shape, 1)
    big = jnp.int32(LANES)

    def first_argmax(vals):
        top = jnp.max(vals, axis=-1, keepdims=True)
        return top, jnp.min(jnp.where(vals == top, lane, big), axis=-1, keepdims=True)

    gl = jnp.where((lane >= N_EXPERTS) & (lane < N_EXPERTS + N_GROUPS), logits, -jnp.inf)
    gmax, gidx = first_argmax(gl)
    p_group = 1.0 / jnp.sum(jnp.exp(gl - gmax), axis=-1, keepdims=True)
    grp = gidx - N_EXPERTS
    lo = grp * EXPERTS_PER_GROUP
    el = jnp.where((lane >= lo) & (lane < lo + EXPERTS_PER_GROUP), logits, -jnp.inf)
    v1, i1 = first_argmax(el)
    v2, i2 = first_argmax(jnp.where(lane == i1, -jnp.inf, el))
    e2 = jnp.exp(v2 - v1)
    p1 = p_group / (1.0 + e2)
    gates = jnp.where(lane == i1, p1, jnp.where(lane == i2, p1 * e2, 0.0))

    onehot = jnp.where(lane == grp, 1.0, 0.0)
    r_i = lax.broadcasted_iota(jnp.int32, (tm, tm), 0)
    c_i = lax.broadcasted_iota(jnp.int32, (tm, tm), 1)
    before = jnp.where(c_i < r_i, 1.0, 0.0).astype(BF16)
    rank = jnp.dot(before, onehot.astype(BF16), preferred_element_type=F32)
    cnt = jnp.sum(onehot, axis=0, keepdims=True)
    c16 = jnp.broadcast_to(jnp.ceil(cnt * (1.0 / CHUNK)) * CHUNK, (8, LANES))
    lane8 = lax.broadcasted_iota(jnp.int32, (8, LANES), 1)
    start = jnp.zeros((8, LANES), F32)
    for sft in range(1, N_GROUPS):
        start = start + jnp.where(lane8 >= sft, pltpu.roll(c16, sft, 1), 0.0)
    dest = jnp.sum(onehot * (start[0:1] + rank), axis=-1, keepdims=True)
    dest_ref[rows, :] = jnp.broadcast_to(dest, (tm, LANES))
    cnt_ref[sub] = jnp.where(lane8 < N_GROUPS, c16, pltpu.roll(start, N_GROUPS, 1)).astype(jnp.int32)

    onehot_t = onehot.T
    rank_t = lax.dot_general(onehot_t.astype(BF16), before, (((1,), (1,)), ((), ())),
                             preferred_element_type=F32)
    g_row = lax.broadcasted_iota(jnp.int32, (LANES, LANES), 0)
    g_col = lax.broadcasted_iota(jnp.int32, (LANES, LANES), 1)
    start_t = jnp.sum(jnp.where(g_col < g_row, c16[0:1], 0.0), axis=-1, keepdims=True)
    dest_t = jnp.sum(onehot_t * (start_t + rank_t), axis=0, keepdims=True)
    perm = jnp.where(lax.broadcasted_iota(jnp.int32, (SORT_ROWS, tm), 0) == dest_t.astype(jnp.int32),
                     1.0, 0.0).astype(BF16)
    g_hi = gates.astype(BF16)
    g_lo = (gates - g_hi.astype(F32)).astype(BF16)
    payload = jnp.concatenate([hn, g_hi, g_lo], axis=1)
    xs_ref[sub * SORT_ROWS:(sub + 1) * SORT_ROWS, :] = (
        jnp.dot(perm, payload, preferred_element_type=F32).astype(BF16))


def _outproj_kernel(*refs):
    for sub in range(ROUTE_TILES_PER_STEP):
        _route_tile(sub, *refs)


def _outproj(attn, ret, x, w_o, again, fgain, rw, rb):
    t, d = x.shape
    per_step = ROUTE_TILES_PER_STEP
    tm = ROUTE_TM * per_step
    nt = t // ROUTE_TM
    row = lambda w: pl.BlockSpec((tm, w), lambda i: (i, 0))
    full = lambda r, c: pl.BlockSpec((r, c), lambda i: (0, 0))
    return pl.pallas_call(
        _outproj_kernel,
        grid=(nt // per_step,),
        in_specs=[row(ATTN_WIDTH), row(RET_WIDTH), row(d), full(ATTN_WIDTH + RET_WIDTH, d),
                  full(1, ATTN_WIDTH), full(1, d), full(d, LANES), full(1, LANES)],
        out_specs=[row(d), pl.BlockSpec((per_step * SORT_ROWS, XS_W), lambda i: (i, 0)), row(LANES),
                   pl.BlockSpec((per_step, 8, LANES), lambda i: (i, 0, 0))],
        out_shape=[jax.ShapeDtypeStruct((t, d), BF16),
                   jax.ShapeDtypeStruct((nt * SORT_ROWS, XS_W), BF16),
                   jax.ShapeDtypeStruct((t, LANES), F32),
                   jax.ShapeDtypeStruct((nt, 8, LANES), jnp.int32)],
        compiler_params=pltpu.CompilerParams(
            dimension_semantics=("arbitrary",), vmem_limit_bytes=VMEM_LIMIT),
        name="outproj",
    )(attn, ret, x, w_o, again, fgain, rw, rb)


def _dispatch_tables(cnt, n_moe_tiles):
    nt = cnt.shape[0]
    cpt = MOE_TM // CHUNK

    n_slots = n_moe_tiles * cpt
    slot_rows = -(-n_slots // (8 * LANES)) * 8

    def schedule_kernel(seg_ref, grp_ref, used_ref, src_ref, tail_ref):
        slot = (lax.broadcasted_iota(jnp.int32, (slot_rows, LANES), 0) * LANES
                + lax.broadcasted_iota(jnp.int32, (slot_rows, LANES), 1))
        src = jnp.full((slot_rows, LANES), -1, jnp.int32)
        pos = jnp.int32(0)
        for g in range(N_GROUPS):
            def tile_body(i, carry):
                p, src = carry
                n = seg_ref[i, g] // CHUNK
                first = (i * SORT_ROWS + seg_ref[i, N_GROUPS + g]) // CHUNK
                src = jnp.where((slot >= p) & (slot < p + n), slot + (first - p), src)
                return p + n, src
            end, src = lax.fori_loop(0, nt, tile_body, (pos, src))
            padded = ((end + cpt - 1) // cpt) * cpt

            def mark_body(m, carry):
                grp_ref[m] = g
                used_ref[m] = 1
                return carry
            lax.fori_loop(pos // cpt, padded // cpt, mark_body, 0)
            pos = padded

        def idle_body(m, carry):
            grp_ref[m] = N_GROUPS - 1
            used_ref[m] = 0
            return carry
        lax.fori_loop(pos // cpt, n_moe_tiles, idle_body, 0)
        src_ref[...] = src

        def tail_body(i, carry):
            rows = seg_ref[i, 0]
            for g in range(1, N_GROUPS):
                rows = rows + seg_ref[i, g]
            tail_ref[i] = rows // CHUNK
            return carry
        lax.fori_loop(0, nt, tail_body, 0)

    smem = lambda: pl.BlockSpec(memory_space=pltpu.SMEM)
    tile_group, tile_used, src, tile_tail = pl.pallas_call(
        schedule_kernel,
        in_specs=[smem()],
        out_specs=[smem(), smem(), pl.BlockSpec(memory_space=pltpu.VMEM), smem()],
        out_shape=[jax.ShapeDtypeStruct((n_moe_tiles,), jnp.int32),
                   jax.ShapeDtypeStruct((n_moe_tiles,), jnp.int32),
                   jax.ShapeDtypeStruct((slot_rows, LANES), jnp.int32),
                   jax.ShapeDtypeStruct((nt,), jnp.int32)],
        name="schedule",
    )(cnt[:, 0, :2 * N_GROUPS])
    return tile_group, tile_used, src.reshape(-1)[:n_slots], tile_tail


def _moe_kernel(grp_ref, used_ref, src_ref, tail_ref, xs_hbm, w1_ref, w3_ref, w2_ref, ys_hbm,
                xbuf, obuf, zbuf, in_sem, out_sem, zero_sem):
    m = pl.program_id(0)
    n_tiles = pl.num_programs(0)
    cpt = MOE_TM // CHUNK
    slot = m % 2

    def rows(c):
        return pl.ds(pl.multiple_of(c * CHUNK, CHUNK), CHUNK)

    def zero_tails(wait):
        cps = SORT_ROWS // CHUNK

        def tile_body(i, carry):
            def body(c, carry2):
                cp = pltpu.make_async_copy(zbuf, ys_hbm.at[rows(i * cps + c), :], zero_sem)
                cp.wait() if wait else cp.start()
                return carry2
            lax.fori_loop(tail_ref[i], cps, body, 0)
            return carry
        lax.fori_loop(0, tail_ref.shape[0], tile_body, 0)

    def gather(tile, sl, wait):
        def body(c, carry):
            src = src_ref[tile * cpt + c]

            @pl.when(src >= 0)
            def _():
                cp = pltpu.make_async_copy(xs_hbm.at[rows(src), :], xbuf.at[sl, rows(c), :], in_sem.at[sl])
                cp.wait() if wait else cp.start()

            if not wait:
                @pl.when(src < 0)
                def _():
                    xbuf[sl, rows(c), :] = jnp.zeros((CHUNK, XS_W), BF16)
            return carry
        lax.fori_loop(0, cpt, body, 0)

    def scatter(tile, sl, wait):
        def body(c, carry):
            src = src_ref[tile * cpt + c]

            @pl.when(src >= 0)
            def _():
                cp = pltpu.make_async_copy(obuf.at[sl, rows(c), :], ys_hbm.at[rows(src), :], out_sem.at[sl])
                cp.wait() if wait else cp.start()
            return carry
        lax.fori_loop(0, cpt, body, 0)

    @pl.when(m == 0)
    def _():
        gather(0, 0, False)
        zbuf[...] = jnp.zeros(zbuf.shape, zbuf.dtype)
        zero_tails(False)

    @pl.when(m + 1 < n_tiles)
    def _():
        gather(m + 1, 1 - slot, False)

    gather(m, slot, True)

    @pl.when(m >= 2)
    def _():
        scatter(m - 2, slot, True)

    @pl.when(used_ref[m] > 0)
    def _():
        x = xbuf[slot, :, :D_MODEL]
        gate = (xbuf[slot, :, D_MODEL:D_MODEL + LANES].astype(F32)
                + xbuf[slot, :, D_MODEL + LANES:].astype(F32))
        lane = lax.broadcasted_iota(jnp.int32, gate.shape, 1)
        base = grp_ref[m] * EXPERTS_PER_GROUP
        acc = jnp.zeros((MOE_TM, D_MODEL), F32)
        for j in range(EXPERTS_PER_GROUP):
            a = jnp.dot(x, w1_ref[j], preferred_element_type=F32)
            b = jnp.dot(x, w3_ref[j], preferred_element_type=F32)
            gj = jnp.sum(jnp.where(lane == base + j, gate, 0.0), axis=-1, keepdims=True)
            hid = (a * jax.nn.sigmoid(a) * b * gj).astype(BF16)
            acc = acc + jnp.dot(hid, w2_ref[j], preferred_element_type=F32)
        obuf[slot] = acc.astype(BF16)

    scatter(m, slot, False)

    @pl.when(m == n_tiles - 1)
    def _():
        zero_tails(True)
        scatter(m, slot, True)

        @pl.when(m >= 1)
        def _():
            scatter(m - 1, 1 - slot, True)


def _moe(xs, w1, w3, w2, tile_group, tile_used, src_chunk, tile_tail):
    n_moe_tiles = tile_group.shape[0]
    rows = xs.shape[0]
    d = D_MODEL
    wspec = lambda r, c: pl.BlockSpec((EXPERTS_PER_GROUP, r, c), lambda m, grp, *_: (grp[m], 0, 0))
    return pl.pallas_call(
        _moe_kernel,
        grid_spec=pltpu.PrefetchScalarGridSpec(
            num_scalar_prefetch=4,
            grid=(n_moe_tiles,),
            in_specs=[pl.BlockSpec(memory_space=pl.ANY),
                      wspec(d, EXPERT_FF), wspec(d, EXPERT_FF), wspec(EXPERT_FF, d)],
            out_specs=pl.BlockSpec(memory_space=pl.ANY),
            scratch_shapes=[pltpu.VMEM((2, MOE_TM, XS_W), BF16),
                            pltpu.VMEM((2, MOE_TM, d), BF16),
                            pltpu.VMEM((CHUNK, d), BF16),
                            pltpu.SemaphoreType.DMA((2,)),
                            pltpu.SemaphoreType.DMA((2,)),
                            pltpu.SemaphoreType.DMA(())]),
        out_shape=jax.ShapeDtypeStruct((rows, d), BF16),
        compiler_params=pltpu.CompilerParams(
            dimension_semantics=("arbitrary",), vmem_limit_bytes=VMEM_LIMIT),
        name="moe",
    )(tile_group, tile_used, src_chunk, tile_tail, xs, w1, w3, w2)


def _combine_kernel(ys_ref, h_ref, dest_ref, gain_ref, o_ref):
    tm = h_ref.shape[0]
    dest = dest_ref[:, 0:1].astype(jnp.int32)
    perm_t = jnp.where(lax.broadcasted_iota(jnp.int32, (tm, SORT_ROWS), 1) == dest, 1.0, 0.0).astype(BF16)
    moe = jnp.dot(perm_t, ys_ref[...], preferred_element_type=F32)
    o_ref[...] = _rms(h_ref[...].astype(F32) + moe, gain_ref[...])


def _combine(ys, h, dest, gain):
    t, d = h.shape
    tm = ROUTE_TM
    return pl.pallas_call(
        _combine_kernel,
        grid=(t // tm,),
        in_specs=[pl.BlockSpec((SORT_ROWS, d), lambda i: (i, 0)),
                  pl.BlockSpec((tm, d), lambda i: (i, 0)),
                  pl.BlockSpec((tm, LANES), lambda i: (i, 0)),
                  pl.BlockSpec((1, d), lambda i: (0, 0))],
        out_specs=pl.BlockSpec((tm, d), lambda i: (i, 0)),
        out_shape=jax.ShapeDtypeStruct((t, d), F32),
        compiler_params=pltpu.CompilerParams(
            dimension_semantics=("arbitrary",), vmem_limit_bytes=VMEM_LIMIT),
        name="combine",
    )(ys, h, dest, gain)


def _rotary_tables(s):
    half = HEAD_DIM // 2
    inv = ROPE_BASE ** (-jnp.arange(half, dtype=F32) / half)
    ang = jnp.arange(s, dtype=F32)[:, None] * inv[None, :]
    cos, sin = jnp.cos(ang), jnp.sin(ang)
    cos_t = jnp.tile(jnp.concatenate([cos, cos], axis=-1), (1, LANES // HEAD_DIM))
    sin_t = jnp.tile(jnp.concatenate([-sin, sin], axis=-1), (1, LANES // HEAD_DIM))
    return cos_t, sin_t


def kernel(x, w_in, w_out, norm_mix, norm_ffn, norm_final, attn_out_gain, rel_bias, ret_decay_fwd, ret_decay_bwd, router_group_w, router_group_b, router_expert_w, router_expert_b, expert_w1, expert_w3, expert_w2):
    b, s, d = x.shape
    depth = w_in.shape[0]
    cos_t, sin_t = _rotary_tables(s)
    bias_rows = _attn_bias_rows(rel_bias)
    h = x
    for layer in range(depth):
        (aq, ak, av, rq, rk, rv, rg), qkv_grouped = _inproj(
            h, norm_mix[layer][None], w_in[layer].astype(BF16), cos_t, sin_t)
        attn = _attention((aq, ak, av), qkv_grouped, bias_rows)
        ret = _retention(rq, rk, rv, rg, *_retention_tables(ret_decay_fwd[layer], ret_decay_bwd[layer]))

        w_o = w_out[layer].astype(BF16)
        rw = jnp.concatenate(
            [jnp.transpose(router_expert_w[layer], (1, 0, 2)).reshape(d, N_EXPERTS),
             router_group_w[layer],
             jnp.zeros((d, LANES - N_EXPERTS - N_GROUPS), F32)], axis=1).astype(BF16)
        rb = jnp.concatenate(
            [router_expert_b[layer].reshape(N_EXPERTS), router_group_b[layer],
             jnp.zeros((LANES - N_EXPERTS - N_GROUPS,), F32)])[None].astype(F32)
        h1, xs, dest, cnt = _outproj(
            attn.reshape(b * s, ATTN_WIDTH), ret.reshape(b * s, RET_WIDTH), h.reshape(b * s, d),
            w_o, attn_out_gain[layer][None], norm_ffn[layer][None],
            rw, rb)
        n_route_tiles = (b * s) // ROUTE_TM
        n_moe_tiles = (b * s + n_route_tiles * N_GROUPS * (CHUNK - 1)) // MOE_TM + N_GROUPS
        schedule = _dispatch_tables(cnt, n_moe_tiles)
        ys = _moe(xs, expert_w1[layer].astype(BF16), expert_w3[layer].astype(BF16),
                  expert_w2[layer].astype(BF16), *schedule)
        assert depth == 1, "the combine kernel fuses the final norm, so it must run on the last layer"
        h = _combine(ys, h1, dest, norm_final[None]).reshape(b, s, d)
    return h
```

```python
import functools
import math

import jax
import jax.numpy as jnp
from jax import lax
from jax.experimental import pallas as pl
from jax.experimental.pallas import tpu as pltpu

F32 = jnp.float32
BF16 = jnp.bfloat16

D_MODEL = 1024
HEAD_DIM = 64
ATTN_WIDTH = 512
RET_WIDTH = 512
N_HEADS = 8
PAIR = 2 * HEAD_DIM
N_PAIRS = N_HEADS // 2
ATTN_DILATIONS = (1, 4, 16)
ATTN_RADIUS = 64
N_BUCKETS = 32
REL_MAX_DIST = 1024
ROPE_BASE = 10000.0
N_GROUPS = 4
EXPERTS_PER_GROUP = 4
N_EXPERTS = 16
EXPERT_FF = 512
EPS = 1e-6
NEG_INF = -1e30
LOG2_E = math.log2(math.e)

LANES = 128
ATTN_TQ = 128
ATTN_W = 256
N_BIAS_VARIANTS = 8
RET_CHUNK = 256
ROUTE_TM = 512
ROUTE_TILES_PER_STEP = 2
SORT_ROWS = 576
CHUNK = 16
MOE_TM = 512
XS_W = D_MODEL + 2 * LANES
VMEM_LIMIT = 48 * 1024 * 1024


def _rms(x, gain):
    return x * lax.rsqrt(jnp.mean(x * x, axis=-1, keepdims=True) + EPS) * gain


def _inproj_kernel(x_ref, gain_ref, w_ref, cos_ref, sin_ref,
                   aq_ref, ak_ref, av_ref, rq_ref, rk_ref, rv_ref, rg_ref,
                   aq4_ref, ak4_ref, av4_ref, aq16_ref, ak16_ref, av16_ref, stage_ref, stage4_ref):
    tm = x_ref.shape[0]
    xn = _rms(x_ref[...], gain_ref[...]).astype(BF16)

    def emit(i, t, nat_ref, d4_ref, d16_ref):
        nat_ref[...] = t.astype(BF16)
        for hp in range(N_PAIRS):
            stage_ref[i, hp] = t[:, hp * PAIR:(hp + 1) * PAIR]
            for r4 in range(4):
                g4 = stage_ref[i, hp, pl.ds(r4, tm // 4, stride=4), :]
                d4_ref[hp, r4] = g4.astype(BF16)
                stage4_ref[i, hp, r4] = g4
                for j in range(4):
                    d16_ref[hp, r4 + 4 * j] = (
                        stage4_ref[i, hp, r4, pl.ds(j, tm // 16, stride=4), :].astype(BF16))

    def seg(i):
        return jnp.dot(xn, w_ref[:, i * 512:(i + 1) * 512], preferred_element_type=F32)

    def rotary(t):
        cos, sin = cos_ref[...], sin_ref[...]
        first_half = (lax.broadcasted_iota(jnp.int32, (1, LANES), 1) % HEAD_DIM) < HEAD_DIM // 2
        outs = []
        for j in range(t.shape[1] // LANES):
            tj = t[:, j * LANES:(j + 1) * LANES]
            partner = jnp.where(first_half, pltpu.roll(tj, LANES - 32, 1), pltpu.roll(tj, 32, 1))
            outs.append(tj * cos + partner * sin)
        return jnp.concatenate(outs, axis=1)

    emit(0, seg(0) * (HEAD_DIM ** -0.5 * LOG2_E), aq_ref, aq4_ref, aq16_ref)
    emit(1, seg(1), ak_ref, ak4_ref, ak16_ref)
    emit(2, seg(2), av_ref, av4_ref, av16_ref)
    rq_ref[...] = rotary(seg(3)).astype(BF16)
    rk_ref[...] = (rotary(seg(4)) * (HEAD_DIM ** -0.5)).astype(BF16)
    rv_ref[...] = seg(5).astype(BF16)
    rg_ref[...] = seg(6).astype(BF16)


def _inproj(x, gain, w_in, cos_t, sin_t, tm=512):
    b, s, d = x.shape
    n = w_in.shape[1]
    out = jax.ShapeDtypeStruct((b, s, 512), BF16)
    ospec = pl.BlockSpec((None, tm, 512), lambda si, bi: (bi, si, 0))

    def grouped(dil):
        shape = jax.ShapeDtypeStruct((b, N_PAIRS, dil, s // dil, PAIR), BF16)
        spec = pl.BlockSpec((None, N_PAIRS, dil, tm // dil, PAIR), lambda si, bi: (bi, 0, 0, si, 0))
        return [shape] * 3, [spec] * 3

    shapes4, specs4 = grouped(4)
    shapes16, specs16 = grouped(16)
    outs = pl.pallas_call(
        _inproj_kernel,
        grid=(s // tm, b),
        in_specs=[
            pl.BlockSpec((None, tm, d), lambda si, bi: (bi, si, 0)),
            pl.BlockSpec((1, d), lambda si, bi: (0, 0)),
            pl.BlockSpec((d, n), lambda si, bi: (0, 0)),
            pl.BlockSpec((tm, LANES), lambda si, bi: (si, 0)),
            pl.BlockSpec((tm, LANES), lambda si, bi: (si, 0)),
        ],
        out_specs=[ospec] * 7 + specs4 + specs16,
        out_shape=[out] * 7 + shapes4 + shapes16,
        scratch_shapes=[pltpu.VMEM((3, N_PAIRS, tm, PAIR), F32),
                        pltpu.VMEM((3, N_PAIRS, 4, tm // 4, PAIR), F32)],
        compiler_params=pltpu.CompilerParams(
            dimension_semantics=("arbitrary", "arbitrary"), vmem_limit_bytes=VMEM_LIMIT),
        name="inproj",
    )(x, gain, w_in, cos_t, sin_t)
    return outs[:7], [o.reshape(b, N_PAIRS, s, PAIR) for o in outs[7:]]


def _t5_bucket(rel):
    half = N_BUCKETS // 2
    max_exact = half // 2
    offset = jnp.where(rel > 0, half, 0)
    n = jnp.abs(rel)
    nf = jnp.maximum(n, 1).astype(F32)
    large = max_exact + (jnp.log(nf / max_exact) / math.log(REL_MAX_DIST / max_exact)
                         * (half - max_exact)).astype(jnp.int32)
    large = jnp.minimum(large, half - 1)
    return offset + jnp.where(n < max_exact, n, large)


def _attn_bias_rows(rel_bias):
    period = 2 * ATTN_W
    band = 2 * ATTN_RADIUS + 1
    rel = jnp.arange(-ATTN_RADIUS, ATTN_RADIUS + 1)
    rows = []
    for dil, offs in ((1, (0, 64, 128)), (4, (0, 64, 128)), (16, (0, 128))):
        vals = rel_bias[_t5_bucket(rel * dil)].astype(F32).T * LOG2_E
        for off in offs:
            lo = off - ATTN_RADIUS
            pad = jnp.full((N_HEADS, period - band), NEG_INF, F32)
            if lo >= 0:
                row = jnp.concatenate([pad[:, :lo], vals, pad[:, lo:]], axis=1)
            else:
                row = jnp.concatenate([vals[:, -lo:], pad, vals[:, :-lo]], axis=1)
            rows.append(row)
    v = jnp.stack(rows, axis=1)
    return v.reshape(N_PAIRS, 2 * N_BIAS_VARIANTS, period)


def _attention_kernel(q_ref, k_ref, v_ref, q4_ref, k4_ref, v4_ref, q16_ref, k16_ref, v16_ref,
                      rows_ref, o_ref, bias_ref, acc_ref, m_ref, l_ref, out_ref):
    s = q_ref.shape[0]
    n_tiles = s // ATTN_TQ
    lane = lax.broadcasted_iota(jnp.int32, (1, PAIR), 1)
    head0 = lane < HEAD_DIM

    @pl.when(pl.program_id(1) == 0)
    def _():
        col = lax.broadcasted_iota(jnp.int32, (ATTN_TQ, ATTN_W), 1)
        for idx in range(2 * N_BIAS_VARIANTS):
            gen = jnp.broadcast_to(rows_ref[idx:idx + 1, :], (ATTN_TQ, 2 * ATTN_W))
            tab = pltpu.roll(gen, 0, 1, stride=1, stride_axis=0)[:, :ATTN_W]
            var = idx % N_BIAS_VARIANTS
            if var >= 6:
                tab = jnp.where((col // ATTN_TQ) == var - 6, tab, NEG_INF)
            head = idx // N_BIAS_VARIANTS
            bias_ref[var, head * ATTN_TQ:(head + 1) * ATTN_TQ, :] = tab

    def run_branch(bi, dil, qs_ref, ks_ref, vs_ref):
        sub_len = s // dil
        tiles_per_sub = sub_len // ATTN_TQ

        def tile(t, carry):
            q0 = pl.multiple_of(t * ATTN_TQ, ATTN_TQ)
            if tiles_per_sub == 1:
                ws = pl.multiple_of((t // 2) * ATTN_W, ATTN_W)
                var = 6 + t % 2
            else:
                pos = t % tiles_per_sub
                sub_lo = (t // tiles_per_sub) * sub_len
                ws = jnp.clip(q0 - 64, sub_lo, sub_lo + sub_len - ATTN_W)
                ws = pl.multiple_of(ws, 64)
                var = jnp.where(pos == 0, 0, jnp.where(pos == tiles_per_sub - 1, 2, 1)) + 3 * bi
            q = qs_ref[pl.ds(q0, ATTN_TQ), :]
            k = ks_ref[pl.ds(ws, ATTN_W), :]
            v = vs_ref[pl.ds(ws, ATTN_W), :]
            q2 = jnp.concatenate([jnp.where(head0, q, jnp.zeros_like(q)),
                                  jnp.where(head0, jnp.zeros_like(q), q)], axis=0)
            sc = lax.dot_general(q2, k, (((1,), (1,)), ((), ())), preferred_element_type=F32)
            sc = sc + bias_ref[var]
            m = jnp.max(sc, axis=-1, keepdims=True)
            p = jnp.exp2(sc - m).astype(BF16)
            o = jnp.dot(p, jnp.concatenate([v, jnp.ones_like(v)], axis=1), preferred_element_type=F32)
            l = o[:, PAIR:]
            outs, ms, ls = (o[:ATTN_TQ, :PAIR], o[ATTN_TQ:, :PAIR]), (m[:ATTN_TQ], m[ATTN_TQ:]), (l[:ATTN_TQ], l[ATTN_TQ:])
            if dil == 1:
                dst = pl.ds(q0, ATTN_TQ)
            elif dil == 4:
                dst = pl.ds((t % 4) * (4 * ATTN_TQ) + t // 4, ATTN_TQ, stride=4)
            else:
                dst = pl.ds((t % 4) * (s // 4) + t // 4, ATTN_TQ, stride=4)
            acc_ref[bi, dst, :] = jnp.where(head0, outs[0], outs[1])
            m_ref[bi, dst, :] = jnp.where(head0, ms[0], ms[1])
            l_ref[bi, dst, :] = jnp.where(head0, ls[0], ls[1])
            return carry

        lax.fori_loop(0, n_tiles, tile, 0, unroll=16)

    run_branch(0, 1, q_ref, k_ref, v_ref)
    run_branch(1, 4, q4_ref, k4_ref, v4_ref)
    run_branch(2, 16, q16_ref, k16_ref, v16_ref)

    rows = ATTN_TQ
    for r4 in range(4):
        for blk in range(s // (4 * rows)):
            nat = pl.ds(r4 + 4 * rows * blk, rows, stride=4)
            sl = (nat, nat, pl.ds(r4 * (s // 4) + rows * blk, rows))
            m = [m_ref[bi, sl[bi], :] for bi in range(3)]
            mx = jnp.maximum(jnp.maximum(m[0], m[1]), m[2])
            num = jnp.zeros((rows, PAIR), F32)
            den = jnp.zeros((rows, PAIR), F32)
            for bi in range(3):
                e = jnp.exp2(m[bi] - mx)
                num = num + e * acc_ref[bi, sl[bi], :]
                den = den + e * l_ref[bi, sl[bi], :]
            out_ref[nat, :] = num / den
    o_ref[...] = out_ref[...].astype(o_ref.dtype)


def _attention(qkv, qkv_grouped, bias_rows):
    b, s, _ = qkv[0].shape
    spec = pl.BlockSpec((None, s, PAIR), lambda hp, bi: (bi, 0, hp))
    gspec = pl.BlockSpec((None, None, s, PAIR), lambda hp, bi: (bi, hp, 0, 0))
    return pl.pallas_call(
        _attention_kernel,
        grid=(N_PAIRS, b),
        in_specs=[spec] * 3 + [gspec] * 6 + [
            pl.BlockSpec((None, 2 * N_BIAS_VARIANTS, 2 * ATTN_W), lambda hp, bi: (hp, 0, 0))],
        out_specs=spec,
        out_shape=jax.ShapeDtypeStruct((b, s, ATTN_WIDTH), BF16),
        scratch_shapes=[
            pltpu.VMEM((N_BIAS_VARIANTS, 2 * ATTN_TQ, ATTN_W), F32),
            pltpu.VMEM((3, s, PAIR), F32),
            pltpu.VMEM((3, s, PAIR), F32),
            pltpu.VMEM((3, s, PAIR), F32),
            pltpu.VMEM((s, PAIR), F32),
        ],
        compiler_params=pltpu.CompilerParams(
            dimension_semantics=("arbitrary", "arbitrary"), vmem_limit_bytes=VMEM_LIMIT),
        name="attention",
    )(*qkv, *qkv_grouped, bias_rows)


def _retention_tables(decay_fwd, decay_bwd):
    c = RET_CHUNK
    lg_f = -jnp.exp(decay_fwd.astype(F32))
    lg_b = -jnp.exp(decay_bwd.astype(F32))
    idx = jnp.arange(c, dtype=F32)
    rel = idx[:, None] - idx[None, :]
    dmat = jnp.where(rel >= 0,
                     jnp.exp(lg_f[:, None, None] * jnp.maximum(rel, 0.0)[None]),
                     jnp.exp(lg_b[:, None, None] * jnp.maximum(-rel, 0.0)[None]))
    dmat = dmat.reshape(N_PAIRS, 2, c, c)

    def lanes(v):
        v = v.reshape(N_PAIRS, 2, -1)
        return jnp.repeat(jnp.transpose(v, (0, 2, 1)), HEAD_DIM, axis=2)

    vec = jnp.stack([
        lanes(jnp.exp(lg_f[:, None] * (idx + 1.0)[None])),
        lanes(jnp.exp(lg_f[:, None] * (c - 1.0 - idx)[None])),
        lanes(jnp.exp(lg_b[:, None] * (c - idx)[None])),
        lanes(jnp.exp(lg_b[:, None] * idx[None])),
    ], axis=1)
    same_head = (jnp.arange(PAIR)[:, None] // HEAD_DIM) == (jnp.arange(PAIR)[None, :] // HEAD_DIM)
    cd = jnp.stack([lanes(jnp.exp(lg_f * c)[:, None]), lanes(jnp.exp(lg_b * c)[:, None])], axis=1)
    cd = jnp.transpose(cd, (0, 1, 3, 2)) * same_head[None, None].astype(F32)
    return dmat, vec, cd


def _retention_kernel(q_ref, k_ref, v_ref, g_ref, dmat_ref, vec_ref, cd_ref, o_ref, kv_ref, st_ref):
    s = q_ref.shape[0]
    c = RET_CHUNK
    nc = s // c
    lane = lax.broadcasted_iota(jnp.int32, (1, PAIR), 1)
    head0 = lane < HEAD_DIM
    same_head = ((lax.broadcasted_iota(jnp.int32, (PAIR, PAIR), 0) // HEAD_DIM)
                 == (lax.broadcasted_iota(jnp.int32, (PAIR, PAIR), 1) // HEAD_DIM))

    same_head2 = jnp.concatenate([same_head, same_head], axis=0)
    for n in range(nc):
        rows = slice(n * c, (n + 1) * c)
        kf = k_ref[rows, :].astype(F32)
        kcat = jnp.concatenate([(kf * vec_ref[1]).astype(BF16), (kf * vec_ref[3]).astype(BF16)], axis=1)
        kv = lax.dot_general(kcat, v_ref[rows, :], (((0,), (0,)), ((), ())), preferred_element_type=F32)
        kv_ref[n] = jnp.where(same_head2, kv, 0.0)

    state = jnp.zeros((PAIR, PAIR), F32)
    for n in range(nc):
        st_ref[n, :PAIR, :] = state.astype(BF16)
        state = state * cd_ref[0] + kv_ref[n, :PAIR, :]
    state = jnp.zeros((PAIR, PAIR), F32)
    for n in reversed(range(nc)):
        st_ref[n, PAIR:, :] = state.astype(BF16)
        state = state * cd_ref[1] + kv_ref[n, PAIR:, :]

    for n in range(nc):
        rows = slice(n * c, (n + 1) * c)
        q, k, v = q_ref[rows, :], k_ref[rows, :], v_ref[rows, :]
        outs = []
        for h in range(2):
            qh = jnp.where(head0 if h == 0 else ~head0, q, jnp.zeros_like(q))
            sc = lax.dot_general(qh, k, (((1,), (1,)), ((), ())), preferred_element_type=F32)
            outs.append(jnp.dot((sc * dmat_ref[h]).astype(BF16), v, preferred_element_type=F32))
        qf = q.astype(F32)
        qcat = jnp.concatenate([(qf * vec_ref[0]).astype(BF16), (qf * vec_ref[2]).astype(BF16)], axis=1)
        y = jnp.where(head0, outs[0], outs[1]) + jnp.dot(qcat, st_ref[n], preferred_element_type=F32)
        y2 = y * y
        ms0 = jnp.sum(jnp.where(head0, y2, 0.0), axis=-1, keepdims=True)
        ms1 = jnp.sum(jnp.where(head0, 0.0, y2), axis=-1, keepdims=True)
        ms = jnp.where(head0, ms0, ms1) * (1.0 / HEAD_DIM)
        g = g_ref[rows, :].astype(F32)
        o_ref[rows, :] = (y * lax.rsqrt(ms + EPS) * (g * jax.nn.sigmoid(g))).astype(o_ref.dtype)


def _retention(rq, rk, rv, rg, dmat, vec, cd, weights_f32):
    b, s, _ = rq.shape
    c = RET_CHUNK
    n_steps = N_PAIRS * b
    spec = pl.BlockSpec((None, s, PAIR), lambda hp, bi: (bi, 0, hp))
    slabs = [w.reshape(n_steps, -1, w.shape[-1]) for w in weights_f32]
    slab_specs = [pl.BlockSpec((None,) + w.shape[1:], lambda hp, bi: (hp * b + bi, 0, 0)) for w in slabs]

    def kernel(*refs):
        n_in, n_w = 7, len(slabs)
        _retention_kernel(*refs[:n_in], refs[n_in + n_w], *refs[n_in + 2 * n_w + 1:])
        for src, dst in zip(refs[n_in:n_in + n_w], refs[n_in + n_w + 1:n_in + 2 * n_w + 1]):
            dst[...] = src[...].astype(BF16)

    outs = pl.pallas_call(
        kernel,
        grid=(N_PAIRS, b),
        in_specs=[spec, spec, spec, spec,
                  pl.BlockSpec((None, 2, c, c), lambda hp, bi: (hp, 0, 0, 0)),
                  pl.BlockSpec((None, 4, c, PAIR), lambda hp, bi: (hp, 0, 0, 0)),
                  pl.BlockSpec((None, 2, PAIR, PAIR), lambda hp, bi: (hp, 0, 0, 0))] + slab_specs,
        out_specs=[spec] + slab_specs,
        out_shape=[jax.ShapeDtypeStruct((b, s, RET_WIDTH), BF16)]
        + [jax.ShapeDtypeStruct(w.shape, BF16) for w in slabs],
        scratch_shapes=[pltpu.VMEM((s // c, 2 * PAIR, PAIR), F32),
                        pltpu.VMEM((s // c, 2 * PAIR, PAIR), BF16)],
        compiler_params=pltpu.CompilerParams(
            dimension_semantics=("arbitrary", "arbitrary"), vmem_limit_bytes=VMEM_LIMIT),
        name="retention",
    )(rq, rk, rv, rg, dmat, vec, cd, *slabs)
    return outs[0], [o.reshape(w.shape) for o, w in zip(outs[1:], weights_f32)]


def _route_tile(sub, attn_ref, ret_ref, x_ref, wo_ref, again_ref, fgain_ref,
                rw_ref, rb_ref, h_ref, xs_ref, dest_ref, cnt_ref):
    tm = ROUTE_TM
    rows = slice(sub * tm, (sub + 1) * tm)
    a = _rms(attn_ref[rows, :].astype(F32), again_ref[...]).astype(BF16)
    mixed = jnp.concatenate([a, ret_ref[rows, :]], axis=1)
    h = x_ref[rows, :] + jnp.dot(mixed, wo_ref[...], preferred_element_type=F32)
    h_ref[rows, :] = h.astype(h_ref.dtype)
    hn = _rms(h, fgain_ref[...]).astype(BF16)

    logits = jnp.dot(hn, rw_ref[...], preferred_element_type=F32) + rb_ref[...]
    lane = lax.broadcasted_iota(jnp.int32, logits.shape, 1)
    big = jnp.int32(LANES)

    def first_argmax(vals):
        top = jnp.max(vals, axis=-1, keepdims=True)
        return top, jnp.min(jnp.where(vals == top, lane, big), axis=-1, keepdims=True)

    gl = jnp.where((lane >= N_EXPERTS) & (lane < N_EXPERTS + N_GROUPS), logits, -jnp.inf)
    gmax, gidx = first_argmax(gl)
    p_group = 1.0 / jnp.sum(jnp.exp(gl - gmax), axis=-1, keepdims=True)
    grp = gidx - N_EXPERTS
    lo = grp * EXPERTS_PER_GROUP
    el = jnp.where((lane >= lo) & (lane < lo + EXPERTS_PER_GROUP), logits, -jnp.inf)
    v1, i1 = first_argmax(el)
    v2, i2 = first_argmax(jnp.where(lane == i1, -jnp.inf, el))
    e2 = jnp.exp(v2 - v1)
    p1 = p_group / (1.0 + e2)
    gates = jnp.where(lane == i1, p1, jnp.where(lane == i2, p1 * e2, 0.0))

    onehot = jnp.where(lane == grp, 1.0, 0.0)
    r_i = lax.broadcasted_iota(jnp.int32, (tm, tm), 0)
    c_i = lax.broadcasted_iota(jnp.int32, (tm, tm), 1)
    before = jnp.where(c_i < r_i, 1.0, 0.0).astype(BF16)
    rank = jnp.dot(before, onehot.astype(BF16), preferred_element_type=F32)
    cnt = jnp.sum(onehot, axis=0, keepdims=True)
    c16 = jnp.broadcast_to(jnp.ceil(cnt * (1.0 / CHUNK)) * CHUNK, (8, LANES))
    lane8 = lax.broadcasted_iota(jnp.int32, (8, LANES), 1)
    start = jnp.zeros((8, LANES), F32)
    for sft in range(1, N_GROUPS):
        start = start + jnp.where(lane8 >= sft, pltpu.roll(c16, sft, 1), 0.0)
    dest = jnp.sum(onehot * (start[0:1] + rank), axis=-1, keepdims=True)
    dest_ref[rows, :] = jnp.broadcast_to(dest, (tm, LANES))
    cnt_ref[sub] = jnp.where(lane8 < N_GROUPS, c16, pltpu.roll(start, N_GROUPS, 1)).astype(jnp.int32)

    onehot_t = onehot.T
    rank_t = lax.dot_general(onehot_t.astype(BF16), before, (((1,), (1,)), ((), ())),
                             preferred_element_type=F32)
    g_row = lax.broadcasted_iota(jnp.int32, (LANES, LANES), 0)
    g_col = lax.broadcasted_iota(jnp.int32, (LANES, LANES), 1)
    start_t = jnp.sum(jnp.where(g_col < g_row, c16[0:1], 0.0), axis=-1, keepdims=True)
    dest_t = jnp.sum(onehot_t * (start_t + rank_t), axis=0, keepdims=True)
    perm = jnp.where(lax.broadcasted_iota(jnp.int32, (SORT_ROWS, tm), 0) == dest_t.astype(jnp.int32),
                     1.0, 0.0).astype(BF16)
    g_hi = gates.astype(BF16)
    g_lo = (gates - g_hi.astype(F32)).astype(BF16)
    payload = jnp.concatenate([hn, g_hi, g_lo], axis=1)
    xs_ref[sub * SORT_ROWS:(sub + 1) * SORT_ROWS, :] = (
        jnp.dot(perm, payload, preferred_element_type=F32).astype(BF16))


def _outproj_kernel(*refs):
    for sub in range(ROUTE_TILES_PER_STEP):
        _route_tile(sub, *refs)


def _outproj(attn, ret, x, w_o, again, fgain, rw, rb):
    t, d = x.shape
    per_step = ROUTE_TILES_PER_STEP
    tm = ROUTE_TM * per_step
    nt = t // ROUTE_TM
    row = lambda w: pl.BlockSpec((tm, w), lambda i: (i, 0))
    full = lambda r, c: pl.BlockSpec((r, c), lambda i: (0, 0))
    return pl.pallas_call(
        _outproj_kernel,
        grid=(nt // per_step,),
        in_specs=[row(ATTN_WIDTH), row(RET_WIDTH), row(d), full(ATTN_WIDTH + RET_WIDTH, d),
                  full(1, ATTN_WIDTH), full(1, d), full(d, LANES), full(1, LANES)],
        out_specs=[row(d), pl.BlockSpec((per_step * SORT_ROWS, XS_W), lambda i: (i, 0)), row(LANES),
                   pl.BlockSpec((per_step, 8, LANES), lambda i: (i, 0, 0))],
        out_shape=[jax.ShapeDtypeStruct((t, d), BF16),
                   jax.ShapeDtypeStruct((nt * SORT_ROWS, XS_W), BF16),
                   jax.ShapeDtypeStruct((t, LANES), F32),
                   jax.ShapeDtypeStruct((nt, 8, LANES), jnp.int32)],
        compiler_params=pltpu.CompilerParams(
            dimension_semantics=("arbitrary",), vmem_limit_bytes=VMEM_LIMIT),
        name="outproj",
    )(attn, ret, x, w_o, again, fgain, rw, rb)


def _dispatch_tables(cnt, n_moe_tiles):
    nt = cnt.shape[0]
    cpt = MOE_TM // CHUNK

    n_slots = n_moe_tiles * cpt
    slot_rows = -(-n_slots // (8 * LANES)) * 8

    def schedule_kernel(seg_ref, grp_ref, used_ref, src_ref, tail_ref):
        slot = (lax.broadcasted_iota(jnp.int32, (slot_rows, LANES), 0) * LANES
                + lax.broadcasted_iota(jnp.int32, (slot_rows, LANES), 1))
        src = jnp.full((slot_rows, LANES), -1, jnp.int32)
        pos = jnp.int32(0)
        for g in range(N_GROUPS):
            def tile_body(i, carry):
                p, src = carry
                n = seg_ref[i, g] // CHUNK
                first = (i * SORT_ROWS + seg_ref[i, N_GROUPS + g]) // CHUNK
                src = jnp.where((slot >= p) & (slot < p + n), slot + (first - p), src)
                return p + n, src
            end, src = lax.fori_loop(0, nt, tile_body, (pos, src))
            padded = ((end + cpt - 1) // cpt) * cpt

            def mark_body(m, carry):
                grp_ref[m] = g
                used_ref[m] = 1
                return carry
            lax.fori_loop(pos // cpt, padded // cpt, mark_body, 0)
            pos = padded

        def idle_body(m, carry):
            grp_ref[m] = N_GROUPS - 1
            used_ref[m] = 0
            return carry
        lax.fori_loop(pos // cpt, n_moe_tiles, idle_body, 0)
        src_ref[...] = src

        def tail_body(i, carry):
            rows = seg_ref[i, 0]
            for g in range(1, N_GROUPS):
                rows = rows + seg_ref[i, g]
            tail_ref[i] = rows // CHUNK
            return carry
        lax.fori_loop(0, nt, tail_body, 0)

    smem = lambda: pl.BlockSpec(memory_space=pltpu.SMEM)
    tile_group, tile_used, src, tile_tail = pl.pallas_call(
        schedule_kernel,
        in_specs=[smem()],
        out_specs=[smem(), smem(), pl.BlockSpec(memory_space=pltpu.VMEM), smem()],
        out_shape=[jax.ShapeDtypeStruct((n_moe_tiles,), jnp.int32),
                   jax.ShapeDtypeStruct((n_moe_tiles,), jnp.int32),
                   jax.ShapeDtypeStruct((slot_rows, LANES), jnp.int32),
                   jax.ShapeDtypeStruct((nt,), jnp.int32)],
        name="schedule",
    )(cnt[:, 0, :2 * N_GROUPS])
    return tile_group, tile_used, src.reshape(-1)[:n_slots], tile_tail


def _moe_kernel(grp_ref, used_ref, src_ref, tail_ref, xs_hbm, w1_ref, w3_ref, w2_ref, ys_hbm,
                xbuf, obuf, zbuf, in_sem, out_sem, zero_sem):
    m = pl.program_id(0)
    n_tiles = pl.num_programs(0)
    cpt = MOE_TM // CHUNK
    slot = m % 2

    def rows(c):
        return pl.ds(pl.multiple_of(c * CHUNK, CHUNK), CHUNK)

    def zero_tails(wait):
        cps = SORT_ROWS // CHUNK

        def tile_body(i, carry):
            def body(c, carry2):
                cp = pltpu.make_async_copy(zbuf, ys_hbm.at[rows(i * cps + c), :], zero_sem)
                cp.wait() if wait else cp.start()
                return carry2
            lax.fori_loop(tail_ref[i], cps, body, 0)
            return carry
        lax.fori_loop(0, tail_ref.shape[0], tile_body, 0)

    def gather(tile, sl, wait):
        def body(c, carry):
            src = src_ref[tile * cpt + c]

            @pl.when(src >= 0)
            def _():
                cp = pltpu.make_async_copy(xs_hbm.at[rows(src), :], xbuf.at[sl, rows(c), :], in_sem.at[sl])
                cp.wait() if wait else cp.start()

            if not wait:
                @pl.when(src < 0)
                def _():
                    xbuf[sl, rows(c), :] = jnp.zeros((CHUNK, XS_W), BF16)
            return carry
        lax.fori_loop(0, cpt, body, 0)

    def scatter(tile, sl, wait):
        def body(c, carry):
            src = src_ref[tile * cpt + c]

            @pl.when(src >= 0)
            def _():
                cp = pltpu.make_async_copy(obuf.at[sl, rows(c), :], ys_hbm.at[rows(src), :], out_sem.at[sl])
                cp.wait() if wait else cp.start()
            return carry
        lax.fori_loop(0, cpt, body, 0)

    @pl.when(m == 0)
    def _():
        gather(0, 0, False)
        zbuf[...] = jnp.zeros(zbuf.shape, zbuf.dtype)
        zero_tails(False)

    @pl.when(m + 1 < n_tiles)
    def _():
        gather(m + 1, 1 - slot, False)

    gather(m, slot, True)

    @pl.when(m >= 2)
    def _():
        scatter(m - 2, slot, True)

    @pl.when(used_ref[m] > 0)
    def _():
        x = xbuf[slot, :, :D_MODEL]
        gate = (xbuf[slot, :, D_MODEL:D_MODEL + LANES].astype(F32)
                + xbuf[slot, :, D_MODEL + LANES:].astype(F32))
        lane = lax.broadcasted_iota(jnp.int32, gate.shape, 1)
        base = grp_ref[m] * EXPERTS_PER_GROUP
        acc = jnp.zeros((MOE_TM, D_MODEL), F32)
        for j in range(EXPERTS_PER_GROUP):
            a = jnp.dot(x, w1_ref[j], preferred_element_type=F32)
            b = jnp.dot(x, w3_ref[j], preferred_element_type=F32)
            gj = jnp.sum(jnp.where(lane == base + j, gate, 0.0), axis=-1, keepdims=True)
            hid = (a * jax.nn.sigmoid(a) * b * gj).astype(BF16)
            acc = acc + jnp.dot(hid, w2_ref[j], preferred_element_type=F32)
        obuf[slot] = acc.astype(BF16)

    scatter(m, slot, False)

    @pl.when(m == n_tiles - 1)
    def _():
        zero_tails(True)
        scatter(m, slot, True)

        @pl.when(m >= 1)
        def _():
            scatter(m - 1, 1 - slot, True)


def _moe(xs, w1, w3, w2, tile_group, tile_used, src_chunk, tile_tail):
    n_moe_tiles = tile_group.shape[0]
    rows = xs.shape[0]
    d = D_MODEL
    wspec = lambda r, c: pl.BlockSpec((EXPERTS_PER_GROUP, r, c), lambda m, grp, *_: (grp[m], 0, 0))
    return pl.pallas_call(
        _moe_kernel,
        grid_spec=pltpu.PrefetchScalarGridSpec(
            num_scalar_prefetch=4,
            grid=(n_moe_tiles,),
            in_specs=[pl.BlockSpec(memory_space=pl.ANY),
                      wspec(d, EXPERT_FF), wspec(d, EXPERT_FF), wspec(EXPERT_FF, d)],
            out_specs=pl.BlockSpec(memory_space=pl.ANY),
            scratch_shapes=[pltpu.VMEM((2, MOE_TM, XS_W), BF16),
                            pltpu.VMEM((2, MOE_TM, d), BF16),
                            pltpu.VMEM((CHUNK, d), BF16),
                            pltpu.SemaphoreType.DMA((2,)),
                            pltpu.SemaphoreType.DMA((2,)),
                            pltpu.SemaphoreType.DMA(())]),
        out_shape=jax.ShapeDtypeStruct((rows, d), BF16),
        compiler_params=pltpu.CompilerParams(
            dimension_semantics=("arbitrary",), vmem_limit_bytes=VMEM_LIMIT),
        name="moe",
    )(tile_group, tile_used, src_chunk, tile_tail, xs, w1, w3, w2)


def _combine_kernel(ys_ref, h_ref, dest_ref, gain_ref, o_ref):
    tm = h_ref.shape[0]
    dest = dest_ref[:, 0:1].astype(jnp.int32)
    perm_t = jnp.where(lax.broadcasted_iota(jnp.int32, (tm, SORT_ROWS), 1) == dest, 1.0, 0.0).astype(BF16)
    moe = jnp.dot(perm_t, ys_ref[...], preferred_element_type=F32)
    o_ref[...] = _rms(h_ref[...].astype(F32) + moe, gain_ref[...])


def _combine(ys, h, dest, gain):
    t, d = h.shape
    tm = ROUTE_TM
    return pl.pallas_call(
        _combine_kernel,
        grid=(t // tm,),
        in_specs=[pl.BlockSpec((SORT_ROWS, d), lambda i: (i, 0)),
                  pl.BlockSpec((tm, d), lambda i: (i, 0)),
                  pl.BlockSpec((tm, LANES), lambda i: (i, 0)),
                  pl.BlockSpec((1, d), lambda i: (0, 0))],
        out_specs=pl.BlockSpec((tm, d), lambda i: (i, 0)),
        out_shape=jax.ShapeDtypeStruct((t, d), F32),
        compiler_params=pltpu.CompilerParams(
            dimension_semantics=("arbitrary",), vmem_limit_bytes=VMEM_LIMIT),
        name="combine",
    )(ys, h, dest, gain)


def _rotary_tables(s):
    half = HEAD_DIM // 2
    inv = ROPE_BASE ** (-jnp.arange(half, dtype=F32) / half)
    ang = jnp.arange(s, dtype=F32)[:, None] * inv[None, :]
    cos, sin = jnp.cos(ang), jnp.sin(ang)
    cos_t = jnp.tile(jnp.concatenate([cos, cos], axis=-1), (1, LANES // HEAD_DIM))
    sin_t = jnp.tile(jnp.concatenate([-sin, sin], axis=-1), (1, LANES // HEAD_DIM))
    return cos_t, sin_t


def kernel(x, w_in, w_out, norm_mix, norm_ffn, norm_final, attn_out_gain, rel_bias, ret_decay_fwd, ret_decay_bwd, router_group_w, router_group_b, router_expert_w, router_expert_b, expert_w1, expert_w3, expert_w2):
    b, s, d = x.shape
    depth = w_in.shape[0]
    cos_t, sin_t = _rotary_tables(s)
    bias_rows = _attn_bias_rows(rel_bias)
    h = x
    for layer in range(depth):
        (aq, ak, av, rq, rk, rv, rg), qkv_grouped = _inproj(
            h, norm_mix[layer][None], w_in[layer].astype(BF16), cos_t, sin_t)
        attn = _attention((aq, ak, av), qkv_grouped, bias_rows)
        ret, (w1, w3, w2, w_o) = _retention(
            rq, rk, rv, rg, *_retention_tables(ret_decay_fwd[layer], ret_decay_bwd[layer]),
            (expert_w1[layer], expert_w3[layer], expert_w2[layer], w_out[layer]))

        rw = jnp.concatenate(
            [jnp.transpose(router_expert_w[layer], (1, 0, 2)).reshape(d, N_EXPERTS),
             router_group_w[layer],
             jnp.zeros((d, LANES - N_EXPERTS - N_GROUPS), F32)], axis=1).astype(BF16)
        rb = jnp.concatenate(
            [router_expert_b[layer].reshape(N_EXPERTS), router_group_b[layer],
             jnp.zeros((LANES - N_EXPERTS - N_GROUPS,), F32)])[None].astype(F32)
        h1, xs, dest, cnt = _outproj(
            attn.reshape(b * s, ATTN_WIDTH), ret.reshape(b * s, RET_WIDTH), h.reshape(b * s, d),
            w_o, attn_out_gain[layer][None], norm_ffn[layer][None],
            rw, rb)
        n_route_tiles = (b * s) // ROUTE_TM
        n_moe_tiles = (b * s + n_route_tiles * N_GROUPS * (CHUNK - 1)) // MOE_TM + N_GROUPS
        schedule = _dispatch_tables(cnt, n_moe_tiles)
        ys = _moe(xs, w1, w3, w2, *schedule)
        assert depth == 1, "the combine kernel fuses the final norm, so it must run on the last layer"
        h = _combine(ys, h1, dest, norm_final[None]).reshape(b, s, d)
    return h
```

```python
import functools
import math

import jax
import jax.numpy as jnp
from jax import lax
from jax.experimental import pallas as pl
from jax.experimental.pallas import tpu as pltpu

F32 = jnp.float32
BF16 = jnp.bfloat16

D_MODEL = 1024
HEAD_DIM = 64
ATTN_WIDTH = 512
RET_WIDTH = 512
N_HEADS = 8
PAIR = 2 * HEAD_DIM
N_PAIRS = N_HEADS // 2
ATTN_DILATIONS = (1, 4, 16)
ATTN_RADIUS = 64
N_BUCKETS = 32
REL_MAX_DIST = 1024
ROPE_BASE = 10000.0
N_GROUPS = 4
EXPERTS_PER_GROUP = 4
N_EXPERTS = 16
EXPERT_FF = 512
EPS = 1e-6
NEG_INF = -1e30
LOG2_E = math.log2(math.e)

LANES = 128
ATTN_TQ = 128
ATTN_W = 256
N_BIAS_VARIANTS = 8
RET_CHUNK = 256
ROUTE_TM = 512
ROUTE_TILES_PER_STEP = 2
SORT_ROWS = 592
CHUNK = 16
MOE_TM = 512
XS_W = D_MODEL + 2 * LANES
VMEM_LIMIT = 48 * 1024 * 1024


def _rms(x, gain):
    return x * lax.rsqrt(jnp.mean(x * x, axis=-1, keepdims=True) + EPS) * gain


def _inproj_kernel(x_ref, gain_ref, w_ref, cos_ref, sin_ref,
                   aq_ref, ak_ref, av_ref, rq_ref, rk_ref, rv_ref, rg_ref,
                   aq4_ref, ak4_ref, av4_ref, aq16_ref, ak16_ref, av16_ref, stage_ref, stage4_ref):
    tm = x_ref.shape[0]
    xn = _rms(x_ref[...], gain_ref[...]).astype(BF16)

    def emit(i, t, nat_ref, d4_ref, d16_ref):
        nat_ref[...] = t.astype(BF16)
        for hp in range(N_PAIRS):
            stage_ref[i, hp] = t[:, hp * PAIR:(hp + 1) * PAIR]
            for r4 in range(4):
                g4 = stage_ref[i, hp, pl.ds(r4, tm // 4, stride=4), :]
                d4_ref[hp, r4] = g4.astype(BF16)
                stage4_ref[i, hp, r4] = g4
                for j in range(4):
                    d16_ref[hp, r4 + 4 * j] = (
                        stage4_ref[i, hp, r4, pl.ds(j, tm // 16, stride=4), :].astype(BF16))

    def seg(i):
        return jnp.dot(xn, w_ref[:, i * 512:(i + 1) * 512], preferred_element_type=F32)

    def rotary(t):
        cos, sin = cos_ref[...], sin_ref[...]
        first_half = (lax.broadcasted_iota(jnp.int32, (1, LANES), 1) % HEAD_DIM) < HEAD_DIM // 2
        outs = []
        for j in range(t.shape[1] // LANES):
            tj = t[:, j * LANES:(j + 1) * LANES]
            partner = jnp.where(first_half, pltpu.roll(tj, LANES - 32, 1), pltpu.roll(tj, 32, 1))
            outs.append(tj * cos + partner * sin)
        return jnp.concatenate(outs, axis=1)

    emit(0, seg(0) * (HEAD_DIM ** -0.5 * LOG2_E), aq_ref, aq4_ref, aq16_ref)
    emit(1, seg(1), ak_ref, ak4_ref, ak16_ref)
    emit(2, seg(2), av_ref, av4_ref, av16_ref)
    rq_ref[...] = rotary(seg(3)).astype(BF16)
    rk_ref[...] = (rotary(seg(4)) * (HEAD_DIM ** -0.5)).astype(BF16)
    rv_ref[...] = seg(5).astype(BF16)
    rg_ref[...] = seg(6).astype(BF16)


def _inproj(x, gain, w_in, cos_t, sin_t, tm=512):
    b, s, d = x.shape
    n = w_in.shape[1]
    out = jax.ShapeDtypeStruct((b, s, 512), BF16)
    ospec = pl.BlockSpec((None, tm, 512), lambda si, bi: (bi, si, 0))

    def grouped(dil):
        shape = jax.ShapeDtypeStruct((b, N_PAIRS, dil, s // dil, PAIR), BF16)
        spec = pl.BlockSpec((None, N_PAIRS, dil, tm // dil, PAIR), lambda si, bi: (bi, 0, 0, si, 0))
        return [shape] * 3, [spec] * 3

    shapes4, specs4 = grouped(4)
    shapes16, specs16 = grouped(16)
    outs = pl.pallas_call(
        _inproj_kernel,
        grid=(s // tm, b),
        in_specs=[
            pl.BlockSpec((None, tm, d), lambda si, bi: (bi, si, 0)),
            pl.BlockSpec((1, d), lambda si, bi: (0, 0)),
            pl.BlockSpec((d, n), lambda si, bi: (0, 0)),
            pl.BlockSpec((tm, LANES), lambda si, bi: (si, 0)),
            pl.BlockSpec((tm, LANES), lambda si, bi: (si, 0)),
        ],
        out_specs=[ospec] * 7 + specs4 + specs16,
        out_shape=[out] * 7 + shapes4 + shapes16,
        scratch_shapes=[pltpu.VMEM((3, N_PAIRS, tm, PAIR), F32),
                        pltpu.VMEM((3, N_PAIRS, 4, tm // 4, PAIR), F32)],
        compiler_params=pltpu.CompilerParams(
            dimension_semantics=("arbitrary", "arbitrary"), vmem_limit_bytes=VMEM_LIMIT),
        name="inproj",
    )(x, gain, w_in, cos_t, sin_t)
    return outs[:7], [o.reshape(b, N_PAIRS, s, PAIR) for o in outs[7:]]


def _t5_bucket(rel):
    half = N_BUCKETS // 2
    max_exact = half // 2
    offset = jnp.where(rel > 0, half, 0)
    n = jnp.abs(rel)
    nf = jnp.maximum(n, 1).astype(F32)
    large = max_exact + (jnp.log(nf / max_exact) / math.log(REL_MAX_DIST / max_exact)
                         * (half - max_exact)).astype(jnp.int32)
    large = jnp.minimum(large, half - 1)
    return offset + jnp.where(n < max_exact, n, large)


def _attn_bias_rows(rel_bias):
    period = 2 * ATTN_W
    band = 2 * ATTN_RADIUS + 1
    rel = jnp.arange(-ATTN_RADIUS, ATTN_RADIUS + 1)
    rows = []
    for dil, offs in ((1, (0, 64, 128)), (4, (0, 64, 128)), (16, (0, 128))):
        vals = rel_bias[_t5_bucket(rel * dil)].astype(F32).T * LOG2_E
        for off in offs:
            lo = off - ATTN_RADIUS
            pad = jnp.full((N_HEADS, period - band), NEG_INF, F32)
            if lo >= 0:
                row = jnp.concatenate([pad[:, :lo], vals, pad[:, lo:]], axis=1)
            else:
                row = jnp.concatenate([vals[:, -lo:], pad, vals[:, :-lo]], axis=1)
            rows.append(row)
    v = jnp.stack(rows, axis=1)
    return v.reshape(N_PAIRS, 2 * N_BIAS_VARIANTS, period)


def _attention_kernel(q_ref, k_ref, v_ref, q4_ref, k4_ref, v4_ref, q16_ref, k16_ref, v16_ref,
                      rows_ref, o_ref, bias_ref, acc_ref, m_ref, l_ref, out_ref):
    s = q_ref.shape[0]
    n_tiles = s // ATTN_TQ
    lane = lax.broadcasted_iota(jnp.int32, (1, PAIR), 1)
    head0 = lane < HEAD_DIM

    @pl.when(pl.program_id(1) == 0)
    def _():
        col = lax.broadcasted_iota(jnp.int32, (ATTN_TQ, ATTN_W), 1)
        for idx in range(2 * N_BIAS_VARIANTS):
            gen = jnp.broadcast_to(rows_ref[idx:idx + 1, :], (ATTN_TQ, 2 * ATTN_W))
            tab = pltpu.roll(gen, 0, 1, stride=1, stride_axis=0)[:, :ATTN_W]
            var = idx % N_BIAS_VARIANTS
            if var >= 6:
                tab = jnp.where((col // ATTN_TQ) == var - 6, tab, NEG_INF)
            head = idx // N_BIAS_VARIANTS
            bias_ref[var, head * ATTN_TQ:(head + 1) * ATTN_TQ, :] = tab

    def run_branch(bi, dil, qs_ref, ks_ref, vs_ref):
        sub_len = s // dil
        tiles_per_sub = sub_len // ATTN_TQ

        def tile(t, carry):
            q0 = pl.multiple_of(t * ATTN_TQ, ATTN_TQ)
            if tiles_per_sub == 1:
                ws = pl.multiple_of((t // 2) * ATTN_W, ATTN_W)
                var = 6 + t % 2
            else:
                pos = t % tiles_per_sub
                sub_lo = (t // tiles_per_sub) * sub_len
                ws = jnp.clip(q0 - 64, sub_lo, sub_lo + sub_len - ATTN_W)
                ws = pl.multiple_of(ws, 64)
                var = jnp.where(pos == 0, 0, jnp.where(pos == tiles_per_sub - 1, 2, 1)) + 3 * bi
            q = qs_ref[pl.ds(q0, ATTN_TQ), :]
            k = ks_ref[pl.ds(ws, ATTN_W), :]
            v = vs_ref[pl.ds(ws, ATTN_W), :]
            q2 = jnp.concatenate([jnp.where(head0, q, jnp.zeros_like(q)),
                                  jnp.where(head0, jnp.zeros_like(q), q)], axis=0)
            sc = lax.dot_general(q2, k, (((1,), (1,)), ((), ())), preferred_element_type=F32)
            sc = sc + bias_ref[var]
            m = jnp.max(sc, axis=-1, keepdims=True)
            p = jnp.exp2(sc - m).astype(BF16)
            o = jnp.dot(p, jnp.concatenate([v, jnp.ones_like(v)], axis=1), preferred_element_type=F32)
            l = o[:, PAIR:]
            outs, ms, ls = (o[:ATTN_TQ, :PAIR], o[ATTN_TQ:, :PAIR]), (m[:ATTN_TQ], m[ATTN_TQ:]), (l[:ATTN_TQ], l[ATTN_TQ:])
            if dil == 1:
                dst = pl.ds(q0, ATTN_TQ)
            elif dil == 4:
                dst = pl.ds((t % 4) * (4 * ATTN_TQ) + t // 4, ATTN_TQ, stride=4)
            else:
                dst = pl.ds((t % 4) * (s // 4) + t // 4, ATTN_TQ, stride=4)
            acc_ref[bi, dst, :] = jnp.where(head0, outs[0], outs[1])
            m_ref[bi, dst, :] = jnp.where(head0, ms[0], ms[1])
            l_ref[bi, dst, :] = jnp.where(head0, ls[0], ls[1])
            return carry

        lax.fori_loop(0, n_tiles, tile, 0, unroll=16)

    run_branch(0, 1, q_ref, k_ref, v_ref)
    run_branch(1, 4, q4_ref, k4_ref, v4_ref)
    run_branch(2, 16, q16_ref, k16_ref, v16_ref)

    rows = ATTN_TQ
    for r4 in range(4):
        for blk in range(s // (4 * rows)):
            nat = pl.ds(r4 + 4 * rows * blk, rows, stride=4)
            sl = (nat, nat, pl.ds(r4 * (s // 4) + rows * blk, rows))
            m = [m_ref[bi, sl[bi], :] for bi in range(3)]
            mx = jnp.maximum(jnp.maximum(m[0], m[1]), m[2])
            num = jnp.zeros((rows, PAIR), F32)
            den = jnp.zeros((rows, PAIR), F32)
            for bi in range(3):
                e = jnp.exp2(m[bi] - mx)
                num = num + e * acc_ref[bi, sl[bi], :]
                den = den + e * l_ref[bi, sl[bi], :]
            out_ref[nat, :] = num / den
    o_ref[...] = out_ref[...].astype(o_ref.dtype)


def _attention(qkv, qkv_grouped, bias_rows):
    b, s, _ = qkv[0].shape
    spec = pl.BlockSpec((None, s, PAIR), lambda hp, bi: (bi, 0, hp))
    gspec = pl.BlockSpec((None, None, s, PAIR), lambda hp, bi: (bi, hp, 0, 0))
    return pl.pallas_call(
        _attention_kernel,
        grid=(N_PAIRS, b),
        in_specs=[spec] * 3 + [gspec] * 6 + [
            pl.BlockSpec((None, 2 * N_BIAS_VARIANTS, 2 * ATTN_W), lambda hp, bi: (hp, 0, 0))],
        out_specs=spec,
        out_shape=jax.ShapeDtypeStruct((b, s, ATTN_WIDTH), BF16),
        scratch_shapes=[
            pltpu.VMEM((N_BIAS_VARIANTS, 2 * ATTN_TQ, ATTN_W), F32),
            pltpu.VMEM((3, s, PAIR), F32),
            pltpu.VMEM((3, s, PAIR), F32),
            pltpu.VMEM((3, s, PAIR), F32),
            pltpu.VMEM((s, PAIR), F32),
        ],
        compiler_params=pltpu.CompilerParams(
            dimension_semantics=("arbitrary", "arbitrary"), vmem_limit_bytes=VMEM_LIMIT),
        name="attention",
    )(*qkv, *qkv_grouped, bias_rows)


def _retention_tables(decay_fwd, decay_bwd):
    c = RET_CHUNK
    lg_f = -jnp.exp(decay_fwd.astype(F32))
    lg_b = -jnp.exp(decay_bwd.astype(F32))
    idx = jnp.arange(c, dtype=F32)
    rel = idx[:, None] - idx[None, :]
    dmat = jnp.where(rel >= 0,
                     jnp.exp(lg_f[:, None, None] * jnp.maximum(rel, 0.0)[None]),
                     jnp.exp(lg_b[:, None, None] * jnp.maximum(-rel, 0.0)[None]))
    dmat = dmat.reshape(N_PAIRS, 2, c, c)

    def lanes(v):
        v = v.reshape(N_PAIRS, 2, -1)
        return jnp.repeat(jnp.transpose(v, (0, 2, 1)), HEAD_DIM, axis=2)

    vec = jnp.stack([
        lanes(jnp.exp(lg_f[:, None] * (idx + 1.0)[None])),
        lanes(jnp.exp(lg_f[:, None] * (c - 1.0 - idx)[None])),
        lanes(jnp.exp(lg_b[:, None] * (c - idx)[None])),
        lanes(jnp.exp(lg_b[:, None] * idx[None])),
    ], axis=1)
    same_head = (jnp.arange(PAIR)[:, None] // HEAD_DIM) == (jnp.arange(PAIR)[None, :] // HEAD_DIM)
    cd = jnp.stack([lanes(jnp.exp(lg_f * c)[:, None]), lanes(jnp.exp(lg_b * c)[:, None])], axis=1)
    cd = jnp.transpose(cd, (0, 1, 3, 2)) * same_head[None, None].astype(F32)
    return dmat, vec, cd


def _retention_kernel(q_ref, k_ref, v_ref, g_ref, dmat_ref, vec_ref, cd_ref, o_ref, kv_ref, st_ref):
    s = q_ref.shape[0]
    c = RET_CHUNK
    nc = s // c
    lane = lax.broadcasted_iota(jnp.int32, (1, PAIR), 1)
    head0 = lane < HEAD_DIM
    same_head = ((lax.broadcasted_iota(jnp.int32, (PAIR, PAIR), 0) // HEAD_DIM)
                 == (lax.broadcasted_iota(jnp.int32, (PAIR, PAIR), 1) // HEAD_DIM))

    same_head2 = jnp.concatenate([same_head, same_head], axis=0)
    for n in range(nc):
        rows = slice(n * c, (n + 1) * c)
        kf = k_ref[rows, :].astype(F32)
        kcat = jnp.concatenate([(kf * vec_ref[1]).astype(BF16), (kf * vec_ref[3]).astype(BF16)], axis=1)
        kv = lax.dot_general(kcat, v_ref[rows, :], (((0,), (0,)), ((), ())), preferred_element_type=F32)
        kv_ref[n] = jnp.where(same_head2, kv, 0.0)

    state = jnp.zeros((PAIR, PAIR), F32)
    for n in range(nc):
        st_ref[n, :PAIR, :] = state.astype(BF16)
        state = state * cd_ref[0] + kv_ref[n, :PAIR, :]
    state = jnp.zeros((PAIR, PAIR), F32)
    for n in reversed(range(nc)):
        st_ref[n, PAIR:, :] = state.astype(BF16)
        state = state * cd_ref[1] + kv_ref[n, PAIR:, :]

    for n in range(nc):
        rows = slice(n * c, (n + 1) * c)
        q, k, v = q_ref[rows, :], k_ref[rows, :], v_ref[rows, :]
        outs = []
        for h in range(2):
            qh = jnp.where(head0 if h == 0 else ~head0, q, jnp.zeros_like(q))
            sc = lax.dot_general(qh, k, (((1,), (1,)), ((), ())), preferred_element_type=F32)
            outs.append(jnp.dot((sc * dmat_ref[h]).astype(BF16), v, preferred_element_type=F32))
        qf = q.astype(F32)
        qcat = jnp.concatenate([(qf * vec_ref[0]).astype(BF16), (qf * vec_ref[2]).astype(BF16)], axis=1)
        y = jnp.where(head0, outs[0], outs[1]) + jnp.dot(qcat, st_ref[n], preferred_element_type=F32)
        y2 = y * y
        ms0 = jnp.sum(jnp.where(head0, y2, 0.0), axis=-1, keepdims=True)
        ms1 = jnp.sum(jnp.where(head0, 0.0, y2), axis=-1, keepdims=True)
        ms = jnp.where(head0, ms0, ms1) * (1.0 / HEAD_DIM)
        g = g_ref[rows, :].astype(F32)
        o_ref[rows, :] = (y * lax.rsqrt(ms + EPS) * (g * jax.nn.sigmoid(g))).astype(o_ref.dtype)


def _retention(rq, rk, rv, rg, dmat, vec, cd, weights_f32):
    b, s, _ = rq.shape
    c = RET_CHUNK
    n_steps = N_PAIRS * b
    spec = pl.BlockSpec((None, s, PAIR), lambda hp, bi: (bi, 0, hp))
    slabs = [w.reshape(n_steps, -1, w.shape[-1]) for w in weights_f32]
    slab_specs = [pl.BlockSpec((None,) + w.shape[1:], lambda hp, bi: (hp * b + bi, 0, 0)) for w in slabs]

    def kernel(*refs):
        n_in, n_w = 7, len(slabs)
        _retention_kernel(*refs[:n_in], refs[n_in + n_w], *refs[n_in + 2 * n_w + 1:])
        for src, dst in zip(refs[n_in:n_in + n_w], refs[n_in + n_w + 1:n_in + 2 * n_w + 1]):
            dst[...] = src[...].astype(BF16)

    outs = pl.pallas_call(
        kernel,
        grid=(N_PAIRS, b),
        in_specs=[spec, spec, spec, spec,
                  pl.BlockSpec((None, 2, c, c), lambda hp, bi: (hp, 0, 0, 0)),
                  pl.BlockSpec((None, 4, c, PAIR), lambda hp, bi: (hp, 0, 0, 0)),
                  pl.BlockSpec((None, 2, PAIR, PAIR), lambda hp, bi: (hp, 0, 0, 0))] + slab_specs,
        out_specs=[spec] + slab_specs,
        out_shape=[jax.ShapeDtypeStruct((b, s, RET_WIDTH), BF16)]
        + [jax.ShapeDtypeStruct(w.shape, BF16) for w in slabs],
        scratch_shapes=[pltpu.VMEM((s // c, 2 * PAIR, PAIR), F32),
                        pltpu.VMEM((s // c, 2 * PAIR, PAIR), BF16)],
        compiler_params=pltpu.CompilerParams(
            dimension_semantics=("arbitrary", "arbitrary"), vmem_limit_bytes=VMEM_LIMIT),
        name="retention",
    )(rq, rk, rv, rg, dmat, vec, cd, *slabs)
    return outs[0], [o.reshape(w.shape) for o, w in zip(outs[1:], weights_f32)]


def _route_tile(sub, attn_ref, ret_ref, x_ref, wo_ref, again_ref, fgain_ref,
                rw_ref, rb_ref, h_ref, xs_ref, dest_ref, cnt_ref):
    tm = ROUTE_TM
    rows = slice(sub * tm, (sub + 1) * tm)
    a = _rms(attn_ref[rows, :].astype(F32), again_ref[...]).astype(BF16)
    mixed = jnp.concatenate([a, ret_ref[rows, :]], axis=1)
    h = x_ref[rows, :] + jnp.dot(mixed, wo_ref[...], preferred_element_type=F32)
    h_ref[rows, :] = h.astype(h_ref.dtype)
    hn = _rms(h, fgain_ref[...]).astype(BF16)

    logits = jnp.dot(hn, rw_ref[...], preferred_element_type=F32) + rb_ref[...]
    lane = lax.broadcasted_iota(jnp.int32, logits.shape, 1)
    big = jnp.int32(LANES)

    def first_argmax(vals):
        top = jnp.max(vals, axis=-1, keepdims=True)
        return top, jnp.min(jnp.where(vals == top, lane, big), axis=-1, keepdims=True)

    gl = jnp.where((lane >= N_EXPERTS) & (lane < N_EXPERTS + N_GROUPS), logits, -jnp.inf)
    gmax, gidx = first_argmax(gl)
    p_group = 1.0 / jnp.sum(jnp.exp(gl - gmax), axis=-1, keepdims=True)
    grp = gidx - N_EXPERTS
    lo = grp * EXPERTS_PER_GROUP
    el = jnp.where((lane >= lo) & (lane < lo + EXPERTS_PER_GROUP), logits, -jnp.inf)
    v1, i1 = first_argmax(el)
    v2, i2 = first_argmax(jnp.where(lane == i1, -jnp.inf, el))
    e2 = jnp.exp(v2 - v1)
    p1 = p_group / (1.0 + e2)
    gates = jnp.where(lane == i1, p1, jnp.where(lane == i2, p1 * e2, 0.0))

    onehot = jnp.where(lane == grp, 1.0, 0.0)
    r_i = lax.broadcasted_iota(jnp.int32, (tm, tm), 0)
    c_i = lax.broadcasted_iota(jnp.int32, (tm, tm), 1)
    before = jnp.where(c_i < r_i, 1.0, 0.0).astype(BF16)
    rank = jnp.dot(before, onehot.astype(BF16), preferred_element_type=F32)
    cnt = jnp.sum(onehot, axis=0, keepdims=True)
    c16 = jnp.broadcast_to(jnp.ceil(cnt * (1.0 / CHUNK)) * CHUNK, (8, LANES))
    lane8 = lax.broadcasted_iota(jnp.int32, (8, LANES), 1)
    start = jnp.zeros((8, LANES), F32)
    for sft in range(1, N_GROUPS):
        start = start + jnp.where(lane8 >= sft, pltpu.roll(c16, sft, 1), 0.0)
    dest = jnp.sum(onehot * (start[0:1] + rank), axis=-1, keepdims=True)
    dest_ref[rows, :] = jnp.broadcast_to(dest, (tm, LANES))
    cnt_ref[sub] = jnp.where(lane8 < N_GROUPS, c16, pltpu.roll(start, N_GROUPS, 1)).astype(jnp.int32)

    onehot_t = onehot.T
    rank_t = lax.dot_general(onehot_t.astype(BF16), before, (((1,), (1,)), ((), ())),
                             preferred_element_type=F32)
    g_row = lax.broadcasted_iota(jnp.int32, (LANES, LANES), 0)
    g_col = lax.broadcasted_iota(jnp.int32, (LANES, LANES), 1)
    start_t = jnp.sum(jnp.where(g_col < g_row, c16[0:1], 0.0), axis=-1, keepdims=True)
    dest_t = jnp.sum(onehot_t * (start_t + rank_t), axis=0, keepdims=True)
    perm = jnp.where(lax.broadcasted_iota(jnp.int32, (SORT_ROWS, tm), 0) == dest_t.astype(jnp.int32),
                     1.0, 0.0).astype(BF16)
    g_hi = gates.astype(BF16)
    g_lo = (gates - g_hi.astype(F32)).astype(BF16)
    payload = jnp.concatenate([hn, g_hi, g_lo], axis=1)
    xs_ref[sub * SORT_ROWS:(sub + 1) * SORT_ROWS, :] = (
        jnp.dot(perm, payload, preferred_element_type=F32).astype(BF16))


def _outproj_kernel(*refs):
    for sub in range(ROUTE_TILES_PER_STEP):
        _route_tile(sub, *refs)


def _outproj(attn, ret, x, w_o, again, fgain, rw, rb):
    t, d = x.shape
    per_step = ROUTE_TILES_PER_STEP
    tm = ROUTE_TM * per_step
    nt = t // ROUTE_TM
    row = lambda w: pl.BlockSpec((tm, w), lambda i: (i, 0))
    full = lambda r, c: pl.BlockSpec((r, c), lambda i: (0, 0))
    return pl.pallas_call(
        _outproj_kernel,
        grid=(nt // per_step,),
        in_specs=[row(ATTN_WIDTH), row(RET_WIDTH), row(d), full(ATTN_WIDTH + RET_WIDTH, d),
                  full(1, ATTN_WIDTH), full(1, d), full(d, LANES), full(1, LANES)],
        out_specs=[row(d), pl.BlockSpec((per_step * SORT_ROWS, XS_W), lambda i: (i, 0)), row(LANES),
                   pl.BlockSpec((per_step, 8, LANES), lambda i: (i, 0, 0))],
        out_shape=[jax.ShapeDtypeStruct((t, d), BF16),
                   jax.ShapeDtypeStruct((nt * SORT_ROWS, XS_W), BF16),
                   jax.ShapeDtypeStruct((t, LANES), F32),
                   jax.ShapeDtypeStruct((nt, 8, LANES), jnp.int32)],
        compiler_params=pltpu.CompilerParams(
            dimension_semantics=("arbitrary",), vmem_limit_bytes=VMEM_LIMIT),
        name="outproj",
    )(attn, ret, x, w_o, again, fgain, rw, rb)


def _dispatch_tables(cnt, n_moe_tiles):
    nt = cnt.shape[0]
    cpt = MOE_TM // CHUNK

    n_slots = n_moe_tiles * cpt
    slot_rows = -(-n_slots // (8 * LANES)) * 8

    def schedule_kernel(seg_ref, grp_ref, used_ref, src_ref, tail_ref):
        slot = (lax.broadcasted_iota(jnp.int32, (slot_rows, LANES), 0) * LANES
                + lax.broadcasted_iota(jnp.int32, (slot_rows, LANES), 1))
        src = jnp.full((slot_rows, LANES), -1, jnp.int32)
        pos = jnp.int32(0)
        for g in range(N_GROUPS):
            def tile_body(i, carry):
                p, src = carry
                n = seg_ref[i, g] // CHUNK
                first = (i * SORT_ROWS + seg_ref[i, N_GROUPS + g]) // CHUNK
                src = jnp.where((slot >= p) & (slot < p + n), slot + (first - p), src)
                return p + n, src
            end, src = lax.fori_loop(0, nt, tile_body, (pos, src))
            padded = ((end + cpt - 1) // cpt) * cpt

            def mark_body(m, carry):
                grp_ref[m] = g
                used_ref[m] = 1
                return carry
            lax.fori_loop(pos // cpt, padded // cpt, mark_body, 0)
            pos = padded

        def idle_body(m, carry):
            grp_ref[m] = N_GROUPS - 1
            used_ref[m] = 0
            return carry
        lax.fori_loop(pos // cpt, n_moe_tiles, idle_body, 0)
        src_ref[...] = src

        def tail_body(i, carry):
            rows = seg_ref[i, 0]
            for g in range(1, N_GROUPS):
                rows = rows + seg_ref[i, g]
            tail_ref[i] = rows // CHUNK
            return carry
        lax.fori_loop(0, nt, tail_body, 0)

    smem = lambda: pl.BlockSpec(memory_space=pltpu.SMEM)
    tile_group, tile_used, src, tile_tail = pl.pallas_call(
        schedule_kernel,
        in_specs=[smem()],
        out_specs=[smem(), smem(), pl.BlockSpec(memory_space=pltpu.VMEM), smem()],
        out_shape=[jax.ShapeDtypeStruct((n_moe_tiles,), jnp.int32),
                   jax.ShapeDtypeStruct((n_moe_tiles,), jnp.int32),
                   jax.ShapeDtypeStruct((slot_rows, LANES), jnp.int32),
                   jax.ShapeDtypeStruct((nt,), jnp.int32)],
        name="schedule",
    )(cnt[:, 0, :2 * N_GROUPS])
    return tile_group, tile_used, src.reshape(-1)[:n_slots], tile_tail


def _moe_kernel(grp_ref, used_ref, src_ref, tail_ref, xs_hbm, w1_ref, w3_ref, w2_ref, ys_hbm,
                xbuf, obuf, zbuf, in_sem, out_sem, zero_sem):
    m = pl.program_id(0)
    n_tiles = pl.num_programs(0)
    cpt = MOE_TM // CHUNK
    cps = SORT_ROWS // CHUNK
    slot = m % 2
    zero_chunk = cps - 1

    def rows(c):
        return pl.ds(pl.multiple_of(c * CHUNK, CHUNK), CHUNK)

    def zero_tails(wait):
        def tile_body(i, carry):
            def body(c, carry2):
                cp = pltpu.make_async_copy(zbuf, ys_hbm.at[rows(i * cps + c), :], zero_sem)
                cp.wait() if wait else cp.start()
                return carry2
            lax.fori_loop(tail_ref[i], cps, body, 0)
            return carry
        lax.fori_loop(0, tail_ref.shape[0], tile_body, 0)

    def gather_start(tile, sl):
        for c in range(cpt):
            src = src_ref[tile * cpt + c]
            src = jnp.where(src < 0, zero_chunk, src)
            pltpu.make_async_copy(xs_hbm.at[rows(src), :], xbuf.at[sl, c * CHUNK:(c + 1) * CHUNK, :],
                                  in_sem.at[sl]).start()

    def gather_wait(sl):
        pltpu.make_async_copy(xs_hbm.at[0:MOE_TM, :], xbuf.at[sl], in_sem.at[sl]).wait()

    def scatter(tile, sl, wait):
        full = src_ref[tile * cpt + cpt - 1] >= 0

        def chunk_copy(c, src):
            return pltpu.make_async_copy(obuf.at[sl, c * CHUNK:(c + 1) * CHUNK, :], ys_hbm.at[rows(src), :],
                                         out_sem.at[sl])

        @pl.when(full)
        def _():
            if wait:
                pltpu.make_async_copy(obuf.at[sl], ys_hbm.at[0:MOE_TM, :], out_sem.at[sl]).wait()
            else:
                for c in range(cpt):
                    chunk_copy(c, src_ref[tile * cpt + c]).start()

        @pl.when(jnp.logical_not(full))
        def _():
            for c in range(cpt):
                src = src_ref[tile * cpt + c]

                @pl.when(src >= 0)
                def _():
                    cp = chunk_copy(c, src)
                    cp.wait() if wait else cp.start()

    @pl.when(m == 0)
    def _():
        gather_start(0, 0)
        zbuf[...] = jnp.zeros(zbuf.shape, zbuf.dtype)
        zero_tails(False)

    prev_used = used_ref[jnp.maximum(m - 1, 0)] > 0

    @pl.when((m == 0) | prev_used)
    def _():
        gather_wait(slot)

    @pl.when((m >= 2) & (used_ref[jnp.maximum(m - 2, 0)] > 0))
    def _():
        scatter(m - 2, slot, True)

    @pl.when(used_ref[m] > 0)
    def _():
        gather_start(jnp.minimum(m + 1, n_tiles - 1), 1 - slot)
        x = xbuf[slot, :, :D_MODEL]
        gate = (xbuf[slot, :, D_MODEL:D_MODEL + LANES].astype(F32)
                + xbuf[slot, :, D_MODEL + LANES:].astype(F32))
        lane = lax.broadcasted_iota(jnp.int32, gate.shape, 1)
        base = grp_ref[m] * EXPERTS_PER_GROUP
        acc = jnp.zeros((MOE_TM, D_MODEL), F32)
        for j in range(EXPERTS_PER_GROUP):
            a = jnp.dot(x, w1_ref[j], preferred_element_type=F32)
            b = jnp.dot(x, w3_ref[j], preferred_element_type=F32)
            gj = jnp.sum(jnp.where(lane == base + j, gate, 0.0), axis=-1, keepdims=True)
            hid = (a * jax.nn.sigmoid(a) * b * gj).astype(BF16)
            acc = acc + jnp.dot(hid, w2_ref[j], preferred_element_type=F32)
        obuf[slot] = acc.astype(BF16)
        scatter(m, slot, False)

    @pl.when(m == n_tiles - 1)
    def _():
        zero_tails(True)

        @pl.when(used_ref[m] > 0)
        def _():
            gather_wait(1 - slot)
            scatter(m, slot, True)

        @pl.when((m >= 1) & prev_used)
        def _():
            scatter(m - 1, 1 - slot, True)


def _moe(xs, w1, w3, w2, tile_group, tile_used, src_chunk, tile_tail):
    n_moe_tiles = tile_group.shape[0]
    rows = xs.shape[0]
    d = D_MODEL
    wspec = lambda r, c: pl.BlockSpec((EXPERTS_PER_GROUP, r, c), lambda m, grp, *_: (grp[m], 0, 0))
    return pl.pallas_call(
        _moe_kernel,
        grid_spec=pltpu.PrefetchScalarGridSpec(
            num_scalar_prefetch=4,
            grid=(n_moe_tiles,),
            in_specs=[pl.BlockSpec(memory_space=pl.ANY),
                      wspec(d, EXPERT_FF), wspec(d, EXPERT_FF), wspec(EXPERT_FF, d)],
            out_specs=pl.BlockSpec(memory_space=pl.ANY),
            scratch_shapes=[pltpu.VMEM((2, MOE_TM, XS_W), BF16),
                            pltpu.VMEM((2, MOE_TM, d), BF16),
                            pltpu.VMEM((CHUNK, d), BF16),
                            pltpu.SemaphoreType.DMA((2,)),
                            pltpu.SemaphoreType.DMA((2,)),
                            pltpu.SemaphoreType.DMA(())]),
        out_shape=jax.ShapeDtypeStruct((rows, d), BF16),
        compiler_params=pltpu.CompilerParams(
            dimension_semantics=("arbitrary",), vmem_limit_bytes=VMEM_LIMIT),
        name="moe",
    )(tile_group, tile_used, src_chunk, tile_tail, xs, w1, w3, w2)


def _combine_kernel(ys_ref, h_ref, dest_ref, gain_ref, o_ref):
    tm = h_ref.shape[0]
    dest = dest_ref[:, 0:1].astype(jnp.int32)
    perm_t = jnp.where(lax.broadcasted_iota(jnp.int32, (tm, SORT_ROWS), 1) == dest, 1.0, 0.0).astype(BF16)
    moe = jnp.dot(perm_t, ys_ref[...], preferred_element_type=F32)
    o_ref[...] = _rms(h_ref[...].astype(F32) + moe, gain_ref[...])


def _combine(ys, h, dest, gain):
    t, d = h.shape
    tm = ROUTE_TM
    return pl.pallas_call(
        _combine_kernel,
        grid=(t // tm,),
        in_specs=[pl.BlockSpec((SORT_ROWS, d), lambda i: (i, 0)),
                  pl.BlockSpec((tm, d), lambda i: (i, 0)),
                  pl.BlockSpec((tm, LANES), lambda i: (i, 0)),
                  pl.BlockSpec((1, d), lambda i: (0, 0))],
        out_specs=pl.BlockSpec((tm, d), lambda i: (i, 0)),
        out_shape=jax.ShapeDtypeStruct((t, d), F32),
        compiler_params=pltpu.CompilerParams(
            dimension_semantics=("arbitrary",), vmem_limit_bytes=VMEM_LIMIT),
        name="combine",
    )(ys, h, dest, gain)


def _rotary_tables(s):
    half = HEAD_DIM // 2
    inv = ROPE_BASE ** (-jnp.arange(half, dtype=F32) / half)
    ang = jnp.arange(s, dtype=F32)[:, None] * inv[None, :]
    cos, sin = jnp.cos(ang), jnp.sin(ang)
    cos_t = jnp.tile(jnp.concatenate([cos, cos], axis=-1), (1, LANES // HEAD_DIM))
    sin_t = jnp.tile(jnp.concatenate([-sin, sin], axis=-1), (1, LANES // HEAD_DIM))
    return cos_t, sin_t


def kernel(x, w_in, w_out, norm_mix, norm_ffn, norm_final, attn_out_gain, rel_bias, ret_decay_fwd, ret_decay_bwd, router_group_w, router_group_b, router_expert_w, router_expert_b, expert_w1, expert_w3, expert_w2):
    b, s, d = x.shape
    depth = w_in.shape[0]
    cos_t, sin_t = _rotary_tables(s)
    bias_rows = _attn_bias_rows(rel_bias)
    h = x
    for layer in range(depth):
        (aq, ak, av, rq, rk, rv, rg), qkv_grouped = _inproj(
            h, norm_mix[layer][None], w_in[layer].astype(BF16), cos_t, sin_t)
        attn = _attention((aq, ak, av), qkv_grouped, bias_rows)
        ret, (w1, w3, w2, w_o) = _retention(
            rq, rk, rv, rg, *_retention_tables(ret_decay_fwd[layer], ret_decay_bwd[layer]),
            (expert_w1[layer], expert_w3[layer], expert_w2[layer], w_out[layer]))

        rw = jnp.concatenate(
            [jnp.transpose(router_expert_w[layer], (1, 0, 2)).reshape(d, N_EXPERTS),
             router_group_w[layer],
             jnp.zeros((d, LANES - N_EXPERTS - N_GROUPS), F32)], axis=1).astype(BF16)
        rb = jnp.concatenate(
            [router_expert_b[layer].reshape(N_EXPERTS), router_group_b[layer],
             jnp.zeros((LANES - N_EXPERTS - N_GROUPS,), F32)])[None].astype(F32)
        h1, xs, dest, cnt = _outproj(
            attn.reshape(b * s, ATTN_WIDTH), ret.reshape(b * s, RET_WIDTH), h.reshape(b * s, d),
            w_o, attn_out_gain[layer][None], norm_ffn[layer][None],
            rw, rb)
        n_route_tiles = (b * s) // ROUTE_TM
        n_moe_tiles = (b * s + n_route_tiles * N_GROUPS * (CHUNK - 1)) // MOE_TM + N_GROUPS
        schedule = _dispatch_tables(cnt, n_moe_tiles)
        ys = _moe(xs, w1, w3, w2, *schedule)
        assert depth == 1, "the combine kernel fuses the final norm, so it must run on the last layer"
        h = _combine(ys, h1, dest, norm_final[None]).reshape(b, s, d)
    return h
```

```python
import functools
import math

import jax
import jax.numpy as jnp
from jax import lax
from jax.experimental import pallas as pl
from jax.experimental.pallas import tpu as pltpu

F32 = jnp.float32
BF16 = jnp.bfloat16

D_MODEL = 1024
HEAD_DIM = 64
ATTN_WIDTH = 512
RET_WIDTH = 512
N_HEADS = 8
PAIR = 2 * HEAD_DIM
N_PAIRS = N_HEADS // 2
ATTN_DILATIONS = (1, 4, 16)
ATTN_RADIUS = 64
N_BUCKETS = 32
REL_MAX_DIST = 1024
ROPE_BASE = 10000.0
N_GROUPS = 4
EXPERTS_PER_GROUP = 4
N_EXPERTS = 16
EXPERT_FF = 512
EPS = 1e-6
NEG_INF = -1e30
LOG2_E = math.log2(math.e)

LANES = 128
ATTN_TQ = 128
ATTN_W = 256
N_BIAS_VARIANTS = 8
RET_CHUNK = 256
ROUTE_TM = 512
ROUTE_TILES_PER_STEP = 2
SORT_ROWS = 592
CHUNK = 16
MOE_TM = 512
XS_W = D_MODEL + 2 * LANES
VMEM_LIMIT = 48 * 1024 * 1024


def _rms(x, gain):
    return x * lax.rsqrt(jnp.mean(x * x, axis=-1, keepdims=True) + EPS) * gain


def _inproj_kernel(x_ref, gain_ref, w_ref, cos_ref, sin_ref,
                   aq_ref, ak_ref, av_ref, rq_ref, rk_ref, rv_ref, rg_ref,
                   aq4_ref, ak4_ref, av4_ref, aq16_ref, ak16_ref, av16_ref, stage_ref, stage4_ref):
    tm = x_ref.shape[0]
    xn = _rms(x_ref[...], gain_ref[...]).astype(BF16)

    def emit(i, t, nat_ref, d4_ref, d16_ref):
        nat_ref[...] = t.astype(BF16)
        for hp in range(N_PAIRS):
            stage_ref[i, hp] = t[:, hp * PAIR:(hp + 1) * PAIR]
            for r4 in range(4):
                g4 = stage_ref[i, hp, pl.ds(r4, tm // 4, stride=4), :]
                d4_ref[hp, r4] = g4.astype(BF16)
                stage4_ref[i, hp, r4] = g4
                for j in range(4):
                    d16_ref[hp, r4 + 4 * j] = (
                        stage4_ref[i, hp, r4, pl.ds(j, tm // 16, stride=4), :].astype(BF16))

    def seg(i):
        return jnp.dot(xn, w_ref[:, i * 512:(i + 1) * 512], preferred_element_type=F32)

    def rotary(t):
        cos, sin = cos_ref[...], sin_ref[...]
        first_half = (lax.broadcasted_iota(jnp.int32, (1, LANES), 1) % HEAD_DIM) < HEAD_DIM // 2
        outs = []
        for j in range(t.shape[1] // LANES):
            tj = t[:, j * LANES:(j + 1) * LANES]
            partner = jnp.where(first_half, pltpu.roll(tj, LANES - 32, 1), pltpu.roll(tj, 32, 1))
            outs.append(tj * cos + partner * sin)
        return jnp.concatenate(outs, axis=1)

    emit(0, seg(0) * (HEAD_DIM ** -0.5 * LOG2_E), aq_ref, aq4_ref, aq16_ref)
    emit(1, seg(1), ak_ref, ak4_ref, ak16_ref)
    emit(2, seg(2), av_ref, av4_ref, av16_ref)
    rq_ref[...] = rotary(seg(3)).astype(BF16)
    rk_ref[...] = (rotary(seg(4)) * (HEAD_DIM ** -0.5)).astype(BF16)
    rv_ref[...] = seg(5).astype(BF16)
    rg_ref[...] = seg(6).astype(BF16)


def _inproj(x, gain, w_in, cos_t, sin_t, tm=512):
    b, s, d = x.shape
    n = w_in.shape[1]
    out = jax.ShapeDtypeStruct((b, s, 512), BF16)
    ospec = pl.BlockSpec((None, tm, 512), lambda si, bi: (bi, si, 0))

    def grouped(dil):
        shape = jax.ShapeDtypeStruct((b, N_PAIRS, dil, s // dil, PAIR), BF16)
        spec = pl.BlockSpec((None, N_PAIRS, dil, tm // dil, PAIR), lambda si, bi: (bi, 0, 0, si, 0))
        return [shape] * 3, [spec] * 3

    shapes4, specs4 = grouped(4)
    shapes16, specs16 = grouped(16)
    outs = pl.pallas_call(
        _inproj_kernel,
        grid=(s // tm, b),
        in_specs=[
            pl.BlockSpec((None, tm, d), lambda si, bi: (bi, si, 0)),
            pl.BlockSpec((1, d), lambda si, bi: (0, 0)),
            pl.BlockSpec((d, n), lambda si, bi: (0, 0)),
            pl.BlockSpec((tm, LANES), lambda si, bi: (si, 0)),
            pl.BlockSpec((tm, LANES), lambda si, bi: (si, 0)),
        ],
        out_specs=[ospec] * 7 + specs4 + specs16,
        out_shape=[out] * 7 + shapes4 + shapes16,
        scratch_shapes=[pltpu.VMEM((3, N_PAIRS, tm, PAIR), F32),
                        pltpu.VMEM((3, N_PAIRS, 4, tm // 4, PAIR), F32)],
        compiler_params=pltpu.CompilerParams(
            dimension_semantics=("arbitrary", "arbitrary"), vmem_limit_bytes=VMEM_LIMIT),
        name="inproj",
    )(x, gain, w_in, cos_t, sin_t)
    return outs[:7], [o.reshape(b, N_PAIRS, s, PAIR) for o in outs[7:]]


def _t5_bucket(rel):
    half = N_BUCKETS // 2
    max_exact = half // 2
    offset = jnp.where(rel > 0, half, 0)
    n = jnp.abs(rel)
    nf = jnp.maximum(n, 1).astype(F32)
    large = max_exact + (jnp.log(nf / max_exact) / math.log(REL_MAX_DIST / max_exact)
                         * (half - max_exact)).astype(jnp.int32)
    large = jnp.minimum(large, half - 1)
    return offset + jnp.where(n < max_exact, n, large)


def _attn_bias_rows(rel_bias):
    period = 2 * ATTN_W
    band = 2 * ATTN_RADIUS + 1
    rel = jnp.arange(-ATTN_RADIUS, ATTN_RADIUS + 1)
    rows = []
    for dil, offs in ((1, (0, 64, 128)), (4, (0, 64, 128)), (16, (0, 128))):
        vals = rel_bias[_t5_bucket(rel * dil)].astype(F32).T * LOG2_E
        for off in offs:
            lo = off - ATTN_RADIUS
            pad = jnp.full((N_HEADS, period - band), NEG_INF, F32)
            if lo >= 0:
                row = jnp.concatenate([pad[:, :lo], vals, pad[:, lo:]], axis=1)
            else:
                row = jnp.concatenate([vals[:, -lo:], pad, vals[:, :-lo]], axis=1)
            rows.append(row)
    v = jnp.stack(rows, axis=1)
    return v.reshape(N_PAIRS, 2 * N_BIAS_VARIANTS, period)


def _attention_kernel(q_ref, k_ref, v_ref, q4_ref, k4_ref, v4_ref, q16_ref, k16_ref, v16_ref,
                      rows_ref, o_ref, bias_ref, acc_ref, m_ref, l_ref, out_ref):
    s = q_ref.shape[0]
    n_tiles = s // ATTN_TQ
    lane = lax.broadcasted_iota(jnp.int32, (1, PAIR), 1)
    head0 = lane < HEAD_DIM

    @pl.when(pl.program_id(1) == 0)
    def _():
        col = lax.broadcasted_iota(jnp.int32, (ATTN_TQ, ATTN_W), 1)
        for idx in range(2 * N_BIAS_VARIANTS):
            gen = jnp.broadcast_to(rows_ref[idx:idx + 1, :], (ATTN_TQ, 2 * ATTN_W))
            tab = pltpu.roll(gen, 0, 1, stride=1, stride_axis=0)[:, :ATTN_W]
            var = idx % N_BIAS_VARIANTS
            if var >= 6:
                tab = jnp.where((col // ATTN_TQ) == var - 6, tab, NEG_INF)
            head = idx // N_BIAS_VARIANTS
            bias_ref[var, head * ATTN_TQ:(head + 1) * ATTN_TQ, :] = tab

    def run_branch(bi, dil, qs_ref, ks_ref, vs_ref):
        sub_len = s // dil
        tiles_per_sub = sub_len // ATTN_TQ

        def tile(t, carry):
            q0 = pl.multiple_of(t * ATTN_TQ, ATTN_TQ)
            if tiles_per_sub == 1:
                ws = pl.multiple_of((t // 2) * ATTN_W, ATTN_W)
                var = 6 + t % 2
            else:
                pos = t % tiles_per_sub
                sub_lo = (t // tiles_per_sub) * sub_len
                ws = jnp.clip(q0 - 64, sub_lo, sub_lo + sub_len - ATTN_W)
                ws = pl.multiple_of(ws, 64)
                var = jnp.where(pos == 0, 0, jnp.where(pos == tiles_per_sub - 1, 2, 1)) + 3 * bi
            q = qs_ref[pl.ds(q0, ATTN_TQ), :]
            k = ks_ref[pl.ds(ws, ATTN_W), :]
            v = vs_ref[pl.ds(ws, ATTN_W), :]
            q2 = jnp.concatenate([jnp.where(head0, q, jnp.zeros_like(q)),
                                  jnp.where(head0, jnp.zeros_like(q), q)], axis=0)
            sc = lax.dot_general(q2, k, (((1,), (1,)), ((), ())), preferred_element_type=F32)
            sc = sc + bias_ref[var]
            m = jnp.max(sc, axis=-1, keepdims=True)
            p = jnp.exp2(sc - m).astype(BF16)
            o = jnp.dot(p, jnp.concatenate([v, jnp.ones_like(v)], axis=1), preferred_element_type=F32)
            l = o[:, PAIR:]
            outs, ms, ls = (o[:ATTN_TQ, :PAIR], o[ATTN_TQ:, :PAIR]), (m[:ATTN_TQ], m[ATTN_TQ:]), (l[:ATTN_TQ], l[ATTN_TQ:])
            if dil == 1:
                dst = pl.ds(q0, ATTN_TQ)
            elif dil == 4:
                dst = pl.ds((t % 4) * (4 * ATTN_TQ) + t // 4, ATTN_TQ, stride=4)
            else:
                dst = pl.ds((t % 4) * (s // 4) + t // 4, ATTN_TQ, stride=4)
            acc_ref[bi, dst, :] = jnp.where(head0, outs[0], outs[1])
            m_ref[bi, dst, :] = jnp.where(head0, ms[0], ms[1])
            l_ref[bi, dst, :] = jnp.where(head0, ls[0], ls[1])
            return carry

        lax.fori_loop(0, n_tiles, tile, 0, unroll=16)

    run_branch(0, 1, q_ref, k_ref, v_ref)
    run_branch(1, 4, q4_ref, k4_ref, v4_ref)
    run_branch(2, 16, q16_ref, k16_ref, v16_ref)

    rows = ATTN_TQ
    for r4 in range(4):
        for blk in range(s // (4 * rows)):
            nat = pl.ds(r4 + 4 * rows * blk, rows, stride=4)
            sl = (nat, nat, pl.ds(r4 * (s // 4) + rows * blk, rows))
            m = [m_ref[bi, sl[bi], :] for bi in range(3)]
            mx = jnp.maximum(jnp.maximum(m[0], m[1]), m[2])
            num = jnp.zeros((rows, PAIR), F32)
            den = jnp.zeros((rows, PAIR), F32)
            for bi in range(3):
                e = jnp.exp2(m[bi] - mx)
                num = num + e * acc_ref[bi, sl[bi], :]
                den = den + e * l_ref[bi, sl[bi], :]
            out_ref[nat, :] = num / den
    o_ref[...] = out_ref[...].astype(o_ref.dtype)


def _attention(qkv, qkv_grouped, bias_rows):
    b, s, _ = qkv[0].shape
    spec = pl.BlockSpec((None, s, PAIR), lambda hp, bi: (bi, 0, hp))
    gspec = pl.BlockSpec((None, None, s, PAIR), lambda hp, bi: (bi, hp, 0, 0))
    return pl.pallas_call(
        _attention_kernel,
        grid=(N_PAIRS, b),
        in_specs=[spec] * 3 + [gspec] * 6 + [
            pl.BlockSpec((None, 2 * N_BIAS_VARIANTS, 2 * ATTN_W), lambda hp, bi: (hp, 0, 0))],
        out_specs=spec,
        out_shape=jax.ShapeDtypeStruct((b, s, ATTN_WIDTH), BF16),
        scratch_shapes=[
            pltpu.VMEM((N_BIAS_VARIANTS, 2 * ATTN_TQ, ATTN_W), F32),
            pltpu.VMEM((3, s, PAIR), F32),
            pltpu.VMEM((3, s, PAIR), F32),
            pltpu.VMEM((3, s, PAIR), F32),
            pltpu.VMEM((s, PAIR), F32),
        ],
        compiler_params=pltpu.CompilerParams(
            dimension_semantics=("arbitrary", "arbitrary"), vmem_limit_bytes=VMEM_LIMIT),
        name="attention",
    )(*qkv, *qkv_grouped, bias_rows)


def _retention_tables(decay_fwd, decay_bwd):
    c = RET_CHUNK
    lg_f = -jnp.exp(decay_fwd.astype(F32))
    lg_b = -jnp.exp(decay_bwd.astype(F32))
    idx = jnp.arange(c, dtype=F32)
    rel = idx[:, None] - idx[None, :]
    dmat = jnp.where(rel >= 0,
                     jnp.exp(lg_f[:, None, None] * jnp.maximum(rel, 0.0)[None]),
                     jnp.exp(lg_b[:, None, None] * jnp.maximum(-rel, 0.0)[None]))
    dmat = dmat.reshape(N_PAIRS, 2 * c, c)

    def lanes(v):
        v = v.reshape(N_PAIRS, 2, -1)
        return jnp.repeat(jnp.transpose(v, (0, 2, 1)), HEAD_DIM, axis=2)

    vec = jnp.stack([
        lanes(jnp.exp(lg_f[:, None] * (idx + 1.0)[None])),
        lanes(jnp.exp(lg_f[:, None] * (c - 1.0 - idx)[None])),
        lanes(jnp.exp(lg_b[:, None] * (c - idx)[None])),
        lanes(jnp.exp(lg_b[:, None] * idx[None])),
    ], axis=1)
    same_head = (jnp.arange(PAIR)[:, None] // HEAD_DIM) == (jnp.arange(PAIR)[None, :] // HEAD_DIM)
    cd = jnp.stack([lanes(jnp.exp(lg_f * c)[:, None]), lanes(jnp.exp(lg_b * c)[:, None])], axis=1)
    cd = jnp.transpose(cd, (0, 1, 3, 2)) * same_head[None, None].astype(F32)
    return dmat, vec, cd


def _retention_kernel(q_ref, k_ref, v_ref, g_ref, dmat_ref, vec_ref, cd_ref, o_ref, kv_ref, st_ref):
    s = q_ref.shape[0]
    c = RET_CHUNK
    nc = s // c
    lane = lax.broadcasted_iota(jnp.int32, (1, PAIR), 1)
    head0 = lane < HEAD_DIM
    same_head = ((lax.broadcasted_iota(jnp.int32, (PAIR, PAIR), 0) // HEAD_DIM)
                 == (lax.broadcasted_iota(jnp.int32, (PAIR, PAIR), 1) // HEAD_DIM))

    same_head2 = jnp.concatenate([same_head, same_head], axis=0)
    for n in range(nc):
        rows = slice(n * c, (n + 1) * c)
        kf = k_ref[rows, :].astype(F32)
        kcat = jnp.concatenate([(kf * vec_ref[1]).astype(BF16), (kf * vec_ref[3]).astype(BF16)], axis=1)
        kv = lax.dot_general(kcat, v_ref[rows, :], (((0,), (0,)), ((), ())), preferred_element_type=F32)
        kv_ref[n] = jnp.where(same_head2, kv, 0.0)

    state = jnp.zeros((PAIR, PAIR), F32)
    for n in range(nc):
        st_ref[n, :PAIR, :] = state.astype(BF16)
        state = state * cd_ref[0] + kv_ref[n, :PAIR, :]
    state = jnp.zeros((PAIR, PAIR), F32)
    for n in reversed(range(nc)):
        st_ref[n, PAIR:, :] = state.astype(BF16)
        state = state * cd_ref[1] + kv_ref[n, PAIR:, :]

    for n in range(nc):
        rows = slice(n * c, (n + 1) * c)
        q, k, v = q_ref[rows, :], k_ref[rows, :], v_ref[rows, :]
        q2 = jnp.concatenate([jnp.where(head0, q, jnp.zeros_like(q)),
                              jnp.where(head0, jnp.zeros_like(q), q)], axis=0)
        sc = lax.dot_general(q2, k, (((1,), (1,)), ((), ())), preferred_element_type=F32)
        intra = jnp.dot((sc * dmat_ref[...]).astype(BF16), v, preferred_element_type=F32)
        qf = q.astype(F32)
        qcat = jnp.concatenate([(qf * vec_ref[0]).astype(BF16), (qf * vec_ref[2]).astype(BF16)], axis=1)
        y = jnp.where(head0, intra[:c], intra[c:]) + jnp.dot(qcat, st_ref[n], preferred_element_type=F32)
        y2 = y * y
        ms0 = jnp.sum(jnp.where(head0, y2, 0.0), axis=-1, keepdims=True)
        ms1 = jnp.sum(jnp.where(head0, 0.0, y2), axis=-1, keepdims=True)
        ms = jnp.where(head0, ms0, ms1) * (1.0 / HEAD_DIM)
        g = g_ref[rows, :].astype(F32)
        o_ref[rows, :] = (y * lax.rsqrt(ms + EPS) * (g * jax.nn.sigmoid(g))).astype(o_ref.dtype)


def _retention(rq, rk, rv, rg, dmat, vec, cd, weights_f32):
    b, s, _ = rq.shape
    c = RET_CHUNK
    n_steps = N_PAIRS * b
    spec = pl.BlockSpec((None, s, PAIR), lambda hp, bi: (bi, 0, hp))
    slabs = [w.reshape(n_steps, -1, w.shape[-1]) for w in weights_f32]
    slab_specs = [pl.BlockSpec((None,) + w.shape[1:], lambda hp, bi: (hp * b + bi, 0, 0)) for w in slabs]

    def kernel(*refs):
        n_in, n_w = 7, len(slabs)
        _retention_kernel(*refs[:n_in], refs[n_in + n_w], *refs[n_in + 2 * n_w + 1:])
        for src, dst in zip(refs[n_in:n_in + n_w], refs[n_in + n_w + 1:n_in + 2 * n_w + 1]):
            dst[...] = src[...].astype(BF16)

    outs = pl.pallas_call(
        kernel,
        grid=(N_PAIRS, b),
        in_specs=[spec, spec, spec, spec,
                  pl.BlockSpec((None, 2 * c, c), lambda hp, bi: (hp, 0, 0)),
                  pl.BlockSpec((None, 4, c, PAIR), lambda hp, bi: (hp, 0, 0, 0)),
                  pl.BlockSpec((None, 2, PAIR, PAIR), lambda hp, bi: (hp, 0, 0, 0))] + slab_specs,
        out_specs=[spec] + slab_specs,
        out_shape=[jax.ShapeDtypeStruct((b, s, RET_WIDTH), BF16)]
        + [jax.ShapeDtypeStruct(w.shape, BF16) for w in slabs],
        scratch_shapes=[pltpu.VMEM((s // c, 2 * PAIR, PAIR), F32),
                        pltpu.VMEM((s // c, 2 * PAIR, PAIR), BF16)],
        compiler_params=pltpu.CompilerParams(
            dimension_semantics=("arbitrary", "arbitrary"), vmem_limit_bytes=VMEM_LIMIT),
        name="retention",
    )(rq, rk, rv, rg, dmat, vec, cd, *slabs)
    return outs[0], [o.reshape(w.shape) for o, w in zip(outs[1:], weights_f32)]


def _route_tile(sub, attn_ref, ret_ref, x_ref, wo_ref, again_ref, fgain_ref,
                rw_ref, rb_ref, h_ref, xs_ref, dest_ref, cnt_ref):
    tm = ROUTE_TM
    rows = slice(sub * tm, (sub + 1) * tm)
    a = _rms(attn_ref[rows, :].astype(F32), again_ref[...]).astype(BF16)
    mixed = jnp.concatenate([a, ret_ref[rows, :]], axis=1)
    h = x_ref[rows, :] + jnp.dot(mixed, wo_ref[...], preferred_element_type=F32)
    h_ref[rows, :] = h.astype(h_ref.dtype)
    hn = _rms(h, fgain_ref[...]).astype(BF16)

    logits = jnp.dot(hn, rw_ref[...], preferred_element_type=F32) + rb_ref[...]
    lane = lax.broadcasted_iota(jnp.int32, logits.shape, 1)
    big = jnp.int32(LANES)

    def first_argmax(vals):
        top = jnp.max(vals, axis=-1, keepdims=True)
        return top, jnp.min(jnp.where(vals == top, lane, big), axis=-1, keepdims=True)

    gl = jnp.where((lane >= N_EXPERTS) & (lane < N_EXPERTS + N_GROUPS), logits, -jnp.inf)
    gmax, gidx = first_argmax(gl)
    p_group = 1.0 / jnp.sum(jnp.exp(gl - gmax), axis=-1, keepdims=True)
    grp = gidx - N_EXPERTS
    lo = grp * EXPERTS_PER_GROUP
    el = jnp.where((lane >= lo) & (lane < lo + EXPERTS_PER_GROUP), logits, -jnp.inf)
    v1, i1 = first_argmax(el)
    v2, i2 = first_argmax(jnp.where(lane == i1, -jnp.inf, el))
    e2 = jnp.exp(v2 - v1)
    p1 = p_group / (1.0 + e2)
    gates = jnp.where(lane == i1, p1, jnp.where(lane == i2, p1 * e2, 0.0))

    onehot = jnp.where(lane == grp, 1.0, 0.0)
    r_i = lax.broadcasted_iota(jnp.int32, (tm, tm), 0)
    c_i = lax.broadcasted_iota(jnp.int32, (tm, tm), 1)
    before = jnp.where(c_i < r_i, 1.0, 0.0).astype(BF16)
    rank = jnp.dot(before, onehot.astype(BF16), preferred_element_type=F32)
    cnt = jnp.sum(onehot, axis=0, keepdims=True)
    c16 = jnp.broadcast_to(jnp.ceil(cnt * (1.0 / CHUNK)) * CHUNK, (8, LANES))
    lane8 = lax.broadcasted_iota(jnp.int32, (8, LANES), 1)
    start = jnp.zeros((8, LANES), F32)
    for sft in range(1, N_GROUPS):
        start = start + jnp.where(lane8 >= sft, pltpu.roll(c16, sft, 1), 0.0)
    dest = jnp.sum(onehot * (start[0:1] + rank), axis=-1, keepdims=True)
    dest_ref[rows, :] = jnp.broadcast_to(dest, (tm, LANES))
    cnt_ref[sub] = jnp.where(lane8 < N_GROUPS, c16, pltpu.roll(start, N_GROUPS, 1)).astype(jnp.int32)

    onehot_t = onehot.T
    rank_t = lax.dot_general(onehot_t.astype(BF16), before, (((1,), (1,)), ((), ())),
                             preferred_element_type=F32)
    g_row = lax.broadcasted_iota(jnp.int32, (LANES, LANES), 0)
    g_col = lax.broadcasted_iota(jnp.int32, (LANES, LANES), 1)
    start_t = jnp.sum(jnp.where(g_col < g_row, c16[0:1], 0.0), axis=-1, keepdims=True)
    dest_t = jnp.sum(onehot_t * (start_t + rank_t), axis=0, keepdims=True)
    perm = jnp.where(lax.broadcasted_iota(jnp.int32, (SORT_ROWS, tm), 0) == dest_t.astype(jnp.int32),
                     1.0, 0.0).astype(BF16)
    g_hi = gates.astype(BF16)
    g_lo = (gates - g_hi.astype(F32)).astype(BF16)
    payload = jnp.concatenate([hn, g_hi, g_lo], axis=1)
    xs_ref[sub * SORT_ROWS:(sub + 1) * SORT_ROWS, :] = (
        jnp.dot(perm, payload, preferred_element_type=F32).astype(BF16))


def _outproj_kernel(*refs):
    for sub in range(ROUTE_TILES_PER_STEP):
        _route_tile(sub, *refs)


def _outproj(attn, ret, x, w_o, again, fgain, rw, rb):
    t, d = x.shape
    per_step = ROUTE_TILES_PER_STEP
    tm = ROUTE_TM * per_step
    nt = t // ROUTE_TM
    row = lambda w: pl.BlockSpec((tm, w), lambda i: (i, 0))
    full = lambda r, c: pl.BlockSpec((r, c), lambda i: (0, 0))
    return pl.pallas_call(
        _outproj_kernel,
        grid=(nt // per_step,),
        in_specs=[row(ATTN_WIDTH), row(RET_WIDTH), row(d), full(ATTN_WIDTH + RET_WIDTH, d),
                  full(1, ATTN_WIDTH), full(1, d), full(d, LANES), full(1, LANES)],
        out_specs=[row(d), pl.BlockSpec((per_step * SORT_ROWS, XS_W), lambda i: (i, 0)), row(LANES),
                   pl.BlockSpec((per_step, 8, LANES), lambda i: (i, 0, 0))],
        out_shape=[jax.ShapeDtypeStruct((t, d), BF16),
                   jax.ShapeDtypeStruct((nt * SORT_ROWS, XS_W), BF16),
                   jax.ShapeDtypeStruct((t, LANES), F32),
                   jax.ShapeDtypeStruct((nt, 8, LANES), jnp.int32)],
        compiler_params=pltpu.CompilerParams(
            dimension_semantics=("arbitrary",), vmem_limit_bytes=VMEM_LIMIT),
        name="outproj",
    )(attn, ret, x, w_o, again, fgain, rw, rb)


def _dispatch_tables(cnt, n_moe_tiles):
    nt = cnt.shape[0]
    cpt = MOE_TM // CHUNK

    n_slots = n_moe_tiles * cpt
    slot_rows = -(-n_slots // (8 * LANES)) * 8

    def schedule_kernel(seg_ref, grp_ref, used_ref, src_ref, tail_ref):
        slot = (lax.broadcasted_iota(jnp.int32, (slot_rows, LANES), 0) * LANES
                + lax.broadcasted_iota(jnp.int32, (slot_rows, LANES), 1))
        src = jnp.full((slot_rows, LANES), -1, jnp.int32)
        pos = jnp.int32(0)
        for g in range(N_GROUPS):
            def tile_body(i, carry):
                p, src = carry
                n = seg_ref[i, g] // CHUNK
                first = (i * SORT_ROWS + seg_ref[i, N_GROUPS + g]) // CHUNK
                src = jnp.where((slot >= p) & (slot < p + n), slot + (first - p), src)
                return p + n, src
            end, src = lax.fori_loop(0, nt, tile_body, (pos, src))
            padded = ((end + cpt - 1) // cpt) * cpt

            def mark_body(m, carry):
                grp_ref[m] = g
                used_ref[m] = 1
                return carry
            lax.fori_loop(pos // cpt, padded // cpt, mark_body, 0)
            pos = padded

        def idle_body(m, carry):
            grp_ref[m] = N_GROUPS - 1
            used_ref[m] = 0
            return carry
        lax.fori_loop(pos // cpt, n_moe_tiles, idle_body, 0)
        src_ref[...] = src

        def tail_body(i, carry):
            rows = seg_ref[i, 0]
            for g in range(1, N_GROUPS):
                rows = rows + seg_ref[i, g]
            tail_ref[i] = rows // CHUNK
            return carry
        lax.fori_loop(0, nt, tail_body, 0)

    smem = lambda: pl.BlockSpec(memory_space=pltpu.SMEM)
    tile_group, tile_used, src, tile_tail = pl.pallas_call(
        schedule_kernel,
        in_specs=[smem()],
        out_specs=[smem(), smem(), pl.BlockSpec(memory_space=pltpu.VMEM), smem()],
        out_shape=[jax.ShapeDtypeStruct((n_moe_tiles,), jnp.int32),
                   jax.ShapeDtypeStruct((n_moe_tiles,), jnp.int32),
                   jax.ShapeDtypeStruct((slot_rows, LANES), jnp.int32),
                   jax.ShapeDtypeStruct((nt,), jnp.int32)],
        name="schedule",
    )(cnt[:, 0, :2 * N_GROUPS])
    return tile_group, tile_used, src.reshape(-1)[:n_slots], tile_tail


def _moe_kernel(grp_ref, used_ref, src_ref, tail_ref, xs_hbm, w1_ref, w3_ref, w2_ref, ys_hbm,
                xbuf, obuf, zbuf, in_sem, out_sem, zero_sem):
    m = pl.program_id(0)
    n_tiles = pl.num_programs(0)
    cpt = MOE_TM // CHUNK
    cps = SORT_ROWS // CHUNK
    slot = m % 2
    zero_chunk = cps - 1

    def rows(c):
        return pl.ds(pl.multiple_of(c * CHUNK, CHUNK), CHUNK)

    def zero_tails(wait):
        def tile_body(i, carry):
            def body(c, carry2):
                cp = pltpu.make_async_copy(zbuf, ys_hbm.at[rows(i * cps + c), :], zero_sem)
                cp.wait() if wait else cp.start()
                return carry2
            lax.fori_loop(tail_ref[i], cps, body, 0)
            return carry
        lax.fori_loop(0, tail_ref.shape[0], tile_body, 0)

    def gather_start(tile, sl):
        for c in range(cpt):
            src = src_ref[tile * cpt + c]
            src = jnp.where(src < 0, zero_chunk, src)
            pltpu.make_async_copy(xs_hbm.at[rows(src), :], xbuf.at[sl, c * CHUNK:(c + 1) * CHUNK, :],
                                  in_sem.at[sl]).start()

    def gather_wait(sl):
        pltpu.make_async_copy(xs_hbm.at[0:MOE_TM, :], xbuf.at[sl], in_sem.at[sl]).wait()

    def scatter(tile, sl, wait):
        full = src_ref[tile * cpt + cpt - 1] >= 0

        def chunk_copy(c, src):
            return pltpu.make_async_copy(obuf.at[sl, c * CHUNK:(c + 1) * CHUNK, :], ys_hbm.at[rows(src), :],
                                         out_sem.at[sl])

        @pl.when(full)
        def _():
            if wait:
                pltpu.make_async_copy(obuf.at[sl], ys_hbm.at[0:MOE_TM, :], out_sem.at[sl]).wait()
            else:
                for c in range(cpt):
                    chunk_copy(c, src_ref[tile * cpt + c]).start()

        @pl.when(jnp.logical_not(full))
        def _():
            for c in range(cpt):
                src = src_ref[tile * cpt + c]

                @pl.when(src >= 0)
                def _():
                    cp = chunk_copy(c, src)
                    cp.wait() if wait else cp.start()

    @pl.when(m == 0)
    def _():
        gather_start(0, 0)
        zbuf[...] = jnp.zeros(zbuf.shape, zbuf.dtype)
        zero_tails(False)

    prev_used = used_ref[jnp.maximum(m - 1, 0)] > 0

    @pl.when((m == 0) | prev_used)
    def _():
        gather_wait(slot)

    @pl.when((m >= 2) & (used_ref[jnp.maximum(m - 2, 0)] > 0))
    def _():
        scatter(m - 2, slot, True)

    @pl.when(used_ref[m] > 0)
    def _():
        gather_start(jnp.minimum(m + 1, n_tiles - 1), 1 - slot)
        x = xbuf[slot, :, :D_MODEL]
        gate = (xbuf[slot, :, D_MODEL:D_MODEL + LANES].astype(F32)
                + xbuf[slot, :, D_MODEL + LANES:].astype(F32))
        lane = lax.broadcasted_iota(jnp.int32, gate.shape, 1)
        base = grp_ref[m] * EXPERTS_PER_GROUP
        acc = jnp.zeros((MOE_TM, D_MODEL), F32)
        for j in range(EXPERTS_PER_GROUP):
            a = jnp.dot(x, w1_ref[j], preferred_element_type=F32)
            b = jnp.dot(x, w3_ref[j], preferred_element_type=F32)
            gj = jnp.sum(jnp.where(lane == base + j, gate, 0.0), axis=-1, keepdims=True)
            hid = (a * jax.nn.sigmoid(a) * b * gj).astype(BF16)
            acc = acc + jnp.dot(hid, w2_ref[j], preferred_element_type=F32)
        obuf[slot] = acc.astype(BF16)
        scatter(m, slot, False)

    @pl.when(m == n_tiles - 1)
    def _():
        zero_tails(True)

        @pl.when(used_ref[m] > 0)
        def _():
            gather_wait(1 - slot)
            scatter(m, slot, True)

        @pl.when((m >= 1) & prev_used)
        def _():
            scatter(m - 1, 1 - slot, True)


def _moe(xs, w1, w3, w2, tile_group, tile_used, src_chunk, tile_tail):
    n_moe_tiles = tile_group.shape[0]
    rows = xs.shape[0]
    d = D_MODEL
    wspec = lambda r, c: pl.BlockSpec((EXPERTS_PER_GROUP, r, c), lambda m, grp, *_: (grp[m], 0, 0))
    return pl.pallas_call(
        _moe_kernel,
        grid_spec=pltpu.PrefetchScalarGridSpec(
            num_scalar_prefetch=4,
            grid=(n_moe_tiles,),
            in_specs=[pl.BlockSpec(memory_space=pl.ANY),
                      wspec(d, EXPERT_FF), wspec(d, EXPERT_FF), wspec(EXPERT_FF, d)],
            out_specs=pl.BlockSpec(memory_space=pl.ANY),
            scratch_shapes=[pltpu.VMEM((2, MOE_TM, XS_W), BF16),
                            pltpu.VMEM((2, MOE_TM, d), BF16),
                            pltpu.VMEM((CHUNK, d), BF16),
                            pltpu.SemaphoreType.DMA((2,)),
                            pltpu.SemaphoreType.DMA((2,)),
                            pltpu.SemaphoreType.DMA(())]),
        out_shape=jax.ShapeDtypeStruct((rows, d), BF16),
        compiler_params=pltpu.CompilerParams(
            dimension_semantics=("arbitrary",), vmem_limit_bytes=VMEM_LIMIT),
        name="moe",
    )(tile_group, tile_used, src_chunk, tile_tail, xs, w1, w3, w2)


def _combine_kernel(ys_ref, h_ref, dest_ref, gain_ref, o_ref):
    tm = ROUTE_TM
    for sub in range(h_ref.shape[0] // tm):
        rows = slice(sub * tm, (sub + 1) * tm)
        dest = dest_ref[rows, 0:1].astype(jnp.int32)
        perm_t = jnp.where(lax.broadcasted_iota(jnp.int32, (tm, SORT_ROWS), 1) == dest, 1.0, 0.0).astype(BF16)
        moe = jnp.dot(perm_t, ys_ref[sub * SORT_ROWS:(sub + 1) * SORT_ROWS, :], preferred_element_type=F32)
        o_ref[rows, :] = _rms(h_ref[rows, :].astype(F32) + moe, gain_ref[...])


def _combine(ys, h, dest, gain):
    t, d = h.shape
    per_step = ROUTE_TILES_PER_STEP
    tm = ROUTE_TM * per_step
    return pl.pallas_call(
        _combine_kernel,
        grid=(t // tm,),
        in_specs=[pl.BlockSpec((per_step * SORT_ROWS, d), lambda i: (i, 0)),
                  pl.BlockSpec((tm, d), lambda i: (i, 0)),
                  pl.BlockSpec((tm, LANES), lambda i: (i, 0)),
                  pl.BlockSpec((1, d), lambda i: (0, 0))],
        out_specs=pl.BlockSpec((tm, d), lambda i: (i, 0)),
        out_shape=jax.ShapeDtypeStruct((t, d), F32),
        compiler_params=pltpu.CompilerParams(
            dimension_semantics=("arbitrary",), vmem_limit_bytes=VMEM_LIMIT),
        name="combine",
    )(ys, h, dest, gain)


def _rotary_tables(s):
    half = HEAD_DIM // 2
    inv = ROPE_BASE ** (-jnp.arange(half, dtype=F32) / half)
    ang = jnp.arange(s, dtype=F32)[:, None] * inv[None, :]
    cos, sin = jnp.cos(ang), jnp.sin(ang)
    cos_t = jnp.tile(jnp.concatenate([cos, cos], axis=-1), (1, LANES // HEAD_DIM))
    sin_t = jnp.tile(jnp.concatenate([-sin, sin], axis=-1), (1, LANES // HEAD_DIM))
    return cos_t, sin_t


def kernel(x, w_in, w_out, norm_mix, norm_ffn, norm_final, attn_out_gain, rel_bias, ret_decay_fwd, ret_decay_bwd, router_group_w, router_group_b, router_expert_w, router_expert_b, expert_w1, expert_w3, expert_w2):
    b, s, d = x.shape
    depth = w_in.shape[0]
    cos_t, sin_t = _rotary_tables(s)
    bias_rows = _attn_bias_rows(rel_bias)
    h = x
    for layer in range(depth):
        (aq, ak, av, rq, rk, rv, rg), qkv_grouped = _inproj(
            h, norm_mix[layer][None], w_in[layer].astype(BF16), cos_t, sin_t)
        attn = _attention((aq, ak, av), qkv_grouped, bias_rows)
        ret, (w1, w3, w2, w_o) = _retention(
            rq, rk, rv, rg, *_retention_tables(ret_decay_fwd[layer], ret_decay_bwd[layer]),
            (expert_w1[layer], expert_w3[layer], expert_w2[layer], w_out[layer]))

        rw = jnp.concatenate(
            [jnp.transpose(router_expert_w[layer], (1, 0, 2)).reshape(d, N_EXPERTS),
             router_group_w[layer],
             jnp.zeros((d, LANES - N_EXPERTS - N_GROUPS), F32)], axis=1).astype(BF16)
        rb = jnp.concatenate(
            [router_expert_b[layer].reshape(N_EXPERTS), router_group_b[layer],
             jnp.zeros((LANES - N_EXPERTS - N_GROUPS,), F32)])[None].astype(F32)
        h1, xs, dest, cnt = _outproj(
            attn.reshape(b * s, ATTN_WIDTH), ret.reshape(b * s, RET_WIDTH), h.reshape(b * s, d),
            w_o, attn_out_gain[layer][None], norm_ffn[layer][None],
            rw, rb)
        n_route_tiles = (b * s) // ROUTE_TM
        n_moe_tiles = (b * s + n_route_tiles * N_GROUPS * (CHUNK - 1)) // MOE_TM + N_GROUPS
        schedule = _dispatch_tables(cnt, n_moe_tiles)
        ys = _moe(xs, w1, w3, w2, *schedule)
        assert depth == 1, "the combine kernel fuses the final norm, so it must run on the last layer"
        h = _combine(ys, h1, dest, norm_final[None]).reshape(b, s, d)
    return h
```

```python
import functools
import math

import jax
import jax.numpy as jnp
from jax import lax
from jax.experimental import pallas as pl
from jax.experimental.pallas import tpu as pltpu

F32 = jnp.float32
BF16 = jnp.bfloat16

D_MODEL = 1024
HEAD_DIM = 64
ATTN_WIDTH = 512
RET_WIDTH = 512
N_HEADS = 8
PAIR = 2 * HEAD_DIM
N_PAIRS = N_HEADS // 2
ATTN_DILATIONS = (1, 4, 16)
ATTN_RADIUS = 64
N_BUCKETS = 32
REL_MAX_DIST = 1024
ROPE_BASE = 10000.0
N_GROUPS = 4
EXPERTS_PER_GROUP = 4
N_EXPERTS = 16
EXPERT_FF = 512
EPS = 1e-6
NEG_INF = -1e30
LOG2_E = math.log2(math.e)

LANES = 128
ATTN_TQ = 128
ATTN_W = 256
N_BIAS_VARIANTS = 8
RET_CHUNK = 256
ROUTE_TM = 512
ROUTE_TILES_PER_STEP = 2
SORT_ROWS = 592
CHUNK = 16
MOE_TM = 512
XS_W = D_MODEL + 2 * LANES
VMEM_LIMIT = 48 * 1024 * 1024


def _rms(x, gain):
    return x * lax.rsqrt(jnp.mean(x * x, axis=-1, keepdims=True) + EPS) * gain


def _inproj_kernel(x_ref, gain_ref, w_ref, cos_ref, sin_ref,
                   aq_ref, ak_ref, av_ref, rq_ref, rk_ref, rv_ref, rg_ref,
                   aq4_ref, ak4_ref, av4_ref, aq16_ref, ak16_ref, av16_ref, stage_ref, stage4_ref):
    tm = x_ref.shape[0]
    xn = _rms(x_ref[...], gain_ref[...]).astype(BF16)

    def emit(i, t, nat_ref, d4_ref, d16_ref):
        nat_ref[...] = t.astype(BF16)
        for hp in range(N_PAIRS):
            stage_ref[i, hp] = t[:, hp * PAIR:(hp + 1) * PAIR]
            for r4 in range(4):
                g4 = stage_ref[i, hp, pl.ds(r4, tm // 4, stride=4), :]
                d4_ref[hp, r4] = g4.astype(BF16)
                stage4_ref[i, hp, r4] = g4
                for j in range(4):
                    d16_ref[hp, r4 + 4 * j] = (
                        stage4_ref[i, hp, r4, pl.ds(j, tm // 16, stride=4), :].astype(BF16))

    def seg(i):
        return jnp.dot(xn, w_ref[:, i * 512:(i + 1) * 512], preferred_element_type=F32)

    def rotary(t):
        cos, sin = cos_ref[...], sin_ref[...]
        first_half = (lax.broadcasted_iota(jnp.int32, (1, LANES), 1) % HEAD_DIM) < HEAD_DIM // 2
        outs = []
        for j in range(t.shape[1] // LANES):
            tj = t[:, j * LANES:(j + 1) * LANES]
            partner = jnp.where(first_half, pltpu.roll(tj, LANES - 32, 1), pltpu.roll(tj, 32, 1))
            outs.append(tj * cos + partner * sin)
        return jnp.concatenate(outs, axis=1)

    emit(0, seg(0) * (HEAD_DIM ** -0.5 * LOG2_E), aq_ref, aq4_ref, aq16_ref)
    emit(1, seg(1), ak_ref, ak4_ref, ak16_ref)
    emit(2, seg(2), av_ref, av4_ref, av16_ref)
    rq_ref[...] = rotary(seg(3)).astype(BF16)
    rk_ref[...] = (rotary(seg(4)) * (HEAD_DIM ** -0.5)).astype(BF16)
    rv_ref[...] = seg(5).astype(BF16)
    rg_ref[...] = seg(6).astype(BF16)


def _inproj(x, gain, w_in, cos_t, sin_t, tm=512):
    b, s, d = x.shape
    n = w_in.shape[1]
    out = jax.ShapeDtypeStruct((b, s, 512), BF16)
    ospec = pl.BlockSpec((None, tm, 512), lambda si, bi: (bi, si, 0))

    def grouped(dil):
        shape = jax.ShapeDtypeStruct((b, N_PAIRS, dil, s // dil, PAIR), BF16)
        spec = pl.BlockSpec((None, N_PAIRS, dil, tm // dil, PAIR), lambda si, bi: (bi, 0, 0, si, 0))
        return [shape] * 3, [spec] * 3

    shapes4, specs4 = grouped(4)
    shapes16, specs16 = grouped(16)
    outs = pl.pallas_call(
        _inproj_kernel,
        grid=(s // tm, b),
        in_specs=[
            pl.BlockSpec((None, tm, d), lambda si, bi: (bi, si, 0)),
            pl.BlockSpec((1, d), lambda si, bi: (0, 0)),
            pl.BlockSpec((d, n), lambda si, bi: (0, 0)),
            pl.BlockSpec((tm, LANES), lambda si, bi: (si, 0)),
            pl.BlockSpec((tm, LANES), lambda si, bi: (si, 0)),
        ],
        out_specs=[ospec] * 7 + specs4 + specs16,
        out_shape=[out] * 7 + shapes4 + shapes16,
        scratch_shapes=[pltpu.VMEM((3, N_PAIRS, tm, PAIR), F32),
                        pltpu.VMEM((3, N_PAIRS, 4, tm // 4, PAIR), F32)],
        compiler_params=pltpu.CompilerParams(
            dimension_semantics=("arbitrary", "arbitrary"), vmem_limit_bytes=VMEM_LIMIT),
        name="inproj",
    )(x, gain, w_in, cos_t, sin_t)
    return outs[:7], [o.reshape(b, N_PAIRS, s, PAIR) for o in outs[7:]]


def _t5_bucket(rel):
    half = N_BUCKETS // 2
    max_exact = half // 2
    offset = jnp.where(rel > 0, half, 0)
    n = jnp.abs(rel)
    nf = jnp.maximum(n, 1).astype(F32)
    large = max_exact + (jnp.log(nf / max_exact) / math.log(REL_MAX_DIST / max_exact)
                         * (half - max_exact)).astype(jnp.int32)
    large = jnp.minimum(large, half - 1)
    return offset + jnp.where(n < max_exact, n, large)


def _attn_bias_rows(rel_bias):
    period = 2 * ATTN_W
    band = 2 * ATTN_RADIUS + 1
    rel = jnp.arange(-ATTN_RADIUS, ATTN_RADIUS + 1)
    rows = []
    for dil, offs in ((1, (0, 64, 128)), (4, (0, 64, 128)), (16, (0, 128))):
        vals = rel_bias[_t5_bucket(rel * dil)].astype(F32).T * LOG2_E
        for off in offs:
            lo = off - ATTN_RADIUS
            pad = jnp.full((N_HEADS, period - band), NEG_INF, F32)
            if lo >= 0:
                row = jnp.concatenate([pad[:, :lo], vals, pad[:, lo:]], axis=1)
            else:
                row = jnp.concatenate([vals[:, -lo:], pad, vals[:, :-lo]], axis=1)
            rows.append(row)
    v = jnp.stack(rows, axis=1)
    return v.reshape(N_PAIRS, 2 * N_BIAS_VARIANTS, period)


def _attention_kernel(q_ref, k_ref, v_ref, q4_ref, k4_ref, v4_ref, q16_ref, k16_ref, v16_ref,
                      rows_ref, o_ref, bias_ref, acc_ref, m_ref, l_ref, out_ref):
    s = q_ref.shape[0]
    n_tiles = s // ATTN_TQ
    lane = lax.broadcasted_iota(jnp.int32, (1, PAIR), 1)
    head0 = lane < HEAD_DIM

    @pl.when(pl.program_id(1) == 0)
    def _():
        col = lax.broadcasted_iota(jnp.int32, (ATTN_TQ, ATTN_W), 1)
        for idx in range(2 * N_BIAS_VARIANTS):
            gen = jnp.broadcast_to(rows_ref[idx:idx + 1, :], (ATTN_TQ, 2 * ATTN_W))
            tab = pltpu.roll(gen, 0, 1, stride=1, stride_axis=0)[:, :ATTN_W]
            var = idx % N_BIAS_VARIANTS
            if var >= 6:
                tab = jnp.where((col // ATTN_TQ) == var - 6, tab, NEG_INF)
            head = idx // N_BIAS_VARIANTS
            bias_ref[var, head * ATTN_TQ:(head + 1) * ATTN_TQ, :] = tab

    def run_branch(bi, dil, qs_ref, ks_ref, vs_ref):
        sub_len = s // dil
        tiles_per_sub = sub_len // ATTN_TQ

        def tile(t, carry):
            q0 = pl.multiple_of(t * ATTN_TQ, ATTN_TQ)
            if tiles_per_sub == 1:
                ws = pl.multiple_of((t // 2) * ATTN_W, ATTN_W)
                var = 6 + t % 2
            else:
                pos = t % tiles_per_sub
                sub_lo = (t // tiles_per_sub) * sub_len
                ws = jnp.clip(q0 - 64, sub_lo, sub_lo + sub_len - ATTN_W)
                ws = pl.multiple_of(ws, 64)
                var = jnp.where(pos == 0, 0, jnp.where(pos == tiles_per_sub - 1, 2, 1)) + 3 * bi
            q = qs_ref[pl.ds(q0, ATTN_TQ), :]
            k = ks_ref[pl.ds(ws, ATTN_W), :]
            v = vs_ref[pl.ds(ws, ATTN_W), :]
            q2 = jnp.concatenate([jnp.where(head0, q, jnp.zeros_like(q)),
                                  jnp.where(head0, jnp.zeros_like(q), q)], axis=0)
            sc = lax.dot_general(q2, k, (((1,), (1,)), ((), ())), preferred_element_type=F32)
            sc = sc + bias_ref[var]
            m = jnp.max(sc, axis=-1, keepdims=True)
            p = jnp.exp2(sc - m).astype(BF16)
            o = jnp.dot(p, jnp.concatenate([v, jnp.ones_like(v)], axis=1), preferred_element_type=F32)
            l = o[:, PAIR:]
            outs, ms, ls = (o[:ATTN_TQ, :PAIR], o[ATTN_TQ:, :PAIR]), (m[:ATTN_TQ], m[ATTN_TQ:]), (l[:ATTN_TQ], l[ATTN_TQ:])
            if dil == 1:
                dst = pl.ds(q0, ATTN_TQ)
            elif dil == 4:
                dst = pl.ds((t % 4) * (4 * ATTN_TQ) + t // 4, ATTN_TQ, stride=4)
            else:
                dst = pl.ds((t % 4) * (s // 4) + t // 4, ATTN_TQ, stride=4)
            acc_ref[bi, dst, :] = jnp.where(head0, outs[0], outs[1])
            m_ref[bi, dst, :] = jnp.where(head0, ms[0], ms[1])
            l_ref[bi, dst, :] = jnp.where(head0, ls[0], ls[1])
            return carry

        lax.fori_loop(0, n_tiles, tile, 0, unroll=16)

    run_branch(0, 1, q_ref, k_ref, v_ref)
    run_branch(1, 4, q4_ref, k4_ref, v4_ref)
    run_branch(2, 16, q16_ref, k16_ref, v16_ref)

    rows = ATTN_TQ
    for r4 in range(4):
        for blk in range(s // (4 * rows)):
            nat = pl.ds(r4 + 4 * rows * blk, rows, stride=4)
            sl = (nat, nat, pl.ds(r4 * (s // 4) + rows * blk, rows))
            m = [m_ref[bi, sl[bi], :] for bi in range(3)]
            mx = jnp.maximum(jnp.maximum(m[0], m[1]), m[2])
            num = jnp.zeros((rows, PAIR), F32)
            den = jnp.zeros((rows, PAIR), F32)
            for bi in range(3):
                e = jnp.exp2(m[bi] - mx)
                num = num + e * acc_ref[bi, sl[bi], :]
                den = den + e * l_ref[bi, sl[bi], :]
            out_ref[nat, :] = num / den
    o_ref[...] = out_ref[...].astype(o_ref.dtype)


def _attention(qkv, qkv_grouped, bias_rows):
    b, s, _ = qkv[0].shape
    spec = pl.BlockSpec((None, s, PAIR), lambda hp, bi: (bi, 0, hp))
    gspec = pl.BlockSpec((None, None, s, PAIR), lambda hp, bi: (bi, hp, 0, 0))
    return pl.pallas_call(
        _attention_kernel,
        grid=(N_PAIRS, b),
        in_specs=[spec] * 3 + [gspec] * 6 + [
            pl.BlockSpec((None, 2 * N_BIAS_VARIANTS, 2 * ATTN_W), lambda hp, bi: (hp, 0, 0))],
        out_specs=spec,
        out_shape=jax.ShapeDtypeStruct((b, s, ATTN_WIDTH), BF16),
        scratch_shapes=[
            pltpu.VMEM((N_BIAS_VARIANTS, 2 * ATTN_TQ, ATTN_W), F32),
            pltpu.VMEM((3, s, PAIR), F32),
            pltpu.VMEM((3, s, PAIR), F32),
            pltpu.VMEM((3, s, PAIR), F32),
            pltpu.VMEM((s, PAIR), F32),
        ],
        compiler_params=pltpu.CompilerParams(
            dimension_semantics=("arbitrary", "arbitrary"), vmem_limit_bytes=VMEM_LIMIT),
        name="attention",
    )(*qkv, *qkv_grouped, bias_rows)


def _retention_tables(decay_fwd, decay_bwd):
    c = RET_CHUNK
    lg_f = -jnp.exp(decay_fwd.astype(F32))
    lg_b = -jnp.exp(decay_bwd.astype(F32))
    idx = jnp.arange(c, dtype=F32)
    rel = idx[:, None] - idx[None, :]
    dmat = jnp.where(rel >= 0,
                     jnp.exp(lg_f[:, None, None] * jnp.maximum(rel, 0.0)[None]),
                     jnp.exp(lg_b[:, None, None] * jnp.maximum(-rel, 0.0)[None]))
    dmat = dmat.reshape(N_PAIRS, 2 * c, c)

    def lanes(v):
        v = v.reshape(N_PAIRS, 2, -1)
        return jnp.repeat(jnp.transpose(v, (0, 2, 1)), HEAD_DIM, axis=2)

    vec = jnp.stack([
        lanes(jnp.exp(lg_f[:, None] * (idx + 1.0)[None])),
        lanes(jnp.exp(lg_f[:, None] * (c - 1.0 - idx)[None])),
        lanes(jnp.exp(lg_b[:, None] * (c - idx)[None])),
        lanes(jnp.exp(lg_b[:, None] * idx[None])),
    ], axis=1)
    same_head = (jnp.arange(PAIR)[:, None] // HEAD_DIM) == (jnp.arange(PAIR)[None, :] // HEAD_DIM)
    cd = jnp.stack([lanes(jnp.exp(lg_f * c)[:, None]), lanes(jnp.exp(lg_b * c)[:, None])], axis=1)
    cd = jnp.transpose(cd, (0, 1, 3, 2)) * same_head[None, None].astype(F32)
    return dmat, vec, cd


def _retention_kernel(q_ref, k_ref, v_ref, g_ref, dmat_ref, vec_ref, cd_ref, o_ref, kv_ref, st_ref):
    s = q_ref.shape[0]
    c = RET_CHUNK
    nc = s // c
    lane = lax.broadcasted_iota(jnp.int32, (1, PAIR), 1)
    head0 = lane < HEAD_DIM
    same_head = ((lax.broadcasted_iota(jnp.int32, (PAIR, PAIR), 0) // HEAD_DIM)
                 == (lax.broadcasted_iota(jnp.int32, (PAIR, PAIR), 1) // HEAD_DIM))

    same_head2 = jnp.concatenate([same_head, same_head], axis=0)
    for n in range(nc):
        rows = slice(n * c, (n + 1) * c)
        kf = k_ref[rows, :].astype(F32)
        kcat = jnp.concatenate([(kf * vec_ref[1]).astype(BF16), (kf * vec_ref[3]).astype(BF16)], axis=1)
        kv = lax.dot_general(kcat, v_ref[rows, :], (((0,), (0,)), ((), ())), preferred_element_type=F32)
        kv_ref[n] = jnp.where(same_head2, kv, 0.0)

    state = jnp.zeros((PAIR, PAIR), F32)
    for n in range(nc):
        st_ref[n, :PAIR, :] = state.astype(BF16)
        state = state * cd_ref[0] + kv_ref[n, :PAIR, :]
    state = jnp.zeros((PAIR, PAIR), F32)
    for n in reversed(range(nc)):
        st_ref[n, PAIR:, :] = state.astype(BF16)
        state = state * cd_ref[1] + kv_ref[n, PAIR:, :]

    for n in range(nc):
        rows = slice(n * c, (n + 1) * c)
        q, k, v = q_ref[rows, :], k_ref[rows, :], v_ref[rows, :]
        q2 = jnp.concatenate([jnp.where(head0, q, jnp.zeros_like(q)),
                              jnp.where(head0, jnp.zeros_like(q), q)], axis=0)
        sc = lax.dot_general(q2, k, (((1,), (1,)), ((), ())), preferred_element_type=F32)
        intra = jnp.dot((sc * dmat_ref[...]).astype(BF16), v, preferred_element_type=F32)
        qf = q.astype(F32)
        qcat = jnp.concatenate([(qf * vec_ref[0]).astype(BF16), (qf * vec_ref[2]).astype(BF16)], axis=1)
        y = jnp.where(head0, intra[:c], intra[c:]) + jnp.dot(qcat, st_ref[n], preferred_element_type=F32)
        y2 = y * y
        ms0 = jnp.sum(jnp.where(head0, y2, 0.0), axis=-1, keepdims=True)
        ms1 = jnp.sum(jnp.where(head0, 0.0, y2), axis=-1, keepdims=True)
        ms = jnp.where(head0, ms0, ms1) * (1.0 / HEAD_DIM)
        g = g_ref[rows, :].astype(F32)
        o_ref[rows, :] = (y * lax.rsqrt(ms + EPS) * (g * jax.nn.sigmoid(g))).astype(o_ref.dtype)


def _retention(rq, rk, rv, rg, dmat, vec, cd, weights_f32):
    b, s, _ = rq.shape
    c = RET_CHUNK
    n_steps = N_PAIRS * b
    spec = pl.BlockSpec((None, s, PAIR), lambda hp, bi: (bi, 0, hp))
    slabs = [w.reshape(n_steps, -1, w.shape[-1]) for w in weights_f32]
    slab_specs = [pl.BlockSpec((None,) + w.shape[1:], lambda hp, bi: (hp * b + bi, 0, 0)) for w in slabs]

    def kernel(*refs):
        n_in, n_w = 7, len(slabs)
        _retention_kernel(*refs[:n_in], refs[n_in + n_w], *refs[n_in + 2 * n_w + 1:])
        for src, dst in zip(refs[n_in:n_in + n_w], refs[n_in + n_w + 1:n_in + 2 * n_w + 1]):
            dst[...] = src[...].astype(BF16)

    outs = pl.pallas_call(
        kernel,
        grid=(N_PAIRS, b),
        in_specs=[spec, spec, spec, spec,
                  pl.BlockSpec((None, 2 * c, c), lambda hp, bi: (hp, 0, 0)),
                  pl.BlockSpec((None, 4, c, PAIR), lambda hp, bi: (hp, 0, 0, 0)),
                  pl.BlockSpec((None, 2, PAIR, PAIR), lambda hp, bi: (hp, 0, 0, 0))] + slab_specs,
        out_specs=[spec] + slab_specs,
        out_shape=[jax.ShapeDtypeStruct((b, s, RET_WIDTH), BF16)]
        + [jax.ShapeDtypeStruct(w.shape, BF16) for w in slabs],
        scratch_shapes=[pltpu.VMEM((s // c, 2 * PAIR, PAIR), F32),
                        pltpu.VMEM((s // c, 2 * PAIR, PAIR), BF16)],
        compiler_params=pltpu.CompilerParams(
            dimension_semantics=("arbitrary", "arbitrary"), vmem_limit_bytes=VMEM_LIMIT),
        name="retention",
    )(rq, rk, rv, rg, dmat, vec, cd, *slabs)
    return outs[0], [o.reshape(w.shape) for o, w in zip(outs[1:], weights_f32)]


def _mixers(qkv, qkv_grouped, bias_rows, ret_inputs, ret_tables, weights_f32):
    b, s, _ = qkv[0].shape
    c = RET_CHUNK
    n_steps = N_PAIRS * b
    spec = pl.BlockSpec((None, s, PAIR), lambda hp, bi: (bi, 0, hp))
    gspec = pl.BlockSpec((None, None, s, PAIR), lambda hp, bi: (bi, hp, 0, 0))
    slabs = [w.reshape(n_steps, -1, w.shape[-1]) for w in weights_f32]
    slab_specs = [pl.BlockSpec((None,) + w.shape[1:], lambda hp, bi: (hp * b + bi, 0, 0)) for w in slabs]
    n_attn_in, n_ret_in, n_w = 10, 7, len(slabs)
    attn_scratch = [
        pltpu.VMEM((N_BIAS_VARIANTS, 2 * ATTN_TQ, ATTN_W), F32),
        pltpu.VMEM((3, s, PAIR), F32),
        pltpu.VMEM((3, s, PAIR), F32),
        pltpu.VMEM((3, s, PAIR), F32),
        pltpu.VMEM((s, PAIR), F32),
    ]
    ret_scratch = [pltpu.VMEM((s // c, 2 * PAIR, PAIR), F32),
                   pltpu.VMEM((s // c, 2 * PAIR, PAIR), BF16)]

    def kernel(*refs):
        ins, rest = refs[:n_attn_in + n_ret_in + n_w], refs[n_attn_in + n_ret_in + n_w:]
        attn_o, ret_o, w_out, scratch = rest[0], rest[1], rest[2:2 + n_w], rest[2 + n_w:]
        _attention_kernel(*ins[:n_attn_in], attn_o, *scratch[:len(attn_scratch)])
        _retention_kernel(*ins[n_attn_in:n_attn_in + n_ret_in], ret_o, *scratch[len(attn_scratch):])
        for src, dst in zip(ins[n_attn_in + n_ret_in:], w_out):
            dst[...] = src[...].astype(BF16)

    outs = pl.pallas_call(
        kernel,
        grid=(N_PAIRS, b),
        in_specs=[spec] * 3 + [gspec] * 6
        + [pl.BlockSpec((None, 2 * N_BIAS_VARIANTS, 2 * ATTN_W), lambda hp, bi: (hp, 0, 0))]
        + [spec] * 4
        + [pl.BlockSpec((None, 2 * c, c), lambda hp, bi: (hp, 0, 0)),
           pl.BlockSpec((None, 4, c, PAIR), lambda hp, bi: (hp, 0, 0, 0)),
           pl.BlockSpec((None, 2, PAIR, PAIR), lambda hp, bi: (hp, 0, 0, 0))]
        + slab_specs,
        out_specs=[spec, spec] + slab_specs,
        out_shape=[jax.ShapeDtypeStruct((b, s, ATTN_WIDTH), BF16), jax.ShapeDtypeStruct((b, s, RET_WIDTH), BF16)]
        + [jax.ShapeDtypeStruct(w.shape, BF16) for w in slabs],
        scratch_shapes=attn_scratch + ret_scratch,
        compiler_params=pltpu.CompilerParams(
            dimension_semantics=("arbitrary", "arbitrary"), vmem_limit_bytes=VMEM_LIMIT),
        name="mixers",
    )(*qkv, *qkv_grouped, bias_rows, *ret_inputs, *ret_tables, *slabs)
    return outs[0], outs[1], [o.reshape(w.shape) for o, w in zip(outs[2:], weights_f32)]


def _route_tile(sub, attn_ref, ret_ref, x_ref, wo_ref, again_ref, fgain_ref,
                rw_ref, rb_ref, h_ref, xs_ref, dest_ref, cnt_ref):
    tm = ROUTE_TM
    rows = slice(sub * tm, (sub + 1) * tm)
    a = _rms(attn_ref[rows, :].astype(F32), again_ref[...]).astype(BF16)
    mixed = jnp.concatenate([a, ret_ref[rows, :]], axis=1)
    h = x_ref[rows, :] + jnp.dot(mixed, wo_ref[...], preferred_element_type=F32)
    h_ref[rows, :] = h.astype(h_ref.dtype)
    hn = _rms(h, fgain_ref[...]).astype(BF16)

    logits = jnp.dot(hn, rw_ref[...], preferred_element_type=F32) + rb_ref[...]
    lane = lax.broadcasted_iota(jnp.int32, logits.shape, 1)
    big = jnp.int32(LANES)

    def first_argmax(vals):
        top = jnp.max(vals, axis=-1, keepdims=True)
        return top, jnp.min(jnp.where(vals == top, lane, big), axis=-1, keepdims=True)

    gl = jnp.where((lane >= N_EXPERTS) & (lane < N_EXPERTS + N_GROUPS), logits, -jnp.inf)
    gmax, gidx = first_argmax(gl)
    p_group = 1.0 / jnp.sum(jnp.exp(gl - gmax), axis=-1, keepdims=True)
    grp = gidx - N_EXPERTS
    lo = grp * EXPERTS_PER_GROUP
    el = jnp.where((lane >= lo) & (lane < lo + EXPERTS_PER_GROUP), logits, -jnp.inf)
    v1, i1 = first_argmax(el)
    v2, i2 = first_argmax(jnp.where(lane == i1, -jnp.inf, el))
    e2 = jnp.exp(v2 - v1)
    p1 = p_group / (1.0 + e2)
    gates = jnp.where(lane == i1, p1, jnp.where(lane == i2, p1 * e2, 0.0))

    onehot = jnp.where(lane == grp, 1.0, 0.0)
    r_i = lax.broadcasted_iota(jnp.int32, (tm, tm), 0)
    c_i = lax.broadcasted_iota(jnp.int32, (tm, tm), 1)
    before = jnp.where(c_i < r_i, 1.0, 0.0).astype(BF16)
    rank = jnp.dot(before, onehot.astype(BF16), preferred_element_type=F32)
    cnt = jnp.sum(onehot, axis=0, keepdims=True)
    c16 = jnp.broadcast_to(jnp.ceil(cnt * (1.0 / CHUNK)) * CHUNK, (8, LANES))
    lane8 = lax.broadcasted_iota(jnp.int32, (8, LANES), 1)
    start = jnp.zeros((8, LANES), F32)
    for sft in range(1, N_GROUPS):
        start = start + jnp.where(lane8 >= sft, pltpu.roll(c16, sft, 1), 0.0)
    dest = jnp.sum(onehot * (start[0:1] + rank), axis=-1, keepdims=True)
    dest_ref[rows, :] = jnp.broadcast_to(dest, (tm, LANES))
    cnt_ref[sub] = jnp.where(lane8 < N_GROUPS, c16, pltpu.roll(start, N_GROUPS, 1)).astype(jnp.int32)

    onehot_t = onehot.T
    rank_t = lax.dot_general(onehot_t.astype(BF16), before, (((1,), (1,)), ((), ())),
                             preferred_element_type=F32)
    g_row = lax.broadcasted_iota(jnp.int32, (LANES, LANES), 0)
    g_col = lax.broadcasted_iota(jnp.int32, (LANES, LANES), 1)
    start_t = jnp.sum(jnp.where(g_col < g_row, c16[0:1], 0.0), axis=-1, keepdims=True)
    dest_t = jnp.sum(onehot_t * (start_t + rank_t), axis=0, keepdims=True)
    perm = jnp.where(lax.broadcasted_iota(jnp.int32, (SORT_ROWS, tm), 0) == dest_t.astype(jnp.int32),
                     1.0, 0.0).astype(BF16)
    g_hi = gates.astype(BF16)
    g_lo = (gates - g_hi.astype(F32)).astype(BF16)
    payload = jnp.concatenate([hn, g_hi, g_lo], axis=1)
    xs_ref[sub * SORT_ROWS:(sub + 1) * SORT_ROWS, :] = (
        jnp.dot(perm, payload, preferred_element_type=F32).astype(BF16))


def _outproj_kernel(*refs):
    for sub in range(ROUTE_TILES_PER_STEP):
        _route_tile(sub, *refs)


def _outproj(attn, ret, x, w_o, again, fgain, rw, rb):
    t, d = x.shape
    per_step = ROUTE_TILES_PER_STEP
    tm = ROUTE_TM * per_step
    nt = t // ROUTE_TM
    row = lambda w: pl.BlockSpec((tm, w), lambda i: (i, 0))
    full = lambda r, c: pl.BlockSpec((r, c), lambda i: (0, 0))
    return pl.pallas_call(
        _outproj_kernel,
        grid=(nt // per_step,),
        in_specs=[row(ATTN_WIDTH), row(RET_WIDTH), row(d), full(ATTN_WIDTH + RET_WIDTH, d),
                  full(1, ATTN_WIDTH), full(1, d), full(d, LANES), full(1, LANES)],
        out_specs=[row(d), pl.BlockSpec((per_step * SORT_ROWS, XS_W), lambda i: (i, 0)), row(LANES),
                   pl.BlockSpec((per_step, 8, LANES), lambda i: (i, 0, 0))],
        out_shape=[jax.ShapeDtypeStruct((t, d), BF16),
                   jax.ShapeDtypeStruct((nt * SORT_ROWS, XS_W), BF16),
                   jax.ShapeDtypeStruct((t, LANES), F32),
                   jax.ShapeDtypeStruct((nt, 8, LANES), jnp.int32)],
        compiler_params=pltpu.CompilerParams(
            dimension_semantics=("arbitrary",), vmem_limit_bytes=VMEM_LIMIT),
        name="outproj",
    )(attn, ret, x, w_o, again, fgain, rw, rb)


def _dispatch_tables(cnt, n_moe_tiles):
    nt = cnt.shape[0]
    cpt = MOE_TM // CHUNK

    n_slots = n_moe_tiles * cpt
    slot_rows = -(-n_slots // (8 * LANES)) * 8

    def schedule_kernel(seg_ref, grp_ref, used_ref, src_ref, tail_ref):
        slot = (lax.broadcasted_iota(jnp.int32, (slot_rows, LANES), 0) * LANES
                + lax.broadcasted_iota(jnp.int32, (slot_rows, LANES), 1))
        src = jnp.full((slot_rows, LANES), -1, jnp.int32)
        pos = jnp.int32(0)
        for g in range(N_GROUPS):
            def tile_body(i, carry):
                p, src = carry
                n = seg_ref[i, g] // CHUNK
                first = (i * SORT_ROWS + seg_ref[i, N_GROUPS + g]) // CHUNK
                src = jnp.where((slot >= p) & (slot < p + n), slot + (first - p), src)
                return p + n, src
            end, src = lax.fori_loop(0, nt, tile_body, (pos, src))
            padded = ((end + cpt - 1) // cpt) * cpt

            def mark_body(m, carry):
                grp_ref[m] = g
                used_ref[m] = 1
                return carry
            lax.fori_loop(pos // cpt, padded // cpt, mark_body, 0)
            pos = padded

        def idle_body(m, carry):
            grp_ref[m] = N_GROUPS - 1
            used_ref[m] = 0
            return carry
        lax.fori_loop(pos // cpt, n_moe_tiles, idle_body, 0)
        src_ref[...] = src

        def tail_body(i, carry):
            rows = seg_ref[i, 0]
            for g in range(1, N_GROUPS):
                rows = rows + seg_ref[i, g]
            tail_ref[i] = rows // CHUNK
            return carry
        lax.fori_loop(0, nt, tail_body, 0)

    smem = lambda: pl.BlockSpec(memory_space=pltpu.SMEM)
    tile_group, tile_used, src, tile_tail = pl.pallas_call(
        schedule_kernel,
        in_specs=[smem()],
        out_specs=[smem(), smem(), pl.BlockSpec(memory_space=pltpu.VMEM), smem()],
        out_shape=[jax.ShapeDtypeStruct((n_moe_tiles,), jnp.int32),
                   jax.ShapeDtypeStruct((n_moe_tiles,), jnp.int32),
                   jax.ShapeDtypeStruct((slot_rows, LANES), jnp.int32),
                   jax.ShapeDtypeStruct((nt,), jnp.int32)],
        name="schedule",
    )(cnt[:, 0, :2 * N_GROUPS])
    return tile_group, tile_used, src.reshape(-1)[:n_slots], tile_tail


def _moe_kernel(grp_ref, used_ref, src_ref, tail_ref, xs_hbm, w1_ref, w3_ref, w2_ref, ys_hbm,
                xbuf, obuf, zbuf, in_sem, out_sem, zero_sem):
    m = pl.program_id(0)
    n_tiles = pl.num_programs(0)
    cpt = MOE_TM // CHUNK
    cps = SORT_ROWS // CHUNK
    slot = m % 2
    zero_chunk = cps - 1

    def rows(c):
        return pl.ds(pl.multiple_of(c * CHUNK, CHUNK), CHUNK)

    def zero_tails(wait):
        def tile_body(i, carry):
            def body(c, carry2):
                cp = pltpu.make_async_copy(zbuf, ys_hbm.at[rows(i * cps + c), :], zero_sem)
                cp.wait() if wait else cp.start()
                return carry2
            lax.fori_loop(tail_ref[i], cps, body, 0)
            return carry
        lax.fori_loop(0, tail_ref.shape[0], tile_body, 0)

    def gather_start(tile, sl):
        for c in range(cpt):
            src = src_ref[tile * cpt + c]
            src = jnp.where(src < 0, zero_chunk, src)
            pltpu.make_async_copy(xs_hbm.at[rows(src), :], xbuf.at[sl, c * CHUNK:(c + 1) * CHUNK, :],
                                  in_sem.at[sl]).start()

    def gather_wait(sl):
        pltpu.make_async_copy(xs_hbm.at[0:MOE_TM, :], xbuf.at[sl], in_sem.at[sl]).wait()

    def scatter(tile, sl, wait):
        full = src_ref[tile * cpt + cpt - 1] >= 0

        def chunk_copy(c, src):
            return pltpu.make_async_copy(obuf.at[sl, c * CHUNK:(c + 1) * CHUNK, :], ys_hbm.at[rows(src), :],
                                         out_sem.at[sl])

        @pl.when(full)
        def _():
            if wait:
                pltpu.make_async_copy(obuf.at[sl], ys_hbm.at[0:MOE_TM, :], out_sem.at[sl]).wait()
            else:
                for c in range(cpt):
                    chunk_copy(c, src_ref[tile * cpt + c]).start()

        @pl.when(jnp.logical_not(full))
        def _():
            for c in range(cpt):
                src = src_ref[tile * cpt + c]

                @pl.when(src >= 0)
                def _():
                    cp = chunk_copy(c, src)
                    cp.wait() if wait else cp.start()

    @pl.when(m == 0)
    def _():
        gather_start(0, 0)
        zbuf[...] = jnp.zeros(zbuf.shape, zbuf.dtype)
        zero_tails(False)

    prev_used = used_ref[jnp.maximum(m - 1, 0)] > 0

    @pl.when((m == 0) | prev_used)
    def _():
        gather_wait(slot)

    @pl.when((m >= 2) & (used_ref[jnp.maximum(m - 2, 0)] > 0))
    def _():
        scatter(m - 2, slot, True)

    @pl.when(used_ref[m] > 0)
    def _():
        gather_start(jnp.minimum(m + 1, n_tiles - 1), 1 - slot)
        x = xbuf[slot, :, :D_MODEL]
        gate = (xbuf[slot, :, D_MODEL:D_MODEL + LANES].astype(F32)
                + xbuf[slot, :, D_MODEL + LANES:].astype(F32))
        lane = lax.broadcasted_iota(jnp.int32, gate.shape, 1)
        base = grp_ref[m] * EXPERTS_PER_GROUP
        acc = jnp.zeros((MOE_TM, D_MODEL), F32)
        for j in range(EXPERTS_PER_GROUP):
            a = jnp.dot(x, w1_ref[j], preferred_element_type=F32)
            b = jnp.dot(x, w3_ref[j], preferred_element_type=F32)
            gj = jnp.sum(jnp.where(lane == base + j, gate, 0.0), axis=-1, keepdims=True)
            hid = (a * jax.nn.sigmoid(a) * b * gj).astype(BF16)
            acc = acc + jnp.dot(hid, w2_ref[j], preferred_element_type=F32)
        obuf[slot] = acc.astype(BF16)
        scatter(m, slot, False)

    @pl.when(m == n_tiles - 1)
    def _():
        zero_tails(True)

        @pl.when(used_ref[m] > 0)
        def _():
            gather_wait(1 - slot)
            scatter(m, slot, True)

        @pl.when((m >= 1) & prev_used)
        def _():
            scatter(m - 1, 1 - slot, True)


def _moe(xs, w1, w3, w2, tile_group, tile_used, src_chunk, tile_tail):
    n_moe_tiles = tile_group.shape[0]
    rows = xs.shape[0]
    d = D_MODEL
    wspec = lambda r, c: pl.BlockSpec((EXPERTS_PER_GROUP, r, c), lambda m, grp, *_: (grp[m], 0, 0))
    return pl.pallas_call(
        _moe_kernel,
        grid_spec=pltpu.PrefetchScalarGridSpec(
            num_scalar_prefetch=4,
            grid=(n_moe_tiles,),
            in_specs=[pl.BlockSpec(memory_space=pl.ANY),
                      wspec(d, EXPERT_FF), wspec(d, EXPERT_FF), wspec(EXPERT_FF, d)],
            out_specs=pl.BlockSpec(memory_space=pl.ANY),
            scratch_shapes=[pltpu.VMEM((2, MOE_TM, XS_W), BF16),
                            pltpu.VMEM((2, MOE_TM, d), BF16),
                            pltpu.VMEM((CHUNK, d), BF16),
                            pltpu.SemaphoreType.DMA((2,)),
                            pltpu.SemaphoreType.DMA((2,)),
                            pltpu.SemaphoreType.DMA(())]),
        out_shape=jax.ShapeDtypeStruct((rows, d), BF16),
        compiler_params=pltpu.CompilerParams(
            dimension_semantics=("arbitrary",), vmem_limit_bytes=VMEM_LIMIT),
        name="moe",
    )(tile_group, tile_used, src_chunk, tile_tail, xs, w1, w3, w2)


def _combine_kernel(ys_ref, h_ref, dest_ref, gain_ref, o_ref):
    tm = ROUTE_TM
    for sub in range(h_ref.shape[0] // tm):
        rows = slice(sub * tm, (sub + 1) * tm)
        dest = dest_ref[rows, 0:1].astype(jnp.int32)
        perm_t = jnp.where(lax.broadcasted_iota(jnp.int32, (tm, SORT_ROWS), 1) == dest, 1.0, 0.0).astype(BF16)
        moe = jnp.dot(perm_t, ys_ref[sub * SORT_ROWS:(sub + 1) * SORT_ROWS, :], preferred_element_type=F32)
        o_ref[rows, :] = _rms(h_ref[rows, :].astype(F32) + moe, gain_ref[...])


def _combine(ys, h, dest, gain):
    t, d = h.shape
    per_step = ROUTE_TILES_PER_STEP
    tm = ROUTE_TM * per_step
    return pl.pallas_call(
        _combine_kernel,
        grid=(t // tm,),
        in_specs=[pl.BlockSpec((per_step * SORT_ROWS, d), lambda i: (i, 0)),
                  pl.BlockSpec((tm, d), lambda i: (i, 0)),
                  pl.BlockSpec((tm, LANES), lambda i: (i, 0)),
                  pl.BlockSpec((1, d), lambda i: (0, 0))],
        out_specs=pl.BlockSpec((tm, d), lambda i: (i, 0)),
        out_shape=jax.ShapeDtypeStruct((t, d), F32),
        compiler_params=pltpu.CompilerParams(
            dimension_semantics=("arbitrary",), vmem_limit_bytes=VMEM_LIMIT),
        name="combine",
    )(ys, h, dest, gain)


def _rotary_tables(s):
    half = HEAD_DIM // 2
    inv = ROPE_BASE ** (-jnp.arange(half, dtype=F32) / half)
    ang = jnp.arange(s, dtype=F32)[:, None] * inv[None, :]
    cos, sin = jnp.cos(ang), jnp.sin(ang)
    cos_t = jnp.tile(jnp.concatenate([cos, cos], axis=-1), (1, LANES // HEAD_DIM))
    sin_t = jnp.tile(jnp.concatenate([-sin, sin], axis=-1), (1, LANES // HEAD_DIM))
    return cos_t, sin_t


def kernel(x, w_in, w_out, norm_mix, norm_ffn, norm_final, attn_out_gain, rel_bias, ret_decay_fwd, ret_decay_bwd, router_group_w, router_group_b, router_expert_w, router_expert_b, expert_w1, expert_w3, expert_w2):
    b, s, d = x.shape
    depth = w_in.shape[0]
    cos_t, sin_t = _rotary_tables(s)
    bias_rows = _attn_bias_rows(rel_bias)
    h = x
    for layer in range(depth):
        (aq, ak, av, rq, rk, rv, rg), qkv_grouped = _inproj(
            h, norm_mix[layer][None], w_in[layer].astype(BF16), cos_t, sin_t)
        attn, ret, (w1, w3, w2, w_o) = _mixers(
            (aq, ak, av), qkv_grouped, bias_rows, (rq, rk, rv, rg),
            _retention_tables(ret_decay_fwd[layer], ret_decay_bwd[layer]),
            (expert_w1[layer], expert_w3[layer], expert_w2[layer], w_out[layer]))

        rw = jnp.concatenate(
            [jnp.transpose(router_expert_w[layer], (1, 0, 2)).reshape(d, N_EXPERTS),
             router_group_w[layer],
             jnp.zeros((d, LANES - N_EXPERTS - N_GROUPS), F32)], axis=1).astype(BF16)
        rb = jnp.concatenate(
            [router_expert_b[layer].reshape(N_EXPERTS), router_group_b[layer],
             jnp.zeros((LANES - N_EXPERTS - N_GROUPS,), F32)])[None].astype(F32)
        h1, xs, dest, cnt = _outproj(
            attn.reshape(b * s, ATTN_WIDTH), ret.reshape(b * s, RET_WIDTH), h.reshape(b * s, d),
            w_o, attn_out_gain[layer][None], norm_ffn[layer][None],
            rw, rb)
        n_route_tiles = (b * s) // ROUTE_TM
        n_moe_tiles = (b * s + n_route_tiles * N_GROUPS * (CHUNK - 1)) // MOE_TM + N_GROUPS
        schedule = _dispatch_tables(cnt, n_moe_tiles)
        ys = _moe(xs, w1, w3, w2, *schedule)
        assert depth == 1, "the combine kernel fuses the final norm, so it must run on the last layer"
        h = _combine(ys, h1, dest, norm_final[None]).reshape(b, s, d)
    return h
```

```python
import math

import jax
import jax.numpy as jnp
from jax import lax
from jax.experimental import pallas as pl
from jax.experimental.pallas import tpu as pltpu

F32 = jnp.float32
BF16 = jnp.bfloat16

D_MODEL = 1024
HEAD_DIM = 64
ATTN_WIDTH = 512
RET_WIDTH = 512
N_HEADS = 8
PAIR = 2 * HEAD_DIM
N_PAIRS = N_HEADS // 2
ATTN_DILATIONS = (1, 4, 16)
ATTN_RADIUS = 64
N_BUCKETS = 32
REL_MAX_DIST = 1024
ROPE_BASE = 10000.0
N_GROUPS = 4
EXPERTS_PER_GROUP = 4
N_EXPERTS = 16
EXPERT_FF = 512
EPS = 1e-6
NEG_INF = -1e30
LOG2_E = math.log2(math.e)

LANES = 128
ATTN_TQ = 128
ATTN_W = 256
N_BIAS_VARIANTS = 8
RET_CHUNK = 256
ROUTE_TM = 512
ROUTE_TILES_PER_STEP = 2
SORT_ROWS = 592
CHUNK = 16
MOE_TM = 512
XS_W = D_MODEL + 2 * LANES
VMEM_LIMIT = 48 * 1024 * 1024


def _rms(x, gain):
    return x * lax.rsqrt(jnp.mean(x * x, axis=-1, keepdims=True) + EPS) * gain


def _inproj_kernel(x_ref, gain_ref, w_ref, cos_ref, sin_ref,
                   aq_ref, ak_ref, av_ref, rq_ref, rk_ref, rv_ref, rg_ref,
                   aq4_ref, ak4_ref, av4_ref, aq16_ref, ak16_ref, av16_ref, stage_ref, stage4_ref):
    tm = x_ref.shape[0]
    xn = _rms(x_ref[...], gain_ref[...]).astype(BF16)

    def stage(i, t, nat_ref):
        nat_ref[...] = t.astype(BF16)
        for hp in range(N_PAIRS):
            stage_ref[i, hp] = t[:, hp * PAIR:(hp + 1) * PAIR]

    def regroup4(i, d4_ref):
        for hp in range(N_PAIRS):
            for r4 in range(4):
                g4 = stage_ref[i, hp, pl.ds(r4, tm // 4, stride=4), :]
                d4_ref[hp, r4] = g4.astype(BF16)
                stage4_ref[i, hp, r4] = g4

    def regroup16(i, d16_ref):
        for hp in range(N_PAIRS):
            for r4 in range(4):
                for j in range(4):
                    d16_ref[hp, r4 + 4 * j] = (
                        stage4_ref[i, hp, r4, pl.ds(j, tm // 16, stride=4), :].astype(BF16))

    def seg(i):
        return jnp.dot(xn, w_ref[:, i * 512:(i + 1) * 512], preferred_element_type=F32)

    def rotary(t):
        cos, sin = cos_ref[...], sin_ref[...]
        first_half = (lax.broadcasted_iota(jnp.int32, (1, LANES), 1) % HEAD_DIM) < HEAD_DIM // 2
        outs = []
        for j in range(t.shape[1] // LANES):
            tj = t[:, j * LANES:(j + 1) * LANES]
            partner = jnp.where(first_half, pltpu.roll(tj, LANES - 32, 1), pltpu.roll(tj, 32, 1))
            outs.append(tj * cos + partner * sin)
        return jnp.concatenate(outs, axis=1)

    stage(0, seg(0) * (HEAD_DIM ** -0.5 * LOG2_E), aq_ref)
    stage(1, seg(1), ak_ref)
    stage(2, seg(2), av_ref)
    rq_ref[...] = rotary(seg(3)).astype(BF16)
    for i, d4_ref in enumerate((aq4_ref, ak4_ref, av4_ref)):
        regroup4(i, d4_ref)
    rk_ref[...] = (rotary(seg(4)) * (HEAD_DIM ** -0.5)).astype(BF16)
    rv_ref[...] = seg(5).astype(BF16)
    for i, d16_ref in enumerate((aq16_ref, ak16_ref, av16_ref)):
        regroup16(i, d16_ref)
    rg_ref[...] = seg(6).astype(BF16)


def _inproj(x, gain, w_in, cos_t, sin_t, tm=512):
    b, s, d = x.shape
    n = w_in.shape[1]
    out = jax.ShapeDtypeStruct((b, s, 512), BF16)
    ospec = pl.BlockSpec((None, tm, 512), lambda si, bi: (bi, si, 0))

    def grouped(dil):
        shape = jax.ShapeDtypeStruct((b, s // tm, N_PAIRS, dil, tm // dil, PAIR), BF16)
        spec = pl.BlockSpec((None, None, N_PAIRS, dil, tm // dil, PAIR), lambda si, bi: (bi, si, 0, 0, 0, 0))
        return [shape] * 3, [spec] * 3

    shapes4, specs4 = grouped(4)
    shapes16, specs16 = grouped(16)
    outs = pl.pallas_call(
        _inproj_kernel,
        grid=(s // tm, b),
        in_specs=[
            pl.BlockSpec((None, tm, d), lambda si, bi: (bi, si, 0)),
            pl.BlockSpec((1, d), lambda si, bi: (0, 0)),
            pl.BlockSpec((d, n), lambda si, bi: (0, 0)),
            pl.BlockSpec((tm, LANES), lambda si, bi: (si, 0)),
            pl.BlockSpec((tm, LANES), lambda si, bi: (si, 0)),
        ],
        out_specs=[ospec] * 7 + specs4 + specs16,
        out_shape=[out] * 7 + shapes4 + shapes16,
        scratch_shapes=[pltpu.VMEM((3, N_PAIRS, tm, PAIR), F32),
                        pltpu.VMEM((3, N_PAIRS, 4, tm // 4, PAIR), F32)],
        compiler_params=pltpu.CompilerParams(
            dimension_semantics=("arbitrary", "arbitrary"), vmem_limit_bytes=VMEM_LIMIT),
        name="inproj",
    )(x, gain, w_in, cos_t, sin_t)
    return outs[:7], outs[7:]


def _t5_bucket(rel):
    half = N_BUCKETS // 2
    max_exact = half // 2
    offset = jnp.where(rel > 0, half, 0)
    n = jnp.abs(rel)
    nf = jnp.maximum(n, 1).astype(F32)
    large = max_exact + (jnp.log(nf / max_exact) / math.log(REL_MAX_DIST / max_exact)
                         * (half - max_exact)).astype(jnp.int32)
    large = jnp.minimum(large, half - 1)
    return offset + jnp.where(n < max_exact, n, large)


def _attn_bias_rows(rel_bias):
    period = 2 * ATTN_W
    band = 2 * ATTN_RADIUS + 1
    rel = jnp.arange(-ATTN_RADIUS, ATTN_RADIUS + 1)
    rows = []
    for dil, offs in ((1, (0, 64, 128)), (4, (0, 64, 128)), (16, (0, 128))):
        vals = rel_bias[_t5_bucket(rel * dil)].astype(F32).T * LOG2_E
        for off in offs:
            lo = off - ATTN_RADIUS
            pad = jnp.full((N_HEADS, period - band), NEG_INF, F32)
            if lo >= 0:
                row = jnp.concatenate([pad[:, :lo], vals, pad[:, lo:]], axis=1)
            else:
                row = jnp.concatenate([vals[:, -lo:], pad, vals[:, :-lo]], axis=1)
            rows.append(row)
    v = jnp.stack(rows, axis=1)
    return v.reshape(N_PAIRS, 2 * N_BIAS_VARIANTS, period)


def _attention_kernel(q_ref, k_ref, v_ref, q4_ref, k4_ref, v4_ref, q16_ref, k16_ref, v16_ref,
                      rows_ref, o_ref, bias_ref, acc_ref, m_ref, l_ref, out_ref):
    s = q_ref.shape[0]
    n_tiles = s // ATTN_TQ
    lane = lax.broadcasted_iota(jnp.int32, (1, PAIR), 1)
    head0 = lane < HEAD_DIM

    @pl.when(pl.program_id(1) == 0)
    def _():
        col = lax.broadcasted_iota(jnp.int32, (ATTN_TQ, ATTN_W), 1)
        for idx in range(2 * N_BIAS_VARIANTS):
            gen = jnp.broadcast_to(rows_ref[idx:idx + 1, :], (ATTN_TQ, 2 * ATTN_W))
            tab = pltpu.roll(gen, 0, 1, stride=1, stride_axis=0)[:, :ATTN_W]
            var = idx % N_BIAS_VARIANTS
            if var >= 6:
                tab = jnp.where((col // ATTN_TQ) == var - 6, tab, NEG_INF)
            head = idx // N_BIAS_VARIANTS
            bias_ref[var, head * ATTN_TQ:(head + 1) * ATTN_TQ, :] = tab

    def rows_of(ref, dil, lo, n):
        if dil == 1:
            return ref[lo:lo + n, :]
        piece, sub_len = ref.shape[2], s // dil
        parts, pos = [], lo
        while pos < lo + n:
            within = pos % sub_len
            off = within % piece
            take = min(piece - off, lo + n - pos)
            parts.append(ref[within // piece, pos // sub_len, off:off + take, :])
            pos += take
        return parts[0] if len(parts) == 1 else jnp.concatenate(parts, axis=0)

    def run_branch(bi, dil, qs_ref, ks_ref, vs_ref):
        sub_len = s // dil
        tiles_per_sub = sub_len // ATTN_TQ

        def tile(t):
            q0 = t * ATTN_TQ
            if tiles_per_sub == 1:
                ws = (t // 2) * ATTN_W
                var = 6 + t % 2
            else:
                pos = t % tiles_per_sub
                sub_lo = (t // tiles_per_sub) * sub_len
                ws = min(max(q0 - 64, sub_lo), sub_lo + sub_len - ATTN_W)
                var = (0 if pos == 0 else 2 if pos == tiles_per_sub - 1 else 1) + 3 * bi
            q = rows_of(qs_ref, dil, q0, ATTN_TQ)
            k = rows_of(ks_ref, dil, ws, ATTN_W)
            v = rows_of(vs_ref, dil, ws, ATTN_W)
            q2 = jnp.concatenate([jnp.where(head0, q, jnp.zeros_like(q)),
                                  jnp.where(head0, jnp.zeros_like(q), q)], axis=0)
            sc = lax.dot_general(q2, k, (((1,), (1,)), ((), ())), preferred_element_type=F32)
            sc = sc + bias_ref[var]
            m = jnp.max(sc, axis=-1, keepdims=True)
            p = jnp.exp2(sc - m).astype(BF16)
            o = jnp.dot(p, jnp.concatenate([v, jnp.ones_like(v)], axis=1), preferred_element_type=F32)
            l = o[:, PAIR:]
            outs, ms, ls = (o[:ATTN_TQ, :PAIR], o[ATTN_TQ:, :PAIR]), (m[:ATTN_TQ], m[ATTN_TQ:]), (l[:ATTN_TQ], l[ATTN_TQ:])
            if dil == 1:
                dst = pl.ds(q0, ATTN_TQ)
            elif dil == 4:
                dst = pl.ds((t % 4) * (4 * ATTN_TQ) + t // 4, ATTN_TQ, stride=4)
            else:
                dst = pl.ds((t % 4) * (s // 4) + t // 4, ATTN_TQ, stride=4)
            acc_ref[bi, dst, :] = jnp.where(head0, outs[0], outs[1])
            m_ref[bi, dst, :] = jnp.where(head0, ms[0], ms[1])
            l_ref[bi, dst, :] = jnp.where(head0, ls[0], ls[1])

        for t in range(n_tiles):
            tile(t)

    run_branch(0, 1, q_ref, k_ref, v_ref)
    run_branch(1, 4, q4_ref, k4_ref, v4_ref)
    run_branch(2, 16, q16_ref, k16_ref, v16_ref)

    rows = ATTN_TQ
    for r4 in range(4):
        for blk in range(s // (4 * rows)):
            nat = pl.ds(r4 + 4 * rows * blk, rows, stride=4)
            sl = (nat, nat, pl.ds(r4 * (s // 4) + rows * blk, rows))
            m = [m_ref[bi, sl[bi], :] for bi in range(3)]
            mx = jnp.maximum(jnp.maximum(m[0], m[1]), m[2])
            num = jnp.zeros((rows, PAIR), F32)
            den = jnp.zeros((rows, PAIR), F32)
            for bi in range(3):
                e = jnp.exp2(m[bi] - mx)
                num = num + e * acc_ref[bi, sl[bi], :]
                den = den + e * l_ref[bi, sl[bi], :]
            out_ref[nat, :] = num / den
    o_ref[...] = out_ref[...].astype(o_ref.dtype)


def _retention_tables(decay_fwd, decay_bwd):
    c = RET_CHUNK
    lg_f = -jnp.exp(decay_fwd.astype(F32))
    lg_b = -jnp.exp(decay_bwd.astype(F32))
    idx = jnp.arange(c, dtype=F32)
    rel = idx[:, None] - idx[None, :]
    dmat = jnp.where(rel >= 0,
                     jnp.exp(lg_f[:, None, None] * jnp.maximum(rel, 0.0)[None]),
                     jnp.exp(lg_b[:, None, None] * jnp.maximum(-rel, 0.0)[None]))
    dmat = dmat.reshape(N_PAIRS, 2 * c, c)

    def lanes(v):
        v = v.reshape(N_PAIRS, 2, -1)
        return jnp.repeat(jnp.transpose(v, (0, 2, 1)), HEAD_DIM, axis=2)

    vec = jnp.stack([
        lanes(jnp.exp(lg_f[:, None] * (idx + 1.0)[None])),
        lanes(jnp.exp(lg_f[:, None] * (c - 1.0 - idx)[None])),
        lanes(jnp.exp(lg_b[:, None] * (c - idx)[None])),
        lanes(jnp.exp(lg_b[:, None] * idx[None])),
    ], axis=1)
    same_head = (jnp.arange(PAIR)[:, None] // HEAD_DIM) == (jnp.arange(PAIR)[None, :] // HEAD_DIM)
    cd = jnp.stack([lanes(jnp.exp(lg_f * c)[:, None]), lanes(jnp.exp(lg_b * c)[:, None])], axis=1)
    cd = jnp.transpose(cd, (0, 1, 3, 2)) * same_head[None, None].astype(F32)
    return dmat, vec, cd


def _retention_kernel(q_ref, k_ref, v_ref, g_ref, dmat_ref, vec_ref, cd_ref, o_ref, kv_ref, st_ref):
    s = q_ref.shape[0]
    c = RET_CHUNK
    nc = s // c
    lane = lax.broadcasted_iota(jnp.int32, (1, PAIR), 1)
    head0 = lane < HEAD_DIM
    same_head = ((lax.broadcasted_iota(jnp.int32, (PAIR, PAIR), 0) // HEAD_DIM)
                 == (lax.broadcasted_iota(jnp.int32, (PAIR, PAIR), 1) // HEAD_DIM))

    same_head2 = jnp.concatenate([same_head, same_head], axis=0)
    for n in range(nc):
        rows = slice(n * c, (n + 1) * c)
        kf = k_ref[rows, :].astype(F32)
        kcat = jnp.concatenate([(kf * vec_ref[1]).astype(BF16), (kf * vec_ref[3]).astype(BF16)], axis=1)
        kv = lax.dot_general(kcat, v_ref[rows, :], (((0,), (0,)), ((), ())), preferred_element_type=F32)
        kv_ref[n] = jnp.where(same_head2, kv, 0.0)

    state = jnp.zeros((PAIR, PAIR), F32)
    for n in range(nc):
        st_ref[n, :PAIR, :] = state.astype(BF16)
        state = state * cd_ref[0] + kv_ref[n, :PAIR, :]
    state = jnp.zeros((PAIR, PAIR), F32)
    for n in reversed(range(nc)):
        st_ref[n, PAIR:, :] = state.astype(BF16)
        state = state * cd_ref[1] + kv_ref[n, PAIR:, :]

    for n in range(nc):
        rows = slice(n * c, (n + 1) * c)
        q, k, v = q_ref[rows, :], k_ref[rows, :], v_ref[rows, :]
        q2 = jnp.concatenate([jnp.where(head0, q, jnp.zeros_like(q)),
                              jnp.where(head0, jnp.zeros_like(q), q)], axis=0)
        sc = lax.dot_general(q2, k, (((1,), (1,)), ((), ())), preferred_element_type=F32)
        intra = jnp.dot((sc * dmat_ref[...]).astype(BF16), v, preferred_element_type=F32)
        qf = q.astype(F32)
        qcat = jnp.concatenate([(qf * vec_ref[0]).astype(BF16), (qf * vec_ref[2]).astype(BF16)], axis=1)
        y = jnp.where(head0, intra[:c], intra[c:]) + jnp.dot(qcat, st_ref[n], preferred_element_type=F32)
        y2 = y * y
        ms0 = jnp.sum(jnp.where(head0, y2, 0.0), axis=-1, keepdims=True)
        ms1 = jnp.sum(jnp.where(head0, 0.0, y2), axis=-1, keepdims=True)
        ms = jnp.where(head0, ms0, ms1) * (1.0 / HEAD_DIM)
        g = g_ref[rows, :].astype(F32)
        o_ref[rows, :] = (y * lax.rsqrt(ms + EPS) * (g * jax.nn.sigmoid(g))).astype(o_ref.dtype)


def _mixers(qkv, qkv_grouped, bias_rows, ret_inputs, ret_tables, weights_f32):
    b, s, _ = qkv[0].shape
    c = RET_CHUNK
    n_steps = N_PAIRS * b
    spec = pl.BlockSpec((None, s, PAIR), lambda hp, bi: (bi, 0, hp))
    gspecs = [pl.BlockSpec((None, g.shape[1], None) + g.shape[3:], lambda hp, bi: (bi, 0, hp, 0, 0, 0))
              for g in qkv_grouped]
    slabs = [w.reshape(n_steps, -1, w.shape[-1]) for w in weights_f32]
    slab_specs = [pl.BlockSpec((None,) + w.shape[1:], lambda hp, bi: (hp * b + bi, 0, 0)) for w in slabs]
    n_attn_in, n_ret_in, n_w = 10, 7, len(slabs)
    attn_scratch = [
        pltpu.VMEM((N_BIAS_VARIANTS, 2 * ATTN_TQ, ATTN_W), F32),
        pltpu.VMEM((3, s, PAIR), F32),
        pltpu.VMEM((3, s, PAIR), F32),
        pltpu.VMEM((3, s, PAIR), F32),
        pltpu.VMEM((s, PAIR), F32),
    ]
    ret_scratch = [pltpu.VMEM((s // c, 2 * PAIR, PAIR), F32),
                   pltpu.VMEM((s // c, 2 * PAIR, PAIR), BF16)]

    def kernel(*refs):
        ins, rest = refs[:n_attn_in + n_ret_in + n_w], refs[n_attn_in + n_ret_in + n_w:]
        attn_o, ret_o, w_out, scratch = rest[0], rest[1], rest[2:2 + n_w], rest[2 + n_w:]
        _attention_kernel(*ins[:n_attn_in], attn_o, *scratch[:len(attn_scratch)])
        _retention_kernel(*ins[n_attn_in:n_attn_in + n_ret_in], ret_o, *scratch[len(attn_scratch):])
        for src, dst in zip(ins[n_attn_in + n_ret_in:], w_out):
            dst[...] = src[...].astype(BF16)

    outs = pl.pallas_call(
        kernel,
        grid=(N_PAIRS, b),
        in_specs=[spec] * 3 + gspecs
        + [pl.BlockSpec((None, 2 * N_BIAS_VARIANTS, 2 * ATTN_W), lambda hp, bi: (hp, 0, 0))]
        + [spec] * 4
        + [pl.BlockSpec((None, 2 * c, c), lambda hp, bi: (hp, 0, 0)),
           pl.BlockSpec((None, 4, c, PAIR), lambda hp, bi: (hp, 0, 0, 0)),
           pl.BlockSpec((None, 2, PAIR, PAIR), lambda hp, bi: (hp, 0, 0, 0))]
        + slab_specs,
        out_specs=[spec, spec] + slab_specs,
        out_shape=[jax.ShapeDtypeStruct((b, s, ATTN_WIDTH), BF16), jax.ShapeDtypeStruct((b, s, RET_WIDTH), BF16)]
        + [jax.ShapeDtypeStruct(w.shape, BF16) for w in slabs],
        scratch_shapes=attn_scratch + ret_scratch,
        compiler_params=pltpu.CompilerParams(
            dimension_semantics=("arbitrary", "arbitrary"), vmem_limit_bytes=VMEM_LIMIT),
        name="mixers",
    )(*qkv, *qkv_grouped, bias_rows, *ret_inputs, *ret_tables, *slabs)
    return outs[0], outs[1], [o.reshape(w.shape) for o, w in zip(outs[2:], weights_f32)]


def _route_tile(sub, attn_ref, ret_ref, x_ref, wo_ref, again_ref, fgain_ref,
                rw_ref, rb_ref, h_ref, xs_ref, dest_ref, cnt_ref):
    tm = ROUTE_TM
    rows = slice(sub * tm, (sub + 1) * tm)
    a = _rms(attn_ref[rows, :].astype(F32), again_ref[...]).astype(BF16)
    mixed = jnp.concatenate([a, ret_ref[rows, :]], axis=1)
    h = x_ref[rows, :] + jnp.dot(mixed, wo_ref[...], preferred_element_type=F32)
    h_ref[rows, :] = h.astype(h_ref.dtype)
    hn = _rms(h, fgain_ref[...]).astype(BF16)

    logits = jnp.dot(hn, rw_ref[...], preferred_element_type=F32) + rb_ref[...]
    lane = lax.broadcasted_iota(jnp.int32, logits.shape, 1)
    big = jnp.int32(LANES)

    def first_argmax(vals):
        top = jnp.max(vals, axis=-1, keepdims=True)
        return top, jnp.min(jnp.where(vals == top, lane, big), axis=-1, keepdims=True)

    gl = jnp.where((lane >= N_EXPERTS) & (lane < N_EXPERTS + N_GROUPS), logits, -jnp.inf)
    gmax, gidx = first_argmax(gl)
    p_group = 1.0 / jnp.sum(jnp.exp(gl - gmax), axis=-1, keepdims=True)
    grp = gidx - N_EXPERTS
    lo = grp * EXPERTS_PER_GROUP
    el = jnp.where((lane >= lo) & (lane < lo + EXPERTS_PER_GROUP), logits, -jnp.inf)
    v1, i1 = first_argmax(el)
    v2, i2 = first_argmax(jnp.where(lane == i1, -jnp.inf, el))
    e2 = jnp.exp(v2 - v1)
    p1 = p_group / (1.0 + e2)
    gates = jnp.where(lane == i1, p1, jnp.where(lane == i2, p1 * e2, 0.0))

    onehot = jnp.where(lane == grp, 1.0, 0.0)
    r_i = lax.broadcasted_iota(jnp.int32, (tm, tm), 0)
    c_i = lax.broadcasted_iota(jnp.int32, (tm, tm), 1)
    before = jnp.where(c_i < r_i, 1.0, 0.0).astype(BF16)
    rank = jnp.dot(before, onehot.astype(BF16), preferred_element_type=F32)
    cnt = jnp.sum(onehot, axis=0, keepdims=True)
    c16 = jnp.broadcast_to(jnp.ceil(cnt * (1.0 / CHUNK)) * CHUNK, (8, LANES))
    lane8 = lax.broadcasted_iota(jnp.int32, (8, LANES), 1)
    start = jnp.zeros((8, LANES), F32)
    for sft in range(1, N_GROUPS):
        start = start + jnp.where(lane8 >= sft, pltpu.roll(c16, sft, 1), 0.0)
    dest = jnp.sum(onehot * (start[0:1] + rank), axis=-1, keepdims=True)
    dest_ref[rows, :] = jnp.broadcast_to(dest, (tm, LANES))
    cnt_ref[sub] = jnp.where(lane8 < N_GROUPS, c16, pltpu.roll(start, N_GROUPS, 1)).astype(jnp.int32)

    onehot_t = onehot.T
    rank_t = lax.dot_general(onehot_t.astype(BF16), before, (((1,), (1,)), ((), ())),
                             preferred_element_type=F32)
    g_row = lax.broadcasted_iota(jnp.int32, (LANES, LANES), 0)
    g_col = lax.broadcasted_iota(jnp.int32, (LANES, LANES), 1)
    start_t = jnp.sum(jnp.where(g_col < g_row, c16[0:1], 0.0), axis=-1, keepdims=True)
    dest_t = jnp.sum(onehot_t * (start_t + rank_t), axis=0, keepdims=True)
    perm = jnp.where(lax.broadcasted_iota(jnp.int32, (SORT_ROWS, tm), 0) == dest_t.astype(jnp.int32),
                     1.0, 0.0).astype(BF16)
    g_hi = gates.astype(BF16)
    g_lo = (gates - g_hi.astype(F32)).astype(BF16)
    payload = jnp.concatenate([hn, g_hi, g_lo], axis=1)
    xs_ref[sub * SORT_ROWS:(sub + 1) * SORT_ROWS, :] = (
        jnp.dot(perm, payload, preferred_element_type=F32).astype(BF16))


def _outproj_kernel(*refs):
    for sub in range(ROUTE_TILES_PER_STEP):
        _route_tile(sub, *refs)


def _outproj(attn, ret, x, w_o, again, fgain, rw, rb):
    t, d = x.shape
    per_step = ROUTE_TILES_PER_STEP
    tm = ROUTE_TM * per_step
    nt = t // ROUTE_TM
    row = lambda w: pl.BlockSpec((tm, w), lambda i: (i, 0))
    full = lambda r, c: pl.BlockSpec((r, c), lambda i: (0, 0))
    return pl.pallas_call(
        _outproj_kernel,
        grid=(nt // per_step,),
        in_specs=[row(ATTN_WIDTH), row(RET_WIDTH), row(d), full(ATTN_WIDTH + RET_WIDTH, d),
                  full(1, ATTN_WIDTH), full(1, d), full(d, LANES), full(1, LANES)],
        out_specs=[row(d), pl.BlockSpec((per_step * SORT_ROWS, XS_W), lambda i: (i, 0)), row(LANES),
                   pl.BlockSpec((per_step, 8, LANES), lambda i: (i, 0, 0))],
        out_shape=[jax.ShapeDtypeStruct((t, d), BF16),
                   jax.ShapeDtypeStruct((nt * SORT_ROWS, XS_W), BF16),
                   jax.ShapeDtypeStruct((t, LANES), F32),
                   jax.ShapeDtypeStruct((nt, 8, LANES), jnp.int32)],
        compiler_params=pltpu.CompilerParams(
            dimension_semantics=("arbitrary",), vmem_limit_bytes=VMEM_LIMIT),
        name="outproj",
    )(attn, ret, x, w_o, again, fgain, rw, rb)


def _dispatch_tables(cnt, n_moe_tiles):
    nt = cnt.shape[0]
    cpt = MOE_TM // CHUNK

    n_slots = n_moe_tiles * cpt
    slot_rows = -(-n_slots // (8 * LANES)) * 8

    def schedule_kernel(seg_ref, grp_ref, used_ref, src_ref, tail_ref):
        slot = (lax.broadcasted_iota(jnp.int32, (slot_rows, LANES), 0) * LANES
                + lax.broadcasted_iota(jnp.int32, (slot_rows, LANES), 1))
        src = jnp.full((slot_rows, LANES), -1, jnp.int32)
        pos = jnp.int32(0)
        for g in range(N_GROUPS):
            def tile_body(i, carry):
                p, src = carry
                n = seg_ref[i, g] // CHUNK
                first = (i * SORT_ROWS + seg_ref[i, N_GROUPS + g]) // CHUNK
                src = jnp.where((slot >= p) & (slot < p + n), slot + (first - p), src)
                return p + n, src
            end, src = lax.fori_loop(0, nt, tile_body, (pos, src))
            padded = ((end + cpt - 1) // cpt) * cpt

            def mark_body(m, carry):
                grp_ref[m] = g
                used_ref[m] = 1
                return carry
            lax.fori_loop(pos // cpt, padded // cpt, mark_body, 0)
            pos = padded

        def idle_body(m, carry):
            grp_ref[m] = N_GROUPS - 1
            used_ref[m] = 0
            return carry
        lax.fori_loop(pos // cpt, n_moe_tiles, idle_body, 0)
        src_ref[...] = src

        def tail_body(i, carry):
            rows = seg_ref[i, 0]
            for g in range(1, N_GROUPS):
                rows = rows + seg_ref[i, g]
            tail_ref[i] = rows // CHUNK
            return carry
        lax.fori_loop(0, nt, tail_body, 0)

    smem = lambda: pl.BlockSpec(memory_space=pltpu.SMEM)
    tile_group, tile_used, src, tile_tail = pl.pallas_call(
        schedule_kernel,
        in_specs=[smem()],
        out_specs=[smem(), smem(), pl.BlockSpec(memory_space=pltpu.VMEM), smem()],
        out_shape=[jax.ShapeDtypeStruct((n_moe_tiles,), jnp.int32),
                   jax.ShapeDtypeStruct((n_moe_tiles,), jnp.int32),
                   jax.ShapeDtypeStruct((slot_rows, LANES), jnp.int32),
                   jax.ShapeDtypeStruct((nt,), jnp.int32)],
        name="schedule",
    )(cnt[:, 0, :2 * N_GROUPS])
    return tile_group, tile_used, src.reshape(-1)[:n_slots], tile_tail


def _moe_kernel(grp_ref, used_ref, src_ref, tail_ref, xs_hbm, w1_ref, w3_ref, w2_ref, ys_hbm,
                xbuf, obuf, zbuf, in_sem, out_sem, zero_sem):
    m = pl.program_id(0)
    n_tiles = pl.num_programs(0)
    cpt = MOE_TM // CHUNK
    cps = SORT_ROWS // CHUNK
    slot = m % 2
    zero_chunk = cps - 1

    def rows(c):
        return pl.ds(pl.multiple_of(c * CHUNK, CHUNK), CHUNK)

    def zero_tails(wait):
        def tile_body(i, carry):
            def body(c, carry2):
                cp = pltpu.make_async_copy(zbuf, ys_hbm.at[rows(i * cps + c), :], zero_sem)
                cp.wait() if wait else cp.start()
                return carry2
            lax.fori_loop(tail_ref[i], cps, body, 0)
            return carry
        lax.fori_loop(0, tail_ref.shape[0], tile_body, 0)

    def gather_start(tile, sl):
        for c in range(cpt):
            src = src_ref[tile * cpt + c]
            src = jnp.where(src < 0, zero_chunk, src)
            pltpu.make_async_copy(xs_hbm.at[rows(src), :], xbuf.at[sl, c * CHUNK:(c + 1) * CHUNK, :],
                                  in_sem.at[sl]).start()

    def gather_wait(sl):
        pltpu.make_async_copy(xs_hbm.at[0:MOE_TM, :], xbuf.at[sl], in_sem.at[sl]).wait()

    def scatter(tile, sl, wait):
        full = src_ref[tile * cpt + cpt - 1] >= 0

        def chunk_copy(c, src):
            return pltpu.make_async_copy(obuf.at[sl, c * CHUNK:(c + 1) * CHUNK, :], ys_hbm.at[rows(src), :],
                                         out_sem.at[sl])

        @pl.when(full)
        def _():
            if wait:
                pltpu.make_async_copy(obuf.at[sl], ys_hbm.at[0:MOE_TM, :], out_sem.at[sl]).wait()
            else:
                for c in range(cpt):
                    chunk_copy(c, src_ref[tile * cpt + c]).start()

        @pl.when(jnp.logical_not(full))
        def _():
            for c in range(cpt):
                src = src_ref[tile * cpt + c]

                @pl.when(src >= 0)
                def _():
                    cp = chunk_copy(c, src)
                    cp.wait() if wait else cp.start()

    @pl.when(m == 0)
    def _():
        gather_start(0, 0)
        zbuf[...] = jnp.zeros(zbuf.shape, zbuf.dtype)
        zero_tails(False)

    prev_used = used_ref[jnp.maximum(m - 1, 0)] > 0

    @pl.when((m == 0) | prev_used)
    def _():
        gather_wait(slot)

    @pl.when((m >= 2) & (used_ref[jnp.maximum(m - 2, 0)] > 0))
    def _():
        scatter(m - 2, slot, True)

    def expert_tile(n_rows):
        gather_start(jnp.minimum(m + 1, n_tiles - 1), 1 - slot)
        x = xbuf[slot, :n_rows, :D_MODEL]
        gate = (xbuf[slot, :n_rows, D_MODEL:D_MODEL + LANES].astype(F32)
                + xbuf[slot, :n_rows, D_MODEL + LANES:].astype(F32))
        lane = lax.broadcasted_iota(jnp.int32, gate.shape, 1)
        base = grp_ref[m] * EXPERTS_PER_GROUP
        acc = jnp.zeros((n_rows, D_MODEL), F32)
        for j in range(EXPERTS_PER_GROUP):
            a = jnp.dot(x, w1_ref[j], preferred_element_type=F32)
            b = jnp.dot(x, w3_ref[j], preferred_element_type=F32)
            gj = jnp.sum(jnp.where(lane == base + j, gate, 0.0), axis=-1, keepdims=True)
            hid = (a * jax.nn.sigmoid(a) * b * gj).astype(BF16)
            acc = acc + jnp.dot(hid, w2_ref[j], preferred_element_type=F32)
        obuf[slot, :n_rows, :] = acc.astype(BF16)
        scatter(m, slot, False)

    half_empty = src_ref[m * cpt + cpt // 2] < 0

    @pl.when((used_ref[m] > 0) & jnp.logical_not(half_empty))
    def _():
        expert_tile(MOE_TM)

    @pl.when((used_ref[m] > 0) & half_empty)
    def _():
        expert_tile(MOE_TM // 2)

    @pl.when(m == n_tiles - 1)
    def _():
        zero_tails(True)

        @pl.when(used_ref[m] > 0)
        def _():
            gather_wait(1 - slot)
            scatter(m, slot, True)

        @pl.when((m >= 1) & prev_used)
        def _():
            scatter(m - 1, 1 - slot, True)


def _moe(xs, w1, w3, w2, tile_group, tile_used, src_chunk, tile_tail):
    n_moe_tiles = tile_group.shape[0]
    rows = xs.shape[0]
    d = D_MODEL
    wspec = lambda r, c: pl.BlockSpec((EXPERTS_PER_GROUP, r, c), lambda m, grp, *_: (grp[m], 0, 0))
    return pl.pallas_call(
        _moe_kernel,
        grid_spec=pltpu.PrefetchScalarGridSpec(
            num_scalar_prefetch=4,
            grid=(n_moe_tiles,),
            in_specs=[pl.BlockSpec(memory_space=pl.ANY),
                      wspec(d, EXPERT_FF), wspec(d, EXPERT_FF), wspec(EXPERT_FF, d)],
            out_specs=pl.BlockSpec(memory_space=pl.ANY),
            scratch_shapes=[pltpu.VMEM((2, MOE_TM, XS_W), BF16),
                            pltpu.VMEM((2, MOE_TM, d), BF16),
                            pltpu.VMEM((CHUNK, d), BF16),
                            pltpu.SemaphoreType.DMA((2,)),
                            pltpu.SemaphoreType.DMA((2,)),
                            pltpu.SemaphoreType.DMA(())]),
        out_shape=jax.ShapeDtypeStruct((rows, d), BF16),
        compiler_params=pltpu.CompilerParams(
            dimension_semantics=("arbitrary",), vmem_limit_bytes=VMEM_LIMIT),
        name="moe",
    )(tile_group, tile_used, src_chunk, tile_tail, xs, w1, w3, w2)


def _combine_kernel(ys_ref, h_ref, dest_ref, gain_ref, o_ref):
    tm = ROUTE_TM
    for sub in range(h_ref.shape[0] // tm):
        rows = slice(sub * tm, (sub + 1) * tm)
        dest = dest_ref[rows, 0:1].astype(jnp.int32)
        perm_t = jnp.where(lax.broadcasted_iota(jnp.int32, (tm, SORT_ROWS), 1) == dest, 1.0, 0.0).astype(BF16)
        moe = jnp.dot(perm_t, ys_ref[sub * SORT_ROWS:(sub + 1) * SORT_ROWS, :], preferred_element_type=F32)
        o_ref[rows, :] = _rms(h_ref[rows, :].astype(F32) + moe, gain_ref[...])


def _combine(ys, h, dest, gain):
    t, d = h.shape
    per_step = ROUTE_TILES_PER_STEP
    tm = ROUTE_TM * per_step
    return pl.pallas_call(
        _combine_kernel,
        grid=(t // tm,),
        in_specs=[pl.BlockSpec((per_step * SORT_ROWS, d), lambda i: (i, 0)),
                  pl.BlockSpec((tm, d), lambda i: (i, 0)),
                  pl.BlockSpec((tm, LANES), lambda i: (i, 0)),
                  pl.BlockSpec((1, d), lambda i: (0, 0))],
        out_specs=pl.BlockSpec((tm, d), lambda i: (i, 0)),
        out_shape=jax.ShapeDtypeStruct((t, d), F32),
        compiler_params=pltpu.CompilerParams(
            dimension_semantics=("arbitrary",), vmem_limit_bytes=VMEM_LIMIT),
        name="combine",
    )(ys, h, dest, gain)


def _rotary_tables(s):
    half = HEAD_DIM // 2
    inv = ROPE_BASE ** (-jnp.arange(half, dtype=F32) / half)
    ang = jnp.arange(s, dtype=F32)[:, None] * inv[None, :]
    cos, sin = jnp.cos(ang), jnp.sin(ang)
    cos_t = jnp.tile(jnp.concatenate([cos, cos], axis=-1), (1, LANES // HEAD_DIM))
    sin_t = jnp.tile(jnp.concatenate([-sin, sin], axis=-1), (1, LANES // HEAD_DIM))
    return cos_t, sin_t


def kernel(x, w_in, w_out, norm_mix, norm_ffn, norm_final, attn_out_gain, rel_bias, ret_decay_fwd, ret_decay_bwd, router_group_w, router_group_b, router_expert_w, router_expert_b, expert_w1, expert_w3, expert_w2):
    b, s, d = x.shape
    depth = w_in.shape[0]
    cos_t, sin_t = _rotary_tables(s)
    bias_rows = _attn_bias_rows(rel_bias)
    h = x
    for layer in range(depth):
        (aq, ak, av, rq, rk, rv, rg), qkv_grouped = _inproj(
            h, norm_mix[layer][None], w_in[layer].astype(BF16), cos_t, sin_t)
        attn, ret, (w1, w3, w2, w_o) = _mixers(
            (aq, ak, av), qkv_grouped, bias_rows, (rq, rk, rv, rg),
            _retention_tables(ret_decay_fwd[layer], ret_decay_bwd[layer]),
            (expert_w1[layer], expert_w3[layer], expert_w2[layer], w_out[layer]))

        rw = jnp.concatenate(
            [jnp.transpose(router_expert_w[layer], (1, 0, 2)).reshape(d, N_EXPERTS),
             router_group_w[layer],
             jnp.zeros((d, LANES - N_EXPERTS - N_GROUPS), F32)], axis=1).astype(BF16)
        rb = jnp.concatenate(
            [router_expert_b[layer].reshape(N_EXPERTS), router_group_b[layer],
             jnp.zeros((LANES - N_EXPERTS - N_GROUPS,), F32)])[None].astype(F32)
        h1, xs, dest, cnt = _outproj(
            attn.reshape(b * s, ATTN_WIDTH), ret.reshape(b * s, RET_WIDTH), h.reshape(b * s, d),
            w_o, attn_out_gain[layer][None], norm_ffn[layer][None],
            rw, rb)
        n_route_tiles = (b * s) // ROUTE_TM
        n_moe_tiles = (b * s + n_route_tiles * N_GROUPS * (CHUNK - 1)) // MOE_TM + N_GROUPS
        schedule = _dispatch_tables(cnt, n_moe_tiles)
        ys = _moe(xs, w1, w3, w2, *schedule)
        assert depth == 1, "the combine kernel fuses the final norm, so it must run on the last layer"
        h = _combine(ys, h1, dest, norm_final[None]).reshape(b, s, d)
    return h
```

```python
import math

import jax
import jax.numpy as jnp
from jax import lax
from jax.experimental import pallas as pl
from jax.experimental.pallas import tpu as pltpu

F32 = jnp.float32
BF16 = jnp.bfloat16

D_MODEL = 1024
HEAD_DIM = 64
ATTN_WIDTH = 512
RET_WIDTH = 512
N_HEADS = 8
PAIR = 2 * HEAD_DIM
N_PAIRS = N_HEADS // 2
ATTN_DILATIONS = (1, 4, 16)
ATTN_RADIUS = 64
N_BUCKETS = 32
REL_MAX_DIST = 1024
ROPE_BASE = 10000.0
N_GROUPS = 4
EXPERTS_PER_GROUP = 4
N_EXPERTS = 16
EXPERT_FF = 512
EPS = 1e-6
NEG_INF = -1e30
LOG2_E = math.log2(math.e)

LANES = 128
ATTN_TQ = 128
ATTN_W = 256
N_BIAS_VARIANTS = 8
RET_CHUNK = 256
ROUTE_TM = 512
ROUTE_TILES_PER_STEP = 2
SORT_ROWS = 592
CHUNK = 16
MOE_TM = 1024
XS_W = D_MODEL + 2 * LANES
VMEM_LIMIT = 48 * 1024 * 1024


def _rms(x, gain):
    return x * lax.rsqrt(jnp.mean(x * x, axis=-1, keepdims=True) + EPS) * gain


def _inproj_kernel(x_ref, gain_ref, w_ref, cos_ref, sin_ref,
                   aq_ref, ak_ref, av_ref, rq_ref, rk_ref, rv_ref, rg_ref,
                   aq4_ref, ak4_ref, av4_ref, aq16_ref, ak16_ref, av16_ref, stage_ref, stage4_ref):
    tm = x_ref.shape[0]
    xn = _rms(x_ref[...], gain_ref[...]).astype(BF16)

    def stage(i, t, nat_ref):
        nat_ref[...] = t.astype(BF16)
        for hp in range(N_PAIRS):
            stage_ref[i, hp] = t[:, hp * PAIR:(hp + 1) * PAIR]

    def regroup4(i, d4_ref):
        for hp in range(N_PAIRS):
            for r4 in range(4):
                g4 = stage_ref[i, hp, pl.ds(r4, tm // 4, stride=4), :]
                d4_ref[hp, r4] = g4.astype(BF16)
                stage4_ref[i, hp, r4] = g4

    def regroup16(i, d16_ref):
        for hp in range(N_PAIRS):
            for r4 in range(4):
                for j in range(4):
                    d16_ref[hp, r4 + 4 * j] = (
                        stage4_ref[i, hp, r4, pl.ds(j, tm // 16, stride=4), :].astype(BF16))

    def seg(i):
        return jnp.dot(xn, w_ref[:, i * ATTN_WIDTH:(i + 1) * ATTN_WIDTH], preferred_element_type=F32)

    def rotary(t):
        cos, sin = cos_ref[...], sin_ref[...]
        first_half = (lax.broadcasted_iota(jnp.int32, (1, LANES), 1) % HEAD_DIM) < HEAD_DIM // 2
        outs = []
        for j in range(t.shape[1] // LANES):
            tj = t[:, j * LANES:(j + 1) * LANES]
            partner = jnp.where(first_half, pltpu.roll(tj, LANES - HEAD_DIM // 2, 1),
                                pltpu.roll(tj, HEAD_DIM // 2, 1))
            outs.append(tj * cos + partner * sin)
        return jnp.concatenate(outs, axis=1)

    stage(0, seg(0) * (HEAD_DIM ** -0.5 * LOG2_E), aq_ref)
    stage(1, seg(1), ak_ref)
    stage(2, seg(2), av_ref)
    rq_ref[...] = rotary(seg(3)).astype(BF16)
    for i, d4_ref in enumerate((aq4_ref, ak4_ref, av4_ref)):
        regroup4(i, d4_ref)
    rk_ref[...] = (rotary(seg(4)) * (HEAD_DIM ** -0.5)).astype(BF16)
    rv_ref[...] = seg(5).astype(BF16)
    for i, d16_ref in enumerate((aq16_ref, ak16_ref, av16_ref)):
        regroup16(i, d16_ref)
    rg_ref[...] = seg(6).astype(BF16)


def _inproj(x, gain, w_in, cos_t, sin_t, tm=512):
    b, s, d = x.shape
    n = w_in.shape[1]
    out = jax.ShapeDtypeStruct((b, s, ATTN_WIDTH), BF16)
    ospec = pl.BlockSpec((None, tm, ATTN_WIDTH), lambda si, bi: (bi, si, 0))

    def grouped(dil):
        shape = jax.ShapeDtypeStruct((b, s // tm, N_PAIRS, dil, tm // dil, PAIR), BF16)
        spec = pl.BlockSpec((None, None, N_PAIRS, dil, tm // dil, PAIR), lambda si, bi: (bi, si, 0, 0, 0, 0))
        return [shape] * 3, [spec] * 3

    shapes4, specs4 = grouped(4)
    shapes16, specs16 = grouped(16)
    outs = pl.pallas_call(
        _inproj_kernel,
        grid=(s // tm, b),
        in_specs=[
            pl.BlockSpec((None, tm, d), lambda si, bi: (bi, si, 0)),
            pl.BlockSpec((1, d), lambda si, bi: (0, 0)),
            pl.BlockSpec((d, n), lambda si, bi: (0, 0)),
            pl.BlockSpec((tm, LANES), lambda si, bi: (si, 0)),
            pl.BlockSpec((tm, LANES), lambda si, bi: (si, 0)),
        ],
        out_specs=[ospec] * 7 + specs4 + specs16,
        out_shape=[out] * 7 + shapes4 + shapes16,
        scratch_shapes=[pltpu.VMEM((3, N_PAIRS, tm, PAIR), F32),
                        pltpu.VMEM((3, N_PAIRS, 4, tm // 4, PAIR), F32)],
        compiler_params=pltpu.CompilerParams(
            dimension_semantics=("arbitrary", "arbitrary"), vmem_limit_bytes=VMEM_LIMIT),
        name="inproj",
    )(x, gain, w_in, cos_t, sin_t)
    return outs[:7], outs[7:]


def _t5_bucket(rel):
    half = N_BUCKETS // 2
    max_exact = half // 2
    offset = jnp.where(rel > 0, half, 0)
    n = jnp.abs(rel)
    nf = jnp.maximum(n, 1).astype(F32)
    large = max_exact + (jnp.log(nf / max_exact) / math.log(REL_MAX_DIST / max_exact)
                         * (half - max_exact)).astype(jnp.int32)
    large = jnp.minimum(large, half - 1)
    return offset + jnp.where(n < max_exact, n, large)


def _attn_bias_rows(rel_bias):
    period = 2 * ATTN_W
    band = 2 * ATTN_RADIUS + 1
    rel = jnp.arange(-ATTN_RADIUS, ATTN_RADIUS + 1)
    rows = []
    for dil, offs in ((1, (0, 64, 128)), (4, (0, 64, 128)), (16, (0, 128))):
        vals = rel_bias[_t5_bucket(rel * dil)].astype(F32).T * LOG2_E
        for off in offs:
            lo = off - ATTN_RADIUS
            pad = jnp.full((N_HEADS, period - band), NEG_INF, F32)
            if lo >= 0:
                row = jnp.concatenate([pad[:, :lo], vals, pad[:, lo:]], axis=1)
            else:
                row = jnp.concatenate([vals[:, -lo:], pad, vals[:, :-lo]], axis=1)
            rows.append(row)
    v = jnp.stack(rows, axis=1)
    return v.reshape(N_PAIRS, 2 * N_BIAS_VARIANTS, period)


def _attention_kernel(q_ref, k_ref, v_ref, q4_ref, k4_ref, v4_ref, q16_ref, k16_ref, v16_ref,
                      rows_ref, o_ref, bias_ref, acc_ref, m_ref, l_ref, out_ref):
    s = q_ref.shape[0]
    n_tiles = s // ATTN_TQ
    lane = lax.broadcasted_iota(jnp.int32, (1, PAIR), 1)
    head0 = lane < HEAD_DIM

    @pl.when(pl.program_id(1) == 0)
    def _():
        col = lax.broadcasted_iota(jnp.int32, (ATTN_TQ, ATTN_W), 1)
        for idx in range(2 * N_BIAS_VARIANTS):
            gen = jnp.broadcast_to(rows_ref[idx:idx + 1, :], (ATTN_TQ, 2 * ATTN_W))
            tab = pltpu.roll(gen, 0, 1, stride=1, stride_axis=0)[:, :ATTN_W]
            var = idx % N_BIAS_VARIANTS
            if var >= 6:
                tab = jnp.where((col // ATTN_TQ) == var - 6, tab, NEG_INF)
            head = idx // N_BIAS_VARIANTS
            bias_ref[var, head * ATTN_TQ:(head + 1) * ATTN_TQ, :] = tab

    def rows_of(ref, dil, lo, n):
        if dil == 1:
            return ref[lo:lo + n, :]
        piece, sub_len = ref.shape[2], s // dil
        parts, pos = [], lo
        while pos < lo + n:
            within = pos % sub_len
            off = within % piece
            take = min(piece - off, lo + n - pos)
            parts.append(ref[within // piece, pos // sub_len, off:off + take, :])
            pos += take
        return parts[0] if len(parts) == 1 else jnp.concatenate(parts, axis=0)

    def run_branch(bi, dil, qs_ref, ks_ref, vs_ref):
        sub_len = s // dil
        tiles_per_sub = sub_len // ATTN_TQ

        def tile(t):
            q0 = t * ATTN_TQ
            if tiles_per_sub == 1:
                ws = (t // 2) * ATTN_W
                var = 6 + t % 2
            else:
                pos = t % tiles_per_sub
                sub_lo = (t // tiles_per_sub) * sub_len
                ws = min(max(q0 - ATTN_RADIUS, sub_lo), sub_lo + sub_len - ATTN_W)
                var = (0 if pos == 0 else 2 if pos == tiles_per_sub - 1 else 1) + 3 * bi
            q = rows_of(qs_ref, dil, q0, ATTN_TQ)
            k = rows_of(ks_ref, dil, ws, ATTN_W)
            v = rows_of(vs_ref, dil, ws, ATTN_W)
            q2 = jnp.concatenate([jnp.where(head0, q, jnp.zeros_like(q)),
                                  jnp.where(head0, jnp.zeros_like(q), q)], axis=0)
            sc = lax.dot_general(q2, k, (((1,), (1,)), ((), ())), preferred_element_type=F32)
            sc = sc + bias_ref[var]
            m = jnp.max(sc, axis=-1, keepdims=True)
            p = jnp.exp2(sc - m).astype(BF16)
            o = jnp.dot(p, jnp.concatenate([v, jnp.ones_like(v)], axis=1), preferred_element_type=F32)
            l = o[:, PAIR:]
            outs, ms, ls = (o[:ATTN_TQ, :PAIR], o[ATTN_TQ:, :PAIR]), (m[:ATTN_TQ], m[ATTN_TQ:]), (l[:ATTN_TQ], l[ATTN_TQ:])
            if dil == 1:
                dst = pl.ds(q0, ATTN_TQ)
            elif dil == 4:
                dst = pl.ds((t % 4) * (4 * ATTN_TQ) + t // 4, ATTN_TQ, stride=4)
            else:
                dst = pl.ds((t % 4) * (s // 4) + t // 4, ATTN_TQ, stride=4)
            acc_ref[bi, dst, :] = jnp.where(head0, outs[0], outs[1])
            m_ref[bi, dst, :] = jnp.where(head0, ms[0], ms[1])
            l_ref[bi, dst, :] = jnp.where(head0, ls[0], ls[1])

        for t in range(n_tiles):
            tile(t)

    run_branch(0, 1, q_ref, k_ref, v_ref)
    run_branch(1, 4, q4_ref, k4_ref, v4_ref)
    run_branch(2, 16, q16_ref, k16_ref, v16_ref)

    rows = ATTN_TQ
    for r4 in range(4):
        for blk in range(s // (4 * rows)):
            nat = pl.ds(r4 + 4 * rows * blk, rows, stride=4)
            sl = (nat, nat, pl.ds(r4 * (s // 4) + rows * blk, rows))
            m = [m_ref[bi, sl[bi], :] for bi in range(3)]
            mx = jnp.maximum(jnp.maximum(m[0], m[1]), m[2])
            num = jnp.zeros((rows, PAIR), F32)
            den = jnp.zeros((rows, PAIR), F32)
            for bi in range(3):
                e = jnp.exp2(m[bi] - mx)
                num = num + e * acc_ref[bi, sl[bi], :]
                den = den + e * l_ref[bi, sl[bi], :]
            out_ref[nat, :] = num / den
    o_ref[...] = out_ref[...].astype(o_ref.dtype)


def _retention_tables(decay_fwd, decay_bwd):
    c = RET_CHUNK
    lg_f = -jnp.exp(decay_fwd.astype(F32))
    lg_b = -jnp.exp(decay_bwd.astype(F32))
    idx = jnp.arange(c, dtype=F32)
    rel = idx[:, None] - idx[None, :]
    dmat = jnp.where(rel >= 0,
                     jnp.exp(lg_f[:, None, None] * jnp.maximum(rel, 0.0)[None]),
                     jnp.exp(lg_b[:, None, None] * jnp.maximum(-rel, 0.0)[None]))
    dmat = dmat.reshape(N_PAIRS, 2 * c, c)

    def lanes(v):
        v = v.reshape(N_PAIRS, 2, -1)
        return jnp.repeat(jnp.transpose(v, (0, 2, 1)), HEAD_DIM, axis=2)

    vec = jnp.stack([
        lanes(jnp.exp(lg_f[:, None] * (idx + 1.0)[None])),
        lanes(jnp.exp(lg_f[:, None] * (c - 1.0 - idx)[None])),
        lanes(jnp.exp(lg_b[:, None] * (c - idx)[None])),
        lanes(jnp.exp(lg_b[:, None] * idx[None])),
    ], axis=1)
    same_head = (jnp.arange(PAIR)[:, None] // HEAD_DIM) == (jnp.arange(PAIR)[None, :] // HEAD_DIM)
    cd = jnp.stack([lanes(jnp.exp(lg_f * c)[:, None]), lanes(jnp.exp(lg_b * c)[:, None])], axis=1)
    cd = jnp.transpose(cd, (0, 1, 3, 2)) * same_head[None, None].astype(F32)
    return dmat, vec, cd


def _retention_kernel(q_ref, k_ref, v_ref, g_ref, dmat_ref, vec_ref, cd_ref, o_ref, kv_ref, st_ref):
    s = q_ref.shape[0]
    c = RET_CHUNK
    nc = s // c
    lane = lax.broadcasted_iota(jnp.int32, (1, PAIR), 1)
    head0 = lane < HEAD_DIM
    same_head = ((lax.broadcasted_iota(jnp.int32, (PAIR, PAIR), 0) // HEAD_DIM)
                 == (lax.broadcasted_iota(jnp.int32, (PAIR, PAIR), 1) // HEAD_DIM))

    same_head2 = jnp.concatenate([same_head, same_head], axis=0)
    for n in range(nc):
        rows = slice(n * c, (n + 1) * c)
        kf = k_ref[rows, :].astype(F32)
        kcat = jnp.concatenate([(kf * vec_ref[1]).astype(BF16), (kf * vec_ref[3]).astype(BF16)], axis=1)
        kv = lax.dot_general(kcat, v_ref[rows, :], (((0,), (0,)), ((), ())), preferred_element_type=F32)
        kv_ref[n] = jnp.where(same_head2, kv, 0.0)

    state = jnp.zeros((PAIR, PAIR), F32)
    for n in range(nc):
        st_ref[n, :PAIR, :] = state.astype(BF16)
        state = state * cd_ref[0] + kv_ref[n, :PAIR, :]
    state = jnp.zeros((PAIR, PAIR), F32)
    for n in reversed(range(nc)):
        st_ref[n, PAIR:, :] = state.astype(BF16)
        state = state * cd_ref[1] + kv_ref[n, PAIR:, :]

    for n in range(nc):
        rows = slice(n * c, (n + 1) * c)
        q, k, v = q_ref[rows, :], k_ref[rows, :], v_ref[rows, :]
        q2 = jnp.concatenate([jnp.where(head0, q, jnp.zeros_like(q)),
                              jnp.where(head0, jnp.zeros_like(q), q)], axis=0)
        sc = lax.dot_general(q2, k, (((1,), (1,)), ((), ())), preferred_element_type=F32)
        intra = jnp.dot((sc * dmat_ref[...]).astype(BF16), v, preferred_element_type=F32)
        qf = q.astype(F32)
        qcat = jnp.concatenate([(qf * vec_ref[0]).astype(BF16), (qf * vec_ref[2]).astype(BF16)], axis=1)
        y = jnp.where(head0, intra[:c], intra[c:]) + jnp.dot(qcat, st_ref[n], preferred_element_type=F32)
        y2 = y * y
        ms0 = jnp.sum(jnp.where(head0, y2, 0.0), axis=-1, keepdims=True)
        ms1 = jnp.sum(jnp.where(head0, 0.0, y2), axis=-1, keepdims=True)
        ms = jnp.where(head0, ms0, ms1) * (1.0 / HEAD_DIM)
        g = g_ref[rows, :].astype(F32)
        o_ref[rows, :] = (y * lax.rsqrt(ms + EPS) * (g * jax.nn.sigmoid(g))).astype(o_ref.dtype)


def _mixers(qkv, qkv_grouped, bias_rows, ret_inputs, ret_tables, weights_f32):
    b, s, _ = qkv[0].shape
    c = RET_CHUNK
    n_steps = N_PAIRS * b
    spec = pl.BlockSpec((None, s, PAIR), lambda hp, bi: (bi, 0, hp))
    gspecs = [pl.BlockSpec((None, g.shape[1], None) + g.shape[3:], lambda hp, bi: (bi, 0, hp, 0, 0, 0))
              for g in qkv_grouped]
    slabs = [w.reshape(n_steps, -1, w.shape[-1]) for w in weights_f32]
    slab_specs = [pl.BlockSpec((None,) + w.shape[1:], lambda hp, bi: (hp * b + bi, 0, 0)) for w in slabs]
    n_attn_in, n_ret_in, n_w = 10, 7, len(slabs)
    attn_scratch = [
        pltpu.VMEM((N_BIAS_VARIANTS, 2 * ATTN_TQ, ATTN_W), F32),
        pltpu.VMEM((3, s, PAIR), F32),
        pltpu.VMEM((3, s, PAIR), F32),
        pltpu.VMEM((3, s, PAIR), F32),
        pltpu.VMEM((s, PAIR), F32),
    ]
    ret_scratch = [pltpu.VMEM((s // c, 2 * PAIR, PAIR), F32),
                   pltpu.VMEM((s // c, 2 * PAIR, PAIR), BF16)]

    def kernel(*refs):
        ins, rest = refs[:n_attn_in + n_ret_in + n_w], refs[n_attn_in + n_ret_in + n_w:]
        attn_o, ret_o, w_out, scratch = rest[0], rest[1], rest[2:2 + n_w], rest[2 + n_w:]
        _attention_kernel(*ins[:n_attn_in], attn_o, *scratch[:len(attn_scratch)])
        _retention_kernel(*ins[n_attn_in:n_attn_in + n_ret_in], ret_o, *scratch[len(attn_scratch):])
        for src, dst in zip(ins[n_attn_in + n_ret_in:], w_out):
            dst[...] = src[...].astype(BF16)

    outs = pl.pallas_call(
        kernel,
        grid=(N_PAIRS, b),
        in_specs=[spec] * 3 + gspecs
        + [pl.BlockSpec((None, 2 * N_BIAS_VARIANTS, 2 * ATTN_W), lambda hp, bi: (hp, 0, 0))]
        + [spec] * 4
        + [pl.BlockSpec((None, 2 * c, c), lambda hp, bi: (hp, 0, 0)),
           pl.BlockSpec((None, 4, c, PAIR), lambda hp, bi: (hp, 0, 0, 0)),
           pl.BlockSpec((None, 2, PAIR, PAIR), lambda hp, bi: (hp, 0, 0, 0))]
        + slab_specs,
        out_specs=[spec, spec] + slab_specs,
        out_shape=[jax.ShapeDtypeStruct((b, s, ATTN_WIDTH), BF16), jax.ShapeDtypeStruct((b, s, RET_WIDTH), BF16)]
        + [jax.ShapeDtypeStruct(w.shape, BF16) for w in slabs],
        scratch_shapes=attn_scratch + ret_scratch,
        compiler_params=pltpu.CompilerParams(
            dimension_semantics=("arbitrary", "arbitrary"), vmem_limit_bytes=VMEM_LIMIT),
        name="mixers",
    )(*qkv, *qkv_grouped, bias_rows, *ret_inputs, *ret_tables, *slabs)
    return outs[0], outs[1], [o.reshape(w.shape) for o, w in zip(outs[2:], weights_f32)]


def _route_tile(sub, attn_ref, ret_ref, x_ref, wo_ref, again_ref, fgain_ref,
                rw_ref, rb_ref, h_ref, xs_ref, dest_ref, cnt_ref):
    tm = ROUTE_TM
    rows = slice(sub * tm, (sub + 1) * tm)
    a = _rms(attn_ref[rows, :].astype(F32), again_ref[...]).astype(BF16)
    mixed = jnp.concatenate([a, ret_ref[rows, :]], axis=1)
    h = x_ref[rows, :] + jnp.dot(mixed, wo_ref[...], preferred_element_type=F32)
    h_ref[rows, :] = h.astype(h_ref.dtype)
    hn = _rms(h, fgain_ref[...]).astype(BF16)

    logits = jnp.dot(hn, rw_ref[...], preferred_element_type=F32) + rb_ref[...]
    lane = lax.broadcasted_iota(jnp.int32, logits.shape, 1)
    big = jnp.int32(LANES)

    def first_argmax(vals):
        top = jnp.max(vals, axis=-1, keepdims=True)
        return top, jnp.min(jnp.where(vals == top, lane, big), axis=-1, keepdims=True)

    gl = jnp.where((lane >= N_EXPERTS) & (lane < N_EXPERTS + N_GROUPS), logits, -jnp.inf)
    gmax, gidx = first_argmax(gl)
    p_group = 1.0 / jnp.sum(jnp.exp(gl - gmax), axis=-1, keepdims=True)
    grp = gidx - N_EXPERTS
    lo = grp * EXPERTS_PER_GROUP
    el = jnp.where((lane >= lo) & (lane < lo + EXPERTS_PER_GROUP), logits, -jnp.inf)
    v1, i1 = first_argmax(el)
    v2, i2 = first_argmax(jnp.where(lane == i1, -jnp.inf, el))
    e2 = jnp.exp(v2 - v1)
    p1 = p_group / (1.0 + e2)
    gates = jnp.where(lane == i1, p1, jnp.where(lane == i2, p1 * e2, 0.0))

    onehot = jnp.where(lane == grp, 1.0, 0.0)
    r_i = lax.broadcasted_iota(jnp.int32, (tm, tm), 0)
    c_i = lax.broadcasted_iota(jnp.int32, (tm, tm), 1)
    before = jnp.where(c_i < r_i, 1.0, 0.0).astype(BF16)
    rank = jnp.dot(before, onehot.astype(BF16), preferred_element_type=F32)
    cnt = jnp.sum(onehot, axis=0, keepdims=True)
    c16 = jnp.broadcast_to(jnp.ceil(cnt * (1.0 / CHUNK)) * CHUNK, (8, LANES))
    lane8 = lax.broadcasted_iota(jnp.int32, (8, LANES), 1)
    start = jnp.zeros((8, LANES), F32)
    for sft in range(1, N_GROUPS):
        start = start + jnp.where(lane8 >= sft, pltpu.roll(c16, sft, 1), 0.0)
    dest = jnp.sum(onehot * (start[0:1] + rank), axis=-1, keepdims=True)
    dest_ref[rows, :] = jnp.broadcast_to(dest, (tm, LANES))
    cnt_ref[sub] = jnp.where(lane8 < N_GROUPS, c16, pltpu.roll(start, N_GROUPS, 1)).astype(jnp.int32)

    onehot_t = onehot.T
    rank_t = lax.dot_general(onehot_t.astype(BF16), before, (((1,), (1,)), ((), ())),
                             preferred_element_type=F32)
    g_row = lax.broadcasted_iota(jnp.int32, (LANES, LANES), 0)
    g_col = lax.broadcasted_iota(jnp.int32, (LANES, LANES), 1)
    start_t = jnp.sum(jnp.where(g_col < g_row, c16[0:1], 0.0), axis=-1, keepdims=True)
    dest_t = jnp.sum(onehot_t * (start_t + rank_t), axis=0, keepdims=True)
    perm = jnp.where(lax.broadcasted_iota(jnp.int32, (SORT_ROWS, tm), 0) == dest_t.astype(jnp.int32),
                     1.0, 0.0).astype(BF16)
    g_hi = gates.astype(BF16)
    g_lo = (gates - g_hi.astype(F32)).astype(BF16)
    payload = jnp.concatenate([hn, g_hi, g_lo], axis=1)
    xs_ref[sub * SORT_ROWS:(sub + 1) * SORT_ROWS, :] = (
        jnp.dot(perm, payload, preferred_element_type=F32).astype(BF16))


def _outproj_kernel(*refs):
    for sub in range(ROUTE_TILES_PER_STEP):
        _route_tile(sub, *refs)


def _outproj(attn, ret, x, w_o, again, fgain, rw, rb):
    t, d = x.shape
    per_step = ROUTE_TILES_PER_STEP
    tm = ROUTE_TM * per_step
    nt = t // ROUTE_TM
    row = lambda w: pl.BlockSpec((tm, w), lambda i: (i, 0))
    full = lambda r, c: pl.BlockSpec((r, c), lambda i: (0, 0))
    return pl.pallas_call(
        _outproj_kernel,
        grid=(nt // per_step,),
        in_specs=[row(ATTN_WIDTH), row(RET_WIDTH), row(d), full(ATTN_WIDTH + RET_WIDTH, d),
                  full(1, ATTN_WIDTH), full(1, d), full(d, LANES), full(1, LANES)],
        out_specs=[row(d), pl.BlockSpec((per_step * SORT_ROWS, XS_W), lambda i: (i, 0)), row(LANES),
                   pl.BlockSpec((per_step, 8, LANES), lambda i: (i, 0, 0))],
        out_shape=[jax.ShapeDtypeStruct((t, d), BF16),
                   jax.ShapeDtypeStruct((nt * SORT_ROWS, XS_W), BF16),
                   jax.ShapeDtypeStruct((t, LANES), F32),
                   jax.ShapeDtypeStruct((nt, 8, LANES), jnp.int32)],
        compiler_params=pltpu.CompilerParams(
            dimension_semantics=("arbitrary",), vmem_limit_bytes=VMEM_LIMIT),
        name="outproj",
    )(attn, ret, x, w_o, again, fgain, rw, rb)


def _dispatch_tables(cnt, n_moe_tiles):
    nt = cnt.shape[0]
    cpt = MOE_TM // CHUNK

    n_slots = n_moe_tiles * cpt
    slot_rows = -(-n_slots // (8 * LANES)) * 8

    def schedule_kernel(seg_ref, grp_ref, used_ref, src_ref, tail_ref):
        slot = (lax.broadcasted_iota(jnp.int32, (slot_rows, LANES), 0) * LANES
                + lax.broadcasted_iota(jnp.int32, (slot_rows, LANES), 1))
        src = jnp.full((slot_rows, LANES), -1, jnp.int32)
        pos = jnp.int32(0)
        for g in range(N_GROUPS):
            def tile_body(i, carry):
                p, src = carry
                n = seg_ref[i, g] // CHUNK
                first = (i * SORT_ROWS + seg_ref[i, N_GROUPS + g]) // CHUNK
                src = jnp.where((slot >= p) & (slot < p + n), slot + (first - p), src)
                return p + n, src
            end, src = lax.fori_loop(0, nt, tile_body, (pos, src))
            padded = ((end + cpt - 1) // cpt) * cpt

            def mark_body(m, carry):
                grp_ref[m] = g
                used_ref[m] = 1
                return carry
            lax.fori_loop(pos // cpt, padded // cpt, mark_body, 0)
            pos = padded

        def idle_body(m, carry):
            grp_ref[m] = N_GROUPS - 1
            used_ref[m] = 0
            return carry
        lax.fori_loop(pos // cpt, n_moe_tiles, idle_body, 0)
        src_ref[...] = src

        def tail_body(i, carry):
            rows = seg_ref[i, 0]
            for g in range(1, N_GROUPS):
                rows = rows + seg_ref[i, g]
            tail_ref[i] = rows // CHUNK
            return carry
        lax.fori_loop(0, nt, tail_body, 0)

    smem = lambda: pl.BlockSpec(memory_space=pltpu.SMEM)
    tile_group, tile_used, src, tile_tail = pl.pallas_call(
        schedule_kernel,
        in_specs=[smem()],
        out_specs=[smem(), smem(), pl.BlockSpec(memory_space=pltpu.VMEM), smem()],
        out_shape=[jax.ShapeDtypeStruct((n_moe_tiles,), jnp.int32),
                   jax.ShapeDtypeStruct((n_moe_tiles,), jnp.int32),
                   jax.ShapeDtypeStruct((slot_rows, LANES), jnp.int32),
                   jax.ShapeDtypeStruct((nt,), jnp.int32)],
        name="schedule",
    )(cnt[:, 0, :2 * N_GROUPS])
    return tile_group, tile_used, src.reshape(-1)[:n_slots], tile_tail


def _moe_kernel(grp_ref, used_ref, src_ref, tail_ref, xs_hbm, w1_ref, w3_ref, w2_ref, ys_hbm,
                xbuf, obuf, zbuf, in_sem, out_sem, zero_sem):
    m = pl.program_id(0)
    n_tiles = pl.num_programs(0)
    cpt = MOE_TM // CHUNK
    cps = SORT_ROWS // CHUNK
    slot = m % 2
    zero_chunk = cps - 1

    def rows(c):
        return pl.ds(pl.multiple_of(c * CHUNK, CHUNK), CHUNK)

    def zero_tails(wait):
        def tile_body(i, carry):
            def body(c, carry2):
                cp = pltpu.make_async_copy(zbuf, ys_hbm.at[rows(i * cps + c), :], zero_sem)
                cp.wait() if wait else cp.start()
                return carry2
            lax.fori_loop(tail_ref[i], cps, body, 0)
            return carry
        lax.fori_loop(0, tail_ref.shape[0], tile_body, 0)

    def gather_start(tile, sl):
        for c in range(cpt):
            src = src_ref[tile * cpt + c]
            src = jnp.where(src < 0, zero_chunk, src)
            pltpu.make_async_copy(xs_hbm.at[rows(src), :], xbuf.at[sl, c * CHUNK:(c + 1) * CHUNK, :],
                                  in_sem.at[sl]).start()

    def gather_wait(sl):
        pltpu.make_async_copy(xs_hbm.at[0:MOE_TM, :], xbuf.at[sl], in_sem.at[sl]).wait()

    def scatter(tile, sl, wait):
        full = src_ref[tile * cpt + cpt - 1] >= 0

        def chunk_copy(c, src):
            return pltpu.make_async_copy(obuf.at[sl, c * CHUNK:(c + 1) * CHUNK, :], ys_hbm.at[rows(src), :],
                                         out_sem.at[sl])

        @pl.when(full)
        def _():
            if wait:
                pltpu.make_async_copy(obuf.at[sl], ys_hbm.at[0:MOE_TM, :], out_sem.at[sl]).wait()
            else:
                for c in range(cpt):
                    chunk_copy(c, src_ref[tile * cpt + c]).start()

        @pl.when(jnp.logical_not(full))
        def _():
            for c in range(cpt):
                src = src_ref[tile * cpt + c]

                @pl.when(src >= 0)
                def _():
                    cp = chunk_copy(c, src)
                    cp.wait() if wait else cp.start()

    @pl.when(m == 0)
    def _():
        gather_start(0, 0)
        zbuf[...] = jnp.zeros(zbuf.shape, zbuf.dtype)
        zero_tails(False)

    prev_used = used_ref[jnp.maximum(m - 1, 0)] > 0

    @pl.when((m == 0) | prev_used)
    def _():
        gather_wait(slot)

    @pl.when((m >= 2) & (used_ref[jnp.maximum(m - 2, 0)] > 0))
    def _():
        scatter(m - 2, slot, True)

    def expert_tile(n_rows):
        gather_start(jnp.minimum(m + 1, n_tiles - 1), 1 - slot)
        x = xbuf[slot, :n_rows, :D_MODEL]
        gate = (xbuf[slot, :n_rows, D_MODEL:D_MODEL + LANES].astype(F32)
                + xbuf[slot, :n_rows, D_MODEL + LANES:].astype(F32))
        lane = lax.broadcasted_iota(jnp.int32, gate.shape, 1)
        base = grp_ref[m] * EXPERTS_PER_GROUP
        acc = jnp.zeros((n_rows, D_MODEL), F32)
        for j in range(EXPERTS_PER_GROUP):
            a = jnp.dot(x, w1_ref[j], preferred_element_type=F32)
            b = jnp.dot(x, w3_ref[j], preferred_element_type=F32)
            gj = jnp.sum(jnp.where(lane == base + j, gate, 0.0), axis=-1, keepdims=True)
            hid = (a * jax.nn.sigmoid(a) * b * gj).astype(BF16)
            acc = acc + jnp.dot(hid, w2_ref[j], preferred_element_type=F32)
        obuf[slot, :n_rows, :] = acc.astype(BF16)
        scatter(m, slot, False)

    half_empty = src_ref[m * cpt + cpt // 2] < 0

    @pl.when((used_ref[m] > 0) & jnp.logical_not(half_empty))
    def _():
        expert_tile(MOE_TM)

    @pl.when((used_ref[m] > 0) & half_empty)
    def _():
        expert_tile(MOE_TM // 2)

    @pl.when(m == n_tiles - 1)
    def _():
        zero_tails(True)

        @pl.when(used_ref[m] > 0)
        def _():
            gather_wait(1 - slot)
            scatter(m, slot, True)

        @pl.when((m >= 1) & prev_used)
        def _():
            scatter(m - 1, 1 - slot, True)


def _moe(xs, w1, w3, w2, tile_group, tile_used, src_chunk, tile_tail):
    n_moe_tiles = tile_group.shape[0]
    rows = xs.shape[0]
    d = D_MODEL
    wspec = lambda r, c: pl.BlockSpec((EXPERTS_PER_GROUP, r, c), lambda m, grp, *_: (grp[m], 0, 0))
    return pl.pallas_call(
        _moe_kernel,
        grid_spec=pltpu.PrefetchScalarGridSpec(
            num_scalar_prefetch=4,
            grid=(n_moe_tiles,),
            in_specs=[pl.BlockSpec(memory_space=pl.ANY),
                      wspec(d, EXPERT_FF), wspec(d, EXPERT_FF), wspec(EXPERT_FF, d)],
            out_specs=pl.BlockSpec(memory_space=pl.ANY),
            scratch_shapes=[pltpu.VMEM((2, MOE_TM, XS_W), BF16),
                            pltpu.VMEM((2, MOE_TM, d), BF16),
                            pltpu.VMEM((CHUNK, d), BF16),
                            pltpu.SemaphoreType.DMA((2,)),
                            pltpu.SemaphoreType.DMA((2,)),
                            pltpu.SemaphoreType.DMA(())]),
        out_shape=jax.ShapeDtypeStruct((rows, d), BF16),
        compiler_params=pltpu.CompilerParams(
            dimension_semantics=("arbitrary",), vmem_limit_bytes=VMEM_LIMIT),
        name="moe",
    )(tile_group, tile_used, src_chunk, tile_tail, xs, w1, w3, w2)


def _combine_kernel(ys_ref, h_ref, dest_ref, gain_ref, o_ref):
    tm = ROUTE_TM
    for sub in range(h_ref.shape[0] // tm):
        rows = slice(sub * tm, (sub + 1) * tm)
        dest = dest_ref[rows, 0:1].astype(jnp.int32)
        perm_t = jnp.where(lax.broadcasted_iota(jnp.int32, (tm, SORT_ROWS), 1) == dest, 1.0, 0.0).astype(BF16)
        moe = jnp.dot(perm_t, ys_ref[sub * SORT_ROWS:(sub + 1) * SORT_ROWS, :], preferred_element_type=F32)
        o_ref[rows, :] = _rms(h_ref[rows, :].astype(F32) + moe, gain_ref[...])


def _combine(ys, h, dest, gain):
    t, d = h.shape
    per_step = ROUTE_TILES_PER_STEP
    tm = ROUTE_TM * per_step
    return pl.pallas_call(
        _combine_kernel,
        grid=(t // tm,),
        in_specs=[pl.BlockSpec((per_step * SORT_ROWS, d), lambda i: (i, 0)),
                  pl.BlockSpec((tm, d), lambda i: (i, 0)),
                  pl.BlockSpec((tm, LANES), lambda i: (i, 0)),
                  pl.BlockSpec((1, d), lambda i: (0, 0))],
        out_specs=pl.BlockSpec((tm, d), lambda i: (i, 0)),
        out_shape=jax.ShapeDtypeStruct((t, d), F32),
        compiler_params=pltpu.CompilerParams(
            dimension_semantics=("arbitrary",), vmem_limit_bytes=VMEM_LIMIT),
        name="combine",
    )(ys, h, dest, gain)


def _rotary_tables(s):
    half = HEAD_DIM // 2
    inv = ROPE_BASE ** (-jnp.arange(half, dtype=F32) / half)
    ang = jnp.arange(s, dtype=F32)[:, None] * inv[None, :]
    cos, sin = jnp.cos(ang), jnp.sin(ang)
    cos_t = jnp.tile(jnp.concatenate([cos, cos], axis=-1), (1, LANES // HEAD_DIM))
    sin_t = jnp.tile(jnp.concatenate([-sin, sin], axis=-1), (1, LANES // HEAD_DIM))
    return cos_t, sin_t


def kernel(x, w_in, w_out, norm_mix, norm_ffn, norm_final, attn_out_gain, rel_bias, ret_decay_fwd, ret_decay_bwd, router_group_w, router_group_b, router_expert_w, router_expert_b, expert_w1, expert_w3, expert_w2):
    b, s, d = x.shape
    depth = w_in.shape[0]
    cos_t, sin_t = _rotary_tables(s)
    bias_rows = _attn_bias_rows(rel_bias)
    h = x
    for layer in range(depth):
        (aq, ak, av, rq, rk, rv, rg), qkv_grouped = _inproj(
            h, norm_mix[layer][None], w_in[layer].astype(BF16), cos_t, sin_t)
        attn, ret, (w1, w3, w2, w_o) = _mixers(
            (aq, ak, av), qkv_grouped, bias_rows, (rq, rk, rv, rg),
            _retention_tables(ret_decay_fwd[layer], ret_decay_bwd[layer]),
            (expert_w1[layer], expert_w3[layer], expert_w2[layer], w_out[layer]))

        rw = jnp.concatenate(
            [jnp.transpose(router_expert_w[layer], (1, 0, 2)).reshape(d, N_EXPERTS),
             router_group_w[layer],
             jnp.zeros((d, LANES - N_EXPERTS - N_GROUPS), F32)], axis=1).astype(BF16)
        rb = jnp.concatenate(
            [router_expert_b[layer].reshape(N_EXPERTS), router_group_b[layer],
             jnp.zeros((LANES - N_EXPERTS - N_GROUPS,), F32)])[None].astype(F32)
        h1, xs, dest, cnt = _outproj(
            attn.reshape(b * s, ATTN_WIDTH), ret.reshape(b * s, RET_WIDTH), h.reshape(b * s, d),
            w_o, attn_out_gain[layer][None], norm_ffn[layer][None],
            rw, rb)
        n_route_tiles = (b * s) // ROUTE_TM
        n_moe_tiles = (b * s + n_route_tiles * N_GROUPS * (CHUNK - 1)) // MOE_TM + N_GROUPS
        schedule = _dispatch_tables(cnt, n_moe_tiles)
        ys = _moe(xs, w1, w3, w2, *schedule)
        assert depth == 1, "the combine kernel fuses the final norm, so it must run on the last layer"
        h = _combine(ys, h1, dest, norm_final[None]).reshape(b, s, d)
    return h
```

```python
import math

import jax
import jax.numpy as jnp
from jax import lax
from jax.experimental import pallas as pl
from jax.experimental.pallas import tpu as pltpu

F32 = jnp.float32
BF16 = jnp.bfloat16

D_MODEL = 1024
HEAD_DIM = 64
ATTN_WIDTH = 512
RET_WIDTH = 512
N_HEADS = 8
PAIR = 2 * HEAD_DIM
N_PAIRS = N_HEADS // 2
ATTN_DILATIONS = (1, 4, 16)
ATTN_RADIUS = 64
N_BUCKETS = 32
REL_MAX_DIST = 1024
ROPE_BASE = 10000.0
N_GROUPS = 4
EXPERTS_PER_GROUP = 4
N_EXPERTS = 16
EXPERT_FF = 512
EPS = 1e-6
NEG_INF = -1e30
LOG2_E = math.log2(math.e)

LANES = 128
ATTN_TQ = 128
ATTN_W = 256
N_BIAS_VARIANTS = 8
RET_CHUNK = 256
ROUTE_TM = 512
ROUTE_TILES_PER_STEP = 2
COMBINE_TILES_PER_STEP = 4
SORT_ROWS = 592
CHUNK = 16
MOE_TM = 512
XS_W = D_MODEL + 2 * LANES
VMEM_LIMIT = 48 * 1024 * 1024


def _rms(x, gain):
    return x * lax.rsqrt(jnp.mean(x * x, axis=-1, keepdims=True) + EPS) * gain


def _inproj_kernel(x_ref, gain_ref, w_ref, cos_ref, sin_ref,
                   aq_ref, ak_ref, av_ref, rq_ref, rk_ref, rv_ref, rg_ref,
                   aq4_ref, ak4_ref, av4_ref, aq16_ref, ak16_ref, av16_ref, stage_ref, stage4_ref):
    tm = x_ref.shape[0]
    xn = _rms(x_ref[...], gain_ref[...]).astype(BF16)

    def stage(i, t, nat_ref):
        nat_ref[...] = t.astype(BF16)
        for hp in range(N_PAIRS):
            stage_ref[i, hp] = t[:, hp * PAIR:(hp + 1) * PAIR]

    def regroup4(i, d4_ref):
        for hp in range(N_PAIRS):
            for r4 in range(4):
                g4 = stage_ref[i, hp, pl.ds(r4, tm // 4, stride=4), :]
                d4_ref[hp, r4] = g4.astype(BF16)
                stage4_ref[i, hp, r4] = g4

    def regroup16(i, d16_ref):
        for hp in range(N_PAIRS):
            for r4 in range(4):
                for j in range(4):
                    d16_ref[hp, r4 + 4 * j] = (
                        stage4_ref[i, hp, r4, pl.ds(j, tm // 16, stride=4), :].astype(BF16))

    def seg(i):
        return jnp.dot(xn, w_ref[:, i * ATTN_WIDTH:(i + 1) * ATTN_WIDTH], preferred_element_type=F32)

    def rotary(t):
        cos, sin = cos_ref[...], sin_ref[...]
        first_half = (lax.broadcasted_iota(jnp.int32, (1, LANES), 1) % HEAD_DIM) < HEAD_DIM // 2
        outs = []
        for j in range(t.shape[1] // LANES):
            tj = t[:, j * LANES:(j + 1) * LANES]
            partner = jnp.where(first_half, pltpu.roll(tj, LANES - HEAD_DIM // 2, 1),
                                pltpu.roll(tj, HEAD_DIM // 2, 1))
            outs.append(tj * cos + partner * sin)
        return jnp.concatenate(outs, axis=1)

    stage(0, seg(0) * (HEAD_DIM ** -0.5 * LOG2_E), aq_ref)
    stage(1, seg(1), ak_ref)
    stage(2, seg(2), av_ref)
    rq_ref[...] = rotary(seg(3)).astype(BF16)
    for i, d4_ref in enumerate((aq4_ref, ak4_ref, av4_ref)):
        regroup4(i, d4_ref)
    rk_ref[...] = (rotary(seg(4)) * (HEAD_DIM ** -0.5)).astype(BF16)
    rv_ref[...] = seg(5).astype(BF16)
    for i, d16_ref in enumerate((aq16_ref, ak16_ref, av16_ref)):
        regroup16(i, d16_ref)
    rg_ref[...] = seg(6).astype(BF16)


def _inproj(x, gain, w_in, cos_t, sin_t, tm=512):
    b, s, d = x.shape
    n = w_in.shape[1]
    out = jax.ShapeDtypeStruct((b, s, ATTN_WIDTH), BF16)
    ospec = pl.BlockSpec((None, tm, ATTN_WIDTH), lambda si, bi: (bi, si, 0))

    def grouped(dil):
        shape = jax.ShapeDtypeStruct((b, s // tm, N_PAIRS, dil, tm // dil, PAIR), BF16)
        spec = pl.BlockSpec((None, None, N_PAIRS, dil, tm // dil, PAIR), lambda si, bi: (bi, si, 0, 0, 0, 0))
        return [shape] * 3, [spec] * 3

    shapes4, specs4 = grouped(4)
    shapes16, specs16 = grouped(16)
    outs = pl.pallas_call(
        _inproj_kernel,
        grid=(s // tm, b),
        in_specs=[
            pl.BlockSpec((None, tm, d), lambda si, bi: (bi, si, 0)),
            pl.BlockSpec((1, d), lambda si, bi: (0, 0)),
            pl.BlockSpec((d, n), lambda si, bi: (0, 0)),
            pl.BlockSpec((tm, LANES), lambda si, bi: (si, 0)),
            pl.BlockSpec((tm, LANES), lambda si, bi: (si, 0)),
        ],
        out_specs=[ospec] * 7 + specs4 + specs16,
        out_shape=[out] * 7 + shapes4 + shapes16,
        scratch_shapes=[pltpu.VMEM((3, N_PAIRS, tm, PAIR), F32),
                        pltpu.VMEM((3, N_PAIRS, 4, tm // 4, PAIR), F32)],
        compiler_params=pltpu.CompilerParams(
            dimension_semantics=("arbitrary", "arbitrary"), vmem_limit_bytes=VMEM_LIMIT),
        name="inproj",
    )(x, gain, w_in, cos_t, sin_t)
    return outs[:7], outs[7:]


def _t5_bucket(rel):
    half = N_BUCKETS // 2
    max_exact = half // 2
    offset = jnp.where(rel > 0, half, 0)
    n = jnp.abs(rel)
    nf = jnp.maximum(n, 1).astype(F32)
    large = max_exact + (jnp.log(nf / max_exact) / math.log(REL_MAX_DIST / max_exact)
                         * (half - max_exact)).astype(jnp.int32)
    large = jnp.minimum(large, half - 1)
    return offset + jnp.where(n < max_exact, n, large)


def _attn_bias_rows(rel_bias):
    period = 2 * ATTN_W
    band = 2 * ATTN_RADIUS + 1
    rel = jnp.arange(-ATTN_RADIUS, ATTN_RADIUS + 1)
    rows = []
    for dil, offs in ((1, (0, 64, 128)), (4, (0, 64, 128)), (16, (0, 128))):
        vals = rel_bias[_t5_bucket(rel * dil)].astype(F32).T * LOG2_E
        for off in offs:
            lo = off - ATTN_RADIUS
            pad = jnp.full((N_HEADS, period - band), NEG_INF, F32)
            if lo >= 0:
                row = jnp.concatenate([pad[:, :lo], vals, pad[:, lo:]], axis=1)
            else:
                row = jnp.concatenate([vals[:, -lo:], pad, vals[:, :-lo]], axis=1)
            rows.append(row)
    v = jnp.stack(rows, axis=1)
    return v.reshape(N_PAIRS, 2 * N_BIAS_VARIANTS, period)


def _attention_kernel(q_ref, k_ref, v_ref, q4_ref, k4_ref, v4_ref, q16_ref, k16_ref, v16_ref,
                      rows_ref, o_ref, bias_ref, acc_ref, m_ref, l_ref, out_ref):
    s = q_ref.shape[0]
    n_tiles = s // ATTN_TQ
    lane = lax.broadcasted_iota(jnp.int32, (1, PAIR), 1)
    head0 = lane < HEAD_DIM

    @pl.when(pl.program_id(1) == 0)
    def _():
        col = lax.broadcasted_iota(jnp.int32, (ATTN_TQ, ATTN_W), 1)
        for idx in range(2 * N_BIAS_VARIANTS):
            gen = jnp.broadcast_to(rows_ref[idx:idx + 1, :], (ATTN_TQ, 2 * ATTN_W))
            tab = pltpu.roll(gen, 0, 1, stride=1, stride_axis=0)[:, :ATTN_W]
            var = idx % N_BIAS_VARIANTS
            if var >= 6:
                tab = jnp.where((col // ATTN_TQ) == var - 6, tab, NEG_INF)
            head = idx // N_BIAS_VARIANTS
            bias_ref[var, head * ATTN_TQ:(head + 1) * ATTN_TQ, :] = tab

    def rows_of(ref, dil, lo, n):
        if dil == 1:
            return ref[lo:lo + n, :]
        piece, sub_len = ref.shape[2], s // dil
        parts, pos = [], lo
        while pos < lo + n:
            within = pos % sub_len
            off = within % piece
            take = min(piece - off, lo + n - pos)
            parts.append(ref[within // piece, pos // sub_len, off:off + take, :])
            pos += take
        return parts[0] if len(parts) == 1 else jnp.concatenate(parts, axis=0)

    def run_branch(bi, dil, qs_ref, ks_ref, vs_ref):
        sub_len = s // dil
        tiles_per_sub = sub_len // ATTN_TQ

        def tile(t):
            q0 = t * ATTN_TQ
            if tiles_per_sub == 1:
                ws = (t // 2) * ATTN_W
                var = 6 + t % 2
            else:
                pos = t % tiles_per_sub
                sub_lo = (t // tiles_per_sub) * sub_len
                ws = min(max(q0 - ATTN_RADIUS, sub_lo), sub_lo + sub_len - ATTN_W)
                var = (0 if pos == 0 else 2 if pos == tiles_per_sub - 1 else 1) + 3 * bi
            q = rows_of(qs_ref, dil, q0, ATTN_TQ)
            k = rows_of(ks_ref, dil, ws, ATTN_W)
            v = rows_of(vs_ref, dil, ws, ATTN_W)
            q2 = jnp.concatenate([jnp.where(head0, q, jnp.zeros_like(q)),
                                  jnp.where(head0, jnp.zeros_like(q), q)], axis=0)
            sc = lax.dot_general(q2, k, (((1,), (1,)), ((), ())), preferred_element_type=F32)
            sc = sc + bias_ref[var]
            m = jnp.max(sc, axis=-1, keepdims=True)
            p = jnp.exp2(sc - m).astype(BF16)
            o = jnp.dot(p, jnp.concatenate([v, jnp.ones_like(v)], axis=1), preferred_element_type=F32)
            l = o[:, PAIR:]
            outs, ms, ls = (o[:ATTN_TQ, :PAIR], o[ATTN_TQ:, :PAIR]), (m[:ATTN_TQ], m[ATTN_TQ:]), (l[:ATTN_TQ], l[ATTN_TQ:])
            if dil == 1:
                dst = pl.ds(q0, ATTN_TQ)
            elif dil == 4:
                dst = pl.ds((t % 4) * (4 * ATTN_TQ) + t // 4, ATTN_TQ, stride=4)
            else:
                dst = pl.ds((t % 4) * (s // 4) + t // 4, ATTN_TQ, stride=4)
            acc_ref[bi, dst, :] = jnp.where(head0, outs[0], outs[1])
            m_ref[bi, dst, :] = jnp.where(head0, ms[0], ms[1])
            l_ref[bi, dst, :] = jnp.where(head0, ls[0], ls[1])

        for t in range(n_tiles):
            tile(t)

    run_branch(0, 1, q_ref, k_ref, v_ref)
    run_branch(1, 4, q4_ref, k4_ref, v4_ref)
    run_branch(2, 16, q16_ref, k16_ref, v16_ref)

    rows = ATTN_TQ
    for r4 in range(4):
        for blk in range(s // (4 * rows)):
            nat = pl.ds(r4 + 4 * rows * blk, rows, stride=4)
            sl = (nat, nat, pl.ds(r4 * (s // 4) + rows * blk, rows))
            m = [m_ref[bi, sl[bi], :] for bi in range(3)]
            mx = jnp.maximum(jnp.maximum(m[0], m[1]), m[2])
            num = jnp.zeros((rows, PAIR), F32)
            den = jnp.zeros((rows, PAIR), F32)
            for bi in range(3):
                e = jnp.exp2(m[bi] - mx)
                num = num + e * acc_ref[bi, sl[bi], :]
                den = den + e * l_ref[bi, sl[bi], :]
            out_ref[nat, :] = num / den
    o_ref[...] = out_ref[...].astype(o_ref.dtype)


def _retention_tables(decay_fwd, decay_bwd):
    c = RET_CHUNK
    lg_f = -jnp.exp(decay_fwd.astype(F32))
    lg_b = -jnp.exp(decay_bwd.astype(F32))
    idx = jnp.arange(c, dtype=F32)
    rel = idx[:, None] - idx[None, :]
    dmat = jnp.where(rel >= 0,
                     jnp.exp(lg_f[:, None, None] * jnp.maximum(rel, 0.0)[None]),
                     jnp.exp(lg_b[:, None, None] * jnp.maximum(-rel, 0.0)[None]))
    dmat = dmat.reshape(N_PAIRS, 2 * c, c)

    def lanes(v):
        v = v.reshape(N_PAIRS, 2, -1)
        return jnp.repeat(jnp.transpose(v, (0, 2, 1)), HEAD_DIM, axis=2)

    vec = jnp.stack([
        lanes(jnp.exp(lg_f[:, None] * (idx + 1.0)[None])),
        lanes(jnp.exp(lg_f[:, None] * (c - 1.0 - idx)[None])),
        lanes(jnp.exp(lg_b[:, None] * (c - idx)[None])),
        lanes(jnp.exp(lg_b[:, None] * idx[None])),
    ], axis=1)
    same_head = (jnp.arange(PAIR)[:, None] // HEAD_DIM) == (jnp.arange(PAIR)[None, :] // HEAD_DIM)
    cd = jnp.stack([lanes(jnp.exp(lg_f * c)[:, None]), lanes(jnp.exp(lg_b * c)[:, None])], axis=1)
    cd = jnp.transpose(cd, (0, 1, 3, 2)) * same_head[None, None].astype(F32)
    return dmat, vec, cd


def _retention_kernel(q_ref, k_ref, v_ref, g_ref, dmat_ref, vec_ref, cd_ref, o_ref, kv_ref, st_ref):
    s = q_ref.shape[0]
    c = RET_CHUNK
    nc = s // c
    lane = lax.broadcasted_iota(jnp.int32, (1, PAIR), 1)
    head0 = lane < HEAD_DIM
    same_head = ((lax.broadcasted_iota(jnp.int32, (PAIR, PAIR), 0) // HEAD_DIM)
                 == (lax.broadcasted_iota(jnp.int32, (PAIR, PAIR), 1) // HEAD_DIM))

    same_head2 = jnp.concatenate([same_head, same_head], axis=0)
    for n in range(nc):
        rows = slice(n * c, (n + 1) * c)
        kf = k_ref[rows, :].astype(F32)
        kcat = jnp.concatenate([(kf * vec_ref[1]).astype(BF16), (kf * vec_ref[3]).astype(BF16)], axis=1)
        kv = lax.dot_general(kcat, v_ref[rows, :], (((0,), (0,)), ((), ())), preferred_element_type=F32)
        kv_ref[n] = jnp.where(same_head2, kv, 0.0)

    state = jnp.zeros((PAIR, PAIR), F32)
    for n in range(nc):
        st_ref[n, :PAIR, :] = state.astype(BF16)
        state = state * cd_ref[0] + kv_ref[n, :PAIR, :]
    state = jnp.zeros((PAIR, PAIR), F32)
    for n in reversed(range(nc)):
        st_ref[n, PAIR:, :] = state.astype(BF16)
        state = state * cd_ref[1] + kv_ref[n, PAIR:, :]

    for n in range(nc):
        rows = slice(n * c, (n + 1) * c)
        q, k, v = q_ref[rows, :], k_ref[rows, :], v_ref[rows, :]
        q2 = jnp.concatenate([jnp.where(head0, q, jnp.zeros_like(q)),
                              jnp.where(head0, jnp.zeros_like(q), q)], axis=0)
        sc = lax.dot_general(q2, k, (((1,), (1,)), ((), ())), preferred_element_type=F32)
        intra = jnp.dot((sc * dmat_ref[...]).astype(BF16), v, preferred_element_type=F32)
        qf = q.astype(F32)
        qcat = jnp.concatenate([(qf * vec_ref[0]).astype(BF16), (qf * vec_ref[2]).astype(BF16)], axis=1)
        y = jnp.where(head0, intra[:c], intra[c:]) + jnp.dot(qcat, st_ref[n], preferred_element_type=F32)
        y2 = y * y
        ms0 = jnp.sum(jnp.where(head0, y2, 0.0), axis=-1, keepdims=True)
        ms1 = jnp.sum(jnp.where(head0, 0.0, y2), axis=-1, keepdims=True)
        ms = jnp.where(head0, ms0, ms1) * (1.0 / HEAD_DIM)
        g = g_ref[rows, :].astype(F32)
        o_ref[rows, :] = (y * lax.rsqrt(ms + EPS) * (g * jax.nn.sigmoid(g))).astype(o_ref.dtype)


def _mixers(qkv, qkv_grouped, bias_rows, ret_inputs, ret_tables, weights_f32):
    b, s, _ = qkv[0].shape
    c = RET_CHUNK
    n_steps = N_PAIRS * b
    spec = pl.BlockSpec((None, s, PAIR), lambda hp, bi: (bi, 0, hp))
    gspecs = [pl.BlockSpec((None, g.shape[1], None) + g.shape[3:], lambda hp, bi: (bi, 0, hp, 0, 0, 0))
              for g in qkv_grouped]
    slabs = [w.reshape(n_steps, -1, w.shape[-1]) for w in weights_f32]
    slab_specs = [pl.BlockSpec((None,) + w.shape[1:], lambda hp, bi: (hp * b + bi, 0, 0)) for w in slabs]
    n_attn_in, n_ret_in, n_w = 10, 7, len(slabs)
    attn_scratch = [
        pltpu.VMEM((N_BIAS_VARIANTS, 2 * ATTN_TQ, ATTN_W), F32),
        pltpu.VMEM((3, s, PAIR), F32),
        pltpu.VMEM((3, s, PAIR), F32),
        pltpu.VMEM((3, s, PAIR), F32),
        pltpu.VMEM((s, PAIR), F32),
    ]
    ret_scratch = [pltpu.VMEM((s // c, 2 * PAIR, PAIR), F32),
                   pltpu.VMEM((s // c, 2 * PAIR, PAIR), BF16)]

    def kernel(*refs):
        ins, rest = refs[:n_attn_in + n_ret_in + n_w], refs[n_attn_in + n_ret_in + n_w:]
        attn_o, ret_o, w_out, scratch = rest[0], rest[1], rest[2:2 + n_w], rest[2 + n_w:]
        _attention_kernel(*ins[:n_attn_in], attn_o, *scratch[:len(attn_scratch)])
        _retention_kernel(*ins[n_attn_in:n_attn_in + n_ret_in], ret_o, *scratch[len(attn_scratch):])
        for src, dst in zip(ins[n_attn_in + n_ret_in:], w_out):
            dst[...] = src[...].astype(BF16)

    outs = pl.pallas_call(
        kernel,
        grid=(N_PAIRS, b),
        in_specs=[spec] * 3 + gspecs
        + [pl.BlockSpec((None, 2 * N_BIAS_VARIANTS, 2 * ATTN_W), lambda hp, bi: (hp, 0, 0))]
        + [spec] * 4
        + [pl.BlockSpec((None, 2 * c, c), lambda hp, bi: (hp, 0, 0)),
           pl.BlockSpec((None, 4, c, PAIR), lambda hp, bi: (hp, 0, 0, 0)),
           pl.BlockSpec((None, 2, PAIR, PAIR), lambda hp, bi: (hp, 0, 0, 0))]
        + slab_specs,
        out_specs=[spec, spec] + slab_specs,
        out_shape=[jax.ShapeDtypeStruct((b, s, ATTN_WIDTH), BF16), jax.ShapeDtypeStruct((b, s, RET_WIDTH), BF16)]
        + [jax.ShapeDtypeStruct(w.shape, BF16) for w in slabs],
        scratch_shapes=attn_scratch + ret_scratch,
        compiler_params=pltpu.CompilerParams(
            dimension_semantics=("arbitrary", "arbitrary"), vmem_limit_bytes=VMEM_LIMIT),
        name="mixers",
    )(*qkv, *qkv_grouped, bias_rows, *ret_inputs, *ret_tables, *slabs)
    return outs[0], outs[1], [o.reshape(w.shape) for o, w in zip(outs[2:], weights_f32)]


def _route_tile(sub, attn_ref, ret_ref, x_ref, wo_ref, again_ref, fgain_ref,
                rw_ref, rb_ref, h_ref, xs_ref, dest_ref, cnt_ref):
    tm = ROUTE_TM
    rows = slice(sub * tm, (sub + 1) * tm)
    a = _rms(attn_ref[rows, :].astype(F32), again_ref[...]).astype(BF16)
    mixed = jnp.concatenate([a, ret_ref[rows, :]], axis=1)
    h = x_ref[rows, :] + jnp.dot(mixed, wo_ref[...], preferred_element_type=F32)
    h_ref[rows, :] = h.astype(h_ref.dtype)
    hn = _rms(h, fgain_ref[...]).astype(BF16)

    logits = jnp.dot(hn, rw_ref[...], preferred_element_type=F32) + rb_ref[...]
    lane = lax.broadcasted_iota(jnp.int32, logits.shape, 1)
    big = jnp.int32(LANES)

    def first_argmax(vals):
        top = jnp.max(vals, axis=-1, keepdims=True)
        return top, jnp.min(jnp.where(vals == top, lane, big), axis=-1, keepdims=True)

    gl = jnp.where((lane >= N_EXPERTS) & (lane < N_EXPERTS + N_GROUPS), logits, -jnp.inf)
    gmax, gidx = first_argmax(gl)
    p_group = 1.0 / jnp.sum(jnp.exp(gl - gmax), axis=-1, keepdims=True)
    grp = gidx - N_EXPERTS
    lo = grp * EXPERTS_PER_GROUP
    el = jnp.where((lane >= lo) & (lane < lo + EXPERTS_PER_GROUP), logits, -jnp.inf)
    v1, i1 = first_argmax(el)
    v2, i2 = first_argmax(jnp.where(lane == i1, -jnp.inf, el))
    e2 = jnp.exp(v2 - v1)
    p1 = p_group / (1.0 + e2)
    gates = jnp.where(lane == i1, p1, jnp.where(lane == i2, p1 * e2, 0.0))

    onehot = jnp.where(lane == grp, 1.0, 0.0)
    r_i = lax.broadcasted_iota(jnp.int32, (tm, tm), 0)
    c_i = lax.broadcasted_iota(jnp.int32, (tm, tm), 1)
    before = jnp.where(c_i < r_i, 1.0, 0.0).astype(BF16)
    rank = jnp.dot(before, onehot.astype(BF16), preferred_element_type=F32)
    cnt = jnp.sum(onehot, axis=0, keepdims=True)
    c16 = jnp.broadcast_to(jnp.ceil(cnt * (1.0 / CHUNK)) * CHUNK, (8, LANES))
    lane8 = lax.broadcasted_iota(jnp.int32, (8, LANES), 1)
    start = jnp.zeros((8, LANES), F32)
    for sft in range(1, N_GROUPS):
        start = start + jnp.where(lane8 >= sft, pltpu.roll(c16, sft, 1), 0.0)
    dest = jnp.sum(onehot * (start[0:1] + rank), axis=-1, keepdims=True)
    dest_ref[rows, :] = jnp.broadcast_to(dest, (tm, LANES))
    cnt_ref[sub] = jnp.where(lane8 < N_GROUPS, c16, pltpu.roll(start, N_GROUPS, 1)).astype(jnp.int32)

    onehot_t = onehot.T
    rank_t = lax.dot_general(onehot_t.astype(BF16), before, (((1,), (1,)), ((), ())),
                             preferred_element_type=F32)
    g_row = lax.broadcasted_iota(jnp.int32, (LANES, LANES), 0)
    g_col = lax.broadcasted_iota(jnp.int32, (LANES, LANES), 1)
    start_t = jnp.sum(jnp.where(g_col < g_row, c16[0:1], 0.0), axis=-1, keepdims=True)
    dest_t = jnp.sum(onehot_t * (start_t + rank_t), axis=0, keepdims=True)
    perm = jnp.where(lax.broadcasted_iota(jnp.int32, (SORT_ROWS, tm), 0) == dest_t.astype(jnp.int32),
                     1.0, 0.0).astype(BF16)
    g_hi = gates.astype(BF16)
    g_lo = (gates - g_hi.astype(F32)).astype(BF16)
    payload = jnp.concatenate([hn, g_hi, g_lo], axis=1)
    xs_ref[sub * SORT_ROWS:(sub + 1) * SORT_ROWS, :] = (
        jnp.dot(perm, payload, preferred_element_type=F32).astype(BF16))


def _outproj_kernel(*refs):
    for sub in range(ROUTE_TILES_PER_STEP):
        _route_tile(sub, *refs)


def _outproj(attn, ret, x, w_o, again, fgain, rw, rb):
    t, d = x.shape
    per_step = ROUTE_TILES_PER_STEP
    tm = ROUTE_TM * per_step
    nt = t // ROUTE_TM
    row = lambda w: pl.BlockSpec((tm, w), lambda i: (i, 0))
    full = lambda r, c: pl.BlockSpec((r, c), lambda i: (0, 0))
    return pl.pallas_call(
        _outproj_kernel,
        grid=(nt // per_step,),
        in_specs=[row(ATTN_WIDTH), row(RET_WIDTH), row(d), full(ATTN_WIDTH + RET_WIDTH, d),
                  full(1, ATTN_WIDTH), full(1, d), full(d, LANES), full(1, LANES)],
        out_specs=[row(d), pl.BlockSpec((per_step * SORT_ROWS, XS_W), lambda i: (i, 0)), row(LANES),
                   pl.BlockSpec((per_step, 8, LANES), lambda i: (i, 0, 0))],
        out_shape=[jax.ShapeDtypeStruct((t, d), BF16),
                   jax.ShapeDtypeStruct((nt * SORT_ROWS, XS_W), BF16),
                   jax.ShapeDtypeStruct((t, LANES), F32),
                   jax.ShapeDtypeStruct((nt, 8, LANES), jnp.int32)],
        compiler_params=pltpu.CompilerParams(
            dimension_semantics=("arbitrary",), vmem_limit_bytes=VMEM_LIMIT),
        name="outproj",
    )(attn, ret, x, w_o, again, fgain, rw, rb)


def _dispatch_tables(cnt, n_moe_tiles):
    nt = cnt.shape[0]
    cpt = MOE_TM // CHUNK

    n_slots = n_moe_tiles * cpt
    slot_rows = -(-n_slots // (8 * LANES)) * 8

    def schedule_kernel(seg_ref, grp_ref, used_ref, src_ref, tail_ref):
        slot = (lax.broadcasted_iota(jnp.int32, (slot_rows, LANES), 0) * LANES
                + lax.broadcasted_iota(jnp.int32, (slot_rows, LANES), 1))
        src = jnp.full((slot_rows, LANES), -1, jnp.int32)
        pos = jnp.int32(0)
        for g in range(N_GROUPS):
            def tile_body(i, carry):
                p, src = carry
                n = seg_ref[i, g] // CHUNK
                first = (i * SORT_ROWS + seg_ref[i, N_GROUPS + g]) // CHUNK
                src = jnp.where((slot >= p) & (slot < p + n), slot + (first - p), src)
                return p + n, src
            end, src = lax.fori_loop(0, nt, tile_body, (pos, src))
            padded = ((end + cpt - 1) // cpt) * cpt

            def mark_body(m, carry):
                grp_ref[m] = g
                used_ref[m] = 1
                return carry
            lax.fori_loop(pos // cpt, padded // cpt, mark_body, 0)
            pos = padded

        def idle_body(m, carry):
            grp_ref[m] = N_GROUPS - 1
            used_ref[m] = 0
            return carry
        lax.fori_loop(pos // cpt, n_moe_tiles, idle_body, 0)
        src_ref[...] = src

        def tail_body(i, carry):
            rows = seg_ref[i, 0]
            for g in range(1, N_GROUPS):
                rows = rows + seg_ref[i, g]
            tail_ref[i] = rows // CHUNK
            return carry
        lax.fori_loop(0, nt, tail_body, 0)

    smem = lambda: pl.BlockSpec(memory_space=pltpu.SMEM)
    tile_group, tile_used, src, tile_tail = pl.pallas_call(
        schedule_kernel,
        in_specs=[smem()],
        out_specs=[smem(), smem(), pl.BlockSpec(memory_space=pltpu.VMEM), smem()],
        out_shape=[jax.ShapeDtypeStruct((n_moe_tiles,), jnp.int32),
                   jax.ShapeDtypeStruct((n_moe_tiles,), jnp.int32),
                   jax.ShapeDtypeStruct((slot_rows, LANES), jnp.int32),
                   jax.ShapeDtypeStruct((nt,), jnp.int32)],
        name="schedule",
    )(cnt[:, 0, :2 * N_GROUPS])
    return tile_group, tile_used, src.reshape(-1)[:n_slots], tile_tail


def _moe_kernel(grp_ref, used_ref, src_ref, tail_ref, xs_hbm, w1_ref, w3_ref, w2_ref, ys_hbm,
                xbuf, obuf, zbuf, in_sem, out_sem, zero_sem):
    m = pl.program_id(0)
    n_tiles = pl.num_programs(0)
    cpt = MOE_TM // CHUNK
    cps = SORT_ROWS // CHUNK
    slot = m % 2
    zero_chunk = cps - 1

    def rows(c):
        return pl.ds(pl.multiple_of(c * CHUNK, CHUNK), CHUNK)

    def zero_tails(wait):
        def tile_body(i, carry):
            def body(c, carry2):
                cp = pltpu.make_async_copy(zbuf, ys_hbm.at[rows(i * cps + c), :], zero_sem)
                cp.wait() if wait else cp.start()
                return carry2
            lax.fori_loop(tail_ref[i], cps, body, 0)
            return carry
        lax.fori_loop(0, tail_ref.shape[0], tile_body, 0)

    def gather_start(tile, sl):
        for c in range(cpt):
            src = src_ref[tile * cpt + c]
            src = jnp.where(src < 0, zero_chunk, src)
            pltpu.make_async_copy(xs_hbm.at[rows(src), :], xbuf.at[sl, c * CHUNK:(c + 1) * CHUNK, :],
                                  in_sem.at[sl]).start()

    def gather_wait(sl):
        pltpu.make_async_copy(xs_hbm.at[0:MOE_TM, :], xbuf.at[sl], in_sem.at[sl]).wait()

    def scatter(tile, sl, wait):
        full = src_ref[tile * cpt + cpt - 1] >= 0

        def chunk_copy(c, src):
            return pltpu.make_async_copy(obuf.at[sl, c * CHUNK:(c + 1) * CHUNK, :], ys_hbm.at[rows(src), :],
                                         out_sem.at[sl])

        @pl.when(full)
        def _():
            if wait:
                pltpu.make_async_copy(obuf.at[sl], ys_hbm.at[0:MOE_TM, :], out_sem.at[sl]).wait()
            else:
                for c in range(cpt):
                    chunk_copy(c, src_ref[tile * cpt + c]).start()

        @pl.when(jnp.logical_not(full))
        def _():
            for c in range(cpt):
                src = src_ref[tile * cpt + c]

                @pl.when(src >= 0)
                def _():
                    cp = chunk_copy(c, src)
                    cp.wait() if wait else cp.start()

    @pl.when(m == 0)
    def _():
        gather_start(0, 0)
        zbuf[...] = jnp.zeros(zbuf.shape, zbuf.dtype)
        zero_tails(False)

    prev_used = used_ref[jnp.maximum(m - 1, 0)] > 0

    @pl.when((m == 0) | prev_used)
    def _():
        gather_wait(slot)

    @pl.when((m >= 2) & (used_ref[jnp.maximum(m - 2, 0)] > 0))
    def _():
        scatter(m - 2, slot, True)

    def expert_tile(n_rows):
        gather_start(jnp.minimum(m + 1, n_tiles - 1), 1 - slot)
        x = xbuf[slot, :n_rows, :D_MODEL]
        gate = (xbuf[slot, :n_rows, D_MODEL:D_MODEL + LANES].astype(F32)
                + xbuf[slot, :n_rows, D_MODEL + LANES:].astype(F32))
        lane = lax.broadcasted_iota(jnp.int32, gate.shape, 1)
        base = grp_ref[m] * EXPERTS_PER_GROUP
        acc = jnp.zeros((n_rows, D_MODEL), F32)
        for j in range(EXPERTS_PER_GROUP):
            a = jnp.dot(x, w1_ref[j], preferred_element_type=F32)
            b = jnp.dot(x, w3_ref[j], preferred_element_type=F32)
            gj = jnp.sum(jnp.where(lane == base + j, gate, 0.0), axis=-1, keepdims=True)
            hid = (a * jax.nn.sigmoid(a) * b * gj).astype(BF16)
            acc = acc + jnp.dot(hid, w2_ref[j], preferred_element_type=F32)
        obuf[slot, :n_rows, :] = acc.astype(BF16)
        scatter(m, slot, False)

    half_empty = src_ref[m * cpt + cpt // 2] < 0

    @pl.when((used_ref[m] > 0) & jnp.logical_not(half_empty))
    def _():
        expert_tile(MOE_TM)

    @pl.when((used_ref[m] > 0) & half_empty)
    def _():
        expert_tile(MOE_TM // 2)

    @pl.when(m == n_tiles - 1)
    def _():
        zero_tails(True)

        @pl.when(used_ref[m] > 0)
        def _():
            gather_wait(1 - slot)
            scatter(m, slot, True)

        @pl.when((m >= 1) & prev_used)
        def _():
            scatter(m - 1, 1 - slot, True)


def _moe(xs, w1, w3, w2, tile_group, tile_used, src_chunk, tile_tail):
    n_moe_tiles = tile_group.shape[0]
    rows = xs.shape[0]
    d = D_MODEL
    wspec = lambda r, c: pl.BlockSpec((EXPERTS_PER_GROUP, r, c), lambda m, grp, *_: (grp[m], 0, 0))
    return pl.pallas_call(
        _moe_kernel,
        grid_spec=pltpu.PrefetchScalarGridSpec(
            num_scalar_prefetch=4,
            grid=(n_moe_tiles,),
            in_specs=[pl.BlockSpec(memory_space=pl.ANY),
                      wspec(d, EXPERT_FF), wspec(d, EXPERT_FF), wspec(EXPERT_FF, d)],
            out_specs=pl.BlockSpec(memory_space=pl.ANY),
            scratch_shapes=[pltpu.VMEM((2, MOE_TM, XS_W), BF16),
                            pltpu.VMEM((2, MOE_TM, d), BF16),
                            pltpu.VMEM((CHUNK, d), BF16),
                            pltpu.SemaphoreType.DMA((2,)),
                            pltpu.SemaphoreType.DMA((2,)),
                            pltpu.SemaphoreType.DMA(())]),
        out_shape=jax.ShapeDtypeStruct((rows, d), BF16),
        compiler_params=pltpu.CompilerParams(
            dimension_semantics=("arbitrary",), vmem_limit_bytes=VMEM_LIMIT),
        name="moe",
    )(tile_group, tile_used, src_chunk, tile_tail, xs, w1, w3, w2)


def _combine_kernel(ys_ref, h_ref, dest_ref, gain_ref, o_ref):
    tm = ROUTE_TM
    for sub in range(h_ref.shape[0] // tm):
        rows = slice(sub * tm, (sub + 1) * tm)
        dest = dest_ref[rows, 0:1].astype(jnp.int32)
        perm_t = jnp.where(lax.broadcasted_iota(jnp.int32, (tm, SORT_ROWS), 1) == dest, 1.0, 0.0).astype(BF16)
        moe = jnp.dot(perm_t, ys_ref[sub * SORT_ROWS:(sub + 1) * SORT_ROWS, :], preferred_element_type=F32)
        o_ref[rows, :] = _rms(h_ref[rows, :].astype(F32) + moe, gain_ref[...])


def _combine(ys, h, dest, gain):
    t, d = h.shape
    per_step = COMBINE_TILES_PER_STEP
    tm = ROUTE_TM * per_step
    return pl.pallas_call(
        _combine_kernel,
        grid=(t // tm,),
        in_specs=[pl.BlockSpec((per_step * SORT_ROWS, d), lambda i: (i, 0)),
                  pl.BlockSpec((tm, d), lambda i: (i, 0)),
                  pl.BlockSpec((tm, LANES), lambda i: (i, 0)),
                  pl.BlockSpec((1, d), lambda i: (0, 0))],
        out_specs=pl.BlockSpec((tm, d), lambda i: (i, 0)),
        out_shape=jax.ShapeDtypeStruct((t, d), F32),
        compiler_params=pltpu.CompilerParams(
            dimension_semantics=("arbitrary",), vmem_limit_bytes=VMEM_LIMIT),
        name="combine",
    )(ys, h, dest, gain)


def _rotary_tables(s):
    half = HEAD_DIM // 2
    inv = ROPE_BASE ** (-jnp.arange(half, dtype=F32) / half)
    ang = jnp.arange(s, dtype=F32)[:, None] * inv[None, :]
    cos, sin = jnp.cos(ang), jnp.sin(ang)
    cos_t = jnp.tile(jnp.concatenate([cos, cos], axis=-1), (1, LANES // HEAD_DIM))
    sin_t = jnp.tile(jnp.concatenate([-sin, sin], axis=-1), (1, LANES // HEAD_DIM))
    return cos_t, sin_t


def kernel(x, w_in, w_out, norm_mix, norm_ffn, norm_final, attn_out_gain, rel_bias, ret_decay_fwd, ret_decay_bwd, router_group_w, router_group_b, router_expert_w, router_expert_b, expert_w1, expert_w3, expert_w2):
    b, s, d = x.shape
    depth = w_in.shape[0]
    cos_t, sin_t = _rotary_tables(s)
    bias_rows = _attn_bias_rows(rel_bias)
    h = x
    for layer in range(depth):
        (aq, ak, av, rq, rk, rv, rg), qkv_grouped = _inproj(
            h, norm_mix[layer][None], w_in[layer].astype(BF16), cos_t, sin_t)
        attn, ret, (w1, w3, w2, w_o) = _mixers(
            (aq, ak, av), qkv_grouped, bias_rows, (rq, rk, rv, rg),
            _retention_tables(ret_decay_fwd[layer], ret_decay_bwd[layer]),
            (expert_w1[layer], expert_w3[layer], expert_w2[layer], w_out[layer]))

        rw = jnp.concatenate(
            [jnp.transpose(router_expert_w[layer], (1, 0, 2)).reshape(d, N_EXPERTS),
             router_group_w[layer],
             jnp.zeros((d, LANES - N_EXPERTS - N_GROUPS), F32)], axis=1).astype(BF16)
        rb = jnp.concatenate(
            [router_expert_b[layer].reshape(N_EXPERTS), router_group_b[layer],
             jnp.zeros((LANES - N_EXPERTS - N_GROUPS,), F32)])[None].astype(F32)
        h1, xs, dest, cnt = _outproj(
            attn.reshape(b * s, ATTN_WIDTH), ret.reshape(b * s, RET_WIDTH), h.reshape(b * s, d),
            w_o, attn_out_gain[layer][None], norm_ffn[layer][None],
            rw, rb)
        n_route_tiles = (b * s) // ROUTE_TM
        n_moe_tiles = (b * s + n_route_tiles * N_GROUPS * (CHUNK - 1)) // MOE_TM + N_GROUPS
        schedule = _dispatch_tables(cnt, n_moe_tiles)
        ys = _moe(xs, w1, w3, w2, *schedule)
        assert depth == 1, "the combine kernel fuses the final norm, so it must run on the last layer"
        h = _combine(ys, h1, dest, norm_final[None]).reshape(b, s, d)
    return h
```

```python
import functools
import math

import jax
import jax.numpy as jnp
from jax import lax
from jax.experimental import pallas as pl
from jax.experimental.pallas import tpu as pltpu

F32 = jnp.float32
BF16 = jnp.bfloat16

D_MODEL = 1024
HEAD_DIM = 64
ATTN_WIDTH = 512
RET_WIDTH = 512
N_HEADS = 8
PAIR = 2 * HEAD_DIM
N_PAIRS = N_HEADS // 2
ATTN_DILATIONS = (1, 4, 16)
ATTN_RADIUS = 64
N_BUCKETS = 32
REL_MAX_DIST = 1024
ROPE_BASE = 10000.0
N_GROUPS = 4
EXPERTS_PER_GROUP = 4
N_EXPERTS = 16
EXPERT_FF = 512
EPS = 1e-6
NEG_INF = -1e30
LOG2_E = math.log2(math.e)

LANES = 128
ATTN_TQ = 128
ATTN_W = 256
N_BIAS_VARIANTS = 8
RET_CHUNK = 256
ROUTE_TM = 512
ROUTE_TILES_PER_STEP = 2
COMBINE_TILES_PER_STEP = 4
SORT_ROWS = 592
CHUNK = 16
MOE_TM = 512
XS_W = D_MODEL + 2 * LANES
VMEM_LIMIT = 48 * 1024 * 1024


def _rms(x, gain):
    return x * lax.rsqrt(jnp.mean(x * x, axis=-1, keepdims=True) + EPS) * gain


def _inproj_kernel(x_ref, gain_ref, w_ref, cos_ref, sin_ref,
                   aq_ref, ak_ref, av_ref, rq_ref, rk_ref, rv_ref, rg_ref,
                   aq4_ref, ak4_ref, av4_ref, aq16_ref, ak16_ref, av16_ref, stage_ref, stage4_ref):
    tm = x_ref.shape[0]
    xn = _rms(x_ref[...], gain_ref[...]).astype(BF16)

    def stage(i, t, nat_ref):
        nat_ref[...] = t.astype(BF16)
        for hp in range(N_PAIRS):
            stage_ref[i, hp] = t[:, hp * PAIR:(hp + 1) * PAIR]

    def regroup4(i, d4_ref):
        for hp in range(N_PAIRS):
            for r4 in range(4):
                g4 = stage_ref[i, hp, pl.ds(r4, tm // 4, stride=4), :]
                d4_ref[hp, r4] = g4.astype(BF16)
                stage4_ref[i, hp, r4] = g4

    def regroup16(i, d16_ref):
        for hp in range(N_PAIRS):
            for r4 in range(4):
                for j in range(4):
                    d16_ref[hp, r4 + 4 * j] = (
                        stage4_ref[i, hp, r4, pl.ds(j, tm // 16, stride=4), :].astype(BF16))

    def seg(i):
        return jnp.dot(xn, w_ref[:, i * ATTN_WIDTH:(i + 1) * ATTN_WIDTH], preferred_element_type=F32)

    def rotary(t):
        cos, sin = cos_ref[...], sin_ref[...]
        first_half = (lax.broadcasted_iota(jnp.int32, (1, LANES), 1) % HEAD_DIM) < HEAD_DIM // 2
        outs = []
        for j in range(t.shape[1] // LANES):
            tj = t[:, j * LANES:(j + 1) * LANES]
            partner = jnp.where(first_half, pltpu.roll(tj, LANES - HEAD_DIM // 2, 1),
                                pltpu.roll(tj, HEAD_DIM // 2, 1))
            outs.append(tj * cos + partner * sin)
        return jnp.concatenate(outs, axis=1)

    stage(0, seg(0) * (HEAD_DIM ** -0.5 * LOG2_E), aq_ref)
    stage(1, seg(1), ak_ref)
    stage(2, seg(2), av_ref)
    rq_ref[...] = rotary(seg(3)).astype(BF16)
    for i, d4_ref in enumerate((aq4_ref, ak4_ref, av4_ref)):
        regroup4(i, d4_ref)
    rk_ref[...] = (rotary(seg(4)) * (HEAD_DIM ** -0.5)).astype(BF16)
    rv_ref[...] = seg(5).astype(BF16)
    for i, d16_ref in enumerate((aq16_ref, ak16_ref, av16_ref)):
        regroup16(i, d16_ref)
    rg_ref[...] = seg(6).astype(BF16)


def _inproj(x, gain, w_in, cos_t, sin_t, tm=512):
    b, s, d = x.shape
    n = w_in.shape[1]
    out = jax.ShapeDtypeStruct((b, s, ATTN_WIDTH), BF16)
    ospec = pl.BlockSpec((None, tm, ATTN_WIDTH), lambda si, bi: (bi, si, 0))

    def grouped(dil):
        shape = jax.ShapeDtypeStruct((b, s // tm, N_PAIRS, dil, tm // dil, PAIR), BF16)
        spec = pl.BlockSpec((None, None, N_PAIRS, dil, tm // dil, PAIR), lambda si, bi: (bi, si, 0, 0, 0, 0))
        return [shape] * 3, [spec] * 3

    shapes4, specs4 = grouped(4)
    shapes16, specs16 = grouped(16)
    outs = pl.pallas_call(
        _inproj_kernel,
        grid=(s // tm, b),
        in_specs=[
            pl.BlockSpec((None, tm, d), lambda si, bi: (bi, si, 0)),
            pl.BlockSpec((1, d), lambda si, bi: (0, 0)),
            pl.BlockSpec((d, n), lambda si, bi: (0, 0)),
            pl.BlockSpec((tm, LANES), lambda si, bi: (si, 0)),
            pl.BlockSpec((tm, LANES), lambda si, bi: (si, 0)),
        ],
        out_specs=[ospec] * 7 + specs4 + specs16,
        out_shape=[out] * 7 + shapes4 + shapes16,
        scratch_shapes=[pltpu.VMEM((3, N_PAIRS, tm, PAIR), F32),
                        pltpu.VMEM((3, N_PAIRS, 4, tm // 4, PAIR), F32)],
        compiler_params=pltpu.CompilerParams(
            dimension_semantics=("arbitrary", "arbitrary"), vmem_limit_bytes=VMEM_LIMIT),
        name="inproj",
    )(x, gain, w_in, cos_t, sin_t)
    return outs[:7], outs[7:]


def _t5_bucket(rel):
    half = N_BUCKETS // 2
    max_exact = half // 2
    offset = jnp.where(rel > 0, half, 0)
    n = jnp.abs(rel)
    nf = jnp.maximum(n, 1).astype(F32)
    large = max_exact + (jnp.log(nf / max_exact) / math.log(REL_MAX_DIST / max_exact)
                         * (half - max_exact)).astype(jnp.int32)
    large = jnp.minimum(large, half - 1)
    return offset + jnp.where(n < max_exact, n, large)


def _attn_bias_rows(rel_bias):
    period = 2 * ATTN_W
    band = 2 * ATTN_RADIUS + 1
    rel = jnp.arange(-ATTN_RADIUS, ATTN_RADIUS + 1)
    rows = []
    for dil, offs in ((1, (0, 64, 128)), (4, (0, 64, 128)), (16, (0, 128))):
        vals = rel_bias[_t5_bucket(rel * dil)].astype(F32).T * LOG2_E
        for off in offs:
            lo = off - ATTN_RADIUS
            pad = jnp.full((N_HEADS, period - band), NEG_INF, F32)
            if lo >= 0:
                row = jnp.concatenate([pad[:, :lo], vals, pad[:, lo:]], axis=1)
            else:
                row = jnp.concatenate([vals[:, -lo:], pad, vals[:, :-lo]], axis=1)
            rows.append(row)
    v = jnp.stack(rows, axis=1)
    return v.reshape(N_PAIRS, 2 * N_BIAS_VARIANTS, period)


def _attention_kernel(q_ref, k_ref, v_ref, q4_ref, k4_ref, v4_ref, q16_ref, k16_ref, v16_ref,
                      rows_ref, o_ref, bias_ref, acc_ref, m_ref, l_ref, out_ref):
    s = q_ref.shape[0]
    n_tiles = s // ATTN_TQ
    lane = lax.broadcasted_iota(jnp.int32, (1, PAIR), 1)
    head0 = lane < HEAD_DIM

    @pl.when(pl.program_id(1) == 0)
    def _():
        col = lax.broadcasted_iota(jnp.int32, (ATTN_TQ, ATTN_W), 1)
        for idx in range(2 * N_BIAS_VARIANTS):
            gen = jnp.broadcast_to(rows_ref[idx:idx + 1, :], (ATTN_TQ, 2 * ATTN_W))
            tab = pltpu.roll(gen, 0, 1, stride=1, stride_axis=0)[:, :ATTN_W]
            var = idx % N_BIAS_VARIANTS
            if var >= 6:
                tab = jnp.where((col // ATTN_TQ) == var - 6, tab, NEG_INF)
            head = idx // N_BIAS_VARIANTS
            bias_ref[var, head * ATTN_TQ:(head + 1) * ATTN_TQ, :] = tab

    def rows_of(ref, dil, lo, n):
        if dil == 1:
            return ref[lo:lo + n, :]
        piece, sub_len = ref.shape[2], s // dil
        parts, pos = [], lo
        while pos < lo + n:
            within = pos % sub_len
            off = within % piece
            take = min(piece - off, lo + n - pos)
            parts.append(ref[within // piece, pos // sub_len, off:off + take, :])
            pos += take
        return parts[0] if len(parts) == 1 else jnp.concatenate(parts, axis=0)

    def run_branch(bi, dil, qs_ref, ks_ref, vs_ref):
        sub_len = s // dil
        tiles_per_sub = sub_len // ATTN_TQ

        def tile(t):
            q0 = t * ATTN_TQ
            if tiles_per_sub == 1:
                ws = (t // 2) * ATTN_W
                var = 6 + t % 2
            else:
                pos = t % tiles_per_sub
                sub_lo = (t // tiles_per_sub) * sub_len
                ws = min(max(q0 - ATTN_RADIUS, sub_lo), sub_lo + sub_len - ATTN_W)
                var = (0 if pos == 0 else 2 if pos == tiles_per_sub - 1 else 1) + 3 * bi
            q = rows_of(qs_ref, dil, q0, ATTN_TQ)
            k = rows_of(ks_ref, dil, ws, ATTN_W)
            v = rows_of(vs_ref, dil, ws, ATTN_W)
            q2 = jnp.concatenate([jnp.where(head0, q, jnp.zeros_like(q)),
                                  jnp.where(head0, jnp.zeros_like(q), q)], axis=0)
            sc = lax.dot_general(q2, k, (((1,), (1,)), ((), ())), preferred_element_type=F32)
            sc = sc + bias_ref[var]
            m = jnp.max(sc, axis=-1, keepdims=True)
            p = jnp.exp2(sc - m).astype(BF16)
            o = jnp.dot(p, jnp.concatenate([v, jnp.ones_like(v)], axis=1), preferred_element_type=F32)
            l = o[:, PAIR:]
            outs, ms, ls = (o[:ATTN_TQ, :PAIR], o[ATTN_TQ:, :PAIR]), (m[:ATTN_TQ], m[ATTN_TQ:]), (l[:ATTN_TQ], l[ATTN_TQ:])
            if dil == 1:
                dst = pl.ds(q0, ATTN_TQ)
            elif dil == 4:
                dst = pl.ds((t % 4) * (4 * ATTN_TQ) + t // 4, ATTN_TQ, stride=4)
            else:
                dst = pl.ds((t % 4) * (s // 4) + t // 4, ATTN_TQ, stride=4)
            acc_ref[bi, dst, :] = jnp.where(head0, outs[0], outs[1])
            m_ref[bi, dst, :] = jnp.where(head0, ms[0], ms[1])
            l_ref[bi, dst, :] = jnp.where(head0, ls[0], ls[1])

        for t in range(n_tiles):
            tile(t)

    run_branch(0, 1, q_ref, k_ref, v_ref)
    run_branch(1, 4, q4_ref, k4_ref, v4_ref)
    run_branch(2, 16, q16_ref, k16_ref, v16_ref)

    rows = ATTN_TQ
    for r4 in range(4):
        for blk in range(s // (4 * rows)):
            nat = pl.ds(r4 + 4 * rows * blk, rows, stride=4)
            sl = (nat, nat, pl.ds(r4 * (s // 4) + rows * blk, rows))
            m = [m_ref[bi, sl[bi], :] for bi in range(3)]
            mx = jnp.maximum(jnp.maximum(m[0], m[1]), m[2])
            num = jnp.zeros((rows, PAIR), F32)
            den = jnp.zeros((rows, PAIR), F32)
            for bi in range(3):
                e = jnp.exp2(m[bi] - mx)
                num = num + e * acc_ref[bi, sl[bi], :]
                den = den + e * l_ref[bi, sl[bi], :]
            out_ref[nat, :] = num / den
    o_ref[...] = out_ref[...].astype(o_ref.dtype)


def _retention_tables(decay_fwd, decay_bwd):
    c = RET_CHUNK
    lg_f = -jnp.exp(decay_fwd.astype(F32))
    lg_b = -jnp.exp(decay_bwd.astype(F32))
    idx = jnp.arange(c, dtype=F32)
    rel = idx[:, None] - idx[None, :]
    dmat = jnp.where(rel >= 0,
                     jnp.exp(lg_f[:, None, None] * jnp.maximum(rel, 0.0)[None]),
                     jnp.exp(lg_b[:, None, None] * jnp.maximum(-rel, 0.0)[None]))
    dmat = dmat.reshape(N_PAIRS, 2 * c, c)

    def lanes(v):
        v = v.reshape(N_PAIRS, 2, -1)
        return jnp.repeat(jnp.transpose(v, (0, 2, 1)), HEAD_DIM, axis=2)

    vec = jnp.stack([
        lanes(jnp.exp(lg_f[:, None] * (idx + 1.0)[None])),
        lanes(jnp.exp(lg_f[:, None] * (c - 1.0 - idx)[None])),
        lanes(jnp.exp(lg_b[:, None] * (c - idx)[None])),
        lanes(jnp.exp(lg_b[:, None] * idx[None])),
    ], axis=1)
    same_head = (jnp.arange(PAIR)[:, None] // HEAD_DIM) == (jnp.arange(PAIR)[None, :] // HEAD_DIM)
    cd = jnp.stack([lanes(jnp.exp(lg_f * c)[:, None]), lanes(jnp.exp(lg_b * c)[:, None])], axis=1)
    cd = jnp.transpose(cd, (0, 1, 3, 2)) * same_head[None, None].astype(F32)
    return dmat, vec, cd


def _retention_kernel(q_ref, k_ref, v_ref, g_ref, dmat_ref, vec_ref, cd_ref, o_ref, kv_ref, st_ref):
    s = q_ref.shape[0]
    c = RET_CHUNK
    nc = s // c
    lane = lax.broadcasted_iota(jnp.int32, (1, PAIR), 1)
    head0 = lane < HEAD_DIM
    same_head = ((lax.broadcasted_iota(jnp.int32, (PAIR, PAIR), 0) // HEAD_DIM)
                 == (lax.broadcasted_iota(jnp.int32, (PAIR, PAIR), 1) // HEAD_DIM))

    same_head2 = jnp.concatenate([same_head, same_head], axis=0)
    for n in range(nc):
        rows = slice(n * c, (n + 1) * c)
        kf = k_ref[rows, :].astype(F32)
        kcat = jnp.concatenate([(kf * vec_ref[1]).astype(BF16), (kf * vec_ref[3]).astype(BF16)], axis=1)
        kv = lax.dot_general(kcat, v_ref[rows, :], (((0,), (0,)), ((), ())), preferred_element_type=F32)
        kv_ref[n] = jnp.where(same_head2, kv, 0.0)

    state = jnp.zeros((PAIR, PAIR), F32)
    for n in range(nc):
        st_ref[n, :PAIR, :] = state.astype(BF16)
        state = state * cd_ref[0] + kv_ref[n, :PAIR, :]
    state = jnp.zeros((PAIR, PAIR), F32)
    for n in reversed(range(nc)):
        st_ref[n, PAIR:, :] = state.astype(BF16)
        state = state * cd_ref[1] + kv_ref[n, PAIR:, :]

    for n in range(nc):
        rows = slice(n * c, (n + 1) * c)
        q, k, v = q_ref[rows, :], k_ref[rows, :], v_ref[rows, :]
        q2 = jnp.concatenate([jnp.where(head0, q, jnp.zeros_like(q)),
                              jnp.where(head0, jnp.zeros_like(q), q)], axis=0)
        sc = lax.dot_general(q2, k, (((1,), (1,)), ((), ())), preferred_element_type=F32)
        intra = jnp.dot((sc * dmat_ref[...]).astype(BF16), v, preferred_element_type=F32)
        qf = q.astype(F32)
        qcat = jnp.concatenate([(qf * vec_ref[0]).astype(BF16), (qf * vec_ref[2]).astype(BF16)], axis=1)
        y = jnp.where(head0, intra[:c], intra[c:]) + jnp.dot(qcat, st_ref[n], preferred_element_type=F32)
        y2 = y * y
        ms0 = jnp.sum(jnp.where(head0, y2, 0.0), axis=-1, keepdims=True)
        ms1 = jnp.sum(jnp.where(head0, 0.0, y2), axis=-1, keepdims=True)
        ms = jnp.where(head0, ms0, ms1) * (1.0 / HEAD_DIM)
        g = g_ref[rows, :].astype(F32)
        o_ref[rows, :] = (y * lax.rsqrt(ms + EPS) * (g * jax.nn.sigmoid(g))).astype(o_ref.dtype)


def _mixers(qkv, qkv_grouped, bias_rows, ret_inputs, ret_tables, weights_f32):
    b, s, _ = qkv[0].shape
    c = RET_CHUNK
    n_steps = N_PAIRS * b
    spec = pl.BlockSpec((None, s, PAIR), lambda hp, bi: (bi, 0, hp))
    gspecs = [pl.BlockSpec((None, g.shape[1], None) + g.shape[3:], lambda hp, bi: (bi, 0, hp, 0, 0, 0))
              for g in qkv_grouped]
    slabs = [w.reshape(n_steps, -1, w.shape[-1]) for w in weights_f32]
    slab_specs = [pl.BlockSpec((None,) + w.shape[1:], lambda hp, bi: (hp * b + bi, 0, 0)) for w in slabs]
    n_attn_in, n_ret_in, n_w = 10, 7, len(slabs)
    attn_scratch = [
        pltpu.VMEM((N_BIAS_VARIANTS, 2 * ATTN_TQ, ATTN_W), F32),
        pltpu.VMEM((3, s, PAIR), F32),
        pltpu.VMEM((3, s, PAIR), F32),
        pltpu.VMEM((3, s, PAIR), F32),
        pltpu.VMEM((s, PAIR), F32),
    ]
    ret_scratch = [pltpu.VMEM((s // c, 2 * PAIR, PAIR), F32),
                   pltpu.VMEM((s // c, 2 * PAIR, PAIR), BF16)]

    def kernel(*refs):
        ins, rest = refs[:n_attn_in + n_ret_in + n_w], refs[n_attn_in + n_ret_in + n_w:]
        attn_o, ret_o, w_out, scratch = rest[0], rest[1], rest[2:2 + n_w], rest[2 + n_w:]
        _attention_kernel(*ins[:n_attn_in], attn_o, *scratch[:len(attn_scratch)])
        _retention_kernel(*ins[n_attn_in:n_attn_in + n_ret_in], ret_o, *scratch[len(attn_scratch):])
        for src, dst in zip(ins[n_attn_in + n_ret_in:], w_out):
            dst[...] = src[...].astype(BF16)

    outs = pl.pallas_call(
        kernel,
        grid=(N_PAIRS, b),
        in_specs=[spec] * 3 + gspecs
        + [pl.BlockSpec((None, 2 * N_BIAS_VARIANTS, 2 * ATTN_W), lambda hp, bi: (hp, 0, 0))]
        + [spec] * 4
        + [pl.BlockSpec((None, 2 * c, c), lambda hp, bi: (hp, 0, 0)),
           pl.BlockSpec((None, 4, c, PAIR), lambda hp, bi: (hp, 0, 0, 0)),
           pl.BlockSpec((None, 2, PAIR, PAIR), lambda hp, bi: (hp, 0, 0, 0))]
        + slab_specs,
        out_specs=[spec, spec] + slab_specs,
        out_shape=[jax.ShapeDtypeStruct((b, s, ATTN_WIDTH), BF16), jax.ShapeDtypeStruct((b, s, RET_WIDTH), BF16)]
        + [jax.ShapeDtypeStruct(w.shape, BF16) for w in slabs],
        scratch_shapes=attn_scratch + ret_scratch,
        compiler_params=pltpu.CompilerParams(
            dimension_semantics=("arbitrary", "arbitrary"), vmem_limit_bytes=VMEM_LIMIT),
        name="mixers",
    )(*qkv, *qkv_grouped, bias_rows, *ret_inputs, *ret_tables, *slabs)
    return outs[0], outs[1], [o.reshape(w.shape) for o, w in zip(outs[2:], weights_f32)]


def _route_tile(sub, attn_ref, ret_ref, x_ref, wo_ref, again_ref, fgain_ref,
                rw_ref, rb_ref, h_ref, xs_ref, dest_ref, cnt_ref):
    tm = ROUTE_TM
    rows = slice(sub * tm, (sub + 1) * tm)
    a = _rms(attn_ref[rows, :].astype(F32), again_ref[...]).astype(BF16)
    mixed = jnp.concatenate([a, ret_ref[rows, :]], axis=1)
    h = x_ref[rows, :] + jnp.dot(mixed, wo_ref[...], preferred_element_type=F32)
    h_ref[rows, :] = h.astype(h_ref.dtype)
    hn = _rms(h, fgain_ref[...]).astype(BF16)

    logits = jnp.dot(hn, rw_ref[...], preferred_element_type=F32) + rb_ref[...]
    lt = logits.T
    row = lambda i: lt[i:i + 1, :]

    def top1(vals):
        top = functools.reduce(jnp.maximum, vals)
        idx = jnp.full(top.shape, len(vals) - 1, jnp.int32)
        for i in reversed(range(len(vals) - 1)):
            idx = jnp.where(vals[i] == top, i, idx)
        return top, idx

    g_rows = [row(N_EXPERTS + g) for g in range(N_GROUPS)]
    gmax, grp = top1(g_rows)
    p_group = 1.0 / sum(jnp.exp(g - gmax) for g in g_rows)
    chosen = []
    for j in range(EXPERTS_PER_GROUP):
        e = row((N_GROUPS - 1) * EXPERTS_PER_GROUP + j)
        for g in reversed(range(N_GROUPS - 1)):
            e = jnp.where(grp == g, row(g * EXPERTS_PER_GROUP + j), e)
        chosen.append(e)
    v1, i1 = top1(chosen)
    v2, i2 = top1([jnp.where(i1 == j, -jnp.inf, e) for j, e in enumerate(chosen)])
    e2 = jnp.exp(v2 - v1)
    p1 = p_group / (1.0 + e2)
    inner = [jnp.where(i1 == j, p1, jnp.where(i2 == j, p1 * e2, 0.0)) for j in range(EXPERTS_PER_GROUP)]
    gate_rows = [jnp.where(grp == g, inner[j], 0.0)
                 for g in range(N_GROUPS) for j in range(EXPERTS_PER_GROUP)]

    sub8 = lax.broadcasted_iota(jnp.int32, (8, 1), 0)
    onehot_t = jnp.where(sub8 == grp, 1.0, 0.0)
    r_i = lax.broadcasted_iota(jnp.int32, (tm, tm), 0)
    c_i = lax.broadcasted_iota(jnp.int32, (tm, tm), 1)
    before = jnp.where(c_i < r_i, 1.0, 0.0).astype(BF16)
    rank_t = lax.dot_general(onehot_t.astype(BF16), before, (((1,), (1,)), ((), ())),
                             preferred_element_type=F32)
    cnt = jnp.sum(onehot_t, axis=1, keepdims=True)
    c16 = jnp.ceil(cnt * (1.0 / CHUNK)) * CHUNK
    start = jnp.zeros((8, 1), F32)
    for g in range(N_GROUPS - 1):
        start = start + jnp.where(sub8 > g, c16[g:g + 1, :], 0.0)
    dest_t = jnp.sum(onehot_t * (start + rank_t), axis=0, keepdims=True)
    cnt_ref[sub] = jnp.broadcast_to(jnp.where(sub8 < N_GROUPS, c16, pltpu.roll(start, N_GROUPS, 0)),
                                    (8, LANES)).astype(jnp.int32)

    cols = jnp.concatenate(gate_rows + [dest_t, jnp.zeros((LANES - N_EXPERTS - 1, tm), F32)], axis=0).T
    dest_ref[rows, :] = cols
    lane = lax.broadcasted_iota(jnp.int32, cols.shape, 1)
    gates = jnp.where(lane < N_EXPERTS, cols, 0.0)
    perm = jnp.where(lax.broadcasted_iota(jnp.int32, (SORT_ROWS, tm), 0) == dest_t.astype(jnp.int32),
                     1.0, 0.0).astype(BF16)
    g_hi = gates.astype(BF16)
    g_lo = (gates - g_hi.astype(F32)).astype(BF16)
    payload = jnp.concatenate([hn, g_hi, g_lo], axis=1)
    xs_ref[sub * SORT_ROWS:(sub + 1) * SORT_ROWS, :] = (
        jnp.dot(perm, payload, preferred_element_type=F32).astype(BF16))


def _outproj_kernel(*refs):
    for sub in range(ROUTE_TILES_PER_STEP):
        _route_tile(sub, *refs)


def _outproj(attn, ret, x, w_o, again, fgain, rw, rb):
    t, d = x.shape
    per_step = ROUTE_TILES_PER_STEP
    tm = ROUTE_TM * per_step
    nt = t // ROUTE_TM
    row = lambda w: pl.BlockSpec((tm, w), lambda i: (i, 0))
    full = lambda r, c: pl.BlockSpec((r, c), lambda i: (0, 0))
    return pl.pallas_call(
        _outproj_kernel,
        grid=(nt // per_step,),
        in_specs=[row(ATTN_WIDTH), row(RET_WIDTH), row(d), full(ATTN_WIDTH + RET_WIDTH, d),
                  full(1, ATTN_WIDTH), full(1, d), full(d, LANES), full(1, LANES)],
        out_specs=[row(d), pl.BlockSpec((per_step * SORT_ROWS, XS_W), lambda i: (i, 0)), row(LANES),
                   pl.BlockSpec((per_step, 8, LANES), lambda i: (i, 0, 0))],
        out_shape=[jax.ShapeDtypeStruct((t, d), BF16),
                   jax.ShapeDtypeStruct((nt * SORT_ROWS, XS_W), BF16),
                   jax.ShapeDtypeStruct((t, LANES), F32),
                   jax.ShapeDtypeStruct((nt, 8, LANES), jnp.int32)],
        compiler_params=pltpu.CompilerParams(
            dimension_semantics=("arbitrary",), vmem_limit_bytes=VMEM_LIMIT),
        name="outproj",
    )(attn, ret, x, w_o, again, fgain, rw, rb)


def _dispatch_tables(cnt, n_moe_tiles):
    nt = cnt.shape[0]
    cpt = MOE_TM // CHUNK

    n_slots = n_moe_tiles * cpt
    slot_rows = -(-n_slots // (8 * LANES)) * 8

    def schedule_kernel(seg_ref, grp_ref, used_ref, src_ref, tail_ref):
        slot = (lax.broadcasted_iota(jnp.int32, (slot_rows, LANES), 0) * LANES
                + lax.broadcasted_iota(jnp.int32, (slot_rows, LANES), 1))
        src = jnp.full((slot_rows, LANES), -1, jnp.int32)
        pos = jnp.int32(0)
        for g in range(N_GROUPS):
            def tile_body(i, carry):
                p, src = carry
                n = seg_ref[i, g] // CHUNK
                first = (i * SORT_ROWS + seg_ref[i, N_GROUPS + g]) // CHUNK
                src = jnp.where((slot >= p) & (slot < p + n), slot + (first - p), src)
                return p + n, src
            end, src = lax.fori_loop(0, nt, tile_body, (pos, src))
            padded = ((end + cpt - 1) // cpt) * cpt

            def mark_body(m, carry):
                grp_ref[m] = g
                used_ref[m] = 1
                return carry
            lax.fori_loop(pos // cpt, padded // cpt, mark_body, 0)
            pos = padded

        def idle_body(m, carry):
            grp_ref[m] = N_GROUPS - 1
            used_ref[m] = 0
            return carry
        lax.fori_loop(pos // cpt, n_moe_tiles, idle_body, 0)
        src_ref[...] = src

        def tail_body(i, carry):
            rows = seg_ref[i, 0]
            for g in range(1, N_GROUPS):
                rows = rows + seg_ref[i, g]
            tail_ref[i] = rows // CHUNK
            return carry
        lax.fori_loop(0, nt, tail_body, 0)

    smem = lambda: pl.BlockSpec(memory_space=pltpu.SMEM)
    tile_group, tile_used, src, tile_tail = pl.pallas_call(
        schedule_kernel,
        in_specs=[smem()],
        out_specs=[smem(), smem(), pl.BlockSpec(memory_space=pltpu.VMEM), smem()],
        out_shape=[jax.ShapeDtypeStruct((n_moe_tiles,), jnp.int32),
                   jax.ShapeDtypeStruct((n_moe_tiles,), jnp.int32),
                   jax.ShapeDtypeStruct((slot_rows, LANES), jnp.int32),
                   jax.ShapeDtypeStruct((nt,), jnp.int32)],
        name="schedule",
    )(cnt[:, :2 * N_GROUPS, 0])
    return tile_group, tile_used, src.reshape(-1)[:n_slots], tile_tail


def _moe_kernel(grp_ref, used_ref, src_ref, tail_ref, xs_hbm, w1_ref, w3_ref, w2_ref, ys_hbm,
                xbuf, obuf, zbuf, in_sem, out_sem, zero_sem):
    m = pl.program_id(0)
    n_tiles = pl.num_programs(0)
    cpt = MOE_TM // CHUNK
    cps = SORT_ROWS // CHUNK
    slot = m % 2
    zero_chunk = cps - 1

    def rows(c):
        return pl.ds(pl.multiple_of(c * CHUNK, CHUNK), CHUNK)

    def zero_tails(wait):
        def tile_body(i, carry):
            def body(c, carry2):
                cp = pltpu.make_async_copy(zbuf, ys_hbm.at[rows(i * cps + c), :], zero_sem)
                cp.wait() if wait else cp.start()
                return carry2
            lax.fori_loop(tail_ref[i], cps, body, 0)
            return carry
        lax.fori_loop(0, tail_ref.shape[0], tile_body, 0)

    def gather_start(tile, sl):
        for c in range(cpt):
            src = src_ref[tile * cpt + c]
            src = jnp.where(src < 0, zero_chunk, src)
            pltpu.make_async_copy(xs_hbm.at[rows(src), :], xbuf.at[sl, c * CHUNK:(c + 1) * CHUNK, :],
                                  in_sem.at[sl]).start()

    def gather_wait(sl):
        pltpu.make_async_copy(xs_hbm.at[0:MOE_TM, :], xbuf.at[sl], in_sem.at[sl]).wait()

    def scatter(tile, sl, wait):
        full = src_ref[tile * cpt + cpt - 1] >= 0

        def chunk_copy(c, src):
            return pltpu.make_async_copy(obuf.at[sl, c * CHUNK:(c + 1) * CHUNK, :], ys_hbm.at[rows(src), :],
                                         out_sem.at[sl])

        @pl.when(full)
        def _():
            if wait:
                pltpu.make_async_copy(obuf.at[sl], ys_hbm.at[0:MOE_TM, :], out_sem.at[sl]).wait()
            else:
                for c in range(cpt):
                    chunk_copy(c, src_ref[tile * cpt + c]).start()

        @pl.when(jnp.logical_not(full))
        def _():
            for c in range(cpt):
                src = src_ref[tile * cpt + c]

                @pl.when(src >= 0)
                def _():
                    cp = chunk_copy(c, src)
                    cp.wait() if wait else cp.start()

    @pl.when(m == 0)
    def _():
        gather_start(0, 0)
        zbuf[...] = jnp.zeros(zbuf.shape, zbuf.dtype)
        zero_tails(False)

    prev_used = used_ref[jnp.maximum(m - 1, 0)] > 0

    @pl.when((m == 0) | prev_used)
    def _():
        gather_wait(slot)

    @pl.when((m >= 2) & (used_ref[jnp.maximum(m - 2, 0)] > 0))
    def _():
        scatter(m - 2, slot, True)

    def expert_tile(n_rows):
        gather_start(jnp.minimum(m + 1, n_tiles - 1), 1 - slot)
        x = xbuf[slot, :n_rows, :D_MODEL]
        gate = (xbuf[slot, :n_rows, D_MODEL:D_MODEL + LANES].astype(F32)
                + xbuf[slot, :n_rows, D_MODEL + LANES:].astype(F32))
        lane = lax.broadcasted_iota(jnp.int32, gate.shape, 1)
        base = grp_ref[m] * EXPERTS_PER_GROUP
        acc = jnp.zeros((n_rows, D_MODEL), F32)
        for j in range(EXPERTS_PER_GROUP):
            a = jnp.dot(x, w1_ref[j], preferred_element_type=F32)
            b = jnp.dot(x, w3_ref[j], preferred_element_type=F32)
            gj = jnp.sum(jnp.where(lane == base + j, gate, 0.0), axis=-1, keepdims=True)
            hid = (a * jax.nn.sigmoid(a) * b * gj).astype(BF16)
            acc = acc + jnp.dot(hid, w2_ref[j], preferred_element_type=F32)
        obuf[slot, :n_rows, :] = acc.astype(BF16)
        scatter(m, slot, False)

    half_empty = src_ref[m * cpt + cpt // 2] < 0

    @pl.when((used_ref[m] > 0) & jnp.logical_not(half_empty))
    def _():
        expert_tile(MOE_TM)

    @pl.when((used_ref[m] > 0) & half_empty)
    def _():
        expert_tile(MOE_TM // 2)

    @pl.when(m == n_tiles - 1)
    def _():
        zero_tails(True)

        @pl.when(used_ref[m] > 0)
        def _():
            gather_wait(1 - slot)
            scatter(m, slot, True)

        @pl.when((m >= 1) & prev_used)
        def _():
            scatter(m - 1, 1 - slot, True)


def _moe(xs, w1, w3, w2, tile_group, tile_used, src_chunk, tile_tail):
    n_moe_tiles = tile_group.shape[0]
    rows = xs.shape[0]
    d = D_MODEL
    wspec = lambda r, c: pl.BlockSpec((EXPERTS_PER_GROUP, r, c), lambda m, grp, *_: (grp[m], 0, 0))
    return pl.pallas_call(
        _moe_kernel,
        grid_spec=pltpu.PrefetchScalarGridSpec(
            num_scalar_prefetch=4,
            grid=(n_moe_tiles,),
            in_specs=[pl.BlockSpec(memory_space=pl.ANY),
                      wspec(d, EXPERT_FF), wspec(d, EXPERT_FF), wspec(EXPERT_FF, d)],
            out_specs=pl.BlockSpec(memory_space=pl.ANY),
            scratch_shapes=[pltpu.VMEM((2, MOE_TM, XS_W), BF16),
                            pltpu.VMEM((2, MOE_TM, d), BF16),
                            pltpu.VMEM((CHUNK, d), BF16),
                            pltpu.SemaphoreType.DMA((2,)),
                            pltpu.SemaphoreType.DMA((2,)),
                            pltpu.SemaphoreType.DMA(())]),
        out_shape=jax.ShapeDtypeStruct((rows, d), BF16),
        compiler_params=pltpu.CompilerParams(
            dimension_semantics=("arbitrary",), vmem_limit_bytes=VMEM_LIMIT),
        name="moe",
    )(tile_group, tile_used, src_chunk, tile_tail, xs, w1, w3, w2)


def _combine_kernel(ys_ref, h_ref, dest_ref, gain_ref, o_ref):
    tm = ROUTE_TM
    for sub in range(h_ref.shape[0] // tm):
        rows = slice(sub * tm, (sub + 1) * tm)
        dest = dest_ref[rows, N_EXPERTS:N_EXPERTS + 1].astype(jnp.int32)
        perm_t = jnp.where(lax.broadcasted_iota(jnp.int32, (tm, SORT_ROWS), 1) == dest, 1.0, 0.0).astype(BF16)
        moe = jnp.dot(perm_t, ys_ref[sub * SORT_ROWS:(sub + 1) * SORT_ROWS, :], preferred_element_type=F32)
        o_ref[rows, :] = _rms(h_ref[rows, :].astype(F32) + moe, gain_ref[...])


def _combine(ys, h, dest, gain):
    t, d = h.shape
    per_step = COMBINE_TILES_PER_STEP
    tm = ROUTE_TM * per_step
    return pl.pallas_call(
        _combine_kernel,
        grid=(t // tm,),
        in_specs=[pl.BlockSpec((per_step * SORT_ROWS, d), lambda i: (i, 0)),
                  pl.BlockSpec((tm, d), lambda i: (i, 0)),
                  pl.BlockSpec((tm, LANES), lambda i: (i, 0)),
                  pl.BlockSpec((1, d), lambda i: (0, 0))],
        out_specs=pl.BlockSpec((tm, d), lambda i: (i, 0)),
        out_shape=jax.ShapeDtypeStruct((t, d), F32),
        compiler_params=pltpu.CompilerParams(
            dimension_semantics=("arbitrary",), vmem_limit_bytes=VMEM_LIMIT),
        name="combine",
    )(ys, h, dest, gain)


def _rotary_tables(s):
    half = HEAD_DIM // 2
    inv = ROPE_BASE ** (-jnp.arange(half, dtype=F32) / half)
    ang = jnp.arange(s, dtype=F32)[:, None] * inv[None, :]
    cos, sin = jnp.cos(ang), jnp.sin(ang)
    cos_t = jnp.tile(jnp.concatenate([cos, cos], axis=-1), (1, LANES // HEAD_DIM))
    sin_t = jnp.tile(jnp.concatenate([-sin, sin], axis=-1), (1, LANES // HEAD_DIM))
    return cos_t, sin_t


def kernel(x, w_in, w_out, norm_mix, norm_ffn, norm_final, attn_out_gain, rel_bias, ret_decay_fwd, ret_decay_bwd, router_group_w, router_group_b, router_expert_w, router_expert_b, expert_w1, expert_w3, expert_w2):
    b, s, d = x.shape
    depth = w_in.shape[0]
    cos_t, sin_t = _rotary_tables(s)
    bias_rows = _attn_bias_rows(rel_bias)
    h = x
    for layer in range(depth):
        (aq, ak, av, rq, rk, rv, rg), qkv_grouped = _inproj(
            h, norm_mix[layer][None], w_in[layer].astype(BF16), cos_t, sin_t)
        attn, ret, (w1, w3, w2, w_o) = _mixers(
            (aq, ak, av), qkv_grouped, bias_rows, (rq, rk, rv, rg),
            _retention_tables(ret_decay_fwd[layer], ret_decay_bwd[layer]),
            (expert_w1[layer], expert_w3[layer], expert_w2[layer], w_out[layer]))

        rw = jnp.concatenate(
            [jnp.transpose(router_expert_w[layer], (1, 0, 2)).reshape(d, N_EXPERTS),
             router_group_w[layer],
             jnp.zeros((d, LANES - N_EXPERTS - N_GROUPS), F32)], axis=1).astype(BF16)
        rb = jnp.concatenate(
            [router_expert_b[layer].reshape(N_EXPERTS), router_group_b[layer],
             jnp.zeros((LANES - N_EXPERTS - N_GROUPS,), F32)])[None].astype(F32)
        h1, xs, dest, cnt = _outproj(
            attn.reshape(b * s, ATTN_WIDTH), ret.reshape(b * s, RET_WIDTH), h.reshape(b * s, d),
            w_o, attn_out_gain[layer][None], norm_ffn[layer][None],
            rw, rb)
        n_route_tiles = (b * s) // ROUTE_TM
        n_moe_tiles = (b * s + n_route_tiles * N_GROUPS * (CHUNK - 1)) // MOE_TM + N_GROUPS
        schedule = _dispatch_tables(cnt, n_moe_tiles)
        ys = _moe(xs, w1, w3, w2, *schedule)
        assert depth == 1, "the combine kernel fuses the final norm, so it must run on the last layer"
        h = _combine(ys, h1, dest, norm_final[None]).reshape(b, s, d)
    return h
```

```python
import functools
import math

import jax
import jax.numpy as jnp
from jax import lax
from jax.experimental import pallas as pl
from jax.experimental.pallas import tpu as pltpu

F32 = jnp.float32
BF16 = jnp.bfloat16

D_MODEL = 1024
HEAD_DIM = 64
ATTN_WIDTH = 512
RET_WIDTH = 512
N_HEADS = 8
PAIR = 2 * HEAD_DIM
N_PAIRS = N_HEADS // 2
ATTN_DILATIONS = (1, 4, 16)
ATTN_RADIUS = 64
N_BUCKETS = 32
REL_MAX_DIST = 1024
ROPE_BASE = 10000.0
N_GROUPS = 4
EXPERTS_PER_GROUP = 4
N_EXPERTS = 16
EXPERT_FF = 512
EPS = 1e-6
NEG_INF = -1e30
LOG2_E = math.log2(math.e)

LANES = 128
ATTN_TQ = 128
ATTN_W = 256
N_BIAS_VARIANTS = 8
RET_CHUNK = 256
ROUTE_TM = 512
ROUTE_TILES_PER_STEP = 2
COMBINE_TILES_PER_STEP = 4
SORT_ROWS = 592
CHUNK = 16
MOE_TM = 512
XS_W = D_MODEL + 2 * LANES
VMEM_LIMIT = 48 * 1024 * 1024


def _rms(x, gain):
    return x * lax.rsqrt(jnp.mean(x * x, axis=-1, keepdims=True) + EPS) * gain


def _inproj_kernel(x_ref, gain_ref, w_ref, cos_ref, sin_ref,
                   aq_ref, ak_ref, av_ref, rq_ref, rk_ref, rv_ref, rg_ref,
                   aq4_ref, ak4_ref, av4_ref, aq16_ref, ak16_ref, av16_ref, stage_ref, stage4_ref):
    tm = x_ref.shape[0]
    xn = _rms(x_ref[...], gain_ref[...]).astype(BF16)

    def stage(i, t, nat_ref):
        nat_ref[...] = t.astype(BF16)
        for hp in range(N_PAIRS):
            stage_ref[i, hp] = t[:, hp * PAIR:(hp + 1) * PAIR]

    def regroup4(i, d4_ref):
        for hp in range(N_PAIRS):
            for r4 in range(4):
                g4 = stage_ref[i, hp, pl.ds(r4, tm // 4, stride=4), :]
                d4_ref[hp, r4] = g4.astype(BF16)
                stage4_ref[i, hp, r4] = g4

    def regroup16(i, d16_ref):
        for hp in range(N_PAIRS):
            for r4 in range(4):
                for j in range(4):
                    d16_ref[hp, r4 + 4 * j] = (
                        stage4_ref[i, hp, r4, pl.ds(j, tm // 16, stride=4), :].astype(BF16))

    def seg(i):
        return jnp.dot(xn, w_ref[:, i * ATTN_WIDTH:(i + 1) * ATTN_WIDTH], preferred_element_type=F32)

    def rotary(t):
        cos, sin = cos_ref[...], sin_ref[...]
        first_half = (lax.broadcasted_iota(jnp.int32, (1, LANES), 1) % HEAD_DIM) < HEAD_DIM // 2
        outs = []
        for j in range(t.shape[1] // LANES):
            tj = t[:, j * LANES:(j + 1) * LANES]
            partner = jnp.where(first_half, pltpu.roll(tj, LANES - HEAD_DIM // 2, 1),
                                pltpu.roll(tj, HEAD_DIM // 2, 1))
            outs.append(tj * cos + partner * sin)
        return jnp.concatenate(outs, axis=1)

    stage(0, seg(0) * (HEAD_DIM ** -0.5 * LOG2_E), aq_ref)
    stage(1, seg(1), ak_ref)
    stage(2, seg(2), av_ref)
    rq_ref[...] = rotary(seg(3)).astype(BF16)
    for i, d4_ref in enumerate((aq4_ref, ak4_ref, av4_ref)):
        regroup4(i, d4_ref)
    rk_ref[...] = (rotary(seg(4)) * (HEAD_DIM ** -0.5)).astype(BF16)
    rv_ref[...] = seg(5).astype(BF16)
    for i, d16_ref in enumerate((aq16_ref, ak16_ref, av16_ref)):
        regroup16(i, d16_ref)
    rg_ref[...] = seg(6).astype(BF16)


def _inproj(x, gain, w_in, cos_t, sin_t, tm=512):
    b, s, d = x.shape
    n = w_in.shape[1]
    out = jax.ShapeDtypeStruct((b, s, ATTN_WIDTH), BF16)
    ospec = pl.BlockSpec((None, tm, ATTN_WIDTH), lambda si, bi: (bi, si, 0))

    def grouped(dil):
        shape = jax.ShapeDtypeStruct((b, s // tm, N_PAIRS, dil, tm // dil, PAIR), BF16)
        spec = pl.BlockSpec((None, None, N_PAIRS, dil, tm // dil, PAIR), lambda si, bi: (bi, si, 0, 0, 0, 0))
        return [shape] * 3, [spec] * 3

    shapes4, specs4 = grouped(4)
    shapes16, specs16 = grouped(16)
    outs = pl.pallas_call(
        _inproj_kernel,
        grid=(s // tm, b),
        in_specs=[
            pl.BlockSpec((None, tm, d), lambda si, bi: (bi, si, 0)),
            pl.BlockSpec((1, d), lambda si, bi: (0, 0)),
            pl.BlockSpec((d, n), lambda si, bi: (0, 0)),
            pl.BlockSpec((tm, LANES), lambda si, bi: (si, 0)),
            pl.BlockSpec((tm, LANES), lambda si, bi: (si, 0)),
        ],
        out_specs=[ospec] * 7 + specs4 + specs16,
        out_shape=[out] * 7 + shapes4 + shapes16,
        scratch_shapes=[pltpu.VMEM((3, N_PAIRS, tm, PAIR), F32),
                        pltpu.VMEM((3, N_PAIRS, 4, tm // 4, PAIR), F32)],
        compiler_params=pltpu.CompilerParams(
            dimension_semantics=("arbitrary", "arbitrary"), vmem_limit_bytes=VMEM_LIMIT),
        name="inproj",
    )(x, gain, w_in, cos_t, sin_t)
    return outs[:7], outs[7:]


def _t5_bucket(rel):
    half = N_BUCKETS // 2
    max_exact = half // 2
    offset = jnp.where(rel > 0, half, 0)
    n = jnp.abs(rel)
    nf = jnp.maximum(n, 1).astype(F32)
    large = max_exact + (jnp.log(nf / max_exact) / math.log(REL_MAX_DIST / max_exact)
                         * (half - max_exact)).astype(jnp.int32)
    large = jnp.minimum(large, half - 1)
    return offset + jnp.where(n < max_exact, n, large)


def _attn_bias_rows(rel_bias):
    period = 2 * ATTN_W
    band = 2 * ATTN_RADIUS + 1
    rel = jnp.arange(-ATTN_RADIUS, ATTN_RADIUS + 1)
    rows = []
    for dil, offs in ((1, (0, 64, 128)), (4, (0, 64, 128)), (16, (0, 128))):
        vals = rel_bias[_t5_bucket(rel * dil)].astype(F32).T * LOG2_E
        for off in offs:
            lo = off - ATTN_RADIUS
            pad = jnp.full((N_HEADS, period - band), NEG_INF, F32)
            if lo >= 0:
                row = jnp.concatenate([pad[:, :lo], vals, pad[:, lo:]], axis=1)
            else:
                row = jnp.concatenate([vals[:, -lo:], pad, vals[:, :-lo]], axis=1)
            rows.append(row)
    v = jnp.stack(rows, axis=1)
    return v.reshape(N_PAIRS, 2 * N_BIAS_VARIANTS, period)


def _attention_kernel(q_ref, k_ref, v_ref, q4_ref, k4_ref, v4_ref, q16_ref, k16_ref, v16_ref,
                      rows_ref, o_ref, bias_ref, acc_ref, m_ref, l_ref, out_ref):
    s = q_ref.shape[0]
    n_tiles = s // ATTN_TQ
    lane = lax.broadcasted_iota(jnp.int32, (1, PAIR), 1)
    head0 = lane < HEAD_DIM

    @pl.when(pl.program_id(1) == 0)
    def _():
        col = lax.broadcasted_iota(jnp.int32, (ATTN_TQ, ATTN_W), 1)
        for idx in range(2 * N_BIAS_VARIANTS):
            gen = jnp.broadcast_to(rows_ref[idx:idx + 1, :], (ATTN_TQ, 2 * ATTN_W))
            tab = pltpu.roll(gen, 0, 1, stride=1, stride_axis=0)[:, :ATTN_W]
            var = idx % N_BIAS_VARIANTS
            if var >= 6:
                tab = jnp.where((col // ATTN_TQ) == var - 6, tab, NEG_INF)
            head = idx // N_BIAS_VARIANTS
            bias_ref[var, head * ATTN_TQ:(head + 1) * ATTN_TQ, :] = tab

    def rows_of(ref, dil, lo, n):
        if dil == 1:
            return ref[lo:lo + n, :]
        piece, sub_len = ref.shape[2], s // dil
        parts, pos = [], lo
        while pos < lo + n:
            within = pos % sub_len
            off = within % piece
            take = min(piece - off, lo + n - pos)
            parts.append(ref[within // piece, pos // sub_len, off:off + take, :])
            pos += take
        return parts[0] if len(parts) == 1 else jnp.concatenate(parts, axis=0)

    def run_branch(bi, dil, qs_ref, ks_ref, vs_ref):
        sub_len = s // dil
        tiles_per_sub = sub_len // ATTN_TQ

        def tile(t):
            q0 = t * ATTN_TQ
            if tiles_per_sub == 1:
                ws = (t // 2) * ATTN_W
                var = 6 + t % 2
            else:
                pos = t % tiles_per_sub
                sub_lo = (t // tiles_per_sub) * sub_len
                ws = min(max(q0 - ATTN_RADIUS, sub_lo), sub_lo + sub_len - ATTN_W)
                var = (0 if pos == 0 else 2 if pos == tiles_per_sub - 1 else 1) + 3 * bi
            q = rows_of(qs_ref, dil, q0, ATTN_TQ)
            k = rows_of(ks_ref, dil, ws, ATTN_W)
            v = rows_of(vs_ref, dil, ws, ATTN_W)
            q2 = jnp.concatenate([jnp.where(head0, q, jnp.zeros_like(q)),
                                  jnp.where(head0, jnp.zeros_like(q), q)], axis=0)
            sc = lax.dot_general(q2, k, (((1,), (1,)), ((), ())), preferred_element_type=F32)
            sc = sc + bias_ref[var]
            m = jnp.max(sc, axis=-1, keepdims=True)
            p = jnp.exp2(sc - m).astype(BF16)
            o = jnp.dot(p, jnp.concatenate([v, jnp.ones_like(v)], axis=1), preferred_element_type=F32)
            l = o[:, PAIR:]
            outs, ms, ls = (o[:ATTN_TQ, :PAIR], o[ATTN_TQ:, :PAIR]), (m[:ATTN_TQ], m[ATTN_TQ:]), (l[:ATTN_TQ], l[ATTN_TQ:])
            if dil == 16:
                dst = pl.ds((t % 4) * (s // 4) + t // 4, ATTN_TQ, stride=4)
            else:
                dst = pl.ds(q0, ATTN_TQ)
            acc_ref[bi, dst, :] = jnp.where(head0, outs[0], outs[1])
            m_ref[bi, dst, :] = jnp.where(head0, ms[0], ms[1])
            l_ref[bi, dst, :] = jnp.where(head0, ls[0], ls[1])

        for t in range(n_tiles):
            tile(t)

    run_branch(0, 1, q_ref, k_ref, v_ref)
    run_branch(1, 4, q4_ref, k4_ref, v4_ref)
    run_branch(2, 16, q16_ref, k16_ref, v16_ref)

    rows = ATTN_TQ
    for r4 in range(4):
        for blk in range(s // (4 * rows)):
            nat = pl.ds(r4 + 4 * rows * blk, rows, stride=4)
            grouped4 = pl.ds(r4 * (s // 4) + rows * blk, rows)
            sl = (nat, grouped4, grouped4)
            m = [m_ref[bi, sl[bi], :] for bi in range(3)]
            mx = jnp.maximum(jnp.maximum(m[0], m[1]), m[2])
            num = jnp.zeros((rows, PAIR), F32)
            den = jnp.zeros((rows, PAIR), F32)
            for bi in range(3):
                e = jnp.exp2(m[bi] - mx)
                num = num + e * acc_ref[bi, sl[bi], :]
                den = den + e * l_ref[bi, sl[bi], :]
            out_ref[nat, :] = num / den
    o_ref[...] = out_ref[...].astype(o_ref.dtype)


def _retention_tables(decay_fwd, decay_bwd):
    c = RET_CHUNK
    lg_f = -jnp.exp(decay_fwd.astype(F32))
    lg_b = -jnp.exp(decay_bwd.astype(F32))
    idx = jnp.arange(c, dtype=F32)
    rel = idx[:, None] - idx[None, :]
    dmat = jnp.where(rel >= 0,
                     jnp.exp(lg_f[:, None, None] * jnp.maximum(rel, 0.0)[None]),
                     jnp.exp(lg_b[:, None, None] * jnp.maximum(-rel, 0.0)[None]))
    dmat = dmat.reshape(N_PAIRS, 2 * c, c)

    def lanes(v):
        v = v.reshape(N_PAIRS, 2, -1)
        return jnp.repeat(jnp.transpose(v, (0, 2, 1)), HEAD_DIM, axis=2)

    vec = jnp.stack([
        lanes(jnp.exp(lg_f[:, None] * (idx + 1.0)[None])),
        lanes(jnp.exp(lg_f[:, None] * (c - 1.0 - idx)[None])),
        lanes(jnp.exp(lg_b[:, None] * (c - idx)[None])),
        lanes(jnp.exp(lg_b[:, None] * idx[None])),
    ], axis=1)
    same_head = (jnp.arange(PAIR)[:, None] // HEAD_DIM) == (jnp.arange(PAIR)[None, :] // HEAD_DIM)
    cd = jnp.stack([lanes(jnp.exp(lg_f * c)[:, None]), lanes(jnp.exp(lg_b * c)[:, None])], axis=1)
    cd = jnp.transpose(cd, (0, 1, 3, 2)) * same_head[None, None].astype(F32)
    return dmat, vec, cd


def _retention_kernel(q_ref, k_ref, v_ref, g_ref, dmat_ref, vec_ref, cd_ref, o_ref, kv_ref, st_ref):
    s = q_ref.shape[0]
    c = RET_CHUNK
    nc = s // c
    lane = lax.broadcasted_iota(jnp.int32, (1, PAIR), 1)
    head0 = lane < HEAD_DIM
    same_head = ((lax.broadcasted_iota(jnp.int32, (PAIR, PAIR), 0) // HEAD_DIM)
                 == (lax.broadcasted_iota(jnp.int32, (PAIR, PAIR), 1) // HEAD_DIM))

    same_head2 = jnp.concatenate([same_head, same_head], axis=0)
    for n in range(nc):
        rows = slice(n * c, (n + 1) * c)
        kf = k_ref[rows, :].astype(F32)
        kcat = jnp.concatenate([(kf * vec_ref[1]).astype(BF16), (kf * vec_ref[3]).astype(BF16)], axis=1)
        kv = lax.dot_general(kcat, v_ref[rows, :], (((0,), (0,)), ((), ())), preferred_element_type=F32)
        kv_ref[n] = jnp.where(same_head2, kv, 0.0)

    state = jnp.zeros((PAIR, PAIR), F32)
    for n in range(nc):
        st_ref[n, :PAIR, :] = state.astype(BF16)
        state = state * cd_ref[0] + kv_ref[n, :PAIR, :]
    state = jnp.zeros((PAIR, PAIR), F32)
    for n in reversed(range(nc)):
        st_ref[n, PAIR:, :] = state.astype(BF16)
        state = state * cd_ref[1] + kv_ref[n, PAIR:, :]

    for n in range(nc):
        rows = slice(n * c, (n + 1) * c)
        q, k, v = q_ref[rows, :], k_ref[rows, :], v_ref[rows, :]
        q2 = jnp.concatenate([jnp.where(head0, q, jnp.zeros_like(q)),
                              jnp.where(head0, jnp.zeros_like(q), q)], axis=0)
        sc = lax.dot_general(q2, k, (((1,), (1,)), ((), ())), preferred_element_type=F32)
        intra = jnp.dot((sc * dmat_ref[...]).astype(BF16), v, preferred_element_type=F32)
        qf = q.astype(F32)
        qcat = jnp.concatenate([(qf * vec_ref[0]).astype(BF16), (qf * vec_ref[2]).astype(BF16)], axis=1)
        y = jnp.where(head0, intra[:c], intra[c:]) + jnp.dot(qcat, st_ref[n], preferred_element_type=F32)
        y2 = y * y
        ms0 = jnp.sum(jnp.where(head0, y2, 0.0), axis=-1, keepdims=True)
        ms1 = jnp.sum(jnp.where(head0, 0.0, y2), axis=-1, keepdims=True)
        ms = jnp.where(head0, ms0, ms1) * (1.0 / HEAD_DIM)
        g = g_ref[rows, :].astype(F32)
        o_ref[rows, :] = (y * lax.rsqrt(ms + EPS) * (g * jax.nn.sigmoid(g))).astype(o_ref.dtype)


def _mixers(qkv, qkv_grouped, bias_rows, ret_inputs, ret_tables, weights_f32):
    b, s, _ = qkv[0].shape
    c = RET_CHUNK
    n_steps = N_PAIRS * b
    spec = pl.BlockSpec((None, s, PAIR), lambda hp, bi: (bi, 0, hp))
    gspecs = [pl.BlockSpec((None, g.shape[1], None) + g.shape[3:], lambda hp, bi: (bi, 0, hp, 0, 0, 0))
              for g in qkv_grouped]
    slabs = [w.reshape(n_steps, -1, w.shape[-1]) for w in weights_f32]
    slab_specs = [pl.BlockSpec((None,) + w.shape[1:], lambda hp, bi: (hp * b + bi, 0, 0)) for w in slabs]
    n_attn_in, n_ret_in, n_w = 10, 7, len(slabs)
    attn_scratch = [
        pltpu.VMEM((N_BIAS_VARIANTS, 2 * ATTN_TQ, ATTN_W), F32),
        pltpu.VMEM((3, s, PAIR), F32),
        pltpu.VMEM((3, s, PAIR), F32),
        pltpu.VMEM((3, s, PAIR), F32),
        pltpu.VMEM((s, PAIR), F32),
    ]
    ret_scratch = [pltpu.VMEM((s // c, 2 * PAIR, PAIR), F32),
                   pltpu.VMEM((s // c, 2 * PAIR, PAIR), BF16)]

    def kernel(*refs):
        ins, rest = refs[:n_attn_in + n_ret_in + n_w], refs[n_attn_in + n_ret_in + n_w:]
        attn_o, ret_o, w_out, scratch = rest[0], rest[1], rest[2:2 + n_w], rest[2 + n_w:]
        _attention_kernel(*ins[:n_attn_in], attn_o, *scratch[:len(attn_scratch)])
        _retention_kernel(*ins[n_attn_in:n_attn_in + n_ret_in], ret_o, *scratch[len(attn_scratch):])
        for src, dst in zip(ins[n_attn_in + n_ret_in:], w_out):
            dst[...] = src[...].astype(BF16)

    outs = pl.pallas_call(
        kernel,
        grid=(N_PAIRS, b),
        in_specs=[spec] * 3 + gspecs
        + [pl.BlockSpec((None, 2 * N_BIAS_VARIANTS, 2 * ATTN_W), lambda hp, bi: (hp, 0, 0))]
        + [spec] * 4
        + [pl.BlockSpec((None, 2 * c, c), lambda hp, bi: (hp, 0, 0)),
           pl.BlockSpec((None, 4, c, PAIR), lambda hp, bi: (hp, 0, 0, 0)),
           pl.BlockSpec((None, 2, PAIR, PAIR), lambda hp, bi: (hp, 0, 0, 0))]
        + slab_specs,
        out_specs=[spec, spec] + slab_specs,
        out_shape=[jax.ShapeDtypeStruct((b, s, ATTN_WIDTH), BF16), jax.ShapeDtypeStruct((b, s, RET_WIDTH), BF16)]
        + [jax.ShapeDtypeStruct(w.shape, BF16) for w in slabs],
        scratch_shapes=attn_scratch + ret_scratch,
        compiler_params=pltpu.CompilerParams(
            dimension_semantics=("arbitrary", "arbitrary"), vmem_limit_bytes=VMEM_LIMIT),
        name="mixers",
    )(*qkv, *qkv_grouped, bias_rows, *ret_inputs, *ret_tables, *slabs)
    return outs[0], outs[1], [o.reshape(w.shape) for o, w in zip(outs[2:], weights_f32)]


def _route_tile(sub, attn_ref, ret_ref, x_ref, wo_ref, again_ref, fgain_ref,
                rw_ref, rb_ref, h_ref, xs_ref, dest_ref, cnt_ref):
    tm = ROUTE_TM
    rows = slice(sub * tm, (sub + 1) * tm)
    a = _rms(attn_ref[rows, :].astype(F32), again_ref[...]).astype(BF16)
    mixed = jnp.concatenate([a, ret_ref[rows, :]], axis=1)
    h = x_ref[rows, :] + jnp.dot(mixed, wo_ref[...], preferred_element_type=F32)
    h_ref[rows, :] = h.astype(h_ref.dtype)
    hn = _rms(h, fgain_ref[...]).astype(BF16)

    logits = jnp.dot(hn, rw_ref[...], preferred_element_type=F32) + rb_ref[...]
    lt = logits.T
    row = lambda i: lt[i:i + 1, :]

    def top1(vals):
        top = functools.reduce(jnp.maximum, vals)
        idx = jnp.full(top.shape, len(vals) - 1, jnp.int32)
        for i in reversed(range(len(vals) - 1)):
            idx = jnp.where(vals[i] == top, i, idx)
        return top, idx

    g_rows = [row(N_EXPERTS + g) for g in range(N_GROUPS)]
    gmax, grp = top1(g_rows)
    p_group = 1.0 / sum(jnp.exp(g - gmax) for g in g_rows)
    chosen = []
    for j in range(EXPERTS_PER_GROUP):
        e = row((N_GROUPS - 1) * EXPERTS_PER_GROUP + j)
        for g in reversed(range(N_GROUPS - 1)):
            e = jnp.where(grp == g, row(g * EXPERTS_PER_GROUP + j), e)
        chosen.append(e)
    v1, i1 = top1(chosen)
    v2, i2 = top1([jnp.where(i1 == j, -jnp.inf, e) for j, e in enumerate(chosen)])
    e2 = jnp.exp(v2 - v1)
    p1 = p_group / (1.0 + e2)
    inner = [jnp.where(i1 == j, p1, jnp.where(i2 == j, p1 * e2, 0.0)) for j in range(EXPERTS_PER_GROUP)]
    gate_rows = [jnp.where(grp == g, inner[j], 0.0)
                 for g in range(N_GROUPS) for j in range(EXPERTS_PER_GROUP)]

    sub8 = lax.broadcasted_iota(jnp.int32, (8, 1), 0)
    onehot_t = jnp.where(sub8 == grp, 1.0, 0.0)
    r_i = lax.broadcasted_iota(jnp.int32, (tm, tm), 0)
    c_i = lax.broadcasted_iota(jnp.int32, (tm, tm), 1)
    before = jnp.where(c_i < r_i, 1.0, 0.0).astype(BF16)
    rank_t = lax.dot_general(onehot_t.astype(BF16), before, (((1,), (1,)), ((), ())),
                             preferred_element_type=F32)
    cnt = jnp.sum(onehot_t, axis=1, keepdims=True)
    c16 = jnp.ceil(cnt * (1.0 / CHUNK)) * CHUNK
    start = jnp.zeros((8, 1), F32)
    for g in range(N_GROUPS - 1):
        start = start + jnp.where(sub8 > g, c16[g:g + 1, :], 0.0)
    dest_t = jnp.sum(onehot_t * (start + rank_t), axis=0, keepdims=True)
    cnt_ref[sub] = jnp.broadcast_to(jnp.where(sub8 < N_GROUPS, c16, pltpu.roll(start, N_GROUPS, 0)),
                                    (8, LANES)).astype(jnp.int32)

    cols = jnp.concatenate(gate_rows + [dest_t, jnp.zeros((LANES - N_EXPERTS - 1, tm), F32)], axis=0).T
    dest_ref[rows, :] = cols
    lane = lax.broadcasted_iota(jnp.int32, cols.shape, 1)
    gates = jnp.where(lane < N_EXPERTS, cols, 0.0)
    perm = jnp.where(lax.broadcasted_iota(jnp.int32, (SORT_ROWS, tm), 0) == dest_t.astype(jnp.int32),
                     1.0, 0.0).astype(BF16)
    g_hi = gates.astype(BF16)
    g_lo = (gates - g_hi.astype(F32)).astype(BF16)
    payload = jnp.concatenate([hn, g_hi, g_lo], axis=1)
    xs_ref[sub * SORT_ROWS:(sub + 1) * SORT_ROWS, :] = (
        jnp.dot(perm, payload, preferred_element_type=F32).astype(BF16))


def _outproj_kernel(*refs):
    for sub in range(ROUTE_TILES_PER_STEP):
        _route_tile(sub, *refs)


def _outproj(attn, ret, x, w_o, again, fgain, rw, rb):
    t, d = x.shape
    per_step = ROUTE_TILES_PER_STEP
    tm = ROUTE_TM * per_step
    nt = t // ROUTE_TM
    row = lambda w: pl.BlockSpec((tm, w), lambda i: (i, 0))
    full = lambda r, c: pl.BlockSpec((r, c), lambda i: (0, 0))
    return pl.pallas_call(
        _outproj_kernel,
        grid=(nt // per_step,),
        in_specs=[row(ATTN_WIDTH), row(RET_WIDTH), row(d), full(ATTN_WIDTH + RET_WIDTH, d),
                  full(1, ATTN_WIDTH), full(1, d), full(d, LANES), full(1, LANES)],
        out_specs=[row(d), pl.BlockSpec((per_step * SORT_ROWS, XS_W), lambda i: (i, 0)), row(LANES),
                   pl.BlockSpec((per_step, 8, LANES), lambda i: (i, 0, 0))],
        out_shape=[jax.ShapeDtypeStruct((t, d), BF16),
                   jax.ShapeDtypeStruct((nt * SORT_ROWS, XS_W), BF16),
                   jax.ShapeDtypeStruct((t, LANES), F32),
                   jax.ShapeDtypeStruct((nt, 8, LANES), jnp.int32)],
        compiler_params=pltpu.CompilerParams(
            dimension_semantics=("arbitrary",), vmem_limit_bytes=VMEM_LIMIT),
        name="outproj",
    )(attn, ret, x, w_o, again, fgain, rw, rb)


def _dispatch_tables(cnt, n_moe_tiles):
    nt = cnt.shape[0]
    cpt = MOE_TM // CHUNK

    n_slots = n_moe_tiles * cpt
    slot_rows = -(-n_slots // (8 * LANES)) * 8

    def schedule_kernel(seg_ref, grp_ref, used_ref, src_ref, tail_ref):
        slot = (lax.broadcasted_iota(jnp.int32, (slot_rows, LANES), 0) * LANES
                + lax.broadcasted_iota(jnp.int32, (slot_rows, LANES), 1))
        src = jnp.full((slot_rows, LANES), -1, jnp.int32)
        pos = jnp.int32(0)
        for g in range(N_GROUPS):
            def tile_body(i, carry):
                p, src = carry
                n = seg_ref[i, g] // CHUNK
                first = (i * SORT_ROWS + seg_ref[i, N_GROUPS + g]) // CHUNK
                src = jnp.where((slot >= p) & (slot < p + n), slot + (first - p), src)
                return p + n, src
            end, src = lax.fori_loop(0, nt, tile_body, (pos, src))
            padded = ((end + cpt - 1) // cpt) * cpt

            def mark_body(m, carry):
                grp_ref[m] = g
                used_ref[m] = 1
                return carry
            lax.fori_loop(pos // cpt, padded // cpt, mark_body, 0)
            pos = padded

        def idle_body(m, carry):
            grp_ref[m] = N_GROUPS - 1
            used_ref[m] = 0
            return carry
        lax.fori_loop(pos // cpt, n_moe_tiles, idle_body, 0)
        src_ref[...] = src

        def tail_body(i, carry):
            rows = seg_ref[i, 0]
            for g in range(1, N_GROUPS):
                rows = rows + seg_ref[i, g]
            tail_ref[i] = rows // CHUNK
            return carry
        lax.fori_loop(0, nt, tail_body, 0)

    smem = lambda: pl.BlockSpec(memory_space=pltpu.SMEM)
    tile_group, tile_used, src, tile_tail = pl.pallas_call(
        schedule_kernel,
        in_specs=[smem()],
        out_specs=[smem(), smem(), pl.BlockSpec(memory_space=pltpu.VMEM), smem()],
        out_shape=[jax.ShapeDtypeStruct((n_moe_tiles,), jnp.int32),
                   jax.ShapeDtypeStruct((n_moe_tiles,), jnp.int32),
                   jax.ShapeDtypeStruct((slot_rows, LANES), jnp.int32),
                   jax.ShapeDtypeStruct((nt,), jnp.int32)],
        name="schedule",
    )(cnt[:, :2 * N_GROUPS, 0])
    return tile_group, tile_used, src.reshape(-1)[:n_slots], tile_tail


def _moe_kernel(grp_ref, used_ref, src_ref, tail_ref, xs_hbm, w1_ref, w3_ref, w2_ref, ys_hbm,
                xbuf, obuf, zbuf, in_sem, out_sem, zero_sem):
    m = pl.program_id(0)
    n_tiles = pl.num_programs(0)
    cpt = MOE_TM // CHUNK
    cps = SORT_ROWS // CHUNK
    slot = m % 2
    zero_chunk = cps - 1

    def rows(c):
        return pl.ds(pl.multiple_of(c * CHUNK, CHUNK), CHUNK)

    def zero_tails(wait):
        def tile_body(i, carry):
            def body(c, carry2):
                cp = pltpu.make_async_copy(zbuf, ys_hbm.at[rows(i * cps + c), :], zero_sem)
                cp.wait() if wait else cp.start()
                return carry2
            lax.fori_loop(tail_ref[i], cps, body, 0)
            return carry
        lax.fori_loop(0, tail_ref.shape[0], tile_body, 0)

    def gather_start(tile, sl):
        for c in range(cpt):
            src = src_ref[tile * cpt + c]
            src = jnp.where(src < 0, zero_chunk, src)
            pltpu.make_async_copy(xs_hbm.at[rows(src), :], xbuf.at[sl, c * CHUNK:(c + 1) * CHUNK, :],
                                  in_sem.at[sl]).start()

    def gather_wait(sl):
        pltpu.make_async_copy(xs_hbm.at[0:MOE_TM, :], xbuf.at[sl], in_sem.at[sl]).wait()

    def scatter(tile, sl, wait):
        full = src_ref[tile * cpt + cpt - 1] >= 0

        def chunk_copy(c, src):
            return pltpu.make_async_copy(obuf.at[sl, c * CHUNK:(c + 1) * CHUNK, :], ys_hbm.at[rows(src), :],
                                         out_sem.at[sl])

        @pl.when(full)
        def _():
            if wait:
                pltpu.make_async_copy(obuf.at[sl], ys_hbm.at[0:MOE_TM, :], out_sem.at[sl]).wait()
            else:
                for c in range(cpt):
                    chunk_copy(c, src_ref[tile * cpt + c]).start()

        @pl.when(jnp.logical_not(full))
        def _():
            for c in range(cpt):
                src = src_ref[tile * cpt + c]

                @pl.when(src >= 0)
                def _():
                    cp = chunk_copy(c, src)
                    cp.wait() if wait else cp.start()

    @pl.when(m == 0)
    def _():
        gather_start(0, 0)
        zbuf[...] = jnp.zeros(zbuf.shape, zbuf.dtype)
        zero_tails(False)

    prev_used = used_ref[jnp.maximum(m - 1, 0)] > 0

    @pl.when((m == 0) | prev_used)
    def _():
        gather_wait(slot)

    @pl.when((m >= 2) & (used_ref[jnp.maximum(m - 2, 0)] > 0))
    def _():
        scatter(m - 2, slot, True)

    def expert_tile(n_rows):
        gather_start(jnp.minimum(m + 1, n_tiles - 1), 1 - slot)
        x = xbuf[slot, :n_rows, :D_MODEL]
        gate = (xbuf[slot, :n_rows, D_MODEL:D_MODEL + LANES].astype(F32)
                + xbuf[slot, :n_rows, D_MODEL + LANES:].astype(F32))
        lane = lax.broadcasted_iota(jnp.int32, gate.shape, 1)
        base = grp_ref[m] * EXPERTS_PER_GROUP
        acc = jnp.zeros((n_rows, D_MODEL), F32)
        for j in range(EXPERTS_PER_GROUP):
            a = jnp.dot(x, w1_ref[j], preferred_element_type=F32)
            b = jnp.dot(x, w3_ref[j], preferred_element_type=F32)
            gj = jnp.sum(jnp.where(lane == base + j, gate, 0.0), axis=-1, keepdims=True)
            hid = (a * jax.nn.sigmoid(a) * b * gj).astype(BF16)
            acc = acc + jnp.dot(hid, w2_ref[j], preferred_element_type=F32)
        obuf[slot, :n_rows, :] = acc.astype(BF16)
        scatter(m, slot, False)

    half_empty = src_ref[m * cpt + cpt // 2] < 0

    @pl.when((used_ref[m] > 0) & jnp.logical_not(half_empty))
    def _():
        expert_tile(MOE_TM)

    @pl.when((used_ref[m] > 0) & half_empty)
    def _():
        expert_tile(MOE_TM // 2)

    @pl.when(m == n_tiles - 1)
    def _():
        zero_tails(True)

        @pl.when(used_ref[m] > 0)
        def _():
            gather_wait(1 - slot)
            scatter(m, slot, True)

        @pl.when((m >= 1) & prev_used)
        def _():
            scatter(m - 1, 1 - slot, True)


def _moe(xs, w1, w3, w2, tile_group, tile_used, src_chunk, tile_tail):
    n_moe_tiles = tile_group.shape[0]
    rows = xs.shape[0]
    d = D_MODEL
    wspec = lambda r, c: pl.BlockSpec((EXPERTS_PER_GROUP, r, c), lambda m, grp, *_: (grp[m], 0, 0))
    return pl.pallas_call(
        _moe_kernel,
        grid_spec=pltpu.PrefetchScalarGridSpec(
            num_scalar_prefetch=4,
            grid=(n_moe_tiles,),
            in_specs=[pl.BlockSpec(memory_space=pl.ANY),
                      wspec(d, EXPERT_FF), wspec(d, EXPERT_FF), wspec(EXPERT_FF, d)],
            out_specs=pl.BlockSpec(memory_space=pl.ANY),
            scratch_shapes=[pltpu.VMEM((2, MOE_TM, XS_W), BF16),
                            pltpu.VMEM((2, MOE_TM, d), BF16),
                            pltpu.VMEM((CHUNK, d), BF16),
                            pltpu.SemaphoreType.DMA((2,)),
                            pltpu.SemaphoreType.DMA((2,)),
                            pltpu.SemaphoreType.DMA(())]),
        out_shape=jax.ShapeDtypeStruct((rows, d), BF16),
        compiler_params=pltpu.CompilerParams(
            dimension_semantics=("arbitrary",), vmem_limit_bytes=VMEM_LIMIT),
        name="moe",
    )(tile_group, tile_used, src_chunk, tile_tail, xs, w1, w3, w2)


def _combine_kernel(ys_ref, h_ref, dest_ref, gain_ref, o_ref):
    tm = ROUTE_TM
    for sub in range(h_ref.shape[0] // tm):
        rows = slice(sub * tm, (sub + 1) * tm)
        dest = dest_ref[rows, N_EXPERTS:N_EXPERTS + 1].astype(jnp.int32)
        perm_t = jnp.where(lax.broadcasted_iota(jnp.int32, (tm, SORT_ROWS), 1) == dest, 1.0, 0.0).astype(BF16)
        moe = jnp.dot(perm_t, ys_ref[sub * SORT_ROWS:(sub + 1) * SORT_ROWS, :], preferred_element_type=F32)
        o_ref[rows, :] = _rms(h_ref[rows, :].astype(F32) + moe, gain_ref[...])


def _combine(ys, h, dest, gain):
    t, d = h.shape
    per_step = COMBINE_TILES_PER_STEP
    tm = ROUTE_TM * per_step
    return pl.pallas_call(
        _combine_kernel,
        grid=(t // tm,),
        in_specs=[pl.BlockSpec((per_step * SORT_ROWS, d), lambda i: (i, 0)),
                  pl.BlockSpec((tm, d), lambda i: (i, 0)),
                  pl.BlockSpec((tm, LANES), lambda i: (i, 0)),
                  pl.BlockSpec((1, d), lambda i: (0, 0))],
        out_specs=pl.BlockSpec((tm, d), lambda i: (i, 0)),
        out_shape=jax.ShapeDtypeStruct((t, d), F32),
        compiler_params=pltpu.CompilerParams(
            dimension_semantics=("arbitrary",), vmem_limit_bytes=VMEM_LIMIT),
        name="combine",
    )(ys, h, dest, gain)


def _rotary_tables(s):
    half = HEAD_DIM // 2
    inv = ROPE_BASE ** (-jnp.arange(half, dtype=F32) / half)
    ang = jnp.arange(s, dtype=F32)[:, None] * inv[None, :]
    cos, sin = jnp.cos(ang), jnp.sin(ang)
    cos_t = jnp.tile(jnp.concatenate([cos, cos], axis=-1), (1, LANES // HEAD_DIM))
    sin_t = jnp.tile(jnp.concatenate([-sin, sin], axis=-1), (1, LANES // HEAD_DIM))
    return cos_t, sin_t


def kernel(x, w_in, w_out, norm_mix, norm_ffn, norm_final, attn_out_gain, rel_bias, ret_decay_fwd, ret_decay_bwd, router_group_w, router_group_b, router_expert_w, router_expert_b, expert_w1, expert_w3, expert_w2):
    b, s, d = x.shape
    depth = w_in.shape[0]
    cos_t, sin_t = _rotary_tables(s)
    bias_rows = _attn_bias_rows(rel_bias)
    h = x
    for layer in range(depth):
        (aq, ak, av, rq, rk, rv, rg), qkv_grouped = _inproj(
            h, norm_mix[layer][None], w_in[layer].astype(BF16), cos_t, sin_t)
        attn, ret, (w1, w3, w2, w_o) = _mixers(
            (aq, ak, av), qkv_grouped, bias_rows, (rq, rk, rv, rg),
            _retention_tables(ret_decay_fwd[layer], ret_decay_bwd[layer]),
            (expert_w1[layer], expert_w3[layer], expert_w2[layer], w_out[layer]))

        rw = jnp.concatenate(
            [jnp.transpose(router_expert_w[layer], (1, 0, 2)).reshape(d, N_EXPERTS),
             router_group_w[layer],
             jnp.zeros((d, LANES - N_EXPERTS - N_GROUPS), F32)], axis=1).astype(BF16)
        rb = jnp.concatenate(
            [router_expert_b[layer].reshape(N_EXPERTS), router_group_b[layer],
             jnp.zeros((LANES - N_EXPERTS - N_GROUPS,), F32)])[None].astype(F32)
        h1, xs, dest, cnt = _outproj(
            attn.reshape(b * s, ATTN_WIDTH), ret.reshape(b * s, RET_WIDTH), h.reshape(b * s, d),
            w_o, attn_out_gain[layer][None], norm_ffn[layer][None],
            rw, rb)
        n_route_tiles = (b * s) // ROUTE_TM
        n_moe_tiles = (b * s + n_route_tiles * N_GROUPS * (CHUNK - 1)) // MOE_TM + N_GROUPS
        schedule = _dispatch_tables(cnt, n_moe_tiles)
        ys = _moe(xs, w1, w3, w2, *schedule)
        assert depth == 1, "the combine kernel fuses the final norm, so it must run on the last layer"
        h = _combine(ys, h1, dest, norm_final[None]).reshape(b, s, d)
    return h
```

```python
import functools
import math

import jax
import jax.numpy as jnp
from jax import lax
from jax.experimental import pallas as pl
from jax.experimental.pallas import tpu as pltpu

F32 = jnp.float32
BF16 = jnp.bfloat16

D_MODEL = 1024
HEAD_DIM = 64
ATTN_WIDTH = 512
RET_WIDTH = 512
N_HEADS = 8
PAIR = 2 * HEAD_DIM
N_PAIRS = N_HEADS // 2
ATTN_DILATIONS = (1, 4, 16)
ATTN_RADIUS = 64
N_BUCKETS = 32
REL_MAX_DIST = 1024
ROPE_BASE = 10000.0
N_GROUPS = 4
EXPERTS_PER_GROUP = 4
N_EXPERTS = 16
EXPERT_FF = 512
EPS = 1e-6
NEG_INF = -1e30
LOG2_E = math.log2(math.e)

LANES = 128
ATTN_TQ = 128
ATTN_W = 256
N_BIAS_VARIANTS = 8
RET_CHUNK = 256
ROUTE_TM = 512
ROUTE_TILES_PER_STEP = 2
COMBINE_TILES_PER_STEP = 4
SORT_ROWS = 592
CHUNK = 16
MOE_TM = 512
XS_W = D_MODEL + 2 * LANES
VMEM_LIMIT = 48 * 1024 * 1024


def _rms(x, gain):
    return x * lax.rsqrt(jnp.mean(x * x, axis=-1, keepdims=True) + EPS) * gain


def _inproj_kernel(x_ref, gain_ref, w_ref, cos_ref, sin_ref,
                   aq_ref, ak_ref, av_ref, rq_ref, rk_ref, rv_ref, rg_ref,
                   aq4_ref, ak4_ref, av4_ref, aq16_ref, ak16_ref, av16_ref, stage_ref, stage4_ref):
    tm = x_ref.shape[0]
    xn = _rms(x_ref[...], gain_ref[...]).astype(BF16)

    def stage(i, t, nat_ref):
        nat_ref[...] = t.astype(BF16)
        for hp in range(N_PAIRS):
            stage_ref[i, hp] = t[:, hp * PAIR:(hp + 1) * PAIR]

    def regroup4(i, d4_ref):
        for hp in range(N_PAIRS):
            g = pltpu.einshape("(ab)l->(ba)l", stage_ref[i, hp], b=4)
            d4_ref[hp] = g.reshape(4, tm // 4, PAIR).astype(BF16)

    def regroup16(i, d16_ref):
        for hp in range(N_PAIRS):
            g = pltpu.einshape("(ab)l->(ba)l", stage_ref[i, hp], b=16)
            d16_ref[hp] = g.reshape(16, tm // 16, PAIR).astype(BF16)

    def seg(i):
        return jnp.dot(xn, w_ref[:, i * ATTN_WIDTH:(i + 1) * ATTN_WIDTH], preferred_element_type=F32)

    def rotary(t):
        cos, sin = cos_ref[...], sin_ref[...]
        first_half = (lax.broadcasted_iota(jnp.int32, (1, LANES), 1) % HEAD_DIM) < HEAD_DIM // 2
        outs = []
        for j in range(t.shape[1] // LANES):
            tj = t[:, j * LANES:(j + 1) * LANES]
            partner = jnp.where(first_half, pltpu.roll(tj, LANES - HEAD_DIM // 2, 1),
                                pltpu.roll(tj, HEAD_DIM // 2, 1))
            outs.append(tj * cos + partner * sin)
        return jnp.concatenate(outs, axis=1)

    stage(0, seg(0) * (HEAD_DIM ** -0.5 * LOG2_E), aq_ref)
    stage(1, seg(1), ak_ref)
    stage(2, seg(2), av_ref)
    rq_ref[...] = rotary(seg(3)).astype(BF16)
    for i, d4_ref in enumerate((aq4_ref, ak4_ref, av4_ref)):
        regroup4(i, d4_ref)
    rk_ref[...] = (rotary(seg(4)) * (HEAD_DIM ** -0.5)).astype(BF16)
    rv_ref[...] = seg(5).astype(BF16)
    for i, d16_ref in enumerate((aq16_ref, ak16_ref, av16_ref)):
        regroup16(i, d16_ref)
    rg_ref[...] = seg(6).astype(BF16)


def _inproj(x, gain, w_in, cos_t, sin_t, tm=512):
    b, s, d = x.shape
    n = w_in.shape[1]
    out = jax.ShapeDtypeStruct((b, s, ATTN_WIDTH), BF16)
    ospec = pl.BlockSpec((None, tm, ATTN_WIDTH), lambda si, bi: (bi, si, 0))

    def grouped(dil):
        shape = jax.ShapeDtypeStruct((b, s // tm, N_PAIRS, dil, tm // dil, PAIR), BF16)
        spec = pl.BlockSpec((None, None, N_PAIRS, dil, tm // dil, PAIR), lambda si, bi: (bi, si, 0, 0, 0, 0))
        return [shape] * 3, [spec] * 3

    shapes4, specs4 = grouped(4)
    shapes16, specs16 = grouped(16)
    outs = pl.pallas_call(
        _inproj_kernel,
        grid=(s // tm, b),
        in_specs=[
            pl.BlockSpec((None, tm, d), lambda si, bi: (bi, si, 0)),
            pl.BlockSpec((1, d), lambda si, bi: (0, 0)),
            pl.BlockSpec((d, n), lambda si, bi: (0, 0)),
            pl.BlockSpec((tm, LANES), lambda si, bi: (si, 0)),
            pl.BlockSpec((tm, LANES), lambda si, bi: (si, 0)),
        ],
        out_specs=[ospec] * 7 + specs4 + specs16,
        out_shape=[out] * 7 + shapes4 + shapes16,
        scratch_shapes=[pltpu.VMEM((3, N_PAIRS, tm, PAIR), F32),
                        pltpu.VMEM((3, N_PAIRS, 4, tm // 4, PAIR), F32)],
        compiler_params=pltpu.CompilerParams(
            dimension_semantics=("arbitrary", "arbitrary"), vmem_limit_bytes=VMEM_LIMIT),
        name="inproj",
    )(x, gain, w_in, cos_t, sin_t)
    return outs[:7], outs[7:]


def _t5_bucket(rel):
    half = N_BUCKETS // 2
    max_exact = half // 2
    offset = jnp.where(rel > 0, half, 0)
    n = jnp.abs(rel)
    nf = jnp.maximum(n, 1).astype(F32)
    large = max_exact + (jnp.log(nf / max_exact) / math.log(REL_MAX_DIST / max_exact)
                         * (half - max_exact)).astype(jnp.int32)
    large = jnp.minimum(large, half - 1)
    return offset + jnp.where(n < max_exact, n, large)


def _attn_bias_rows(rel_bias):
    period = 2 * ATTN_W
    band = 2 * ATTN_RADIUS + 1
    rel = jnp.arange(-ATTN_RADIUS, ATTN_RADIUS + 1)
    rows = []
    for dil, offs in ((1, (0, 64, 128)), (4, (0, 64, 128)), (16, (0, 128))):
        vals = rel_bias[_t5_bucket(rel * dil)].astype(F32).T * LOG2_E
        for off in offs:
            lo = off - ATTN_RADIUS
            pad = jnp.full((N_HEADS, period - band), NEG_INF, F32)
            if lo >= 0:
                row = jnp.concatenate([pad[:, :lo], vals, pad[:, lo:]], axis=1)
            else:
                row = jnp.concatenate([vals[:, -lo:], pad, vals[:, :-lo]], axis=1)
            rows.append(row)
    v = jnp.stack(rows, axis=1)
    return v.reshape(N_PAIRS, 2 * N_BIAS_VARIANTS, period)


def _attention_kernel(q_ref, k_ref, v_ref, q4_ref, k4_ref, v4_ref, q16_ref, k16_ref, v16_ref,
                      rows_ref, o_ref, bias_ref, acc_ref, m_ref, l_ref, out_ref):
    s = q_ref.shape[0]
    n_tiles = s // ATTN_TQ
    lane = lax.broadcasted_iota(jnp.int32, (1, PAIR), 1)
    head0 = lane < HEAD_DIM

    @pl.when(pl.program_id(1) == 0)
    def _():
        col = lax.broadcasted_iota(jnp.int32, (ATTN_TQ, ATTN_W), 1)
        for idx in range(2 * N_BIAS_VARIANTS):
            gen = jnp.broadcast_to(rows_ref[idx:idx + 1, :], (ATTN_TQ, 2 * ATTN_W))
            tab = pltpu.roll(gen, 0, 1, stride=1, stride_axis=0)[:, :ATTN_W]
            var = idx % N_BIAS_VARIANTS
            if var >= 6:
                tab = jnp.where((col // ATTN_TQ) == var - 6, tab, NEG_INF)
            head = idx // N_BIAS_VARIANTS
            bias_ref[var, head * ATTN_TQ:(head + 1) * ATTN_TQ, :] = tab

    def rows_of(ref, dil, lo, n):
        if dil == 1:
            return ref[lo:lo + n, :]
        piece, sub_len = ref.shape[2], s // dil
        parts, pos = [], lo
        while pos < lo + n:
            within = pos % sub_len
            off = within % piece
            take = min(piece - off, lo + n - pos)
            parts.append(ref[within // piece, pos // sub_len, off:off + take, :])
            pos += take
        return parts[0] if len(parts) == 1 else jnp.concatenate(parts, axis=0)

    def run_branch(bi, dil, qs_ref, ks_ref, vs_ref):
        sub_len = s // dil
        tiles_per_sub = sub_len // ATTN_TQ

        def tile(t):
            q0 = t * ATTN_TQ
            if tiles_per_sub == 1:
                ws = (t // 2) * ATTN_W
                var = 6 + t % 2
            else:
                pos = t % tiles_per_sub
                sub_lo = (t // tiles_per_sub) * sub_len
                ws = min(max(q0 - ATTN_RADIUS, sub_lo), sub_lo + sub_len - ATTN_W)
                var = (0 if pos == 0 else 2 if pos == tiles_per_sub - 1 else 1) + 3 * bi
            q = rows_of(qs_ref, dil, q0, ATTN_TQ)
            k = rows_of(ks_ref, dil, ws, ATTN_W)
            v = rows_of(vs_ref, dil, ws, ATTN_W)
            q2 = jnp.concatenate([jnp.where(head0, q, jnp.zeros_like(q)),
                                  jnp.where(head0, jnp.zeros_like(q), q)], axis=0)
            sc = lax.dot_general(q2, k, (((1,), (1,)), ((), ())), preferred_element_type=F32)
            sc = sc + bias_ref[var]
            m = jnp.max(sc, axis=-1, keepdims=True)
            p = jnp.exp2(sc - m).astype(BF16)
            o = jnp.dot(p, jnp.concatenate([v, jnp.ones_like(v)], axis=1), preferred_element_type=F32)
            l = o[:, PAIR:]
            outs, ms, ls = (o[:ATTN_TQ, :PAIR], o[ATTN_TQ:, :PAIR]), (m[:ATTN_TQ], m[ATTN_TQ:]), (l[:ATTN_TQ], l[ATTN_TQ:])
            if dil == 16:
                dst = pl.ds((t % 4) * (s // 4) + t // 4, ATTN_TQ, stride=4)
            else:
                dst = pl.ds(q0, ATTN_TQ)
            acc_ref[bi, dst, :] = jnp.where(head0, outs[0], outs[1])
            m_ref[bi, dst, :] = jnp.where(head0, ms[0], ms[1])
            l_ref[bi, dst, :] = jnp.where(head0, ls[0], ls[1])

        for t in range(n_tiles):
            tile(t)

    run_branch(0, 1, q_ref, k_ref, v_ref)
    run_branch(1, 4, q4_ref, k4_ref, v4_ref)
    run_branch(2, 16, q16_ref, k16_ref, v16_ref)

    rows = ATTN_TQ
    for r4 in range(4):
        for blk in range(s // (4 * rows)):
            nat = pl.ds(r4 + 4 * rows * blk, rows, stride=4)
            grouped4 = pl.ds(r4 * (s // 4) + rows * blk, rows)
            sl = (nat, grouped4, grouped4)
            m = [m_ref[bi, sl[bi], :] for bi in range(3)]
            mx = jnp.maximum(jnp.maximum(m[0], m[1]), m[2])
            num = jnp.zeros((rows, PAIR), F32)
            den = jnp.zeros((rows, PAIR), F32)
            for bi in range(3):
                e = jnp.exp2(m[bi] - mx)
                num = num + e * acc_ref[bi, sl[bi], :]
                den = den + e * l_ref[bi, sl[bi], :]
            out_ref[nat, :] = num / den
    o_ref[...] = out_ref[...].astype(o_ref.dtype)


def _retention_tables(decay_fwd, decay_bwd):
    c = RET_CHUNK
    lg_f = -jnp.exp(decay_fwd.astype(F32))
    lg_b = -jnp.exp(decay_bwd.astype(F32))
    idx = jnp.arange(c, dtype=F32)
    rel = idx[:, None] - idx[None, :]
    dmat = jnp.where(rel >= 0,
                     jnp.exp(lg_f[:, None, None] * jnp.maximum(rel, 0.0)[None]),
                     jnp.exp(lg_b[:, None, None] * jnp.maximum(-rel, 0.0)[None]))
    dmat = dmat.reshape(N_PAIRS, 2 * c, c)

    def lanes(v):
        v = v.reshape(N_PAIRS, 2, -1)
        return jnp.repeat(jnp.transpose(v, (0, 2, 1)), HEAD_DIM, axis=2)

    vec = jnp.stack([
        lanes(jnp.exp(lg_f[:, None] * (idx + 1.0)[None])),
        lanes(jnp.exp(lg_f[:, None] * (c - 1.0 - idx)[None])),
        lanes(jnp.exp(lg_b[:, None] * (c - idx)[None])),
        lanes(jnp.exp(lg_b[:, None] * idx[None])),
    ], axis=1)
    same_head = (jnp.arange(PAIR)[:, None] // HEAD_DIM) == (jnp.arange(PAIR)[None, :] // HEAD_DIM)
    cd = jnp.stack([lanes(jnp.exp(lg_f * c)[:, None]), lanes(jnp.exp(lg_b * c)[:, None])], axis=1)
    cd = jnp.transpose(cd, (0, 1, 3, 2)) * same_head[None, None].astype(F32)
    return dmat, vec, cd


def _retention_kernel(q_ref, k_ref, v_ref, g_ref, dmat_ref, vec_ref, cd_ref, o_ref, kv_ref, st_ref):
    s = q_ref.shape[0]
    c = RET_CHUNK
    nc = s // c
    lane = lax.broadcasted_iota(jnp.int32, (1, PAIR), 1)
    head0 = lane < HEAD_DIM
    same_head = ((lax.broadcasted_iota(jnp.int32, (PAIR, PAIR), 0) // HEAD_DIM)
                 == (lax.broadcasted_iota(jnp.int32, (PAIR, PAIR), 1) // HEAD_DIM))

    same_head2 = jnp.concatenate([same_head, same_head], axis=0)
    for n in range(nc):
        rows = slice(n * c, (n + 1) * c)
        kf = k_ref[rows, :].astype(F32)
        kcat = jnp.concatenate([(kf * vec_ref[1]).astype(BF16), (kf * vec_ref[3]).astype(BF16)], axis=1)
        kv = lax.dot_general(kcat, v_ref[rows, :], (((0,), (0,)), ((), ())), preferred_element_type=F32)
        kv_ref[n] = jnp.where(same_head2, kv, 0.0)

    state = jnp.zeros((PAIR, PAIR), F32)
    for n in range(nc):
        st_ref[n, :PAIR, :] = state.astype(BF16)
        state = state * cd_ref[0] + kv_ref[n, :PAIR, :]
    state = jnp.zeros((PAIR, PAIR), F32)
    for n in reversed(range(nc)):
        st_ref[n, PAIR:, :] = state.astype(BF16)
        state = state * cd_ref[1] + kv_ref[n, PAIR:, :]

    for n in range(nc):
        rows = slice(n * c, (n + 1) * c)
        q, k, v = q_ref[rows, :], k_ref[rows, :], v_ref[rows, :]
        q2 = jnp.concatenate([jnp.where(head0, q, jnp.zeros_like(q)),
                              jnp.where(head0, jnp.zeros_like(q), q)], axis=0)
        sc = lax.dot_general(q2, k, (((1,), (1,)), ((), ())), preferred_element_type=F32)
        intra = jnp.dot((sc * dmat_ref[...]).astype(BF16), v, preferred_element_type=F32)
        qf = q.astype(F32)
        qcat = jnp.concatenate([(qf * vec_ref[0]).astype(BF16), (qf * vec_ref[2]).astype(BF16)], axis=1)
        y = jnp.where(head0, intra[:c], intra[c:]) + jnp.dot(qcat, st_ref[n], preferred_element_type=F32)
        y2 = y * y
        ms0 = jnp.sum(jnp.where(head0, y2, 0.0), axis=-1, keepdims=True)
        ms1 = jnp.sum(jnp.where(head0, 0.0, y2), axis=-1, keepdims=True)
        ms = jnp.where(head0, ms0, ms1) * (1.0 / HEAD_DIM)
        g = g_ref[rows, :].astype(F32)
        o_ref[rows, :] = (y * lax.rsqrt(ms + EPS) * (g * jax.nn.sigmoid(g))).astype(o_ref.dtype)


def _mixers(qkv, qkv_grouped, bias_rows, ret_inputs, ret_tables, weights_f32):
    b, s, _ = qkv[0].shape
    c = RET_CHUNK
    n_steps = N_PAIRS * b
    spec = pl.BlockSpec((None, s, PAIR), lambda hp, bi: (bi, 0, hp))
    gspecs = [pl.BlockSpec((None, g.shape[1], None) + g.shape[3:], lambda hp, bi: (bi, 0, hp, 0, 0, 0))
              for g in qkv_grouped]
    slabs = [w.reshape(n_steps, -1, w.shape[-1]) for w in weights_f32]
    slab_specs = [pl.BlockSpec((None,) + w.shape[1:], lambda hp, bi: (hp * b + bi, 0, 0)) for w in slabs]
    n_attn_in, n_ret_in, n_w = 10, 7, len(slabs)
    attn_scratch = [
        pltpu.VMEM((N_BIAS_VARIANTS, 2 * ATTN_TQ, ATTN_W), F32),
        pltpu.VMEM((3, s, PAIR), F32),
        pltpu.VMEM((3, s, PAIR), F32),
        pltpu.VMEM((3, s, PAIR), F32),
        pltpu.VMEM((s, PAIR), F32),
    ]
    ret_scratch = [pltpu.VMEM((s // c, 2 * PAIR, PAIR), F32),
                   pltpu.VMEM((s // c, 2 * PAIR, PAIR), BF16)]

    def kernel(*refs):
        ins, rest = refs[:n_attn_in + n_ret_in + n_w], refs[n_attn_in + n_ret_in + n_w:]
        attn_o, ret_o, w_out, scratch = rest[0], rest[1], rest[2:2 + n_w], rest[2 + n_w:]
        _attention_kernel(*ins[:n_attn_in], attn_o, *scratch[:len(attn_scratch)])
        _retention_kernel(*ins[n_attn_in:n_attn_in + n_ret_in], ret_o, *scratch[len(attn_scratch):])
        for src, dst in zip(ins[n_attn_in + n_ret_in:], w_out):
            dst[...] = src[...].astype(BF16)

    outs = pl.pallas_call(
        kernel,
        grid=(N_PAIRS, b),
        in_specs=[spec] * 3 + gspecs
        + [pl.BlockSpec((None, 2 * N_BIAS_VARIANTS, 2 * ATTN_W), lambda hp, bi: (hp, 0, 0))]
        + [spec] * 4
        + [pl.BlockSpec((None, 2 * c, c), lambda hp, bi: (hp, 0, 0)),
           pl.BlockSpec((None, 4, c, PAIR), lambda hp, bi: (hp, 0, 0, 0)),
           pl.BlockSpec((None, 2, PAIR, PAIR), lambda hp, bi: (hp, 0, 0, 0))]
        + slab_specs,
        out_specs=[spec, spec] + slab_specs,
        out_shape=[jax.ShapeDtypeStruct((b, s, ATTN_WIDTH), BF16), jax.ShapeDtypeStruct((b, s, RET_WIDTH), BF16)]
        + [jax.ShapeDtypeStruct(w.shape, BF16) for w in slabs],
        scratch_shapes=attn_scratch + ret_scratch,
        compiler_params=pltpu.CompilerParams(
            dimension_semantics=("arbitrary", "arbitrary"), vmem_limit_bytes=VMEM_LIMIT),
        name="mixers",
    )(*qkv, *qkv_grouped, bias_rows, *ret_inputs, *ret_tables, *slabs)
    return outs[0], outs[1], [o.reshape(w.shape) for o, w in zip(outs[2:], weights_f32)]


def _route_tile(sub, attn_ref, ret_ref, x_ref, wo_ref, again_ref, fgain_ref,
                rw_ref, rb_ref, h_ref, xs_ref, dest_ref, cnt_ref):
    tm = ROUTE_TM
    rows = slice(sub * tm, (sub + 1) * tm)
    a = _rms(attn_ref[rows, :].astype(F32), again_ref[...]).astype(BF16)
    mixed = jnp.concatenate([a, ret_ref[rows, :]], axis=1)
    h = x_ref[rows, :] + jnp.dot(mixed, wo_ref[...], preferred_element_type=F32)
    h_ref[rows, :] = h.astype(h_ref.dtype)
    hn = _rms(h, fgain_ref[...]).astype(BF16)

    logits = jnp.dot(hn, rw_ref[...], preferred_element_type=F32) + rb_ref[...]
    lt = logits.T
    row = lambda i: lt[i:i + 1, :]

    def top1(vals):
        top = functools.reduce(jnp.maximum, vals)
        idx = jnp.full(top.shape, len(vals) - 1, jnp.int32)
        for i in reversed(range(len(vals) - 1)):
            idx = jnp.where(vals[i] == top, i, idx)
        return top, idx

    g_rows = [row(N_EXPERTS + g) for g in range(N_GROUPS)]
    gmax, grp = top1(g_rows)
    p_group = 1.0 / sum(jnp.exp(g - gmax) for g in g_rows)
    chosen = []
    for j in range(EXPERTS_PER_GROUP):
        e = row((N_GROUPS - 1) * EXPERTS_PER_GROUP + j)
        for g in reversed(range(N_GROUPS - 1)):
            e = jnp.where(grp == g, row(g * EXPERTS_PER_GROUP + j), e)
        chosen.append(e)
    v1, i1 = top1(chosen)
    v2, i2 = top1([jnp.where(i1 == j, -jnp.inf, e) for j, e in enumerate(chosen)])
    e2 = jnp.exp(v2 - v1)
    p1 = p_group / (1.0 + e2)
    inner = [jnp.where(i1 == j, p1, jnp.where(i2 == j, p1 * e2, 0.0)) for j in range(EXPERTS_PER_GROUP)]
    gate_rows = [jnp.where(grp == g, inner[j], 0.0)
                 for g in range(N_GROUPS) for j in range(EXPERTS_PER_GROUP)]

    sub8 = lax.broadcasted_iota(jnp.int32, (8, 1), 0)
    onehot_t = jnp.where(sub8 == grp, 1.0, 0.0)
    r_i = lax.broadcasted_iota(jnp.int32, (tm, tm), 0)
    c_i = lax.broadcasted_iota(jnp.int32, (tm, tm), 1)
    before = jnp.where(c_i < r_i, 1.0, 0.0).astype(BF16)
    rank_t = lax.dot_general(onehot_t.astype(BF16), before, (((1,), (1,)), ((), ())),
                             preferred_element_type=F32)
    cnt = jnp.sum(onehot_t, axis=1, keepdims=True)
    c16 = jnp.ceil(cnt * (1.0 / CHUNK)) * CHUNK
    start = jnp.zeros((8, 1), F32)
    for g in range(N_GROUPS - 1):
        start = start + jnp.where(sub8 > g, c16[g:g + 1, :], 0.0)
    dest_t = jnp.sum(onehot_t * (start + rank_t), axis=0, keepdims=True)
    cnt_ref[sub] = jnp.broadcast_to(jnp.where(sub8 < N_GROUPS, c16, pltpu.roll(start, N_GROUPS, 0)),
                                    (8, LANES)).astype(jnp.int32)

    cols = jnp.concatenate(gate_rows + [dest_t, jnp.zeros((LANES - N_EXPERTS - 1, tm), F32)], axis=0).T
    dest_ref[rows, :] = cols
    lane = lax.broadcasted_iota(jnp.int32, cols.shape, 1)
    gates = jnp.where(lane < N_EXPERTS, cols, 0.0)
    perm = jnp.where(lax.broadcasted_iota(jnp.int32, (SORT_ROWS, tm), 0) == dest_t.astype(jnp.int32),
                     1.0, 0.0).astype(BF16)
    g_hi = gates.astype(BF16)
    g_lo = (gates - g_hi.astype(F32)).astype(BF16)
    payload = jnp.concatenate([hn, g_hi, g_lo], axis=1)
    xs_ref[sub * SORT_ROWS:(sub + 1) * SORT_ROWS, :] = (
        jnp.dot(perm, payload, preferred_element_type=F32).astype(BF16))


def _outproj_kernel(*refs):
    for sub in range(ROUTE_TILES_PER_STEP):
        _route_tile(sub, *refs)


def _outproj(attn, ret, x, w_o, again, fgain, rw, rb):
    t, d = x.shape
    per_step = ROUTE_TILES_PER_STEP
    tm = ROUTE_TM * per_step
    nt = t // ROUTE_TM
    row = lambda w: pl.BlockSpec((tm, w), lambda i: (i, 0))
    full = lambda r, c: pl.BlockSpec((r, c), lambda i: (0, 0))
    return pl.pallas_call(
        _outproj_kernel,
        grid=(nt // per_step,),
        in_specs=[row(ATTN_WIDTH), row(RET_WIDTH), row(d), full(ATTN_WIDTH + RET_WIDTH, d),
                  full(1, ATTN_WIDTH), full(1, d), full(d, LANES), full(1, LANES)],
        out_specs=[row(d), pl.BlockSpec((per_step * SORT_ROWS, XS_W), lambda i: (i, 0)), row(LANES),
                   pl.BlockSpec((per_step, 8, LANES), lambda i: (i, 0, 0))],
        out_shape=[jax.ShapeDtypeStruct((t, d), BF16),
                   jax.ShapeDtypeStruct((nt * SORT_ROWS, XS_W), BF16),
                   jax.ShapeDtypeStruct((t, LANES), F32),
                   jax.ShapeDtypeStruct((nt, 8, LANES), jnp.int32)],
        compiler_params=pltpu.CompilerParams(
            dimension_semantics=("arbitrary",), vmem_limit_bytes=VMEM_LIMIT),
        name="outproj",
    )(attn, ret, x, w_o, again, fgain, rw, rb)


def _dispatch_tables(cnt, n_moe_tiles):
    nt = cnt.shape[0]
    cpt = MOE_TM // CHUNK

    n_slots = n_moe_tiles * cpt
    slot_rows = -(-n_slots // (8 * LANES)) * 8

    def schedule_kernel(seg_ref, grp_ref, used_ref, src_ref, tail_ref):
        slot = (lax.broadcasted_iota(jnp.int32, (slot_rows, LANES), 0) * LANES
                + lax.broadcasted_iota(jnp.int32, (slot_rows, LANES), 1))
        src = jnp.full((slot_rows, LANES), -1, jnp.int32)
        pos = jnp.int32(0)
        for g in range(N_GROUPS):
            def tile_body(i, carry):
                p, src = carry
                n = seg_ref[i, g] // CHUNK
                first = (i * SORT_ROWS + seg_ref[i, N_GROUPS + g]) // CHUNK
                src = jnp.where((slot >= p) & (slot < p + n), slot + (first - p), src)
                return p + n, src
            end, src = lax.fori_loop(0, nt, tile_body, (pos, src))
            padded = ((end + cpt - 1) // cpt) * cpt

            def mark_body(m, carry):
                grp_ref[m] = g
                used_ref[m] = 1
                return carry
            lax.fori_loop(pos // cpt, padded // cpt, mark_body, 0)
            pos = padded

        def idle_body(m, carry):
            grp_ref[m] = N_GROUPS - 1
            used_ref[m] = 0
            return carry
        lax.fori_loop(pos // cpt, n_moe_tiles, idle_body, 0)
        src_ref[...] = src

        def tail_body(i, carry):
            rows = seg_ref[i, 0]
            for g in range(1, N_GROUPS):
                rows = rows + seg_ref[i, g]
            tail_ref[i] = rows // CHUNK
            return carry
        lax.fori_loop(0, nt, tail_body, 0)

    smem = lambda: pl.BlockSpec(memory_space=pltpu.SMEM)
    tile_group, tile_used, src, tile_tail = pl.pallas_call(
        schedule_kernel,
        in_specs=[smem()],
        out_specs=[smem(), smem(), pl.BlockSpec(memory_space=pltpu.VMEM), smem()],
        out_shape=[jax.ShapeDtypeStruct((n_moe_tiles,), jnp.int32),
                   jax.ShapeDtypeStruct((n_moe_tiles,), jnp.int32),
                   jax.ShapeDtypeStruct((slot_rows, LANES), jnp.int32),
                   jax.ShapeDtypeStruct((nt,), jnp.int32)],
        name="schedule",
    )(cnt[:, :2 * N_GROUPS, 0])
    return tile_group, tile_used, src.reshape(-1)[:n_slots], tile_tail


def _moe_kernel(grp_ref, used_ref, src_ref, tail_ref, xs_hbm, w1_ref, w3_ref, w2_ref, ys_hbm,
                xbuf, obuf, zbuf, in_sem, out_sem, zero_sem):
    m = pl.program_id(0)
    n_tiles = pl.num_programs(0)
    cpt = MOE_TM // CHUNK
    cps = SORT_ROWS // CHUNK
    slot = m % 2
    zero_chunk = cps - 1

    def rows(c):
        return pl.ds(pl.multiple_of(c * CHUNK, CHUNK), CHUNK)

    def zero_tails(wait):
        def tile_body(i, carry):
            def body(c, carry2):
                cp = pltpu.make_async_copy(zbuf, ys_hbm.at[rows(i * cps + c), :], zero_sem)
                cp.wait() if wait else cp.start()
                return carry2
            lax.fori_loop(tail_ref[i], cps, body, 0)
            return carry
        lax.fori_loop(0, tail_ref.shape[0], tile_body, 0)

    def gather_start(tile, sl):
        for c in range(cpt):
            src = src_ref[tile * cpt + c]
            src = jnp.where(src < 0, zero_chunk, src)
            pltpu.make_async_copy(xs_hbm.at[rows(src), :], xbuf.at[sl, c * CHUNK:(c + 1) * CHUNK, :],
                                  in_sem.at[sl]).start()

    def gather_wait(sl):
        pltpu.make_async_copy(xs_hbm.at[0:MOE_TM, :], xbuf.at[sl], in_sem.at[sl]).wait()

    def scatter(tile, sl, wait):
        full = src_ref[tile * cpt + cpt - 1] >= 0

        def chunk_copy(c, src):
            return pltpu.make_async_copy(obuf.at[sl, c * CHUNK:(c + 1) * CHUNK, :], ys_hbm.at[rows(src), :],
                                         out_sem.at[sl])

        @pl.when(full)
        def _():
            if wait:
                pltpu.make_async_copy(obuf.at[sl], ys_hbm.at[0:MOE_TM, :], out_sem.at[sl]).wait()
            else:
                for c in range(cpt):
                    chunk_copy(c, src_ref[tile * cpt + c]).start()

        @pl.when(jnp.logical_not(full))
        def _():
            for c in range(cpt):
                src = src_ref[tile * cpt + c]

                @pl.when(src >= 0)
                def _():
                    cp = chunk_copy(c, src)
                    cp.wait() if wait else cp.start()

    @pl.when(m == 0)
    def _():
        gather_start(0, 0)
        zbuf[...] = jnp.zeros(zbuf.shape, zbuf.dtype)
        zero_tails(False)

    prev_used = used_ref[jnp.maximum(m - 1, 0)] > 0

    @pl.when((m == 0) | prev_used)
    def _():
        gather_wait(slot)

    @pl.when((m >= 2) & (used_ref[jnp.maximum(m - 2, 0)] > 0))
    def _():
        scatter(m - 2, slot, True)

    def expert_tile(n_rows):
        gather_start(jnp.minimum(m + 1, n_tiles - 1), 1 - slot)
        x = xbuf[slot, :n_rows, :D_MODEL]
        gate = (xbuf[slot, :n_rows, D_MODEL:D_MODEL + LANES].astype(F32)
                + xbuf[slot, :n_rows, D_MODEL + LANES:].astype(F32))
        lane = lax.broadcasted_iota(jnp.int32, gate.shape, 1)
        base = grp_ref[m] * EXPERTS_PER_GROUP
        acc = jnp.zeros((n_rows, D_MODEL), F32)
        for j in range(EXPERTS_PER_GROUP):
            a = jnp.dot(x, w1_ref[j], preferred_element_type=F32)
            b = jnp.dot(x, w3_ref[j], preferred_element_type=F32)
            gj = jnp.sum(jnp.where(lane == base + j, gate, 0.0), axis=-1, keepdims=True)
            hid = (a * jax.nn.sigmoid(a) * b * gj).astype(BF16)
            acc = acc + jnp.dot(hid, w2_ref[j], preferred_element_type=F32)
        obuf[slot, :n_rows, :] = acc.astype(BF16)
        scatter(m, slot, False)

    half_empty = src_ref[m * cpt + cpt // 2] < 0

    @pl.when((used_ref[m] > 0) & jnp.logical_not(half_empty))
    def _():
        expert_tile(MOE_TM)

    @pl.when((used_ref[m] > 0) & half_empty)
    def _():
        expert_tile(MOE_TM // 2)

    @pl.when(m == n_tiles - 1)
    def _():
        zero_tails(True)

        @pl.when(used_ref[m] > 0)
        def _():
            gather_wait(1 - slot)
            scatter(m, slot, True)

        @pl.when((m >= 1) & prev_used)
        def _():
            scatter(m - 1, 1 - slot, True)


def _moe(xs, w1, w3, w2, tile_group, tile_used, src_chunk, tile_tail):
    n_moe_tiles = tile_group.shape[0]
    rows = xs.shape[0]
    d = D_MODEL
    wspec = lambda r, c: pl.BlockSpec((EXPERTS_PER_GROUP, r, c), lambda m, grp, *_: (grp[m], 0, 0))
    return pl.pallas_call(
        _moe_kernel,
        grid_spec=pltpu.PrefetchScalarGridSpec(
            num_scalar_prefetch=4,
            grid=(n_moe_tiles,),
            in_specs=[pl.BlockSpec(memory_space=pl.ANY),
                      wspec(d, EXPERT_FF), wspec(d, EXPERT_FF), wspec(EXPERT_FF, d)],
            out_specs=pl.BlockSpec(memory_space=pl.ANY),
            scratch_shapes=[pltpu.VMEM((2, MOE_TM, XS_W), BF16),
                            pltpu.VMEM((2, MOE_TM, d), BF16),
                            pltpu.VMEM((CHUNK, d), BF16),
                            pltpu.SemaphoreType.DMA((2,)),
                            pltpu.SemaphoreType.DMA((2,)),
                            pltpu.SemaphoreType.DMA(())]),
        out_shape=jax.ShapeDtypeStruct((rows, d), BF16),
        compiler_params=pltpu.CompilerParams(
            dimension_semantics=("arbitrary",), vmem_limit_bytes=VMEM_LIMIT),
        name="moe",
    )(tile_group, tile_used, src_chunk, tile_tail, xs, w1, w3, w2)


def _combine_kernel(ys_ref, h_ref, dest_ref, gain_ref, o_ref):
    tm = ROUTE_TM
    for sub in range(h_ref.shape[0] // tm):
        rows = slice(sub * tm, (sub + 1) * tm)
        dest = dest_ref[rows, N_EXPERTS:N_EXPERTS + 1].astype(jnp.int32)
        perm_t = jnp.where(lax.broadcasted_iota(jnp.int32, (tm, SORT_ROWS), 1) == dest, 1.0, 0.0).astype(BF16)
        moe = jnp.dot(perm_t, ys_ref[sub * SORT_ROWS:(sub + 1) * SORT_ROWS, :], preferred_element_type=F32)
        o_ref[rows, :] = _rms(h_ref[rows, :].astype(F32) + moe, gain_ref[...])


def _combine(ys, h, dest, gain):
    t, d = h.shape
    per_step = COMBINE_TILES_PER_STEP
    tm = ROUTE_TM * per_step
    return pl.pallas_call(
        _combine_kernel,
        grid=(t // tm,),
        in_specs=[pl.BlockSpec((per_step * SORT_ROWS, d), lambda i: (i, 0)),
                  pl.BlockSpec((tm, d), lambda i: (i, 0)),
                  pl.BlockSpec((tm, LANES), lambda i: (i, 0)),
                  pl.BlockSpec((1, d), lambda i: (0, 0))],
        out_specs=pl.BlockSpec((tm, d), lambda i: (i, 0)),
        out_shape=jax.ShapeDtypeStruct((t, d), F32),
        compiler_params=pltpu.CompilerParams(
            dimension_semantics=("arbitrary",), vmem_limit_bytes=VMEM_LIMIT),
        name="combine",
    )(ys, h, dest, gain)


def _rotary_tables(s):
    half = HEAD_DIM // 2
    inv = ROPE_BASE ** (-jnp.arange(half, dtype=F32) / half)
    ang = jnp.arange(s, dtype=F32)[:, None] * inv[None, :]
    cos, sin = jnp.cos(ang), jnp.sin(ang)
    cos_t = jnp.tile(jnp.concatenate([cos, cos], axis=-1), (1, LANES // HEAD_DIM))
    sin_t = jnp.tile(jnp.concatenate([-sin, sin], axis=-1), (1, LANES // HEAD_DIM))
    return cos_t, sin_t


def kernel(x, w_in, w_out, norm_mix, norm_ffn, norm_final, attn_out_gain, rel_bias, ret_decay_fwd, ret_decay_bwd, router_group_w, router_group_b, router_expert_w, router_expert_b, expert_w1, expert_w3, expert_w2):
    b, s, d = x.shape
    depth = w_in.shape[0]
    cos_t, sin_t = _rotary_tables(s)
    bias_rows = _attn_bias_rows(rel_bias)
    h = x
    for layer in range(depth):
        (aq, ak, av, rq, rk, rv, rg), qkv_grouped = _inproj(
            h, norm_mix[layer][None], w_in[layer].astype(BF16), cos_t, sin_t)
        attn, ret, (w1, w3, w2, w_o) = _mixers(
            (aq, ak, av), qkv_grouped, bias_rows, (rq, rk, rv, rg),
            _retention_tables(ret_decay_fwd[layer], ret_decay_bwd[layer]),
            (expert_w1[layer], expert_w3[layer], expert_w2[layer], w_out[layer]))

        rw = jnp.concatenate(
            [jnp.transpose(router_expert_w[layer], (1, 0, 2)).reshape(d, N_EXPERTS),
             router_group_w[layer],
             jnp.zeros((d, LANES - N_EXPERTS - N_GROUPS), F32)], axis=1).astype(BF16)
        rb = jnp.concatenate(
            [router_expert_b[layer].reshape(N_EXPERTS), router_group_b[layer],
             jnp.zeros((LANES - N_EXPERTS - N_GROUPS,), F32)])[None].astype(F32)
        h1, xs, dest, cnt = _outproj(
            attn.reshape(b * s, ATTN_WIDTH), ret.reshape(b * s, RET_WIDTH), h.reshape(b * s, d),
            w_o, attn_out_gain[layer][None], norm_ffn[layer][None],
            rw, rb)
        n_route_tiles = (b * s) // ROUTE_TM
        n_moe_tiles = (b * s + n_route_tiles * N_GROUPS * (CHUNK - 1)) // MOE_TM + N_GROUPS
        schedule = _dispatch_tables(cnt, n_moe_tiles)
        ys = _moe(xs, w1, w3, w2, *schedule)
        assert depth == 1, "the combine kernel fuses the final norm, so it must run on the last layer"
        h = _combine(ys, h1, dest, norm_final[None]).reshape(b, s, d)
    return h
```

```python
import functools
import math

import jax
import jax.numpy as jnp
from jax import lax
from jax.experimental import pallas as pl
from jax.experimental.pallas import tpu as pltpu

F32 = jnp.float32
BF16 = jnp.bfloat16

D_MODEL = 1024
HEAD_DIM = 64
ATTN_WIDTH = 512
RET_WIDTH = 512
N_HEADS = 8
PAIR = 2 * HEAD_DIM
N_PAIRS = N_HEADS // 2
ATTN_DILATIONS = (1, 4, 16)
ATTN_RADIUS = 64
N_BUCKETS = 32
REL_MAX_DIST = 1024
ROPE_BASE = 10000.0
N_GROUPS = 4
EXPERTS_PER_GROUP = 4
N_EXPERTS = 16
EXPERT_FF = 512
EPS = 1e-6
NEG_INF = -1e30
LOG2_E = math.log2(math.e)

LANES = 128
ATTN_TQ = 128
ATTN_W = 256
N_BIAS_VARIANTS = 8
RET_CHUNK = 256
ROUTE_TM = 512
ROUTE_TILES_PER_STEP = 2
COMBINE_TILES_PER_STEP = 2
COMBINE_BUFFERS = 3
SORT_ROWS = 592
CHUNK = 16
MOE_TM = 512
XS_W = D_MODEL + 2 * LANES
VMEM_LIMIT = 48 * 1024 * 1024


def _rms(x, gain):
    return x * lax.rsqrt(jnp.mean(x * x, axis=-1, keepdims=True) + EPS) * gain


def _inproj_kernel(x_ref, gain_ref, w_ref, cos_ref, sin_ref,
                   aq_ref, ak_ref, av_ref, rq_ref, rk_ref, rv_ref, rg_ref,
                   aq4_ref, ak4_ref, av4_ref, aq16_ref, ak16_ref, av16_ref, stage_ref, stage4_ref):
    tm = x_ref.shape[0]
    xn = _rms(x_ref[...], gain_ref[...]).astype(BF16)

    def stage(i, t, nat_ref):
        nat_ref[...] = t.astype(BF16)
        for hp in range(N_PAIRS):
            stage_ref[i, hp] = t[:, hp * PAIR:(hp + 1) * PAIR]

    def regroup4(i, d4_ref):
        for hp in range(N_PAIRS):
            for r4 in range(4):
                g4 = stage_ref[i, hp, pl.ds(r4, tm // 4, stride=4), :]
                d4_ref[hp, r4] = g4.astype(BF16)
                stage4_ref[i, hp, r4] = g4

    def regroup16(i, d16_ref):
        for hp in range(N_PAIRS):
            for r4 in range(4):
                for j in range(4):
                    d16_ref[hp, r4 + 4 * j] = (
                        stage4_ref[i, hp, r4, pl.ds(j, tm // 16, stride=4), :].astype(BF16))

    def seg(i):
        return jnp.dot(xn, w_ref[:, i * ATTN_WIDTH:(i + 1) * ATTN_WIDTH], preferred_element_type=F32)

    def rotary(t):
        cos, sin = cos_ref[...], sin_ref[...]
        first_half = (lax.broadcasted_iota(jnp.int32, (1, LANES), 1) % HEAD_DIM) < HEAD_DIM // 2
        outs = []
        for j in range(t.shape[1] // LANES):
            tj = t[:, j * LANES:(j + 1) * LANES]
            partner = jnp.where(first_half, pltpu.roll(tj, LANES - HEAD_DIM // 2, 1),
                                pltpu.roll(tj, HEAD_DIM // 2, 1))
            outs.append(tj * cos + partner * sin)
        return jnp.concatenate(outs, axis=1)

    stage(0, seg(0) * (HEAD_DIM ** -0.5 * LOG2_E), aq_ref)
    stage(1, seg(1), ak_ref)
    stage(2, seg(2), av_ref)
    rq_ref[...] = rotary(seg(3)).astype(BF16)
    for i, d4_ref in enumerate((aq4_ref, ak4_ref, av4_ref)):
        regroup4(i, d4_ref)
    rk_ref[...] = (rotary(seg(4)) * (HEAD_DIM ** -0.5)).astype(BF16)
    rv_ref[...] = seg(5).astype(BF16)
    for i, d16_ref in enumerate((aq16_ref, ak16_ref, av16_ref)):
        regroup16(i, d16_ref)
    rg_ref[...] = seg(6).astype(BF16)


def _inproj(x, gain, w_in, cos_t, sin_t, tm=512):
    b, s, d = x.shape
    n = w_in.shape[1]
    out = jax.ShapeDtypeStruct((b, s, ATTN_WIDTH), BF16)
    ospec = pl.BlockSpec((None, tm, ATTN_WIDTH), lambda si, bi: (bi, si, 0))

    def grouped(dil):
        shape = jax.ShapeDtypeStruct((b, s // tm, N_PAIRS, dil, tm // dil, PAIR), BF16)
        spec = pl.BlockSpec((None, None, N_PAIRS, dil, tm // dil, PAIR), lambda si, bi: (bi, si, 0, 0, 0, 0))
        return [shape] * 3, [spec] * 3

    shapes4, specs4 = grouped(4)
    shapes16, specs16 = grouped(16)
    outs = pl.pallas_call(
        _inproj_kernel,
        grid=(s // tm, b),
        in_specs=[
            pl.BlockSpec((None, tm, d), lambda si, bi: (bi, si, 0)),
            pl.BlockSpec((1, d), lambda si, bi: (0, 0)),
            pl.BlockSpec((d, n), lambda si, bi: (0, 0)),
            pl.BlockSpec((tm, LANES), lambda si, bi: (si, 0)),
            pl.BlockSpec((tm, LANES), lambda si, bi: (si, 0)),
        ],
        out_specs=[ospec] * 7 + specs4 + specs16,
        out_shape=[out] * 7 + shapes4 + shapes16,
        scratch_shapes=[pltpu.VMEM((3, N_PAIRS, tm, PAIR), F32),
                        pltpu.VMEM((3, N_PAIRS, 4, tm // 4, PAIR), F32)],
        compiler_params=pltpu.CompilerParams(
            dimension_semantics=("arbitrary", "arbitrary"), vmem_limit_bytes=VMEM_LIMIT),
        name="inproj",
    )(x, gain, w_in, cos_t, sin_t)
    return outs[:7], outs[7:]


def _t5_bucket(rel):
    half = N_BUCKETS // 2
    max_exact = half // 2
    offset = jnp.where(rel > 0, half, 0)
    n = jnp.abs(rel)
    nf = jnp.maximum(n, 1).astype(F32)
    large = max_exact + (jnp.log(nf / max_exact) / math.log(REL_MAX_DIST / max_exact)
                         * (half - max_exact)).astype(jnp.int32)
    large = jnp.minimum(large, half - 1)
    return offset + jnp.where(n < max_exact, n, large)


def _attn_bias_rows(rel_bias):
    period = 2 * ATTN_W
    band = 2 * ATTN_RADIUS + 1
    rel = jnp.arange(-ATTN_RADIUS, ATTN_RADIUS + 1)
    rows = []
    for dil, offs in ((1, (0, 64, 128)), (4, (0, 64, 128)), (16, (0, 128))):
        vals = rel_bias[_t5_bucket(rel * dil)].astype(F32).T * LOG2_E
        for off in offs:
            lo = off - ATTN_RADIUS
            pad = jnp.full((N_HEADS, period - band), NEG_INF, F32)
            if lo >= 0:
                row = jnp.concatenate([pad[:, :lo], vals, pad[:, lo:]], axis=1)
            else:
                row = jnp.concatenate([vals[:, -lo:], pad, vals[:, :-lo]], axis=1)
            rows.append(row)
    v = jnp.stack(rows, axis=1)
    return v.reshape(N_PAIRS, 2 * N_BIAS_VARIANTS, period)


def _attention_kernel(q_ref, k_ref, v_ref, q4_ref, k4_ref, v4_ref, q16_ref, k16_ref, v16_ref,
                      rows_ref, o_ref, bias_ref, acc_ref, m_ref, l_ref, out_ref):
    s = q_ref.shape[0]
    n_tiles = s // ATTN_TQ
    lane = lax.broadcasted_iota(jnp.int32, (1, PAIR), 1)
    head0 = lane < HEAD_DIM

    @pl.when(pl.program_id(1) == 0)
    def _():
        col = lax.broadcasted_iota(jnp.int32, (ATTN_TQ, ATTN_W), 1)
        for idx in range(2 * N_BIAS_VARIANTS):
            gen = jnp.broadcast_to(rows_ref[idx:idx + 1, :], (ATTN_TQ, 2 * ATTN_W))
            tab = pltpu.roll(gen, 0, 1, stride=1, stride_axis=0)[:, :ATTN_W]
            var = idx % N_BIAS_VARIANTS
            if var >= 6:
                tab = jnp.where((col // ATTN_TQ) == var - 6, tab, NEG_INF)
            head = idx // N_BIAS_VARIANTS
            bias_ref[var, head * ATTN_TQ:(head + 1) * ATTN_TQ, :] = tab

    def rows_of(ref, dil, lo, n):
        if dil == 1:
            return ref[lo:lo + n, :]
        piece, sub_len = ref.shape[2], s // dil
        parts, pos = [], lo
        while pos < lo + n:
            within = pos % sub_len
            off = within % piece
            take = min(piece - off, lo + n - pos)
            parts.append(ref[within // piece, pos // sub_len, off:off + take, :])
            pos += take
        return parts[0] if len(parts) == 1 else jnp.concatenate(parts, axis=0)

    def run_branch(bi, dil, qs_ref, ks_ref, vs_ref):
        sub_len = s // dil
        tiles_per_sub = sub_len // ATTN_TQ

        def tile(t):
            q0 = t * ATTN_TQ
            if tiles_per_sub == 1:
                ws = (t // 2) * ATTN_W
                var = 6 + t % 2
            else:
                pos = t % tiles_per_sub
                sub_lo = (t // tiles_per_sub) * sub_len
                ws = min(max(q0 - ATTN_RADIUS, sub_lo), sub_lo + sub_len - ATTN_W)
                var = (0 if pos == 0 else 2 if pos == tiles_per_sub - 1 else 1) + 3 * bi
            q = rows_of(qs_ref, dil, q0, ATTN_TQ)
            k = rows_of(ks_ref, dil, ws, ATTN_W)
            v = rows_of(vs_ref, dil, ws, ATTN_W)
            q2 = jnp.concatenate([jnp.where(head0, q, jnp.zeros_like(q)),
                                  jnp.where(head0, jnp.zeros_like(q), q)], axis=0)
            sc = lax.dot_general(q2, k, (((1,), (1,)), ((), ())), preferred_element_type=F32)
            sc = sc + bias_ref[var]
            m = jnp.max(sc, axis=-1, keepdims=True)
            p = jnp.exp2(sc - m).astype(BF16)
            o = jnp.dot(p, jnp.concatenate([v, jnp.ones_like(v)], axis=1), preferred_element_type=F32)
            l = o[:, PAIR:]
            outs, ms, ls = (o[:ATTN_TQ, :PAIR], o[ATTN_TQ:, :PAIR]), (m[:ATTN_TQ], m[ATTN_TQ:]), (l[:ATTN_TQ], l[ATTN_TQ:])
            if dil == 16:
                dst = pl.ds((t % 4) * (s // 4) + t // 4, ATTN_TQ, stride=4)
            else:
                dst = pl.ds(q0, ATTN_TQ)
            acc_ref[bi, dst, :] = jnp.where(head0, outs[0], outs[1])
            m_ref[bi, dst, :] = jnp.where(head0, ms[0], ms[1])
            l_ref[bi, dst, :] = jnp.where(head0, ls[0], ls[1])

        for t in range(n_tiles):
            tile(t)

    run_branch(0, 1, q_ref, k_ref, v_ref)
    run_branch(1, 4, q4_ref, k4_ref, v4_ref)
    run_branch(2, 16, q16_ref, k16_ref, v16_ref)

    rows = ATTN_TQ
    for r4 in range(4):
        for blk in range(s // (4 * rows)):
            nat = pl.ds(r4 + 4 * rows * blk, rows, stride=4)
            grouped4 = pl.ds(r4 * (s // 4) + rows * blk, rows)
            sl = (nat, grouped4, grouped4)
            m = [m_ref[bi, sl[bi], :] for bi in range(3)]
            mx = jnp.maximum(jnp.maximum(m[0], m[1]), m[2])
            num = jnp.zeros((rows, PAIR), F32)
            den = jnp.zeros((rows, PAIR), F32)
            for bi in range(3):
                e = jnp.exp2(m[bi] - mx)
                num = num + e * acc_ref[bi, sl[bi], :]
                den = den + e * l_ref[bi, sl[bi], :]
            out_ref[nat, :] = num / den
    o_ref[...] = out_ref[...].astype(o_ref.dtype)


def _retention_tables(decay_fwd, decay_bwd):
    c = RET_CHUNK
    lg_f = -jnp.exp(decay_fwd.astype(F32))
    lg_b = -jnp.exp(decay_bwd.astype(F32))
    idx = jnp.arange(c, dtype=F32)
    rel = idx[:, None] - idx[None, :]
    dmat = jnp.where(rel >= 0,
                     jnp.exp(lg_f[:, None, None] * jnp.maximum(rel, 0.0)[None]),
                     jnp.exp(lg_b[:, None, None] * jnp.maximum(-rel, 0.0)[None]))
    dmat = dmat.reshape(N_PAIRS, 2 * c, c)

    def lanes(v):
        v = v.reshape(N_PAIRS, 2, -1)
        return jnp.repeat(jnp.transpose(v, (0, 2, 1)), HEAD_DIM, axis=2)

    vec = jnp.stack([
        lanes(jnp.exp(lg_f[:, None] * (idx + 1.0)[None])),
        lanes(jnp.exp(lg_f[:, None] * (c - 1.0 - idx)[None])),
        lanes(jnp.exp(lg_b[:, None] * (c - idx)[None])),
        lanes(jnp.exp(lg_b[:, None] * idx[None])),
    ], axis=1)
    same_head = (jnp.arange(PAIR)[:, None] // HEAD_DIM) == (jnp.arange(PAIR)[None, :] // HEAD_DIM)
    cd = jnp.stack([lanes(jnp.exp(lg_f * c)[:, None]), lanes(jnp.exp(lg_b * c)[:, None])], axis=1)
    cd = jnp.transpose(cd, (0, 1, 3, 2)) * same_head[None, None].astype(F32)
    return dmat, vec, cd


def _retention_kernel(q_ref, k_ref, v_ref, g_ref, dmat_ref, vec_ref, cd_ref, o_ref, kv_ref, st_ref):
    s = q_ref.shape[0]
    c = RET_CHUNK
    nc = s // c
    lane = lax.broadcasted_iota(jnp.int32, (1, PAIR), 1)
    head0 = lane < HEAD_DIM
    same_head = ((lax.broadcasted_iota(jnp.int32, (PAIR, PAIR), 0) // HEAD_DIM)
                 == (lax.broadcasted_iota(jnp.int32, (PAIR, PAIR), 1) // HEAD_DIM))

    same_head2 = jnp.concatenate([same_head, same_head], axis=0)
    for n in range(nc):
        rows = slice(n * c, (n + 1) * c)
        kf = k_ref[rows, :].astype(F32)
        kcat = jnp.concatenate([(kf * vec_ref[1]).astype(BF16), (kf * vec_ref[3]).astype(BF16)], axis=1)
        kv = lax.dot_general(kcat, v_ref[rows, :], (((0,), (0,)), ((), ())), preferred_element_type=F32)
        kv_ref[n] = jnp.where(same_head2, kv, 0.0)

    state = jnp.zeros((PAIR, PAIR), F32)
    for n in range(nc):
        st_ref[n, :PAIR, :] = state.astype(BF16)
        state = state * cd_ref[0] + kv_ref[n, :PAIR, :]
    state = jnp.zeros((PAIR, PAIR), F32)
    for n in reversed(range(nc)):
        st_ref[n, PAIR:, :] = state.astype(BF16)
        state = state * cd_ref[1] + kv_ref[n, PAIR:, :]

    for n in range(nc):
        rows = slice(n * c, (n + 1) * c)
        q, k, v = q_ref[rows, :], k_ref[rows, :], v_ref[rows, :]
        q2 = jnp.concatenate([jnp.where(head0, q, jnp.zeros_like(q)),
                              jnp.where(head0, jnp.zeros_like(q), q)], axis=0)
        sc = lax.dot_general(q2, k, (((1,), (1,)), ((), ())), preferred_element_type=F32)
        intra = jnp.dot((sc * dmat_ref[...]).astype(BF16), v, preferred_element_type=F32)
        qf = q.astype(F32)
        qcat = jnp.concatenate([(qf * vec_ref[0]).astype(BF16), (qf * vec_ref[2]).astype(BF16)], axis=1)
        y = jnp.where(head0, intra[:c], intra[c:]) + jnp.dot(qcat, st_ref[n], preferred_element_type=F32)
        y2 = y * y
        ms0 = jnp.sum(jnp.where(head0, y2, 0.0), axis=-1, keepdims=True)
        ms1 = jnp.sum(jnp.where(head0, 0.0, y2), axis=-1, keepdims=True)
        ms = jnp.where(head0, ms0, ms1) * (1.0 / HEAD_DIM)
        g = g_ref[rows, :].astype(F32)
        o_ref[rows, :] = (y * lax.rsqrt(ms + EPS) * (g * jax.nn.sigmoid(g))).astype(o_ref.dtype)


def _mixers(qkv, qkv_grouped, bias_rows, ret_inputs, ret_tables, weights_f32):
    b, s, _ = qkv[0].shape
    c = RET_CHUNK
    n_steps = N_PAIRS * b
    spec = pl.BlockSpec((None, s, PAIR), lambda hp, bi: (bi, 0, hp))
    gspecs = [pl.BlockSpec((None, g.shape[1], None) + g.shape[3:], lambda hp, bi: (bi, 0, hp, 0, 0, 0))
              for g in qkv_grouped]
    slabs = [w.reshape(n_steps, -1, w.shape[-1]) for w in weights_f32]
    slab_specs = [pl.BlockSpec((None,) + w.shape[1:], lambda hp, bi: (hp * b + bi, 0, 0)) for w in slabs]
    n_attn_in, n_ret_in, n_w = 10, 7, len(slabs)
    attn_scratch = [
        pltpu.VMEM((N_BIAS_VARIANTS, 2 * ATTN_TQ, ATTN_W), F32),
        pltpu.VMEM((3, s, PAIR), F32),
        pltpu.VMEM((3, s, PAIR), F32),
        pltpu.VMEM((3, s, PAIR), F32),
        pltpu.VMEM((s, PAIR), F32),
    ]
    ret_scratch = [pltpu.VMEM((s // c, 2 * PAIR, PAIR), F32),
                   pltpu.VMEM((s // c, 2 * PAIR, PAIR), BF16)]

    def kernel(*refs):
        ins, rest = refs[:n_attn_in + n_ret_in + n_w], refs[n_attn_in + n_ret_in + n_w:]
        attn_o, ret_o, w_out, scratch = rest[0], rest[1], rest[2:2 + n_w], rest[2 + n_w:]
        _attention_kernel(*ins[:n_attn_in], attn_o, *scratch[:len(attn_scratch)])
        _retention_kernel(*ins[n_attn_in:n_attn_in + n_ret_in], ret_o, *scratch[len(attn_scratch):])
        for src, dst in zip(ins[n_attn_in + n_ret_in:], w_out):
            dst[...] = src[...].astype(BF16)

    outs = pl.pallas_call(
        kernel,
        grid=(N_PAIRS, b),
        in_specs=[spec] * 3 + gspecs
        + [pl.BlockSpec((None, 2 * N_BIAS_VARIANTS, 2 * ATTN_W), lambda hp, bi: (hp, 0, 0))]
        + [spec] * 4
        + [pl.BlockSpec((None, 2 * c, c), lambda hp, bi: (hp, 0, 0)),
           pl.BlockSpec((None, 4, c, PAIR), lambda hp, bi: (hp, 0, 0, 0)),
           pl.BlockSpec((None, 2, PAIR, PAIR), lambda hp, bi: (hp, 0, 0, 0))]
        + slab_specs,
        out_specs=[spec, spec] + slab_specs,
        out_shape=[jax.ShapeDtypeStruct((b, s, ATTN_WIDTH), BF16), jax.ShapeDtypeStruct((b, s, RET_WIDTH), BF16)]
        + [jax.ShapeDtypeStruct(w.shape, BF16) for w in slabs],
        scratch_shapes=attn_scratch + ret_scratch,
        compiler_params=pltpu.CompilerParams(
            dimension_semantics=("arbitrary", "arbitrary"), vmem_limit_bytes=VMEM_LIMIT),
        name="mixers",
    )(*qkv, *qkv_grouped, bias_rows, *ret_inputs, *ret_tables, *slabs)
    return outs[0], outs[1], [o.reshape(w.shape) for o, w in zip(outs[2:], weights_f32)]


def _route_tile(sub, attn_ref, ret_ref, x_ref, wo_ref, again_ref, fgain_ref,
                rw_ref, rb_ref, h_ref, xs_ref, dest_ref, cnt_ref):
    tm = ROUTE_TM
    rows = slice(sub * tm, (sub + 1) * tm)
    a = _rms(attn_ref[rows, :].astype(F32), again_ref[...]).astype(BF16)
    mixed = jnp.concatenate([a, ret_ref[rows, :]], axis=1)
    h = x_ref[rows, :] + jnp.dot(mixed, wo_ref[...], preferred_element_type=F32)
    h_ref[rows, :] = h.astype(h_ref.dtype)
    hn = _rms(h, fgain_ref[...]).astype(BF16)

    logits = jnp.dot(hn, rw_ref[...], preferred_element_type=F32) + rb_ref[...]
    lt = logits.T
    row = lambda i: lt[i:i + 1, :]

    def top1(vals):
        top = functools.reduce(jnp.maximum, vals)
        idx = jnp.full(top.shape, len(vals) - 1, jnp.int32)
        for i in reversed(range(len(vals) - 1)):
            idx = jnp.where(vals[i] == top, i, idx)
        return top, idx

    g_rows = [row(N_EXPERTS + g) for g in range(N_GROUPS)]
    gmax, grp = top1(g_rows)
    p_group = 1.0 / sum(jnp.exp(g - gmax) for g in g_rows)
    chosen = []
    for j in range(EXPERTS_PER_GROUP):
        e = row((N_GROUPS - 1) * EXPERTS_PER_GROUP + j)
        for g in reversed(range(N_GROUPS - 1)):
            e = jnp.where(grp == g, row(g * EXPERTS_PER_GROUP + j), e)
        chosen.append(e)
    v1, i1 = top1(chosen)
    v2, i2 = top1([jnp.where(i1 == j, -jnp.inf, e) for j, e in enumerate(chosen)])
    e2 = jnp.exp(v2 - v1)
    p1 = p_group / (1.0 + e2)
    inner = [jnp.where(i1 == j, p1, jnp.where(i2 == j, p1 * e2, 0.0)) for j in range(EXPERTS_PER_GROUP)]
    gate_rows = [jnp.where(grp == g, inner[j], 0.0)
                 for g in range(N_GROUPS) for j in range(EXPERTS_PER_GROUP)]

    sub8 = lax.broadcasted_iota(jnp.int32, (8, 1), 0)
    onehot_t = jnp.where(sub8 == grp, 1.0, 0.0)
    r_i = lax.broadcasted_iota(jnp.int32, (tm, tm), 0)
    c_i = lax.broadcasted_iota(jnp.int32, (tm, tm), 1)
    before = jnp.where(c_i < r_i, 1.0, 0.0).astype(BF16)
    rank_t = lax.dot_general(onehot_t.astype(BF16), before, (((1,), (1,)), ((), ())),
                             preferred_element_type=F32)
    cnt = jnp.sum(onehot_t, axis=1, keepdims=True)
    c16 = jnp.ceil(cnt * (1.0 / CHUNK)) * CHUNK
    start = jnp.zeros((8, 1), F32)
    for g in range(N_GROUPS - 1):
        start = start + jnp.where(sub8 > g, c16[g:g + 1, :], 0.0)
    dest_t = jnp.sum(onehot_t * (start + rank_t), axis=0, keepdims=True)
    cnt_ref[sub] = jnp.broadcast_to(jnp.where(sub8 < N_GROUPS, c16, pltpu.roll(start, N_GROUPS, 0)),
                                    (8, LANES)).astype(jnp.int32)

    cols = jnp.concatenate(gate_rows + [dest_t, jnp.zeros((LANES - N_EXPERTS - 1, tm), F32)], axis=0).T
    dest_ref[rows, :] = cols
    lane = lax.broadcasted_iota(jnp.int32, cols.shape, 1)
    gates = jnp.where(lane < N_EXPERTS, cols, 0.0)
    perm = jnp.where(lax.broadcasted_iota(jnp.int32, (SORT_ROWS, tm), 0) == dest_t.astype(jnp.int32),
                     1.0, 0.0).astype(BF16)
    g_hi = gates.astype(BF16)
    g_lo = (gates - g_hi.astype(F32)).astype(BF16)
    payload = jnp.concatenate([hn, g_hi, g_lo], axis=1)
    xs_ref[sub * SORT_ROWS:(sub + 1) * SORT_ROWS, :] = (
        jnp.dot(perm, payload, preferred_element_type=F32).astype(BF16))


def _outproj_kernel(*refs):
    for sub in range(ROUTE_TILES_PER_STEP):
        _route_tile(sub, *refs)


def _outproj(attn, ret, x, w_o, again, fgain, rw, rb):
    t, d = x.shape
    per_step = ROUTE_TILES_PER_STEP
    tm = ROUTE_TM * per_step
    nt = t // ROUTE_TM
    row = lambda w: pl.BlockSpec((tm, w), lambda i: (i, 0))
    full = lambda r, c: pl.BlockSpec((r, c), lambda i: (0, 0))
    return pl.pallas_call(
        _outproj_kernel,
        grid=(nt // per_step,),
        in_specs=[row(ATTN_WIDTH), row(RET_WIDTH), row(d), full(ATTN_WIDTH + RET_WIDTH, d),
                  full(1, ATTN_WIDTH), full(1, d), full(d, LANES), full(1, LANES)],
        out_specs=[row(d), pl.BlockSpec((per_step * SORT_ROWS, XS_W), lambda i: (i, 0)), row(LANES),
                   pl.BlockSpec((per_step, 8, LANES), lambda i: (i, 0, 0))],
        out_shape=[jax.ShapeDtypeStruct((t, d), BF16),
                   jax.ShapeDtypeStruct((nt * SORT_ROWS, XS_W), BF16),
                   jax.ShapeDtypeStruct((t, LANES), F32),
                   jax.ShapeDtypeStruct((nt, 8, LANES), jnp.int32)],
        compiler_params=pltpu.CompilerParams(
            dimension_semantics=("arbitrary",), vmem_limit_bytes=VMEM_LIMIT),
        name="outproj",
    )(attn, ret, x, w_o, again, fgain, rw, rb)


def _dispatch_tables(cnt, n_moe_tiles):
    nt = cnt.shape[0]
    cpt = MOE_TM // CHUNK

    n_slots = n_moe_tiles * cpt
    slot_rows = -(-n_slots // (8 * LANES)) * 8

    def schedule_kernel(seg_ref, grp_ref, used_ref, src_ref, tail_ref):
        slot = (lax.broadcasted_iota(jnp.int32, (slot_rows, LANES), 0) * LANES
                + lax.broadcasted_iota(jnp.int32, (slot_rows, LANES), 1))
        src = jnp.full((slot_rows, LANES), -1, jnp.int32)
        pos = jnp.int32(0)
        for g in range(N_GROUPS):
            def tile_body(i, carry):
                p, src = carry
                n = seg_ref[i, g] // CHUNK
                first = (i * SORT_ROWS + seg_ref[i, N_GROUPS + g]) // CHUNK
                src = jnp.where((slot >= p) & (slot < p + n), slot + (first - p), src)
                return p + n, src
            end, src = lax.fori_loop(0, nt, tile_body, (pos, src))
            padded = ((end + cpt - 1) // cpt) * cpt

            def mark_body(m, carry):
                grp_ref[m] = g
                used_ref[m] = 1
                return carry
            lax.fori_loop(pos // cpt, padded // cpt, mark_body, 0)
            pos = padded

        def idle_body(m, carry):
            grp_ref[m] = N_GROUPS - 1
            used_ref[m] = 0
            return carry
        lax.fori_loop(pos // cpt, n_moe_tiles, idle_body, 0)
        src_ref[...] = src

        def tail_body(i, carry):
            rows = seg_ref[i, 0]
            for g in range(1, N_GROUPS):
                rows = rows + seg_ref[i, g]
            tail_ref[i] = rows // CHUNK
            return carry
        lax.fori_loop(0, nt, tail_body, 0)

    smem = lambda: pl.BlockSpec(memory_space=pltpu.SMEM)
    tile_group, tile_used, src, tile_tail = pl.pallas_call(
        schedule_kernel,
        in_specs=[smem()],
        out_specs=[smem(), smem(), pl.BlockSpec(memory_space=pltpu.VMEM), smem()],
        out_shape=[jax.ShapeDtypeStruct((n_moe_tiles,), jnp.int32),
                   jax.ShapeDtypeStruct((n_moe_tiles,), jnp.int32),
                   jax.ShapeDtypeStruct((slot_rows, LANES), jnp.int32),
                   jax.ShapeDtypeStruct((nt,), jnp.int32)],
        name="schedule",
    )(cnt[:, :2 * N_GROUPS, 0])
    return tile_group, tile_used, src.reshape(-1)[:n_slots], tile_tail


def _moe_kernel(grp_ref, used_ref, src_ref, tail_ref, xs_hbm, w1_ref, w3_ref, w2_ref, ys_hbm,
                xbuf, obuf, zbuf, in_sem, out_sem, zero_sem):
    m = pl.program_id(0)
    n_tiles = pl.num_programs(0)
    cpt = MOE_TM // CHUNK
    cps = SORT_ROWS // CHUNK
    slot = m % 2
    zero_chunk = cps - 1

    def rows(c):
        return pl.ds(pl.multiple_of(c * CHUNK, CHUNK), CHUNK)

    def zero_tails(wait):
        def tile_body(i, carry):
            def body(c, carry2):
                cp = pltpu.make_async_copy(zbuf, ys_hbm.at[rows(i * cps + c), :], zero_sem)
                cp.wait() if wait else cp.start()
                return carry2
            lax.fori_loop(tail_ref[i], cps, body, 0)
            return carry
        lax.fori_loop(0, tail_ref.shape[0], tile_body, 0)

    def gather_start(tile, sl):
        for c in range(cpt):
            src = src_ref[tile * cpt + c]
            src = jnp.where(src < 0, zero_chunk, src)
            pltpu.make_async_copy(xs_hbm.at[rows(src), :], xbuf.at[sl, c * CHUNK:(c + 1) * CHUNK, :],
                                  in_sem.at[sl]).start()

    def gather_wait(sl):
        pltpu.make_async_copy(xs_hbm.at[0:MOE_TM, :], xbuf.at[sl], in_sem.at[sl]).wait()

    def scatter(tile, sl, wait):
        full = src_ref[tile * cpt + cpt - 1] >= 0

        def chunk_copy(c, src):
            return pltpu.make_async_copy(obuf.at[sl, c * CHUNK:(c + 1) * CHUNK, :], ys_hbm.at[rows(src), :],
                                         out_sem.at[sl])

        @pl.when(full)
        def _():
            if wait:
                pltpu.make_async_copy(obuf.at[sl], ys_hbm.at[0:MOE_TM, :], out_sem.at[sl]).wait()
            else:
                for c in range(cpt):
                    chunk_copy(c, src_ref[tile * cpt + c]).start()

        @pl.when(jnp.logical_not(full))
        def _():
            for c in range(cpt):
                src = src_ref[tile * cpt + c]

                @pl.when(src >= 0)
                def _():
                    cp = chunk_copy(c, src)
                    cp.wait() if wait else cp.start()

    @pl.when(m == 0)
    def _():
        gather_start(0, 0)
        zbuf[...] = jnp.zeros(zbuf.shape, zbuf.dtype)
        zero_tails(False)

    prev_used = used_ref[jnp.maximum(m - 1, 0)] > 0

    @pl.when((m == 0) | prev_used)
    def _():
        gather_wait(slot)

    @pl.when((m >= 2) & (used_ref[jnp.maximum(m - 2, 0)] > 0))
    def _():
        scatter(m - 2, slot, True)

    def expert_tile(n_rows):
        gather_start(jnp.minimum(m + 1, n_tiles - 1), 1 - slot)
        x = xbuf[slot, :n_rows, :D_MODEL]
        gate = (xbuf[slot, :n_rows, D_MODEL:D_MODEL + LANES].astype(F32)
                + xbuf[slot, :n_rows, D_MODEL + LANES:].astype(F32))
        lane = lax.broadcasted_iota(jnp.int32, gate.shape, 1)
        base = grp_ref[m] * EXPERTS_PER_GROUP
        acc = jnp.zeros((n_rows, D_MODEL), F32)
        for j in range(EXPERTS_PER_GROUP):
            a = jnp.dot(x, w1_ref[j], preferred_element_type=F32)
            b = jnp.dot(x, w3_ref[j], preferred_element_type=F32)
            gj = jnp.sum(jnp.where(lane == base + j, gate, 0.0), axis=-1, keepdims=True)
            hid = (a * jax.nn.sigmoid(a) * b * gj).astype(BF16)
            acc = acc + jnp.dot(hid, w2_ref[j], preferred_element_type=F32)
        obuf[slot, :n_rows, :] = acc.astype(BF16)
        scatter(m, slot, False)

    half_empty = src_ref[m * cpt + cpt // 2] < 0

    @pl.when((used_ref[m] > 0) & jnp.logical_not(half_empty))
    def _():
        expert_tile(MOE_TM)

    @pl.when((used_ref[m] > 0) & half_empty)
    def _():
        expert_tile(MOE_TM // 2)

    @pl.when(m == n_tiles - 1)
    def _():
        zero_tails(True)

        @pl.when(used_ref[m] > 0)
        def _():
            gather_wait(1 - slot)
            scatter(m, slot, True)

        @pl.when((m >= 1) & prev_used)
        def _():
            scatter(m - 1, 1 - slot, True)


def _moe(xs, w1, w3, w2, tile_group, tile_used, src_chunk, tile_tail):
    n_moe_tiles = tile_group.shape[0]
    rows = xs.shape[0]
    d = D_MODEL
    wspec = lambda r, c: pl.BlockSpec((EXPERTS_PER_GROUP, r, c), lambda m, grp, *_: (grp[m], 0, 0))
    return pl.pallas_call(
        _moe_kernel,
        grid_spec=pltpu.PrefetchScalarGridSpec(
            num_scalar_prefetch=4,
            grid=(n_moe_tiles,),
            in_specs=[pl.BlockSpec(memory_space=pl.ANY),
                      wspec(d, EXPERT_FF), wspec(d, EXPERT_FF), wspec(EXPERT_FF, d)],
            out_specs=pl.BlockSpec(memory_space=pl.ANY),
            scratch_shapes=[pltpu.VMEM((2, MOE_TM, XS_W), BF16),
                            pltpu.VMEM((2, MOE_TM, d), BF16),
                            pltpu.VMEM((CHUNK, d), BF16),
                            pltpu.SemaphoreType.DMA((2,)),
                            pltpu.SemaphoreType.DMA((2,)),
                            pltpu.SemaphoreType.DMA(())]),
        out_shape=jax.ShapeDtypeStruct((rows, d), BF16),
        compiler_params=pltpu.CompilerParams(
            dimension_semantics=("arbitrary",), vmem_limit_bytes=VMEM_LIMIT),
        name="moe",
    )(tile_group, tile_used, src_chunk, tile_tail, xs, w1, w3, w2)


def _combine_kernel(ys_hbm, h_hbm, dest_ref, gain_ref, o_ref, ysbuf, hbuf, ys_sem, h_sem):
    tm = ROUTE_TM
    per_step = o_ref.shape[0] // tm
    i = pl.program_id(0)
    n = h_hbm.shape[0] // (per_step * tm)
    ahead = min(COMBINE_BUFFERS - 1, n - 1)

    def copies(step, slot):
        ys_rows = pl.ds(pl.multiple_of(step * (per_step * SORT_ROWS), CHUNK), per_step * SORT_ROWS)
        h_rows = pl.ds(pl.multiple_of(step * (per_step * tm), tm), per_step * tm)
        return (pltpu.make_async_copy(ys_hbm.at[ys_rows, :], ysbuf.at[slot], ys_sem.at[slot]),
                pltpu.make_async_copy(h_hbm.at[h_rows, :], hbuf.at[slot], h_sem.at[slot]))

    @pl.when(i == 0)
    def _():
        for step in range(ahead):
            for cp in copies(step, step):
                cp.start()

    @pl.when(i + ahead < n)
    def _():
        for cp in copies(i + ahead, (i + ahead) % COMBINE_BUFFERS):
            cp.start()

    slot = i % COMBINE_BUFFERS
    for cp in copies(i, slot):
        cp.wait()
    for sub in range(per_step):
        rows = slice(sub * tm, (sub + 1) * tm)
        dest = dest_ref[rows, N_EXPERTS:N_EXPERTS + 1].astype(jnp.int32)
        perm_t = jnp.where(lax.broadcasted_iota(jnp.int32, (tm, SORT_ROWS), 1) == dest, 1.0, 0.0).astype(BF16)
        moe = jnp.dot(perm_t, ysbuf[slot, sub * SORT_ROWS:(sub + 1) * SORT_ROWS, :], preferred_element_type=F32)
        o_ref[rows, :] = _rms(hbuf[slot, rows, :].astype(F32) + moe, gain_ref[...])


def _combine(ys, h, dest, gain):
    t, d = h.shape
    per_step = COMBINE_TILES_PER_STEP
    tm = ROUTE_TM * per_step
    return pl.pallas_call(
        _combine_kernel,
        grid=(t // tm,),
        in_specs=[pl.BlockSpec(memory_space=pl.ANY),
                  pl.BlockSpec(memory_space=pl.ANY),
                  pl.BlockSpec((tm, LANES), lambda i: (i, 0)),
                  pl.BlockSpec((1, d), lambda i: (0, 0))],
        out_specs=pl.BlockSpec((tm, d), lambda i: (i, 0)),
        out_shape=jax.ShapeDtypeStruct((t, d), F32),
        scratch_shapes=[pltpu.VMEM((COMBINE_BUFFERS, per_step * SORT_ROWS, d), BF16),
                        pltpu.VMEM((COMBINE_BUFFERS, tm, d), BF16),
                        pltpu.SemaphoreType.DMA((COMBINE_BUFFERS,)),
                        pltpu.SemaphoreType.DMA((COMBINE_BUFFERS,))],
        compiler_params=pltpu.CompilerParams(
            dimension_semantics=("arbitrary",), vmem_limit_bytes=VMEM_LIMIT),
        name="combine",
    )(ys, h, dest, gain)


def _rotary_tables(s):
    half = HEAD_DIM // 2
    inv = ROPE_BASE ** (-jnp.arange(half, dtype=F32) / half)
    ang = jnp.arange(s, dtype=F32)[:, None] * inv[None, :]
    cos, sin = jnp.cos(ang), jnp.sin(ang)
    cos_t = jnp.tile(jnp.concatenate([cos, cos], axis=-1), (1, LANES // HEAD_DIM))
    sin_t = jnp.tile(jnp.concatenate([-sin, sin], axis=-1), (1, LANES // HEAD_DIM))
    return cos_t, sin_t


def kernel(x, w_in, w_out, norm_mix, norm_ffn, norm_final, attn_out_gain, rel_bias, ret_decay_fwd, ret_decay_bwd, router_group_w, router_group_b, router_expert_w, router_expert_b, expert_w1, expert_w3, expert_w2):
    b, s, d = x.shape
    depth = w_in.shape[0]
    cos_t, sin_t = _rotary_tables(s)
    bias_rows = _attn_bias_rows(rel_bias)
    h = x
    for layer in range(depth):
        (aq, ak, av, rq, rk, rv, rg), qkv_grouped = _inproj(
            h, norm_mix[layer][None], w_in[layer].astype(BF16), cos_t, sin_t)
        attn, ret, (w1, w3, w2, w_o) = _mixers(
            (aq, ak, av), qkv_grouped, bias_rows, (rq, rk, rv, rg),
            _retention_tables(ret_decay_fwd[layer], ret_decay_bwd[layer]),
            (expert_w1[layer], expert_w3[layer], expert_w2[layer], w_out[layer]))

        rw = jnp.concatenate(
            [jnp.transpose(router_expert_w[layer], (1, 0, 2)).reshape(d, N_EXPERTS),
             router_group_w[layer],
             jnp.zeros((d, LANES - N_EXPERTS - N_GROUPS), F32)], axis=1).astype(BF16)
        rb = jnp.concatenate(
            [router_expert_b[layer].reshape(N_EXPERTS), router_group_b[layer],
             jnp.zeros((LANES - N_EXPERTS - N_GROUPS,), F32)])[None].astype(F32)
        h1, xs, dest, cnt = _outproj(
            attn.reshape(b * s, ATTN_WIDTH), ret.reshape(b * s, RET_WIDTH), h.reshape(b * s, d),
            w_o, attn_out_gain[layer][None], norm_ffn[layer][None],
            rw, rb)
        n_route_tiles = (b * s) // ROUTE_TM
        n_moe_tiles = (b * s + n_route_tiles * N_GROUPS * (CHUNK - 1)) // MOE_TM + N_GROUPS
        schedule = _dispatch_tables(cnt, n_moe_tiles)
        ys = _moe(xs, w1, w3, w2, *schedule)
        assert depth == 1, "the combine kernel fuses the final norm, so it must run on the last layer"
        h = _combine(ys, h1, dest, norm_final[None]).reshape(b, s, d)
    return h
```

```python
import functools
import math

import jax
import jax.numpy as jnp
from jax import lax
from jax.experimental import pallas as pl
from jax.experimental.pallas import tpu as pltpu

F32 = jnp.float32
BF16 = jnp.bfloat16

D_MODEL = 1024
HEAD_DIM = 64
ATTN_WIDTH = 512
RET_WIDTH = 512
N_HEADS = 8
PAIR = 2 * HEAD_DIM
N_PAIRS = N_HEADS // 2
ATTN_DILATIONS = (1, 4, 16)
ATTN_RADIUS = 64
N_BUCKETS = 32
REL_MAX_DIST = 1024
ROPE_BASE = 10000.0
N_GROUPS = 4
EXPERTS_PER_GROUP = 4
N_EXPERTS = 16
EXPERT_FF = 512
EPS = 1e-6
NEG_INF = -1e30
LOG2_E = math.log2(math.e)

LANES = 128
ATTN_TQ = 128
ATTN_W = 256
N_BIAS_VARIANTS = 8
RET_CHUNK = 256
ROUTE_TM = 512
ROUTE_TILES_PER_STEP = 2
COMBINE_TILES_PER_STEP = 2
COMBINE_BUFFERS = 3
SORT_ROWS = 592
CHUNK = 16
MOE_TM = 512
XS_W = D_MODEL + 2 * LANES
VMEM_LIMIT = 48 * 1024 * 1024


def _rms(x, gain):
    return x * lax.rsqrt(jnp.mean(x * x, axis=-1, keepdims=True) + EPS) * gain


def _inproj_kernel(x_ref, gain_ref, w_ref, cos_ref, sin_ref,
                   aq_ref, ak_ref, av_ref, rq_ref, rk_ref, rv_ref, rg_ref,
                   aq4_ref, ak4_ref, av4_ref, aq16_ref, ak16_ref, av16_ref, stage_ref, stage4_ref):
    tm = x_ref.shape[0]
    xn = _rms(x_ref[...], gain_ref[...]).astype(BF16)

    def stage(i, t, nat_ref):
        nat_ref[...] = t.astype(BF16)
        for hp in range(N_PAIRS):
            stage_ref[i, hp] = t[:, hp * PAIR:(hp + 1) * PAIR]

    def regroup4(i, d4_ref):
        for hp in range(N_PAIRS):
            for r4 in range(4):
                g4 = stage_ref[i, hp, pl.ds(r4, tm // 4, stride=4), :]
                d4_ref[hp, r4] = g4.astype(BF16)
                stage4_ref[i, hp, r4] = g4

    def regroup16(i, d16_ref):
        for hp in range(N_PAIRS):
            for r4 in range(4):
                for j in range(4):
                    d16_ref[hp, r4 + 4 * j] = (
                        stage4_ref[i, hp, r4, pl.ds(j, tm // 16, stride=4), :].astype(BF16))

    def seg(i):
        return jnp.dot(xn, w_ref[:, i * ATTN_WIDTH:(i + 1) * ATTN_WIDTH], preferred_element_type=F32)

    def rotary(t):
        cos, sin = cos_ref[...], sin_ref[...]
        first_half = (lax.broadcasted_iota(jnp.int32, (1, LANES), 1) % HEAD_DIM) < HEAD_DIM // 2
        outs = []
        for j in range(t.shape[1] // LANES):
            tj = t[:, j * LANES:(j + 1) * LANES]
            partner = jnp.where(first_half, pltpu.roll(tj, LANES - HEAD_DIM // 2, 1),
                                pltpu.roll(tj, HEAD_DIM // 2, 1))
            outs.append(tj * cos + partner * sin)
        return jnp.concatenate(outs, axis=1)

    stage(0, seg(0) * (HEAD_DIM ** -0.5 * LOG2_E), aq_ref)
    stage(1, seg(1), ak_ref)
    stage(2, seg(2), av_ref)
    rq_ref[...] = rotary(seg(3)).astype(BF16)
    for i, d4_ref in enumerate((aq4_ref, ak4_ref, av4_ref)):
        regroup4(i, d4_ref)
    rk_ref[...] = (rotary(seg(4)) * (HEAD_DIM ** -0.5)).astype(BF16)
    rv_ref[...] = seg(5).astype(BF16)
    for i, d16_ref in enumerate((aq16_ref, ak16_ref, av16_ref)):
        regroup16(i, d16_ref)
    rg_ref[...] = seg(6).astype(BF16)


def _inproj(x, gain, w_in, cos_t, sin_t, tm=512):
    b, s, d = x.shape
    n = w_in.shape[1]
    out = jax.ShapeDtypeStruct((b, s, ATTN_WIDTH), BF16)
    ospec = pl.BlockSpec((None, tm, ATTN_WIDTH), lambda si, bi: (bi, si, 0))

    def grouped(dil):
        shape = jax.ShapeDtypeStruct((b, s // tm, N_PAIRS, dil, tm // dil, PAIR), BF16)
        spec = pl.BlockSpec((None, None, N_PAIRS, dil, tm // dil, PAIR), lambda si, bi: (bi, si, 0, 0, 0, 0))
        return [shape] * 3, [spec] * 3

    shapes4, specs4 = grouped(4)
    shapes16, specs16 = grouped(16)
    outs = pl.pallas_call(
        _inproj_kernel,
        grid=(s // tm, b),
        in_specs=[
            pl.BlockSpec((None, tm, d), lambda si, bi: (bi, si, 0)),
            pl.BlockSpec((1, d), lambda si, bi: (0, 0)),
            pl.BlockSpec((d, n), lambda si, bi: (0, 0)),
            pl.BlockSpec((tm, LANES), lambda si, bi: (si, 0)),
            pl.BlockSpec((tm, LANES), lambda si, bi: (si, 0)),
        ],
        out_specs=[ospec] * 7 + specs4 + specs16,
        out_shape=[out] * 7 + shapes4 + shapes16,
        scratch_shapes=[pltpu.VMEM((3, N_PAIRS, tm, PAIR), F32),
                        pltpu.VMEM((3, N_PAIRS, 4, tm // 4, PAIR), F32)],
        compiler_params=pltpu.CompilerParams(
            dimension_semantics=("arbitrary", "arbitrary"), vmem_limit_bytes=VMEM_LIMIT),
        name="inproj",
    )(x, gain, w_in, cos_t, sin_t)
    return outs[:7], outs[7:]


def _t5_bucket(rel):
    half = N_BUCKETS // 2
    max_exact = half // 2
    offset = jnp.where(rel > 0, half, 0)
    n = jnp.abs(rel)
    nf = jnp.maximum(n, 1).astype(F32)
    large = max_exact + (jnp.log(nf / max_exact) / math.log(REL_MAX_DIST / max_exact)
                         * (half - max_exact)).astype(jnp.int32)
    large = jnp.minimum(large, half - 1)
    return offset + jnp.where(n < max_exact, n, large)


def _attn_bias_rows(rel_bias):
    period = 2 * ATTN_W
    band = 2 * ATTN_RADIUS + 1
    rel = jnp.arange(-ATTN_RADIUS, ATTN_RADIUS + 1)
    rows = []
    for dil, offs in ((1, (0, 64, 128)), (4, (0, 64, 128)), (16, (0, 128))):
        vals = rel_bias[_t5_bucket(rel * dil)].astype(F32).T * LOG2_E
        for off in offs:
            lo = off - ATTN_RADIUS
            pad = jnp.full((N_HEADS, period - band), NEG_INF, F32)
            if lo >= 0:
                row = jnp.concatenate([pad[:, :lo], vals, pad[:, lo:]], axis=1)
            else:
                row = jnp.concatenate([vals[:, -lo:], pad, vals[:, :-lo]], axis=1)
            rows.append(row)
    v = jnp.stack(rows, axis=1)
    return v.reshape(N_PAIRS, 2 * N_BIAS_VARIANTS, period)


def _attention_kernel(q_ref, k_ref, v_ref, q4_ref, k4_ref, v4_ref, q16_ref, k16_ref, v16_ref,
                      rows_ref, o_ref, bias_ref, acc_ref, m_ref, l_ref, out_ref):
    s = q_ref.shape[0]
    n_tiles = s // ATTN_TQ
    lane = lax.broadcasted_iota(jnp.int32, (1, PAIR), 1)
    head0 = lane < HEAD_DIM

    @pl.when(pl.program_id(1) == 0)
    def _():
        col = lax.broadcasted_iota(jnp.int32, (ATTN_TQ, ATTN_W), 1)
        for idx in range(2 * N_BIAS_VARIANTS):
            gen = jnp.broadcast_to(rows_ref[idx:idx + 1, :], (ATTN_TQ, 2 * ATTN_W))
            tab = pltpu.roll(gen, 0, 1, stride=1, stride_axis=0)[:, :ATTN_W]
            var = idx % N_BIAS_VARIANTS
            if var >= 6:
                tab = jnp.where((col // ATTN_TQ) == var - 6, tab, NEG_INF)
            head = idx // N_BIAS_VARIANTS
            bias_ref[var, head * ATTN_TQ:(head + 1) * ATTN_TQ, :] = tab

    def rows_of(ref, dil, lo, n):
        if dil == 1:
            return ref[lo:lo + n, :]
        piece, sub_len = ref.shape[2], s // dil
        parts, pos = [], lo
        while pos < lo + n:
            within = pos % sub_len
            off = within % piece
            take = min(piece - off, lo + n - pos)
            parts.append(ref[within // piece, pos // sub_len, off:off + take, :])
            pos += take
        return parts[0] if len(parts) == 1 else jnp.concatenate(parts, axis=0)

    def run_branch(bi, dil, qs_ref, ks_ref, vs_ref):
        sub_len = s // dil
        tiles_per_sub = sub_len // ATTN_TQ

        def tile(t):
            q0 = t * ATTN_TQ
            if tiles_per_sub == 1:
                ws = (t // 2) * ATTN_W
                var = 6 + t % 2
            else:
                pos = t % tiles_per_sub
                sub_lo = (t // tiles_per_sub) * sub_len
                ws = min(max(q0 - ATTN_RADIUS, sub_lo), sub_lo + sub_len - ATTN_W)
                var = (0 if pos == 0 else 2 if pos == tiles_per_sub - 1 else 1) + 3 * bi
            q = rows_of(qs_ref, dil, q0, ATTN_TQ)
            k = rows_of(ks_ref, dil, ws, ATTN_W)
            v = rows_of(vs_ref, dil, ws, ATTN_W)
            q2 = jnp.concatenate([jnp.where(head0, q, jnp.zeros_like(q)),
                                  jnp.where(head0, jnp.zeros_like(q), q)], axis=0)
            sc = lax.dot_general(q2, k, (((1,), (1,)), ((), ())), preferred_element_type=F32)
            sc = sc + bias_ref[var]
            m = jnp.max(sc, axis=-1, keepdims=True)
            p = jnp.exp2(sc - m).astype(BF16)
            o = jnp.dot(p, jnp.concatenate([v, jnp.ones_like(v)], axis=1), preferred_element_type=F32)
            l = o[:, PAIR:]
            outs, ms, ls = (o[:ATTN_TQ, :PAIR], o[ATTN_TQ:, :PAIR]), (m[:ATTN_TQ], m[ATTN_TQ:]), (l[:ATTN_TQ], l[ATTN_TQ:])
            if dil == 16:
                dst = pl.ds((t % 4) * (s // 4) + t // 4, ATTN_TQ, stride=4)
            else:
                dst = pl.ds(q0, ATTN_TQ)
            acc_ref[bi, dst, :] = jnp.where(head0, outs[0], outs[1])
            m_ref[bi, dst, :] = jnp.where(head0, ms[0], ms[1])
            l_ref[bi, dst, :] = jnp.where(head0, ls[0], ls[1])

        for t in range(n_tiles):
            tile(t)

    run_branch(0, 1, q_ref, k_ref, v_ref)
    run_branch(1, 4, q4_ref, k4_ref, v4_ref)
    run_branch(2, 16, q16_ref, k16_ref, v16_ref)

    rows = ATTN_TQ
    for r4 in range(4):
        for blk in range(s // (4 * rows)):
            nat = pl.ds(r4 + 4 * rows * blk, rows, stride=4)
            grouped4 = pl.ds(r4 * (s // 4) + rows * blk, rows)
            sl = (nat, grouped4, grouped4)
            m = [m_ref[bi, sl[bi], :] for bi in range(3)]
            mx = jnp.maximum(jnp.maximum(m[0], m[1]), m[2])
            num = jnp.zeros((rows, PAIR), F32)
            den = jnp.zeros((rows, PAIR), F32)
            for bi in range(3):
                e = jnp.exp2(m[bi] - mx)
                num = num + e * acc_ref[bi, sl[bi], :]
                den = den + e * l_ref[bi, sl[bi], :]
            out_ref[nat, :] = num / den
    o_ref[...] = out_ref[...].astype(o_ref.dtype)


def _retention_tables(decay_fwd, decay_bwd):
    c = RET_CHUNK
    lg_f = -jnp.exp(decay_fwd.astype(F32))
    lg_b = -jnp.exp(decay_bwd.astype(F32))
    idx = jnp.arange(c, dtype=F32)
    rel = idx[:, None] - idx[None, :]
    dmat = jnp.where(rel >= 0,
                     jnp.exp(lg_f[:, None, None] * jnp.maximum(rel, 0.0)[None]),
                     jnp.exp(lg_b[:, None, None] * jnp.maximum(-rel, 0.0)[None]))
    dmat = dmat.reshape(N_PAIRS, 2 * c, c)

    def lanes(v):
        v = v.reshape(N_PAIRS, 2, -1)
        return jnp.repeat(jnp.transpose(v, (0, 2, 1)), HEAD_DIM, axis=2)

    vec = jnp.stack([
        lanes(jnp.exp(lg_f[:, None] * (idx + 1.0)[None])),
        lanes(jnp.exp(lg_f[:, None] * (c - 1.0 - idx)[None])),
        lanes(jnp.exp(lg_b[:, None] * (c - idx)[None])),
        lanes(jnp.exp(lg_b[:, None] * idx[None])),
    ], axis=1)
    same_head = (jnp.arange(PAIR)[:, None] // HEAD_DIM) == (jnp.arange(PAIR)[None, :] // HEAD_DIM)
    cd = jnp.stack([lanes(jnp.exp(lg_f * c)[:, None]), lanes(jnp.exp(lg_b * c)[:, None])], axis=1)
    cd = jnp.transpose(cd, (0, 1, 3, 2)) * same_head[None, None].astype(F32)
    return dmat, vec, cd


def _retention_kernel(q_ref, k_ref, v_ref, g_ref, dmat_ref, vec_ref, cd_ref, o_ref, kv_ref, st_ref):
    s = q_ref.shape[0]
    c = RET_CHUNK
    nc = s // c
    lane = lax.broadcasted_iota(jnp.int32, (1, PAIR), 1)
    head0 = lane < HEAD_DIM
    same_head = ((lax.broadcasted_iota(jnp.int32, (PAIR, PAIR), 0) // HEAD_DIM)
                 == (lax.broadcasted_iota(jnp.int32, (PAIR, PAIR), 1) // HEAD_DIM))

    same_head2 = jnp.concatenate([same_head, same_head], axis=0)
    for n in range(nc):
        rows = slice(n * c, (n + 1) * c)
        kf = k_ref[rows, :].astype(F32)
        kcat = jnp.concatenate([(kf * vec_ref[1]).astype(BF16), (kf * vec_ref[3]).astype(BF16)], axis=1)
        kv = lax.dot_general(kcat, v_ref[rows, :], (((0,), (0,)), ((), ())), preferred_element_type=F32)
        kv_ref[n] = jnp.where(same_head2, kv, 0.0)

    state = jnp.zeros((PAIR, PAIR), F32)
    for n in range(nc):
        st_ref[n, :PAIR, :] = state.astype(BF16)
        state = state * cd_ref[0] + kv_ref[n, :PAIR, :]
    state = jnp.zeros((PAIR, PAIR), F32)
    for n in reversed(range(nc)):
        st_ref[n, PAIR:, :] = state.astype(BF16)
        state = state * cd_ref[1] + kv_ref[n, PAIR:, :]

    for n in range(nc):
        rows = slice(n * c, (n + 1) * c)
        q, k, v = q_ref[rows, :], k_ref[rows, :], v_ref[rows, :]
        q2 = jnp.concatenate([jnp.where(head0, q, jnp.zeros_like(q)),
                              jnp.where(head0, jnp.zeros_like(q), q)], axis=0)
        sc = lax.dot_general(q2, k, (((1,), (1,)), ((), ())), preferred_element_type=F32)
        intra = jnp.dot((sc * dmat_ref[...]).astype(BF16), v, preferred_element_type=F32)
        qf = q.astype(F32)
        qcat = jnp.concatenate([(qf * vec_ref[0]).astype(BF16), (qf * vec_ref[2]).astype(BF16)], axis=1)
        y = jnp.where(head0, intra[:c], intra[c:]) + jnp.dot(qcat, st_ref[n], preferred_element_type=F32)
        y2 = y * y
        ms0 = jnp.sum(jnp.where(head0, y2, 0.0), axis=-1, keepdims=True)
        ms1 = jnp.sum(jnp.where(head0, 0.0, y2), axis=-1, keepdims=True)
        ms = jnp.where(head0, ms0, ms1) * (1.0 / HEAD_DIM)
        g = g_ref[rows, :].astype(F32)
        o_ref[rows, :] = (y * lax.rsqrt(ms + EPS) * (g * jax.nn.sigmoid(g))).astype(o_ref.dtype)


def _mixers(qkv, qkv_grouped, bias_rows, ret_inputs, ret_tables, weights_f32):
    b, s, _ = qkv[0].shape
    c = RET_CHUNK
    n_steps = N_PAIRS * b
    spec = pl.BlockSpec((None, s, PAIR), lambda hp, bi: (bi, 0, hp))
    gspecs = [pl.BlockSpec((None, g.shape[1], None) + g.shape[3:], lambda hp, bi: (bi, 0, hp, 0, 0, 0))
              for g in qkv_grouped]
    slabs = [w.reshape(n_steps, -1, w.shape[-1]) for w in weights_f32]
    slab_specs = [pl.BlockSpec((None,) + w.shape[1:], lambda hp, bi: (hp * b + bi, 0, 0)) for w in slabs]
    n_attn_in, n_ret_in, n_w = 10, 7, len(slabs)
    attn_scratch = [
        pltpu.VMEM((N_BIAS_VARIANTS, 2 * ATTN_TQ, ATTN_W), F32),
        pltpu.VMEM((3, s, PAIR), F32),
        pltpu.VMEM((3, s, PAIR), F32),
        pltpu.VMEM((3, s, PAIR), F32),
        pltpu.VMEM((s, PAIR), F32),
    ]
    ret_scratch = [pltpu.VMEM((s // c, 2 * PAIR, PAIR), F32),
                   pltpu.VMEM((s // c, 2 * PAIR, PAIR), BF16)]

    def kernel(*refs):
        ins, rest = refs[:n_attn_in + n_ret_in + n_w], refs[n_attn_in + n_ret_in + n_w:]
        attn_o, ret_o, w_out, scratch = rest[0], rest[1], rest[2:2 + n_w], rest[2 + n_w:]
        _attention_kernel(*ins[:n_attn_in], attn_o, *scratch[:len(attn_scratch)])
        _retention_kernel(*ins[n_attn_in:n_attn_in + n_ret_in], ret_o, *scratch[len(attn_scratch):])
        for src, dst in zip(ins[n_attn_in + n_ret_in:], w_out):
            dst[...] = src[...].astype(BF16)

    outs = pl.pallas_call(
        kernel,
        grid=(N_PAIRS, b),
        in_specs=[spec] * 3 + gspecs
        + [pl.BlockSpec((None, 2 * N_BIAS_VARIANTS, 2 * ATTN_W), lambda hp, bi: (hp, 0, 0))]
        + [spec] * 4
        + [pl.BlockSpec((None, 2 * c, c), lambda hp, bi: (hp, 0, 0)),
           pl.BlockSpec((None, 4, c, PAIR), lambda hp, bi: (hp, 0, 0, 0)),
           pl.BlockSpec((None, 2, PAIR, PAIR), lambda hp, bi: (hp, 0, 0, 0))]
        + slab_specs,
        out_specs=[spec, spec] + slab_specs,
        out_shape=[jax.ShapeDtypeStruct((b, s, ATTN_WIDTH), BF16), jax.ShapeDtypeStruct((b, s, RET_WIDTH), BF16)]
        + [jax.ShapeDtypeStruct(w.shape, BF16) for w in slabs],
        scratch_shapes=attn_scratch + ret_scratch,
        compiler_params=pltpu.CompilerParams(
            dimension_semantics=("arbitrary", "arbitrary"), vmem_limit_bytes=VMEM_LIMIT),
        name="mixers",
    )(*qkv, *qkv_grouped, bias_rows, *ret_inputs, *ret_tables, *slabs)
    return outs[0], outs[1], [o.reshape(w.shape) for o, w in zip(outs[2:], weights_f32)]


def _route_tile(sub, attn_ref, ret_ref, x_ref, wo_ref, again_ref, fgain_ref,
                rw_ref, rb_ref, h_ref, xs_ref, dest_ref, cnt_ref):
    tm = ROUTE_TM
    rows = slice(sub * tm, (sub + 1) * tm)
    a = _rms(attn_ref[rows, :].astype(F32), again_ref[...]).astype(BF16)
    mixed = jnp.concatenate([a, ret_ref[rows, :]], axis=1)
    h = x_ref[rows, :] + jnp.dot(mixed, wo_ref[...], preferred_element_type=F32)
    h_ref[rows, :] = h.astype(h_ref.dtype)
    hn = _rms(h, fgain_ref[...]).astype(BF16)

    logits = jnp.dot(hn, rw_ref[...], preferred_element_type=F32) + rb_ref[...]
    lt = logits.T
    row = lambda i: lt[i:i + 1, :]

    def top1(vals):
        top = functools.reduce(jnp.maximum, vals)
        idx = jnp.full(top.shape, len(vals) - 1, jnp.int32)
        for i in reversed(range(len(vals) - 1)):
            idx = jnp.where(vals[i] == top, i, idx)
        return top, idx

    g_rows = [row(N_EXPERTS + g) for g in range(N_GROUPS)]
    gmax, grp = top1(g_rows)
    p_group = 1.0 / sum(jnp.exp(g - gmax) for g in g_rows)
    chosen = []
    for j in range(EXPERTS_PER_GROUP):
        e = row((N_GROUPS - 1) * EXPERTS_PER_GROUP + j)
        for g in reversed(range(N_GROUPS - 1)):
            e = jnp.where(grp == g, row(g * EXPERTS_PER_GROUP + j), e)
        chosen.append(e)
    v1, i1 = top1(chosen)
    v2, i2 = top1([jnp.where(i1 == j, -jnp.inf, e) for j, e in enumerate(chosen)])
    e2 = jnp.exp(v2 - v1)
    p1 = p_group / (1.0 + e2)
    inner = [jnp.where(i1 == j, p1, jnp.where(i2 == j, p1 * e2, 0.0)) for j in range(EXPERTS_PER_GROUP)]
    gate_rows = [jnp.where(grp == g, inner[j], 0.0)
                 for g in range(N_GROUPS) for j in range(EXPERTS_PER_GROUP)]

    sub8 = lax.broadcasted_iota(jnp.int32, (8, 1), 0)
    onehot_t = jnp.where(sub8 == grp, 1.0, 0.0)
    r_i = lax.broadcasted_iota(jnp.int32, (tm, tm), 0)
    c_i = lax.broadcasted_iota(jnp.int32, (tm, tm), 1)
    before = jnp.where(c_i < r_i, 1.0, 0.0).astype(BF16)
    rank_t = lax.dot_general(onehot_t.astype(BF16), before, (((1,), (1,)), ((), ())),
                             preferred_element_type=F32)
    cnt = jnp.sum(onehot_t, axis=1, keepdims=True)
    c16 = jnp.ceil(cnt * (1.0 / CHUNK)) * CHUNK
    start = jnp.zeros((8, 1), F32)
    for g in range(N_GROUPS - 1):
        start = start + jnp.where(sub8 > g, c16[g:g + 1, :], 0.0)
    dest_t = jnp.sum(onehot_t * (start + rank_t), axis=0, keepdims=True)
    cnt_ref[sub] = jnp.broadcast_to(jnp.where(sub8 < N_GROUPS, c16, pltpu.roll(start, N_GROUPS, 0)),
                                    (8, LANES)).astype(jnp.int32)

    cols = jnp.concatenate(gate_rows + [dest_t, jnp.zeros((LANES - N_EXPERTS - 1, tm), F32)], axis=0).T
    dest_ref[rows, :] = cols
    lane = lax.broadcasted_iota(jnp.int32, cols.shape, 1)
    gates = jnp.where(lane < N_EXPERTS, cols, 0.0)
    perm = jnp.where(lax.broadcasted_iota(jnp.int32, (SORT_ROWS, tm), 0) == dest_t.astype(jnp.int32),
                     1.0, 0.0).astype(BF16)
    g_hi = gates.astype(BF16)
    g_lo = (gates - g_hi.astype(F32)).astype(BF16)
    payload = jnp.concatenate([hn, g_hi, g_lo], axis=1)
    xs_ref[sub * SORT_ROWS:(sub + 1) * SORT_ROWS, :] = (
        jnp.dot(perm, payload, preferred_element_type=F32).astype(BF16))


def _outproj_kernel(*refs):
    for sub in range(ROUTE_TILES_PER_STEP):
        _route_tile(sub, *refs)


def _outproj(attn, ret, x, w_o, again, fgain, rw, rb):
    t, d = x.shape
    per_step = ROUTE_TILES_PER_STEP
    tm = ROUTE_TM * per_step
    nt = t // ROUTE_TM
    row = lambda w: pl.BlockSpec((tm, w), lambda i: (i, 0))
    full = lambda r, c: pl.BlockSpec((r, c), lambda i: (0, 0))
    return pl.pallas_call(
        _outproj_kernel,
        grid=(nt // per_step,),
        in_specs=[row(ATTN_WIDTH), row(RET_WIDTH), row(d), full(ATTN_WIDTH + RET_WIDTH, d),
                  full(1, ATTN_WIDTH), full(1, d), full(d, LANES), full(1, LANES)],
        out_specs=[row(d), pl.BlockSpec((per_step * SORT_ROWS, XS_W), lambda i: (i, 0)), row(LANES),
                   pl.BlockSpec((per_step, 8, LANES), lambda i: (i, 0, 0))],
        out_shape=[jax.ShapeDtypeStruct((t, d), BF16),
                   jax.ShapeDtypeStruct((nt * SORT_ROWS, XS_W), BF16),
                   jax.ShapeDtypeStruct((t, LANES), F32),
                   jax.ShapeDtypeStruct((nt, 8, LANES), jnp.int32)],
        compiler_params=pltpu.CompilerParams(
            dimension_semantics=("arbitrary",), vmem_limit_bytes=VMEM_LIMIT),
        name="outproj",
    )(attn, ret, x, w_o, again, fgain, rw, rb)


def _dispatch_tables(cnt, n_moe_tiles):
    nt = cnt.shape[0]
    cpt = MOE_TM // CHUNK

    n_slots = n_moe_tiles * cpt
    slot_rows = -(-n_slots // (8 * LANES)) * 8

    def schedule_kernel(seg_ref, grp_ref, used_ref, src_ref, tail_ref):
        slot = (lax.broadcasted_iota(jnp.int32, (slot_rows, LANES), 0) * LANES
                + lax.broadcasted_iota(jnp.int32, (slot_rows, LANES), 1))
        src = jnp.full((slot_rows, LANES), -1, jnp.int32)
        pos = jnp.int32(0)
        for g in range(N_GROUPS):
            def tile_body(i, carry):
                p, src = carry
                n = seg_ref[i, g] // CHUNK
                first = (i * SORT_ROWS + seg_ref[i, N_GROUPS + g]) // CHUNK
                src = jnp.where((slot >= p) & (slot < p + n), slot + (first - p), src)
                return p + n, src
            end, src = lax.fori_loop(0, nt, tile_body, (pos, src))
            padded = ((end + cpt - 1) // cpt) * cpt

            def mark_body(m, carry):
                grp_ref[m] = g
                used_ref[m] = 1
                return carry
            lax.fori_loop(pos // cpt, padded // cpt, mark_body, 0)
            pos = padded

        def idle_body(m, carry):
            grp_ref[m] = N_GROUPS - 1
            used_ref[m] = 0
            return carry
        lax.fori_loop(pos // cpt, n_moe_tiles, idle_body, 0)
        src_ref[...] = src

        def tail_body(i, carry):
            rows = seg_ref[i, 0]
            for g in range(1, N_GROUPS):
                rows = rows + seg_ref[i, g]
            tail_ref[i] = rows // CHUNK
            return carry
        lax.fori_loop(0, nt, tail_body, 0)

    smem = lambda: pl.BlockSpec(memory_space=pltpu.SMEM)
    tile_group, tile_used, src, tile_tail = pl.pallas_call(
        schedule_kernel,
        in_specs=[smem()],
        out_specs=[smem(), smem(), pl.BlockSpec(memory_space=pltpu.VMEM), smem()],
        out_shape=[jax.ShapeDtypeStruct((n_moe_tiles,), jnp.int32),
                   jax.ShapeDtypeStruct((n_moe_tiles,), jnp.int32),
                   jax.ShapeDtypeStruct((slot_rows, LANES), jnp.int32),
                   jax.ShapeDtypeStruct((nt,), jnp.int32)],
        name="schedule",
    )(cnt[:, :2 * N_GROUPS, 0])
    return tile_group, tile_used, src.reshape(-1)[:n_slots], tile_tail


def _moe_kernel(grp_ref, used_ref, src_ref, tail_ref, xs_hbm, w1_ref, w3_ref, w2_ref, ys_hbm,
                xbuf, obuf, zbuf, in_sem, out_sem, zero_sem):
    m = pl.program_id(0)
    n_tiles = pl.num_programs(0)
    cpt = MOE_TM // CHUNK
    cps = SORT_ROWS // CHUNK
    slot = m % 2
    zero_chunk = cps - 1

    def rows(c):
        return pl.ds(pl.multiple_of(c * CHUNK, CHUNK), CHUNK)

    def zero_tails(wait):
        def tile_body(i, carry):
            def body(c, carry2):
                cp = pltpu.make_async_copy(zbuf, ys_hbm.at[rows(i * cps + c), :], zero_sem)
                cp.wait() if wait else cp.start()
                return carry2
            lax.fori_loop(tail_ref[i], cps, body, 0)
            return carry
        lax.fori_loop(0, tail_ref.shape[0], tile_body, 0)

    def gather_start(tile, sl):
        for c in range(cpt):
            src = src_ref[tile * cpt + c]
            src = jnp.where(src < 0, zero_chunk, src)
            pltpu.make_async_copy(xs_hbm.at[rows(src), :], xbuf.at[sl, c * CHUNK:(c + 1) * CHUNK, :],
                                  in_sem.at[sl]).start(priority=c % 2)

    def gather_wait(sl):
        pltpu.make_async_copy(xs_hbm.at[0:MOE_TM, :], xbuf.at[sl], in_sem.at[sl]).wait()

    def scatter(tile, sl, wait):
        full = src_ref[tile * cpt + cpt - 1] >= 0

        def chunk_copy(c, src):
            return pltpu.make_async_copy(obuf.at[sl, c * CHUNK:(c + 1) * CHUNK, :], ys_hbm.at[rows(src), :],
                                         out_sem.at[sl])

        @pl.when(full)
        def _():
            if wait:
                pltpu.make_async_copy(obuf.at[sl], ys_hbm.at[0:MOE_TM, :], out_sem.at[sl]).wait()
            else:
                for c in range(cpt):
                    chunk_copy(c, src_ref[tile * cpt + c]).start(priority=c % 2)

        @pl.when(jnp.logical_not(full))
        def _():
            for c in range(cpt):
                src = src_ref[tile * cpt + c]

                @pl.when(src >= 0)
                def _():
                    cp = chunk_copy(c, src)
                    cp.wait() if wait else cp.start()

    @pl.when(m == 0)
    def _():
        gather_start(0, 0)
        zbuf[...] = jnp.zeros(zbuf.shape, zbuf.dtype)
        zero_tails(False)

    prev_used = used_ref[jnp.maximum(m - 1, 0)] > 0

    @pl.when((m == 0) | prev_used)
    def _():
        gather_wait(slot)

    @pl.when((m >= 2) & (used_ref[jnp.maximum(m - 2, 0)] > 0))
    def _():
        scatter(m - 2, slot, True)

    def expert_tile(n_rows):
        gather_start(jnp.minimum(m + 1, n_tiles - 1), 1 - slot)
        x = xbuf[slot, :n_rows, :D_MODEL]
        gate = (xbuf[slot, :n_rows, D_MODEL:D_MODEL + LANES].astype(F32)
                + xbuf[slot, :n_rows, D_MODEL + LANES:].astype(F32))
        lane = lax.broadcasted_iota(jnp.int32, gate.shape, 1)
        base = grp_ref[m] * EXPERTS_PER_GROUP
        acc = jnp.zeros((n_rows, D_MODEL), F32)
        for j in range(EXPERTS_PER_GROUP):
            a = jnp.dot(x, w1_ref[j], preferred_element_type=F32)
            b = jnp.dot(x, w3_ref[j], preferred_element_type=F32)
            gj = jnp.sum(jnp.where(lane == base + j, gate, 0.0), axis=-1, keepdims=True)
            hid = (a * jax.nn.sigmoid(a) * b * gj).astype(BF16)
            acc = acc + jnp.dot(hid, w2_ref[j], preferred_element_type=F32)
        obuf[slot, :n_rows, :] = acc.astype(BF16)
        scatter(m, slot, False)

    half_empty = src_ref[m * cpt + cpt // 2] < 0

    @pl.when((used_ref[m] > 0) & jnp.logical_not(half_empty))
    def _():
        expert_tile(MOE_TM)

    @pl.when((used_ref[m] > 0) & half_empty)
    def _():
        expert_tile(MOE_TM // 2)

    @pl.when(m == n_tiles - 1)
    def _():
        zero_tails(True)

        @pl.when(used_ref[m] > 0)
        def _():
            gather_wait(1 - slot)
            scatter(m, slot, True)

        @pl.when((m >= 1) & prev_used)
        def _():
            scatter(m - 1, 1 - slot, True)


def _moe(xs, w1, w3, w2, tile_group, tile_used, src_chunk, tile_tail):
    n_moe_tiles = tile_group.shape[0]
    rows = xs.shape[0]
    d = D_MODEL
    wspec = lambda r, c: pl.BlockSpec((EXPERTS_PER_GROUP, r, c), lambda m, grp, *_: (grp[m], 0, 0))
    return pl.pallas_call(
        _moe_kernel,
        grid_spec=pltpu.PrefetchScalarGridSpec(
            num_scalar_prefetch=4,
            grid=(n_moe_tiles,),
            in_specs=[pl.BlockSpec(memory_space=pl.ANY),
                      wspec(d, EXPERT_FF), wspec(d, EXPERT_FF), wspec(EXPERT_FF, d)],
            out_specs=pl.BlockSpec(memory_space=pl.ANY),
            scratch_shapes=[pltpu.VMEM((2, MOE_TM, XS_W), BF16),
                            pltpu.VMEM((2, MOE_TM, d), BF16),
                            pltpu.VMEM((CHUNK, d), BF16),
                            pltpu.SemaphoreType.DMA((2,)),
                            pltpu.SemaphoreType.DMA((2,)),
                            pltpu.SemaphoreType.DMA(())]),
        out_shape=jax.ShapeDtypeStruct((rows, d), BF16),
        compiler_params=pltpu.CompilerParams(
            dimension_semantics=("arbitrary",), vmem_limit_bytes=VMEM_LIMIT),
        name="moe",
    )(tile_group, tile_used, src_chunk, tile_tail, xs, w1, w3, w2)


def _combine_kernel(ys_hbm, h_hbm, dest_ref, gain_ref, o_ref, ysbuf, hbuf, ys_sem, h_sem):
    tm = ROUTE_TM
    per_step = o_ref.shape[0] // tm
    i = pl.program_id(0)
    n = h_hbm.shape[0] // (per_step * tm)
    ahead = min(COMBINE_BUFFERS - 1, n - 1)

    def copies(step, slot):
        ys_rows = pl.ds(pl.multiple_of(step * (per_step * SORT_ROWS), CHUNK), per_step * SORT_ROWS)
        h_rows = pl.ds(pl.multiple_of(step * (per_step * tm), tm), per_step * tm)
        return (pltpu.make_async_copy(ys_hbm.at[ys_rows, :], ysbuf.at[slot], ys_sem.at[slot]),
                pltpu.make_async_copy(h_hbm.at[h_rows, :], hbuf.at[slot], h_sem.at[slot]))

    @pl.when(i == 0)
    def _():
        for step in range(ahead):
            for cp in copies(step, step):
                cp.start()

    @pl.when(i + ahead < n)
    def _():
        for cp in copies(i + ahead, (i + ahead) % COMBINE_BUFFERS):
            cp.start()

    slot = i % COMBINE_BUFFERS
    for cp in copies(i, slot):
        cp.wait()
    for sub in range(per_step):
        rows = slice(sub * tm, (sub + 1) * tm)
        dest = dest_ref[rows, N_EXPERTS:N_EXPERTS + 1].astype(jnp.int32)
        perm_t = jnp.where(lax.broadcasted_iota(jnp.int32, (tm, SORT_ROWS), 1) == dest, 1.0, 0.0).astype(BF16)
        moe = jnp.dot(perm_t, ysbuf[slot, sub * SORT_ROWS:(sub + 1) * SORT_ROWS, :], preferred_element_type=F32)
        o_ref[rows, :] = _rms(hbuf[slot, rows, :].astype(F32) + moe, gain_ref[...])


def _combine(ys, h, dest, gain):
    t, d = h.shape
    per_step = COMBINE_TILES_PER_STEP
    tm = ROUTE_TM * per_step
    return pl.pallas_call(
        _combine_kernel,
        grid=(t // tm,),
        in_specs=[pl.BlockSpec(memory_space=pl.ANY),
                  pl.BlockSpec(memory_space=pl.ANY),
                  pl.BlockSpec((tm, LANES), lambda i: (i, 0)),
                  pl.BlockSpec((1, d), lambda i: (0, 0))],
        out_specs=pl.BlockSpec((tm, d), lambda i: (i, 0)),
        out_shape=jax.ShapeDtypeStruct((t, d), F32),
        scratch_shapes=[pltpu.VMEM((COMBINE_BUFFERS, per_step * SORT_ROWS, d), BF16),
                        pltpu.VMEM((COMBINE_BUFFERS, tm, d), BF16),
                        pltpu.SemaphoreType.DMA((COMBINE_BUFFERS,)),
                        pltpu.SemaphoreType.DMA((COMBINE_BUFFERS,))],
        compiler_params=pltpu.CompilerParams(
            dimension_semantics=("arbitrary",), vmem_limit_bytes=VMEM_LIMIT),
        name="combine",
    )(ys, h, dest, gain)


def _rotary_tables(s):
    half = HEAD_DIM // 2
    inv = ROPE_BASE ** (-jnp.arange(half, dtype=F32) / half)
    ang = jnp.arange(s, dtype=F32)[:, None] * inv[None, :]
    cos, sin = jnp.cos(ang), jnp.sin(ang)
    cos_t = jnp.tile(jnp.concatenate([cos, cos], axis=-1), (1, LANES // HEAD_DIM))
    sin_t = jnp.tile(jnp.concatenate([-sin, sin], axis=-1), (1, LANES // HEAD_DIM))
    return cos_t, sin_t


def kernel(x, w_in, w_out, norm_mix, norm_ffn, norm_final, attn_out_gain, rel_bias, ret_decay_fwd, ret_decay_bwd, router_group_w, router_group_b, router_expert_w, router_expert_b, expert_w1, expert_w3, expert_w2):
    b, s, d = x.shape
    depth = w_in.shape[0]
    cos_t, sin_t = _rotary_tables(s)
    bias_rows = _attn_bias_rows(rel_bias)
    h = x
    for layer in range(depth):
        (aq, ak, av, rq, rk, rv, rg), qkv_grouped = _inproj(
            h, norm_mix[layer][None], w_in[layer].astype(BF16), cos_t, sin_t)
        attn, ret, (w1, w3, w2, w_o) = _mixers(
            (aq, ak, av), qkv_grouped, bias_rows, (rq, rk, rv, rg),
            _retention_tables(ret_decay_fwd[layer], ret_decay_bwd[layer]),
            (expert_w1[layer], expert_w3[layer], expert_w2[layer], w_out[layer]))

        rw = jnp.concatenate(
            [jnp.transpose(router_expert_w[layer], (1, 0, 2)).reshape(d, N_EXPERTS),
             router_group_w[layer],
             jnp.zeros((d, LANES - N_EXPERTS - N_GROUPS), F32)], axis=1).astype(BF16)
        rb = jnp.concatenate(
            [router_expert_b[layer].reshape(N_EXPERTS), router_group_b[layer],
             jnp.zeros((LANES - N_EXPERTS - N_GROUPS,), F32)])[None].astype(F32)
        h1, xs, dest, cnt = _outproj(
            attn.reshape(b * s, ATTN_WIDTH), ret.reshape(b * s, RET_WIDTH), h.reshape(b * s, d),
            w_o, attn_out_gain[layer][None], norm_ffn[layer][None],
            rw, rb)
        n_route_tiles = (b * s) // ROUTE_TM
        n_moe_tiles = (b * s + n_route_tiles * N_GROUPS * (CHUNK - 1)) // MOE_TM + N_GROUPS
        schedule = _dispatch_tables(cnt, n_moe_tiles)
        ys = _moe(xs, w1, w3, w2, *schedule)
        assert depth == 1, "the combine kernel fuses the final norm, so it must run on the last layer"
        h = _combine(ys, h1, dest, norm_final[None]).reshape(b, s, d)
    return h
```

```python
import functools
import math

import jax
import jax.numpy as jnp
from jax import lax
from jax.experimental import pallas as pl
from jax.experimental.pallas import tpu as pltpu

F32 = jnp.float32
BF16 = jnp.bfloat16

D_MODEL = 1024
HEAD_DIM = 64
ATTN_WIDTH = 512
RET_WIDTH = 512
N_HEADS = 8
PAIR = 2 * HEAD_DIM
N_PAIRS = N_HEADS // 2
ATTN_DILATIONS = (1, 4, 16)
ATTN_RADIUS = 64
N_BUCKETS = 32
REL_MAX_DIST = 1024
ROPE_BASE = 10000.0
N_GROUPS = 4
EXPERTS_PER_GROUP = 4
N_EXPERTS = 16
EXPERT_FF = 512
EPS = 1e-6
NEG_INF = -1e30
LOG2_E = math.log2(math.e)

LANES = 128
ATTN_TQ = 128
ATTN_W = 256
N_BIAS_VARIANTS = 8
RET_CHUNK = 256
ROUTE_TM = 512
ROUTE_TILES_PER_STEP = 2
COMBINE_TILES_PER_STEP = 2
COMBINE_BUFFERS = 3
SORT_ROWS = 592
CHUNK = 16
MOE_TM = 512
XS_W = D_MODEL + 2 * LANES
VMEM_LIMIT = 48 * 1024 * 1024


def _rms(x, gain):
    return x * lax.rsqrt(jnp.mean(x * x, axis=-1, keepdims=True) + EPS) * gain


def _inproj_kernel(x_ref, gain_ref, w_ref, cos_ref, sin_ref,
                   aq_ref, ak_ref, av_ref, rq_ref, rk_ref, rv_ref, rg_ref,
                   aq4_ref, ak4_ref, av4_ref, aq16_ref, ak16_ref, av16_ref, stage_ref, stage4_ref):
    tm = x_ref.shape[0]
    xn = _rms(x_ref[...], gain_ref[...]).astype(BF16)

    def stage(i, t, nat_ref):
        nat_ref[...] = t.astype(BF16)
        for hp in range(N_PAIRS):
            stage_ref[i, hp] = t[:, hp * PAIR:(hp + 1) * PAIR]

    def regroup4(i, d4_ref):
        for hp in range(N_PAIRS):
            for r4 in range(4):
                g4 = stage_ref[i, hp, pl.ds(r4, tm // 4, stride=4), :]
                d4_ref[hp, r4] = g4.astype(BF16)
                stage4_ref[i, hp, r4] = g4

    def regroup16(i, d16_ref):
        for hp in range(N_PAIRS):
            for r4 in range(4):
                for j in range(4):
                    d16_ref[hp, r4 + 4 * j] = (
                        stage4_ref[i, hp, r4, pl.ds(j, tm // 16, stride=4), :].astype(BF16))

    def seg(i):
        return jnp.dot(xn, w_ref[:, i * ATTN_WIDTH:(i + 1) * ATTN_WIDTH], preferred_element_type=F32)

    def rotary(t):
        cos, sin = cos_ref[...], sin_ref[...]
        first_half = (lax.broadcasted_iota(jnp.int32, (1, LANES), 1) % HEAD_DIM) < HEAD_DIM // 2
        outs = []
        for j in range(t.shape[1] // LANES):
            tj = t[:, j * LANES:(j + 1) * LANES]
            partner = jnp.where(first_half, pltpu.roll(tj, LANES - HEAD_DIM // 2, 1),
                                pltpu.roll(tj, HEAD_DIM // 2, 1))
            outs.append(tj * cos + partner * sin)
        return jnp.concatenate(outs, axis=1)

    stage(0, seg(0) * (HEAD_DIM ** -0.5 * LOG2_E), aq_ref)
    stage(1, seg(1), ak_ref)
    stage(2, seg(2), av_ref)
    rq_ref[...] = rotary(seg(3)).astype(BF16)
    for i, d4_ref in enumerate((aq4_ref, ak4_ref, av4_ref)):
        regroup4(i, d4_ref)
    rk_ref[...] = (rotary(seg(4)) * (HEAD_DIM ** -0.5)).astype(BF16)
    rv_ref[...] = seg(5).astype(BF16)
    for i, d16_ref in enumerate((aq16_ref, ak16_ref, av16_ref)):
        regroup16(i, d16_ref)
    rg_ref[...] = seg(6).astype(BF16)


def _inproj(x, gain, w_in, cos_t, sin_t, tm=512):
    b, s, d = x.shape
    n = w_in.shape[1]
    out = jax.ShapeDtypeStruct((b, s, ATTN_WIDTH), BF16)
    ospec = pl.BlockSpec((None, tm, ATTN_WIDTH), lambda si, bi: (bi, si, 0))

    def grouped(dil):
        shape = jax.ShapeDtypeStruct((b, s // tm, N_PAIRS, dil, tm // dil, PAIR), BF16)
        spec = pl.BlockSpec((None, None, N_PAIRS, dil, tm // dil, PAIR), lambda si, bi: (bi, si, 0, 0, 0, 0))
        return [shape] * 3, [spec] * 3

    shapes4, specs4 = grouped(4)
    shapes16, specs16 = grouped(16)
    outs = pl.pallas_call(
        _inproj_kernel,
        grid=(s // tm, b),
        in_specs=[
            pl.BlockSpec((None, tm, d), lambda si, bi: (bi, si, 0)),
            pl.BlockSpec((1, d), lambda si, bi: (0, 0)),
            pl.BlockSpec((d, n), lambda si, bi: (0, 0)),
            pl.BlockSpec((tm, LANES), lambda si, bi: (si, 0)),
            pl.BlockSpec((tm, LANES), lambda si, bi: (si, 0)),
        ],
        out_specs=[ospec] * 7 + specs4 + specs16,
        out_shape=[out] * 7 + shapes4 + shapes16,
        scratch_shapes=[pltpu.VMEM((3, N_PAIRS, tm, PAIR), F32),
                        pltpu.VMEM((3, N_PAIRS, 4, tm // 4, PAIR), F32)],
        compiler_params=pltpu.CompilerParams(
            dimension_semantics=("arbitrary", "arbitrary"), vmem_limit_bytes=VMEM_LIMIT),
        name="inproj",
    )(x, gain, w_in, cos_t, sin_t)
    return outs[:7], outs[7:]


def _t5_bucket(rel):
    half = N_BUCKETS // 2
    max_exact = half // 2
    offset = jnp.where(rel > 0, half, 0)
    n = jnp.abs(rel)
    nf = jnp.maximum(n, 1).astype(F32)
    large = max_exact + (jnp.log(nf / max_exact) / math.log(REL_MAX_DIST / max_exact)
                         * (half - max_exact)).astype(jnp.int32)
    large = jnp.minimum(large, half - 1)
    return offset + jnp.where(n < max_exact, n, large)


def _attn_bias_rows(rel_bias):
    period = 2 * ATTN_W
    band = 2 * ATTN_RADIUS + 1
    rel = jnp.arange(-ATTN_RADIUS, ATTN_RADIUS + 1)
    rows = []
    for dil, offs in ((1, (0, 64, 128)), (4, (0, 64, 128)), (16, (0, 128))):
        vals = rel_bias[_t5_bucket(rel * dil)].astype(F32).T * LOG2_E
        for off in offs:
            lo = off - ATTN_RADIUS
            pad = jnp.full((N_HEADS, period - band), NEG_INF, F32)
            if lo >= 0:
                row = jnp.concatenate([pad[:, :lo], vals, pad[:, lo:]], axis=1)
            else:
                row = jnp.concatenate([vals[:, -lo:], pad, vals[:, :-lo]], axis=1)
            rows.append(row)
    v = jnp.stack(rows, axis=1)
    return v.reshape(N_PAIRS, 2 * N_BIAS_VARIANTS, period)


def _attention_kernel(q_ref, k_ref, v_ref, q4_ref, k4_ref, v4_ref, q16_ref, k16_ref, v16_ref,
                      rows_ref, o_ref, bias_ref, acc_ref, m_ref, l_ref, out_ref):
    s = q_ref.shape[0]
    n_tiles = s // ATTN_TQ
    lane = lax.broadcasted_iota(jnp.int32, (1, PAIR), 1)
    head0 = lane < HEAD_DIM

    @pl.when(pl.program_id(1) == 0)
    def _():
        col = lax.broadcasted_iota(jnp.int32, (ATTN_TQ, ATTN_W), 1)
        for idx in range(2 * N_BIAS_VARIANTS):
            gen = jnp.broadcast_to(rows_ref[idx:idx + 1, :], (ATTN_TQ, 2 * ATTN_W))
            tab = pltpu.roll(gen, 0, 1, stride=1, stride_axis=0)[:, :ATTN_W]
            var = idx % N_BIAS_VARIANTS
            if var >= 6:
                tab = jnp.where((col // ATTN_TQ) == var - 6, tab, NEG_INF)
            head = idx // N_BIAS_VARIANTS
            bias_ref[var, head * ATTN_TQ:(head + 1) * ATTN_TQ, :] = tab

    def rows_of(ref, dil, lo, n):
        if dil == 1:
            return ref[lo:lo + n, :]
        piece, sub_len = ref.shape[2], s // dil
        parts, pos = [], lo
        while pos < lo + n:
            within = pos % sub_len
            off = within % piece
            take = min(piece - off, lo + n - pos)
            parts.append(ref[within // piece, pos // sub_len, off:off + take, :])
            pos += take
        return parts[0] if len(parts) == 1 else jnp.concatenate(parts, axis=0)

    def run_branch(bi, dil, qs_ref, ks_ref, vs_ref):
        sub_len = s // dil
        tiles_per_sub = sub_len // ATTN_TQ

        def tile(t):
            q0 = t * ATTN_TQ
            if tiles_per_sub == 1:
                ws = (t // 2) * ATTN_W
                var = 6 + t % 2
            else:
                pos = t % tiles_per_sub
                sub_lo = (t // tiles_per_sub) * sub_len
                ws = min(max(q0 - ATTN_RADIUS, sub_lo), sub_lo + sub_len - ATTN_W)
                var = (0 if pos == 0 else 2 if pos == tiles_per_sub - 1 else 1) + 3 * bi
            q = rows_of(qs_ref, dil, q0, ATTN_TQ)
            k = rows_of(ks_ref, dil, ws, ATTN_W)
            v = rows_of(vs_ref, dil, ws, ATTN_W)
            q2 = jnp.concatenate([jnp.where(head0, q, jnp.zeros_like(q)),
                                  jnp.where(head0, jnp.zeros_like(q), q)], axis=0)
            sc = lax.dot_general(q2, k, (((1,), (1,)), ((), ())), preferred_element_type=F32)
            sc = sc + bias_ref[var]
            m = jnp.max(sc, axis=-1, keepdims=True)
            p = jnp.exp2(sc - m).astype(BF16)
            o = jnp.dot(p, jnp.concatenate([v, jnp.ones_like(v)], axis=1), preferred_element_type=F32)
            l = o[:, PAIR:]
            outs, ms, ls = (o[:ATTN_TQ, :PAIR], o[ATTN_TQ:, :PAIR]), (m[:ATTN_TQ], m[ATTN_TQ:]), (l[:ATTN_TQ], l[ATTN_TQ:])
            if dil == 16:
                dst = pl.ds((t % 4) * (s // 4) + t // 4, ATTN_TQ, stride=4)
            else:
                dst = pl.ds(q0, ATTN_TQ)
            acc_ref[bi, dst, :] = jnp.where(head0, outs[0], outs[1])
            m_ref[bi, dst, :] = jnp.where(head0, ms[0], ms[1])
            l_ref[bi, dst, :] = jnp.where(head0, ls[0], ls[1])

        for t in range(n_tiles):
            tile(t)

    run_branch(0, 1, q_ref, k_ref, v_ref)
    run_branch(1, 4, q4_ref, k4_ref, v4_ref)
    run_branch(2, 16, q16_ref, k16_ref, v16_ref)

    rows = ATTN_TQ
    for r4 in range(4):
        for blk in range(s // (4 * rows)):
            nat = pl.ds(r4 + 4 * rows * blk, rows, stride=4)
            grouped4 = pl.ds(r4 * (s // 4) + rows * blk, rows)
            sl = (nat, grouped4, grouped4)
            m = [m_ref[bi, sl[bi], :] for bi in range(3)]
            mx = jnp.maximum(jnp.maximum(m[0], m[1]), m[2])
            num = jnp.zeros((rows, PAIR), F32)
            den = jnp.zeros((rows, PAIR), F32)
            for bi in range(3):
                e = jnp.exp2(m[bi] - mx)
                num = num + e * acc_ref[bi, sl[bi], :]
                den = den + e * l_ref[bi, sl[bi], :]
            out_ref[nat, :] = num / den
    o_ref[...] = out_ref[...].astype(o_ref.dtype)


def _retention_tables(decay_fwd, decay_bwd):
    c = RET_CHUNK
    lg_f = -jnp.exp(decay_fwd.astype(F32))
    lg_b = -jnp.exp(decay_bwd.astype(F32))
    idx = jnp.arange(c, dtype=F32)
    rel = idx[:, None] - idx[None, :]
    dmat = jnp.where(rel >= 0,
                     jnp.exp(lg_f[:, None, None] * jnp.maximum(rel, 0.0)[None]),
                     jnp.exp(lg_b[:, None, None] * jnp.maximum(-rel, 0.0)[None]))
    dmat = dmat.reshape(N_PAIRS, 2 * c, c)

    def lanes(v):
        v = v.reshape(N_PAIRS, 2, -1)
        return jnp.repeat(jnp.transpose(v, (0, 2, 1)), HEAD_DIM, axis=2)

    vec = jnp.stack([
        lanes(jnp.exp(lg_f[:, None] * (idx + 1.0)[None])),
        lanes(jnp.exp(lg_f[:, None] * (c - 1.0 - idx)[None])),
        lanes(jnp.exp(lg_b[:, None] * (c - idx)[None])),
        lanes(jnp.exp(lg_b[:, None] * idx[None])),
    ], axis=1)
    same_head = (jnp.arange(PAIR)[:, None] // HEAD_DIM) == (jnp.arange(PAIR)[None, :] // HEAD_DIM)
    cd = jnp.stack([lanes(jnp.exp(lg_f * c)[:, None]), lanes(jnp.exp(lg_b * c)[:, None])], axis=1)
    cd = jnp.transpose(cd, (0, 1, 3, 2)) * same_head[None, None].astype(F32)
    return dmat, vec, cd


def _retention_kernel(q_ref, k_ref, v_ref, g_ref, dmat_ref, vec_ref, cd_ref, o_ref, kv_ref, st_ref):
    s = q_ref.shape[0]
    c = RET_CHUNK
    nc = s // c
    lane = lax.broadcasted_iota(jnp.int32, (1, PAIR), 1)
    head0 = lane < HEAD_DIM
    same_head = ((lax.broadcasted_iota(jnp.int32, (PAIR, PAIR), 0) // HEAD_DIM)
                 == (lax.broadcasted_iota(jnp.int32, (PAIR, PAIR), 1) // HEAD_DIM))

    same_head2 = jnp.concatenate([same_head, same_head], axis=0)
    for n in range(nc):
        rows = slice(n * c, (n + 1) * c)
        kf = k_ref[rows, :].astype(F32)
        kcat = jnp.concatenate([(kf * vec_ref[1]).astype(BF16), (kf * vec_ref[3]).astype(BF16)], axis=1)
        kv = lax.dot_general(kcat, v_ref[rows, :], (((0,), (0,)), ((), ())), preferred_element_type=F32)
        kv_ref[n] = jnp.where(same_head2, kv, 0.0)

    state = jnp.zeros((PAIR, PAIR), F32)
    for n in range(nc):
        st_ref[n, :PAIR, :] = state.astype(BF16)
        state = state * cd_ref[0] + kv_ref[n, :PAIR, :]
    state = jnp.zeros((PAIR, PAIR), F32)
    for n in reversed(range(nc)):
        st_ref[n, PAIR:, :] = state.astype(BF16)
        state = state * cd_ref[1] + kv_ref[n, PAIR:, :]

    for n in range(nc):
        rows = slice(n * c, (n + 1) * c)
        q, k, v = q_ref[rows, :], k_ref[rows, :], v_ref[rows, :]
        q2 = jnp.concatenate([jnp.where(head0, q, jnp.zeros_like(q)),
                              jnp.where(head0, jnp.zeros_like(q), q)], axis=0)
        sc = lax.dot_general(q2, k, (((1,), (1,)), ((), ())), preferred_element_type=F32)
        intra = jnp.dot((sc * dmat_ref[...]).astype(BF16), v, preferred_element_type=F32)
        qf = q.astype(F32)
        qcat = jnp.concatenate([(qf * vec_ref[0]).astype(BF16), (qf * vec_ref[2]).astype(BF16)], axis=1)
        y = jnp.where(head0, intra[:c], intra[c:]) + jnp.dot(qcat, st_ref[n], preferred_element_type=F32)
        y2 = y * y
        ms0 = jnp.sum(jnp.where(head0, y2, 0.0), axis=-1, keepdims=True)
        ms1 = jnp.sum(jnp.where(head0, 0.0, y2), axis=-1, keepdims=True)
        ms = jnp.where(head0, ms0, ms1) * (1.0 / HEAD_DIM)
        g = g_ref[rows, :].astype(F32)
        o_ref[rows, :] = (y * lax.rsqrt(ms + EPS) * (g * jax.nn.sigmoid(g))).astype(o_ref.dtype)


def _mixers(qkv, qkv_grouped, bias_rows, ret_inputs, ret_tables, weights_f32):
    b, s, _ = qkv[0].shape
    c = RET_CHUNK
    n_steps = N_PAIRS * b
    spec = pl.BlockSpec((None, s, PAIR), lambda hp, bi: (bi, 0, hp))
    gspecs = [pl.BlockSpec((None, g.shape[1], None) + g.shape[3:], lambda hp, bi: (bi, 0, hp, 0, 0, 0))
              for g in qkv_grouped]
    slabs = [w.reshape(n_steps, -1, w.shape[-1]) for w in weights_f32]
    slab_specs = [pl.BlockSpec((None,) + w.shape[1:], lambda hp, bi: (hp * b + bi, 0, 0)) for w in slabs]
    n_attn_in, n_ret_in, n_w = 10, 7, len(slabs)
    attn_scratch = [
        pltpu.VMEM((N_BIAS_VARIANTS, 2 * ATTN_TQ, ATTN_W), F32),
        pltpu.VMEM((3, s, PAIR), F32),
        pltpu.VMEM((3, s, PAIR), F32),
        pltpu.VMEM((3, s, PAIR), F32),
        pltpu.VMEM((s, PAIR), F32),
    ]
    ret_scratch = [pltpu.VMEM((s // c, 2 * PAIR, PAIR), F32),
                   pltpu.VMEM((s // c, 2 * PAIR, PAIR), BF16)]

    def kernel(*refs):
        ins, rest = refs[:n_attn_in + n_ret_in + n_w], refs[n_attn_in + n_ret_in + n_w:]
        attn_o, ret_o, w_out, scratch = rest[0], rest[1], rest[2:2 + n_w], rest[2 + n_w:]
        _attention_kernel(*ins[:n_attn_in], attn_o, *scratch[:len(attn_scratch)])
        _retention_kernel(*ins[n_attn_in:n_attn_in + n_ret_in], ret_o, *scratch[len(attn_scratch):])
        for src, dst in zip(ins[n_attn_in + n_ret_in:], w_out):
            dst[...] = src[...].astype(BF16)

    outs = pl.pallas_call(
        kernel,
        grid=(N_PAIRS, b),
        in_specs=[spec] * 3 + gspecs
        + [pl.BlockSpec((None, 2 * N_BIAS_VARIANTS, 2 * ATTN_W), lambda hp, bi: (hp, 0, 0))]
        + [spec] * 4
        + [pl.BlockSpec((None, 2 * c, c), lambda hp, bi: (hp, 0, 0)),
           pl.BlockSpec((None, 4, c, PAIR), lambda hp, bi: (hp, 0, 0, 0)),
           pl.BlockSpec((None, 2, PAIR, PAIR), lambda hp, bi: (hp, 0, 0, 0))]
        + slab_specs,
        out_specs=[spec, spec] + slab_specs,
        out_shape=[jax.ShapeDtypeStruct((b, s, ATTN_WIDTH), BF16), jax.ShapeDtypeStruct((b, s, RET_WIDTH), BF16)]
        + [jax.ShapeDtypeStruct(w.shape, BF16) for w in slabs],
        scratch_shapes=attn_scratch + ret_scratch,
        compiler_params=pltpu.CompilerParams(
            dimension_semantics=("arbitrary", "arbitrary"), vmem_limit_bytes=VMEM_LIMIT),
        name="mixers",
    )(*qkv, *qkv_grouped, bias_rows, *ret_inputs, *ret_tables, *slabs)
    return outs[0], outs[1], [o.reshape(w.shape) for o, w in zip(outs[2:], weights_f32)]


def _route_tile(sub, attn_ref, ret_ref, x_ref, wo_ref, again_ref, fgain_ref,
                rw_ref, rb_ref, h_ref, xs_ref, dest_ref, cnt_ref):
    tm = ROUTE_TM
    rows = slice(sub * tm, (sub + 1) * tm)
    a = _rms(attn_ref[rows, :].astype(F32), again_ref[...]).astype(BF16)
    mixed = jnp.concatenate([a, ret_ref[rows, :]], axis=1)
    h = x_ref[rows, :] + jnp.dot(mixed, wo_ref[...], preferred_element_type=F32)
    h_ref[rows, :] = h.astype(h_ref.dtype)
    hn = _rms(h, fgain_ref[...]).astype(BF16)

    logits = jnp.dot(hn, rw_ref[...], preferred_element_type=F32) + rb_ref[...]
    lt = logits.T
    row = lambda i: lt[i:i + 1, :]

    def top1(vals):
        top = functools.reduce(jnp.maximum, vals)
        idx = jnp.full(top.shape, len(vals) - 1, jnp.int32)
        for i in reversed(range(len(vals) - 1)):
            idx = jnp.where(vals[i] == top, i, idx)
        return top, idx

    g_rows = [row(N_EXPERTS + g) for g in range(N_GROUPS)]
    gmax, grp = top1(g_rows)
    p_group = 1.0 / sum(jnp.exp(g - gmax) for g in g_rows)
    chosen = []
    for j in range(EXPERTS_PER_GROUP):
        e = row((N_GROUPS - 1) * EXPERTS_PER_GROUP + j)
        for g in reversed(range(N_GROUPS - 1)):
            e = jnp.where(grp == g, row(g * EXPERTS_PER_GROUP + j), e)
        chosen.append(e)
    v1, i1 = top1(chosen)
    v2, i2 = top1([jnp.where(i1 == j, -jnp.inf, e) for j, e in enumerate(chosen)])
    e2 = jnp.exp(v2 - v1)
    p1 = p_group / (1.0 + e2)
    inner = [jnp.where(i1 == j, p1, jnp.where(i2 == j, p1 * e2, 0.0)) for j in range(EXPERTS_PER_GROUP)]
    gate_rows = [jnp.where(grp == g, inner[j], 0.0)
                 for g in range(N_GROUPS) for j in range(EXPERTS_PER_GROUP)]

    sub8 = lax.broadcasted_iota(jnp.int32, (8, 1), 0)
    onehot_t = jnp.where(sub8 == grp, 1.0, 0.0)
    r_i = lax.broadcasted_iota(jnp.int32, (tm, tm), 0)
    c_i = lax.broadcasted_iota(jnp.int32, (tm, tm), 1)
    before = jnp.where(c_i < r_i, 1.0, 0.0).astype(BF16)
    rank_t = lax.dot_general(onehot_t.astype(BF16), before, (((1,), (1,)), ((), ())),
                             preferred_element_type=F32)
    cnt = jnp.sum(onehot_t, axis=1, keepdims=True)
    c16 = jnp.ceil(cnt * (1.0 / CHUNK)) * CHUNK
    start = jnp.zeros((8, 1), F32)
    for g in range(N_GROUPS - 1):
        start = start + jnp.where(sub8 > g, c16[g:g + 1, :], 0.0)
    dest_t = jnp.sum(onehot_t * (start + rank_t), axis=0, keepdims=True)
    cnt_ref[sub] = jnp.broadcast_to(jnp.where(sub8 < N_GROUPS, c16, pltpu.roll(start, N_GROUPS, 0)),
                                    (8, LANES)).astype(jnp.int32)

    cols = jnp.concatenate(gate_rows + [dest_t, jnp.zeros((LANES - N_EXPERTS - 1, tm), F32)], axis=0).T
    dest_ref[rows, :] = cols
    lane = lax.broadcasted_iota(jnp.int32, cols.shape, 1)
    gates = jnp.where(lane < N_EXPERTS, cols, 0.0)
    perm = jnp.where(lax.broadcasted_iota(jnp.int32, (SORT_ROWS, tm), 0) == dest_t.astype(jnp.int32),
                     1.0, 0.0).astype(BF16)
    g_hi = gates.astype(BF16)
    g_lo = (gates - g_hi.astype(F32)).astype(BF16)
    payload = jnp.concatenate([hn, g_hi, g_lo], axis=1)
    xs_ref[sub * SORT_ROWS:(sub + 1) * SORT_ROWS, :] = (
        jnp.dot(perm, payload, preferred_element_type=F32).astype(BF16))


def _outproj_kernel(*refs):
    for sub in range(ROUTE_TILES_PER_STEP):
        _route_tile(sub, *refs)


def _outproj(attn, ret, x, w_o, again, fgain, rw, rb):
    t, d = x.shape
    per_step = ROUTE_TILES_PER_STEP
    tm = ROUTE_TM * per_step
    nt = t // ROUTE_TM
    row = lambda w: pl.BlockSpec((tm, w), lambda i: (i, 0))
    full = lambda r, c: pl.BlockSpec((r, c), lambda i: (0, 0))
    return pl.pallas_call(
        _outproj_kernel,
        grid=(nt // per_step,),
        in_specs=[row(ATTN_WIDTH), row(RET_WIDTH), row(d), full(ATTN_WIDTH + RET_WIDTH, d),
                  full(1, ATTN_WIDTH), full(1, d), full(d, LANES), full(1, LANES)],
        out_specs=[row(d), pl.BlockSpec((per_step * SORT_ROWS, XS_W), lambda i: (i, 0)), row(LANES),
                   pl.BlockSpec((per_step, 8, LANES), lambda i: (i, 0, 0))],
        out_shape=[jax.ShapeDtypeStruct((t, d), BF16),
                   jax.ShapeDtypeStruct((nt * SORT_ROWS, XS_W), BF16),
                   jax.ShapeDtypeStruct((t, LANES), F32),
                   jax.ShapeDtypeStruct((nt, 8, LANES), jnp.int32)],
        compiler_params=pltpu.CompilerParams(
            dimension_semantics=("arbitrary",), vmem_limit_bytes=VMEM_LIMIT),
        name="outproj",
    )(attn, ret, x, w_o, again, fgain, rw, rb)


def _dispatch_tables(cnt, n_moe_tiles):
    nt = cnt.shape[0]
    cpt = MOE_TM // CHUNK

    n_slots = n_moe_tiles * cpt
    slot_rows = -(-n_slots // (8 * LANES)) * 8

    def schedule_kernel(seg_ref, grp_ref, used_ref, src_ref, tail_ref):
        slot = (lax.broadcasted_iota(jnp.int32, (slot_rows, LANES), 0) * LANES
                + lax.broadcasted_iota(jnp.int32, (slot_rows, LANES), 1))
        src = jnp.full((slot_rows, LANES), -1, jnp.int32)
        pos = jnp.int32(0)
        for g in range(N_GROUPS):
            def tile_body(i, carry):
                p, src = carry
                n = seg_ref[i, g] // CHUNK
                first = (i * SORT_ROWS + seg_ref[i, N_GROUPS + g]) // CHUNK
                src = jnp.where((slot >= p) & (slot < p + n), slot + (first - p), src)
                return p + n, src
            end, src = lax.fori_loop(0, nt, tile_body, (pos, src))
            padded = ((end + cpt - 1) // cpt) * cpt

            def mark_body(m, carry):
                grp_ref[m] = g
                used_ref[m] = 1
                return carry
            lax.fori_loop(pos // cpt, padded // cpt, mark_body, 0)
            pos = padded

        def idle_body(m, carry):
            grp_ref[m] = N_GROUPS - 1
            used_ref[m] = 0
            return carry
        lax.fori_loop(pos // cpt, n_moe_tiles, idle_body, 0)
        src_ref[...] = src

        def tail_body(i, carry):
            rows = seg_ref[i, 0]
            for g in range(1, N_GROUPS):
                rows = rows + seg_ref[i, g]
            tail_ref[i] = rows // CHUNK
            return carry
        lax.fori_loop(0, nt, tail_body, 0)

    smem = lambda: pl.BlockSpec(memory_space=pltpu.SMEM)
    tile_group, tile_used, src, tile_tail = pl.pallas_call(
        schedule_kernel,
        in_specs=[smem()],
        out_specs=[smem(), smem(), pl.BlockSpec(memory_space=pltpu.VMEM), smem()],
        out_shape=[jax.ShapeDtypeStruct((n_moe_tiles,), jnp.int32),
                   jax.ShapeDtypeStruct((n_moe_tiles,), jnp.int32),
                   jax.ShapeDtypeStruct((slot_rows, LANES), jnp.int32),
                   jax.ShapeDtypeStruct((nt,), jnp.int32)],
        name="schedule",
    )(cnt[:, :2 * N_GROUPS, 0])
    return tile_group, tile_used, src.reshape(-1)[:n_slots], tile_tail


def _moe_kernel(grp_ref, used_ref, src_ref, tail_ref, xs_hbm, w1_ref, w3_ref, w2_ref, ys_hbm,
                xbuf, obuf, zbuf, in_sem, out_sem, zero_sem):
    m = pl.program_id(0)
    n_tiles = pl.num_programs(0)
    cpt = MOE_TM // CHUNK
    cps = SORT_ROWS // CHUNK
    slot = m % 2
    zero_chunk = cps - 1

    def rows(c):
        return pl.ds(pl.multiple_of(c * CHUNK, CHUNK), CHUNK)

    def zero_tails(wait):
        def tile_body(i, carry):
            def body(c, carry2):
                cp = pltpu.make_async_copy(zbuf, ys_hbm.at[rows(i * cps + c), :], zero_sem)
                cp.wait() if wait else cp.start()
                return carry2
            lax.fori_loop(tail_ref[i], cps, body, 0)
            return carry
        lax.fori_loop(0, tail_ref.shape[0], tile_body, 0)

    def gather_start(tile, sl):
        for c in range(cpt):
            src = src_ref[tile * cpt + c]
            src = jnp.where(src < 0, zero_chunk, src)
            pltpu.make_async_copy(xs_hbm.at[rows(src), :], xbuf.at[sl, c * CHUNK:(c + 1) * CHUNK, :],
                                  in_sem.at[sl]).start(priority=c % 2)

    def gather_wait(sl):
        pltpu.make_async_copy(xs_hbm.at[0:MOE_TM, :], xbuf.at[sl], in_sem.at[sl]).wait()

    def scatter(tile, sl, wait):
        full = src_ref[tile * cpt + cpt - 1] >= 0

        def chunk_copy(c, src):
            return pltpu.make_async_copy(obuf.at[sl, c * CHUNK:(c + 1) * CHUNK, :], ys_hbm.at[rows(src), :],
                                         out_sem.at[sl])

        @pl.when(full)
        def _():
            if wait:
                pltpu.make_async_copy(obuf.at[sl], ys_hbm.at[0:MOE_TM, :], out_sem.at[sl]).wait()
            else:
                for c in range(cpt):
                    chunk_copy(c, src_ref[tile * cpt + c]).start(priority=c % 2)

        @pl.when(jnp.logical_not(full))
        def _():
            for c in range(cpt):
                src = src_ref[tile * cpt + c]

                @pl.when(src >= 0)
                def _():
                    cp = chunk_copy(c, src)
                    cp.wait() if wait else cp.start()

    @pl.when(m == 0)
    def _():
        gather_start(0, 0)
        zbuf[...] = jnp.zeros(zbuf.shape, zbuf.dtype)
        zero_tails(False)

    prev_used = used_ref[jnp.maximum(m - 1, 0)] > 0

    @pl.when((m == 0) | prev_used)
    def _():
        gather_wait(slot)

    @pl.when((m >= 2) & (used_ref[jnp.maximum(m - 2, 0)] > 0))
    def _():
        scatter(m - 2, slot, True)

    def expert_tile(n_rows):
        gather_start(jnp.minimum(m + 1, n_tiles - 1), 1 - slot)
        x = xbuf[slot, :n_rows, :D_MODEL]
        gate = (xbuf[slot, :n_rows, D_MODEL:D_MODEL + LANES].astype(F32)
                + xbuf[slot, :n_rows, D_MODEL + LANES:].astype(F32))
        lane = lax.broadcasted_iota(jnp.int32, gate.shape, 1)
        base = grp_ref[m] * EXPERTS_PER_GROUP
        acc = jnp.zeros((n_rows, D_MODEL), F32)
        for j in range(EXPERTS_PER_GROUP):
            a = jnp.dot(x, w1_ref[j], preferred_element_type=F32)
            b = jnp.dot(x, w3_ref[j], preferred_element_type=F32)
            gj = jnp.sum(jnp.where(lane == base + j, gate, 0.0), axis=-1, keepdims=True)
            hid = (a * jax.nn.sigmoid(a) * b * gj).astype(BF16)
            acc = acc + jnp.dot(hid, w2_ref[j], preferred_element_type=F32)
        obuf[slot, :n_rows, :] = acc.astype(BF16)
        scatter(m, slot, False)

    half_empty = src_ref[m * cpt + cpt // 2] < 0

    @pl.when((used_ref[m] > 0) & jnp.logical_not(half_empty))
    def _():
        expert_tile(MOE_TM)

    @pl.when((used_ref[m] > 0) & half_empty)
    def _():
        expert_tile(MOE_TM // 2)

    @pl.when(m == n_tiles - 1)
    def _():
        zero_tails(True)

        @pl.when(used_ref[m] > 0)
        def _():
            gather_wait(1 - slot)
            scatter(m, slot, True)

        @pl.when((m >= 1) & prev_used)
        def _():
            scatter(m - 1, 1 - slot, True)


def _moe(xs, w1, w3, w2, tile_group, tile_used, src_chunk, tile_tail):
    n_moe_tiles = tile_group.shape[0]
    rows = xs.shape[0]
    d = D_MODEL
    wspec = lambda r, c: pl.BlockSpec((EXPERTS_PER_GROUP, r, c), lambda m, grp, *_: (grp[m], 0, 0))
    return pl.pallas_call(
        _moe_kernel,
        grid_spec=pltpu.PrefetchScalarGridSpec(
            num_scalar_prefetch=4,
            grid=(n_moe_tiles,),
            in_specs=[pl.BlockSpec(memory_space=pl.ANY),
                      wspec(d, EXPERT_FF), wspec(d, EXPERT_FF), wspec(EXPERT_FF, d)],
            out_specs=pl.BlockSpec(memory_space=pl.ANY),
            scratch_shapes=[pltpu.VMEM((2, MOE_TM, XS_W), BF16),
                            pltpu.VMEM((2, MOE_TM, d), BF16),
                            pltpu.VMEM((CHUNK, d), BF16),
                            pltpu.SemaphoreType.DMA((2,)),
                            pltpu.SemaphoreType.DMA((2,)),
                            pltpu.SemaphoreType.DMA(())]),
        out_shape=jax.ShapeDtypeStruct((rows, d), BF16),
        compiler_params=pltpu.CompilerParams(
            dimension_semantics=("arbitrary",), vmem_limit_bytes=VMEM_LIMIT),
        name="moe",
    )(tile_group, tile_used, src_chunk, tile_tail, xs, w1, w3, w2)


def _combine_kernel(ys_hbm, h_hbm, dest_ref, gain_ref, o_ref, ysbuf, hbuf, ys_sem, h_sem):
    tm = ROUTE_TM
    per_step = o_ref.shape[0] // tm
    i = pl.program_id(0)
    n = h_hbm.shape[0] // (per_step * tm)
    ahead = min(COMBINE_BUFFERS - 1, n - 1)

    def copies(step, slot):
        ys_rows = pl.ds(pl.multiple_of(step * (per_step * SORT_ROWS), CHUNK), per_step * SORT_ROWS)
        h_rows = pl.ds(pl.multiple_of(step * (per_step * tm), tm), per_step * tm)
        return (pltpu.make_async_copy(ys_hbm.at[ys_rows, :], ysbuf.at[slot], ys_sem.at[slot]),
                pltpu.make_async_copy(h_hbm.at[h_rows, :], hbuf.at[slot], h_sem.at[slot]))

    @pl.when(i == 0)
    def _():
        for step in range(ahead):
            for queue, cp in enumerate(copies(step, step)):
                cp.start(priority=queue)

    @pl.when(i + ahead < n)
    def _():
        for queue, cp in enumerate(copies(i + ahead, (i + ahead) % COMBINE_BUFFERS)):
            cp.start(priority=queue)

    slot = i % COMBINE_BUFFERS
    for cp in copies(i, slot):
        cp.wait()
    for sub in range(per_step):
        rows = slice(sub * tm, (sub + 1) * tm)
        dest = dest_ref[rows, N_EXPERTS:N_EXPERTS + 1].astype(jnp.int32)
        perm_t = jnp.where(lax.broadcasted_iota(jnp.int32, (tm, SORT_ROWS), 1) == dest, 1.0, 0.0).astype(BF16)
        moe = jnp.dot(perm_t, ysbuf[slot, sub * SORT_ROWS:(sub + 1) * SORT_ROWS, :], preferred_element_type=F32)
        o_ref[rows, :] = _rms(hbuf[slot, rows, :].astype(F32) + moe, gain_ref[...])


def _combine(ys, h, dest, gain):
    t, d = h.shape
    per_step = COMBINE_TILES_PER_STEP
    tm = ROUTE_TM * per_step
    return pl.pallas_call(
        _combine_kernel,
        grid=(t // tm,),
        in_specs=[pl.BlockSpec(memory_space=pl.ANY),
                  pl.BlockSpec(memory_space=pl.ANY),
                  pl.BlockSpec((tm, LANES), lambda i: (i, 0)),
                  pl.BlockSpec((1, d), lambda i: (0, 0))],
        out_specs=pl.BlockSpec((tm, d), lambda i: (i, 0)),
        out_shape=jax.ShapeDtypeStruct((t, d), F32),
        scratch_shapes=[pltpu.VMEM((COMBINE_BUFFERS, per_step * SORT_ROWS, d), BF16),
                        pltpu.VMEM((COMBINE_BUFFERS, tm, d), BF16),
                        pltpu.SemaphoreType.DMA((COMBINE_BUFFERS,)),
                        pltpu.SemaphoreType.DMA((COMBINE_BUFFERS,))],
        compiler_params=pltpu.CompilerParams(
            dimension_semantics=("arbitrary",), vmem_limit_bytes=VMEM_LIMIT),
        name="combine",
    )(ys, h, dest, gain)


def _rotary_tables(s):
    half = HEAD_DIM // 2
    inv = ROPE_BASE ** (-jnp.arange(half, dtype=F32) / half)
    ang = jnp.arange(s, dtype=F32)[:, None] * inv[None, :]
    cos, sin = jnp.cos(ang), jnp.sin(ang)
    cos_t = jnp.tile(jnp.concatenate([cos, cos], axis=-1), (1, LANES // HEAD_DIM))
    sin_t = jnp.tile(jnp.concatenate([-sin, sin], axis=-1), (1, LANES // HEAD_DIM))
    return cos_t, sin_t


def kernel(x, w_in, w_out, norm_mix, norm_ffn, norm_final, attn_out_gain, rel_bias, ret_decay_fwd, ret_decay_bwd, router_group_w, router_group_b, router_expert_w, router_expert_b, expert_w1, expert_w3, expert_w2):
    b, s, d = x.shape
    depth = w_in.shape[0]
    cos_t, sin_t = _rotary_tables(s)
    bias_rows = _attn_bias_rows(rel_bias)
    h = x
    for layer in range(depth):
        (aq, ak, av, rq, rk, rv, rg), qkv_grouped = _inproj(
            h, norm_mix[layer][None], w_in[layer].astype(BF16), cos_t, sin_t)
        attn, ret, (w1, w3, w2, w_o) = _mixers(
            (aq, ak, av), qkv_grouped, bias_rows, (rq, rk, rv, rg),
            _retention_tables(ret_decay_fwd[layer], ret_decay_bwd[layer]),
            (expert_w1[layer], expert_w3[layer], expert_w2[layer], w_out[layer]))

        rw = jnp.concatenate(
            [jnp.transpose(router_expert_w[layer], (1, 0, 2)).reshape(d, N_EXPERTS),
             router_group_w[layer],
             jnp.zeros((d, LANES - N_EXPERTS - N_GROUPS), F32)], axis=1).astype(BF16)
        rb = jnp.concatenate(
            [router_expert_b[layer].reshape(N_EXPERTS), router_group_b[layer],
             jnp.zeros((LANES - N_EXPERTS - N_GROUPS,), F32)])[None].astype(F32)
        h1, xs, dest, cnt = _outproj(
            attn.reshape(b * s, ATTN_WIDTH), ret.reshape(b * s, RET_WIDTH), h.reshape(b * s, d),
            w_o, attn_out_gain[layer][None], norm_ffn[layer][None],
            rw, rb)
        n_route_tiles = (b * s) // ROUTE_TM
        n_moe_tiles = (b * s + n_route_tiles * N_GROUPS * (CHUNK - 1)) // MOE_TM + N_GROUPS
        schedule = _dispatch_tables(cnt, n_moe_tiles)
        ys = _moe(xs, w1, w3, w2, *schedule)
        assert depth == 1, "the combine kernel fuses the final norm, so it must run on the last layer"
        h = _combine(ys, h1, dest, norm_final[None]).reshape(b, s, d)
    return h
```

```python
import functools
import math

import jax
import jax.numpy as jnp
from jax import lax
from jax.experimental import pallas as pl
from jax.experimental.pallas import tpu as pltpu

F32 = jnp.float32
BF16 = jnp.bfloat16

D_MODEL = 1024
HEAD_DIM = 64
ATTN_WIDTH = 512
RET_WIDTH = 512
N_HEADS = 8
PAIR = 2 * HEAD_DIM
N_PAIRS = N_HEADS // 2
ATTN_DILATIONS = (1, 4, 16)
ATTN_RADIUS = 64
N_BUCKETS = 32
REL_MAX_DIST = 1024
ROPE_BASE = 10000.0
N_GROUPS = 4
EXPERTS_PER_GROUP = 4
N_EXPERTS = 16
EXPERT_FF = 512
EPS = 1e-6
NEG_INF = -1e30
LOG2_E = math.log2(math.e)

LANES = 128
ATTN_TQ = 128
ATTN_W = 256
N_BIAS_VARIANTS = 8
RET_CHUNK = 256
ROUTE_TM = 512
ROUTE_TILES_PER_STEP = 2
COMBINE_TILES_PER_STEP = 2
COMBINE_BUFFERS = 4
SORT_ROWS = 592
CHUNK = 16
MOE_TM = 512
XS_W = D_MODEL + 2 * LANES
VMEM_LIMIT = 48 * 1024 * 1024


def _rms(x, gain):
    return x * lax.rsqrt(jnp.mean(x * x, axis=-1, keepdims=True) + EPS) * gain


def _inproj_kernel(x_ref, gain_ref, w_ref, cos_ref, sin_ref,
                   aq_ref, ak_ref, av_ref, rq_ref, rk_ref, rv_ref, rg_ref,
                   aq4_ref, ak4_ref, av4_ref, aq16_ref, ak16_ref, av16_ref, stage_ref, stage4_ref):
    tm = x_ref.shape[0]
    xn = _rms(x_ref[...], gain_ref[...]).astype(BF16)

    def stage(i, t, nat_ref):
        nat_ref[...] = t.astype(BF16)
        for hp in range(N_PAIRS):
            stage_ref[i, hp] = t[:, hp * PAIR:(hp + 1) * PAIR]

    def regroup4(i, d4_ref):
        for hp in range(N_PAIRS):
            for r4 in range(4):
                g4 = stage_ref[i, hp, pl.ds(r4, tm // 4, stride=4), :]
                d4_ref[hp, r4] = g4.astype(BF16)
                stage4_ref[i, hp, r4] = g4

    def regroup16(i, d16_ref):
        for hp in range(N_PAIRS):
            for r4 in range(4):
                for j in range(4):
                    d16_ref[hp, r4 + 4 * j] = (
                        stage4_ref[i, hp, r4, pl.ds(j, tm // 16, stride=4), :].astype(BF16))

    def seg(i):
        return jnp.dot(xn, w_ref[:, i * ATTN_WIDTH:(i + 1) * ATTN_WIDTH], preferred_element_type=F32)

    def rotary(t):
        cos, sin = cos_ref[...], sin_ref[...]
        first_half = (lax.broadcasted_iota(jnp.int32, (1, LANES), 1) % HEAD_DIM) < HEAD_DIM // 2
        outs = []
        for j in range(t.shape[1] // LANES):
            tj = t[:, j * LANES:(j + 1) * LANES]
            partner = jnp.where(first_half, pltpu.roll(tj, LANES - HEAD_DIM // 2, 1),
                                pltpu.roll(tj, HEAD_DIM // 2, 1))
            outs.append(tj * cos + partner * sin)
        return jnp.concatenate(outs, axis=1)

    stage(0, seg(0) * (HEAD_DIM ** -0.5 * LOG2_E), aq_ref)
    stage(1, seg(1), ak_ref)
    stage(2, seg(2), av_ref)
    rq_ref[...] = rotary(seg(3)).astype(BF16)
    for i, d4_ref in enumerate((aq4_ref, ak4_ref, av4_ref)):
        regroup4(i, d4_ref)
    rk_ref[...] = (rotary(seg(4)) * (HEAD_DIM ** -0.5)).astype(BF16)
    rv_ref[...] = seg(5).astype(BF16)
    for i, d16_ref in enumerate((aq16_ref, ak16_ref, av16_ref)):
        regroup16(i, d16_ref)
    rg_ref[...] = seg(6).astype(BF16)


def _inproj(x, gain, w_in, cos_t, sin_t, tm=512):
    b, s, d = x.shape
    n = w_in.shape[1]
    out = jax.ShapeDtypeStruct((b, s, ATTN_WIDTH), BF16)
    ospec = pl.BlockSpec((None, tm, ATTN_WIDTH), lambda si, bi: (bi, si, 0))

    def grouped(dil):
        shape = jax.ShapeDtypeStruct((b, s // tm, N_PAIRS, dil, tm // dil, PAIR), BF16)
        spec = pl.BlockSpec((None, None, N_PAIRS, dil, tm // dil, PAIR), lambda si, bi: (bi, si, 0, 0, 0, 0))
        return [shape] * 3, [spec] * 3

    shapes4, specs4 = grouped(4)
    shapes16, specs16 = grouped(16)
    outs = pl.pallas_call(
        _inproj_kernel,
        grid=(s // tm, b),
        in_specs=[
            pl.BlockSpec((None, tm, d), lambda si, bi: (bi, si, 0)),
            pl.BlockSpec((1, d), lambda si, bi: (0, 0)),
            pl.BlockSpec((d, n), lambda si, bi: (0, 0)),
            pl.BlockSpec((tm, LANES), lambda si, bi: (si, 0)),
            pl.BlockSpec((tm, LANES), lambda si, bi: (si, 0)),
        ],
        out_specs=[ospec] * 7 + specs4 + specs16,
        out_shape=[out] * 7 + shapes4 + shapes16,
        scratch_shapes=[pltpu.VMEM((3, N_PAIRS, tm, PAIR), F32),
                        pltpu.VMEM((3, N_PAIRS, 4, tm // 4, PAIR), F32)],
        compiler_params=pltpu.CompilerParams(
            dimension_semantics=("arbitrary", "arbitrary"), vmem_limit_bytes=VMEM_LIMIT),
        name="inproj",
    )(x, gain, w_in, cos_t, sin_t)
    return outs[:7], outs[7:]


def _t5_bucket(rel):
    half = N_BUCKETS // 2
    max_exact = half // 2
    offset = jnp.where(rel > 0, half, 0)
    n = jnp.abs(rel)
    nf = jnp.maximum(n, 1).astype(F32)
    large = max_exact + (jnp.log(nf / max_exact) / math.log(REL_MAX_DIST / max_exact)
                         * (half - max_exact)).astype(jnp.int32)
    large = jnp.minimum(large, half - 1)
    return offset + jnp.where(n < max_exact, n, large)


def _attn_bias_rows(rel_bias):
    period = 2 * ATTN_W
    band = 2 * ATTN_RADIUS + 1
    rel = jnp.arange(-ATTN_RADIUS, ATTN_RADIUS + 1)
    rows = []
    for dil, offs in ((1, (0, 64, 128)), (4, (0, 64, 128)), (16, (0, 128))):
        vals = rel_bias[_t5_bucket(rel * dil)].astype(F32).T * LOG2_E
        for off in offs:
            lo = off - ATTN_RADIUS
            pad = jnp.full((N_HEADS, period - band), NEG_INF, F32)
            if lo >= 0:
                row = jnp.concatenate([pad[:, :lo], vals, pad[:, lo:]], axis=1)
            else:
                row = jnp.concatenate([vals[:, -lo:], pad, vals[:, :-lo]], axis=1)
            rows.append(row)
    v = jnp.stack(rows, axis=1)
    return v.reshape(N_PAIRS, 2 * N_BIAS_VARIANTS, period)


def _attention_kernel(q_ref, k_ref, v_ref, q4_ref, k4_ref, v4_ref, q16_ref, k16_ref, v16_ref,
                      rows_ref, o_ref, bias_ref, acc_ref, m_ref, l_ref, out_ref):
    s = q_ref.shape[0]
    n_tiles = s // ATTN_TQ
    lane = lax.broadcasted_iota(jnp.int32, (1, PAIR), 1)
    head0 = lane < HEAD_DIM

    @pl.when(pl.program_id(1) == 0)
    def _():
        col = lax.broadcasted_iota(jnp.int32, (ATTN_TQ, ATTN_W), 1)
        for idx in range(2 * N_BIAS_VARIANTS):
            gen = jnp.broadcast_to(rows_ref[idx:idx + 1, :], (ATTN_TQ, 2 * ATTN_W))
            tab = pltpu.roll(gen, 0, 1, stride=1, stride_axis=0)[:, :ATTN_W]
            var = idx % N_BIAS_VARIANTS
            if var >= 6:
                tab = jnp.where((col // ATTN_TQ) == var - 6, tab, NEG_INF)
            head = idx // N_BIAS_VARIANTS
            bias_ref[var, head * ATTN_TQ:(head + 1) * ATTN_TQ, :] = tab

    def rows_of(ref, dil, lo, n):
        if dil == 1:
            return ref[lo:lo + n, :]
        piece, sub_len = ref.shape[2], s // dil
        parts, pos = [], lo
        while pos < lo + n:
            within = pos % sub_len
            off = within % piece
            take = min(piece - off, lo + n - pos)
            parts.append(ref[within // piece, pos // sub_len, off:off + take, :])
            pos += take
        return parts[0] if len(parts) == 1 else jnp.concatenate(parts, axis=0)

    def run_branch(bi, dil, qs_ref, ks_ref, vs_ref):
        sub_len = s // dil
        tiles_per_sub = sub_len // ATTN_TQ

        def tile(t):
            q0 = t * ATTN_TQ
            if tiles_per_sub == 1:
                ws = (t // 2) * ATTN_W
                var = 6 + t % 2
            else:
                pos = t % tiles_per_sub
                sub_lo = (t // tiles_per_sub) * sub_len
                ws = min(max(q0 - ATTN_RADIUS, sub_lo), sub_lo + sub_len - ATTN_W)
                var = (0 if pos == 0 else 2 if pos == tiles_per_sub - 1 else 1) + 3 * bi
            q = rows_of(qs_ref, dil, q0, ATTN_TQ)
            k = rows_of(ks_ref, dil, ws, ATTN_W)
            v = rows_of(vs_ref, dil, ws, ATTN_W)
            q2 = jnp.concatenate([jnp.where(head0, q, jnp.zeros_like(q)),
                                  jnp.where(head0, jnp.zeros_like(q), q)], axis=0)
            sc = lax.dot_general(q2, k, (((1,), (1,)), ((), ())), preferred_element_type=F32)
            sc = sc + bias_ref[var]
            m = jnp.max(sc, axis=-1, keepdims=True)
            p = jnp.exp2(sc - m).astype(BF16)
            o = jnp.dot(p, jnp.concatenate([v, jnp.ones_like(v)], axis=1), preferred_element_type=F32)
            l = o[:, PAIR:]
            outs, ms, ls = (o[:ATTN_TQ, :PAIR], o[ATTN_TQ:, :PAIR]), (m[:ATTN_TQ], m[ATTN_TQ:]), (l[:ATTN_TQ], l[ATTN_TQ:])
            if dil == 16:
                dst = pl.ds((t % 4) * (s // 4) + t // 4, ATTN_TQ, stride=4)
            else:
                dst = pl.ds(q0, ATTN_TQ)
            acc_ref[bi, dst, :] = jnp.where(head0, outs[0], outs[1])
            m_ref[bi, dst, :] = jnp.where(head0, ms[0], ms[1])
            l_ref[bi, dst, :] = jnp.where(head0, ls[0], ls[1])

        for t in range(n_tiles):
            tile(t)

    run_branch(0, 1, q_ref, k_ref, v_ref)
    run_branch(1, 4, q4_ref, k4_ref, v4_ref)
    run_branch(2, 16, q16_ref, k16_ref, v16_ref)

    rows = ATTN_TQ
    for r4 in range(4):
        for blk in range(s // (4 * rows)):
            nat = pl.ds(r4 + 4 * rows * blk, rows, stride=4)
            grouped4 = pl.ds(r4 * (s // 4) + rows * blk, rows)
            sl = (nat, grouped4, grouped4)
            m = [m_ref[bi, sl[bi], :] for bi in range(3)]
            mx = jnp.maximum(jnp.maximum(m[0], m[1]), m[2])
            num = jnp.zeros((rows, PAIR), F32)
            den = jnp.zeros((rows, PAIR), F32)
            for bi in range(3):
                e = jnp.exp2(m[bi] - mx)
                num = num + e * acc_ref[bi, sl[bi], :]
                den = den + e * l_ref[bi, sl[bi], :]
            out_ref[nat, :] = num / den
    o_ref[...] = out_ref[...].astype(o_ref.dtype)


def _retention_tables(decay_fwd, decay_bwd):
    c = RET_CHUNK
    lg_f = -jnp.exp(decay_fwd.astype(F32))
    lg_b = -jnp.exp(decay_bwd.astype(F32))
    idx = jnp.arange(c, dtype=F32)
    rel = idx[:, None] - idx[None, :]
    dmat = jnp.where(rel >= 0,
                     jnp.exp(lg_f[:, None, None] * jnp.maximum(rel, 0.0)[None]),
                     jnp.exp(lg_b[:, None, None] * jnp.maximum(-rel, 0.0)[None]))
    dmat = dmat.reshape(N_PAIRS, 2 * c, c)

    def lanes(v):
        v = v.reshape(N_PAIRS, 2, -1)
        return jnp.repeat(jnp.transpose(v, (0, 2, 1)), HEAD_DIM, axis=2)

    vec = jnp.stack([
        lanes(jnp.exp(lg_f[:, None] * (idx + 1.0)[None])),
        lanes(jnp.exp(lg_f[:, None] * (c - 1.0 - idx)[None])),
        lanes(jnp.exp(lg_b[:, None] * (c - idx)[None])),
        lanes(jnp.exp(lg_b[:, None] * idx[None])),
    ], axis=1)
    same_head = (jnp.arange(PAIR)[:, None] // HEAD_DIM) == (jnp.arange(PAIR)[None, :] // HEAD_DIM)
    cd = jnp.stack([lanes(jnp.exp(lg_f * c)[:, None]), lanes(jnp.exp(lg_b * c)[:, None])], axis=1)
    cd = jnp.transpose(cd, (0, 1, 3, 2)) * same_head[None, None].astype(F32)
    return dmat, vec, cd


def _retention_kernel(q_ref, k_ref, v_ref, g_ref, dmat_ref, vec_ref, cd_ref, o_ref, kv_ref, st_ref):
    s = q_ref.shape[0]
    c = RET_CHUNK
    nc = s // c
    lane = lax.broadcasted_iota(jnp.int32, (1, PAIR), 1)
    head0 = lane < HEAD_DIM
    same_head = ((lax.broadcasted_iota(jnp.int32, (PAIR, PAIR), 0) // HEAD_DIM)
                 == (lax.broadcasted_iota(jnp.int32, (PAIR, PAIR), 1) // HEAD_DIM))

    same_head2 = jnp.concatenate([same_head, same_head], axis=0)
    for n in range(nc):
        rows = slice(n * c, (n + 1) * c)
        kf = k_ref[rows, :].astype(F32)
        kcat = jnp.concatenate([(kf * vec_ref[1]).astype(BF16), (kf * vec_ref[3]).astype(BF16)], axis=1)
        kv = lax.dot_general(kcat, v_ref[rows, :], (((0,), (0,)), ((), ())), preferred_element_type=F32)
        kv_ref[n] = jnp.where(same_head2, kv, 0.0)

    state = jnp.zeros((PAIR, PAIR), F32)
    for n in range(nc):
        st_ref[n, :PAIR, :] = state.astype(BF16)
        state = state * cd_ref[0] + kv_ref[n, :PAIR, :]
    state = jnp.zeros((PAIR, PAIR), F32)
    for n in reversed(range(nc)):
        st_ref[n, PAIR:, :] = state.astype(BF16)
        state = state * cd_ref[1] + kv_ref[n, PAIR:, :]

    for n in range(nc):
        rows = slice(n * c, (n + 1) * c)
        q, k, v = q_ref[rows, :], k_ref[rows, :], v_ref[rows, :]
        q2 = jnp.concatenate([jnp.where(head0, q, jnp.zeros_like(q)),
                              jnp.where(head0, jnp.zeros_like(q), q)], axis=0)
        sc = lax.dot_general(q2, k, (((1,), (1,)), ((), ())), preferred_element_type=F32)
        intra = jnp.dot((sc * dmat_ref[...]).astype(BF16), v, preferred_element_type=F32)
        qf = q.astype(F32)
        qcat = jnp.concatenate([(qf * vec_ref[0]).astype(BF16), (qf * vec_ref[2]).astype(BF16)], axis=1)
        y = jnp.where(head0, intra[:c], intra[c:]) + jnp.dot(qcat, st_ref[n], preferred_element_type=F32)
        y2 = y * y
        ms0 = jnp.sum(jnp.where(head0, y2, 0.0), axis=-1, keepdims=True)
        ms1 = jnp.sum(jnp.where(head0, 0.0, y2), axis=-1, keepdims=True)
        ms = jnp.where(head0, ms0, ms1) * (1.0 / HEAD_DIM)
        g = g_ref[rows, :].astype(F32)
        o_ref[rows, :] = (y * lax.rsqrt(ms + EPS) * (g * jax.nn.sigmoid(g))).astype(o_ref.dtype)


def _mixers(qkv, qkv_grouped, bias_rows, ret_inputs, ret_tables, weights_f32):
    b, s, _ = qkv[0].shape
    c = RET_CHUNK
    n_steps = N_PAIRS * b
    spec = pl.BlockSpec((None, s, PAIR), lambda hp, bi: (bi, 0, hp))
    gspecs = [pl.BlockSpec((None, g.shape[1], None) + g.shape[3:], lambda hp, bi: (bi, 0, hp, 0, 0, 0))
              for g in qkv_grouped]
    slabs = [w.reshape(n_steps, -1, w.shape[-1]) for w in weights_f32]
    slab_specs = [pl.BlockSpec((None,) + w.shape[1:], lambda hp, bi: (hp * b + bi, 0, 0)) for w in slabs]
    n_attn_in, n_ret_in, n_w = 10, 7, len(slabs)
    attn_scratch = [
        pltpu.VMEM((N_BIAS_VARIANTS, 2 * ATTN_TQ, ATTN_W), F32),
        pltpu.VMEM((3, s, PAIR), F32),
        pltpu.VMEM((3, s, PAIR), F32),
        pltpu.VMEM((3, s, PAIR), F32),
        pltpu.VMEM((s, PAIR), F32),
    ]
    ret_scratch = [pltpu.VMEM((s // c, 2 * PAIR, PAIR), F32),
                   pltpu.VMEM((s // c, 2 * PAIR, PAIR), BF16)]

    def kernel(*refs):
        ins, rest = refs[:n_attn_in + n_ret_in + n_w], refs[n_attn_in + n_ret_in + n_w:]
        attn_o, ret_o, w_out, scratch = rest[0], rest[1], rest[2:2 + n_w], rest[2 + n_w:]
        _attention_kernel(*ins[:n_attn_in], attn_o, *scratch[:len(attn_scratch)])
        _retention_kernel(*ins[n_attn_in:n_attn_in + n_ret_in], ret_o, *scratch[len(attn_scratch):])
        for src, dst in zip(ins[n_attn_in + n_ret_in:], w_out):
            dst[...] = src[...].astype(BF16)

    outs = pl.pallas_call(
        kernel,
        grid=(N_PAIRS, b),
        in_specs=[spec] * 3 + gspecs
        + [pl.BlockSpec((None, 2 * N_BIAS_VARIANTS, 2 * ATTN_W), lambda hp, bi: (hp, 0, 0))]
        + [spec] * 4
        + [pl.BlockSpec((None, 2 * c, c), lambda hp, bi: (hp, 0, 0)),
           pl.BlockSpec((None, 4, c, PAIR), lambda hp, bi: (hp, 0, 0, 0)),
           pl.BlockSpec((None, 2, PAIR, PAIR), lambda hp, bi: (hp, 0, 0, 0))]
        + slab_specs,
        out_specs=[spec, spec] + slab_specs,
        out_shape=[jax.ShapeDtypeStruct((b, s, ATTN_WIDTH), BF16), jax.ShapeDtypeStruct((b, s, RET_WIDTH), BF16)]
        + [jax.ShapeDtypeStruct(w.shape, BF16) for w in slabs],
        scratch_shapes=attn_scratch + ret_scratch,
        compiler_params=pltpu.CompilerParams(
            dimension_semantics=("arbitrary", "arbitrary"), vmem_limit_bytes=VMEM_LIMIT),
        name="mixers",
    )(*qkv, *qkv_grouped, bias_rows, *ret_inputs, *ret_tables, *slabs)
    return outs[0], outs[1], [o.reshape(w.shape) for o, w in zip(outs[2:], weights_f32)]


def _route_tile(sub, attn_ref, ret_ref, x_ref, wo_ref, again_ref, fgain_ref,
                rw_ref, rb_ref, h_ref, xs_ref, dest_ref, cnt_ref):
    tm = ROUTE_TM
    rows = slice(sub * tm, (sub + 1) * tm)
    a = _rms(attn_ref[rows, :].astype(F32), again_ref[...]).astype(BF16)
    mixed = jnp.concatenate([a, ret_ref[rows, :]], axis=1)
    h = x_ref[rows, :] + jnp.dot(mixed, wo_ref[...], preferred_element_type=F32)
    h_ref[rows, :] = h.astype(h_ref.dtype)
    hn = _rms(h, fgain_ref[...]).astype(BF16)

    logits = jnp.dot(hn, rw_ref[...], preferred_element_type=F32) + rb_ref[...]
    lt = logits.T
    row = lambda i: lt[i:i + 1, :]

    def top1(vals):
        top = functools.reduce(jnp.maximum, vals)
        idx = jnp.full(top.shape, len(vals) - 1, jnp.int32)
        for i in reversed(range(len(vals) - 1)):
            idx = jnp.where(vals[i] == top, i, idx)
        return top, idx

    g_rows = [row(N_EXPERTS + g) for g in range(N_GROUPS)]
    gmax, grp = top1(g_rows)
    p_group = 1.0 / sum(jnp.exp(g - gmax) for g in g_rows)
    chosen = []
    for j in range(EXPERTS_PER_GROUP):
        e = row((N_GROUPS - 1) * EXPERTS_PER_GROUP + j)
        for g in reversed(range(N_GROUPS - 1)):
            e = jnp.where(grp == g, row(g * EXPERTS_PER_GROUP + j), e)
        chosen.append(e)
    v1, i1 = top1(chosen)
    v2, i2 = top1([jnp.where(i1 == j, -jnp.inf, e) for j, e in enumerate(chosen)])
    e2 = jnp.exp(v2 - v1)
    p1 = p_group / (1.0 + e2)
    inner = [jnp.where(i1 == j, p1, jnp.where(i2 == j, p1 * e2, 0.0)) for j in range(EXPERTS_PER_GROUP)]
    gate_rows = [jnp.where(grp == g, inner[j], 0.0)
                 for g in range(N_GROUPS) for j in range(EXPERTS_PER_GROUP)]

    sub8 = lax.broadcasted_iota(jnp.int32, (8, 1), 0)
    onehot_t = jnp.where(sub8 == grp, 1.0, 0.0)
    r_i = lax.broadcasted_iota(jnp.int32, (tm, tm), 0)
    c_i = lax.broadcasted_iota(jnp.int32, (tm, tm), 1)
    before = jnp.where(c_i < r_i, 1.0, 0.0).astype(BF16)
    rank_t = lax.dot_general(onehot_t.astype(BF16), before, (((1,), (1,)), ((), ())),
                             preferred_element_type=F32)
    cnt = jnp.sum(onehot_t, axis=1, keepdims=True)
    c16 = jnp.ceil(cnt * (1.0 / CHUNK)) * CHUNK
    start = jnp.zeros((8, 1), F32)
    for g in range(N_GROUPS - 1):
        start = start + jnp.where(sub8 > g, c16[g:g + 1, :], 0.0)
    dest_t = jnp.sum(onehot_t * (start + rank_t), axis=0, keepdims=True)
    cnt_ref[sub] = jnp.broadcast_to(jnp.where(sub8 < N_GROUPS, c16, pltpu.roll(start, N_GROUPS, 0)),
                                    (8, LANES)).astype(jnp.int32)

    cols = jnp.concatenate(gate_rows + [dest_t, jnp.zeros((LANES - N_EXPERTS - 1, tm), F32)], axis=0).T
    dest_ref[rows, :] = cols
    lane = lax.broadcasted_iota(jnp.int32, cols.shape, 1)
    gates = jnp.where(lane < N_EXPERTS, cols, 0.0)
    perm = jnp.where(lax.broadcasted_iota(jnp.int32, (SORT_ROWS, tm), 0) == dest_t.astype(jnp.int32),
                     1.0, 0.0).astype(BF16)
    g_hi = gates.astype(BF16)
    g_lo = (gates - g_hi.astype(F32)).astype(BF16)
    payload = jnp.concatenate([hn, g_hi, g_lo], axis=1)
    xs_ref[sub * SORT_ROWS:(sub + 1) * SORT_ROWS, :] = (
        jnp.dot(perm, payload, preferred_element_type=F32).astype(BF16))


def _outproj_kernel(*refs):
    for sub in range(ROUTE_TILES_PER_STEP):
        _route_tile(sub, *refs)


def _outproj(attn, ret, x, w_o, again, fgain, rw, rb):
    t, d = x.shape
    per_step = ROUTE_TILES_PER_STEP
    tm = ROUTE_TM * per_step
    nt = t // ROUTE_TM
    row = lambda w: pl.BlockSpec((tm, w), lambda i: (i, 0))
    full = lambda r, c: pl.BlockSpec((r, c), lambda i: (0, 0))
    return pl.pallas_call(
        _outproj_kernel,
        grid=(nt // per_step,),
        in_specs=[row(ATTN_WIDTH), row(RET_WIDTH), row(d), full(ATTN_WIDTH + RET_WIDTH, d),
                  full(1, ATTN_WIDTH), full(1, d), full(d, LANES), full(1, LANES)],
        out_specs=[row(d), pl.BlockSpec((per_step * SORT_ROWS, XS_W), lambda i: (i, 0)), row(LANES),
                   pl.BlockSpec((per_step, 8, LANES), lambda i: (i, 0, 0))],
        out_shape=[jax.ShapeDtypeStruct((t, d), BF16),
                   jax.ShapeDtypeStruct((nt * SORT_ROWS, XS_W), BF16),
                   jax.ShapeDtypeStruct((t, LANES), F32),
                   jax.ShapeDtypeStruct((nt, 8, LANES), jnp.int32)],
        compiler_params=pltpu.CompilerParams(
            dimension_semantics=("arbitrary",), vmem_limit_bytes=VMEM_LIMIT),
        name="outproj",
    )(attn, ret, x, w_o, again, fgain, rw, rb)


def _dispatch_tables(cnt, n_moe_tiles):
    nt = cnt.shape[0]
    cpt = MOE_TM // CHUNK

    n_slots = n_moe_tiles * cpt
    slot_rows = -(-n_slots // (8 * LANES)) * 8

    def schedule_kernel(seg_ref, grp_ref, used_ref, src_ref, tail_ref):
        slot = (lax.broadcasted_iota(jnp.int32, (slot_rows, LANES), 0) * LANES
                + lax.broadcasted_iota(jnp.int32, (slot_rows, LANES), 1))
        src = jnp.full((slot_rows, LANES), -1, jnp.int32)
        pos = jnp.int32(0)
        for g in range(N_GROUPS):
            def tile_body(i, carry):
                p, src = carry
                n = seg_ref[i, g] // CHUNK
                first = (i * SORT_ROWS + seg_ref[i, N_GROUPS + g]) // CHUNK
                src = jnp.where((slot >= p) & (slot < p + n), slot + (first - p), src)
                return p + n, src
            end, src = lax.fori_loop(0, nt, tile_body, (pos, src))
            padded = ((end + cpt - 1) // cpt) * cpt

            def mark_body(m, carry):
                grp_ref[m] = g
                used_ref[m] = 1
                return carry
            lax.fori_loop(pos // cpt, padded // cpt, mark_body, 0)
            pos = padded

        def idle_body(m, carry):
            grp_ref[m] = N_GROUPS - 1
            used_ref[m] = 0
            return carry
        lax.fori_loop(pos // cpt, n_moe_tiles, idle_body, 0)
        src_ref[...] = src

        def tail_body(i, carry):
            rows = seg_ref[i, 0]
            for g in range(1, N_GROUPS):
                rows = rows + seg_ref[i, g]
            tail_ref[i] = rows // CHUNK
            return carry
        lax.fori_loop(0, nt, tail_body, 0)

    smem = lambda: pl.BlockSpec(memory_space=pltpu.SMEM)
    tile_group, tile_used, src, tile_tail = pl.pallas_call(
        schedule_kernel,
        in_specs=[smem()],
        out_specs=[smem(), smem(), pl.BlockSpec(memory_space=pltpu.VMEM), smem()],
        out_shape=[jax.ShapeDtypeStruct((n_moe_tiles,), jnp.int32),
                   jax.ShapeDtypeStruct((n_moe_tiles,), jnp.int32),
                   jax.ShapeDtypeStruct((slot_rows, LANES), jnp.int32),
                   jax.ShapeDtypeStruct((nt,), jnp.int32)],
        name="schedule",
    )(cnt[:, :2 * N_GROUPS, 0])
    return tile_group, tile_used, src.reshape(-1)[:n_slots], tile_tail


def _moe_kernel(grp_ref, used_ref, src_ref, tail_ref, xs_hbm, w1_ref, w3_ref, w2_ref, ys_hbm,
                xbuf, obuf, zbuf, in_sem, out_sem, zero_sem):
    m = pl.program_id(0)
    n_tiles = pl.num_programs(0)
    cpt = MOE_TM // CHUNK
    cps = SORT_ROWS // CHUNK
    slot = m % 2
    zero_chunk = cps - 1

    def rows(c):
        return pl.ds(pl.multiple_of(c * CHUNK, CHUNK), CHUNK)

    def zero_tails(wait):
        def tile_body(i, carry):
            def body(c, carry2):
                cp = pltpu.make_async_copy(zbuf, ys_hbm.at[rows(i * cps + c), :], zero_sem)
                cp.wait() if wait else cp.start()
                return carry2
            lax.fori_loop(tail_ref[i], cps, body, 0)
            return carry
        lax.fori_loop(0, tail_ref.shape[0], tile_body, 0)

    def gather_start(tile, sl):
        for c in range(cpt):
            src = src_ref[tile * cpt + c]
            src = jnp.where(src < 0, zero_chunk, src)
            pltpu.make_async_copy(xs_hbm.at[rows(src), :], xbuf.at[sl, c * CHUNK:(c + 1) * CHUNK, :],
                                  in_sem.at[sl]).start(priority=c % 2)

    def gather_wait(sl):
        pltpu.make_async_copy(xs_hbm.at[0:MOE_TM, :], xbuf.at[sl], in_sem.at[sl]).wait()

    def scatter(tile, sl, wait):
        full = src_ref[tile * cpt + cpt - 1] >= 0

        def chunk_copy(c, src):
            return pltpu.make_async_copy(obuf.at[sl, c * CHUNK:(c + 1) * CHUNK, :], ys_hbm.at[rows(src), :],
                                         out_sem.at[sl])

        @pl.when(full)
        def _():
            if wait:
                pltpu.make_async_copy(obuf.at[sl], ys_hbm.at[0:MOE_TM, :], out_sem.at[sl]).wait()
            else:
                for c in range(cpt):
                    chunk_copy(c, src_ref[tile * cpt + c]).start(priority=c % 2)

        @pl.when(jnp.logical_not(full))
        def _():
            for c in range(cpt):
                src = src_ref[tile * cpt + c]

                @pl.when(src >= 0)
                def _():
                    cp = chunk_copy(c, src)
                    cp.wait() if wait else cp.start()

    @pl.when(m == 0)
    def _():
        gather_start(0, 0)
        zbuf[...] = jnp.zeros(zbuf.shape, zbuf.dtype)
        zero_tails(False)

    prev_used = used_ref[jnp.maximum(m - 1, 0)] > 0

    @pl.when((m == 0) | prev_used)
    def _():
        gather_wait(slot)

    @pl.when((m >= 2) & (used_ref[jnp.maximum(m - 2, 0)] > 0))
    def _():
        scatter(m - 2, slot, True)

    def expert_tile(n_rows):
        gather_start(jnp.minimum(m + 1, n_tiles - 1), 1 - slot)
        x = xbuf[slot, :n_rows, :D_MODEL]
        gate = (xbuf[slot, :n_rows, D_MODEL:D_MODEL + LANES].astype(F32)
                + xbuf[slot, :n_rows, D_MODEL + LANES:].astype(F32))
        lane = lax.broadcasted_iota(jnp.int32, gate.shape, 1)
        base = grp_ref[m] * EXPERTS_PER_GROUP
        acc = jnp.zeros((n_rows, D_MODEL), F32)
        for j in range(EXPERTS_PER_GROUP):
            a = jnp.dot(x, w1_ref[j], preferred_element_type=F32)
            b = jnp.dot(x, w3_ref[j], preferred_element_type=F32)
            gj = jnp.sum(jnp.where(lane == base + j, gate, 0.0), axis=-1, keepdims=True)
            hid = (a * jax.nn.sigmoid(a) * b * gj).astype(BF16)
            acc = acc + jnp.dot(hid, w2_ref[j], preferred_element_type=F32)
        obuf[slot, :n_rows, :] = acc.astype(BF16)
        scatter(m, slot, False)

    half_empty = src_ref[m * cpt + cpt // 2] < 0

    @pl.when((used_ref[m] > 0) & jnp.logical_not(half_empty))
    def _():
        expert_tile(MOE_TM)

    @pl.when((used_ref[m] > 0) & half_empty)
    def _():
        expert_tile(MOE_TM // 2)

    @pl.when(m == n_tiles - 1)
    def _():
        zero_tails(True)

        @pl.when(used_ref[m] > 0)
        def _():
            gather_wait(1 - slot)
            scatter(m, slot, True)

        @pl.when((m >= 1) & prev_used)
        def _():
            scatter(m - 1, 1 - slot, True)


def _moe(xs, w1, w3, w2, tile_group, tile_used, src_chunk, tile_tail):
    n_moe_tiles = tile_group.shape[0]
    rows = xs.shape[0]
    d = D_MODEL
    wspec = lambda r, c: pl.BlockSpec((EXPERTS_PER_GROUP, r, c), lambda m, grp, *_: (grp[m], 0, 0))
    return pl.pallas_call(
        _moe_kernel,
        grid_spec=pltpu.PrefetchScalarGridSpec(
            num_scalar_prefetch=4,
            grid=(n_moe_tiles,),
            in_specs=[pl.BlockSpec(memory_space=pl.ANY),
                      wspec(d, EXPERT_FF), wspec(d, EXPERT_FF), wspec(EXPERT_FF, d)],
            out_specs=pl.BlockSpec(memory_space=pl.ANY),
            scratch_shapes=[pltpu.VMEM((2, MOE_TM, XS_W), BF16),
                            pltpu.VMEM((2, MOE_TM, d), BF16),
                            pltpu.VMEM((CHUNK, d), BF16),
                            pltpu.SemaphoreType.DMA((2,)),
                            pltpu.SemaphoreType.DMA((2,)),
                            pltpu.SemaphoreType.DMA(())]),
        out_shape=jax.ShapeDtypeStruct((rows, d), BF16),
        compiler_params=pltpu.CompilerParams(
            dimension_semantics=("arbitrary",), vmem_limit_bytes=VMEM_LIMIT),
        name="moe",
    )(tile_group, tile_used, src_chunk, tile_tail, xs, w1, w3, w2)


def _combine_kernel(ys_hbm, h_hbm, dest_ref, gain_ref, o_ref, ysbuf, hbuf, ys_sem, h_sem):
    tm = ROUTE_TM
    per_step = o_ref.shape[0] // tm
    i = pl.program_id(0)
    n = h_hbm.shape[0] // (per_step * tm)
    ahead = min(COMBINE_BUFFERS - 1, n - 1)

    def copies(step, slot):
        ys_rows = pl.ds(pl.multiple_of(step * (per_step * SORT_ROWS), CHUNK), per_step * SORT_ROWS)
        h_rows = pl.ds(pl.multiple_of(step * (per_step * tm), tm), per_step * tm)
        return (pltpu.make_async_copy(ys_hbm.at[ys_rows, :], ysbuf.at[slot], ys_sem.at[slot]),
                pltpu.make_async_copy(h_hbm.at[h_rows, :], hbuf.at[slot], h_sem.at[slot]))

    @pl.when(i == 0)
    def _():
        for step in range(ahead):
            for cp in copies(step, step):
                cp.start()

    @pl.when(i + ahead < n)
    def _():
        for cp in copies(i + ahead, (i + ahead) % COMBINE_BUFFERS):
            cp.start()

    slot = i % COMBINE_BUFFERS
    for cp in copies(i, slot):
        cp.wait()
    for sub in range(per_step):
        rows = slice(sub * tm, (sub + 1) * tm)
        dest = dest_ref[rows, N_EXPERTS:N_EXPERTS + 1].astype(jnp.int32)
        perm_t = jnp.where(lax.broadcasted_iota(jnp.int32, (tm, SORT_ROWS), 1) == dest, 1.0, 0.0).astype(BF16)
        moe = jnp.dot(perm_t, ysbuf[slot, sub * SORT_ROWS:(sub + 1) * SORT_ROWS, :], preferred_element_type=F32)
        o_ref[rows, :] = _rms(hbuf[slot, rows, :].astype(F32) + moe, gain_ref[...])


def _combine(ys, h, dest, gain):
    t, d = h.shape
    per_step = COMBINE_TILES_PER_STEP
    tm = ROUTE_TM * per_step
    return pl.pallas_call(
        _combine_kernel,
        grid=(t // tm,),
        in_specs=[pl.BlockSpec(memory_space=pl.ANY),
                  pl.BlockSpec(memory_space=pl.ANY),
                  pl.BlockSpec((tm, LANES), lambda i: (i, 0)),
                  pl.BlockSpec((1, d), lambda i: (0, 0))],
        out_specs=pl.BlockSpec((tm, d), lambda i: (i, 0)),
        out_shape=jax.ShapeDtypeStruct((t, d), F32),
        scratch_shapes=[pltpu.VMEM((COMBINE_BUFFERS, per_step * SORT_ROWS, d), BF16),
                        pltpu.VMEM((COMBINE_BUFFERS, tm, d), BF16),
                        pltpu.SemaphoreType.DMA((COMBINE_BUFFERS,)),
                        pltpu.SemaphoreType.DMA((COMBINE_BUFFERS,))],
        compiler_params=pltpu.CompilerParams(
            dimension_semantics=("arbitrary",), vmem_limit_bytes=VMEM_LIMIT),
        name="combine",
    )(ys, h, dest, gain)


def _rotary_tables(s):
    half = HEAD_DIM // 2
    inv = ROPE_BASE ** (-jnp.arange(half, dtype=F32) / half)
    ang = jnp.arange(s, dtype=F32)[:, None] * inv[None, :]
    cos, sin = jnp.cos(ang), jnp.sin(ang)
    cos_t = jnp.tile(jnp.concatenate([cos, cos], axis=-1), (1, LANES // HEAD_DIM))
    sin_t = jnp.tile(jnp.concatenate([-sin, sin], axis=-1), (1, LANES // HEAD_DIM))
    return cos_t, sin_t


def kernel(x, w_in, w_out, norm_mix, norm_ffn, norm_final, attn_out_gain, rel_bias, ret_decay_fwd, ret_decay_bwd, router_group_w, router_group_b, router_expert_w, router_expert_b, expert_w1, expert_w3, expert_w2):
    b, s, d = x.shape
    depth = w_in.shape[0]
    cos_t, sin_t = _rotary_tables(s)
    bias_rows = _attn_bias_rows(rel_bias)
    h = x
    for layer in range(depth):
        (aq, ak, av, rq, rk, rv, rg), qkv_grouped = _inproj(
            h, norm_mix[layer][None], w_in[layer].astype(BF16), cos_t, sin_t)
        attn, ret, (w1, w3, w2, w_o) = _mixers(
            (aq, ak, av), qkv_grouped, bias_rows, (rq, rk, rv, rg),
            _retention_tables(ret_decay_fwd[layer], ret_decay_bwd[layer]),
            (expert_w1[layer], expert_w3[layer], expert_w2[layer], w_out[layer]))

        rw = jnp.concatenate(
            [jnp.transpose(router_expert_w[layer], (1, 0, 2)).reshape(d, N_EXPERTS),
             router_group_w[layer],
             jnp.zeros((d, LANES - N_EXPERTS - N_GROUPS), F32)], axis=1).astype(BF16)
        rb = jnp.concatenate(
            [router_expert_b[layer].reshape(N_EXPERTS), router_group_b[layer],
             jnp.zeros((LANES - N_EXPERTS - N_GROUPS,), F32)])[None].astype(F32)
        h1, xs, dest, cnt = _outproj(
            attn.reshape(b * s, ATTN_WIDTH), ret.reshape(b * s, RET_WIDTH), h.reshape(b * s, d),
            w_o, attn_out_gain[layer][None], norm_ffn[layer][None],
            rw, rb)
        n_route_tiles = (b * s) // ROUTE_TM
        n_moe_tiles = (b * s + n_route_tiles * N_GROUPS * (CHUNK - 1)) // MOE_TM + N_GROUPS
        schedule = _dispatch_tables(cnt, n_moe_tiles)
        ys = _moe(xs, w1, w3, w2, *schedule)
        assert depth == 1, "the combine kernel fuses the final norm, so it must run on the last layer"
        h = _combine(ys, h1, dest, norm_final[None]).reshape(b, s, d)
    return h
```

```python
import functools
import math

import jax
import jax.numpy as jnp
from jax import lax
from jax.experimental import pallas as pl
from jax.experimental.pallas import tpu as pltpu

F32 = jnp.float32
BF16 = jnp.bfloat16

D_MODEL = 1024
HEAD_DIM = 64
ATTN_WIDTH = 512
RET_WIDTH = 512
N_HEADS = 8
PAIR = 2 * HEAD_DIM
N_PAIRS = N_HEADS // 2
ATTN_DILATIONS = (1, 4, 16)
ATTN_RADIUS = 64
N_BUCKETS = 32
REL_MAX_DIST = 1024
ROPE_BASE = 10000.0
N_GROUPS = 4
EXPERTS_PER_GROUP = 4
N_EXPERTS = 16
EXPERT_FF = 512
EPS = 1e-6
NEG_INF = -1e30
LOG2_E = math.log2(math.e)

LANES = 128
ATTN_TQ = 128
ATTN_W = 256
N_BIAS_VARIANTS = 8
RET_CHUNK = 256
ROUTE_TM = 512
ROUTE_TILES_PER_STEP = 2
COMBINE_TILES_PER_STEP = 2
COMBINE_BUFFERS = 3
SORT_ROWS = 592
CHUNK = 16
MOE_TM = 512
XS_W = D_MODEL + 2 * LANES
VMEM_LIMIT = 48 * 1024 * 1024


def _rms(x, gain):
    return x * lax.rsqrt(jnp.mean(x * x, axis=-1, keepdims=True) + EPS) * gain


def _inproj_kernel(x_ref, gain_ref, w_ref, cos_ref, sin_ref,
                   aq_ref, ak_ref, av_ref, rq_ref, rk_ref, rv_ref, rg_ref,
                   aq4_ref, ak4_ref, av4_ref, aq16_ref, ak16_ref, av16_ref, stage_ref, stage4_ref):
    tm = x_ref.shape[0]
    xn = _rms(x_ref[...], gain_ref[...]).astype(BF16)

    def stage(i, t, nat_ref):
        nat_ref[...] = t.astype(BF16)
        for hp in range(N_PAIRS):
            stage_ref[i, hp] = t[:, hp * PAIR:(hp + 1) * PAIR]

    def regroup4(i, d4_ref):
        for hp in range(N_PAIRS):
            for r4 in range(4):
                g4 = stage_ref[i, hp, pl.ds(r4, tm // 4, stride=4), :]
                d4_ref[hp, r4] = g4.astype(BF16)
                stage4_ref[i, hp, r4] = g4

    def regroup16(i, d16_ref):
        for hp in range(N_PAIRS):
            for r4 in range(4):
                for j in range(4):
                    d16_ref[hp, r4 + 4 * j] = (
                        stage4_ref[i, hp, r4, pl.ds(j, tm // 16, stride=4), :].astype(BF16))

    def seg(i):
        return jnp.dot(xn, w_ref[:, i * ATTN_WIDTH:(i + 1) * ATTN_WIDTH], preferred_element_type=F32)

    def rotary(t):
        cos, sin = cos_ref[...], sin_ref[...]
        first_half = (lax.broadcasted_iota(jnp.int32, (1, LANES), 1) % HEAD_DIM) < HEAD_DIM // 2
        outs = []
        for j in range(t.shape[1] // LANES):
            tj = t[:, j * LANES:(j + 1) * LANES]
            partner = jnp.where(first_half, pltpu.roll(tj, LANES - HEAD_DIM // 2, 1),
                                pltpu.roll(tj, HEAD_DIM // 2, 1))
            outs.append(tj * cos + partner * sin)
        return jnp.concatenate(outs, axis=1)

    stage(0, seg(0) * (HEAD_DIM ** -0.5 * LOG2_E), aq_ref)
    stage(1, seg(1), ak_ref)
    stage(2, seg(2), av_ref)
    rq_ref[...] = rotary(seg(3)).astype(BF16)
    for i, d4_ref in enumerate((aq4_ref, ak4_ref, av4_ref)):
        regroup4(i, d4_ref)
    rk_ref[...] = (rotary(seg(4)) * (HEAD_DIM ** -0.5)).astype(BF16)
    rv_ref[...] = seg(5).astype(BF16)
    for i, d16_ref in enumerate((aq16_ref, ak16_ref, av16_ref)):
        regroup16(i, d16_ref)
    rg_ref[...] = seg(6).astype(BF16)


def _inproj(x, gain, w_in, cos_t, sin_t, tm=512):
    b, s, d = x.shape
    n = w_in.shape[1]
    out = jax.ShapeDtypeStruct((b, s, ATTN_WIDTH), BF16)
    ospec = pl.BlockSpec((None, tm, ATTN_WIDTH), lambda si, bi: (bi, si, 0))

    def grouped(dil):
        shape = jax.ShapeDtypeStruct((b, s // tm, N_PAIRS, dil, tm // dil, PAIR), BF16)
        spec = pl.BlockSpec((None, None, N_PAIRS, dil, tm // dil, PAIR), lambda si, bi: (bi, si, 0, 0, 0, 0))
        return [shape] * 3, [spec] * 3

    shapes4, specs4 = grouped(4)
    shapes16, specs16 = grouped(16)
    outs = pl.pallas_call(
        _inproj_kernel,
        grid=(s // tm, b),
        in_specs=[
            pl.BlockSpec((None, tm, d), lambda si, bi: (bi, si, 0)),
            pl.BlockSpec((1, d), lambda si, bi: (0, 0)),
            pl.BlockSpec((d, n), lambda si, bi: (0, 0)),
            pl.BlockSpec((tm, LANES), lambda si, bi: (si, 0)),
            pl.BlockSpec((tm, LANES), lambda si, bi: (si, 0)),
        ],
        out_specs=[ospec] * 7 + specs4 + specs16,
        out_shape=[out] * 7 + shapes4 + shapes16,
        scratch_shapes=[pltpu.VMEM((3, N_PAIRS, tm, PAIR), F32),
                        pltpu.VMEM((3, N_PAIRS, 4, tm // 4, PAIR), F32)],
        compiler_params=pltpu.CompilerParams(
            dimension_semantics=("arbitrary", "arbitrary"), vmem_limit_bytes=VMEM_LIMIT),
        name="inproj",
    )(x, gain, w_in, cos_t, sin_t)
    return outs[:7], outs[7:]


def _t5_bucket(rel):
    half = N_BUCKETS // 2
    max_exact = half // 2
    offset = jnp.where(rel > 0, half, 0)
    n = jnp.abs(rel)
    nf = jnp.maximum(n, 1).astype(F32)
    large = max_exact + (jnp.log(nf / max_exact) / math.log(REL_MAX_DIST / max_exact)
                         * (half - max_exact)).astype(jnp.int32)
    large = jnp.minimum(large, half - 1)
    return offset + jnp.where(n < max_exact, n, large)


def _attn_bias_rows(rel_bias):
    period = 2 * ATTN_W
    band = 2 * ATTN_RADIUS + 1
    rel = jnp.arange(-ATTN_RADIUS, ATTN_RADIUS + 1)
    rows = []
    for dil, offs in ((1, (0, 64, 128)), (4, (0, 64, 128)), (16, (0, 128))):
        vals = rel_bias[_t5_bucket(rel * dil)].astype(F32).T * LOG2_E
        for off in offs:
            lo = off - ATTN_RADIUS
            pad = jnp.full((N_HEADS, period - band), NEG_INF, F32)
            if lo >= 0:
                row = jnp.concatenate([pad[:, :lo], vals, pad[:, lo:]], axis=1)
            else:
                row = jnp.concatenate([vals[:, -lo:], pad, vals[:, :-lo]], axis=1)
            rows.append(row)
    v = jnp.stack(rows, axis=1)
    return v.reshape(N_PAIRS, 2 * N_BIAS_VARIANTS, period)


def _attention_kernel(q_ref, k_ref, v_ref, q4_ref, k4_ref, v4_ref, q16_ref, k16_ref, v16_ref,
                      rows_ref, o_ref, bias_ref, acc_ref, m_ref, l_ref, out_ref):
    s = q_ref.shape[0]
    n_tiles = s // ATTN_TQ
    lane = lax.broadcasted_iota(jnp.int32, (1, PAIR), 1)
    head0 = lane < HEAD_DIM

    @pl.when(pl.program_id(1) == 0)
    def _():
        col = lax.broadcasted_iota(jnp.int32, (ATTN_TQ, ATTN_W), 1)
        for idx in range(2 * N_BIAS_VARIANTS):
            gen = jnp.broadcast_to(rows_ref[idx:idx + 1, :], (ATTN_TQ, 2 * ATTN_W))
            tab = pltpu.roll(gen, 0, 1, stride=1, stride_axis=0)[:, :ATTN_W]
            var = idx % N_BIAS_VARIANTS
            if var >= 6:
                tab = jnp.where((col // ATTN_TQ) == var - 6, tab, NEG_INF)
            head = idx // N_BIAS_VARIANTS
            bias_ref[var, head * ATTN_TQ:(head + 1) * ATTN_TQ, :] = tab

    def rows_of(ref, dil, lo, n):
        if dil == 1:
            return ref[lo:lo + n, :]
        piece, sub_len = ref.shape[2], s // dil
        parts, pos = [], lo
        while pos < lo + n:
            within = pos % sub_len
            off = within % piece
            take = min(piece - off, lo + n - pos)
            parts.append(ref[within // piece, pos // sub_len, off:off + take, :])
            pos += take
        return parts[0] if len(parts) == 1 else jnp.concatenate(parts, axis=0)

    def run_branch(bi, dil, qs_ref, ks_ref, vs_ref):
        sub_len = s // dil
        tiles_per_sub = sub_len // ATTN_TQ

        def tile(t):
            q0 = t * ATTN_TQ
            if tiles_per_sub == 1:
                ws = (t // 2) * ATTN_W
                var = 6 + t % 2
            else:
                pos = t % tiles_per_sub
                sub_lo = (t // tiles_per_sub) * sub_len
                ws = min(max(q0 - ATTN_RADIUS, sub_lo), sub_lo + sub_len - ATTN_W)
                var = (0 if pos == 0 else 2 if pos == tiles_per_sub - 1 else 1) + 3 * bi
            q = rows_of(qs_ref, dil, q0, ATTN_TQ)
            k = rows_of(ks_ref, dil, ws, ATTN_W)
            v = rows_of(vs_ref, dil, ws, ATTN_W)
            q2 = jnp.concatenate([jnp.where(head0, q, jnp.zeros_like(q)),
                                  jnp.where(head0, jnp.zeros_like(q), q)], axis=0)
            sc = lax.dot_general(q2, k, (((1,), (1,)), ((), ())), preferred_element_type=F32)
            sc = sc + bias_ref[var]
            m = jnp.max(sc, axis=-1, keepdims=True)
            p = jnp.exp2(sc - m).astype(BF16)
            o = jnp.dot(p, jnp.concatenate([v, jnp.ones_like(v)], axis=1), preferred_element_type=F32)
            l = o[:, PAIR:]
            outs, ms, ls = (o[:ATTN_TQ, :PAIR], o[ATTN_TQ:, :PAIR]), (m[:ATTN_TQ], m[ATTN_TQ:]), (l[:ATTN_TQ], l[ATTN_TQ:])
            if dil == 16:
                dst = pl.ds((t % 4) * (s // 4) + t // 4, ATTN_TQ, stride=4)
            else:
                dst = pl.ds(q0, ATTN_TQ)
            acc_ref[bi, dst, :] = jnp.where(head0, outs[0], outs[1])
            m_ref[bi, dst, :] = jnp.where(head0, ms[0], ms[1])
            l_ref[bi, dst, :] = jnp.where(head0, ls[0], ls[1])

        for t in range(n_tiles):
            tile(t)

    run_branch(0, 1, q_ref, k_ref, v_ref)
    run_branch(1, 4, q4_ref, k4_ref, v4_ref)
    run_branch(2, 16, q16_ref, k16_ref, v16_ref)

    rows = ATTN_TQ
    for r4 in range(4):
        for blk in range(s // (4 * rows)):
            nat = pl.ds(r4 + 4 * rows * blk, rows, stride=4)
            grouped4 = pl.ds(r4 * (s // 4) + rows * blk, rows)
            sl = (nat, grouped4, grouped4)
            m = [m_ref[bi, sl[bi], :] for bi in range(3)]
            mx = jnp.maximum(jnp.maximum(m[0], m[1]), m[2])
            num = jnp.zeros((rows, PAIR), F32)
            den = jnp.zeros((rows, PAIR), F32)
            for bi in range(3):
                e = jnp.exp2(m[bi] - mx)
                num = num + e * acc_ref[bi, sl[bi], :]
                den = den + e * l_ref[bi, sl[bi], :]
            out_ref[nat, :] = num / den
    o_ref[...] = out_ref[...].astype(o_ref.dtype)


def _retention_tables(decay_fwd, decay_bwd):
    c = RET_CHUNK
    lg_f = -jnp.exp(decay_fwd.astype(F32))
    lg_b = -jnp.exp(decay_bwd.astype(F32))
    idx = jnp.arange(c, dtype=F32)
    rel = idx[:, None] - idx[None, :]
    dmat = jnp.where(rel >= 0,
                     jnp.exp(lg_f[:, None, None] * jnp.maximum(rel, 0.0)[None]),
                     jnp.exp(lg_b[:, None, None] * jnp.maximum(-rel, 0.0)[None]))
    dmat = dmat.reshape(N_PAIRS, 2 * c, c)

    def lanes(v):
        v = v.reshape(N_PAIRS, 2, -1)
        return jnp.repeat(jnp.transpose(v, (0, 2, 1)), HEAD_DIM, axis=2)

    vec = jnp.stack([
        lanes(jnp.exp(lg_f[:, None] * (idx + 1.0)[None])),
        lanes(jnp.exp(lg_f[:, None] * (c - 1.0 - idx)[None])),
        lanes(jnp.exp(lg_b[:, None] * (c - idx)[None])),
        lanes(jnp.exp(lg_b[:, None] * idx[None])),
    ], axis=1)
    same_head = (jnp.arange(PAIR)[:, None] // HEAD_DIM) == (jnp.arange(PAIR)[None, :] // HEAD_DIM)
    cd = jnp.stack([lanes(jnp.exp(lg_f * c)[:, None]), lanes(jnp.exp(lg_b * c)[:, None])], axis=1)
    cd = jnp.transpose(cd, (0, 1, 3, 2)) * same_head[None, None].astype(F32)
    return dmat, vec, cd


def _retention_kernel(q_ref, k_ref, v_ref, g_ref, dmat_ref, vec_ref, cd_ref, o_ref, kv_ref, st_ref):
    s = q_ref.shape[0]
    c = RET_CHUNK
    nc = s // c
    lane = lax.broadcasted_iota(jnp.int32, (1, PAIR), 1)
    head0 = lane < HEAD_DIM
    same_head = ((lax.broadcasted_iota(jnp.int32, (PAIR, PAIR), 0) // HEAD_DIM)
                 == (lax.broadcasted_iota(jnp.int32, (PAIR, PAIR), 1) // HEAD_DIM))

    same_head2 = jnp.concatenate([same_head, same_head], axis=0)
    for n in range(nc):
        rows = slice(n * c, (n + 1) * c)
        kf = k_ref[rows, :].astype(F32)
        kcat = jnp.concatenate([(kf * vec_ref[1]).astype(BF16), (kf * vec_ref[3]).astype(BF16)], axis=1)
        kv = lax.dot_general(kcat, v_ref[rows, :], (((0,), (0,)), ((), ())), preferred_element_type=F32)
        kv_ref[n] = jnp.where(same_head2, kv, 0.0)

    state = jnp.zeros((PAIR, PAIR), F32)
    for n in range(nc):
        st_ref[n, :PAIR, :] = state.astype(BF16)
        state = state * cd_ref[0] + kv_ref[n, :PAIR, :]
    state = jnp.zeros((PAIR, PAIR), F32)
    for n in reversed(range(nc)):
        st_ref[n, PAIR:, :] = state.astype(BF16)
        state = state * cd_ref[1] + kv_ref[n, PAIR:, :]

    for n in range(nc):
        rows = slice(n * c, (n + 1) * c)
        q, k, v = q_ref[rows, :], k_ref[rows, :], v_ref[rows, :]
        q2 = jnp.concatenate([jnp.where(head0, q, jnp.zeros_like(q)),
                              jnp.where(head0, jnp.zeros_like(q), q)], axis=0)
        sc = lax.dot_general(q2, k, (((1,), (1,)), ((), ())), preferred_element_type=F32)
        intra = jnp.dot((sc * dmat_ref[...]).astype(BF16), v, preferred_element_type=F32)
        qf = q.astype(F32)
        qcat = jnp.concatenate([(qf * vec_ref[0]).astype(BF16), (qf * vec_ref[2]).astype(BF16)], axis=1)
        y = jnp.where(head0, intra[:c], intra[c:]) + jnp.dot(qcat, st_ref[n], preferred_element_type=F32)
        y2 = y * y
        ms0 = jnp.sum(jnp.where(head0, y2, 0.0), axis=-1, keepdims=True)
        ms1 = jnp.sum(jnp.where(head0, 0.0, y2), axis=-1, keepdims=True)
        ms = jnp.where(head0, ms0, ms1) * (1.0 / HEAD_DIM)
        g = g_ref[rows, :].astype(F32)
        o_ref[rows, :] = (y * lax.rsqrt(ms + EPS) * (g * jax.nn.sigmoid(g))).astype(o_ref.dtype)


def _mixers(qkv, qkv_grouped, bias_rows, ret_inputs, ret_tables, weights_f32):
    b, s, _ = qkv[0].shape
    c = RET_CHUNK
    n_steps = N_PAIRS * b
    spec = pl.BlockSpec((None, s, PAIR), lambda hp, bi: (bi, 0, hp))
    gspecs = [pl.BlockSpec((None, g.shape[1], None) + g.shape[3:], lambda hp, bi: (bi, 0, hp, 0, 0, 0))
              for g in qkv_grouped]
    slabs = [w.reshape(n_steps, -1, w.shape[-1]) for w in weights_f32]
    slab_specs = [pl.BlockSpec((None,) + w.shape[1:], lambda hp, bi: (hp * b + bi, 0, 0)) for w in slabs]
    n_attn_in, n_ret_in, n_w = 10, 7, len(slabs)
    attn_scratch = [
        pltpu.VMEM((N_BIAS_VARIANTS, 2 * ATTN_TQ, ATTN_W), F32),
        pltpu.VMEM((3, s, PAIR), F32),
        pltpu.VMEM((3, s, PAIR), F32),
        pltpu.VMEM((3, s, PAIR), F32),
        pltpu.VMEM((s, PAIR), F32),
    ]
    ret_scratch = [pltpu.VMEM((s // c, 2 * PAIR, PAIR), F32),
                   pltpu.VMEM((s // c, 2 * PAIR, PAIR), BF16)]

    def kernel(*refs):
        ins, rest = refs[:n_attn_in + n_ret_in + n_w], refs[n_attn_in + n_ret_in + n_w:]
        attn_o, ret_o, w_out, scratch = rest[0], rest[1], rest[2:2 + n_w], rest[2 + n_w:]
        _attention_kernel(*ins[:n_attn_in], attn_o, *scratch[:len(attn_scratch)])
        _retention_kernel(*ins[n_attn_in:n_attn_in + n_ret_in], ret_o, *scratch[len(attn_scratch):])
        for src, dst in zip(ins[n_attn_in + n_ret_in:], w_out):
            dst[...] = src[...].astype(BF16)

    outs = pl.pallas_call(
        kernel,
        grid=(N_PAIRS, b),
        in_specs=[spec] * 3 + gspecs
        + [pl.BlockSpec((None, 2 * N_BIAS_VARIANTS, 2 * ATTN_W), lambda hp, bi: (hp, 0, 0))]
        + [spec] * 4
        + [pl.BlockSpec((None, 2 * c, c), lambda hp, bi: (hp, 0, 0)),
           pl.BlockSpec((None, 4, c, PAIR), lambda hp, bi: (hp, 0, 0, 0)),
           pl.BlockSpec((None, 2, PAIR, PAIR), lambda hp, bi: (hp, 0, 0, 0))]
        + slab_specs,
        out_specs=[spec, spec] + slab_specs,
        out_shape=[jax.ShapeDtypeStruct((b, s, ATTN_WIDTH), BF16), jax.ShapeDtypeStruct((b, s, RET_WIDTH), BF16)]
        + [jax.ShapeDtypeStruct(w.shape, BF16) for w in slabs],
        scratch_shapes=attn_scratch + ret_scratch,
        compiler_params=pltpu.CompilerParams(
            dimension_semantics=("arbitrary", "arbitrary"), vmem_limit_bytes=VMEM_LIMIT),
        name="mixers",
    )(*qkv, *qkv_grouped, bias_rows, *ret_inputs, *ret_tables, *slabs)
    return outs[0], outs[1], [o.reshape(w.shape) for o, w in zip(outs[2:], weights_f32)]


def _route_tile(sub, attn_ref, ret_ref, x_ref, wo_ref, again_ref, fgain_ref,
                rw_ref, rb_ref, h_ref, xs_ref, dest_ref, cnt_ref):
    tm = ROUTE_TM
    rows = slice(sub * tm, (sub + 1) * tm)
    a = _rms(attn_ref[rows, :].astype(F32), again_ref[...]).astype(BF16)
    mixed = jnp.concatenate([a, ret_ref[rows, :]], axis=1)
    h = x_ref[rows, :] + jnp.dot(mixed, wo_ref[...], preferred_element_type=F32)
    h_ref[rows, :] = h.astype(h_ref.dtype)
    hn = _rms(h, fgain_ref[...]).astype(BF16)

    logits = jnp.dot(hn, rw_ref[...], preferred_element_type=F32) + rb_ref[...]
    lt = logits.T
    row = lambda i: lt[i:i + 1, :]

    def top1(vals):
        top = functools.reduce(jnp.maximum, vals)
        idx = jnp.full(top.shape, len(vals) - 1, jnp.int32)
        for i in reversed(range(len(vals) - 1)):
            idx = jnp.where(vals[i] == top, i, idx)
        return top, idx

    g_rows = [row(N_EXPERTS + g) for g in range(N_GROUPS)]
    gmax, grp = top1(g_rows)
    p_group = 1.0 / sum(jnp.exp(g - gmax) for g in g_rows)
    chosen = []
    for j in range(EXPERTS_PER_GROUP):
        e = row((N_GROUPS - 1) * EXPERTS_PER_GROUP + j)
        for g in reversed(range(N_GROUPS - 1)):
            e = jnp.where(grp == g, row(g * EXPERTS_PER_GROUP + j), e)
        chosen.append(e)
    v1, i1 = top1(chosen)
    v2, i2 = top1([jnp.where(i1 == j, -jnp.inf, e) for j, e in enumerate(chosen)])
    e2 = jnp.exp(v2 - v1)
    p1 = p_group / (1.0 + e2)
    inner = [jnp.where(i1 == j, p1, jnp.where(i2 == j, p1 * e2, 0.0)) for j in range(EXPERTS_PER_GROUP)]
    gate_rows = [jnp.where(grp == g, inner[j], 0.0)
                 for g in range(N_GROUPS) for j in range(EXPERTS_PER_GROUP)]

    sub8 = lax.broadcasted_iota(jnp.int32, (8, 1), 0)
    onehot_t = jnp.where(sub8 == grp, 1.0, 0.0)
    r_i = lax.broadcasted_iota(jnp.int32, (tm, tm), 0)
    c_i = lax.broadcasted_iota(jnp.int32, (tm, tm), 1)
    before = jnp.where(c_i < r_i, 1.0, 0.0).astype(BF16)
    rank_t = lax.dot_general(onehot_t.astype(BF16), before, (((1,), (1,)), ((), ())),
                             preferred_element_type=F32)
    cnt = jnp.sum(onehot_t, axis=1, keepdims=True)
    c16 = jnp.ceil(cnt * (1.0 / CHUNK)) * CHUNK
    start = jnp.zeros((8, 1), F32)
    for g in range(N_GROUPS - 1):
        start = start + jnp.where(sub8 > g, c16[g:g + 1, :], 0.0)
    dest_t = jnp.sum(onehot_t * (start + rank_t), axis=0, keepdims=True)
    cnt_ref[sub] = jnp.broadcast_to(jnp.where(sub8 < N_GROUPS, c16, pltpu.roll(start, N_GROUPS, 0)),
                                    (8, LANES)).astype(jnp.int32)

    cols = jnp.concatenate(gate_rows + [dest_t, jnp.zeros((LANES - N_EXPERTS - 1, tm), F32)], axis=0).T
    dest_ref[rows, :] = cols
    lane = lax.broadcasted_iota(jnp.int32, cols.shape, 1)
    gates = jnp.where(lane < N_EXPERTS, cols, 0.0)
    perm = jnp.where(lax.broadcasted_iota(jnp.int32, (SORT_ROWS, tm), 0) == dest_t.astype(jnp.int32),
                     1.0, 0.0).astype(BF16)
    g_hi = gates.astype(BF16)
    g_lo = (gates - g_hi.astype(F32)).astype(BF16)
    payload = jnp.concatenate([hn, g_hi, g_lo], axis=1)
    xs_ref[sub * SORT_ROWS:(sub + 1) * SORT_ROWS, :] = (
        jnp.dot(perm, payload, preferred_element_type=F32).astype(BF16))


def _outproj_kernel(*refs):
    for sub in range(ROUTE_TILES_PER_STEP):
        _route_tile(sub, *refs)


def _outproj(attn, ret, x, w_o, again, fgain, rw, rb):
    t, d = x.shape
    per_step = ROUTE_TILES_PER_STEP
    tm = ROUTE_TM * per_step
    nt = t // ROUTE_TM
    row = lambda w: pl.BlockSpec((tm, w), lambda i: (i, 0))
    full = lambda r, c: pl.BlockSpec((r, c), lambda i: (0, 0))
    return pl.pallas_call(
        _outproj_kernel,
        grid=(nt // per_step,),
        in_specs=[row(ATTN_WIDTH), row(RET_WIDTH), row(d), full(ATTN_WIDTH + RET_WIDTH, d),
                  full(1, ATTN_WIDTH), full(1, d), full(d, LANES), full(1, LANES)],
        out_specs=[row(d), pl.BlockSpec((per_step * SORT_ROWS, XS_W), lambda i: (i, 0)), row(LANES),
                   pl.BlockSpec((per_step, 8, LANES), lambda i: (i, 0, 0))],
        out_shape=[jax.ShapeDtypeStruct((t, d), BF16),
                   jax.ShapeDtypeStruct((nt * SORT_ROWS, XS_W), BF16),
                   jax.ShapeDtypeStruct((t, LANES), F32),
                   jax.ShapeDtypeStruct((nt, 8, LANES), jnp.int32)],
        compiler_params=pltpu.CompilerParams(
            dimension_semantics=("arbitrary",), vmem_limit_bytes=VMEM_LIMIT),
        name="outproj",
    )(attn, ret, x, w_o, again, fgain, rw, rb)


def _dispatch_tables(cnt, n_moe_tiles):
    nt = cnt.shape[0]
    cpt = MOE_TM // CHUNK

    n_slots = n_moe_tiles * cpt
    slot_rows = -(-n_slots // (8 * LANES)) * 8

    def schedule_kernel(seg_ref, grp_ref, used_ref, src_ref, tail_ref):
        slot = (lax.broadcasted_iota(jnp.int32, (slot_rows, LANES), 0) * LANES
                + lax.broadcasted_iota(jnp.int32, (slot_rows, LANES), 1))
        src = jnp.full((slot_rows, LANES), -1, jnp.int32)
        pos = jnp.int32(0)
        for g in range(N_GROUPS):
            def tile_body(i, carry):
                p, src = carry
                n = seg_ref[i, g] // CHUNK
                first = (i * SORT_ROWS + seg_ref[i, N_GROUPS + g]) // CHUNK
                src = jnp.where((slot >= p) & (slot < p + n), slot + (first - p), src)
                return p + n, src
            end, src = lax.fori_loop(0, nt, tile_body, (pos, src))
            padded = ((end + cpt - 1) // cpt) * cpt

            def mark_body(m, carry):
                grp_ref[m] = g
                used_ref[m] = 1
                return carry
            lax.fori_loop(pos // cpt, padded // cpt, mark_body, 0)
            pos = padded

        def idle_body(m, carry):
            grp_ref[m] = N_GROUPS - 1
            used_ref[m] = 0
            return carry
        lax.fori_loop(pos // cpt, n_moe_tiles, idle_body, 0)
        src_ref[...] = src

        def tail_body(i, carry):
            rows = seg_ref[i, 0]
            for g in range(1, N_GROUPS):
                rows = rows + seg_ref[i, g]
            tail_ref[i] = rows // CHUNK
            return carry
        lax.fori_loop(0, nt, tail_body, 0)

    smem = lambda: pl.BlockSpec(memory_space=pltpu.SMEM)
    tile_group, tile_used, src, tile_tail = pl.pallas_call(
        schedule_kernel,
        in_specs=[smem()],
        out_specs=[smem(), smem(), pl.BlockSpec(memory_space=pltpu.VMEM), smem()],
        out_shape=[jax.ShapeDtypeStruct((n_moe_tiles,), jnp.int32),
                   jax.ShapeDtypeStruct((n_moe_tiles,), jnp.int32),
                   jax.ShapeDtypeStruct((slot_rows, LANES), jnp.int32),
                   jax.ShapeDtypeStruct((nt,), jnp.int32)],
        name="schedule",
    )(cnt[:, :2 * N_GROUPS, 0])
    return tile_group, tile_used, src.reshape(-1)[:n_slots], tile_tail


def _moe_kernel(grp_ref, used_ref, src_ref, tail_ref, xs_hbm, w1_ref, w3_ref, w2_ref, ys_hbm,
                xbuf, obuf, zbuf, in_sem, out_sem, zero_sem):
    m = pl.program_id(0)
    n_tiles = pl.num_programs(0)
    cpt = MOE_TM // CHUNK
    cps = SORT_ROWS // CHUNK
    slot = m % 2
    zero_chunk = cps - 1

    def rows(c):
        return pl.ds(pl.multiple_of(c * CHUNK, CHUNK), CHUNK)

    def zero_tails(wait):
        def tile_body(i, carry):
            def body(c, carry2):
                cp = pltpu.make_async_copy(zbuf, ys_hbm.at[rows(i * cps + c), :], zero_sem)
                cp.wait() if wait else cp.start()
                return carry2
            lax.fori_loop(tail_ref[i], cps, body, 0)
            return carry
        lax.fori_loop(0, tail_ref.shape[0], tile_body, 0)

    def gather_start(tile, sl):
        for c in range(cpt):
            src = src_ref[tile * cpt + c]
            src = jnp.where(src < 0, zero_chunk, src)
            pltpu.make_async_copy(xs_hbm.at[rows(src), :], xbuf.at[sl, c * CHUNK:(c + 1) * CHUNK, :],
                                  in_sem.at[sl]).start(priority=c % 2)

    def gather_wait(sl):
        pltpu.make_async_copy(xs_hbm.at[0:MOE_TM, :], xbuf.at[sl], in_sem.at[sl]).wait()

    def scatter(tile, sl, wait):
        full = src_ref[tile * cpt + cpt - 1] >= 0

        def chunk_copy(c, src):
            return pltpu.make_async_copy(obuf.at[sl, c * CHUNK:(c + 1) * CHUNK, :], ys_hbm.at[rows(src), :],
                                         out_sem.at[sl])

        @pl.when(full)
        def _():
            if wait:
                pltpu.make_async_copy(obuf.at[sl], ys_hbm.at[0:MOE_TM, :], out_sem.at[sl]).wait()
            else:
                for c in range(cpt):
                    chunk_copy(c, src_ref[tile * cpt + c]).start(priority=c % 2)

        @pl.when(jnp.logical_not(full))
        def _():
            for c in range(cpt):
                src = src_ref[tile * cpt + c]

                @pl.when(src >= 0)
                def _():
                    cp = chunk_copy(c, src)
                    cp.wait() if wait else cp.start()

    @pl.when(m == 0)
    def _():
        gather_start(0, 0)
        zbuf[...] = jnp.zeros(zbuf.shape, zbuf.dtype)
        zero_tails(False)

    prev_used = used_ref[jnp.maximum(m - 1, 0)] > 0

    @pl.when((m == 0) | prev_used)
    def _():
        gather_wait(slot)

    @pl.when((m >= 2) & (used_ref[jnp.maximum(m - 2, 0)] > 0))
    def _():
        scatter(m - 2, slot, True)

    def expert_tile(n_rows):
        gather_start(jnp.minimum(m + 1, n_tiles - 1), 1 - slot)
        x = xbuf[slot, :n_rows, :D_MODEL]
        gate = (xbuf[slot, :n_rows, D_MODEL:D_MODEL + LANES].astype(F32)
                + xbuf[slot, :n_rows, D_MODEL + LANES:].astype(F32))
        lane = lax.broadcasted_iota(jnp.int32, gate.shape, 1)
        base = grp_ref[m] * EXPERTS_PER_GROUP
        acc = jnp.zeros((n_rows, D_MODEL), F32)
        half = EXPERT_FF // 2
        for j in range(EXPERTS_PER_GROUP):
            gj = jnp.sum(jnp.where(lane == base + j, gate, 0.0), axis=-1, keepdims=True)
            for lo in (0, half):
                a = jnp.dot(x, w1_ref[j, :, lo:lo + half], preferred_element_type=F32)
                b = jnp.dot(x, w3_ref[j, :, lo:lo + half], preferred_element_type=F32)
                hid = (a * jax.nn.sigmoid(a) * b * gj).astype(BF16)
                acc = acc + jnp.dot(hid, w2_ref[j, lo:lo + half, :], preferred_element_type=F32)
        obuf[slot, :n_rows, :] = acc.astype(BF16)
        scatter(m, slot, False)

    half_empty = src_ref[m * cpt + cpt // 2] < 0

    @pl.when((used_ref[m] > 0) & jnp.logical_not(half_empty))
    def _():
        expert_tile(MOE_TM)

    @pl.when((used_ref[m] > 0) & half_empty)
    def _():
        expert_tile(MOE_TM // 2)

    @pl.when(m == n_tiles - 1)
    def _():
        zero_tails(True)

        @pl.when(used_ref[m] > 0)
        def _():
            gather_wait(1 - slot)
            scatter(m, slot, True)

        @pl.when((m >= 1) & prev_used)
        def _():
            scatter(m - 1, 1 - slot, True)


def _moe(xs, w1, w3, w2, tile_group, tile_used, src_chunk, tile_tail):
    n_moe_tiles = tile_group.shape[0]
    rows = xs.shape[0]
    d = D_MODEL
    wspec = lambda r, c: pl.BlockSpec((EXPERTS_PER_GROUP, r, c), lambda m, grp, *_: (grp[m], 0, 0))
    return pl.pallas_call(
        _moe_kernel,
        grid_spec=pltpu.PrefetchScalarGridSpec(
            num_scalar_prefetch=4,
            grid=(n_moe_tiles,),
            in_specs=[pl.BlockSpec(memory_space=pl.ANY),
                      wspec(d, EXPERT_FF), wspec(d, EXPERT_FF), wspec(EXPERT_FF, d)],
            out_specs=pl.BlockSpec(memory_space=pl.ANY),
            scratch_shapes=[pltpu.VMEM((2, MOE_TM, XS_W), BF16),
                            pltpu.VMEM((2, MOE_TM, d), BF16),
                            pltpu.VMEM((CHUNK, d), BF16),
                            pltpu.SemaphoreType.DMA((2,)),
                            pltpu.SemaphoreType.DMA((2,)),
                            pltpu.SemaphoreType.DMA(())]),
        out_shape=jax.ShapeDtypeStruct((rows, d), BF16),
        compiler_params=pltpu.CompilerParams(
            dimension_semantics=("arbitrary",), vmem_limit_bytes=VMEM_LIMIT),
        name="moe",
    )(tile_group, tile_used, src_chunk, tile_tail, xs, w1, w3, w2)


def _combine_kernel(ys_hbm, h_hbm, dest_ref, gain_ref, o_ref, ysbuf, hbuf, ys_sem, h_sem):
    tm = ROUTE_TM
    per_step = o_ref.shape[0] // tm
    i = pl.program_id(0)
    n = h_hbm.shape[0] // (per_step * tm)
    ahead = min(COMBINE_BUFFERS - 1, n - 1)

    def copies(step, slot):
        ys_rows = pl.ds(pl.multiple_of(step * (per_step * SORT_ROWS), CHUNK), per_step * SORT_ROWS)
        h_rows = pl.ds(pl.multiple_of(step * (per_step * tm), tm), per_step * tm)
        return (pltpu.make_async_copy(ys_hbm.at[ys_rows, :], ysbuf.at[slot], ys_sem.at[slot]),
                pltpu.make_async_copy(h_hbm.at[h_rows, :], hbuf.at[slot], h_sem.at[slot]))

    @pl.when(i == 0)
    def _():
        for step in range(ahead):
            for cp in copies(step, step):
                cp.start()

    @pl.when(i + ahead < n)
    def _():
        for cp in copies(i + ahead, (i + ahead) % COMBINE_BUFFERS):
            cp.start()

    slot = i % COMBINE_BUFFERS
    for cp in copies(i, slot):
        cp.wait()
    for sub in range(per_step):
        rows = slice(sub * tm, (sub + 1) * tm)
        dest = dest_ref[rows, N_EXPERTS:N_EXPERTS + 1].astype(jnp.int32)
        perm_t = jnp.where(lax.broadcasted_iota(jnp.int32, (tm, SORT_ROWS), 1) == dest, 1.0, 0.0).astype(BF16)
        moe = jnp.dot(perm_t, ysbuf[slot, sub * SORT_ROWS:(sub + 1) * SORT_ROWS, :], preferred_element_type=F32)
        o_ref[rows, :] = _rms(hbuf[slot, rows, :].astype(F32) + moe, gain_ref[...])


def _combine(ys, h, dest, gain):
    t, d = h.shape
    per_step = COMBINE_TILES_PER_STEP
    tm = ROUTE_TM * per_step
    return pl.pallas_call(
        _combine_kernel,
        grid=(t // tm,),
        in_specs=[pl.BlockSpec(memory_space=pl.ANY),
                  pl.BlockSpec(memory_space=pl.ANY),
                  pl.BlockSpec((tm, LANES), lambda i: (i, 0)),
                  pl.BlockSpec((1, d), lambda i: (0, 0))],
        out_specs=pl.BlockSpec((tm, d), lambda i: (i, 0)),
        out_shape=jax.ShapeDtypeStruct((t, d), F32),
        scratch_shapes=[pltpu.VMEM((COMBINE_BUFFERS, per_step * SORT_ROWS, d), BF16),
                        pltpu.VMEM((COMBINE_BUFFERS, tm, d), BF16),
                        pltpu.SemaphoreType.DMA((COMBINE_BUFFERS,)),
                        pltpu.SemaphoreType.DMA((COMBINE_BUFFERS,))],
        compiler_params=pltpu.CompilerParams(
            dimension_semantics=("arbitrary",), vmem_limit_bytes=VMEM_LIMIT),
        name="combine",
    )(ys, h, dest, gain)


def _rotary_tables(s):
    half = HEAD_DIM // 2
    inv = ROPE_BASE ** (-jnp.arange(half, dtype=F32) / half)
    ang = jnp.arange(s, dtype=F32)[:, None] * inv[None, :]
    cos, sin = jnp.cos(ang), jnp.sin(ang)
    cos_t = jnp.tile(jnp.concatenate([cos, cos], axis=-1), (1, LANES // HEAD_DIM))
    sin_t = jnp.tile(jnp.concatenate([-sin, sin], axis=-1), (1, LANES // HEAD_DIM))
    return cos_t, sin_t


def kernel(x, w_in, w_out, norm_mix, norm_ffn, norm_final, attn_out_gain, rel_bias, ret_decay_fwd, ret_decay_bwd, router_group_w, router_group_b, router_expert_w, router_expert_b, expert_w1, expert_w3, expert_w2):
    b, s, d = x.shape
    depth = w_in.shape[0]
    cos_t, sin_t = _rotary_tables(s)
    bias_rows = _attn_bias_rows(rel_bias)
    h = x
    for layer in range(depth):
        (aq, ak, av, rq, rk, rv, rg), qkv_grouped = _inproj(
            h, norm_mix[layer][None], w_in[layer].astype(BF16), cos_t, sin_t)
        attn, ret, (w1, w3, w2, w_o) = _mixers(
            (aq, ak, av), qkv_grouped, bias_rows, (rq, rk, rv, rg),
            _retention_tables(ret_decay_fwd[layer], ret_decay_bwd[layer]),
            (expert_w1[layer], expert_w3[layer], expert_w2[layer], w_out[layer]))

        rw = jnp.concatenate(
            [jnp.transpose(router_expert_w[layer], (1, 0, 2)).reshape(d, N_EXPERTS),
             router_group_w[layer],
             jnp.zeros((d, LANES - N_EXPERTS - N_GROUPS), F32)], axis=1).astype(BF16)
        rb = jnp.concatenate(
            [router_expert_b[layer].reshape(N_EXPERTS), router_group_b[layer],
             jnp.zeros((LANES - N_EXPERTS - N_GROUPS,), F32)])[None].astype(F32)
        h1, xs, dest, cnt = _outproj(
            attn.reshape(b * s, ATTN_WIDTH), ret.reshape(b * s, RET_WIDTH), h.reshape(b * s, d),
            w_o, attn_out_gain[layer][None], norm_ffn[layer][None],
            rw, rb)
        n_route_tiles = (b * s) // ROUTE_TM
        n_moe_tiles = (b * s + n_route_tiles * N_GROUPS * (CHUNK - 1)) // MOE_TM + N_GROUPS
        schedule = _dispatch_tables(cnt, n_moe_tiles)
        ys = _moe(xs, w1, w3, w2, *schedule)
        assert depth == 1, "the combine kernel fuses the final norm, so it must run on the last layer"
        h = _combine(ys, h1, dest, norm_final[None]).reshape(b, s, d)
    return h
```
